```python
import math
import jax, jax.numpy as jnp
from jax import lax
import numpy as np

D_MODEL = 1024
BATCH = 32
SEQ = 256
DEPTH = 2
DEC_BATCH = 4
DEC_SEQ = 1024
PAST_LEN = 512

GRID_W = 64
MIX_W = D_MODEL // 4
N_BRANCH = 4
MLSTM_HEADS = 4
MLSTM_DK = MIX_W // MLSTM_HEADS
MLSTM_DV = MIX_W // MLSTM_HEADS
NA_HEADS = 4
NA_HEAD_DIM = MIX_W // NA_HEADS
NA_WIN_R = 8
NA_WIN_C = 16
DIFF_HEADS = 4
DIFF_V_DIM = MIX_W // DIFF_HEADS
DIFF_QK_DIM = DIFF_V_DIM // 2
SSM_HEADS = 4
SSM_HEAD_DIM = MIX_W // SSM_HEADS
SSM_GROUPS = 2
SSM_STATE = 64
SSM_CONV_K = 5
SSM_CONV_CH = MIX_W + 2 * SSM_GROUPS * SSM_STATE
CHUNK = 64
Q_BLOCK = 128
N_EXPERTS = 16
EXPERT_FF = 1024
CAPACITY_FACTOR = 2
ROPE_BASE = 10000.0
EPS = 1e-6
SPLITS = (MIX_W, MIX_W, MIX_W, MIX_W, 4 * MLSTM_HEADS, MIX_W, MIX_W, MIX_W, MIX_W, MIX_W, MIX_W, MIX_W, SSM_CONV_CH, 2 * SSM_HEADS, N_BRANCH * D_MODEL)
IN_COLS = sum(SPLITS)

kernel_name = 'hybrid_diffusion_mlstm_na_diff_ssd_ec_step'


def rms_norm(x, w):
    xf = x.astype(jnp.float32)
    y = xf * lax.rsqrt(jnp.mean(xf * xf, axis=-1, keepdims=True) + EPS)
    return (y * w.astype(jnp.float32)).astype(x.dtype)


def split_cols(u):
    points, acc = [], 0
    for s in SPLITS[:-1]:
        acc += s
        points.append(acc)
    return jnp.split(u, points, axis=-1)


def to_blocks(t, size):
    b, s = t.shape[:2]
    return jnp.moveaxis(t.reshape((b, s // size, size) + t.shape[2:]), 1, 0)


def from_blocks(t):
    t = jnp.moveaxis(t, 0, 1)
    return t.reshape((t.shape[0], t.shape[1] * t.shape[2]) + t.shape[3:])


def rev(t):
    return jnp.flip(t, axis=1)


def rope_2d(x, n_tokens):
    t = jnp.arange(n_tokens)
    half = x.shape[-1] // 2
    nf = half // 2
    inv = ROPE_BASE ** (-jnp.arange(nf, dtype=jnp.float32) / nf)
    bshape = (1, n_tokens) + (1,) * (x.ndim - 3) + (nf,)

    def rot(xs, pos):
        ang = (pos.astype(jnp.float32)[:, None] * inv).reshape(bshape)
        cos, sin = jnp.cos(ang), jnp.sin(ang)
        x1, x2 = xs[..., :nf], xs[..., nf:]
        return jnp.concatenate([x1 * cos - x2 * sin, x1 * sin + x2 * cos], axis=-1)

    out = jnp.concatenate([rot(x[..., :half], t // GRID_W), rot(x[..., half:], t % GRID_W)], axis=-1)
    return out.astype(x.dtype)


def modulation(cvec, w_ada, b_ada):
    m = (jax.nn.silu(cvec) @ w_ada + b_ada)[:, None, :]
    return jnp.split(m, 6, axis=-1)


def block_attention(q, k, v, scale):
    def one(qi):
        s = jnp.einsum('bqhd,bkhd->bhqk', qi, k).astype(jnp.float32) * scale
        p = jax.nn.softmax(s, axis=-1).astype(v.dtype)
        return jnp.einsum('bhqk,bkhd->bqhd', p, v)
    return from_blocks(lax.map(one, to_blocks(q, Q_BLOCK)))


def mlstm_scan(q, k, v, ig, lf, C0, n0, m0):
    tril = jnp.tril(jnp.ones((CHUNK, CHUNK), bool))

    def step(carry, inp):
        C, n, m = carry
        qc, kc, vc, ic, fc = inp
        b = jnp.moveaxis(jnp.cumsum(fc, axis=1), 2, 1)
        it = jnp.moveaxis(ic, 2, 1)
        dmat = jnp.where(tril, b[..., :, None] - b[..., None, :] + it[..., None, :], -jnp.inf)
        g = b + m[..., None]
        mt = jnp.maximum(g, jnp.max(dmat, axis=-1))
        s = jnp.einsum('blhd,bshd->bhls', qc, kc) * jnp.exp(dmat - mt[..., None])
        w_prev = jnp.exp(g - mt)
        num = jnp.einsum('bhls,bshe->bhle', s, vc) + jnp.einsum('blhd,bhde->bhle', qc, C) * w_prev[..., None]
        den = jnp.sum(s, axis=-1) + jnp.einsum('blhd,bhd->bhl', qc, n) * w_prev
        h = num / jnp.maximum(jnp.abs(den), jnp.exp(-mt))[..., None]
        bl = b[..., -1]
        lw = bl[..., None] - b + it
        m_new = jnp.maximum(bl + m, jnp.max(lw, axis=-1))
        ws = jnp.exp(lw - m_new[..., None])
        dec = jnp.exp(bl + m - m_new)
        C_new = dec[..., None, None] * C + jnp.einsum('bhs,bshd,bshe->bhde', ws, kc, vc)
        n_new = dec[..., None] * n + jnp.einsum('bhs,bshd->bhd', ws, kc)
        return (C_new, n_new, m_new), jnp.moveaxis(h, 1, 2)

    inp = tuple(to_blocks(t, CHUNK) for t in (q, k, v, ig, lf))
    (C, n, m), hs = lax.scan(step, (C0, n0, m0), inp)
    return from_blocks(hs), (C, n, m)


def mlstm_mixer(q, k, v, o, g_pre, gate_b, norm_w, state):
    B, S, _ = q.shape
    f32 = jnp.float32
    qh = q.reshape(B, S, MLSTM_HEADS, MLSTM_DK).astype(f32)
    kh = k.reshape(B, S, MLSTM_HEADS, MLSTM_DK).astype(f32) * (MLSTM_DK ** -0.5)
    vh = v.reshape(B, S, MLSTM_HEADS, MLSTM_DV).astype(f32)
    g = g_pre.reshape(B, S, 2, 2, MLSTM_HEADS).astype(f32) + gate_b.astype(f32)
    ig = g[:, :, :, 0]
    lf = jax.nn.log_sigmoid(g[:, :, :, 1])
    if state is None:
        C0 = jnp.zeros((B, 2, MLSTM_HEADS, MLSTM_DK, MLSTM_DV), f32)
        n0 = jnp.zeros((B, 2, MLSTM_HEADS, MLSTM_DK), f32)
        m0 = jnp.zeros((B, 2, MLSTM_HEADS), f32)
    else:
        C0, n0, m0 = (s.astype(f32) for s in state)
    hf, (Cf, nf, mf) = mlstm_scan(qh, kh, vh, ig[:, :, 0], lf[:, :, 0], C0[:, 0], n0[:, 0], m0[:, 0])
    hb, (Cb, nb, mb) = mlstm_scan(rev(qh), rev(kh), rev(vh), rev(ig[:, :, 1]), rev(lf[:, :, 1]), C0[:, 1], n0[:, 1], m0[:, 1])
    h = rms_norm(hf + rev(hb), norm_w.reshape(MLSTM_HEADS, MLSTM_DV))
    out = h.reshape(B, S, MIX_W) * jax.nn.sigmoid(o.astype(f32))
    new_state = (jnp.stack([Cf, Cb], axis=1), jnp.stack([nf, nb], axis=1), jnp.stack([mf, mb], axis=1))
    return out.astype(q.dtype), new_state


def na_mixer(q, k, v, qn_w, kn_w, rpb, ctx_kv):
    B, S, _ = q.shape
    H, D = NA_HEADS, NA_HEAD_DIM
    scale = D ** -0.5
    qh = rms_norm(q.reshape(B, S, H, D), qn_w)
    kh = rms_norm(k.reshape(B, S, H, D), kn_w)
    vh = v.reshape(B, S, H, D)
    if ctx_kv is None:
        out = block_attention(qh, kh, vh, scale)
        return out.reshape(B, S, MIX_W), (kh, vh)
    k_ctx, v_ctx = ctx_kv
    rows = S // GRID_W
    wr, wc = min(NA_WIN_R, rows), NA_WIN_C
    ri, ci = jnp.arange(rows), jnp.arange(GRID_W)
    row_idx = jnp.clip(ri - wr // 2, 0, rows - wr)[:, None] + jnp.arange(wr)
    col_idx = jnp.clip(ci - wc // 2, 0, GRID_W - wc)[:, None] + jnp.arange(wc)
    dr = row_idx - ri[:, None] + (NA_WIN_R - 1)
    dc = col_idx - ci[:, None] + (NA_WIN_C - 1)
    bias = rpb[:, dr[:, None, :, None], dc[None, :, None, :]]
    qg = qh.reshape(B, rows, GRID_W, H, D)
    k_win = kh.reshape(B, rows, GRID_W, H, D)[:, row_idx][:, :, :, col_idx]
    v_win = vh.reshape(B, rows, GRID_W, H, D)[:, row_idx][:, :, :, col_idx]
    s_win = jnp.einsum('bijhd,birjwhd->bhijrw', qg, k_win).astype(jnp.float32) * scale + bias[None].astype(jnp.float32)
    s_win = s_win.reshape(B, H, rows, GRID_W, wr * wc)
    s_ctx = jnp.einsum('bijhd,bthd->bhijt', qg, k_ctx).astype(jnp.float32) * scale
    p = jax.nn.softmax(jnp.concatenate([s_win, s_ctx], axis=-1), axis=-1).astype(v.dtype)
    p_win = p[..., :wr * wc].reshape(B, H, rows, GRID_W, wr, wc)
    p_ctx = p[..., wr * wc:]
    out = jnp.einsum('bhijrw,birjwhd->bijhd', p_win, v_win) + jnp.einsum('bhijt,bthd->bijhd', p_ctx, v_ctx)
    return out.reshape(B, S, MIX_W), None


def diff_attention(q, k, v, lam):
    scale = DIFF_QK_DIM ** -0.5

    def one(qi):
        s = jnp.einsum('bqhmd,bkhmd->bhmqk', qi, k).astype(jnp.float32) * scale
        p = jax.nn.softmax(s, axis=-1)
        a = (p[:, :, 0] - lam * p[:, :, 1]).astype(v.dtype)
        return jnp.einsum('bhqk,bkhd->bqhd', a, v)
    return from_blocks(lax.map(one, to_blocks(q, Q_BLOCK)))


def diff_mixer(q, k, v, qn_w, kn_w, lam_vecs, norm_w, lam_init, ctx_kv):
    B, S, _ = q.shape
    H, Dq = DIFF_HEADS, DIFF_QK_DIM
    qh = rms_norm(q.reshape(B, S, H, 2, Dq), qn_w)
    kh = rms_norm(k.reshape(B, S, H, 2, Dq), kn_w)
    vh = v.reshape(B, S, H, DIFF_V_DIM)
    lv = lam_vecs.astype(jnp.float32)
    lam = jnp.exp(jnp.sum(lv[0] * lv[1])) - jnp.exp(jnp.sum(lv[2] * lv[3])) + lam_init
    if ctx_kv is None:
        o = diff_attention(qh, kh, vh, lam)
        new = (kh.reshape(B, S, H, 2 * Dq), vh)
    else:
        k_ctx, v_ctx = ctx_kv
        keys = jnp.concatenate([rope_2d(kh, S), k_ctx.reshape(B, k_ctx.shape[1], H, 2, Dq).astype(kh.dtype)], axis=1)
        vals = jnp.concatenate([vh, v_ctx.astype(vh.dtype)], axis=1)
        o = diff_attention(rope_2d(qh, S), keys, vals, lam)
        new = None
    o = rms_norm(o, norm_w) * (1.0 - lam_init)
    return o.reshape(B, S, MIX_W), new


def centred_dwconv(x, w, b):
    pad = w.shape[0] // 2
    y = lax.conv_general_dilated(x, w[:, None, :].astype(x.dtype), window_strides=(1,), padding=[(pad, pad)],
                                 dimension_numbers=('NWC', 'WIO', 'NWC'), feature_group_count=x.shape[-1])
    return y + b


def ssd_scan(x, dt, A, Bm, Cm, h0):
    tril = jnp.tril(jnp.ones((CHUNK, CHUNK), bool))
    xdt = x * dt[..., None]
    a = dt * A

    def step(h, inp):
        xc, ac, bc, cc = inp
        acum = jnp.moveaxis(jnp.cumsum(ac, axis=1), 2, 1)
        seg = jnp.where(tril, acum[..., :, None] - acum[..., None, :], -jnp.inf)
        scores = jnp.einsum('blhn,bshn->bhls', cc, bc) * jnp.exp(seg)
        y = jnp.einsum('bhls,bshp->blhp', scores, xc) + jnp.einsum('blhn,bhpn->blhp', cc, h) * jnp.moveaxis(jnp.exp(acum), 1, 2)[..., None]
        dec = jnp.exp(acum[..., -1:] - acum)
        h_new = jnp.exp(acum[..., -1])[..., None, None] * h + jnp.einsum('bhs,bshn,bshp->bhpn', dec, bc, xc)
        return h_new, y

    inp = tuple(to_blocks(t, CHUNK) for t in (xdt, a, Bm, Cm))
    h_fin, ys = lax.scan(step, h0, inp)
    return from_blocks(ys), h_fin


def ssm_mixer(z, xbc, dt_pre, conv_w, conv_b, dt_bias, a_log, d_skip, norm_w, state):
    B, S, _ = z.shape
    f32 = jnp.float32
    xbc = jax.nn.silu(centred_dwconv(xbc, conv_w, conv_b))
    xs, bm, cm = jnp.split(xbc, [MIX_W, MIX_W + SSM_GROUPS * SSM_STATE], axis=-1)
    rep = SSM_HEADS // SSM_GROUPS
    x = xs.reshape(B, S, SSM_HEADS, SSM_HEAD_DIM).astype(f32)
    bm = jnp.repeat(bm.reshape(B, S, SSM_GROUPS, SSM_STATE), rep, axis=2).astype(f32)
    cm = jnp.repeat(cm.reshape(B, S, SSM_GROUPS, SSM_STATE), rep, axis=2).astype(f32)
    dt = jax.nn.softplus(dt_pre.reshape(B, S, 2, SSM_HEADS).astype(f32) + dt_bias.astype(f32))
    A = -jnp.exp(a_log.astype(f32))
    if state is None:
        h0 = jnp.zeros((B, 2, SSM_HEADS, SSM_HEAD_DIM, SSM_STATE), f32)
    else:
        h0 = state.astype(f32)
    yf, hf = ssd_scan(x, dt[:, :, 0], A[0], bm, cm, h0[:, 0])
    yb, hb = ssd_scan(rev(x), rev(dt[:, :, 1]), A[1], rev(bm), rev(cm), h0[:, 1])
    y = yf + rev(yb) + d_skip.astype(f32)[:, None] * x
    y = rms_norm(y.reshape(B, S, MIX_W) * jax.nn.silu(z.astype(f32)), norm_w)
    return y.astype(z.dtype), jnp.stack([hf, hb], axis=1)


def expert_choice_ffn(h, w_router, w_gate, w_up, w_down):
    B, T, D = h.shape
    cap = CAPACITY_FACTOR * T // N_EXPERTS
    aff = jax.nn.softmax((h @ w_router).astype(jnp.float32), axis=-1)
    g, idx = lax.top_k(jnp.swapaxes(aff, 1, 2), cap)
    xg = jax.vmap(lambda xb, ib: xb[ib])(h, idx)
    a = jnp.einsum('becd,edf->becf', xg, w_gate)
    u = jnp.einsum('becd,edf->becf', xg, w_up)
    y = jnp.einsum('becf,efd->becd', jax.nn.silu(a) * u, w_down) * g[..., None].astype(h.dtype)
    return jax.vmap(lambda yb, ib: jnp.zeros((T, D), yb.dtype).at[ib.reshape(-1)].add(yb.reshape(-1, D)))(y, idx)


def trunk_layer(x, cvec, p, lam_init, ctx):
    B, S, _ = x.shape
    shift1, scale1, gate1, shift2, scale2, gate2 = modulation(cvec, p['w_ada'], p['b_ada'])
    h = rms_norm(x, p['norm1_w']) * (1 + scale1) + shift1
    (mq, mk, mv, mo, mg, nq, nk, nv, dq, dk, dv, sz, sxbc, sdt, gpre) = split_cols(h @ p['w_in'])
    if ctx is None:
        na_ctx, diff_ctx, m_state, s_state = None, None, None, None
    else:
        na_ctx, diff_ctx, m_state, s_state = ctx
    a_out, m_new = mlstm_mixer(mq, mk, mv, mo, mg, p['mlstm_gate_b'], p['mlstm_norm_w'], m_state)
    b_out, na_new = na_mixer(nq, nk, nv, p['na_qnorm_w'], p['na_knorm_w'], p['na_rpb'], na_ctx)
    c_out, diff_new = diff_mixer(dq, dk, dv, p['diff_qnorm_w'], p['diff_knorm_w'], p['diff_lambda'], p['diff_norm_w'], lam_init, diff_ctx)
    d_out, s_new = ssm_mixer(sz, sxbc, sdt, p['ssm_conv_w'], p['ssm_conv_b'], p['ssm_dt_bias'], p['ssm_a_log'], p['ssm_d'], p['ssm_norm_w'], s_state)
    br = jnp.stack([a_out, b_out, c_out, d_out], axis=2)
    gates = jax.nn.sigmoid(gpre.reshape(B, S, N_BRANCH, D_MODEL))
    mixed = jnp.sum(gates * jnp.einsum('bsnc,ncd->bsnd', br, p['w_branch']), axis=2)
    x = x + gate1 * (mixed @ p['w_out'])
    h2 = rms_norm(x, p['norm2_w']) * (1 + scale2) + shift2
    x = x + gate2 * expert_choice_ffn(h2, p['w_router'], p['w_gate'], p['w_up'], p['w_down'])
    return x, (na_new, diff_new, m_new, s_new)


def setup_inputs(seed: int = 0) -> dict:
    key = jax.random.key(seed)
    ks = iter(jax.random.split(key, 64))
    f32 = jnp.float32

    def nrm(shape, s=1.0):
        return s * jax.random.normal(next(ks), shape, f32)

    D = D_MODEL
    x_prompt = nrm((BATCH, SEQ, D))
    x_sample = nrm((DEC_BATCH, DEC_SEQ, D))
    cache_na_k = nrm((DEC_BATCH, DEPTH, PAST_LEN, NA_HEADS, NA_HEAD_DIM))
    cache_na_v = nrm((DEC_BATCH, DEPTH, PAST_LEN, NA_HEADS, NA_HEAD_DIM))
    cache_diff_k = nrm((DEC_BATCH, DEPTH, PAST_LEN, DIFF_HEADS, 2 * DIFF_QK_DIM))
    cache_diff_v = nrm((DEC_BATCH, DEPTH, PAST_LEN, DIFF_HEADS, DIFF_V_DIM))
    state_mlstm_C = nrm((DEC_BATCH, DEPTH, 2, MLSTM_HEADS, MLSTM_DK, MLSTM_DV), 0.5)
    state_mlstm_n = nrm((DEC_BATCH, DEPTH, 2, MLSTM_HEADS, MLSTM_DK), 0.5)
    state_mlstm_m = nrm((DEC_BATCH, DEPTH, 2, MLSTM_HEADS))
    state_ssm = nrm((DEC_BATCH, DEPTH, 2, SSM_HEADS, SSM_HEAD_DIM, SSM_STATE), 0.5)
    c = nrm((DEC_BATCH, D))
    c_ctx = nrm((D,))
    norm1_w = 1.0 + nrm((DEPTH, D), 0.02)
    norm2_w = 1.0 + nrm((DEPTH, D), 0.02)
    w_ada = nrm((DEPTH, D, 6 * D), 0.5 * D ** -0.5)
    b_ada = nrm((DEPTH, 6 * D), 0.02)
    w_in = nrm((DEPTH, D, IN_COLS), D ** -0.5)
    mlstm_gate_b = jnp.concatenate([nrm((DEPTH, 2, 1, MLSTM_HEADS), 0.1), 3.0 + nrm((DEPTH, 2, 1, MLSTM_HEADS), 0.5)], axis=2)
    mlstm_norm_w = 1.0 + nrm((DEPTH, MIX_W), 0.02)
    na_qnorm_w = 1.0 + nrm((DEPTH, NA_HEAD_DIM), 0.02)
    na_knorm_w = 1.0 + nrm((DEPTH, NA_HEAD_DIM), 0.02)
    na_rpb = nrm((DEPTH, NA_HEADS, 2 * NA_WIN_R - 1, 2 * NA_WIN_C - 1), 0.1)
    diff_qnorm_w = 1.0 + nrm((DEPTH, DIFF_QK_DIM), 0.02)
    diff_knorm_w = 1.0 + nrm((DEPTH, DIFF_QK_DIM), 0.02)
    diff_lambda = nrm((DEPTH, 4, DIFF_QK_DIM), 0.1)
    diff_norm_w = 1.0 + nrm((DEPTH, DIFF_V_DIM), 0.02)
    ssm_conv_w = nrm((DEPTH, SSM_CONV_K, SSM_CONV_CH), SSM_CONV_K ** -0.5)
    ssm_conv_b = nrm((DEPTH, SSM_CONV_CH), 0.02)
    u = jax.random.uniform(next(ks), (DEPTH, 2, SSM_HEADS), f32)
    dt0 = jnp.exp(u * (math.log(0.1) - math.log(0.001)) + math.log(0.001))
    ssm_dt_bias = dt0 + jnp.log(-jnp.expm1(-dt0))
    ssm_a_log = jnp.log(jax.random.uniform(next(ks), (DEPTH, 2, SSM_HEADS), f32, minval=1.0, maxval=16.0))
    ssm_d = 1.0 + nrm((DEPTH, SSM_HEADS), 0.1)
    ssm_norm_w = 1.0 + nrm((DEPTH, MIX_W), 0.02)
    w_branch = nrm((DEPTH, N_BRANCH, MIX_W, D), MIX_W ** -0.5)
    w_out = nrm((DEPTH, D, D), D ** -0.5)
    w_router = nrm((DEPTH, D, N_EXPERTS), D ** -0.5)
    w_gate = nrm((DEPTH, N_EXPERTS, D, EXPERT_FF), D ** -0.5)
    w_up = nrm((DEPTH, N_EXPERTS, D, EXPERT_FF), D ** -0.5)
    w_down = nrm((DEPTH, N_EXPERTS, EXPERT_FF, D), EXPERT_FF ** -0.5)
    return {'x_prompt': x_prompt, 'x_sample': x_sample, 'cache_na_k': cache_na_k, 'cache_na_v': cache_na_v,
            'cache_diff_k': cache_diff_k, 'cache_diff_v': cache_diff_v, 'state_mlstm_C': state_mlstm_C,
            'state_mlstm_n': state_mlstm_n, 'state_mlstm_m': state_mlstm_m, 'state_ssm': state_ssm,
            'c': c, 'c_ctx': c_ctx, 'norm1_w': norm1_w, 'norm2_w': norm2_w, 'w_ada': w_ada, 'b_ada': b_ada,
            'w_in': w_in, 'mlstm_gate_b': mlstm_gate_b, 'mlstm_norm_w': mlstm_norm_w, 'na_qnorm_w': na_qnorm_w,
            'na_knorm_w': na_knorm_w, 'na_rpb': na_rpb, 'diff_qnorm_w': diff_qnorm_w, 'diff_knorm_w': diff_knorm_w,
            'diff_lambda': diff_lambda, 'diff_norm_w': diff_norm_w, 'ssm_conv_w': ssm_conv_w, 'ssm_conv_b': ssm_conv_b,
            'ssm_dt_bias': ssm_dt_bias, 'ssm_a_log': ssm_a_log, 'ssm_d': ssm_d, 'ssm_norm_w': ssm_norm_w,
            'w_branch': w_branch, 'w_out': w_out, 'w_router': w_router, 'w_gate': w_gate, 'w_up': w_up, 'w_down': w_down}


def reference(x_prompt, x_sample, cache_na_k, cache_na_v, cache_diff_k, cache_diff_v, state_mlstm_C, state_mlstm_n,
              state_mlstm_m, state_ssm, c, c_ctx, norm1_w, norm2_w, w_ada, b_ada, w_in, mlstm_gate_b, mlstm_norm_w,
              na_qnorm_w, na_knorm_w, na_rpb, diff_qnorm_w, diff_knorm_w, diff_lambda, diff_norm_w, ssm_conv_w,
              ssm_conv_b, ssm_dt_bias, ssm_a_log, ssm_d, ssm_norm_w, w_branch, w_out, w_router, w_gate, w_up, w_down):
    y_prompt, y_sample = x_prompt, x_sample
    na_k_l, na_v_l, df_k_l, df_v_l, mC_l, mn_l, mm_l, ss_l = [], [], [], [], [], [], [], []
    for l in range(DEPTH):
        p = {'norm1_w': norm1_w[l], 'norm2_w': norm2_w[l], 'w_ada': w_ada[l], 'b_ada': b_ada[l], 'w_in': w_in[l],
             'mlstm_gate_b': mlstm_gate_b[l], 'mlstm_norm_w': mlstm_norm_w[l], 'na_qnorm_w': na_qnorm_w[l],
             'na_knorm_w': na_knorm_w[l], 'na_rpb': na_rpb[l], 'diff_qnorm_w': diff_qnorm_w[l],
             'diff_knorm_w': diff_knorm_w[l], 'diff_lambda': diff_lambda[l], 'diff_norm_w': diff_norm_w[l],
             'ssm_conv_w': ssm_conv_w[l], 'ssm_conv_b': ssm_conv_b[l], 'ssm_dt_bias': ssm_dt_bias[l],
             'ssm_a_log': ssm_a_log[l], 'ssm_d': ssm_d[l], 'ssm_norm_w': ssm_norm_w[l], 'w_branch': w_branch[l],
             'w_out': w_out[l], 'w_router': w_router[l], 'w_gate': w_gate[l], 'w_up': w_up[l], 'w_down': w_down[l]}
        lam_init = 0.8 - 0.6 * math.exp(-0.3 * l)
        y_prompt, (na_kv, diff_kv, m_st, s_st) = trunk_layer(y_prompt, c_ctx[None, :], p, lam_init, None)
        na_k_l.append(na_kv[0]); na_v_l.append(na_kv[1])
        df_k_l.append(diff_kv[0]); df_v_l.append(diff_kv[1])
        mC_l.append(m_st[0]); mn_l.append(m_st[1]); mm_l.append(m_st[2]); ss_l.append(s_st)
        ctx = ((cache_na_k[:, l], cache_na_v[:, l]), (cache_diff_k[:, l], cache_diff_v[:, l]),
               (state_mlstm_C[:, l], state_mlstm_n[:, l], state_mlstm_m[:, l]), state_ssm[:, l])
        y_sample, _ = trunk_layer(y_sample, c, p, lam_init, ctx)
    new_na_k = jnp.stack(na_k_l, axis=1)
    new_na_v = jnp.stack(na_v_l, axis=1)
    new_diff_k = jnp.stack(df_k_l, axis=1)
    new_diff_v = jnp.stack(df_v_l, axis=1)
    new_mlstm_C = jnp.stack(mC_l, axis=1)
    new_mlstm_n = jnp.stack(mn_l, axis=1)
    new_mlstm_m = jnp.stack(mm_l, axis=1)
    new_ssm = jnp.stack(ss_l, axis=1)
    return (y_prompt, y_sample, new_na_k, new_na_v, new_diff_k, new_diff_v, new_mlstm_C, new_mlstm_n, new_mlstm_m, new_ssm)
```

```python
import functools
import math

import jax
import jax.numpy as jnp
from jax import lax
from jax.experimental import pallas as pl
from jax.experimental.pallas import tpu as pltpu

f32 = jnp.float32
bf16 = jnp.bfloat16

D_MODEL = 1024
BATCH = 32
SEQ = 256
DEPTH = 2
DEC_BATCH = 4
DEC_SEQ = 1024
PAST_LEN = 512
GRID_W = 64
MIX_W = D_MODEL // 4
N_BRANCH = 4
HEADS = 4
HEAD_W = MIX_W // HEADS
NA_WIN_R = 8
NA_WIN_C = 16
DIFF_QK_DIM = HEAD_W // 2
SSM_GROUPS = 2
SSM_STATE = 64
SSM_CONV_K = 5
SSM_CONV_CH = MIX_W + 2 * SSM_GROUPS * SSM_STATE
CHUNK = 64
N_EXPERTS = 16
EXPERT_FF = 1024
CAPACITY_FACTOR = 2
ROPE_BASE = 10000.0
EPS = 1e-6
NEG = -1e30

LANE = 128
VMEM_LIMIT = 56 * 1024 * 1024

COL_GPRE = 0
COL_MLSTM = 4096
COL_NA = 5120
COL_DIFF = 5888
COL_XBC = 6656
COL_Z = 7168
U_COLS = 7424
G_MLSTM = 0
G_DT = 16


def _params(sem):
    return pltpu.CompilerParams(dimension_semantics=sem, vmem_limit_bytes=VMEM_LIMIT)


def _mm(a, b):
    return jnp.dot(a.astype(bf16), b.astype(bf16), preferred_element_type=f32)


def _mm_nt(a, b):
    return lax.dot_general(a.astype(bf16), b.astype(bf16), (((1,), (1,)), ((), ())), preferred_element_type=f32)


def _mm_tn(a, b):
    return lax.dot_general(a.astype(bf16), b.astype(bf16), (((0,), (0,)), ((), ())), preferred_element_type=f32)


def _seg_sum(x, seg):
    w = x.shape[-1]
    r = lax.broadcasted_iota(jnp.int32, (w, w), 0) // seg
    c = lax.broadcasted_iota(jnp.int32, (w, w), 1) // seg
    ones = jnp.where(r == c, 1.0, 0.0).astype(bf16)
    hi = x.astype(bf16)
    lo = (x - hi.astype(f32)).astype(bf16)
    return jnp.dot(hi, ones, preferred_element_type=f32) + jnp.dot(lo, ones, preferred_element_type=f32)


def _seg_rms(x, seg, w_row):
    ms = _seg_sum(x * x, seg) * (1.0 / seg)
    return x * lax.rsqrt(ms + EPS) * w_row


def _tri_masks():
    r = lax.broadcasted_iota(jnp.int32, (CHUNK, CHUNK), 0)
    c = lax.broadcasted_iota(jnp.int32, (CHUNK, CHUNK), 1)
    return r >= c, r <= c


def _mod_body(c_ref, w_ref, b_ref, o_ref):
    o_ref[...] = _mm(jax.nn.silu(c_ref[...]), w_ref[...]) + b_ref[...]


def _modulation(cc, w_ada, b_ada):
    tn = 512
    n = w_ada.shape[1]
    return pl.pallas_call(
        _mod_body,
        grid=(n // tn,),
        in_specs=[pl.BlockSpec((8, D_MODEL), lambda j: (0, 0)),
                  pl.BlockSpec((D_MODEL, tn), lambda j: (0, j)),
                  pl.BlockSpec((1, tn), lambda j: (0, j))],
        out_specs=pl.BlockSpec((8, tn), lambda j: (0, j)),
        out_shape=jax.ShapeDtypeStruct((8, n), f32),
        compiler_params=_params(("arbitrary",)),
        name="modulation",
    )(cc, w_ada, b_ada.reshape(1, n))


def _inproj_body(x_ref, mod_ref, nw_ref, w_ref, wg_ref, u_ref, g_ref, h_scr):
    @pl.when(pl.program_id(1) == 0)
    def _():
        x = x_ref[...]
        y = x * lax.rsqrt(jnp.mean(x * x, axis=-1, keepdims=True) + EPS) * nw_ref[...]
        h = y * (1.0 + mod_ref[:, D_MODEL:2 * D_MODEL]) + mod_ref[:, 0:D_MODEL]
        hb = h.astype(bf16)
        h_scr[...] = hb
        g_ref[...] = jnp.dot(hb, wg_ref[...], preferred_element_type=f32)

    u_ref[...] = jnp.dot(h_scr[...], w_ref[...], preferred_element_type=f32).astype(u_ref.dtype)


def _inproj(x, mod3, mod_base, mod_stride, seq, nw_row, w_main, w_side):
    n = x.shape[0]
    tm, tn = 512, 256
    tps = seq // tm if seq >= tm else 1
    return pl.pallas_call(
        _inproj_body,
        grid=(n // tm, U_COLS // tn),
        in_specs=[pl.BlockSpec((tm, D_MODEL), lambda i, j: (i, 0)),
                  pl.BlockSpec((None, 1, 6 * D_MODEL), lambda i, j: (mod_base + mod_stride * (i // tps), 0, 0)),
                  pl.BlockSpec((1, D_MODEL), lambda i, j: (0, 0)),
                  pl.BlockSpec((D_MODEL, tn), lambda i, j: (0, j)),
                  pl.BlockSpec((D_MODEL, LANE), lambda i, j: (0, 0))],
        out_specs=[pl.BlockSpec((tm, tn), lambda i, j: (i, j)),
                   pl.BlockSpec((tm, LANE), lambda i, j: (i, 0))],
        out_shape=[jax.ShapeDtypeStruct((n, U_COLS), f32), jax.ShapeDtypeStruct((n, LANE), f32)],
        scratch_shapes=[pltpu.VMEM((tm, D_MODEL), bf16)],
        compiler_params=_params(("arbitrary", "arbitrary")),
        name="inproj",
    )(x, mod3, nw_row, w_main, w_side)


def _mlstm_body(seq, has_state, *refs):
    if has_state:
        (qkvo_ref, g_ref, gb_ref, nw_ref, c0_ref, n0_ref, m0_ref,
         out_ref, cs_ref, ns_ref, ms_ref, gcol_scr, lcol_scr, rows_scr, hf_scr, hb_scr) = refs
    else:
        (qkvo_ref, g_ref, gb_ref, nw_ref,
         out_ref, cs_ref, ns_ref, ms_ref, gcol_scr, lcol_scr, rows_scr, hf_scr, hb_scr) = refs
    nc = seq // CHUNK
    g = g_ref[...] + gb_ref[...]
    ls = jax.nn.log_sigmoid(g)
    gcol_scr[...] = g
    lcol_scr[...] = ls
    gt = g.T
    lst = ls.T
    rsel = (lax.broadcasted_iota(jnp.int32, (16, CHUNK), 0) % 8) < HEADS
    for c in range(nc):
        sl = slice(c * CHUNK, (c + 1) * CHUNK)
        rows_scr[c] = jnp.where(rsel, gt[0:16, sl], lst[0:16, sl])
    if has_state:
        cs_ref[...] = c0_ref[...]
        ns_ref[...] = n0_ref[...]
        ms_ref[...] = m0_ref[...]
    else:
        cs_ref[...] = jnp.zeros(cs_ref.shape, f32)
        ns_ref[...] = jnp.zeros(ns_ref.shape, f32)
        ms_ref[...] = jnp.zeros(ms_ref.shape, f32)
    tril, triu = _tri_masks()

    def step(c, carry):
        for d in range(2):
            cidx = c if d == 0 else nc - 1 - c
            r0 = pl.multiple_of(cidx * CHUNK, CHUNK)
            rows = rows_scr[cidx]
            gc = gcol_scr[pl.ds(r0, CHUNK), :]
            lc = lcol_scr[pl.ds(r0, CHUNK), :]
            valid, other = (tril, triu) if d == 0 else (triu, tril)
            h_scr = hf_scr if d == 0 else hb_scr
            last = CHUNK - 1 if d == 0 else 0
            for h in range(HEADS):
                lanes = slice(h * HEAD_W, (h + 1) * HEAD_W)
                q = qkvo_ref[pl.ds(r0, CHUNK), h * HEAD_W:(h + 1) * HEAD_W]
                k = qkvo_ref[pl.ds(r0, CHUNK), MIX_W + h * HEAD_W:MIX_W + (h + 1) * HEAD_W] * (HEAD_W ** -0.5)
                v = qkvo_ref[pl.ds(r0, CHUNK), 2 * MIX_W + h * HEAD_W:2 * MIX_W + (h + 1) * HEAD_W]
                gi = G_MLSTM + d * 8 + h
                icol = gc[:, gi:gi + 1]
                fcol = lc[:, gi + 4:gi + 5]
                irow = rows[d * 8 + h:d * 8 + h + 1, :]
                frow = rows[d * 8 + 4 + h:d * 8 + 5 + h, :]
                bcol = jnp.sum(jnp.where(valid, frow, 0.0), axis=1, keepdims=True)
                brow = jnp.sum(jnp.where(other, fcol, 0.0), axis=0, keepdims=True)
                m_old = ms_ref[d:d + 1, h:h + 1]
                c_old = cs_ref[d, h]
                n_old = ns_ref[d, h:h + 1, :]
                dmat = jnp.where(valid, bcol - brow + irow, NEG)
                gg = bcol + m_old
                mt = jnp.maximum(gg, jnp.max(dmat, axis=1, keepdims=True))
                smat = _mm_nt(q, k) * jnp.exp(dmat - mt)
                w_prev = jnp.exp(gg - mt)
                num = _mm(smat, v) + _mm(q, c_old) * w_prev
                den = jnp.sum(smat, axis=1, keepdims=True) + jnp.sum(q * n_old, axis=1, keepdims=True) * w_prev
                hout = num / jnp.maximum(jnp.abs(den), jnp.exp(-mt))
                h_scr[pl.ds(r0, CHUNK), lanes] = hout
                bl = bcol[last:last + 1, :]
                lw_row = bl - brow + irow
                lw_col = bl - bcol + icol
                m_new = jnp.maximum(bl + m_old, jnp.max(lw_row, axis=1, keepdims=True))
                dec = jnp.exp(bl + m_old - m_new)
                kw = k * jnp.exp(lw_col - m_new)
                cs_ref[d, h] = dec * c_old + _mm_tn(kw, v)
                ns_ref[d, h:h + 1, :] = dec * n_old + jnp.sum(kw, axis=0, keepdims=True)
                ms_ref[d:d + 1, h:h + 1] = m_new
        return carry

    lax.fori_loop(0, nc, step, 0)

    blk = 256

    def fin(i, carry):
        r0 = pl.multiple_of(i * blk, blk)
        hs = hf_scr[pl.ds(r0, blk), :] + hb_scr[pl.ds(r0, blk), :]
        hn = _seg_rms(hs, HEAD_W, nw_ref[...])
        o = qkvo_ref[pl.ds(r0, blk), 3 * MIX_W:4 * MIX_W]
        out_ref[pl.ds(r0, blk), :] = hn * jax.nn.sigmoid(o)
        return carry

    lax.fori_loop(0, seq // blk, fin, 0)


def _mlstm(u, g, nb, seq, gb_row, nw_row, state):
    has_state = state is not None
    nc = seq // CHUNK
    in_specs = [pl.BlockSpec((seq, 4 * MIX_W), lambda b: (b, COL_MLSTM // (4 * MIX_W))),
                pl.BlockSpec((seq, LANE), lambda b: (b, 0)),
                pl.BlockSpec((1, LANE), lambda b: (0, 0)),
                pl.BlockSpec((1, MIX_W), lambda b: (0, 0))]
    args = [u, g, gb_row, nw_row]
    if has_state:
        in_specs += [pl.BlockSpec((None, 2, HEADS, HEAD_W, HEAD_W), lambda b: (b, 0, 0, 0, 0)),
                     pl.BlockSpec((None, 2, HEADS, HEAD_W), lambda b: (b, 0, 0, 0)),
                     pl.BlockSpec((None, 2, HEADS), lambda b: (b, 0, 0))]
        args += list(state)
    return pl.pallas_call(
        functools.partial(_mlstm_body, seq, has_state),
        grid=(nb,),
        in_specs=in_specs,
        out_specs=[pl.BlockSpec((seq, MIX_W), lambda b: (b, 0)),
                   pl.BlockSpec((None, 2, HEADS, HEAD_W, HEAD_W), lambda b: (b, 0, 0, 0, 0)),
                   pl.BlockSpec((None, 2, HEADS, HEAD_W), lambda b: (b, 0, 0, 0)),
                   pl.BlockSpec((None, 2, HEADS), lambda b: (b, 0, 0))],
        out_shape=[jax.ShapeDtypeStruct((nb * seq, MIX_W), f32),
                   jax.ShapeDtypeStruct((nb, 2, HEADS, HEAD_W, HEAD_W), f32),
                   jax.ShapeDtypeStruct((nb, 2, HEADS, HEAD_W), f32),
                   jax.ShapeDtypeStruct((nb, 2, HEADS), f32)],
        scratch_shapes=[pltpu.VMEM((seq, LANE), f32), pltpu.VMEM((seq, LANE), f32),
                        pltpu.VMEM((nc, 16, CHUNK), f32),
                        pltpu.VMEM((seq, MIX_W), f32), pltpu.VMEM((seq, MIX_W), f32)],
        compiler_params=_params(("arbitrary",)),
        name="mlstm_state" if has_state else "mlstm",
    )(*args)


def _ssd_body(seq, has_state, *refs):
    if has_state:
        (z_ref, xbc_ref, g_ref, cw_ref, cb_ref, dtb_ref, alog_ref, dskip_ref, nw_ref, h0_ref,
         out_ref, hs_ref, xpad_scr, xc_scr, dt_scr, a_scr, rows_scr, yf_scr, yb_scr) = refs
    else:
        (z_ref, xbc_ref, g_ref, cw_ref, cb_ref, dtb_ref, alog_ref, dskip_ref, nw_ref,
         out_ref, hs_ref, xpad_scr, xc_scr, dt_scr, a_scr, rows_scr, yf_scr, yb_scr) = refs
    nc = seq // CHUNK
    pad = 8
    blk = 256
    nblk = seq // blk
    xpad_scr[0:pad, :] = jnp.zeros((pad, SSM_CONV_CH), f32)
    xpad_scr[pad + seq:2 * pad + seq, :] = jnp.zeros((pad, SSM_CONV_CH), f32)
    xpad_scr[pad:pad + seq, :] = xbc_ref[...].astype(f32)

    cblk = 128
    for i in range(seq // cblk):
        r0 = i * cblk
        acc = jnp.zeros((cblk, SSM_CONV_CH), f32) + cb_ref[...]
        for kk in range(SSM_CONV_K):
            off = r0 + pad - SSM_CONV_K // 2 + kk
            acc = acc + xpad_scr[off:off + cblk, :] * cw_ref[kk:kk + 1, :]
        xc_scr[r0:r0 + cblk, :] = jax.nn.silu(acc)

    dt = jax.nn.softplus(g_ref[...] + dtb_ref[...])
    a = dt * (-jnp.exp(alog_ref[...]))
    dt_scr[...] = dt
    a_scr[...] = a
    at = a.T
    for c in range(nc):
        rows_scr[c] = at[G_DT:G_DT + 8, c * CHUNK:(c + 1) * CHUNK]
    if has_state:
        hs_ref[...] = h0_ref[...]
    else:
        hs_ref[...] = jnp.zeros(hs_ref.shape, f32)
    tril, triu = _tri_masks()
    rep = HEADS // SSM_GROUPS

    def step(c, carry):
        for d in range(2):
            cidx = c if d == 0 else nc - 1 - c
            r0 = pl.multiple_of(cidx * CHUNK, CHUNK)
            rows = rows_scr[cidx]
            ac = a_scr[pl.ds(r0, CHUNK), :]
            dc = dt_scr[pl.ds(r0, CHUNK), :]
            valid, other = (tril, triu) if d == 0 else (triu, tril)
            y_scr = yf_scr if d == 0 else yb_scr
            last = CHUNK - 1 if d == 0 else 0
            for grp in range(SSM_GROUPS):
                bm = xc_scr[pl.ds(r0, CHUNK), MIX_W + grp * SSM_STATE:MIX_W + (grp + 1) * SSM_STATE]
                cm = xc_scr[pl.ds(r0, CHUNK), MIX_W + (SSM_GROUPS + grp) * SSM_STATE:MIX_W + (SSM_GROUPS + grp + 1) * SSM_STATE]
                cb = _mm_nt(cm, bm)
                for h in range(grp * rep, (grp + 1) * rep):
                    lanes = slice(h * HEAD_W, (h + 1) * HEAD_W)
                    gi = G_DT + d * HEADS + h
                    acol = ac[:, gi:gi + 1]
                    dtcol = dc[:, gi:gi + 1]
                    arow = rows[d * HEADS + h:d * HEADS + h + 1, :]
                    cum_col = jnp.sum(jnp.where(valid, arow, 0.0), axis=1, keepdims=True)
                    cum_row = jnp.sum(jnp.where(other, acol, 0.0), axis=0, keepdims=True)
                    decay = jnp.exp(jnp.where(valid, cum_col - cum_row, NEG))
                    xdt = xc_scr[pl.ds(r0, CHUNK), h * HEAD_W:(h + 1) * HEAD_W] * dtcol
                    h_old = hs_ref[d, h]
                    y = _mm(cb * decay, xdt) + _mm_nt(cm, h_old) * jnp.exp(cum_col)
                    y_scr[pl.ds(r0, CHUNK), lanes] = y
                    al = cum_col[last:last + 1, :]
                    hs_ref[d, h] = jnp.exp(al) * h_old + _mm_tn(xdt * jnp.exp(al - cum_col), bm)
        return carry

    lax.fori_loop(0, nc, step, 0)

    def fin(i, carry):
        r0 = pl.multiple_of(i * blk, blk)
        y = yf_scr[pl.ds(r0, blk), :] + yb_scr[pl.ds(r0, blk), :] + dskip_ref[...] * xc_scr[pl.ds(r0, blk), 0:MIX_W]
        y = y * jax.nn.silu(z_ref[pl.ds(r0, blk), :].astype(f32))
        out_ref[pl.ds(r0, blk), :] = y * lax.rsqrt(jnp.mean(y * y, axis=-1, keepdims=True) + EPS) * nw_ref[...]
        return carry

    lax.fori_loop(0, nblk, fin, 0)


def _ssd(u, g, nb, seq, cw, cb_row, dtb_row, alog_row, dskip_row, nw_row, state):
    has_state = state is not None
    nc = seq // CHUNK
    const = lambda shape: pl.BlockSpec(shape, lambda b: (0,) * len(shape))
    in_specs = [pl.BlockSpec((seq, MIX_W), lambda b: (b, COL_Z // MIX_W)),
                pl.BlockSpec((seq, SSM_CONV_CH), lambda b: (b, COL_XBC // SSM_CONV_CH)),
                pl.BlockSpec((seq, LANE), lambda b: (b, 0)),
                const((SSM_CONV_K, SSM_CONV_CH)), const((1, SSM_CONV_CH)), const((1, LANE)), const((1, LANE)),
                const((1, MIX_W)), const((1, MIX_W))]
    args = [u, u, g, cw, cb_row, dtb_row, alog_row, dskip_row, nw_row]
    if has_state:
        in_specs.append(pl.BlockSpec((None, 2, HEADS, HEAD_W, SSM_STATE), lambda b: (b, 0, 0, 0, 0)))
        args.append(state)
    return pl.pallas_call(
        functools.partial(_ssd_body, seq, has_state),
        grid=(nb,),
        in_specs=in_specs,
        out_specs=[pl.BlockSpec((seq, MIX_W), lambda b: (b, 0)),
                   pl.BlockSpec((None, 2, HEADS, HEAD_W, SSM_STATE), lambda b: (b, 0, 0, 0, 0))],
        out_shape=[jax.ShapeDtypeStruct((nb * seq, MIX_W), f32),
                   jax.ShapeDtypeStruct((nb, 2, HEADS, HEAD_W, SSM_STATE), f32)],
        scratch_shapes=[pltpu.VMEM((seq + 16, SSM_CONV_CH), f32), pltpu.VMEM((seq, SSM_CONV_CH), f32),
                        pltpu.VMEM((seq, LANE), f32), pltpu.VMEM((seq, LANE), f32),
                        pltpu.VMEM((nc, 8, CHUNK), f32),
                        pltpu.VMEM((seq, MIX_W), f32), pltpu.VMEM((seq, MIX_W), f32)],
        compiler_params=_params(("arbitrary",)),
        name="ssd_state" if has_state else "ssd",
    )(*args)


def _softmax_pv(scores, values, hl):
    m = scores[0].max(axis=1, keepdims=True)
    for s in scores[1:]:
        m = jnp.maximum(m, s.max(axis=1, keepdims=True))
    acc, den = None, None
    for s, v in zip(scores, values):
        p = jnp.exp(s - m)
        d = jnp.sum(p, axis=1, keepdims=True)
        o = _mm(p, v[:, hl])
        acc = o if acc is None else acc + o
        den = d if den is None else den + d
    return acc / den


def _na_prompt_body(q_ref, k_ref, v_ref, qn_ref, kn_ref, o_ref, ko_ref, vo_ref):
    scale = HEAD_W ** -0.5
    qn = _seg_rms(q_ref[...].astype(f32), HEAD_W, qn_ref[...]) * scale
    kn = _seg_rms(k_ref[...].astype(f32), HEAD_W, kn_ref[...])
    v = v_ref[...].astype(f32)
    ko_ref[...] = kn
    vo_ref[...] = v
    for h in range(HEADS):
        hl = slice(h * HEAD_W, (h + 1) * HEAD_W)
        s = _mm_nt(qn[:, hl], kn[:, hl])
        o_ref[:, hl] = _softmax_pv([s], [v], hl)


def _na_prompt(u, nb, seq, qn_row, kn_row):
    cb = COL_NA // MIX_W
    blk = lambda j: pl.BlockSpec((seq, MIX_W), lambda b: (b, j))
    const = pl.BlockSpec((1, MIX_W), lambda b: (0, 0))
    sds = jax.ShapeDtypeStruct((nb * seq, MIX_W), f32)
    return pl.pallas_call(
        _na_prompt_body,
        grid=(nb,),
        in_specs=[blk(cb), blk(cb + 1), blk(cb + 2), const, const],
        out_specs=[blk(0), blk(0), blk(0)],
        out_shape=[sds, sds, sds],
        compiler_params=_params(("arbitrary",)),
        name="na_prompt",
    )(u, u, u, qn_row, kn_row)


def _na_sample_body(q_ref, k_ref, v_ref, kc_ref, vc_ref, bias_ref, qn_ref, kn_ref, o_ref, kn_scr, v_scr):
    i = pl.program_id(1)

    @pl.when(i == 0)
    def _():
        kn_scr[...] = _seg_rms(k_ref[...].astype(f32), HEAD_W, kn_ref[...]).astype(bf16)
        v_scr[...] = v_ref[...].astype(bf16)

    rows = DEC_SEQ // GRID_W
    rs = jnp.clip(i - NA_WIN_R // 2, 0, rows - NA_WIN_R)
    d0 = rs - i + (NA_WIN_R - 1)
    k0 = pl.multiple_of(rs * GRID_W, GRID_W)
    nwin = NA_WIN_R * GRID_W
    qn = _seg_rms(q_ref[...].astype(f32), HEAD_W, qn_ref[...]) * (HEAD_W ** -0.5)
    kwin = kn_scr[pl.ds(k0, nwin), :]
    vwin = v_scr[pl.ds(k0, nwin), :]
    kctx = kc_ref[...]
    vctx = vc_ref[...]
    for h in range(HEADS):
        hl = slice(h * HEAD_W, (h + 1) * HEAD_W)
        s_win = _mm_nt(qn[:, hl], kwin[:, hl]) + bias_ref[h, d0]
        s_ctx = _mm_nt(qn[:, hl], kctx[:, hl])
        o_ref[:, hl] = _softmax_pv([s_win, s_ctx], [vwin, vctx], hl)


def _na_sample(u, nb, layer, cache_k, cache_v, bias, qn_row, kn_row):
    seq = DEC_SEQ
    rows = seq // GRID_W
    cb = COL_NA // MIX_W
    full = lambda j: pl.BlockSpec((seq, MIX_W), lambda b, i: (b, j))
    ctx = pl.BlockSpec((None, None, PAST_LEN, MIX_W), lambda b, i: (b, layer, 0, 0))
    const = pl.BlockSpec((1, MIX_W), lambda b, i: (0, 0))
    return pl.pallas_call(
        _na_sample_body,
        grid=(nb, rows),
        in_specs=[pl.BlockSpec((GRID_W, MIX_W), lambda b, i: (b * rows + i, cb)), full(cb + 1), full(cb + 2),
                  ctx, ctx,
                  pl.BlockSpec((HEADS, NA_WIN_R, GRID_W, NA_WIN_R * GRID_W), lambda b, i: (0, 0, 0, 0)),
                  const, const],
        out_specs=pl.BlockSpec((GRID_W, MIX_W), lambda b, i: (b * rows + i, 0)),
        out_shape=jax.ShapeDtypeStruct((nb * seq, MIX_W), f32),
        scratch_shapes=[pltpu.VMEM((seq, MIX_W), bf16), pltpu.VMEM((seq, MIX_W), bf16)],
        compiler_params=_params(("arbitrary", "arbitrary")),
        name="na_sample",
    )(u, u, u, cache_k, cache_v, bias, qn_row, kn_row)


def _na_bias_table(rpb):
    j = jnp.arange(GRID_W)
    cs = jnp.clip(j - NA_WIN_C // 2, 0, GRID_W - NA_WIN_C)
    dc = j[None, :] - j[:, None] + (NA_WIN_C - 1)
    ok = (j[None, :] >= cs[:, None]) & (j[None, :] < cs[:, None] + NA_WIN_C)
    dr = jnp.arange(NA_WIN_R)[:, None] + jnp.arange(NA_WIN_R)[None, :]
    t = rpb[:, dr[:, None, :, None], jnp.clip(dc, 0, 2 * NA_WIN_C - 2)[None, :, None, :]]
    t = jnp.where(ok[None, None, :, None, :], t, NEG)
    return t.reshape(HEADS, NA_WIN_R, GRID_W, NA_WIN_R * GRID_W).astype(f32)


def _diff_lambda(lam_ref, lam_init):
    lv = lam_ref[...]
    s1 = jnp.sum(lv[0:1, :] * lv[1:2, :], axis=1, keepdims=True)
    s2 = jnp.sum(lv[2:3, :] * lv[3:4, :], axis=1, keepdims=True)
    return jnp.exp(s1) - jnp.exp(s2) + lam_init


def _diff_head(q, keys, vals, lam, h):
    hl = slice(h * HEAD_W, (h + 1) * HEAD_W)
    acc = None
    for m in range(2):
        ml = slice(h * HEAD_W + m * DIFF_QK_DIM, h * HEAD_W + (m + 1) * DIFF_QK_DIM)
        scores = [_mm_nt(q[:, ml], k[:, ml]) for k in keys]
        mx = scores[0].max(axis=1, keepdims=True)
        for s in scores[1:]:
            mx = jnp.maximum(mx, s.max(axis=1, keepdims=True))
        ps = [jnp.exp(s - mx) for s in scores]
        den = ps[0].sum(axis=1, keepdims=True)
        for p in ps[1:]:
            den = den + p.sum(axis=1, keepdims=True)
        coef = (1.0 / den) if m == 0 else (-lam / den)
        ps = [p * coef for p in ps]
        acc = ps if acc is None else [a + p for a, p in zip(acc, ps)]
    out = None
    for a, v in zip(acc, vals):
        o = _mm(a, v[:, hl])
        out = o if out is None else out + o
    return out


def _diff_prompt_body(lam_init, q_ref, k_ref, v_ref, qn_ref, kn_ref, lam_ref, nw_ref, o_ref, ko_ref, vo_ref):
    lam = _diff_lambda(lam_ref, lam_init)
    qn = _seg_rms(q_ref[...].astype(f32), DIFF_QK_DIM, qn_ref[...]) * (DIFF_QK_DIM ** -0.5)
    kn = _seg_rms(k_ref[...].astype(f32), DIFF_QK_DIM, kn_ref[...])
    v = v_ref[...].astype(f32)
    ko_ref[...] = kn
    vo_ref[...] = v
    o = jnp.concatenate([_diff_head(qn, [kn], [v], lam, h) for h in range(HEADS)], axis=1)
    o_ref[...] = _seg_rms(o, HEAD_W, nw_ref[...]) * (1.0 - lam_init)


def _diff_prompt(u, nb, seq, lam_init, qn_row, kn_row, lam_vecs, nw_row):
    cb = COL_DIFF // MIX_W
    blk = lambda j: pl.BlockSpec((seq, MIX_W), lambda b: (b, j))
    const = pl.BlockSpec((1, MIX_W), lambda b: (0, 0))
    sds = jax.ShapeDtypeStruct((nb * seq, MIX_W), f32)
    return pl.pallas_call(
        functools.partial(_diff_prompt_body, lam_init),
        grid=(nb,),
        in_specs=[blk(cb), blk(cb + 1), blk(cb + 2), const, const,
                  pl.BlockSpec((4, DIFF_QK_DIM), lambda b: (0, 0)), const],
        out_specs=[blk(0), blk(0), blk(0)],
        out_shape=[sds, sds, sds],
        compiler_params=_params(("arbitrary",)),
        name="diff_prompt",
    )(u, u, u, qn_row, kn_row, lam_vecs, nw_row)


def _rope(x, cos, sin):
    w = x.shape[-1]
    up = pltpu.roll(x, w - 8, 1)
    dn = pltpu.roll(x, 8, 1)
    first = (lax.broadcasted_iota(jnp.int32, x.shape, 1) % 16) < 8
    return x * cos + jnp.where(first, up, dn) * sin


def _diff_sample_body(lam_init, qblk, q_ref, k_ref, v_ref, kc_ref, vc_ref, cos_ref, sin_ref, qn_ref, kn_ref,
                      lam_ref, nw_ref, o_ref, kn_scr, v_scr):
    i = pl.program_id(1)

    @pl.when(i == 0)
    def _():
        kn = _seg_rms(k_ref[...].astype(f32), DIFF_QK_DIM, kn_ref[...])
        kn_scr[...] = _rope(kn, cos_ref[...], sin_ref[...]).astype(bf16)
        v_scr[...] = v_ref[...].astype(bf16)

    lam = _diff_lambda(lam_ref, lam_init)
    r0 = pl.multiple_of(i * qblk, qblk)
    qn = _seg_rms(q_ref[...].astype(f32), DIFF_QK_DIM, qn_ref[...])
    qn = _rope(qn, cos_ref[pl.ds(r0, qblk), :], sin_ref[pl.ds(r0, qblk), :]) * (DIFF_QK_DIM ** -0.5)
    keys = [kn_scr[...], kc_ref[...]]
    vals = [v_scr[...], vc_ref[...]]
    o = jnp.concatenate([_diff_head(qn, keys, vals, lam, h) for h in range(HEADS)], axis=1)
    o_ref[...] = _seg_rms(o, HEAD_W, nw_ref[...]) * (1.0 - lam_init)


def _diff_sample(u, nb, layer, lam_init, cache_k, cache_v, cos, sin, qn_row, kn_row, lam_vecs, nw_row):
    seq = DEC_SEQ
    qblk = 128
    nq = seq // qblk
    cb = COL_DIFF // MIX_W
    full = lambda j: pl.BlockSpec((seq, MIX_W), lambda b, i: (b, j))
    ctx = pl.BlockSpec((None, None, PAST_LEN, MIX_W), lambda b, i: (b, layer, 0, 0))
    const = pl.BlockSpec((1, MIX_W), lambda b, i: (0, 0))
    tab = pl.BlockSpec((seq, MIX_W), lambda b, i: (0, 0))
    return pl.pallas_call(
        functools.partial(_diff_sample_body, lam_init, qblk),
        grid=(nb, nq),
        in_specs=[pl.BlockSpec((qblk, MIX_W), lambda b, i: (b * nq + i, cb)), full(cb + 1), full(cb + 2),
                  ctx, ctx, tab, tab, const, const,
                  pl.BlockSpec((4, DIFF_QK_DIM), lambda b, i: (0, 0)), const],
        out_specs=pl.BlockSpec((qblk, MIX_W), lambda b, i: (b * nq + i, 0)),
        out_shape=jax.ShapeDtypeStruct((nb * seq, MIX_W), f32),
        scratch_shapes=[pltpu.VMEM((seq, MIX_W), bf16), pltpu.VMEM((seq, MIX_W), bf16)],
        compiler_params=_params(("arbitrary", "arbitrary")),
        name="diff_sample",
    )(u, u, u, cache_k, cache_v, cos, sin, qn_row, kn_row, lam_vecs, nw_row)


def _rope_tables():
    t = jnp.arange(DEC_SEQ)
    nf = DIFF_QK_DIM // 4
    inv = ROPE_BASE ** (-jnp.arange(nf, dtype=f32) / nf)
    ar = (t // GRID_W).astype(f32)[:, None] * inv
    ac = (t % GRID_W).astype(f32)[:, None] * inv
    cos = jnp.concatenate([jnp.cos(ar), jnp.cos(ar), jnp.cos(ac), jnp.cos(ac)], axis=1)
    sin = jnp.concatenate([-jnp.sin(ar), jnp.sin(ar), -jnp.sin(ac), jnp.sin(ac)], axis=1)
    reps = MIX_W // DIFF_QK_DIM
    return jnp.tile(cos, (1, reps)), jnp.tile(sin, (1, reps))


def _merge_body(x_ref, a_ref, b_ref, c_ref, d_ref, gp_ref, mod_ref, wb_ref, wo_ref, nw_ref, wr_ref,
                xo_ref, h2_ref, aff_ref):
    mixed = None
    for n, br in enumerate((a_ref, b_ref, c_ref, d_ref)):
        gate = jax.nn.sigmoid(gp_ref[:, n * D_MODEL:(n + 1) * D_MODEL].astype(f32))
        t = gate * jnp.dot(br[...].astype(bf16), wb_ref[n], preferred_element_type=f32)
        mixed = t if mixed is None else mixed + t
    y = jnp.dot(mixed.astype(bf16), wo_ref[...], preferred_element_type=f32)
    x = x_ref[...] + mod_ref[:, 2 * D_MODEL:3 * D_MODEL] * y
    xo_ref[...] = x
    h = x * lax.rsqrt(jnp.mean(x * x, axis=-1, keepdims=True) + EPS) * nw_ref[...]
    h = h * (1.0 + mod_ref[:, 4 * D_MODEL:5 * D_MODEL]) + mod_ref[:, 3 * D_MODEL:4 * D_MODEL]
    hb = h.astype(bf16)
    h2_ref[...] = hb
    logits = jnp.dot(hb, wr_ref[...], preferred_element_type=f32)
    lane = lax.broadcasted_iota(jnp.int32, logits.shape, 1)
    logits = jnp.where(lane < N_EXPERTS, logits, NEG)
    e = jnp.exp(logits - logits.max(axis=1, keepdims=True))
    aff_ref[...] = e / e.sum(axis=1, keepdims=True)


def _merge(x, brs, u, mod3, mod_base, mod_stride, seq, wb, wo, nw_row, wr):
    n = x.shape[0]
    tm = 256
    tps = seq // tm
    row = lambda w: pl.BlockSpec((tm, w), lambda i: (i, 0))
    return pl.pallas_call(
        _merge_body,
        grid=(n // tm,),
        in_specs=[row(D_MODEL), row(MIX_W), row(MIX_W), row(MIX_W), row(MIX_W),
                  pl.BlockSpec((tm, N_BRANCH * D_MODEL), lambda i: (i, COL_GPRE // (N_BRANCH * D_MODEL))),
                  pl.BlockSpec((None, 1, 6 * D_MODEL), lambda i: (mod_base + mod_stride * (i // tps), 0, 0)),
                  pl.BlockSpec((N_BRANCH, MIX_W, D_MODEL), lambda i: (0, 0, 0)),
                  pl.BlockSpec((D_MODEL, D_MODEL), lambda i: (0, 0)),
                  pl.BlockSpec((1, D_MODEL), lambda i: (0, 0)),
                  pl.BlockSpec((D_MODEL, LANE), lambda i: (0, 0))],
        out_specs=[row(D_MODEL), row(D_MODEL), row(LANE)],
        out_shape=[jax.ShapeDtypeStruct((n, D_MODEL), f32), jax.ShapeDtypeStruct((n, D_MODEL), bf16),
                   jax.ShapeDtypeStruct((n, LANE), f32)],
        compiler_params=_params(("arbitrary",)),
        name="merge",
    )(x, *brs, u, mod3, wb, wo, nw_row, wr)


def _route_body(t, cap, aff_ref, h2_ref, xg_ref, g_ref, rank_ref, rrow_scr):
    aff = aff_ref[...]
    afft = aff.T
    sub = lax.broadcasted_iota(jnp.int32, (t, t), 0)
    lan = lax.broadcasted_iota(jnp.int32, (t, t), 1)
    earlier = sub < lan
    rrow_scr[...] = jnp.zeros(rrow_scr.shape, f32) + float(cap)
    slot = lax.broadcasted_iota(jnp.int32, (cap, t), 0).astype(f32)
    h2 = h2_ref[...]
    for e in range(N_EXPERTS):
        a_col = aff[:, e:e + 1]
        a_row = afft[e:e + 1, :]
        beats = (a_col > a_row) | ((a_col == a_row) & earlier)
        rank = jnp.sum(jnp.where(beats, 1.0, 0.0), axis=0, keepdims=True)
        rank = jnp.minimum(rank, float(cap))
        rrow_scr[e:e + 1, :] = rank
        onehot = jnp.where(rank == slot, 1.0, 0.0)
        xg_ref[e] = jnp.dot(onehot.astype(bf16), h2, preferred_element_type=f32).astype(bf16)
        g_ref[e] = jnp.sum(onehot * a_row, axis=1, keepdims=True)
    rank_ref[...] = rrow_scr[...].T


def _route(aff, h2, nb, t):
    cap = CAPACITY_FACTOR * t // N_EXPERTS
    return pl.pallas_call(
        functools.partial(_route_body, t, cap),
        grid=(nb,),
        in_specs=[pl.BlockSpec((t, LANE), lambda b: (b, 0)), pl.BlockSpec((t, D_MODEL), lambda b: (b, 0))],
        out_specs=[pl.BlockSpec((N_EXPERTS, cap, D_MODEL), lambda b: (0, b, 0)),
                   pl.BlockSpec((N_EXPERTS, cap, 1), lambda b: (0, b, 0)),
                   pl.BlockSpec((t, LANE), lambda b: (b, 0))],
        out_shape=[jax.ShapeDtypeStruct((N_EXPERTS, nb * cap, D_MODEL), bf16),
                   jax.ShapeDtypeStruct((N_EXPERTS, nb * cap, 1), f32),
                   jax.ShapeDtypeStruct((nb * t, LANE), f32)],
        scratch_shapes=[pltpu.VMEM((LANE, t), f32)],
        compiler_params=_params(("arbitrary",)),
        name=f"route_{t}",
    )(aff, h2)


def _ffn_body(xp_ref, xs_ref, gp_ref, gs_ref, wg_ref, wu_ref, wd_ref, yp_ref, ys_ref, wg_scr, wu_scr, wd_scr):
    wg_scr[...] = wg_ref[...].astype(bf16)
    wu_scr[...] = wu_ref[...].astype(bf16)
    wd_scr[...] = wd_ref[...].astype(bf16)
    blk = 256
    for x_ref, g_ref, y_ref in ((xp_ref, gp_ref, yp_ref), (xs_ref, gs_ref, ys_ref)):
        def step(i, carry, x_ref=x_ref, g_ref=g_ref, y_ref=y_ref):
            r0 = pl.multiple_of(i * blk, blk)
            x = x_ref[pl.ds(r0, blk), :]
            a = jnp.dot(x, wg_scr[...], preferred_element_type=f32)
            up = jnp.dot(x, wu_scr[...], preferred_element_type=f32)
            mid = (jax.nn.silu(a) * up).astype(bf16)
            y = jnp.dot(mid, wd_scr[...], preferred_element_type=f32)
            y_ref[pl.ds(r0, blk), :] = (y * g_ref[pl.ds(r0, blk), :]).astype(bf16)
            return carry
        lax.fori_loop(0, x_ref.shape[0] // blk, step, 0)


def _ffn(xp, xs, gp, gs, w_gate, w_up, w_down):
    np_, ns = xp.shape[1], xs.shape[1]
    tok = lambda n, w: pl.BlockSpec((None, n, w), lambda e: (e, 0, 0))
    wspec = lambda a, b: pl.BlockSpec((None, a, b), lambda e: (e, 0, 0))
    return pl.pallas_call(
        _ffn_body,
        grid=(N_EXPERTS,),
        in_specs=[tok(np_, D_MODEL), tok(ns, D_MODEL), tok(np_, 1), tok(ns, 1),
                  wspec(D_MODEL, EXPERT_FF), wspec(D_MODEL, EXPERT_FF), wspec(EXPERT_FF, D_MODEL)],
        out_specs=[tok(np_, D_MODEL), tok(ns, D_MODEL)],
        out_shape=[jax.ShapeDtypeStruct(xp.shape, bf16), jax.ShapeDtypeStruct(xs.shape, bf16)],
        scratch_shapes=[pltpu.VMEM((D_MODEL, EXPERT_FF), bf16), pltpu.VMEM((D_MODEL, EXPERT_FF), bf16),
                        pltpu.VMEM((EXPERT_FF, D_MODEL), bf16)],
        compiler_params=_params(("arbitrary",)),
        name="expert_ffn",
    )(xp, xs, gp, gs, w_gate, w_up, w_down)


def _combine_body(t, cap, x_ref, rank_ref, y_ref, mod_ref, o_ref):
    k = N_EXPERTS * cap
    er = lax.broadcasted_iota(jnp.int32, (LANE, k), 0)
    ec = lax.broadcasted_iota(jnp.int32, (LANE, k), 1) // cap
    expand = jnp.where(er == ec, 1.0, 0.0).astype(bf16)
    rexp = jnp.dot(rank_ref[...].astype(bf16), expand, preferred_element_type=f32)
    slot = (lax.broadcasted_iota(jnp.int32, (t, k), 1) % cap).astype(f32)
    pt = jnp.where(rexp == slot, 1.0, 0.0).astype(bf16)
    moe = jnp.dot(pt, y_ref[...].reshape(k, D_MODEL), preferred_element_type=f32)
    o_ref[...] = x_ref[...] + mod_ref[:, 5 * D_MODEL:6 * D_MODEL] * moe


def _combine(x, rank, y, mod3, mod_base, mod_stride, nb, t):
    cap = CAPACITY_FACTOR * t // N_EXPERTS
    return pl.pallas_call(
        functools.partial(_combine_body, t, cap),
        grid=(nb,),
        in_specs=[pl.BlockSpec((t, D_MODEL), lambda b: (b, 0)),
                  pl.BlockSpec((t, LANE), lambda b: (b, 0)),
                  pl.BlockSpec((N_EXPERTS, cap, D_MODEL), lambda b: (0, b, 0)),
                  pl.BlockSpec((None, 1, 6 * D_MODEL), lambda b: (mod_base + mod_stride * b, 0, 0))],
        out_specs=pl.BlockSpec((t, D_MODEL), lambda b: (b, 0)),
        out_shape=jax.ShapeDtypeStruct(x.shape, f32),
        compiler_params=_params(("arbitrary",)),
        name=f"combine_{t}",
    )(x, rank, y, mod3)


def _pad_row(v, off, width=LANE):
    return jnp.zeros((1, width), f32).at[0, off:off + v.shape[0]].set(v.astype(f32))


def _layer_params(l, a):
    w_in = a["w_in"][l]
    pts, acc = [], 0
    splits = (MIX_W,) * 4 + (4 * HEADS,) + (MIX_W,) * 7 + (SSM_CONV_CH, 2 * HEADS, N_BRANCH * D_MODEL)
    for s in splits:
        pts.append((acc, acc + s))
        acc += s
    seg = lambda i: w_in[:, pts[i][0]:pts[i][1]]
    order = [14, 0, 1, 2, 3, 5, 6, 7, 8, 9, 10, 12, 11]
    w_main = jnp.concatenate([seg(i) for i in order], axis=1).astype(bf16)
    w_side = jnp.concatenate([seg(4), seg(13), jnp.zeros((D_MODEL, LANE - 24), f32)], axis=1).astype(bf16)
    tile = lambda v, reps: jnp.tile(v.astype(f32), reps).reshape(1, -1)
    p = dict(
        w_main=w_main, w_side=w_side,
        norm1=a["norm1_w"][l].reshape(1, -1), norm2=a["norm2_w"][l].reshape(1, -1),
        mlstm_gb=_pad_row(a["mlstm_gate_b"][l].reshape(-1), G_MLSTM),
        mlstm_nw=a["mlstm_norm_w"][l].reshape(1, -1),
        na_qn=tile(a["na_qnorm_w"][l], HEADS), na_kn=tile(a["na_knorm_w"][l], HEADS),
        na_bias=_na_bias_table(a["na_rpb"][l]),
        diff_qn=tile(a["diff_qnorm_w"][l], 2 * HEADS), diff_kn=tile(a["diff_knorm_w"][l], 2 * HEADS),
        diff_lam=a["diff_lambda"][l], diff_nw=tile(a["diff_norm_w"][l], HEADS),
        ssm_cw=a["ssm_conv_w"][l], ssm_cb=a["ssm_conv_b"][l].reshape(1, -1),
        ssm_dtb=_pad_row(a["ssm_dt_bias"][l].reshape(-1), G_DT),
        ssm_alog=_pad_row(a["ssm_a_log"][l].reshape(-1), G_DT),
        ssm_d=jnp.repeat(a["ssm_d"][l].astype(f32), HEAD_W).reshape(1, -1),
        ssm_nw=a["ssm_norm_w"][l].reshape(1, -1),
        wb=a["w_branch"][l].astype(bf16), wo=a["w_out"][l].astype(bf16),
        wr=jnp.concatenate([a["w_router"][l], jnp.zeros((D_MODEL, LANE - N_EXPERTS), f32)], axis=1).astype(bf16),
        w_gate=a["w_gate"][l], w_up=a["w_up"][l], w_down=a["w_down"][l],
    )
    return p


def _mixers(u, g, nb, seq, p, l, lam_init, ctx, tables):
    if ctx is None:
        a_out, m_c, m_n, m_m = _mlstm(u, g, nb, seq, p["mlstm_gb"], p["mlstm_nw"], None)
        b_out, na_k, na_v = _na_prompt(u, nb, seq, p["na_qn"], p["na_kn"])
        c_out, df_k, df_v = _diff_prompt(u, nb, seq, lam_init, p["diff_qn"], p["diff_kn"], p["diff_lam"], p["diff_nw"])
        d_out, s_h = _ssd(u, g, nb, seq, p["ssm_cw"], p["ssm_cb"], p["ssm_dtb"], p["ssm_alog"], p["ssm_d"],
                          p["ssm_nw"], None)
        return (a_out, b_out, c_out, d_out), (na_k, na_v, df_k, df_v, m_c, m_n, m_m, s_h)
    (na_ck, na_cv, df_ck, df_cv, st_c, st_n, st_m, st_s) = ctx
    cos, sin = tables
    a_out = _mlstm(u, g, nb, seq, p["mlstm_gb"], p["mlstm_nw"], (st_c[:, l], st_n[:, l], st_m[:, l]))[0]
    b_out = _na_sample(u, nb, l, na_ck, na_cv, p["na_bias"], p["na_qn"], p["na_kn"])
    c_out = _diff_sample(u, nb, l, lam_init, df_ck, df_cv, cos, sin, p["diff_qn"], p["diff_kn"], p["diff_lam"],
                         p["diff_nw"])
    d_out = _ssd(u, g, nb, seq, p["ssm_cw"], p["ssm_cb"], p["ssm_dtb"], p["ssm_alog"], p["ssm_d"], p["ssm_nw"],
                 st_s[:, l])[0]
    return (a_out, b_out, c_out, d_out), None


def kernel(x_prompt, x_sample, cache_na_k, cache_na_v, cache_diff_k, cache_diff_v, state_mlstm_C, state_mlstm_n, state_mlstm_m, state_ssm, c, c_ctx, norm1_w, norm2_w, w_ada, b_ada, w_in, mlstm_gate_b, mlstm_norm_w, na_qnorm_w, na_knorm_w, na_rpb, diff_qnorm_w, diff_knorm_w, diff_lambda, diff_norm_w, ssm_conv_w, ssm_conv_b, ssm_dt_bias, ssm_a_log, ssm_d, ssm_norm_w, w_branch, w_out, w_router, w_gate, w_up, w_down):
    a = dict(norm1_w=norm1_w, norm2_w=norm2_w, w_in=w_in, mlstm_gate_b=mlstm_gate_b, mlstm_norm_w=mlstm_norm_w,
             na_qnorm_w=na_qnorm_w, na_knorm_w=na_knorm_w, na_rpb=na_rpb, diff_qnorm_w=diff_qnorm_w,
             diff_knorm_w=diff_knorm_w, diff_lambda=diff_lambda, diff_norm_w=diff_norm_w, ssm_conv_w=ssm_conv_w,
             ssm_conv_b=ssm_conv_b, ssm_dt_bias=ssm_dt_bias, ssm_a_log=ssm_a_log, ssm_d=ssm_d,
             ssm_norm_w=ssm_norm_w, w_branch=w_branch, w_out=w_out, w_router=w_router, w_gate=w_gate, w_up=w_up,
             w_down=w_down)
    nb_p, nb_s = x_prompt.shape[0], x_sample.shape[0]
    xp = x_prompt.reshape(nb_p * SEQ, D_MODEL)
    xs = x_sample.reshape(nb_s * DEC_SEQ, D_MODEL)
    cc = jnp.concatenate([c_ctx[None, :], c, jnp.zeros((8 - 1 - nb_s, D_MODEL), f32)], axis=0)
    ctx = (cache_na_k.reshape(nb_s, DEPTH, PAST_LEN, MIX_W), cache_na_v.reshape(nb_s, DEPTH, PAST_LEN, MIX_W),
           cache_diff_k.reshape(nb_s, DEPTH, PAST_LEN, MIX_W), cache_diff_v.reshape(nb_s, DEPTH, PAST_LEN, MIX_W),
           state_mlstm_C, state_mlstm_n, state_mlstm_m, state_ssm)
    tables = _rope_tables()
    new = []
    for l in range(DEPTH):
        p = _layer_params(l, a)
        lam_init = 0.8 - 0.6 * math.exp(-0.3 * l)
        mod3 = _modulation(cc, w_ada[l], b_ada[l]).reshape(8, 1, 6 * D_MODEL)
        groups = []
        for (x, nb, seq, base, stride, gctx) in ((xp, nb_p, SEQ, 0, 0, None), (xs, nb_s, DEC_SEQ, 1, 1, ctx)):
            u, g = _inproj(x, mod3, base, stride, seq, p["norm1"], p["w_main"], p["w_side"])
            brs, st = _mixers(u, g, nb, seq, p, l, lam_init, gctx, tables)
            if st is not None:
                new.append(st)
            x1, h2, aff = _merge(x, brs, u, mod3, base, stride, seq, p["wb"], p["wo"], p["norm2"], p["wr"])
            xg, gg, rank = _route(aff, h2, nb, seq)
            groups.append((x1, xg, gg, rank, nb, seq, base, stride))
        yp, ys = _ffn(groups[0][1], groups[1][1], groups[0][2], groups[1][2], p["w_gate"], p["w_up"], p["w_down"])
        outs = []
        for (x1, _, _, rank, nb, seq, base, stride), y in zip(groups, (yp, ys)):
            outs.append(_combine(x1, rank, y, mod3, base, stride, nb, seq))
        xp, xs = outs
    stack = lambda i, shape: jnp.stack([new[l][i].reshape(shape) for l in range(DEPTH)], axis=1)
    kv = (nb_p, SEQ, HEADS, HEAD_W)
    return (xp.reshape(x_prompt.shape), xs.reshape(x_sample.shape),
            stack(0, kv), stack(1, kv), stack(2, kv), stack(3, kv),
            stack(4, (nb_p, 2, HEADS, HEAD_W, HEAD_W)), stack(5, (nb_p, 2, HEADS, HEAD_W)),
            stack(6, (nb_p, 2, HEADS)), stack(7, (nb_p, 2, HEADS, HEAD_W, SSM_STATE)))
```

```python
import functools
import math

import jax
import jax.numpy as jnp
from jax import lax
from jax.experimental import pallas as pl
from jax.experimental.pallas import tpu as pltpu

f32 = jnp.float32
bf16 = jnp.bfloat16

D_MODEL = 1024
BATCH = 32
SEQ = 256
DEPTH = 2
DEC_BATCH = 4
DEC_SEQ = 1024
PAST_LEN = 512
GRID_W = 64
MIX_W = D_MODEL // 4
N_BRANCH = 4
HEADS = 4
HEAD_W = MIX_W // HEADS
NA_WIN_R = 8
NA_WIN_C = 16
DIFF_QK_DIM = HEAD_W // 2
SSM_GROUPS = 2
SSM_STATE = 64
SSM_CONV_K = 5
SSM_CONV_CH = MIX_W + 2 * SSM_GROUPS * SSM_STATE
CHUNK = 64
N_EXPERTS = 16
EXPERT_FF = 1024
CAPACITY_FACTOR = 2
ROPE_BASE = 10000.0
EPS = 1e-6
NEG = -1e30

LANE = 128
VMEM_LIMIT = 56 * 1024 * 1024

COL_GPRE = 0
COL_MLSTM = 4096
COL_NA = 5120
COL_DIFF = 5888
COL_XBC = 6656
COL_Z = 7168
U_USED = 7424
U_COLS = 7680
G_MLSTM = 0
G_DT = 16


def _params(sem):
    return pltpu.CompilerParams(dimension_semantics=sem, vmem_limit_bytes=VMEM_LIMIT)


def _mm(a, b):
    return jnp.dot(a.astype(bf16), b.astype(bf16), preferred_element_type=f32)


def _mm_nt(a, b):
    return lax.dot_general(a.astype(bf16), b.astype(bf16), (((1,), (1,)), ((), ())), preferred_element_type=f32)


def _mm_tn(a, b):
    return lax.dot_general(a.astype(bf16), b.astype(bf16), (((0,), (0,)), ((), ())), preferred_element_type=f32)


def _seg_sum(x, seg):
    w = x.shape[-1]
    r = lax.broadcasted_iota(jnp.int32, (w, w), 0) // seg
    c = lax.broadcasted_iota(jnp.int32, (w, w), 1) // seg
    ones = jnp.where(r == c, 1.0, 0.0).astype(bf16)
    hi = x.astype(bf16)
    lo = (x - hi.astype(f32)).astype(bf16)
    return jnp.dot(hi, ones, preferred_element_type=f32) + jnp.dot(lo, ones, preferred_element_type=f32)


def _seg_rms(x, seg, w_row):
    ms = _seg_sum(x * x, seg) * (1.0 / seg)
    return x * lax.rsqrt(ms + EPS) * w_row


def _tri_masks():
    r = lax.broadcasted_iota(jnp.int32, (CHUNK, CHUNK), 0)
    c = lax.broadcasted_iota(jnp.int32, (CHUNK, CHUNK), 1)
    return r >= c, r <= c


def _mod_body(c_ref, w_ref, b_ref, o_ref):
    o_ref[...] = _mm(jax.nn.silu(c_ref[...]), w_ref[...]) + b_ref[...]


def _modulation(cc, w_ada, b_ada, layer):
    tn = 512
    n = w_ada.shape[2]
    return pl.pallas_call(
        _mod_body,
        grid=(n // tn,),
        in_specs=[pl.BlockSpec((8, D_MODEL), lambda j: (0, 0)),
                  pl.BlockSpec((None, D_MODEL, tn), lambda j: (layer, 0, j)),
                  pl.BlockSpec((None, 1, tn), lambda j: (layer, 0, j))],
        out_specs=pl.BlockSpec((8, tn), lambda j: (0, j)),
        out_shape=jax.ShapeDtypeStruct((8, n), f32),
        compiler_params=_params(("arbitrary",)),
        name="modulation",
    )(cc, w_ada, b_ada.reshape(b_ada.shape[0], 1, n))


def _inproj_body(x_ref, mod_ref, nw_ref, w_ref, wg_ref, u_ref, g_ref, h_scr):
    @pl.when(pl.program_id(1) == 0)
    def _():
        x = x_ref[...]
        y = x * lax.rsqrt(jnp.mean(x * x, axis=-1, keepdims=True) + EPS) * nw_ref[...]
        h = y * (1.0 + mod_ref[:, D_MODEL:2 * D_MODEL]) + mod_ref[:, 0:D_MODEL]
        hb = h.astype(bf16)
        h_scr[...] = hb
        g_ref[...] = jnp.dot(hb, wg_ref[...], preferred_element_type=f32)

    u_ref[...] = jnp.dot(h_scr[...], w_ref[...], preferred_element_type=f32).astype(u_ref.dtype)


def _inproj(x, mod3, mod_base, mod_stride, seq, nw_row, w_main, w_side):
    n = x.shape[0]
    tm, tn = 1024, U_COLS // 4
    tps = seq // tm if seq >= tm else 1
    return pl.pallas_call(
        _inproj_body,
        grid=(n // tm, U_COLS // tn),
        in_specs=[pl.BlockSpec((tm, D_MODEL), lambda i, j: (i, 0)),
                  pl.BlockSpec((None, 1, 6 * D_MODEL), lambda i, j: (mod_base + mod_stride * (i // tps), 0, 0)),
                  pl.BlockSpec((1, D_MODEL), lambda i, j: (0, 0)),
                  pl.BlockSpec((D_MODEL, tn), lambda i, j: (0, j)),
                  pl.BlockSpec((D_MODEL, LANE), lambda i, j: (0, 0))],
        out_specs=[pl.BlockSpec((tm, tn), lambda i, j: (i, j)),
                   pl.BlockSpec((tm, LANE), lambda i, j: (i, 0))],
        out_shape=[jax.ShapeDtypeStruct((n, U_COLS), f32), jax.ShapeDtypeStruct((n, LANE), f32)],
        scratch_shapes=[pltpu.VMEM((tm, D_MODEL), bf16)],
        compiler_params=_params(("arbitrary", "arbitrary")),
        name="inproj",
    )(x, mod3, nw_row, w_main, w_side)


def _mlstm_body(seq, has_state, *refs):
    if has_state:
        (qkvo_ref, g_ref, gb_ref, nw_ref, c0_ref, n0_ref, m0_ref,
         out_ref, cs_ref, ns_ref, ms_ref, gcol_scr, lcol_scr, rows_scr, hf_scr, hb_scr) = refs
    else:
        (qkvo_ref, g_ref, gb_ref, nw_ref,
         out_ref, cs_ref, ns_ref, ms_ref, gcol_scr, lcol_scr, rows_scr, hf_scr, hb_scr) = refs
    nc = seq // CHUNK
    g = g_ref[...] + gb_ref[...]
    ls = jax.nn.log_sigmoid(g)
    gcol_scr[...] = g
    lcol_scr[...] = ls
    gt = g.T
    lst = ls.T
    rsel = (lax.broadcasted_iota(jnp.int32, (16, CHUNK), 0) % 8) < HEADS
    for c in range(nc):
        sl = slice(c * CHUNK, (c + 1) * CHUNK)
        rows_scr[c] = jnp.where(rsel, gt[0:16, sl], lst[0:16, sl])
    if has_state:
        cs_ref[...] = c0_ref[...]
        ns_ref[...] = n0_ref[...]
        ms_ref[...] = m0_ref[...]
    else:
        cs_ref[...] = jnp.zeros(cs_ref.shape, f32)
        ns_ref[...] = jnp.zeros(ns_ref.shape, f32)
        ms_ref[...] = jnp.zeros(ms_ref.shape, f32)
    tril, triu = _tri_masks()

    def step(c, carry):
        for d in range(2):
            cidx = c if d == 0 else nc - 1 - c
            r0 = pl.multiple_of(cidx * CHUNK, CHUNK)
            rows = rows_scr[cidx]
            gc = gcol_scr[pl.ds(r0, CHUNK), :]
            lc = lcol_scr[pl.ds(r0, CHUNK), :]
            valid, other = (tril, triu) if d == 0 else (triu, tril)
            h_scr = hf_scr if d == 0 else hb_scr
            last = CHUNK - 1 if d == 0 else 0
            for h in range(HEADS):
                lanes = slice(h * HEAD_W, (h + 1) * HEAD_W)
                q = qkvo_ref[pl.ds(r0, CHUNK), h * HEAD_W:(h + 1) * HEAD_W]
                k = qkvo_ref[pl.ds(r0, CHUNK), MIX_W + h * HEAD_W:MIX_W + (h + 1) * HEAD_W] * (HEAD_W ** -0.5)
                v = qkvo_ref[pl.ds(r0, CHUNK), 2 * MIX_W + h * HEAD_W:2 * MIX_W + (h + 1) * HEAD_W]
                gi = G_MLSTM + d * 8 + h
                icol = gc[:, gi:gi + 1]
                fcol = lc[:, gi + 4:gi + 5]
                irow = rows[d * 8 + h:d * 8 + h + 1, :]
                frow = rows[d * 8 + 4 + h:d * 8 + 5 + h, :]
                bcol = jnp.sum(jnp.where(valid, frow, 0.0), axis=1, keepdims=True)
                brow = jnp.sum(jnp.where(other, fcol, 0.0), axis=0, keepdims=True)
                m_old = ms_ref[d:d + 1, h:h + 1]
                c_old = cs_ref[d, h]
                n_old = ns_ref[d, h:h + 1, :]
                dmat = jnp.where(valid, bcol - brow + irow, NEG)
                gg = bcol + m_old
                mt = jnp.maximum(gg, jnp.max(dmat, axis=1, keepdims=True))
                smat = _mm_nt(q, k) * jnp.exp(dmat - mt)
                w_prev = jnp.exp(gg - mt)
                num = _mm(smat, v) + _mm(q, c_old) * w_prev
                den = jnp.sum(smat, axis=1, keepdims=True) + jnp.sum(q * n_old, axis=1, keepdims=True) * w_prev
                hout = num / jnp.maximum(jnp.abs(den), jnp.exp(-mt))
                h_scr[pl.ds(r0, CHUNK), lanes] = hout
                bl = bcol[last:last + 1, :]
                lw_row = bl - brow + irow
                lw_col = bl - bcol + icol
                m_new = jnp.maximum(bl + m_old, jnp.max(lw_row, axis=1, keepdims=True))
                dec = jnp.exp(bl + m_old - m_new)
                kw = k * jnp.exp(lw_col - m_new)
                cs_ref[d, h] = dec * c_old + _mm_tn(kw, v)
                ns_ref[d, h:h + 1, :] = dec * n_old + jnp.sum(kw, axis=0, keepdims=True)
                ms_ref[d:d + 1, h:h + 1] = m_new
        return carry

    lax.fori_loop(0, nc, step, 0)

    blk = 256

    def fin(i, carry):
        r0 = pl.multiple_of(i * blk, blk)
        hs = hf_scr[pl.ds(r0, blk), :] + hb_scr[pl.ds(r0, blk), :]
        hn = _seg_rms(hs, HEAD_W, nw_ref[...])
        o = qkvo_ref[pl.ds(r0, blk), 3 * MIX_W:4 * MIX_W]
        out_ref[pl.ds(r0, blk), :] = hn * jax.nn.sigmoid(o)
        return carry

    lax.fori_loop(0, seq // blk, fin, 0)


def _mlstm(u, g, nb, seq, gb_row, nw_row, state):
    has_state = state is not None
    nc = seq // CHUNK
    in_specs = [pl.BlockSpec((seq, 4 * MIX_W), lambda b: (b, COL_MLSTM // (4 * MIX_W))),
                pl.BlockSpec((seq, LANE), lambda b: (b, 0)),
                pl.BlockSpec((1, LANE), lambda b: (0, 0)),
                pl.BlockSpec((1, MIX_W), lambda b: (0, 0))]
    args = [u, g, gb_row, nw_row]
    if has_state:
        in_specs += [pl.BlockSpec((None, 2, HEADS, HEAD_W, HEAD_W), lambda b: (b, 0, 0, 0, 0)),
                     pl.BlockSpec((None, 2, HEADS, HEAD_W), lambda b: (b, 0, 0, 0)),
                     pl.BlockSpec((None, 2, HEADS), lambda b: (b, 0, 0))]
        args += list(state)
    return pl.pallas_call(
        functools.partial(_mlstm_body, seq, has_state),
        grid=(nb,),
        in_specs=in_specs,
        out_specs=[pl.BlockSpec((seq, MIX_W), lambda b: (b, 0)),
                   pl.BlockSpec((None, 2, HEADS, HEAD_W, HEAD_W), lambda b: (b, 0, 0, 0, 0)),
                   pl.BlockSpec((None, 2, HEADS, HEAD_W), lambda b: (b, 0, 0, 0)),
                   pl.BlockSpec((None, 2, HEADS), lambda b: (b, 0, 0))],
        out_shape=[jax.ShapeDtypeStruct((nb * seq, MIX_W), f32),
                   jax.ShapeDtypeStruct((nb, 2, HEADS, HEAD_W, HEAD_W), f32),
                   jax.ShapeDtypeStruct((nb, 2, HEADS, HEAD_W), f32),
                   jax.ShapeDtypeStruct((nb, 2, HEADS), f32)],
        scratch_shapes=[pltpu.VMEM((seq, LANE), f32), pltpu.VMEM((seq, LANE), f32),
                        pltpu.VMEM((nc, 16, CHUNK), f32),
                        pltpu.VMEM((seq, MIX_W), f32), pltpu.VMEM((seq, MIX_W), f32)],
        compiler_params=_params(("arbitrary",)),
        name="mlstm_state" if has_state else "mlstm",
    )(*args)


def _ssd_body(seq, has_state, *refs):
    if has_state:
        (z_ref, xbc_ref, g_ref, cw_ref, cb_ref, dtb_ref, alog_ref, dskip_ref, nw_ref, h0_ref,
         out_ref, hs_ref, xpad_scr, xc_scr, dt_scr, a_scr, rows_scr, yf_scr, yb_scr) = refs
    else:
        (z_ref, xbc_ref, g_ref, cw_ref, cb_ref, dtb_ref, alog_ref, dskip_ref, nw_ref,
         out_ref, hs_ref, xpad_scr, xc_scr, dt_scr, a_scr, rows_scr, yf_scr, yb_scr) = refs
    nc = seq // CHUNK
    pad = 8
    blk = 256
    nblk = seq // blk
    xpad_scr[0:pad, :] = jnp.zeros((pad, SSM_CONV_CH), f32)
    xpad_scr[pad + seq:2 * pad + seq, :] = jnp.zeros((pad, SSM_CONV_CH), f32)
    xpad_scr[pad:pad + seq, :] = xbc_ref[...].astype(f32)

    cblk = 128
    for i in range(seq // cblk):
        r0 = i * cblk
        acc = jnp.zeros((cblk, SSM_CONV_CH), f32) + cb_ref[...]
        for kk in range(SSM_CONV_K):
            off = r0 + pad - SSM_CONV_K // 2 + kk
            acc = acc + xpad_scr[off:off + cblk, :] * cw_ref[kk:kk + 1, :]
        xc_scr[r0:r0 + cblk, :] = jax.nn.silu(acc)

    dt = jax.nn.softplus(g_ref[...] + dtb_ref[...])
    a = dt * (-jnp.exp(alog_ref[...]))
    dt_scr[...] = dt
    a_scr[...] = a
    at = a.T
    for c in range(nc):
        rows_scr[c] = at[G_DT:G_DT + 8, c * CHUNK:(c + 1) * CHUNK]
    if has_state:
        hs_ref[...] = h0_ref[...]
    else:
        hs_ref[...] = jnp.zeros(hs_ref.shape, f32)
    tril, triu = _tri_masks()
    rep = HEADS // SSM_GROUPS

    def step(c, carry):
        for d in range(2):
            cidx = c if d == 0 else nc - 1 - c
            r0 = pl.multiple_of(cidx * CHUNK, CHUNK)
            rows = rows_scr[cidx]
            ac = a_scr[pl.ds(r0, CHUNK), :]
            dc = dt_scr[pl.ds(r0, CHUNK), :]
            valid, other = (tril, triu) if d == 0 else (triu, tril)
            y_scr = yf_scr if d == 0 else yb_scr
            last = CHUNK - 1 if d == 0 else 0
            for grp in range(SSM_GROUPS):
                bm = xc_scr[pl.ds(r0, CHUNK), MIX_W + grp * SSM_STATE:MIX_W + (grp + 1) * SSM_STATE]
                cm = xc_scr[pl.ds(r0, CHUNK), MIX_W + (SSM_GROUPS + grp) * SSM_STATE:MIX_W + (SSM_GROUPS + grp + 1) * SSM_STATE]
                cb = _mm_nt(cm, bm)
                for h in range(grp * rep, (grp + 1) * rep):
                    lanes = slice(h * HEAD_W, (h + 1) * HEAD_W)
                    gi = G_DT + d * HEADS + h
                    acol = ac[:, gi:gi + 1]
                    dtcol = dc[:, gi:gi + 1]
                    arow = rows[d * HEADS + h:d * HEADS + h + 1, :]
                    cum_col = jnp.sum(jnp.where(valid, arow, 0.0), axis=1, keepdims=True)
                    cum_row = jnp.sum(jnp.where(other, acol, 0.0), axis=0, keepdims=True)
                    decay = jnp.exp(jnp.where(valid, cum_col - cum_row, NEG))
                    xdt = xc_scr[pl.ds(r0, CHUNK), h * HEAD_W:(h + 1) * HEAD_W] * dtcol
                    h_old = hs_ref[d, h]
                    y = _mm(cb * decay, xdt) + _mm_nt(cm, h_old) * jnp.exp(cum_col)
                    y_scr[pl.ds(r0, CHUNK), lanes] = y
                    al = cum_col[last:last + 1, :]
                    hs_ref[d, h] = jnp.exp(al) * h_old + _mm_tn(xdt * jnp.exp(al - cum_col), bm)
        return carry

    lax.fori_loop(0, nc, step, 0)

    def fin(i, carry):
        r0 = pl.multiple_of(i * blk, blk)
        y = yf_scr[pl.ds(r0, blk), :] + yb_scr[pl.ds(r0, blk), :] + dskip_ref[...] * xc_scr[pl.ds(r0, blk), 0:MIX_W]
        y = y * jax.nn.silu(z_ref[pl.ds(r0, blk), :].astype(f32))
        out_ref[pl.ds(r0, blk), :] = y * lax.rsqrt(jnp.mean(y * y, axis=-1, keepdims=True) + EPS) * nw_ref[...]
        return carry

    lax.fori_loop(0, nblk, fin, 0)


def _ssd(u, g, nb, seq, cw, cb_row, dtb_row, alog_row, dskip_row, nw_row, state):
    has_state = state is not None
    nc = seq // CHUNK
    const = lambda shape: pl.BlockSpec(shape, lambda b: (0,) * len(shape))
    in_specs = [pl.BlockSpec((seq, MIX_W), lambda b: (b, COL_Z // MIX_W)),
                pl.BlockSpec((seq, SSM_CONV_CH), lambda b: (b, COL_XBC // SSM_CONV_CH)),
                pl.BlockSpec((seq, LANE), lambda b: (b, 0)),
                const((SSM_CONV_K, SSM_CONV_CH)), const((1, SSM_CONV_CH)), const((1, LANE)), const((1, LANE)),
                const((1, MIX_W)), const((1, MIX_W))]
    args = [u, u, g, cw, cb_row, dtb_row, alog_row, dskip_row, nw_row]
    if has_state:
        in_specs.append(pl.BlockSpec((None, 2, HEADS, HEAD_W, SSM_STATE), lambda b: (b, 0, 0, 0, 0)))
        args.append(state)
    return pl.pallas_call(
        functools.partial(_ssd_body, seq, has_state),
        grid=(nb,),
        in_specs=in_specs,
        out_specs=[pl.BlockSpec((seq, MIX_W), lambda b: (b, 0)),
                   pl.BlockSpec((None, 2, HEADS, HEAD_W, SSM_STATE), lambda b: (b, 0, 0, 0, 0))],
        out_shape=[jax.ShapeDtypeStruct((nb * seq, MIX_W), f32),
                   jax.ShapeDtypeStruct((nb, 2, HEADS, HEAD_W, SSM_STATE), f32)],
        scratch_shapes=[pltpu.VMEM((seq + 16, SSM_CONV_CH), f32), pltpu.VMEM((seq, SSM_CONV_CH), f32),
                        pltpu.VMEM((seq, LANE), f32), pltpu.VMEM((seq, LANE), f32),
                        pltpu.VMEM((nc, 8, CHUNK), f32),
                        pltpu.VMEM((seq, MIX_W), f32), pltpu.VMEM((seq, MIX_W), f32)],
        compiler_params=_params(("arbitrary",)),
        name="ssd_state" if has_state else "ssd",
    )(*args)


def _softmax_pv(scores, values, hl):
    m = scores[0].max(axis=1, keepdims=True)
    for s in scores[1:]:
        m = jnp.maximum(m, s.max(axis=1, keepdims=True))
    acc, den = None, None
    for s, v in zip(scores, values):
        p = jnp.exp(s - m)
        d = jnp.sum(p, axis=1, keepdims=True)
        o = _mm(p, v[:, hl])
        acc = o if acc is None else acc + o
        den = d if den is None else den + d
    return acc / den


def _na_prompt_body(q_ref, k_ref, v_ref, qn_ref, kn_ref, o_ref, ko_ref, vo_ref):
    scale = HEAD_W ** -0.5
    qn = _seg_rms(q_ref[...].astype(f32), HEAD_W, qn_ref[...]) * scale
    kn = _seg_rms(k_ref[...].astype(f32), HEAD_W, kn_ref[...])
    v = v_ref[...].astype(f32)
    ko_ref[...] = kn
    vo_ref[...] = v
    for h in range(HEADS):
        hl = slice(h * HEAD_W, (h + 1) * HEAD_W)
        s = _mm_nt(qn[:, hl], kn[:, hl])
        o_ref[:, hl] = _softmax_pv([s], [v], hl)


def _na_prompt(u, nb, seq, qn_row, kn_row):
    cb = COL_NA // MIX_W
    blk = lambda j: pl.BlockSpec((seq, MIX_W), lambda b: (b, j))
    const = pl.BlockSpec((1, MIX_W), lambda b: (0, 0))
    sds = jax.ShapeDtypeStruct((nb * seq, MIX_W), f32)
    return pl.pallas_call(
        _na_prompt_body,
        grid=(nb,),
        in_specs=[blk(cb), blk(cb + 1), blk(cb + 2), const, const],
        out_specs=[blk(0), blk(0), blk(0)],
        out_shape=[sds, sds, sds],
        compiler_params=_params(("arbitrary",)),
        name="na_prompt",
    )(u, u, u, qn_row, kn_row)


def _na_sample_body(q_ref, k_ref, v_ref, kc_ref, vc_ref, bias_ref, qn_ref, kn_ref, o_ref, kn_scr, v_scr):
    i = pl.program_id(1)

    @pl.when(i == 0)
    def _():
        kn_scr[...] = _seg_rms(k_ref[...].astype(f32), HEAD_W, kn_ref[...]).astype(bf16)
        v_scr[...] = v_ref[...].astype(bf16)

    rows = DEC_SEQ // GRID_W
    rs = jnp.clip(i - NA_WIN_R // 2, 0, rows - NA_WIN_R)
    d0 = rs - i + (NA_WIN_R - 1)
    k0 = pl.multiple_of(rs * GRID_W, GRID_W)
    nwin = NA_WIN_R * GRID_W
    qn = _seg_rms(q_ref[...].astype(f32), HEAD_W, qn_ref[...]) * (HEAD_W ** -0.5)
    kwin = kn_scr[pl.ds(k0, nwin), :]
    vwin = v_scr[pl.ds(k0, nwin), :]
    kctx = kc_ref[...]
    vctx = vc_ref[...]
    for h in range(HEADS):
        hl = slice(h * HEAD_W, (h + 1) * HEAD_W)
        s_win = _mm_nt(qn[:, hl], kwin[:, hl]) + bias_ref[h, d0]
        s_ctx = _mm_nt(qn[:, hl], kctx[:, hl])
        o_ref[:, hl] = _softmax_pv([s_win, s_ctx], [vwin, vctx], hl)


def _na_sample(u, nb, layer, cache_k, cache_v, bias, qn_row, kn_row):
    seq = DEC_SEQ
    rows = seq // GRID_W
    cb = COL_NA // MIX_W
    full = lambda j: pl.BlockSpec((seq, MIX_W), lambda b, i: (b, j))
    ctx = pl.BlockSpec((None, None, PAST_LEN, MIX_W), lambda b, i: (b, layer, 0, 0))
    const = pl.BlockSpec((1, MIX_W), lambda b, i: (0, 0))
    return pl.pallas_call(
        _na_sample_body,
        grid=(nb, rows),
        in_specs=[pl.BlockSpec((GRID_W, MIX_W), lambda b, i: (b * rows + i, cb)), full(cb + 1), full(cb + 2),
                  ctx, ctx,
                  pl.BlockSpec((HEADS, NA_WIN_R, GRID_W, NA_WIN_R * GRID_W), lambda b, i: (0, 0, 0, 0)),
                  const, const],
        out_specs=pl.BlockSpec((GRID_W, MIX_W), lambda b, i: (b * rows + i, 0)),
        out_shape=jax.ShapeDtypeStruct((nb * seq, MIX_W), f32),
        scratch_shapes=[pltpu.VMEM((seq, MIX_W), bf16), pltpu.VMEM((seq, MIX_W), bf16)],
        compiler_params=_params(("arbitrary", "arbitrary")),
        name="na_sample",
    )(u, u, u, cache_k, cache_v, bias, qn_row, kn_row)


def _na_bias_table(rpb):
    j = jnp.arange(GRID_W)
    cs = jnp.clip(j - NA_WIN_C // 2, 0, GRID_W - NA_WIN_C)
    ok = (j[None, :] >= cs[:, None]) & (j[None, :] < cs[:, None] + NA_WIN_C)
    wide = jnp.pad(rpb.astype(f32), ((0, 0), (0, 0), (GRID_W, GRID_W)))
    toe = jnp.stack([wide[:, :, GRID_W + NA_WIN_C - 1 - jj:2 * GRID_W + NA_WIN_C - 1 - jj] for jj in range(GRID_W)],
                    axis=2)
    toe = jnp.where(ok[None, None], toe, NEG)
    slabs = jnp.stack([toe[:, d0:d0 + NA_WIN_R] for d0 in range(NA_WIN_R)], axis=1)
    return jnp.transpose(slabs, (0, 1, 3, 2, 4)).reshape(HEADS, NA_WIN_R, GRID_W, NA_WIN_R * GRID_W)


def _diff_lambda(lam_ref, lam_init):
    lv = lam_ref[...]
    s1 = jnp.sum(lv[0:1, :] * lv[1:2, :], axis=1, keepdims=True)
    s2 = jnp.sum(lv[2:3, :] * lv[3:4, :], axis=1, keepdims=True)
    return jnp.exp(s1) - jnp.exp(s2) + lam_init


def _diff_head(q, keys, vals, lam, h):
    hl = slice(h * HEAD_W, (h + 1) * HEAD_W)
    acc = None
    for m in range(2):
        ml = slice(h * HEAD_W + m * DIFF_QK_DIM, h * HEAD_W + (m + 1) * DIFF_QK_DIM)
        scores = [_mm_nt(q[:, ml], k[:, ml]) for k in keys]
        mx = scores[0].max(axis=1, keepdims=True)
        for s in scores[1:]:
            mx = jnp.maximum(mx, s.max(axis=1, keepdims=True))
        ps = [jnp.exp(s - mx) for s in scores]
        den = ps[0].sum(axis=1, keepdims=True)
        for p in ps[1:]:
            den = den + p.sum(axis=1, keepdims=True)
        coef = (1.0 / den) if m == 0 else (-lam / den)
        ps = [p * coef for p in ps]
        acc = ps if acc is None else [a + p for a, p in zip(acc, ps)]
    out = None
    for a, v in zip(acc, vals):
        o = _mm(a, v[:, hl])
        out = o if out is None else out + o
    return out


def _diff_prompt_body(lam_init, q_ref, k_ref, v_ref, qn_ref, kn_ref, lam_ref, nw_ref, o_ref, ko_ref, vo_ref):
    lam = _diff_lambda(lam_ref, lam_init)
    qn = _seg_rms(q_ref[...].astype(f32), DIFF_QK_DIM, qn_ref[...]) * (DIFF_QK_DIM ** -0.5)
    kn = _seg_rms(k_ref[...].astype(f32), DIFF_QK_DIM, kn_ref[...])
    v = v_ref[...].astype(f32)
    ko_ref[...] = kn
    vo_ref[...] = v
    o = jnp.concatenate([_diff_head(qn, [kn], [v], lam, h) for h in range(HEADS)], axis=1)
    o_ref[...] = _seg_rms(o, HEAD_W, nw_ref[...]) * (1.0 - lam_init)


def _diff_prompt(u, nb, seq, lam_init, qn_row, kn_row, lam_vecs, nw_row):
    cb = COL_DIFF // MIX_W
    blk = lambda j: pl.BlockSpec((seq, MIX_W), lambda b: (b, j))
    const = pl.BlockSpec((1, MIX_W), lambda b: (0, 0))
    sds = jax.ShapeDtypeStruct((nb * seq, MIX_W), f32)
    return pl.pallas_call(
        functools.partial(_diff_prompt_body, lam_init),
        grid=(nb,),
        in_specs=[blk(cb), blk(cb + 1), blk(cb + 2), const, const,
                  pl.BlockSpec((4, DIFF_QK_DIM), lambda b: (0, 0)), const],
        out_specs=[blk(0), blk(0), blk(0)],
        out_shape=[sds, sds, sds],
        compiler_params=_params(("arbitrary",)),
        name="diff_prompt",
    )(u, u, u, qn_row, kn_row, lam_vecs, nw_row)


def _rope(x, cos, sin):
    w = x.shape[-1]
    up = pltpu.roll(x, w - 8, 1)
    dn = pltpu.roll(x, 8, 1)
    first = (lax.broadcasted_iota(jnp.int32, x.shape, 1) % 16) < 8
    return x * cos + jnp.where(first, up, dn) * sin


def _diff_sample_body(lam_init, qblk, q_ref, k_ref, v_ref, kc_ref, vc_ref, cos_ref, sin_ref, qn_ref, kn_ref,
                      lam_ref, nw_ref, o_ref, kn_scr, v_scr):
    i = pl.program_id(1)

    @pl.when(i == 0)
    def _():
        kn = _seg_rms(k_ref[...].astype(f32), DIFF_QK_DIM, kn_ref[...])
        kn_scr[...] = _rope(kn, cos_ref[...], sin_ref[...]).astype(bf16)
        v_scr[...] = v_ref[...].astype(bf16)

    lam = _diff_lambda(lam_ref, lam_init)
    r0 = pl.multiple_of(i * qblk, qblk)
    qn = _seg_rms(q_ref[...].astype(f32), DIFF_QK_DIM, qn_ref[...])
    qn = _rope(qn, cos_ref[pl.ds(r0, qblk), :], sin_ref[pl.ds(r0, qblk), :]) * (DIFF_QK_DIM ** -0.5)
    keys = [kn_scr[...], kc_ref[...]]
    vals = [v_scr[...], vc_ref[...]]
    o = jnp.concatenate([_diff_head(qn, keys, vals, lam, h) for h in range(HEADS)], axis=1)
    o_ref[...] = _seg_rms(o, HEAD_W, nw_ref[...]) * (1.0 - lam_init)


def _diff_sample(u, nb, layer, lam_init, cache_k, cache_v, cos, sin, qn_row, kn_row, lam_vecs, nw_row):
    seq = DEC_SEQ
    qblk = 128
    nq = seq // qblk
    cb = COL_DIFF // MIX_W
    full = lambda j: pl.BlockSpec((seq, MIX_W), lambda b, i: (b, j))
    ctx = pl.BlockSpec((None, None, PAST_LEN, MIX_W), lambda b, i: (b, layer, 0, 0))
    const = pl.BlockSpec((1, MIX_W), lambda b, i: (0, 0))
    tab = pl.BlockSpec((seq, MIX_W), lambda b, i: (0, 0))
    return pl.pallas_call(
        functools.partial(_diff_sample_body, lam_init, qblk),
        grid=(nb, nq),
        in_specs=[pl.BlockSpec((qblk, MIX_W), lambda b, i: (b * nq + i, cb)), full(cb + 1), full(cb + 2),
                  ctx, ctx, tab, tab, const, const,
                  pl.BlockSpec((4, DIFF_QK_DIM), lambda b, i: (0, 0)), const],
        out_specs=pl.BlockSpec((qblk, MIX_W), lambda b, i: (b * nq + i, 0)),
        out_shape=jax.ShapeDtypeStruct((nb * seq, MIX_W), f32),
        scratch_shapes=[pltpu.VMEM((seq, MIX_W), bf16), pltpu.VMEM((seq, MIX_W), bf16)],
        compiler_params=_params(("arbitrary", "arbitrary")),
        name="diff_sample",
    )(u, u, u, cache_k, cache_v, cos, sin, qn_row, kn_row, lam_vecs, nw_row)


def _rope_tables():
    t = jnp.arange(DEC_SEQ)
    nf = DIFF_QK_DIM // 4
    inv = ROPE_BASE ** (-jnp.arange(nf, dtype=f32) / nf)
    ar = (t // GRID_W).astype(f32)[:, None] * inv
    ac = (t % GRID_W).astype(f32)[:, None] * inv
    cos = jnp.concatenate([jnp.cos(ar), jnp.cos(ar), jnp.cos(ac), jnp.cos(ac)], axis=1)
    sin = jnp.concatenate([-jnp.sin(ar), jnp.sin(ar), -jnp.sin(ac), jnp.sin(ac)], axis=1)
    reps = MIX_W // DIFF_QK_DIM
    return jnp.tile(cos, (1, reps)), jnp.tile(sin, (1, reps))


def _merge_body(x_ref, a_ref, b_ref, c_ref, d_ref, gp_ref, mod_ref, wb_ref, wo_ref, nw_ref, wr_ref,
                xo_ref, h2_ref, aff_ref):
    mixed = None
    for n, br in enumerate((a_ref, b_ref, c_ref, d_ref)):
        gate = jax.nn.sigmoid(gp_ref[:, n * D_MODEL:(n + 1) * D_MODEL].astype(f32))
        t = gate * jnp.dot(br[...].astype(bf16), wb_ref[n], preferred_element_type=f32)
        mixed = t if mixed is None else mixed + t
    y = jnp.dot(mixed.astype(bf16), wo_ref[...], preferred_element_type=f32)
    x = x_ref[...] + mod_ref[:, 2 * D_MODEL:3 * D_MODEL] * y
    xo_ref[...] = x
    h = x * lax.rsqrt(jnp.mean(x * x, axis=-1, keepdims=True) + EPS) * nw_ref[...]
    h = h * (1.0 + mod_ref[:, 4 * D_MODEL:5 * D_MODEL]) + mod_ref[:, 3 * D_MODEL:4 * D_MODEL]
    hb = h.astype(bf16)
    h2_ref[...] = hb
    logits = jnp.dot(hb, wr_ref[...], preferred_element_type=f32)
    lane = lax.broadcasted_iota(jnp.int32, logits.shape, 1)
    logits = jnp.where(lane < N_EXPERTS, logits, NEG)
    e = jnp.exp(logits - logits.max(axis=1, keepdims=True))
    aff_ref[...] = e / e.sum(axis=1, keepdims=True)


def _merge(x, brs, u, mod3, mod_base, mod_stride, seq, wb, wo, nw_row, wr):
    n = x.shape[0]
    tm = 256
    tps = seq // tm
    row = lambda w: pl.BlockSpec((tm, w), lambda i: (i, 0))
    return pl.pallas_call(
        _merge_body,
        grid=(n // tm,),
        in_specs=[row(D_MODEL), row(MIX_W), row(MIX_W), row(MIX_W), row(MIX_W),
                  pl.BlockSpec((tm, N_BRANCH * D_MODEL), lambda i: (i, COL_GPRE // (N_BRANCH * D_MODEL))),
                  pl.BlockSpec((None, 1, 6 * D_MODEL), lambda i: (mod_base + mod_stride * (i // tps), 0, 0)),
                  pl.BlockSpec((N_BRANCH, MIX_W, D_MODEL), lambda i: (0, 0, 0)),
                  pl.BlockSpec((D_MODEL, D_MODEL), lambda i: (0, 0)),
                  pl.BlockSpec((1, D_MODEL), lambda i: (0, 0)),
                  pl.BlockSpec((D_MODEL, LANE), lambda i: (0, 0))],
        out_specs=[row(D_MODEL), row(D_MODEL), row(LANE)],
        out_shape=[jax.ShapeDtypeStruct((n, D_MODEL), f32), jax.ShapeDtypeStruct((n, D_MODEL), bf16),
                   jax.ShapeDtypeStruct((n, LANE), f32)],
        compiler_params=_params(("arbitrary",)),
        name="merge",
    )(x, *brs, u, mod3, wb, wo, nw_row, wr)


def _route_body(t, cap, aff_ref, h2_ref, xg_ref, g_ref, rank_ref, rrow_scr):
    aff = aff_ref[...]
    afft = aff.T
    sub = lax.broadcasted_iota(jnp.int32, (t, t), 0)
    lan = lax.broadcasted_iota(jnp.int32, (t, t), 1)
    earlier = sub < lan
    rrow_scr[...] = jnp.zeros(rrow_scr.shape, f32) + float(cap)
    slot = lax.broadcasted_iota(jnp.int32, (cap, t), 0).astype(f32)
    h2 = h2_ref[...]
    for e in range(N_EXPERTS):
        a_col = aff[:, e:e + 1]
        a_row = afft[e:e + 1, :]
        beats = (a_col > a_row) | ((a_col == a_row) & earlier)
        rank = jnp.sum(jnp.where(beats, 1.0, 0.0), axis=0, keepdims=True)
        rank = jnp.minimum(rank, float(cap))
        rrow_scr[e:e + 1, :] = rank
        onehot = jnp.where(rank == slot, 1.0, 0.0)
        xg_ref[e] = jnp.dot(onehot.astype(bf16), h2, preferred_element_type=f32).astype(bf16)
        g_ref[e] = jnp.sum(onehot * a_row, axis=1, keepdims=True)
    rank_ref[...] = rrow_scr[...].T


def _route(aff, h2, nb, t):
    cap = CAPACITY_FACTOR * t // N_EXPERTS
    return pl.pallas_call(
        functools.partial(_route_body, t, cap),
        grid=(nb,),
        in_specs=[pl.BlockSpec((t, LANE), lambda b: (b, 0)), pl.BlockSpec((t, D_MODEL), lambda b: (b, 0))],
        out_specs=[pl.BlockSpec((N_EXPERTS, cap, D_MODEL), lambda b: (0, b, 0)),
                   pl.BlockSpec((N_EXPERTS, cap, 1), lambda b: (0, b, 0)),
                   pl.BlockSpec((t, LANE), lambda b: (b, 0))],
        out_shape=[jax.ShapeDtypeStruct((N_EXPERTS, nb * cap, D_MODEL), bf16),
                   jax.ShapeDtypeStruct((N_EXPERTS, nb * cap, 1), f32),
                   jax.ShapeDtypeStruct((nb * t, LANE), f32)],
        scratch_shapes=[pltpu.VMEM((LANE, t), f32)],
        compiler_params=_params(("arbitrary",)),
        name=f"route_{t}",
    )(aff, h2)


def _ffn_body(xp_ref, xs_ref, gp_ref, gs_ref, wg_ref, wu_ref, wd_ref, yp_ref, ys_ref, wg_scr, wu_scr, wd_scr):
    wg_scr[...] = wg_ref[...].astype(bf16)
    wu_scr[...] = wu_ref[...].astype(bf16)
    wd_scr[...] = wd_ref[...].astype(bf16)
    blk = 256
    for x_ref, g_ref, y_ref in ((xp_ref, gp_ref, yp_ref), (xs_ref, gs_ref, ys_ref)):
        def step(i, carry, x_ref=x_ref, g_ref=g_ref, y_ref=y_ref):
            r0 = pl.multiple_of(i * blk, blk)
            x = x_ref[pl.ds(r0, blk), :]
            a = jnp.dot(x, wg_scr[...], preferred_element_type=f32)
            up = jnp.dot(x, wu_scr[...], preferred_element_type=f32)
            mid = (jax.nn.silu(a) * up).astype(bf16)
            y = jnp.dot(mid, wd_scr[...], preferred_element_type=f32)
            y_ref[pl.ds(r0, blk), :] = (y * g_ref[pl.ds(r0, blk), :]).astype(bf16)
            return carry
        lax.fori_loop(0, x_ref.shape[0] // blk, step, 0)


def _ffn(xp, xs, gp, gs, w_gate, w_up, w_down, layer):
    np_, ns = xp.shape[1], xs.shape[1]
    tok = lambda n, w: pl.BlockSpec((None, n, w), lambda e: (e, 0, 0))
    wspec = lambda a, b: pl.BlockSpec((None, None, a, b), lambda e: (layer, e, 0, 0))
    return pl.pallas_call(
        _ffn_body,
        grid=(N_EXPERTS,),
        in_specs=[tok(np_, D_MODEL), tok(ns, D_MODEL), tok(np_, 1), tok(ns, 1),
                  wspec(D_MODEL, EXPERT_FF), wspec(D_MODEL, EXPERT_FF), wspec(EXPERT_FF, D_MODEL)],
        out_specs=[tok(np_, D_MODEL), tok(ns, D_MODEL)],
        out_shape=[jax.ShapeDtypeStruct(xp.shape, bf16), jax.ShapeDtypeStruct(xs.shape, bf16)],
        scratch_shapes=[pltpu.VMEM((D_MODEL, EXPERT_FF), bf16), pltpu.VMEM((D_MODEL, EXPERT_FF), bf16),
                        pltpu.VMEM((EXPERT_FF, D_MODEL), bf16)],
        compiler_params=_params(("arbitrary",)),
        name="expert_ffn",
    )(xp, xs, gp, gs, w_gate, w_up, w_down)


def _combine_body(t, cap, x_ref, rank_ref, y_ref, mod_ref, o_ref):
    k = N_EXPERTS * cap
    er = lax.broadcasted_iota(jnp.int32, (LANE, k), 0)
    ec = lax.broadcasted_iota(jnp.int32, (LANE, k), 1) // cap
    expand = jnp.where(er == ec, 1.0, 0.0).astype(bf16)
    rexp = jnp.dot(rank_ref[...].astype(bf16), expand, preferred_element_type=f32)
    slot = (lax.broadcasted_iota(jnp.int32, (t, k), 1) % cap).astype(f32)
    pt = jnp.where(rexp == slot, 1.0, 0.0).astype(bf16)
    moe = jnp.dot(pt, y_ref[...].reshape(k, D_MODEL), preferred_element_type=f32)
    o_ref[...] = x_ref[...] + mod_ref[:, 5 * D_MODEL:6 * D_MODEL] * moe


def _combine(x, rank, y, mod3, mod_base, mod_stride, nb, t):
    cap = CAPACITY_FACTOR * t // N_EXPERTS
    return pl.pallas_call(
        functools.partial(_combine_body, t, cap),
        grid=(nb,),
        in_specs=[pl.BlockSpec((t, D_MODEL), lambda b: (b, 0)),
                  pl.BlockSpec((t, LANE), lambda b: (b, 0)),
                  pl.BlockSpec((N_EXPERTS, cap, D_MODEL), lambda b: (0, b, 0)),
                  pl.BlockSpec((None, 1, 6 * D_MODEL), lambda b: (mod_base + mod_stride * b, 0, 0))],
        out_specs=pl.BlockSpec((t, D_MODEL), lambda b: (b, 0)),
        out_shape=jax.ShapeDtypeStruct(x.shape, f32),
        compiler_params=_params(("arbitrary",)),
        name=f"combine_{t}",
    )(x, rank, y, mod3)


def _pad_row(v, off, width=LANE):
    return jnp.zeros((1, width), f32).at[0, off:off + v.shape[0]].set(v.astype(f32))


def _layer_params(l, a):
    w_in = a["w_in"][l]
    pts, acc = [], 0
    splits = (MIX_W,) * 4 + (4 * HEADS,) + (MIX_W,) * 7 + (SSM_CONV_CH, 2 * HEADS, N_BRANCH * D_MODEL)
    for s in splits:
        pts.append((acc, acc + s))
        acc += s
    seg = lambda i: w_in[:, pts[i][0]:pts[i][1]]
    order = [14, 0, 1, 2, 3, 5, 6, 7, 8, 9, 10, 12, 11]
    w_main = jnp.concatenate([seg(i) for i in order] + [jnp.zeros((D_MODEL, U_COLS - U_USED), f32)],
                             axis=1).astype(bf16)
    w_side = jnp.concatenate([seg(4), seg(13), jnp.zeros((D_MODEL, LANE - 24), f32)], axis=1).astype(bf16)
    tile = lambda v, reps: jnp.tile(v.astype(f32), reps).reshape(1, -1)
    p = dict(
        w_main=w_main, w_side=w_side,
        norm1=a["norm1_w"][l].reshape(1, -1), norm2=a["norm2_w"][l].reshape(1, -1),
        mlstm_gb=_pad_row(a["mlstm_gate_b"][l].reshape(-1), G_MLSTM),
        mlstm_nw=a["mlstm_norm_w"][l].reshape(1, -1),
        na_qn=tile(a["na_qnorm_w"][l], HEADS), na_kn=tile(a["na_knorm_w"][l], HEADS),
        na_bias=_na_bias_table(a["na_rpb"][l]),
        diff_qn=tile(a["diff_qnorm_w"][l], 2 * HEADS), diff_kn=tile(a["diff_knorm_w"][l], 2 * HEADS),
        diff_lam=a["diff_lambda"][l], diff_nw=tile(a["diff_norm_w"][l], HEADS),
        ssm_cw=a["ssm_conv_w"][l], ssm_cb=a["ssm_conv_b"][l].reshape(1, -1),
        ssm_dtb=_pad_row(a["ssm_dt_bias"][l].reshape(-1), G_DT),
        ssm_alog=_pad_row(a["ssm_a_log"][l].reshape(-1), G_DT),
        ssm_d=jnp.repeat(a["ssm_d"][l].astype(f32), HEAD_W).reshape(1, -1),
        ssm_nw=a["ssm_norm_w"][l].reshape(1, -1),
        wb=a["w_branch"][l].astype(bf16), wo=a["w_out"][l].astype(bf16),
        wr=jnp.concatenate([a["w_router"][l], jnp.zeros((D_MODEL, LANE - N_EXPERTS), f32)], axis=1).astype(bf16),
        w_gate=a["w_gate"], w_up=a["w_up"], w_down=a["w_down"],
    )
    return p


def _mixers(u, g, nb, seq, p, l, lam_init, ctx, tables):
    if ctx is None:
        a_out, m_c, m_n, m_m = _mlstm(u, g, nb, seq, p["mlstm_gb"], p["mlstm_nw"], None)
        b_out, na_k, na_v = _na_prompt(u, nb, seq, p["na_qn"], p["na_kn"])
        c_out, df_k, df_v = _diff_prompt(u, nb, seq, lam_init, p["diff_qn"], p["diff_kn"], p["diff_lam"], p["diff_nw"])
        d_out, s_h = _ssd(u, g, nb, seq, p["ssm_cw"], p["ssm_cb"], p["ssm_dtb"], p["ssm_alog"], p["ssm_d"],
                          p["ssm_nw"], None)
        return (a_out, b_out, c_out, d_out), (na_k, na_v, df_k, df_v, m_c, m_n, m_m, s_h)
    (na_ck, na_cv, df_ck, df_cv, st_c, st_n, st_m, st_s) = ctx
    cos, sin = tables
    a_out = _mlstm(u, g, nb, seq, p["mlstm_gb"], p["mlstm_nw"], (st_c[:, l], st_n[:, l], st_m[:, l]))[0]
    b_out = _na_sample(u, nb, l, na_ck, na_cv, p["na_bias"], p["na_qn"], p["na_kn"])
    c_out = _diff_sample(u, nb, l, lam_init, df_ck, df_cv, cos, sin, p["diff_qn"], p["diff_kn"], p["diff_lam"],
                         p["diff_nw"])
    d_out = _ssd(u, g, nb, seq, p["ssm_cw"], p["ssm_cb"], p["ssm_dtb"], p["ssm_alog"], p["ssm_d"], p["ssm_nw"],
                 st_s[:, l])[0]
    return (a_out, b_out, c_out, d_out), None


def kernel(x_prompt, x_sample, cache_na_k, cache_na_v, cache_diff_k, cache_diff_v, state_mlstm_C, state_mlstm_n, state_mlstm_m, state_ssm, c, c_ctx, norm1_w, norm2_w, w_ada, b_ada, w_in, mlstm_gate_b, mlstm_norm_w, na_qnorm_w, na_knorm_w, na_rpb, diff_qnorm_w, diff_knorm_w, diff_lambda, diff_norm_w, ssm_conv_w, ssm_conv_b, ssm_dt_bias, ssm_a_log, ssm_d, ssm_norm_w, w_branch, w_out, w_router, w_gate, w_up, w_down):
    a = dict(norm1_w=norm1_w, norm2_w=norm2_w, w_in=w_in, mlstm_gate_b=mlstm_gate_b, mlstm_norm_w=mlstm_norm_w,
             na_qnorm_w=na_qnorm_w, na_knorm_w=na_knorm_w, na_rpb=na_rpb, diff_qnorm_w=diff_qnorm_w,
             diff_knorm_w=diff_knorm_w, diff_lambda=diff_lambda, diff_norm_w=diff_norm_w, ssm_conv_w=ssm_conv_w,
             ssm_conv_b=ssm_conv_b, ssm_dt_bias=ssm_dt_bias, ssm_a_log=ssm_a_log, ssm_d=ssm_d,
             ssm_norm_w=ssm_norm_w, w_branch=w_branch, w_out=w_out, w_router=w_router, w_gate=w_gate, w_up=w_up,
             w_down=w_down)
    nb_p, nb_s = x_prompt.shape[0], x_sample.shape[0]
    xp = x_prompt.reshape(nb_p * SEQ, D_MODEL)
    xs = x_sample.reshape(nb_s * DEC_SEQ, D_MODEL)
    cc = jnp.concatenate([c_ctx[None, :], c, jnp.zeros((8 - 1 - nb_s, D_MODEL), f32)], axis=0)
    ctx = (cache_na_k.reshape(nb_s, DEPTH, PAST_LEN, MIX_W), cache_na_v.reshape(nb_s, DEPTH, PAST_LEN, MIX_W),
           cache_diff_k.reshape(nb_s, DEPTH, PAST_LEN, MIX_W), cache_diff_v.reshape(nb_s, DEPTH, PAST_LEN, MIX_W),
           state_mlstm_C, state_mlstm_n, state_mlstm_m, state_ssm)
    tables = _rope_tables()
    new = []
    for l in range(DEPTH):
        p = _layer_params(l, a)
        lam_init = 0.8 - 0.6 * math.exp(-0.3 * l)
        mod3 = _modulation(cc, w_ada, b_ada, l).reshape(8, 1, 6 * D_MODEL)
        groups = []
        for (x, nb, seq, base, stride, gctx) in ((xp, nb_p, SEQ, 0, 0, None), (xs, nb_s, DEC_SEQ, 1, 1, ctx)):
            u, g = _inproj(x, mod3, base, stride, seq, p["norm1"], p["w_main"], p["w_side"])
            brs, st = _mixers(u, g, nb, seq, p, l, lam_init, gctx, tables)
            if st is not None:
                new.append(st)
            x1, h2, aff = _merge(x, brs, u, mod3, base, stride, seq, p["wb"], p["wo"], p["norm2"], p["wr"])
            xg, gg, rank = _route(aff, h2, nb, seq)
            groups.append((x1, xg, gg, rank, nb, seq, base, stride))
        yp, ys = _ffn(groups[0][1], groups[1][1], groups[0][2], groups[1][2], p["w_gate"], p["w_up"], p["w_down"], l)
        outs = []
        for (x1, _, _, rank, nb, seq, base, stride), y in zip(groups, (yp, ys)):
            outs.append(_combine(x1, rank, y, mod3, base, stride, nb, seq))
        xp, xs = outs
    stack = lambda i, shape: jnp.stack([new[l][i].reshape(shape) for l in range(DEPTH)], axis=1)
    kv = (nb_p, SEQ, HEADS, HEAD_W)
    return (xp.reshape(x_prompt.shape), xs.reshape(x_sample.shape),
            stack(0, kv), stack(1, kv), stack(2, kv), stack(3, kv),
            stack(4, (nb_p, 2, HEADS, HEAD_W, HEAD_W)), stack(5, (nb_p, 2, HEADS, HEAD_W)),
            stack(6, (nb_p, 2, HEADS)), stack(7, (nb_p, 2, HEADS, HEAD_W, SSM_STATE)))
```

```python
import functools
import math

import jax
import jax.numpy as jnp
from jax import lax
from jax.experimental import pallas as pl
from jax.experimental.pallas import tpu as pltpu

f32 = jnp.float32
bf16 = jnp.bfloat16

D_MODEL = 1024
BATCH = 32
SEQ = 256
DEPTH = 2
DEC_BATCH = 4
DEC_SEQ = 1024
PAST_LEN = 512
GRID_W = 64
MIX_W = D_MODEL // 4
N_BRANCH = 4
HEADS = 4
HEAD_W = MIX_W // HEADS
NA_WIN_R = 8
NA_WIN_C = 16
DIFF_QK_DIM = HEAD_W // 2
SSM_GROUPS = 2
SSM_STATE = 64
SSM_CONV_K = 5
SSM_CONV_CH = MIX_W + 2 * SSM_GROUPS * SSM_STATE
CHUNK = 64
N_EXPERTS = 16
EXPERT_FF = 1024
CAPACITY_FACTOR = 2
ROPE_BASE = 10000.0
EPS = 1e-6
NEG = -1e30

LANE = 128
VMEM_LIMIT = 56 * 1024 * 1024

COL_GPRE = 0
COL_MLSTM = 4096
COL_NA = 5120
COL_DIFF = 5888
COL_XBC = 6656
COL_Z = 7168
U_USED = 7424
U_COLS = 7680
G_MLSTM = 0
G_DT = 16


def _params(sem):
    return pltpu.CompilerParams(dimension_semantics=sem, vmem_limit_bytes=VMEM_LIMIT)


def _mm(a, b):
    return jnp.dot(a.astype(bf16), b.astype(bf16), preferred_element_type=f32)


def _mm_nt(a, b):
    return lax.dot_general(a.astype(bf16), b.astype(bf16), (((1,), (1,)), ((), ())), preferred_element_type=f32)


def _mm_tn(a, b):
    return lax.dot_general(a.astype(bf16), b.astype(bf16), (((0,), (0,)), ((), ())), preferred_element_type=f32)


def _seg_sum(x, seg):
    w = x.shape[-1]
    r = lax.broadcasted_iota(jnp.int32, (w, w), 0) // seg
    c = lax.broadcasted_iota(jnp.int32, (w, w), 1) // seg
    ones = jnp.where(r == c, 1.0, 0.0).astype(bf16)
    hi = x.astype(bf16)
    lo = (x - hi.astype(f32)).astype(bf16)
    return jnp.dot(hi, ones, preferred_element_type=f32) + jnp.dot(lo, ones, preferred_element_type=f32)


def _seg_rms(x, seg, w_row):
    ms = _seg_sum(x * x, seg) * (1.0 / seg)
    return x * lax.rsqrt(ms + EPS) * w_row


def _mod_body(c_ref, w_ref, b_ref, o_ref):
    o_ref[...] = _mm(jax.nn.silu(c_ref[...]), w_ref[...]) + b_ref[...]


def _modulation(cc, w_ada, b_ada, layer):
    tn = 512
    n = w_ada.shape[2]
    return pl.pallas_call(
        _mod_body,
        grid=(n // tn,),
        in_specs=[pl.BlockSpec((8, D_MODEL), lambda j: (0, 0)),
                  pl.BlockSpec((None, D_MODEL, tn), lambda j: (layer, 0, j)),
                  pl.BlockSpec((None, 1, tn), lambda j: (layer, 0, j))],
        out_specs=pl.BlockSpec((8, tn), lambda j: (0, j)),
        out_shape=jax.ShapeDtypeStruct((8, n), f32),
        compiler_params=_params(("arbitrary",)),
        name="modulation",
    )(cc, w_ada, b_ada.reshape(b_ada.shape[0], 1, n))


def _inproj_body(x_ref, mod_ref, nw_ref, w_ref, wg_ref, u_ref, g_ref, h_scr):
    @pl.when(pl.program_id(1) == 0)
    def _():
        x = x_ref[...]
        y = x * lax.rsqrt(jnp.mean(x * x, axis=-1, keepdims=True) + EPS) * nw_ref[...]
        h = y * (1.0 + mod_ref[:, D_MODEL:2 * D_MODEL]) + mod_ref[:, 0:D_MODEL]
        hb = h.astype(bf16)
        h_scr[...] = hb
        g_ref[...] = jnp.dot(hb, wg_ref[...], preferred_element_type=f32)

    u_ref[...] = jnp.dot(h_scr[...], w_ref[...], preferred_element_type=f32).astype(u_ref.dtype)


def _inproj(x, mod3, mod_base, mod_stride, seq, nw_row, w_main, w_side):
    n = x.shape[0]
    tm, tn = 1024, U_COLS // 4
    tps = seq // tm if seq >= tm else 1
    return pl.pallas_call(
        _inproj_body,
        grid=(n // tm, U_COLS // tn),
        in_specs=[pl.BlockSpec((tm, D_MODEL), lambda i, j: (i, 0)),
                  pl.BlockSpec((None, 1, 6 * D_MODEL), lambda i, j: (mod_base + mod_stride * (i // tps), 0, 0)),
                  pl.BlockSpec((1, D_MODEL), lambda i, j: (0, 0)),
                  pl.BlockSpec((D_MODEL, tn), lambda i, j: (0, j)),
                  pl.BlockSpec((D_MODEL, LANE), lambda i, j: (0, 0))],
        out_specs=[pl.BlockSpec((tm, tn), lambda i, j: (i, j)),
                   pl.BlockSpec((tm, LANE), lambda i, j: (i, 0))],
        out_shape=[jax.ShapeDtypeStruct((n, U_COLS), f32), jax.ShapeDtypeStruct((n, LANE), f32)],
        scratch_shapes=[pltpu.VMEM((tm, D_MODEL), bf16)],
        compiler_params=_params(("arbitrary", "arbitrary")),
        name="inproj",
    )(x, mod3, nw_row, w_main, w_side)


def _split3(x):
    hi = x.astype(bf16)
    r = x - hi.astype(f32)
    mid = r.astype(bf16)
    lo = (r - mid.astype(f32)).astype(bf16)
    return hi, mid, lo


def _select_cols(x, onehot):
    hi, mid, lo = _split3(x)
    d = lambda p: jnp.dot(p, onehot, preferred_element_type=f32)
    return (d(hi) + d(mid)) + d(lo)


def _select_rows(onehot, x):
    hi, mid, lo = _split3(x)
    d = lambda p: jnp.dot(onehot, p, preferred_element_type=f32)
    return (d(hi) + d(mid)) + d(lo)


def _chunk_scan(x, op, identity, reverse):
    n = x.shape[0]
    pos = lax.broadcasted_iota(jnp.int32, x.shape, 0) % CHUNK
    k = 1
    while k < CHUNK:
        if reverse:
            shifted, ok = pltpu.roll(x, n - k, 0), pos < CHUNK - k
        else:
            shifted, ok = pltpu.roll(x, k, 0), pos >= k
        x = op(x, jnp.where(ok, shifted, identity))
        k *= 2
    return x


def _dir_scan(x, op, identity, bwd_col):
    return jnp.where(bwd_col, _chunk_scan(x, op, identity, True), _chunk_scan(x, op, identity, False))


def _head_expand(col0):
    c = lax.broadcasted_iota(jnp.int32, (LANE, MIX_W), 0)
    h = lax.broadcasted_iota(jnp.int32, (LANE, MIX_W), 1) // HEAD_W
    return jnp.where(c == col0 + h, 1.0, 0.0).astype(bf16)


def _chunk_rows(x, seq):
    nrow = max(seq // CHUNK, 8)
    t = lax.broadcasted_iota(jnp.int32, (seq, MIX_W), 0) % CHUNK
    s_ = lax.broadcasted_iota(jnp.int32, (seq, MIX_W), 1) % HEAD_W
    sel = lax.broadcasted_iota(jnp.int32, (nrow, seq), 1) // CHUNK == lax.broadcasted_iota(jnp.int32, (nrow, seq), 0)
    return _select_rows(jnp.where(sel, 1.0, 0.0).astype(bf16), jnp.where(t == s_, x, 0.0))


def _block_diag(x, reps):
    r, c = x.shape
    t = jnp.concatenate([x] * reps, axis=0)
    rb = lax.broadcasted_iota(jnp.int32, t.shape, 0) // r
    cb = lax.broadcasted_iota(jnp.int32, t.shape, 1) // (c // reps)
    return jnp.where(rb == cb, t, jnp.zeros_like(t))


def _head_mask(rows, cols, rseg, cseg):
    return (lax.broadcasted_iota(jnp.int32, (rows, cols), 0) // rseg) == (lax.broadcasted_iota(jnp.int32, (rows, cols), 1) // cseg)


def _mlstm_body(seq, has_state, *refs):
    if has_state:
        (qkvo_ref, g_ref, gb_ref, nw_ref, c0_ref, n0_ref, m0_ref,
         out_ref, cs_ref, ns_ref, ms_ref, bx_scr, mx_scr, vx_scr, vrow_scr, cbd_scr, hf_scr, hb_scr) = refs
    else:
        (qkvo_ref, g_ref, gb_ref, nw_ref,
         out_ref, cs_ref, ns_ref, ms_ref, bx_scr, mx_scr, vx_scr, vrow_scr, cbd_scr, hf_scr, hb_scr) = refs
    nc = seq // CHUNK
    g = g_ref[...] + gb_ref[...]
    lane = lax.broadcasted_iota(jnp.int32, g.shape, 1)
    bwd_col = (lane % 16) >= 8
    bsum = _dir_scan(jax.nn.log_sigmoid(g), jnp.add, 0.0, bwd_col)
    b_i = pltpu.roll(bsum, LANE - HEADS, 1)
    vcol = g - b_i
    mcol = b_i + _dir_scan(vcol, jnp.maximum, NEG, bwd_col)
    for d in range(2):
        e = _head_expand(G_MLSTM + d * 8)
        bx_scr[d] = _select_cols(b_i, e)
        mx_scr[d] = _select_cols(mcol, e)
        vx = _select_cols(vcol, e)
        vx_scr[d] = vx
        vrow = _chunk_rows(vx, seq)
        for c in range(nc):
            vrow_scr[d, c] = vrow[c:c + 1, :]

    grp = lax.broadcasted_iota(jnp.int32, (1, MIX_W), 1) // HEAD_W
    cbd_scr[...] = jnp.zeros(cbd_scr.shape, f32)
    n_rows, m_rows = [], []
    for d in range(2):
        if has_state:
            for h in range(HEADS):
                cbd_scr[d, h * HEAD_W:(h + 1) * HEAD_W, h * HEAD_W:(h + 1) * HEAD_W] = c0_ref[d, h]
            n_rows.append(jnp.concatenate([n0_ref[d, h:h + 1, :] for h in range(HEADS)], axis=1))
            m_row = jnp.zeros((1, MIX_W), f32)
            for h in range(HEADS):
                m_row = jnp.where(grp == h, m0_ref[d:d + 1, h:h + 1], m_row)
            m_rows.append(m_row)
        else:
            n_rows.append(jnp.zeros((1, MIX_W), f32))
            m_rows.append(jnp.zeros((1, MIX_W), f32))

    li = lax.broadcasted_iota(jnp.int32, (CHUNK, MIX_W), 0)
    si = lax.broadcasted_iota(jnp.int32, (CHUNK, MIX_W), 1) % HEAD_W
    valid = (si <= li, si >= li)
    bd = _head_mask(MIX_W, MIX_W, HEAD_W, HEAD_W)

    def step(c, carry):
        new = []
        for d in range(2):
            n_row, m_row = carry[2 * d], carry[2 * d + 1]
            cidx = c if d == 0 else nc - 1 - c
            r0 = pl.multiple_of(cidx * CHUNK, CHUNK)
            last = CHUNK - 1 if d == 0 else 0
            q = qkvo_ref[pl.ds(r0, CHUNK), 0:MIX_W]
            k = qkvo_ref[pl.ds(r0, CHUNK), MIX_W:2 * MIX_W] * (HEAD_W ** -0.5)
            v = qkvo_ref[pl.ds(r0, CHUNK), 2 * MIX_W:3 * MIX_W]
            qb, vb = q.astype(bf16), v.astype(bf16)
            bx = bx_scr[d, pl.ds(r0, CHUNK), :]
            mx = mx_scr[d, pl.ds(r0, CHUNK), :]
            vx = vx_scr[d, pl.ds(r0, CHUNK), :]
            p = jnp.exp(jnp.where(valid[d], bx + vrow_scr[d, cidx] - mx, NEG))
            smat = _mm_nt(qb, _block_diag(k.astype(bf16), HEADS)) * p
            intra = _mm(smat, _block_diag(vb, HEADS))
            dsum = _seg_sum(smat, HEAD_W)
            gg = bx + m_row
            mt = jnp.maximum(gg, mx)
            a = jnp.exp(mx - mt)
            w_prev = jnp.exp(gg - mt)
            inter = _mm(qb, cbd_scr[d])
            dint = _seg_sum(q * n_row, HEAD_W)
            num = intra * a + inter * w_prev
            den = dsum * a + dint * w_prev
            h_scr = hf_scr if d == 0 else hb_scr
            h_scr[pl.ds(r0, CHUNK), :] = num / jnp.maximum(jnp.abs(den), jnp.exp(-mt))
            bl = bx[last:last + 1, :]
            mloc = mx[last:last + 1, :]
            m_new = jnp.maximum(bl + m_row, mloc)
            dec = jnp.exp(bl + m_row - m_new)
            fac = jnp.exp(mloc - m_new)
            kw = k * jnp.exp(bl + vx - mloc)
            kv = jnp.where(bd, _mm_tn(kw, vb), 0.0)
            cbd_scr[d] = cbd_scr[d] * dec + kv * fac
            new += [n_row * dec + jnp.sum(kw, axis=0, keepdims=True) * fac, m_new]
        return tuple(new)

    fin_state = lax.fori_loop(0, nc, step, (n_rows[0], m_rows[0], n_rows[1], m_rows[1]))
    for d in range(2):
        n_row, m_row = fin_state[2 * d], fin_state[2 * d + 1]
        for h in range(HEADS):
            hl = slice(h * HEAD_W, (h + 1) * HEAD_W)
            cs_ref[d, h] = cbd_scr[d, hl, hl]
            ns_ref[d, h:h + 1, :] = n_row[:, hl]
            ms_ref[d:d + 1, h:h + 1] = m_row[:, h * HEAD_W:h * HEAD_W + 1]

    blk = 256

    def fin(i, carry):
        r0 = pl.multiple_of(i * blk, blk)
        hs = hf_scr[pl.ds(r0, blk), :] + hb_scr[pl.ds(r0, blk), :]
        hn = _seg_rms(hs, HEAD_W, nw_ref[...])
        o = qkvo_ref[pl.ds(r0, blk), 3 * MIX_W:4 * MIX_W]
        out_ref[pl.ds(r0, blk), :] = hn * jax.nn.sigmoid(o)
        return carry

    lax.fori_loop(0, seq // blk, fin, 0)


def _mlstm(u, g, nb, seq, gb_row, nw_row, state):
    has_state = state is not None
    nc = seq // CHUNK
    in_specs = [pl.BlockSpec((seq, 4 * MIX_W), lambda b: (b, COL_MLSTM // (4 * MIX_W))),
                pl.BlockSpec((seq, LANE), lambda b: (b, 0)),
                pl.BlockSpec((1, LANE), lambda b: (0, 0)),
                pl.BlockSpec((1, MIX_W), lambda b: (0, 0))]
    args = [u, g, gb_row, nw_row]
    if has_state:
        in_specs += [pl.BlockSpec((None, 2, HEADS, HEAD_W, HEAD_W), lambda b: (b, 0, 0, 0, 0)),
                     pl.BlockSpec((None, 2, HEADS, HEAD_W), lambda b: (b, 0, 0, 0)),
                     pl.BlockSpec((None, 2, HEADS), lambda b: (b, 0, 0))]
        args += list(state)
    return pl.pallas_call(
        functools.partial(_mlstm_body, seq, has_state),
        grid=(nb,),
        in_specs=in_specs,
        out_specs=[pl.BlockSpec((seq, MIX_W), lambda b: (b, 0)),
                   pl.BlockSpec((None, 2, HEADS, HEAD_W, HEAD_W), lambda b: (b, 0, 0, 0, 0)),
                   pl.BlockSpec((None, 2, HEADS, HEAD_W), lambda b: (b, 0, 0, 0)),
                   pl.BlockSpec((None, 2, HEADS), lambda b: (b, 0, 0))],
        out_shape=[jax.ShapeDtypeStruct((nb * seq, MIX_W), f32),
                   jax.ShapeDtypeStruct((nb, 2, HEADS, HEAD_W, HEAD_W), f32),
                   jax.ShapeDtypeStruct((nb, 2, HEADS, HEAD_W), f32),
                   jax.ShapeDtypeStruct((nb, 2, HEADS), f32)],
        scratch_shapes=[pltpu.VMEM((2, seq, MIX_W), f32), pltpu.VMEM((2, seq, MIX_W), f32),
                        pltpu.VMEM((2, seq, MIX_W), f32), pltpu.VMEM((2, nc, 1, MIX_W), f32),
                        pltpu.VMEM((2, MIX_W, MIX_W), f32),
                        pltpu.VMEM((seq, MIX_W), f32), pltpu.VMEM((seq, MIX_W), f32)],
        compiler_params=_params(("arbitrary",)),
        name="mlstm_state" if has_state else "mlstm",
    )(*args)


def _ssd_body(seq, has_state, *refs):
    if has_state:
        (z_ref, xbc_ref, g_ref, cw_ref, cb_ref, dtb_ref, alog_ref, dskip_ref, nw_ref, h0_ref,
         out_ref, hs_ref, xpad_scr, xc_scr, ax_scr, dx_scr, arow_scr, ht_scr, yf_scr, yb_scr) = refs
    else:
        (z_ref, xbc_ref, g_ref, cw_ref, cb_ref, dtb_ref, alog_ref, dskip_ref, nw_ref,
         out_ref, hs_ref, xpad_scr, xc_scr, ax_scr, dx_scr, arow_scr, ht_scr, yf_scr, yb_scr) = refs
    nc = seq // CHUNK
    pad = 8
    blk = 256
    nblk = seq // blk
    xpad_scr[0:pad, :] = jnp.zeros((pad, SSM_CONV_CH), f32)
    xpad_scr[pad + seq:2 * pad + seq, :] = jnp.zeros((pad, SSM_CONV_CH), f32)
    xpad_scr[pad:pad + seq, :] = xbc_ref[...].astype(f32)

    cblk = 128
    for i in range(seq // cblk):
        r0 = i * cblk
        acc = jnp.zeros((cblk, SSM_CONV_CH), f32) + cb_ref[...]
        for kk in range(SSM_CONV_K):
            off = r0 + pad - SSM_CONV_K // 2 + kk
            acc = acc + xpad_scr[off:off + cblk, :] * cw_ref[kk:kk + 1, :]
        xc_scr[r0:r0 + cblk, :] = jax.nn.silu(acc)

    dt = jax.nn.softplus(g_ref[...] + dtb_ref[...])
    lane = lax.broadcasted_iota(jnp.int32, dt.shape, 1)
    acum = _dir_scan(dt * (-jnp.exp(alog_ref[...])), jnp.add, 0.0, (lane % 8) >= HEADS)
    for d in range(2):
        e = _head_expand(G_DT + d * HEADS)
        ax = _select_cols(acum, e)
        ax_scr[d] = ax
        dx_scr[d] = _select_cols(dt, e)
        arow = _chunk_rows(ax, seq)
        for c in range(nc):
            arow_scr[d, c] = arow[c:c + 1, :]

    gn = SSM_GROUPS * SSM_STATE
    rep = HEADS // SSM_GROUPS
    ht_scr[...] = jnp.zeros(ht_scr.shape, f32)
    if has_state:
        for d in range(2):
            for h in range(HEADS):
                g0 = (h // rep) * SSM_STATE
                ht_scr[d, g0:g0 + SSM_STATE, h * HEAD_W:(h + 1) * HEAD_W] = h0_ref[d, h].T
    li = lax.broadcasted_iota(jnp.int32, (CHUNK, MIX_W), 0)
    si = lax.broadcasted_iota(jnp.int32, (CHUNK, MIX_W), 1) % HEAD_W
    valid = (si <= li, si >= li)
    state_mask = _head_mask(gn, MIX_W, SSM_STATE, rep * HEAD_W)
    key_mask = _head_mask(HEADS * CHUNK, gn, rep * CHUNK, SSM_STATE)

    def step(c, carry):
        for d in range(2):
            cidx = c if d == 0 else nc - 1 - c
            r0 = pl.multiple_of(cidx * CHUNK, CHUNK)
            last = CHUNK - 1 if d == 0 else 0
            xs = xc_scr[pl.ds(r0, CHUNK), 0:MIX_W]
            bm = xc_scr[pl.ds(r0, CHUNK), MIX_W:MIX_W + gn].astype(bf16)
            cm = xc_scr[pl.ds(r0, CHUNK), MIX_W + gn:MIX_W + 2 * gn].astype(bf16)
            ax = ax_scr[d, pl.ds(r0, CHUNK), :]
            decay = jnp.exp(jnp.where(valid[d], ax - arow_scr[d, cidx], NEG))
            bexp = jnp.concatenate([bm] * HEADS, axis=0)
            bexp = jnp.where(key_mask, bexp, jnp.zeros_like(bexp))
            scores = _mm_nt(cm, bexp) * decay
            xdt = xs * dx_scr[d, pl.ds(r0, CHUNK), :]
            y = _mm(scores, _block_diag(xdt.astype(bf16), HEADS)) + _mm(cm, ht_scr[d]) * jnp.exp(ax)
            y_scr = yf_scr if d == 0 else yb_scr
            y_scr[pl.ds(r0, CHUNK), :] = y
            al = ax[last:last + 1, :]
            upd = jnp.where(state_mask, _mm_tn(bm, xdt * jnp.exp(al - ax)), 0.0)
            ht_scr[d] = ht_scr[d] * jnp.exp(al) + upd
        return carry

    lax.fori_loop(0, nc, step, 0)
    for d in range(2):
        for h in range(HEADS):
            g0 = (h // rep) * SSM_STATE
            hs_ref[d, h] = ht_scr[d, g0:g0 + SSM_STATE, h * HEAD_W:(h + 1) * HEAD_W].T

    def fin(i, carry):
        r0 = pl.multiple_of(i * blk, blk)
        y = yf_scr[pl.ds(r0, blk), :] + yb_scr[pl.ds(r0, blk), :] + dskip_ref[...] * xc_scr[pl.ds(r0, blk), 0:MIX_W]
        y = y * jax.nn.silu(z_ref[pl.ds(r0, blk), :].astype(f32))
        out_ref[pl.ds(r0, blk), :] = y * lax.rsqrt(jnp.mean(y * y, axis=-1, keepdims=True) + EPS) * nw_ref[...]
        return carry

    lax.fori_loop(0, nblk, fin, 0)


def _ssd(u, g, nb, seq, cw, cb_row, dtb_row, alog_row, dskip_row, nw_row, state):
    has_state = state is not None
    nc = seq // CHUNK
    const = lambda shape: pl.BlockSpec(shape, lambda b: (0,) * len(shape))
    in_specs = [pl.BlockSpec((seq, MIX_W), lambda b: (b, COL_Z // MIX_W)),
                pl.BlockSpec((seq, SSM_CONV_CH), lambda b: (b, COL_XBC // SSM_CONV_CH)),
                pl.BlockSpec((seq, LANE), lambda b: (b, 0)),
                const((SSM_CONV_K, SSM_CONV_CH)), const((1, SSM_CONV_CH)), const((1, LANE)), const((1, LANE)),
                const((1, MIX_W)), const((1, MIX_W))]
    args = [u, u, g, cw, cb_row, dtb_row, alog_row, dskip_row, nw_row]
    if has_state:
        in_specs.append(pl.BlockSpec((None, 2, HEADS, HEAD_W, SSM_STATE), lambda b: (b, 0, 0, 0, 0)))
        args.append(state)
    return pl.pallas_call(
        functools.partial(_ssd_body, seq, has_state),
        grid=(nb,),
        in_specs=in_specs,
        out_specs=[pl.BlockSpec((seq, MIX_W), lambda b: (b, 0)),
                   pl.BlockSpec((None, 2, HEADS, HEAD_W, SSM_STATE), lambda b: (b, 0, 0, 0, 0))],
        out_shape=[jax.ShapeDtypeStruct((nb * seq, MIX_W), f32),
                   jax.ShapeDtypeStruct((nb, 2, HEADS, HEAD_W, SSM_STATE), f32)],
        scratch_shapes=[pltpu.VMEM((seq + 16, SSM_CONV_CH), f32), pltpu.VMEM((seq, SSM_CONV_CH), f32),
                        pltpu.VMEM((2, seq, MIX_W), f32), pltpu.VMEM((2, seq, MIX_W), f32),
                        pltpu.VMEM((2, nc, 1, MIX_W), f32),
                        pltpu.VMEM((2, SSM_GROUPS * SSM_STATE, MIX_W), f32),
                        pltpu.VMEM((seq, MIX_W), f32), pltpu.VMEM((seq, MIX_W), f32)],
        compiler_params=_params(("arbitrary",)),
        name="ssd_state" if has_state else "ssd",
    )(*args)


def _softmax_pv(scores, values, hl):
    m = scores[0].max(axis=1, keepdims=True)
    for s in scores[1:]:
        m = jnp.maximum(m, s.max(axis=1, keepdims=True))
    acc, den = None, None
    for s, v in zip(scores, values):
        p = jnp.exp(s - m)
        d = jnp.sum(p, axis=1, keepdims=True)
        o = _mm(p, v[:, hl])
        acc = o if acc is None else acc + o
        den = d if den is None else den + d
    return acc / den


def _na_prompt_body(q_ref, k_ref, v_ref, qn_ref, kn_ref, o_ref, ko_ref, vo_ref):
    scale = HEAD_W ** -0.5
    qn = _seg_rms(q_ref[...].astype(f32), HEAD_W, qn_ref[...]) * scale
    kn = _seg_rms(k_ref[...].astype(f32), HEAD_W, kn_ref[...])
    v = v_ref[...].astype(f32)
    ko_ref[...] = kn
    vo_ref[...] = v
    for h in range(HEADS):
        hl = slice(h * HEAD_W, (h + 1) * HEAD_W)
        s = _mm_nt(qn[:, hl], kn[:, hl])
        o_ref[:, hl] = _softmax_pv([s], [v], hl)


def _na_prompt(u, nb, seq, qn_row, kn_row):
    cb = COL_NA // MIX_W
    blk = lambda j: pl.BlockSpec((seq, MIX_W), lambda b: (b, j))
    const = pl.BlockSpec((1, MIX_W), lambda b: (0, 0))
    sds = jax.ShapeDtypeStruct((nb * seq, MIX_W), f32)
    return pl.pallas_call(
        _na_prompt_body,
        grid=(nb,),
        in_specs=[blk(cb), blk(cb + 1), blk(cb + 2), const, const],
        out_specs=[blk(0), blk(0), blk(0)],
        out_shape=[sds, sds, sds],
        compiler_params=_params(("arbitrary",)),
        name="na_prompt",
    )(u, u, u, qn_row, kn_row)


def _na_sample_body(q_ref, k_ref, v_ref, kc_ref, vc_ref, bias_ref, qn_ref, kn_ref, o_ref, kn_scr, v_scr):
    i = pl.program_id(1)

    @pl.when(i == 0)
    def _():
        kn_scr[...] = _seg_rms(k_ref[...].astype(f32), HEAD_W, kn_ref[...]).astype(bf16)
        v_scr[...] = v_ref[...].astype(bf16)

    rows = DEC_SEQ // GRID_W
    rs = jnp.clip(i - NA_WIN_R // 2, 0, rows - NA_WIN_R)
    d0 = rs - i + (NA_WIN_R - 1)
    k0 = pl.multiple_of(rs * GRID_W, GRID_W)
    nwin = NA_WIN_R * GRID_W
    qn = _seg_rms(q_ref[...].astype(f32), HEAD_W, qn_ref[...]) * (HEAD_W ** -0.5)
    kwin = kn_scr[pl.ds(k0, nwin), :]
    vwin = v_scr[pl.ds(k0, nwin), :]
    kctx = kc_ref[...]
    vctx = vc_ref[...]
    for h in range(HEADS):
        hl = slice(h * HEAD_W, (h + 1) * HEAD_W)
        s_win = _mm_nt(qn[:, hl], kwin[:, hl]) + bias_ref[h, d0]
        s_ctx = _mm_nt(qn[:, hl], kctx[:, hl])
        o_ref[:, hl] = _softmax_pv([s_win, s_ctx], [vwin, vctx], hl)


def _na_sample(u, nb, layer, cache_k, cache_v, bias, qn_row, kn_row):
    seq = DEC_SEQ
    rows = seq // GRID_W
    cb = COL_NA // MIX_W
    full = lambda j: pl.BlockSpec((seq, MIX_W), lambda b, i: (b, j))
    ctx = pl.BlockSpec((None, None, PAST_LEN, MIX_W), lambda b, i: (b, layer, 0, 0))
    const = pl.BlockSpec((1, MIX_W), lambda b, i: (0, 0))
    return pl.pallas_call(
        _na_sample_body,
        grid=(nb, rows),
        in_specs=[pl.BlockSpec((GRID_W, MIX_W), lambda b, i: (b * rows + i, cb)), full(cb + 1), full(cb + 2),
                  ctx, ctx,
                  pl.BlockSpec((HEADS, NA_WIN_R, GRID_W, NA_WIN_R * GRID_W), lambda b, i: (0, 0, 0, 0)),
                  const, const],
        out_specs=pl.BlockSpec((GRID_W, MIX_W), lambda b, i: (b * rows + i, 0)),
        out_shape=jax.ShapeDtypeStruct((nb * seq, MIX_W), f32),
        scratch_shapes=[pltpu.VMEM((seq, MIX_W), bf16), pltpu.VMEM((seq, MIX_W), bf16)],
        compiler_params=_params(("arbitrary", "arbitrary")),
        name="na_sample",
    )(u, u, u, cache_k, cache_v, bias, qn_row, kn_row)


def _na_bias_table(rpb):
    j = jnp.arange(GRID_W)
    cs = jnp.clip(j - NA_WIN_C // 2, 0, GRID_W - NA_WIN_C)
    ok = (j[None, :] >= cs[:, None]) & (j[None, :] < cs[:, None] + NA_WIN_C)
    wide = jnp.pad(rpb.astype(f32), ((0, 0), (0, 0), (GRID_W, GRID_W)))
    toe = jnp.stack([wide[:, :, GRID_W + NA_WIN_C - 1 - jj:2 * GRID_W + NA_WIN_C - 1 - jj] for jj in range(GRID_W)],
                    axis=2)
    toe = jnp.where(ok[None, None], toe, NEG)
    slabs = jnp.stack([toe[:, d0:d0 + NA_WIN_R] for d0 in range(NA_WIN_R)], axis=1)
    return jnp.transpose(slabs, (0, 1, 3, 2, 4)).reshape(HEADS, NA_WIN_R, GRID_W, NA_WIN_R * GRID_W)


def _diff_lambda(lam_ref, lam_init):
    lv = lam_ref[...]
    s1 = jnp.sum(lv[0:1, :] * lv[1:2, :], axis=1, keepdims=True)
    s2 = jnp.sum(lv[2:3, :] * lv[3:4, :], axis=1, keepdims=True)
    return jnp.exp(s1) - jnp.exp(s2) + lam_init


def _diff_head(q, keys, vals, lam, h):
    hl = slice(h * HEAD_W, (h + 1) * HEAD_W)
    acc = None
    for m in range(2):
        ml = slice(h * HEAD_W + m * DIFF_QK_DIM, h * HEAD_W + (m + 1) * DIFF_QK_DIM)
        scores = [_mm_nt(q[:, ml], k[:, ml]) for k in keys]
        mx = scores[0].max(axis=1, keepdims=True)
        for s in scores[1:]:
            mx = jnp.maximum(mx, s.max(axis=1, keepdims=True))
        ps = [jnp.exp(s - mx) for s in scores]
        den = ps[0].sum(axis=1, keepdims=True)
        for p in ps[1:]:
            den = den + p.sum(axis=1, keepdims=True)
        coef = (1.0 / den) if m == 0 else (-lam / den)
        ps = [p * coef for p in ps]
        acc = ps if acc is None else [a + p for a, p in zip(acc, ps)]
    out = None
    for a, v in zip(acc, vals):
        o = _mm(a, v[:, hl])
        out = o if out is None else out + o
    return out


def _diff_prompt_body(lam_init, q_ref, k_ref, v_ref, qn_ref, kn_ref, lam_ref, nw_ref, o_ref, ko_ref, vo_ref):
    lam = _diff_lambda(lam_ref, lam_init)
    qn = _seg_rms(q_ref[...].astype(f32), DIFF_QK_DIM, qn_ref[...]) * (DIFF_QK_DIM ** -0.5)
    kn = _seg_rms(k_ref[...].astype(f32), DIFF_QK_DIM, kn_ref[...])
    v = v_ref[...].astype(f32)
    ko_ref[...] = kn
    vo_ref[...] = v
    o = jnp.concatenate([_diff_head(qn, [kn], [v], lam, h) for h in range(HEADS)], axis=1)
    o_ref[...] = _seg_rms(o, HEAD_W, nw_ref[...]) * (1.0 - lam_init)


def _diff_prompt(u, nb, seq, lam_init, qn_row, kn_row, lam_vecs, nw_row):
    cb = COL_DIFF // MIX_W
    blk = lambda j: pl.BlockSpec((seq, MIX_W), lambda b: (b, j))
    const = pl.BlockSpec((1, MIX_W), lambda b: (0, 0))
    sds = jax.ShapeDtypeStruct((nb * seq, MIX_W), f32)
    return pl.pallas_call(
        functools.partial(_diff_prompt_body, lam_init),
        grid=(nb,),
        in_specs=[blk(cb), blk(cb + 1), blk(cb + 2), const, const,
                  pl.BlockSpec((4, DIFF_QK_DIM), lambda b: (0, 0)), const],
        out_specs=[blk(0), blk(0), blk(0)],
        out_shape=[sds, sds, sds],
        compiler_params=_params(("arbitrary",)),
        name="diff_prompt",
    )(u, u, u, qn_row, kn_row, lam_vecs, nw_row)


def _rope(x, cos, sin):
    w = x.shape[-1]
    up = pltpu.roll(x, w - 8, 1)
    dn = pltpu.roll(x, 8, 1)
    first = (lax.broadcasted_iota(jnp.int32, x.shape, 1) % 16) < 8
    return x * cos + jnp.where(first, up, dn) * sin


def _diff_sample_body(lam_init, qblk, q_ref, k_ref, v_ref, kc_ref, vc_ref, cos_ref, sin_ref, qn_ref, kn_ref,
                      lam_ref, nw_ref, o_ref, kn_scr, v_scr):
    i = pl.program_id(1)

    @pl.when(i == 0)
    def _():
        kn = _seg_rms(k_ref[...].astype(f32), DIFF_QK_DIM, kn_ref[...])
        kn_scr[...] = _rope(kn, cos_ref[...], sin_ref[...]).astype(bf16)
        v_scr[...] = v_ref[...].astype(bf16)

    lam = _diff_lambda(lam_ref, lam_init)
    r0 = pl.multiple_of(i * qblk, qblk)
    qn = _seg_rms(q_ref[...].astype(f32), DIFF_QK_DIM, qn_ref[...])
    qn = _rope(qn, cos_ref[pl.ds(r0, qblk), :], sin_ref[pl.ds(r0, qblk), :]) * (DIFF_QK_DIM ** -0.5)
    keys = [kn_scr[...], kc_ref[...]]
    vals = [v_scr[...], vc_ref[...]]
    o = jnp.concatenate([_diff_head(qn, keys, vals, lam, h) for h in range(HEADS)], axis=1)
    o_ref[...] = _seg_rms(o, HEAD_W, nw_ref[...]) * (1.0 - lam_init)


def _diff_sample(u, nb, layer, lam_init, cache_k, cache_v, cos, sin, qn_row, kn_row, lam_vecs, nw_row):
    seq = DEC_SEQ
    qblk = 128
    nq = seq // qblk
    cb = COL_DIFF // MIX_W
    full = lambda j: pl.BlockSpec((seq, MIX_W), lambda b, i: (b, j))
    ctx = pl.BlockSpec((None, None, PAST_LEN, MIX_W), lambda b, i: (b, layer, 0, 0))
    const = pl.BlockSpec((1, MIX_W), lambda b, i: (0, 0))
    tab = pl.BlockSpec((seq, MIX_W), lambda b, i: (0, 0))
    return pl.pallas_call(
        functools.partial(_diff_sample_body, lam_init, qblk),
        grid=(nb, nq),
        in_specs=[pl.BlockSpec((qblk, MIX_W), lambda b, i: (b * nq + i, cb)), full(cb + 1), full(cb + 2),
                  ctx, ctx, tab, tab, const, const,
                  pl.BlockSpec((4, DIFF_QK_DIM), lambda b, i: (0, 0)), const],
        out_specs=pl.BlockSpec((qblk, MIX_W), lambda b, i: (b * nq + i, 0)),
        out_shape=jax.ShapeDtypeStruct((nb * seq, MIX_W), f32),
        scratch_shapes=[pltpu.VMEM((seq, MIX_W), bf16), pltpu.VMEM((seq, MIX_W), bf16)],
        compiler_params=_params(("arbitrary", "arbitrary")),
        name="diff_sample",
    )(u, u, u, cache_k, cache_v, cos, sin, qn_row, kn_row, lam_vecs, nw_row)


def _rope_tables():
    t = jnp.arange(DEC_SEQ)
    nf = DIFF_QK_DIM // 4
    inv = ROPE_BASE ** (-jnp.arange(nf, dtype=f32) / nf)
    ar = (t // GRID_W).astype(f32)[:, None] * inv
    ac = (t % GRID_W).astype(f32)[:, None] * inv
    cos = jnp.concatenate([jnp.cos(ar), jnp.cos(ar), jnp.cos(ac), jnp.cos(ac)], axis=1)
    sin = jnp.concatenate([-jnp.sin(ar), jnp.sin(ar), -jnp.sin(ac), jnp.sin(ac)], axis=1)
    reps = MIX_W // DIFF_QK_DIM
    return jnp.tile(cos, (1, reps)), jnp.tile(sin, (1, reps))


def _merge_body(x_ref, a_ref, b_ref, c_ref, d_ref, gp_ref, mod_ref, wb_ref, wo_ref, nw_ref, wr_ref,
                xo_ref, h2_ref, aff_ref):
    mixed = None
    for n, br in enumerate((a_ref, b_ref, c_ref, d_ref)):
        gate = jax.nn.sigmoid(gp_ref[:, n * D_MODEL:(n + 1) * D_MODEL].astype(f32))
        t = gate * jnp.dot(br[...].astype(bf16), wb_ref[n], preferred_element_type=f32)
        mixed = t if mixed is None else mixed + t
    y = jnp.dot(mixed.astype(bf16), wo_ref[...], preferred_element_type=f32)
    x = x_ref[...] + mod_ref[:, 2 * D_MODEL:3 * D_MODEL] * y
    xo_ref[...] = x
    h = x * lax.rsqrt(jnp.mean(x * x, axis=-1, keepdims=True) + EPS) * nw_ref[...]
    h = h * (1.0 + mod_ref[:, 4 * D_MODEL:5 * D_MODEL]) + mod_ref[:, 3 * D_MODEL:4 * D_MODEL]
    hb = h.astype(bf16)
    h2_ref[...] = hb
    logits = jnp.dot(hb, wr_ref[...], preferred_element_type=f32)
    lane = lax.broadcasted_iota(jnp.int32, logits.shape, 1)
    logits = jnp.where(lane < N_EXPERTS, logits, NEG)
    e = jnp.exp(logits - logits.max(axis=1, keepdims=True))
    aff_ref[...] = e / e.sum(axis=1, keepdims=True)


def _merge(x, brs, u, mod3, mod_base, mod_stride, seq, wb, wo, nw_row, wr):
    n = x.shape[0]
    tm = 256
    tps = seq // tm
    row = lambda w: pl.BlockSpec((tm, w), lambda i: (i, 0))
    return pl.pallas_call(
        _merge_body,
        grid=(n // tm,),
        in_specs=[row(D_MODEL), row(MIX_W), row(MIX_W), row(MIX_W), row(MIX_W),
                  pl.BlockSpec((tm, N_BRANCH * D_MODEL), lambda i: (i, COL_GPRE // (N_BRANCH * D_MODEL))),
                  pl.BlockSpec((None, 1, 6 * D_MODEL), lambda i: (mod_base + mod_stride * (i // tps), 0, 0)),
                  pl.BlockSpec((N_BRANCH, MIX_W, D_MODEL), lambda i: (0, 0, 0)),
                  pl.BlockSpec((D_MODEL, D_MODEL), lambda i: (0, 0)),
                  pl.BlockSpec((1, D_MODEL), lambda i: (0, 0)),
                  pl.BlockSpec((D_MODEL, LANE), lambda i: (0, 0))],
        out_specs=[row(D_MODEL), row(D_MODEL), row(LANE)],
        out_shape=[jax.ShapeDtypeStruct((n, D_MODEL), f32), jax.ShapeDtypeStruct((n, D_MODEL), bf16),
                   jax.ShapeDtypeStruct((n, LANE), f32)],
        compiler_params=_params(("arbitrary",)),
        name="merge",
    )(x, *brs, u, mod3, wb, wo, nw_row, wr)


def _route_body(t, cap, aff_ref, h2_ref, xg_ref, g_ref, rank_ref, rrow_scr):
    aff = aff_ref[...]
    afft = aff.T
    sub = lax.broadcasted_iota(jnp.int32, (t, t), 0)
    lan = lax.broadcasted_iota(jnp.int32, (t, t), 1)
    earlier = sub < lan
    rrow_scr[...] = jnp.zeros(rrow_scr.shape, f32) + float(cap)
    slot = lax.broadcasted_iota(jnp.int32, (cap, t), 0).astype(f32)
    h2 = h2_ref[...]
    for e in range(N_EXPERTS):
        a_col = aff[:, e:e + 1]
        a_row = afft[e:e + 1, :]
        beats = (a_col > a_row) | ((a_col == a_row) & earlier)
        rank = jnp.sum(jnp.where(beats, 1.0, 0.0), axis=0, keepdims=True)
        rank = jnp.minimum(rank, float(cap))
        rrow_scr[e:e + 1, :] = rank
        onehot = jnp.where(rank == slot, 1.0, 0.0)
        xg_ref[e] = jnp.dot(onehot.astype(bf16), h2, preferred_element_type=f32).astype(bf16)
        g_ref[e] = jnp.sum(onehot * a_row, axis=1, keepdims=True)
    rank_ref[...] = rrow_scr[...].T


def _route(aff, h2, nb, t):
    cap = CAPACITY_FACTOR * t // N_EXPERTS
    return pl.pallas_call(
        functools.partial(_route_body, t, cap),
        grid=(nb,),
        in_specs=[pl.BlockSpec((t, LANE), lambda b: (b, 0)), pl.BlockSpec((t, D_MODEL), lambda b: (b, 0))],
        out_specs=[pl.BlockSpec((N_EXPERTS, cap, D_MODEL), lambda b: (0, b, 0)),
                   pl.BlockSpec((N_EXPERTS, cap, 1), lambda b: (0, b, 0)),
                   pl.BlockSpec((t, LANE), lambda b: (b, 0))],
        out_shape=[jax.ShapeDtypeStruct((N_EXPERTS, nb * cap, D_MODEL), bf16),
                   jax.ShapeDtypeStruct((N_EXPERTS, nb * cap, 1), f32),
                   jax.ShapeDtypeStruct((nb * t, LANE), f32)],
        scratch_shapes=[pltpu.VMEM((LANE, t), f32)],
        compiler_params=_params(("arbitrary",)),
        name=f"route_{t}",
    )(aff, h2)


def _ffn_body(xp_ref, xs_ref, gp_ref, gs_ref, wg_ref, wu_ref, wd_ref, yp_ref, ys_ref, wg_scr, wu_scr, wd_scr):
    wg_scr[...] = wg_ref[...].astype(bf16)
    wu_scr[...] = wu_ref[...].astype(bf16)
    wd_scr[...] = wd_ref[...].astype(bf16)
    blk = 256
    for x_ref, g_ref, y_ref in ((xp_ref, gp_ref, yp_ref), (xs_ref, gs_ref, ys_ref)):
        def step(i, carry, x_ref=x_ref, g_ref=g_ref, y_ref=y_ref):
            r0 = pl.multiple_of(i * blk, blk)
            x = x_ref[pl.ds(r0, blk), :]
            a = jnp.dot(x, wg_scr[...], preferred_element_type=f32)
            up = jnp.dot(x, wu_scr[...], preferred_element_type=f32)
            mid = (jax.nn.silu(a) * up).astype(bf16)
            y = jnp.dot(mid, wd_scr[...], preferred_element_type=f32)
            y_ref[pl.ds(r0, blk), :] = (y * g_ref[pl.ds(r0, blk), :]).astype(bf16)
            return carry
        lax.fori_loop(0, x_ref.shape[0] // blk, step, 0)


def _ffn(xp, xs, gp, gs, w_gate, w_up, w_down, layer):
    np_, ns = xp.shape[1], xs.shape[1]
    tok = lambda n, w: pl.BlockSpec((None, n, w), lambda e: (e, 0, 0))
    wspec = lambda a, b: pl.BlockSpec((None, None, a, b), lambda e: (layer, e, 0, 0))
    return pl.pallas_call(
        _ffn_body,
        grid=(N_EXPERTS,),
        in_specs=[tok(np_, D_MODEL), tok(ns, D_MODEL), tok(np_, 1), tok(ns, 1),
                  wspec(D_MODEL, EXPERT_FF), wspec(D_MODEL, EXPERT_FF), wspec(EXPERT_FF, D_MODEL)],
        out_specs=[tok(np_, D_MODEL), tok(ns, D_MODEL)],
        out_shape=[jax.ShapeDtypeStruct(xp.shape, bf16), jax.ShapeDtypeStruct(xs.shape, bf16)],
        scratch_shapes=[pltpu.VMEM((D_MODEL, EXPERT_FF), bf16), pltpu.VMEM((D_MODEL, EXPERT_FF), bf16),
                        pltpu.VMEM((EXPERT_FF, D_MODEL), bf16)],
        compiler_params=_params(("arbitrary",)),
        name="expert_ffn",
    )(xp, xs, gp, gs, w_gate, w_up, w_down)


def _combine_body(t, cap, x_ref, rank_ref, y_ref, mod_ref, o_ref):
    k = N_EXPERTS * cap
    er = lax.broadcasted_iota(jnp.int32, (LANE, k), 0)
    ec = lax.broadcasted_iota(jnp.int32, (LANE, k), 1) // cap
    expand = jnp.where(er == ec, 1.0, 0.0).astype(bf16)
    rexp = jnp.dot(rank_ref[...].astype(bf16), expand, preferred_element_type=f32)
    slot = (lax.broadcasted_iota(jnp.int32, (t, k), 1) % cap).astype(f32)
    pt = jnp.where(rexp == slot, 1.0, 0.0).astype(bf16)
    moe = jnp.dot(pt, y_ref[...].reshape(k, D_MODEL), preferred_element_type=f32)
    o_ref[...] = x_ref[...] + mod_ref[:, 5 * D_MODEL:6 * D_MODEL] * moe


def _combine(x, rank, y, mod3, mod_base, mod_stride, nb, t):
    cap = CAPACITY_FACTOR * t // N_EXPERTS
    return pl.pallas_call(
        functools.partial(_combine_body, t, cap),
        grid=(nb,),
        in_specs=[pl.BlockSpec((t, D_MODEL), lambda b: (b, 0)),
                  pl.BlockSpec((t, LANE), lambda b: (b, 0)),
                  pl.BlockSpec((N_EXPERTS, cap, D_MODEL), lambda b: (0, b, 0)),
                  pl.BlockSpec((None, 1, 6 * D_MODEL), lambda b: (mod_base + mod_stride * b, 0, 0))],
        out_specs=pl.BlockSpec((t, D_MODEL), lambda b: (b, 0)),
        out_shape=jax.ShapeDtypeStruct(x.shape, f32),
        compiler_params=_params(("arbitrary",)),
        name=f"combine_{t}",
    )(x, rank, y, mod3)


def _pad_row(v, off, width=LANE):
    return jnp.zeros((1, width), f32).at[0, off:off + v.shape[0]].set(v.astype(f32))


def _layer_params(l, a):
    w_in = a["w_in"][l]
    pts, acc = [], 0
    splits = (MIX_W,) * 4 + (4 * HEADS,) + (MIX_W,) * 7 + (SSM_CONV_CH, 2 * HEADS, N_BRANCH * D_MODEL)
    for s in splits:
        pts.append((acc, acc + s))
        acc += s
    seg = lambda i: w_in[:, pts[i][0]:pts[i][1]]
    order = [14, 0, 1, 2, 3, 5, 6, 7, 8, 9, 10, 12, 11]
    w_main = jnp.concatenate([seg(i) for i in order] + [jnp.zeros((D_MODEL, U_COLS - U_USED), f32)],
                             axis=1).astype(bf16)
    w_side = jnp.concatenate([seg(4), seg(13), jnp.zeros((D_MODEL, LANE - 24), f32)], axis=1).astype(bf16)
    tile = lambda v, reps: jnp.tile(v.astype(f32), reps).reshape(1, -1)
    p = dict(
        w_main=w_main, w_side=w_side,
        norm1=a["norm1_w"][l].reshape(1, -1), norm2=a["norm2_w"][l].reshape(1, -1),
        mlstm_gb=_pad_row(a["mlstm_gate_b"][l].reshape(-1), G_MLSTM),
        mlstm_nw=a["mlstm_norm_w"][l].reshape(1, -1),
        na_qn=tile(a["na_qnorm_w"][l], HEADS), na_kn=tile(a["na_knorm_w"][l], HEADS),
        na_bias=_na_bias_table(a["na_rpb"][l]),
        diff_qn=tile(a["diff_qnorm_w"][l], 2 * HEADS), diff_kn=tile(a["diff_knorm_w"][l], 2 * HEADS),
        diff_lam=a["diff_lambda"][l], diff_nw=tile(a["diff_norm_w"][l], HEADS),
        ssm_cw=a["ssm_conv_w"][l], ssm_cb=a["ssm_conv_b"][l].reshape(1, -1),
        ssm_dtb=_pad_row(a["ssm_dt_bias"][l].reshape(-1), G_DT),
        ssm_alog=_pad_row(a["ssm_a_log"][l].reshape(-1), G_DT),
        ssm_d=jnp.repeat(a["ssm_d"][l].astype(f32), HEAD_W).reshape(1, -1),
        ssm_nw=a["ssm_norm_w"][l].reshape(1, -1),
        wb=a["w_branch"][l].astype(bf16), wo=a["w_out"][l].astype(bf16),
        wr=jnp.concatenate([a["w_router"][l], jnp.zeros((D_MODEL, LANE - N_EXPERTS), f32)], axis=1).astype(bf16),
        w_gate=a["w_gate"], w_up=a["w_up"], w_down=a["w_down"],
    )
    return p


def _mixers(u, g, nb, seq, p, l, lam_init, ctx, tables):
    if ctx is None:
        a_out, m_c, m_n, m_m = _mlstm(u, g, nb, seq, p["mlstm_gb"], p["mlstm_nw"], None)
        b_out, na_k, na_v = _na_prompt(u, nb, seq, p["na_qn"], p["na_kn"])
        c_out, df_k, df_v = _diff_prompt(u, nb, seq, lam_init, p["diff_qn"], p["diff_kn"], p["diff_lam"], p["diff_nw"])
        d_out, s_h = _ssd(u, g, nb, seq, p["ssm_cw"], p["ssm_cb"], p["ssm_dtb"], p["ssm_alog"], p["ssm_d"],
                          p["ssm_nw"], None)
        return (a_out, b_out, c_out, d_out), (na_k, na_v, df_k, df_v, m_c, m_n, m_m, s_h)
    (na_ck, na_cv, df_ck, df_cv, st_c, st_n, st_m, st_s) = ctx
    cos, sin = tables
    a_out = _mlstm(u, g, nb, seq, p["mlstm_gb"], p["mlstm_nw"], (st_c[:, l], st_n[:, l], st_m[:, l]))[0]
    b_out = _na_sample(u, nb, l, na_ck, na_cv, p["na_bias"], p["na_qn"], p["na_kn"])
    c_out = _diff_sample(u, nb, l, lam_init, df_ck, df_cv, cos, sin, p["diff_qn"], p["diff_kn"], p["diff_lam"],
                         p["diff_nw"])
    d_out = _ssd(u, g, nb, seq, p["ssm_cw"], p["ssm_cb"], p["ssm_dtb"], p["ssm_alog"], p["ssm_d"], p["ssm_nw"],
                 st_s[:, l])[0]
    return (a_out, b_out, c_out, d_out), None


def kernel(x_prompt, x_sample, cache_na_k, cache_na_v, cache_diff_k, cache_diff_v, state_mlstm_C, state_mlstm_n, state_mlstm_m, state_ssm, c, c_ctx, norm1_w, norm2_w, w_ada, b_ada, w_in, mlstm_gate_b, mlstm_norm_w, na_qnorm_w, na_knorm_w, na_rpb, diff_qnorm_w, diff_knorm_w, diff_lambda, diff_norm_w, ssm_conv_w, ssm_conv_b, ssm_dt_bias, ssm_a_log, ssm_d, ssm_norm_w, w_branch, w_out, w_router, w_gate, w_up, w_down):
    a = dict(norm1_w=norm1_w, norm2_w=norm2_w, w_in=w_in, mlstm_gate_b=mlstm_gate_b, mlstm_norm_w=mlstm_norm_w,
             na_qnorm_w=na_qnorm_w, na_knorm_w=na_knorm_w, na_rpb=na_rpb, diff_qnorm_w=diff_qnorm_w,
             diff_knorm_w=diff_knorm_w, diff_lambda=diff_lambda, diff_norm_w=diff_norm_w, ssm_conv_w=ssm_conv_w,
             ssm_conv_b=ssm_conv_b, ssm_dt_bias=ssm_dt_bias, ssm_a_log=ssm_a_log, ssm_d=ssm_d,
             ssm_norm_w=ssm_norm_w, w_branch=w_branch, w_out=w_out, w_router=w_router, w_gate=w_gate, w_up=w_up,
             w_down=w_down)
    nb_p, nb_s = x_prompt.shape[0], x_sample.shape[0]
    xp = x_prompt.reshape(nb_p * SEQ, D_MODEL)
    xs = x_sample.reshape(nb_s * DEC_SEQ, D_MODEL)
    cc = jnp.concatenate([c_ctx[None, :], c, jnp.zeros((8 - 1 - nb_s, D_MODEL), f32)], axis=0)
    ctx = (cache_na_k.reshape(nb_s, DEPTH, PAST_LEN, MIX_W), cache_na_v.reshape(nb_s, DEPTH, PAST_LEN, MIX_W),
           cache_diff_k.reshape(nb_s, DEPTH, PAST_LEN, MIX_W), cache_diff_v.reshape(nb_s, DEPTH, PAST_LEN, MIX_W),
           state_mlstm_C, state_mlstm_n, state_mlstm_m, state_ssm)
    tables = _rope_tables()
    new = []
    for l in range(DEPTH):
        p = _layer_params(l, a)
        lam_init = 0.8 - 0.6 * math.exp(-0.3 * l)
        mod3 = _modulation(cc, w_ada, b_ada, l).reshape(8, 1, 6 * D_MODEL)
        groups = []
        for (x, nb, seq, base, stride, gctx) in ((xp, nb_p, SEQ, 0, 0, None), (xs, nb_s, DEC_SEQ, 1, 1, ctx)):
            u, g = _inproj(x, mod3, base, stride, seq, p["norm1"], p["w_main"], p["w_side"])
            brs, st = _mixers(u, g, nb, seq, p, l, lam_init, gctx, tables)
            if st is not None:
                new.append(st)
            x1, h2, aff = _merge(x, brs, u, mod3, base, stride, seq, p["wb"], p["wo"], p["norm2"], p["wr"])
            xg, gg, rank = _route(aff, h2, nb, seq)
            groups.append((x1, xg, gg, rank, nb, seq, base, stride))
        yp, ys = _ffn(groups[0][1], groups[1][1], groups[0][2], groups[1][2], p["w_gate"], p["w_up"], p["w_down"], l)
        outs = []
        for (x1, _, _, rank, nb, seq, base, stride), y in zip(groups, (yp, ys)):
            outs.append(_combine(x1, rank, y, mod3, base, stride, nb, seq))
        xp, xs = outs
    stack = lambda i, shape: jnp.stack([new[l][i].reshape(shape) for l in range(DEPTH)], axis=1)
    kv = (nb_p, SEQ, HEADS, HEAD_W)
    return (xp.reshape(x_prompt.shape), xs.reshape(x_sample.shape),
            stack(0, kv), stack(1, kv), stack(2, kv), stack(3, kv),
            stack(4, (nb_p, 2, HEADS, HEAD_W, HEAD_W)), stack(5, (nb_p, 2, HEADS, HEAD_W)),
            stack(6, (nb_p, 2, HEADS)), stack(7, (nb_p, 2, HEADS, HEAD_W, SSM_STATE)))
```

```python
import functools
import math

import jax
import jax.numpy as jnp
from jax import lax
from jax.experimental import pallas as pl
from jax.experimental.pallas import tpu as pltpu

f32 = jnp.float32
bf16 = jnp.bfloat16

D_MODEL = 1024
BATCH = 32
SEQ = 256
DEPTH = 2
DEC_BATCH = 4
DEC_SEQ = 1024
PAST_LEN = 512
GRID_W = 64
MIX_W = D_MODEL // 4
N_BRANCH = 4
HEADS = 4
HEAD_W = MIX_W // HEADS
NA_WIN_R = 8
NA_WIN_C = 16
DIFF_QK_DIM = HEAD_W // 2
SSM_GROUPS = 2
SSM_STATE = 64
SSM_CONV_K = 5
SSM_CONV_CH = MIX_W + 2 * SSM_GROUPS * SSM_STATE
CHUNK = 64
N_EXPERTS = 16
EXPERT_FF = 1024
CAPACITY_FACTOR = 2
ROPE_BASE = 10000.0
EPS = 1e-6
NEG = -1e30

LANE = 128
VMEM_LIMIT = 56 * 1024 * 1024

COL_GPRE = 0
COL_MLSTM = 4096
COL_NA = 5120
COL_DIFF = 5888
COL_XBC = 6656
COL_Z = 7168
U_USED = 7424
U_COLS = 7680
G_MLSTM = 0
G_DT = 16


def _params(sem):
    return pltpu.CompilerParams(dimension_semantics=sem, vmem_limit_bytes=VMEM_LIMIT)


def _mm(a, b):
    return jnp.dot(a.astype(bf16), b.astype(bf16), preferred_element_type=f32)


def _mm_nt(a, b):
    return lax.dot_general(a.astype(bf16), b.astype(bf16), (((1,), (1,)), ((), ())), preferred_element_type=f32)


def _mm_tn(a, b):
    return lax.dot_general(a.astype(bf16), b.astype(bf16), (((0,), (0,)), ((), ())), preferred_element_type=f32)


def _seg_sum(x, seg):
    w = x.shape[-1]
    r = lax.broadcasted_iota(jnp.int32, (w, w), 0) // seg
    c = lax.broadcasted_iota(jnp.int32, (w, w), 1) // seg
    ones = jnp.where(r == c, 1.0, 0.0).astype(bf16)
    hi = x.astype(bf16)
    lo = (x - hi.astype(f32)).astype(bf16)
    return jnp.dot(hi, ones, preferred_element_type=f32) + jnp.dot(lo, ones, preferred_element_type=f32)


def _seg_rms(x, seg, w_row):
    ms = _seg_sum(x * x, seg) * (1.0 / seg)
    return x * lax.rsqrt(ms + EPS) * w_row


def _mod_body(c_ref, w_ref, b_ref, o_ref):
    o_ref[...] = _mm(jax.nn.silu(c_ref[...]), w_ref[...]) + b_ref[...]


def _modulation(cc, w_ada, b_ada, layer):
    tn = 512
    n = w_ada.shape[2]
    return pl.pallas_call(
        _mod_body,
        grid=(n // tn,),
        in_specs=[pl.BlockSpec((8, D_MODEL), lambda j: (0, 0)),
                  pl.BlockSpec((None, D_MODEL, tn), lambda j: (layer, 0, j)),
                  pl.BlockSpec((None, 1, tn), lambda j: (layer, 0, j))],
        out_specs=pl.BlockSpec((8, tn), lambda j: (0, j)),
        out_shape=jax.ShapeDtypeStruct((8, n), f32),
        compiler_params=_params(("arbitrary",)),
        name="modulation",
    )(cc, w_ada, b_ada.reshape(b_ada.shape[0], 1, n))


def _inproj_body(x_ref, mod_ref, nw_ref, w_ref, wg_ref, u_ref, g_ref, h_scr):
    @pl.when(pl.program_id(1) == 0)
    def _():
        x = x_ref[...]
        y = x * lax.rsqrt(jnp.mean(x * x, axis=-1, keepdims=True) + EPS) * nw_ref[...]
        h = y * (1.0 + mod_ref[:, D_MODEL:2 * D_MODEL]) + mod_ref[:, 0:D_MODEL]
        hb = h.astype(bf16)
        h_scr[...] = hb
        g_ref[...] = jnp.dot(hb, wg_ref[...], preferred_element_type=f32)

    u_ref[...] = jnp.dot(h_scr[...], w_ref[...], preferred_element_type=f32).astype(u_ref.dtype)


def _inproj(x, mod3, mod_base, mod_stride, seq, nw_row, w_main, w_side):
    n = x.shape[0]
    tm, tn = 1024, U_COLS // 4
    tps = seq // tm if seq >= tm else 1
    return pl.pallas_call(
        _inproj_body,
        grid=(n // tm, U_COLS // tn),
        in_specs=[pl.BlockSpec((tm, D_MODEL), lambda i, j: (i, 0)),
                  pl.BlockSpec((None, 1, 6 * D_MODEL), lambda i, j: (mod_base + mod_stride * (i // tps), 0, 0)),
                  pl.BlockSpec((1, D_MODEL), lambda i, j: (0, 0)),
                  pl.BlockSpec((D_MODEL, tn), lambda i, j: (0, j)),
                  pl.BlockSpec((D_MODEL, LANE), lambda i, j: (0, 0))],
        out_specs=[pl.BlockSpec((tm, tn), lambda i, j: (i, j)),
                   pl.BlockSpec((tm, LANE), lambda i, j: (i, 0))],
        out_shape=[jax.ShapeDtypeStruct((n, U_COLS), bf16), jax.ShapeDtypeStruct((n, LANE), f32)],
        scratch_shapes=[pltpu.VMEM((tm, D_MODEL), bf16)],
        compiler_params=_params(("arbitrary", "arbitrary")),
        name="inproj",
    )(x, mod3, nw_row, w_main, w_side)


def _split3(x):
    hi = x.astype(bf16)
    r = x - hi.astype(f32)
    mid = r.astype(bf16)
    lo = (r - mid.astype(f32)).astype(bf16)
    return hi, mid, lo


def _select_cols(x, onehot):
    hi, mid, lo = _split3(x)
    d = lambda p: jnp.dot(p, onehot, preferred_element_type=f32)
    return (d(hi) + d(mid)) + d(lo)


def _select_rows(onehot, x):
    hi, mid, lo = _split3(x)
    d = lambda p: jnp.dot(onehot, p, preferred_element_type=f32)
    return (d(hi) + d(mid)) + d(lo)


def _chunk_scan(x, op, identity, reverse):
    n = x.shape[0]
    pos = lax.broadcasted_iota(jnp.int32, x.shape, 0) % CHUNK
    k = 1
    while k < CHUNK:
        if reverse:
            shifted, ok = pltpu.roll(x, n - k, 0), pos < CHUNK - k
        else:
            shifted, ok = pltpu.roll(x, k, 0), pos >= k
        x = op(x, jnp.where(ok, shifted, identity))
        k *= 2
    return x


def _dir_scan(x, op, identity, bwd_col):
    return jnp.where(bwd_col, _chunk_scan(x, op, identity, True), _chunk_scan(x, op, identity, False))


def _head_expand(col0):
    c = lax.broadcasted_iota(jnp.int32, (LANE, MIX_W), 0)
    h = lax.broadcasted_iota(jnp.int32, (LANE, MIX_W), 1) // HEAD_W
    return jnp.where(c == col0 + h, 1.0, 0.0).astype(bf16)


def _chunk_rows(x, seq):
    nrow = max(seq // CHUNK, 8)
    t = lax.broadcasted_iota(jnp.int32, (seq, MIX_W), 0) % CHUNK
    s_ = lax.broadcasted_iota(jnp.int32, (seq, MIX_W), 1) % HEAD_W
    sel = lax.broadcasted_iota(jnp.int32, (nrow, seq), 1) // CHUNK == lax.broadcasted_iota(jnp.int32, (nrow, seq), 0)
    return _select_rows(jnp.where(sel, 1.0, 0.0).astype(bf16), jnp.where(t == s_, x, 0.0))


def _block_diag(x, reps):
    r, c = x.shape
    t = jnp.concatenate([x] * reps, axis=0)
    rb = lax.broadcasted_iota(jnp.int32, t.shape, 0) // r
    cb = lax.broadcasted_iota(jnp.int32, t.shape, 1) // (c // reps)
    return jnp.where(rb == cb, t, jnp.zeros_like(t))


def _head_mask(rows, cols, rseg, cseg):
    return (lax.broadcasted_iota(jnp.int32, (rows, cols), 0) // rseg) == (lax.broadcasted_iota(jnp.int32, (rows, cols), 1) // cseg)


def _mlstm_body(seq, has_state, *refs):
    if has_state:
        (qkvo_ref, g_ref, gb_ref, nw_ref, c0_ref, n0_ref, m0_ref,
         out_ref, cs_ref, ns_ref, ms_ref, bx_scr, mx_scr, vx_scr, vrow_scr, cbd_scr, hf_scr, hb_scr) = refs
    else:
        (qkvo_ref, g_ref, gb_ref, nw_ref,
         out_ref, cs_ref, ns_ref, ms_ref, bx_scr, mx_scr, vx_scr, vrow_scr, cbd_scr, hf_scr, hb_scr) = refs
    nc = seq // CHUNK
    g = g_ref[...] + gb_ref[...]
    lane = lax.broadcasted_iota(jnp.int32, g.shape, 1)
    bwd_col = (lane % 16) >= 8
    bsum = _dir_scan(jax.nn.log_sigmoid(g), jnp.add, 0.0, bwd_col)
    b_i = pltpu.roll(bsum, LANE - HEADS, 1)
    vcol = g - b_i
    mcol = b_i + _dir_scan(vcol, jnp.maximum, NEG, bwd_col)
    for d in range(2):
        e = _head_expand(G_MLSTM + d * 8)
        bx_scr[d] = _select_cols(b_i, e)
        mx_scr[d] = _select_cols(mcol, e)
        vx = _select_cols(vcol, e)
        vx_scr[d] = vx
        vrow = _chunk_rows(vx, seq)
        for c in range(nc):
            vrow_scr[d, c] = vrow[c:c + 1, :]

    grp = lax.broadcasted_iota(jnp.int32, (1, MIX_W), 1) // HEAD_W
    cbd_scr[...] = jnp.zeros(cbd_scr.shape, f32)
    n_rows, m_rows = [], []
    for d in range(2):
        if has_state:
            for h in range(HEADS):
                cbd_scr[d, h * HEAD_W:(h + 1) * HEAD_W, h * HEAD_W:(h + 1) * HEAD_W] = c0_ref[d, h]
            n_rows.append(jnp.concatenate([n0_ref[d, h:h + 1, :] for h in range(HEADS)], axis=1))
            m_row = jnp.zeros((1, MIX_W), f32)
            for h in range(HEADS):
                m_row = jnp.where(grp == h, m0_ref[d:d + 1, h:h + 1], m_row)
            m_rows.append(m_row)
        else:
            n_rows.append(jnp.zeros((1, MIX_W), f32))
            m_rows.append(jnp.zeros((1, MIX_W), f32))

    li = lax.broadcasted_iota(jnp.int32, (CHUNK, MIX_W), 0)
    si = lax.broadcasted_iota(jnp.int32, (CHUNK, MIX_W), 1) % HEAD_W
    valid = (si <= li, si >= li)
    bd = _head_mask(MIX_W, MIX_W, HEAD_W, HEAD_W)

    def step(c, carry):
        new = []
        for d in range(2):
            n_row, m_row = carry[2 * d], carry[2 * d + 1]
            cidx = c if d == 0 else nc - 1 - c
            r0 = pl.multiple_of(cidx * CHUNK, CHUNK)
            last = CHUNK - 1 if d == 0 else 0
            q = qkvo_ref[pl.ds(r0, CHUNK), 0:MIX_W].astype(f32)
            k = qkvo_ref[pl.ds(r0, CHUNK), MIX_W:2 * MIX_W].astype(f32) * (HEAD_W ** -0.5)
            vb = qkvo_ref[pl.ds(r0, CHUNK), 2 * MIX_W:3 * MIX_W].astype(bf16)
            qb = q.astype(bf16)
            bx = bx_scr[d, pl.ds(r0, CHUNK), :]
            mx = mx_scr[d, pl.ds(r0, CHUNK), :]
            vx = vx_scr[d, pl.ds(r0, CHUNK), :]
            p = jnp.exp(jnp.where(valid[d], bx + vrow_scr[d, cidx] - mx, NEG))
            smat = _mm_nt(qb, _block_diag(k.astype(bf16), HEADS)) * p
            intra = _mm(smat, _block_diag(vb, HEADS))
            dsum = _seg_sum(smat, HEAD_W)
            gg = bx + m_row
            mt = jnp.maximum(gg, mx)
            a = jnp.exp(mx - mt)
            w_prev = jnp.exp(gg - mt)
            inter = _mm(qb, cbd_scr[d])
            dint = _seg_sum(q * n_row, HEAD_W)
            num = intra * a + inter * w_prev
            den = dsum * a + dint * w_prev
            h_scr = hf_scr if d == 0 else hb_scr
            h_scr[pl.ds(r0, CHUNK), :] = num / jnp.maximum(jnp.abs(den), jnp.exp(-mt))
            bl = bx[last:last + 1, :]
            mloc = mx[last:last + 1, :]
            m_new = jnp.maximum(bl + m_row, mloc)
            dec = jnp.exp(bl + m_row - m_new)
            fac = jnp.exp(mloc - m_new)
            kw = k * jnp.exp(bl + vx - mloc)
            kv = jnp.where(bd, _mm_tn(kw, vb), 0.0)
            cbd_scr[d] = cbd_scr[d] * dec + kv * fac
            new += [n_row * dec + jnp.sum(kw, axis=0, keepdims=True) * fac, m_new]
        return tuple(new)

    fin_state = lax.fori_loop(0, nc, step, (n_rows[0], m_rows[0], n_rows[1], m_rows[1]))
    for d in range(2):
        n_row, m_row = fin_state[2 * d], fin_state[2 * d + 1]
        for h in range(HEADS):
            hl = slice(h * HEAD_W, (h + 1) * HEAD_W)
            cs_ref[d, h] = cbd_scr[d, hl, hl]
            ns_ref[d, h:h + 1, :] = n_row[:, hl]
            ms_ref[d:d + 1, h:h + 1] = m_row[:, h * HEAD_W:h * HEAD_W + 1]

    blk = 256

    def fin(i, carry):
        r0 = pl.multiple_of(i * blk, blk)
        hs = hf_scr[pl.ds(r0, blk), :] + hb_scr[pl.ds(r0, blk), :]
        hn = _seg_rms(hs, HEAD_W, nw_ref[...])
        o = qkvo_ref[pl.ds(r0, blk), 3 * MIX_W:4 * MIX_W].astype(f32)
        out_ref[pl.ds(r0, blk), :] = (hn * jax.nn.sigmoid(o)).astype(out_ref.dtype)
        return carry

    lax.fori_loop(0, seq // blk, fin, 0)


def _mlstm(u, g, nb, seq, gb_row, nw_row, state):
    has_state = state is not None
    nc = seq // CHUNK
    in_specs = [pl.BlockSpec((seq, 4 * MIX_W), lambda b: (b, COL_MLSTM // (4 * MIX_W))),
                pl.BlockSpec((seq, LANE), lambda b: (b, 0)),
                pl.BlockSpec((1, LANE), lambda b: (0, 0)),
                pl.BlockSpec((1, MIX_W), lambda b: (0, 0))]
    args = [u, g, gb_row, nw_row]
    if has_state:
        in_specs += [pl.BlockSpec((None, 2, HEADS, HEAD_W, HEAD_W), lambda b: (b, 0, 0, 0, 0)),
                     pl.BlockSpec((None, 2, HEADS, HEAD_W), lambda b: (b, 0, 0, 0)),
                     pl.BlockSpec((None, 2, HEADS), lambda b: (b, 0, 0))]
        args += list(state)
    return pl.pallas_call(
        functools.partial(_mlstm_body, seq, has_state),
        grid=(nb,),
        in_specs=in_specs,
        out_specs=[pl.BlockSpec((seq, MIX_W), lambda b: (b, 0)),
                   pl.BlockSpec((None, 2, HEADS, HEAD_W, HEAD_W), lambda b: (b, 0, 0, 0, 0)),
                   pl.BlockSpec((None, 2, HEADS, HEAD_W), lambda b: (b, 0, 0, 0)),
                   pl.BlockSpec((None, 2, HEADS), lambda b: (b, 0, 0))],
        out_shape=[jax.ShapeDtypeStruct((nb * seq, MIX_W), bf16),
                   jax.ShapeDtypeStruct((nb, 2, HEADS, HEAD_W, HEAD_W), f32),
                   jax.ShapeDtypeStruct((nb, 2, HEADS, HEAD_W), f32),
                   jax.ShapeDtypeStruct((nb, 2, HEADS), f32)],
        scratch_shapes=[pltpu.VMEM((2, seq, MIX_W), f32), pltpu.VMEM((2, seq, MIX_W), f32),
                        pltpu.VMEM((2, seq, MIX_W), f32), pltpu.VMEM((2, nc, 1, MIX_W), f32),
                        pltpu.VMEM((2, MIX_W, MIX_W), f32),
                        pltpu.VMEM((seq, MIX_W), f32), pltpu.VMEM((seq, MIX_W), f32)],
        compiler_params=_params(("arbitrary",)),
        name="mlstm_state" if has_state else "mlstm",
    )(*args)


def _ssd_body(seq, has_state, *refs):
    if has_state:
        (z_ref, xbc_ref, g_ref, cw_ref, cb_ref, dtb_ref, alog_ref, dskip_ref, nw_ref, h0_ref,
         out_ref, hs_ref, xpad_scr, xc_scr, ax_scr, dx_scr, arow_scr, ht_scr, yf_scr, yb_scr) = refs
    else:
        (z_ref, xbc_ref, g_ref, cw_ref, cb_ref, dtb_ref, alog_ref, dskip_ref, nw_ref,
         out_ref, hs_ref, xpad_scr, xc_scr, ax_scr, dx_scr, arow_scr, ht_scr, yf_scr, yb_scr) = refs
    nc = seq // CHUNK
    pad = 8
    blk = 256
    nblk = seq // blk
    xpad_scr[0:pad, :] = jnp.zeros((pad, SSM_CONV_CH), f32)
    xpad_scr[pad + seq:2 * pad + seq, :] = jnp.zeros((pad, SSM_CONV_CH), f32)
    xpad_scr[pad:pad + seq, :] = xbc_ref[...].astype(f32)

    cblk = 128
    for i in range(seq // cblk):
        r0 = i * cblk
        acc = jnp.zeros((cblk, SSM_CONV_CH), f32) + cb_ref[...]
        for kk in range(SSM_CONV_K):
            off = r0 + pad - SSM_CONV_K // 2 + kk
            acc = acc + xpad_scr[off:off + cblk, :] * cw_ref[kk:kk + 1, :]
        xc_scr[r0:r0 + cblk, :] = jax.nn.silu(acc)

    dt = jax.nn.softplus(g_ref[...] + dtb_ref[...])
    lane = lax.broadcasted_iota(jnp.int32, dt.shape, 1)
    acum = _dir_scan(dt * (-jnp.exp(alog_ref[...])), jnp.add, 0.0, (lane % 8) >= HEADS)
    for d in range(2):
        e = _head_expand(G_DT + d * HEADS)
        ax = _select_cols(acum, e)
        ax_scr[d] = ax
        dx_scr[d] = _select_cols(dt, e)
        arow = _chunk_rows(ax, seq)
        for c in range(nc):
            arow_scr[d, c] = arow[c:c + 1, :]

    gn = SSM_GROUPS * SSM_STATE
    rep = HEADS // SSM_GROUPS
    ht_scr[...] = jnp.zeros(ht_scr.shape, f32)
    if has_state:
        for d in range(2):
            for h in range(HEADS):
                g0 = (h // rep) * SSM_STATE
                ht_scr[d, g0:g0 + SSM_STATE, h * HEAD_W:(h + 1) * HEAD_W] = h0_ref[d, h].T
    li = lax.broadcasted_iota(jnp.int32, (CHUNK, MIX_W), 0)
    si = lax.broadcasted_iota(jnp.int32, (CHUNK, MIX_W), 1) % HEAD_W
    valid = (si <= li, si >= li)
    state_mask = _head_mask(gn, MIX_W, SSM_STATE, rep * HEAD_W)
    key_mask = _head_mask(HEADS * CHUNK, gn, rep * CHUNK, SSM_STATE)

    def step(c, carry):
        for d in range(2):
            cidx = c if d == 0 else nc - 1 - c
            r0 = pl.multiple_of(cidx * CHUNK, CHUNK)
            last = CHUNK - 1 if d == 0 else 0
            xs = xc_scr[pl.ds(r0, CHUNK), 0:MIX_W]
            bm = xc_scr[pl.ds(r0, CHUNK), MIX_W:MIX_W + gn].astype(bf16)
            cm = xc_scr[pl.ds(r0, CHUNK), MIX_W + gn:MIX_W + 2 * gn].astype(bf16)
            ax = ax_scr[d, pl.ds(r0, CHUNK), :]
            decay = jnp.exp(jnp.where(valid[d], ax - arow_scr[d, cidx], NEG))
            bexp = jnp.concatenate([bm] * HEADS, axis=0)
            bexp = jnp.where(key_mask, bexp, jnp.zeros_like(bexp))
            scores = _mm_nt(cm, bexp) * decay
            xdt = xs * dx_scr[d, pl.ds(r0, CHUNK), :]
            y = _mm(scores, _block_diag(xdt.astype(bf16), HEADS)) + _mm(cm, ht_scr[d]) * jnp.exp(ax)
            y_scr = yf_scr if d == 0 else yb_scr
            y_scr[pl.ds(r0, CHUNK), :] = y
            al = ax[last:last + 1, :]
            upd = jnp.where(state_mask, _mm_tn(bm, xdt * jnp.exp(al - ax)), 0.0)
            ht_scr[d] = ht_scr[d] * jnp.exp(al) + upd
        return carry

    lax.fori_loop(0, nc, step, 0)
    for d in range(2):
        for h in range(HEADS):
            g0 = (h // rep) * SSM_STATE
            hs_ref[d, h] = ht_scr[d, g0:g0 + SSM_STATE, h * HEAD_W:(h + 1) * HEAD_W].T

    def fin(i, carry):
        r0 = pl.multiple_of(i * blk, blk)
        y = yf_scr[pl.ds(r0, blk), :] + yb_scr[pl.ds(r0, blk), :] + dskip_ref[...] * xc_scr[pl.ds(r0, blk), 0:MIX_W]
        y = y * jax.nn.silu(z_ref[pl.ds(r0, blk), :].astype(f32))
        y = y * lax.rsqrt(jnp.mean(y * y, axis=-1, keepdims=True) + EPS) * nw_ref[...]
        out_ref[pl.ds(r0, blk), :] = y.astype(out_ref.dtype)
        return carry

    lax.fori_loop(0, nblk, fin, 0)


def _ssd(u, g, nb, seq, cw, cb_row, dtb_row, alog_row, dskip_row, nw_row, state):
    has_state = state is not None
    nc = seq // CHUNK
    const = lambda shape: pl.BlockSpec(shape, lambda b: (0,) * len(shape))
    in_specs = [pl.BlockSpec((seq, MIX_W), lambda b: (b, COL_Z // MIX_W)),
                pl.BlockSpec((seq, SSM_CONV_CH), lambda b: (b, COL_XBC // SSM_CONV_CH)),
                pl.BlockSpec((seq, LANE), lambda b: (b, 0)),
                const((SSM_CONV_K, SSM_CONV_CH)), const((1, SSM_CONV_CH)), const((1, LANE)), const((1, LANE)),
                const((1, MIX_W)), const((1, MIX_W))]
    args = [u, u, g, cw, cb_row, dtb_row, alog_row, dskip_row, nw_row]
    if has_state:
        in_specs.append(pl.BlockSpec((None, 2, HEADS, HEAD_W, SSM_STATE), lambda b: (b, 0, 0, 0, 0)))
        args.append(state)
    return pl.pallas_call(
        functools.partial(_ssd_body, seq, has_state),
        grid=(nb,),
        in_specs=in_specs,
        out_specs=[pl.BlockSpec((seq, MIX_W), lambda b: (b, 0)),
                   pl.BlockSpec((None, 2, HEADS, HEAD_W, SSM_STATE), lambda b: (b, 0, 0, 0, 0))],
        out_shape=[jax.ShapeDtypeStruct((nb * seq, MIX_W), bf16),
                   jax.ShapeDtypeStruct((nb, 2, HEADS, HEAD_W, SSM_STATE), f32)],
        scratch_shapes=[pltpu.VMEM((seq + 16, SSM_CONV_CH), f32), pltpu.VMEM((seq, SSM_CONV_CH), f32),
                        pltpu.VMEM((2, seq, MIX_W), f32), pltpu.VMEM((2, seq, MIX_W), f32),
                        pltpu.VMEM((2, nc, 1, MIX_W), f32),
                        pltpu.VMEM((2, SSM_GROUPS * SSM_STATE, MIX_W), f32),
                        pltpu.VMEM((seq, MIX_W), f32), pltpu.VMEM((seq, MIX_W), f32)],
        compiler_params=_params(("arbitrary",)),
        name="ssd_state" if has_state else "ssd",
    )(*args)


def _softmax_pv(scores, values, hl):
    m = scores[0].max(axis=1, keepdims=True)
    for s in scores[1:]:
        m = jnp.maximum(m, s.max(axis=1, keepdims=True))
    acc, den = None, None
    for s, v in zip(scores, values):
        p = jnp.exp(s - m)
        d = jnp.sum(p, axis=1, keepdims=True)
        o = _mm(p, v[:, hl])
        acc = o if acc is None else acc + o
        den = d if den is None else den + d
    return acc / den


def _na_prompt_body(q_ref, k_ref, v_ref, qn_ref, kn_ref, o_ref, ko_ref, vo_ref):
    scale = HEAD_W ** -0.5
    qn = _seg_rms(q_ref[...].astype(f32), HEAD_W, qn_ref[...]) * scale
    kn = _seg_rms(k_ref[...].astype(f32), HEAD_W, kn_ref[...])
    v = v_ref[...].astype(f32)
    ko_ref[...] = kn
    vo_ref[...] = v
    for h in range(HEADS):
        hl = slice(h * HEAD_W, (h + 1) * HEAD_W)
        s = _mm_nt(qn[:, hl], kn[:, hl])
        o_ref[:, hl] = _softmax_pv([s], [v], hl).astype(o_ref.dtype)


def _na_prompt(u, nb, seq, qn_row, kn_row):
    cb = COL_NA // MIX_W
    blk = lambda j: pl.BlockSpec((seq, MIX_W), lambda b: (b, j))
    const = pl.BlockSpec((1, MIX_W), lambda b: (0, 0))
    sds = jax.ShapeDtypeStruct((nb * seq, MIX_W), f32)
    return pl.pallas_call(
        _na_prompt_body,
        grid=(nb,),
        in_specs=[blk(cb), blk(cb + 1), blk(cb + 2), const, const],
        out_specs=[blk(0), blk(0), blk(0)],
        out_shape=[jax.ShapeDtypeStruct(sds.shape, bf16), sds, sds],
        compiler_params=_params(("arbitrary",)),
        name="na_prompt",
    )(u, u, u, qn_row, kn_row)


def _na_sample_body(q_ref, k_ref, v_ref, kc_ref, vc_ref, rpb_ref, qn_ref, kn_ref, o_ref, kn_scr, v_scr, toe_scr):
    b = pl.program_id(0)
    i = pl.program_id(1)
    ndr = 2 * NA_WIN_R - 1

    @pl.when((b == 0) & (i == 0))
    def _():
        j = lax.broadcasted_iota(jnp.int32, (GRID_W, LANE), 0)
        lane = lax.broadcasted_iota(jnp.int32, (GRID_W, LANE), 1)
        j2 = lane % GRID_W
        cs = jnp.clip(j - NA_WIN_C // 2, 0, GRID_W - NA_WIN_C)
        ok = (j2 >= cs) & (j2 < cs + NA_WIN_C)
        for h in range(HEADS):
            blocks = []
            for dr in range(ndr):
                row = jnp.broadcast_to(rpb_ref[h, dr:dr + 1, :], (GRID_W, LANE))
                blocks.append((pltpu.roll(row, LANE - (NA_WIN_C - 1), 1, stride=1, stride_axis=0),
                               pltpu.roll(row, GRID_W - (NA_WIN_C - 1), 1, stride=1, stride_axis=0)))
            for dr in range(ndr - 1):
                toe_scr[h, dr] = jnp.where(ok, jnp.where(lane < GRID_W, blocks[dr][0], blocks[dr + 1][1]), NEG)

    @pl.when(i == 0)
    def _():
        kn_scr[...] = _seg_rms(k_ref[...].astype(f32), HEAD_W, kn_ref[...]).astype(bf16)
        v_scr[...] = v_ref[...].astype(bf16)

    rows = DEC_SEQ // GRID_W
    rs = jnp.clip(i - NA_WIN_R // 2, 0, rows - NA_WIN_R)
    d0 = rs - i + (NA_WIN_R - 1)
    k0 = pl.multiple_of(rs * GRID_W, GRID_W)
    nwin = NA_WIN_R * GRID_W
    qn = _seg_rms(q_ref[...].astype(f32), HEAD_W, qn_ref[...]) * (HEAD_W ** -0.5)
    kwin = kn_scr[pl.ds(k0, nwin), :]
    vwin = v_scr[pl.ds(k0, nwin), :]
    kctx = kc_ref[...]
    vctx = vc_ref[...]
    for h in range(HEADS):
        hl = slice(h * HEAD_W, (h + 1) * HEAD_W)
        bias = jnp.concatenate([toe_scr[h, d0 + 2 * r] for r in range(NA_WIN_R // 2)], axis=1)
        s_win = _mm_nt(qn[:, hl], kwin[:, hl]) + bias
        s_ctx = _mm_nt(qn[:, hl], kctx[:, hl])
        o_ref[:, hl] = _softmax_pv([s_win, s_ctx], [vwin, vctx], hl).astype(o_ref.dtype)


def _na_sample(u, nb, layer, cache_k, cache_v, rpb_pad, qn_row, kn_row):
    seq = DEC_SEQ
    rows = seq // GRID_W
    cb = COL_NA // MIX_W
    ndr = 2 * NA_WIN_R - 1
    full = lambda j: pl.BlockSpec((seq, MIX_W), lambda b, i: (b, j))
    ctx = pl.BlockSpec((None, None, PAST_LEN, MIX_W), lambda b, i: (b, layer, 0, 0))
    const = pl.BlockSpec((1, MIX_W), lambda b, i: (0, 0))
    return pl.pallas_call(
        _na_sample_body,
        grid=(nb, rows),
        in_specs=[pl.BlockSpec((GRID_W, MIX_W), lambda b, i: (b * rows + i, cb)), full(cb + 1), full(cb + 2),
                  ctx, ctx,
                  pl.BlockSpec((HEADS, ndr + 1, LANE), lambda b, i: (0, 0, 0)),
                  const, const],
        out_specs=pl.BlockSpec((GRID_W, MIX_W), lambda b, i: (b * rows + i, 0)),
        out_shape=jax.ShapeDtypeStruct((nb * seq, MIX_W), bf16),
        scratch_shapes=[pltpu.VMEM((seq, MIX_W), bf16), pltpu.VMEM((seq, MIX_W), bf16),
                        pltpu.VMEM((HEADS, ndr - 1, GRID_W, LANE), f32)],
        compiler_params=_params(("arbitrary", "arbitrary")),
        name="na_sample",
    )(u, u, u, cache_k, cache_v, rpb_pad, qn_row, kn_row)


def _na_rpb_pad(rpb):
    c = rpb.shape[2]
    return jnp.pad(rpb.astype(f32), ((0, 0), (0, 1), (0, LANE - c)))


def _diff_lambda(lam_ref, lam_init):
    lv = lam_ref[...]
    s1 = jnp.sum(lv[0:1, :] * lv[1:2, :], axis=1, keepdims=True)
    s2 = jnp.sum(lv[2:3, :] * lv[3:4, :], axis=1, keepdims=True)
    return jnp.exp(s1) - jnp.exp(s2) + lam_init


def _diff_head(q, keys, vals, lam, h):
    hl = slice(h * HEAD_W, (h + 1) * HEAD_W)
    acc = None
    for m in range(2):
        ml = slice(h * HEAD_W + m * DIFF_QK_DIM, h * HEAD_W + (m + 1) * DIFF_QK_DIM)
        scores = [_mm_nt(q[:, ml], k[:, ml]) for k in keys]
        mx = scores[0].max(axis=1, keepdims=True)
        for s in scores[1:]:
            mx = jnp.maximum(mx, s.max(axis=1, keepdims=True))
        ps = [jnp.exp(s - mx) for s in scores]
        den = ps[0].sum(axis=1, keepdims=True)
        for p in ps[1:]:
            den = den + p.sum(axis=1, keepdims=True)
        coef = (1.0 / den) if m == 0 else (-lam / den)
        ps = [p * coef for p in ps]
        acc = ps if acc is None else [a + p for a, p in zip(acc, ps)]
    out = None
    for a, v in zip(acc, vals):
        o = _mm(a, v[:, hl])
        out = o if out is None else out + o
    return out


def _diff_prompt_body(lam_init, q_ref, k_ref, v_ref, qn_ref, kn_ref, lam_ref, nw_ref, o_ref, ko_ref, vo_ref):
    lam = _diff_lambda(lam_ref, lam_init)
    qn = _seg_rms(q_ref[...].astype(f32), DIFF_QK_DIM, qn_ref[...]) * (DIFF_QK_DIM ** -0.5)
    kn = _seg_rms(k_ref[...].astype(f32), DIFF_QK_DIM, kn_ref[...])
    v = v_ref[...].astype(f32)
    ko_ref[...] = kn
    vo_ref[...] = v
    o = jnp.concatenate([_diff_head(qn, [kn], [v], lam, h) for h in range(HEADS)], axis=1)
    o_ref[...] = (_seg_rms(o, HEAD_W, nw_ref[...]) * (1.0 - lam_init)).astype(o_ref.dtype)


def _diff_prompt(u, nb, seq, lam_init, qn_row, kn_row, lam_vecs, nw_row):
    cb = COL_DIFF // MIX_W
    blk = lambda j: pl.BlockSpec((seq, MIX_W), lambda b: (b, j))
    const = pl.BlockSpec((1, MIX_W), lambda b: (0, 0))
    sds = jax.ShapeDtypeStruct((nb * seq, MIX_W), f32)
    return pl.pallas_call(
        functools.partial(_diff_prompt_body, lam_init),
        grid=(nb,),
        in_specs=[blk(cb), blk(cb + 1), blk(cb + 2), const, const,
                  pl.BlockSpec((4, DIFF_QK_DIM), lambda b: (0, 0)), const],
        out_specs=[blk(0), blk(0), blk(0)],
        out_shape=[jax.ShapeDtypeStruct(sds.shape, bf16), sds, sds],
        compiler_params=_params(("arbitrary",)),
        name="diff_prompt",
    )(u, u, u, qn_row, kn_row, lam_vecs, nw_row)


def _rope(x, cos, sin):
    w = x.shape[-1]
    up = pltpu.roll(x, w - 8, 1)
    dn = pltpu.roll(x, 8, 1)
    first = (lax.broadcasted_iota(jnp.int32, x.shape, 1) % 16) < 8
    return x * cos + jnp.where(first, up, dn) * sin


def _diff_sample_body(lam_init, qblk, q_ref, k_ref, v_ref, kc_ref, vc_ref, cos_ref, sin_ref, qn_ref, kn_ref,
                      lam_ref, nw_ref, o_ref, kn_scr, v_scr):
    i = pl.program_id(1)

    @pl.when(i == 0)
    def _():
        kn = _seg_rms(k_ref[...].astype(f32), DIFF_QK_DIM, kn_ref[...])
        kn_scr[...] = _rope(kn, cos_ref[...], sin_ref[...]).astype(bf16)
        v_scr[...] = v_ref[...].astype(bf16)

    lam = _diff_lambda(lam_ref, lam_init)
    r0 = pl.multiple_of(i * qblk, qblk)
    qn = _seg_rms(q_ref[...].astype(f32), DIFF_QK_DIM, qn_ref[...])
    qn = _rope(qn, cos_ref[pl.ds(r0, qblk), :], sin_ref[pl.ds(r0, qblk), :]) * (DIFF_QK_DIM ** -0.5)
    keys = [kn_scr[...], kc_ref[...]]
    vals = [v_scr[...], vc_ref[...]]
    o = jnp.concatenate([_diff_head(qn, keys, vals, lam, h) for h in range(HEADS)], axis=1)
    o_ref[...] = (_seg_rms(o, HEAD_W, nw_ref[...]) * (1.0 - lam_init)).astype(o_ref.dtype)


def _diff_sample(u, nb, layer, lam_init, cache_k, cache_v, cos, sin, qn_row, kn_row, lam_vecs, nw_row):
    seq = DEC_SEQ
    qblk = 128
    nq = seq // qblk
    cb = COL_DIFF // MIX_W
    full = lambda j: pl.BlockSpec((seq, MIX_W), lambda b, i: (b, j))
    ctx = pl.BlockSpec((None, None, PAST_LEN, MIX_W), lambda b, i: (b, layer, 0, 0))
    const = pl.BlockSpec((1, MIX_W), lambda b, i: (0, 0))
    tab = pl.BlockSpec((seq, MIX_W), lambda b, i: (0, 0))
    return pl.pallas_call(
        functools.partial(_diff_sample_body, lam_init, qblk),
        grid=(nb, nq),
        in_specs=[pl.BlockSpec((qblk, MIX_W), lambda b, i: (b * nq + i, cb)), full(cb + 1), full(cb + 2),
                  ctx, ctx, tab, tab, const, const,
                  pl.BlockSpec((4, DIFF_QK_DIM), lambda b, i: (0, 0)), const],
        out_specs=pl.BlockSpec((qblk, MIX_W), lambda b, i: (b * nq + i, 0)),
        out_shape=jax.ShapeDtypeStruct((nb * seq, MIX_W), bf16),
        scratch_shapes=[pltpu.VMEM((seq, MIX_W), bf16), pltpu.VMEM((seq, MIX_W), bf16)],
        compiler_params=_params(("arbitrary", "arbitrary")),
        name="diff_sample",
    )(u, u, u, cache_k, cache_v, cos, sin, qn_row, kn_row, lam_vecs, nw_row)


def _rope_tables():
    t = jnp.arange(DEC_SEQ)
    nf = DIFF_QK_DIM // 4
    inv = ROPE_BASE ** (-jnp.arange(nf, dtype=f32) / nf)
    ar = (t // GRID_W).astype(f32)[:, None] * inv
    ac = (t % GRID_W).astype(f32)[:, None] * inv
    cos = jnp.concatenate([jnp.cos(ar), jnp.cos(ar), jnp.cos(ac), jnp.cos(ac)], axis=1)
    sin = jnp.concatenate([-jnp.sin(ar), jnp.sin(ar), -jnp.sin(ac), jnp.sin(ac)], axis=1)
    reps = MIX_W // DIFF_QK_DIM
    return jnp.tile(cos, (1, reps)), jnp.tile(sin, (1, reps))


def _merge_body(x_ref, a_ref, b_ref, c_ref, d_ref, gp_ref, mod_ref, wb_ref, wo_ref, nw_ref, wr_ref,
                xo_ref, h2_ref, aff_ref):
    mixed = None
    for n, br in enumerate((a_ref, b_ref, c_ref, d_ref)):
        gate = jax.nn.sigmoid(gp_ref[:, n * D_MODEL:(n + 1) * D_MODEL].astype(f32))
        t = gate * jnp.dot(br[...].astype(bf16), wb_ref[n], preferred_element_type=f32)
        mixed = t if mixed is None else mixed + t
    y = jnp.dot(mixed.astype(bf16), wo_ref[...], preferred_element_type=f32)
    x = x_ref[...] + mod_ref[:, 2 * D_MODEL:3 * D_MODEL] * y
    xo_ref[...] = x
    h = x * lax.rsqrt(jnp.mean(x * x, axis=-1, keepdims=True) + EPS) * nw_ref[...]
    h = h * (1.0 + mod_ref[:, 4 * D_MODEL:5 * D_MODEL]) + mod_ref[:, 3 * D_MODEL:4 * D_MODEL]
    hb = h.astype(bf16)
    h2_ref[...] = hb
    logits = jnp.dot(hb, wr_ref[...], preferred_element_type=f32)
    lane = lax.broadcasted_iota(jnp.int32, logits.shape, 1)
    logits = jnp.where(lane < N_EXPERTS, logits, NEG)
    e = jnp.exp(logits - logits.max(axis=1, keepdims=True))
    aff_ref[...] = e / e.sum(axis=1, keepdims=True)


def _merge(x, brs, u, mod3, mod_base, mod_stride, seq, wb, wo, nw_row, wr):
    n = x.shape[0]
    tm = 256
    tps = seq // tm
    row = lambda w: pl.BlockSpec((tm, w), lambda i: (i, 0))
    return pl.pallas_call(
        _merge_body,
        grid=(n // tm,),
        in_specs=[row(D_MODEL), row(MIX_W), row(MIX_W), row(MIX_W), row(MIX_W),
                  pl.BlockSpec((tm, N_BRANCH * D_MODEL), lambda i: (i, COL_GPRE // (N_BRANCH * D_MODEL))),
                  pl.BlockSpec((None, 1, 6 * D_MODEL), lambda i: (mod_base + mod_stride * (i // tps), 0, 0)),
                  pl.BlockSpec((N_BRANCH, MIX_W, D_MODEL), lambda i: (0, 0, 0)),
                  pl.BlockSpec((D_MODEL, D_MODEL), lambda i: (0, 0)),
                  pl.BlockSpec((1, D_MODEL), lambda i: (0, 0)),
                  pl.BlockSpec((D_MODEL, LANE), lambda i: (0, 0))],
        out_specs=[row(D_MODEL), row(D_MODEL), row(LANE)],
        out_shape=[jax.ShapeDtypeStruct((n, D_MODEL), f32), jax.ShapeDtypeStruct((n, D_MODEL), bf16),
                   jax.ShapeDtypeStruct((n, LANE), f32)],
        compiler_params=_params(("arbitrary",)),
        name="merge",
    )(x, *brs, u, mod3, wb, wo, nw_row, wr)


def _route_body(t, cap, aff_ref, h2_ref, xg_ref, g_ref, rank_ref, rrow_scr, oh_scr):
    aff = aff_ref[...]
    afft = aff.T
    sub = lax.broadcasted_iota(jnp.int32, (t, t), 0)
    lan = lax.broadcasted_iota(jnp.int32, (t, t), 1)
    earlier = sub < lan
    rrow_scr[...] = jnp.zeros(rrow_scr.shape, f32) + float(cap)
    slot = lax.broadcasted_iota(jnp.int32, (cap, t), 0).astype(f32)
    h2 = h2_ref[...]
    for e in range(N_EXPERTS):
        a_col = aff[:, e:e + 1]
        a_row = afft[e:e + 1, :]
        beats = (a_col > a_row) | ((a_col == a_row) & earlier)
        rank = jnp.sum(jnp.where(beats, 1.0, 0.0), axis=0, keepdims=True)
        rank = jnp.minimum(rank, float(cap))
        rrow_scr[e:e + 1, :] = rank
        onehot = jnp.where(rank == slot, 1.0, 0.0)
        oh_scr[e * cap:(e + 1) * cap, :] = onehot.astype(bf16)
        g_ref[e] = jnp.sum(onehot * a_row, axis=1, keepdims=True)
    epb = max(1, 512 // cap)
    for e0 in range(0, N_EXPERTS, epb):
        rows = jnp.dot(oh_scr[e0 * cap:(e0 + epb) * cap, :], h2, preferred_element_type=f32)
        xg_ref[e0:e0 + epb] = rows.astype(bf16).reshape(epb, cap, D_MODEL)
    rank_ref[...] = rrow_scr[...].T


def _route(aff, h2, nb, t):
    cap = CAPACITY_FACTOR * t // N_EXPERTS
    return pl.pallas_call(
        functools.partial(_route_body, t, cap),
        grid=(nb,),
        in_specs=[pl.BlockSpec((t, LANE), lambda b: (b, 0)), pl.BlockSpec((t, D_MODEL), lambda b: (b, 0))],
        out_specs=[pl.BlockSpec((N_EXPERTS, cap, D_MODEL), lambda b: (0, b, 0)),
                   pl.BlockSpec((N_EXPERTS, cap, 1), lambda b: (0, b, 0)),
                   pl.BlockSpec((t, LANE), lambda b: (b, 0))],
        out_shape=[jax.ShapeDtypeStruct((N_EXPERTS, nb * cap, D_MODEL), bf16),
                   jax.ShapeDtypeStruct((N_EXPERTS, nb * cap, 1), f32),
                   jax.ShapeDtypeStruct((nb * t, LANE), f32)],
        scratch_shapes=[pltpu.VMEM((LANE, t), f32), pltpu.VMEM((N_EXPERTS * cap, t), bf16)],
        compiler_params=_params(("arbitrary",)),
        name=f"route_{t}",
    )(aff, h2)


def _ffn_body(xp_ref, xs_ref, gp_ref, gs_ref, wg_ref, wu_ref, wd_ref, yp_ref, ys_ref, wg_scr, wu_scr, wd_scr):
    wg_scr[...] = wg_ref[...].astype(bf16)
    wu_scr[...] = wu_ref[...].astype(bf16)
    wd_scr[...] = wd_ref[...].astype(bf16)
    blk = 512
    for x_ref, g_ref, y_ref in ((xp_ref, gp_ref, yp_ref), (xs_ref, gs_ref, ys_ref)):
        def step(i, carry, x_ref=x_ref, g_ref=g_ref, y_ref=y_ref):
            r0 = pl.multiple_of(i * blk, blk)
            x = x_ref[pl.ds(r0, blk), :]
            a = jnp.dot(x, wg_scr[...], preferred_element_type=f32)
            up = jnp.dot(x, wu_scr[...], preferred_element_type=f32)
            mid = (jax.nn.silu(a) * up).astype(bf16)
            y = jnp.dot(mid, wd_scr[...], preferred_element_type=f32)
            y_ref[pl.ds(r0, blk), :] = (y * g_ref[pl.ds(r0, blk), :]).astype(bf16)
            return carry
        lax.fori_loop(0, x_ref.shape[0] // blk, step, 0)


def _ffn(xp, xs, gp, gs, w_gate, w_up, w_down, layer):
    np_, ns = xp.shape[1], xs.shape[1]
    tok = lambda n, w: pl.BlockSpec((None, n, w), lambda e: (e, 0, 0))
    wspec = lambda a, b: pl.BlockSpec((None, None, a, b), lambda e: (layer, e, 0, 0))
    return pl.pallas_call(
        _ffn_body,
        grid=(N_EXPERTS,),
        in_specs=[tok(np_, D_MODEL), tok(ns, D_MODEL), tok(np_, 1), tok(ns, 1),
                  wspec(D_MODEL, EXPERT_FF), wspec(D_MODEL, EXPERT_FF), wspec(EXPERT_FF, D_MODEL)],
        out_specs=[tok(np_, D_MODEL), tok(ns, D_MODEL)],
        out_shape=[jax.ShapeDtypeStruct(xp.shape, bf16), jax.ShapeDtypeStruct(xs.shape, bf16)],
        scratch_shapes=[pltpu.VMEM((D_MODEL, EXPERT_FF), bf16), pltpu.VMEM((D_MODEL, EXPERT_FF), bf16),
                        pltpu.VMEM((EXPERT_FF, D_MODEL), bf16)],
        compiler_params=_params(("arbitrary",)),
        name="expert_ffn",
    )(xp, xs, gp, gs, w_gate, w_up, w_down)


def _combine_body(t, cap, x_ref, rank_ref, y_ref, mod_ref, o_ref):
    k = N_EXPERTS * cap
    er = lax.broadcasted_iota(jnp.int32, (LANE, k), 0)
    ec = lax.broadcasted_iota(jnp.int32, (LANE, k), 1) // cap
    expand = jnp.where(er == ec, 1.0, 0.0).astype(bf16)
    rexp = jnp.dot(rank_ref[...].astype(bf16), expand, preferred_element_type=f32)
    slot = (lax.broadcasted_iota(jnp.int32, (t, k), 1) % cap).astype(f32)
    pt = jnp.where(rexp == slot, 1.0, 0.0).astype(bf16)
    moe = jnp.dot(pt, y_ref[...].reshape(k, D_MODEL), preferred_element_type=f32)
    o_ref[...] = x_ref[...] + mod_ref[:, 5 * D_MODEL:6 * D_MODEL] * moe


def _combine(x, rank, y, mod3, mod_base, mod_stride, nb, t):
    cap = CAPACITY_FACTOR * t // N_EXPERTS
    return pl.pallas_call(
        functools.partial(_combine_body, t, cap),
        grid=(nb,),
        in_specs=[pl.BlockSpec((t, D_MODEL), lambda b: (b, 0)),
                  pl.BlockSpec((t, LANE), lambda b: (b, 0)),
                  pl.BlockSpec((N_EXPERTS, cap, D_MODEL), lambda b: (0, b, 0)),
                  pl.BlockSpec((None, 1, 6 * D_MODEL), lambda b: (mod_base + mod_stride * b, 0, 0))],
        out_specs=pl.BlockSpec((t, D_MODEL), lambda b: (b, 0)),
        out_shape=jax.ShapeDtypeStruct(x.shape, f32),
        compiler_params=_params(("arbitrary",)),
        name=f"combine_{t}",
    )(x, rank, y, mod3)


def _pad_row(v, off, width=LANE):
    return jnp.zeros((1, width), f32).at[0, off:off + v.shape[0]].set(v.astype(f32))


def _layer_params(l, a):
    w_in = a["w_in"][l]
    pts, acc = [], 0
    splits = (MIX_W,) * 4 + (4 * HEADS,) + (MIX_W,) * 7 + (SSM_CONV_CH, 2 * HEADS, N_BRANCH * D_MODEL)
    for s in splits:
        pts.append((acc, acc + s))
        acc += s
    seg = lambda i: w_in[:, pts[i][0]:pts[i][1]]
    order = [14, 0, 1, 2, 3, 5, 6, 7, 8, 9, 10, 12, 11]
    w_main = jnp.concatenate([seg(i) for i in order] + [jnp.zeros((D_MODEL, U_COLS - U_USED), f32)],
                             axis=1).astype(bf16)
    w_side = jnp.concatenate([seg(4), seg(13), jnp.zeros((D_MODEL, LANE - 24), f32)], axis=1).astype(bf16)
    tile = lambda v, reps: jnp.tile(v.astype(f32), reps).reshape(1, -1)
    p = dict(
        w_main=w_main, w_side=w_side,
        norm1=a["norm1_w"][l].reshape(1, -1), norm2=a["norm2_w"][l].reshape(1, -1),
        mlstm_gb=_pad_row(a["mlstm_gate_b"][l].reshape(-1), G_MLSTM),
        mlstm_nw=a["mlstm_norm_w"][l].reshape(1, -1),
        na_qn=tile(a["na_qnorm_w"][l], HEADS), na_kn=tile(a["na_knorm_w"][l], HEADS),
        na_rpb=_na_rpb_pad(a["na_rpb"][l]),
        diff_qn=tile(a["diff_qnorm_w"][l], 2 * HEADS), diff_kn=tile(a["diff_knorm_w"][l], 2 * HEADS),
        diff_lam=a["diff_lambda"][l], diff_nw=tile(a["diff_norm_w"][l], HEADS),
        ssm_cw=a["ssm_conv_w"][l], ssm_cb=a["ssm_conv_b"][l].reshape(1, -1),
        ssm_dtb=_pad_row(a["ssm_dt_bias"][l].reshape(-1), G_DT),
        ssm_alog=_pad_row(a["ssm_a_log"][l].reshape(-1), G_DT),
        ssm_d=jnp.repeat(a["ssm_d"][l].astype(f32), HEAD_W).reshape(1, -1),
        ssm_nw=a["ssm_norm_w"][l].reshape(1, -1),
        wb=a["w_branch"][l].astype(bf16), wo=a["w_out"][l].astype(bf16),
        wr=jnp.concatenate([a["w_router"][l], jnp.zeros((D_MODEL, LANE - N_EXPERTS), f32)], axis=1).astype(bf16),
        w_gate=a["w_gate"], w_up=a["w_up"], w_down=a["w_down"],
    )
    return p


def _mixers(u, g, nb, seq, p, l, lam_init, ctx, tables):
    if ctx is None:
        a_out, m_c, m_n, m_m = _mlstm(u, g, nb, seq, p["mlstm_gb"], p["mlstm_nw"], None)
        b_out, na_k, na_v = _na_prompt(u, nb, seq, p["na_qn"], p["na_kn"])
        c_out, df_k, df_v = _diff_prompt(u, nb, seq, lam_init, p["diff_qn"], p["diff_kn"], p["diff_lam"], p["diff_nw"])
        d_out, s_h = _ssd(u, g, nb, seq, p["ssm_cw"], p["ssm_cb"], p["ssm_dtb"], p["ssm_alog"], p["ssm_d"],
                          p["ssm_nw"], None)
        return (a_out, b_out, c_out, d_out), (na_k, na_v, df_k, df_v, m_c, m_n, m_m, s_h)
    (na_ck, na_cv, df_ck, df_cv, st_c, st_n, st_m, st_s) = ctx
    cos, sin = tables
    a_out = _mlstm(u, g, nb, seq, p["mlstm_gb"], p["mlstm_nw"], (st_c[:, l], st_n[:, l], st_m[:, l]))[0]
    b_out = _na_sample(u, nb, l, na_ck, na_cv, p["na_rpb"], p["na_qn"], p["na_kn"])
    c_out = _diff_sample(u, nb, l, lam_init, df_ck, df_cv, cos, sin, p["diff_qn"], p["diff_kn"], p["diff_lam"],
                         p["diff_nw"])
    d_out = _ssd(u, g, nb, seq, p["ssm_cw"], p["ssm_cb"], p["ssm_dtb"], p["ssm_alog"], p["ssm_d"], p["ssm_nw"],
                 st_s[:, l])[0]
    return (a_out, b_out, c_out, d_out), None


def kernel(x_prompt, x_sample, cache_na_k, cache_na_v, cache_diff_k, cache_diff_v, state_mlstm_C, state_mlstm_n, state_mlstm_m, state_ssm, c, c_ctx, norm1_w, norm2_w, w_ada, b_ada, w_in, mlstm_gate_b, mlstm_norm_w, na_qnorm_w, na_knorm_w, na_rpb, diff_qnorm_w, diff_knorm_w, diff_lambda, diff_norm_w, ssm_conv_w, ssm_conv_b, ssm_dt_bias, ssm_a_log, ssm_d, ssm_norm_w, w_branch, w_out, w_router, w_gate, w_up, w_down):
    a = dict(norm1_w=norm1_w, norm2_w=norm2_w, w_in=w_in, mlstm_gate_b=mlstm_gate_b, mlstm_norm_w=mlstm_norm_w,
             na_qnorm_w=na_qnorm_w, na_knorm_w=na_knorm_w, na_rpb=na_rpb, diff_qnorm_w=diff_qnorm_w,
             diff_knorm_w=diff_knorm_w, diff_lambda=diff_lambda, diff_norm_w=diff_norm_w, ssm_conv_w=ssm_conv_w,
             ssm_conv_b=ssm_conv_b, ssm_dt_bias=ssm_dt_bias, ssm_a_log=ssm_a_log, ssm_d=ssm_d,
             ssm_norm_w=ssm_norm_w, w_branch=w_branch, w_out=w_out, w_router=w_router, w_gate=w_gate, w_up=w_up,
             w_down=w_down)
    nb_p, nb_s = x_prompt.shape[0], x_sample.shape[0]
    xp = x_prompt.reshape(nb_p * SEQ, D_MODEL)
    xs = x_sample.reshape(nb_s * DEC_SEQ, D_MODEL)
    cc = jnp.concatenate([c_ctx[None, :], c, jnp.zeros((8 - 1 - nb_s, D_MODEL), f32)], axis=0)
    ctx = (cache_na_k.reshape(nb_s, DEPTH, PAST_LEN, MIX_W), cache_na_v.reshape(nb_s, DEPTH, PAST_LEN, MIX_W),
           cache_diff_k.reshape(nb_s, DEPTH, PAST_LEN, MIX_W), cache_diff_v.reshape(nb_s, DEPTH, PAST_LEN, MIX_W),
           state_mlstm_C, state_mlstm_n, state_mlstm_m, state_ssm)
    tables = _rope_tables()
    new = []
    for l in range(DEPTH):
        p = _layer_params(l, a)
        lam_init = 0.8 - 0.6 * math.exp(-0.3 * l)
        mod3 = _modulation(cc, w_ada, b_ada, l).reshape(8, 1, 6 * D_MODEL)
        groups = []
        for (x, nb, seq, base, stride, gctx) in ((xp, nb_p, SEQ, 0, 0, None), (xs, nb_s, DEC_SEQ, 1, 1, ctx)):
            u, g = _inproj(x, mod3, base, stride, seq, p["norm1"], p["w_main"], p["w_side"])
            brs, st = _mixers(u, g, nb, seq, p, l, lam_init, gctx, tables)
            if st is not None:
                new.append(st)
            x1, h2, aff = _merge(x, brs, u, mod3, base, stride, seq, p["wb"], p["wo"], p["norm2"], p["wr"])
            xg, gg, rank = _route(aff, h2, nb, seq)
            groups.append((x1, xg, gg, rank, nb, seq, base, stride))
        yp, ys = _ffn(groups[0][1], groups[1][1], groups[0][2], groups[1][2], p["w_gate"], p["w_up"], p["w_down"], l)
        outs = []
        for (x1, _, _, rank, nb, seq, base, stride), y in zip(groups, (yp, ys)):
            outs.append(_combine(x1, rank, y, mod3, base, stride, nb, seq))
        xp, xs = outs
    stack = lambda i, shape: jnp.stack([new[l][i].reshape(shape) for l in range(DEPTH)], axis=1)
    kv = (nb_p, SEQ, HEADS, HEAD_W)
    return (xp.reshape(x_prompt.shape), xs.reshape(x_sample.shape),
            stack(0, kv), stack(1, kv), stack(2, kv), stack(3, kv),
            stack(4, (nb_p, 2, HEADS, HEAD_W, HEAD_W)), stack(5, (nb_p, 2, HEADS, HEAD_W)),
            stack(6, (nb_p, 2, HEADS)), stack(7, (nb_p, 2, HEADS, HEAD_W, SSM_STATE)))
```

```python
import functools
import math

import jax
import jax.numpy as jnp
from jax import lax
from jax.experimental import pallas as pl
from jax.experimental.pallas import tpu as pltpu

f32 = jnp.float32
bf16 = jnp.bfloat16

D_MODEL = 1024
BATCH = 32
SEQ = 256
DEPTH = 2
DEC_BATCH = 4
DEC_SEQ = 1024
PAST_LEN = 512
GRID_W = 64
MIX_W = D_MODEL // 4
N_BRANCH = 4
HEADS = 4
HEAD_W = MIX_W // HEADS
NA_WIN_R = 8
NA_WIN_C = 16
NA_ROWS_PER_STEP = 2
DIFF_QK_DIM = HEAD_W // 2
SSM_GROUPS = 2
SSM_STATE = 64
SSM_CONV_K = 5
SSM_CONV_CH = MIX_W + 2 * SSM_GROUPS * SSM_STATE
CHUNK = 64
N_EXPERTS = 16
EXPERT_FF = 1024
CAPACITY_FACTOR = 2
ROPE_BASE = 10000.0
EPS = 1e-6
NEG = -1e30

LANE = 128
VMEM_LIMIT = 56 * 1024 * 1024

COL_GPRE = 0
COL_MLSTM = 4096
COL_NA = 5120
COL_DIFF = 5888
COL_XBC = 6656
COL_Z = 7168
U_USED = 7424
U_COLS = 7680
G_MLSTM = 0
G_DT = 16


def _params(sem):
    return pltpu.CompilerParams(dimension_semantics=sem, vmem_limit_bytes=VMEM_LIMIT)


def _mm(a, b):
    return jnp.dot(a.astype(bf16), b.astype(bf16), preferred_element_type=f32)


def _mm_nt(a, b):
    return lax.dot_general(a.astype(bf16), b.astype(bf16), (((1,), (1,)), ((), ())), preferred_element_type=f32)


def _mm_tn(a, b):
    return lax.dot_general(a.astype(bf16), b.astype(bf16), (((0,), (0,)), ((), ())), preferred_element_type=f32)


def _seg_sum(x, seg):
    w = x.shape[-1]
    r = lax.broadcasted_iota(jnp.int32, (w, w), 0) // seg
    c = lax.broadcasted_iota(jnp.int32, (w, w), 1) // seg
    ones = jnp.where(r == c, 1.0, 0.0).astype(bf16)
    hi = x.astype(bf16)
    lo = (x - hi.astype(f32)).astype(bf16)
    return jnp.dot(hi, ones, preferred_element_type=f32) + jnp.dot(lo, ones, preferred_element_type=f32)


def _seg_rms(x, seg, w_row):
    ms = _seg_sum(x * x, seg) * (1.0 / seg)
    return x * lax.rsqrt(ms + EPS) * w_row


def _mod_body(c_ref, w_ref, b_ref, o_ref):
    o_ref[...] = _mm(jax.nn.silu(c_ref[...]), w_ref[...]) + b_ref[...]


def _modulation(cc, w_ada, b_ada, layer):
    tn = 1536
    n = w_ada.shape[2]
    return pl.pallas_call(
        _mod_body,
        grid=(n // tn,),
        in_specs=[pl.BlockSpec((8, D_MODEL), lambda j: (0, 0)),
                  pl.BlockSpec((None, D_MODEL, tn), lambda j: (layer, 0, j)),
                  pl.BlockSpec((None, 1, tn), lambda j: (layer, 0, j))],
        out_specs=pl.BlockSpec((8, tn), lambda j: (0, j)),
        out_shape=jax.ShapeDtypeStruct((8, n), f32),
        compiler_params=_params(("arbitrary",)),
        name="modulation",
    )(cc, w_ada, b_ada.reshape(b_ada.shape[0], 1, n))


def _inproj_body(ncol, x_ref, mod_ref, nw_ref, w_ref, wg_ref, u_ref, g_ref):
    x = x_ref[...]
    y = x * lax.rsqrt(jnp.mean(x * x, axis=-1, keepdims=True) + EPS) * nw_ref[...]
    hb = (y * (1.0 + mod_ref[:, D_MODEL:2 * D_MODEL]) + mod_ref[:, 0:D_MODEL]).astype(bf16)
    g_ref[...] = jnp.dot(hb, wg_ref[...], preferred_element_type=f32)
    cw = U_COLS // ncol
    for c in range(ncol):
        u_ref[:, c * cw:(c + 1) * cw] = jnp.dot(hb, w_ref[:, c * cw:(c + 1) * cw],
                                                preferred_element_type=f32).astype(u_ref.dtype)


def _inproj(x, mod3, mod_base, mod_stride, seq, nw_row, w_main, w_side):
    n = x.shape[0]
    tm = 512
    tps = max(seq // tm, 1)
    resident = lambda shape: pl.BlockSpec(shape, lambda i: (0, 0), pipeline_mode=pl.Buffered(1))
    return pl.pallas_call(
        functools.partial(_inproj_body, 6),
        grid=(n // tm,),
        in_specs=[pl.BlockSpec((tm, D_MODEL), lambda i: (i, 0)),
                  pl.BlockSpec((None, 1, 6 * D_MODEL), lambda i: (mod_base + mod_stride * (i // tps), 0, 0)),
                  pl.BlockSpec((1, D_MODEL), lambda i: (0, 0)),
                  resident((D_MODEL, U_COLS)), resident((D_MODEL, LANE))],
        out_specs=[pl.BlockSpec((tm, U_COLS), lambda i: (i, 0)),
                   pl.BlockSpec((tm, LANE), lambda i: (i, 0))],
        out_shape=[jax.ShapeDtypeStruct((n, U_COLS), bf16), jax.ShapeDtypeStruct((n, LANE), f32)],
        compiler_params=_params(("arbitrary",)),
        name="inproj",
    )(x, mod3, nw_row, w_main, w_side)


def _split3(x):
    hi = x.astype(bf16)
    r = x - hi.astype(f32)
    mid = r.astype(bf16)
    lo = (r - mid.astype(f32)).astype(bf16)
    return hi, mid, lo


def _select_cols(x, onehot):
    return jnp.dot(jnp.concatenate(_split3(x), axis=1), jnp.concatenate([onehot] * 3, axis=0),
                   preferred_element_type=f32)


def _select_rows(onehot, x):
    return jnp.dot(jnp.concatenate([onehot] * 3, axis=1), jnp.concatenate(_split3(x), axis=0),
                   preferred_element_type=f32)


def _chunk_scan(x, op, identity, reverse):
    n = x.shape[0]
    pos = lax.broadcasted_iota(jnp.int32, x.shape, 0) % CHUNK
    k = 1
    while k < CHUNK:
        if reverse:
            shifted, ok = pltpu.roll(x, n - k, 0), pos < CHUNK - k
        else:
            shifted, ok = pltpu.roll(x, k, 0), pos >= k
        x = op(x, jnp.where(ok, shifted, identity))
        k *= 2
    return x


def _dir_scan(x, op, identity, bwd_col):
    return jnp.where(bwd_col, _chunk_scan(x, op, identity, True), _chunk_scan(x, op, identity, False))


def _head_expand(col0):
    c = lax.broadcasted_iota(jnp.int32, (LANE, MIX_W), 0)
    h = lax.broadcasted_iota(jnp.int32, (LANE, MIX_W), 1) // HEAD_W
    return jnp.where(c == col0 + h, 1.0, 0.0).astype(bf16)


def _chunk_rows(x, seq):
    nrow = max(seq // CHUNK, 8)
    t = lax.broadcasted_iota(jnp.int32, (seq, MIX_W), 0) % CHUNK
    s_ = lax.broadcasted_iota(jnp.int32, (seq, MIX_W), 1) % HEAD_W
    sel = lax.broadcasted_iota(jnp.int32, (nrow, seq), 1) // CHUNK == lax.broadcasted_iota(jnp.int32, (nrow, seq), 0)
    return _select_rows(jnp.where(sel, 1.0, 0.0).astype(bf16), jnp.where(t == s_, x, 0.0))


def _block_diag(x, reps):
    r, c = x.shape
    t = jnp.concatenate([x] * reps, axis=0)
    rb = lax.broadcasted_iota(jnp.int32, t.shape, 0) // r
    cb = lax.broadcasted_iota(jnp.int32, t.shape, 1) // (c // reps)
    return jnp.where(rb == cb, t, jnp.zeros_like(t))


def _head_mask(rows, cols, rseg, cseg):
    return (lax.broadcasted_iota(jnp.int32, (rows, cols), 0) // rseg) == (lax.broadcasted_iota(jnp.int32, (rows, cols), 1) // cseg)


def _mlstm_body(seq, has_state, *refs):
    if has_state:
        (qkvo_ref, g_ref, gb_ref, nw_ref, c0_ref, n0_ref, m0_ref,
         out_ref, cs_ref, ns_ref, ms_ref, bx_scr, mx_scr, vx_scr, vrow_scr, cbd_scr, hf_scr, hb_scr) = refs
    else:
        (qkvo_ref, g_ref, gb_ref, nw_ref,
         out_ref, cs_ref, ns_ref, ms_ref, bx_scr, mx_scr, vx_scr, vrow_scr, cbd_scr, hf_scr, hb_scr) = refs
    nc = seq // CHUNK
    g = g_ref[...] + gb_ref[...]
    lane = lax.broadcasted_iota(jnp.int32, g.shape, 1)
    bwd_col = (lane % 16) >= 8
    bsum = _dir_scan(jax.nn.log_sigmoid(g), jnp.add, 0.0, bwd_col)
    b_i = pltpu.roll(bsum, LANE - HEADS, 1)
    vcol = g - b_i
    mcol = b_i + _dir_scan(vcol, jnp.maximum, NEG, bwd_col)
    for d in range(2):
        e = _head_expand(G_MLSTM + d * 8)
        bx_scr[d] = _select_cols(b_i, e)
        mx_scr[d] = _select_cols(mcol, e)
        vx = _select_cols(vcol, e)
        vx_scr[d] = vx
        vrow = _chunk_rows(vx, seq)
        for c in range(nc):
            vrow_scr[d, c] = vrow[c:c + 1, :]

    grp = lax.broadcasted_iota(jnp.int32, (1, MIX_W), 1) // HEAD_W
    cbd_scr[...] = jnp.zeros(cbd_scr.shape, f32)
    n_rows, m_rows = [], []
    for d in range(2):
        if has_state:
            for h in range(HEADS):
                cbd_scr[d, h * HEAD_W:(h + 1) * HEAD_W, h * HEAD_W:(h + 1) * HEAD_W] = c0_ref[d, h]
            n_rows.append(jnp.concatenate([n0_ref[d, h:h + 1, :] for h in range(HEADS)], axis=1))
            m_row = jnp.zeros((1, MIX_W), f32)
            for h in range(HEADS):
                m_row = jnp.where(grp == h, m0_ref[d:d + 1, h:h + 1], m_row)
            m_rows.append(m_row)
        else:
            n_rows.append(jnp.zeros((1, MIX_W), f32))
            m_rows.append(jnp.zeros((1, MIX_W), f32))

    li = lax.broadcasted_iota(jnp.int32, (CHUNK, MIX_W), 0)
    si = lax.broadcasted_iota(jnp.int32, (CHUNK, MIX_W), 1) % HEAD_W
    valid = (si <= li, si >= li)
    bd = _head_mask(MIX_W, MIX_W, HEAD_W, HEAD_W)
    ones_bd = jnp.where(bd, 1.0, 0.0).astype(bf16)

    def step(c, carry):
        new = []
        for d in range(2):
            n_row, m_row = carry[2 * d], carry[2 * d + 1]
            cidx = c if d == 0 else nc - 1 - c
            r0 = pl.multiple_of(cidx * CHUNK, CHUNK)
            last = CHUNK - 1 if d == 0 else 0
            q = qkvo_ref[pl.ds(r0, CHUNK), 0:MIX_W].astype(f32)
            k = qkvo_ref[pl.ds(r0, CHUNK), MIX_W:2 * MIX_W].astype(f32) * (HEAD_W ** -0.5)
            vb = qkvo_ref[pl.ds(r0, CHUNK), 2 * MIX_W:3 * MIX_W].astype(bf16)
            qb = q.astype(bf16)
            bx = bx_scr[d, pl.ds(r0, CHUNK), :]
            mx = mx_scr[d, pl.ds(r0, CHUNK), :]
            vx = vx_scr[d, pl.ds(r0, CHUNK), :]
            p = jnp.exp(jnp.where(valid[d], bx + vrow_scr[d, cidx] - mx, NEG))
            smat = _mm_nt(qb, _block_diag(k.astype(bf16), HEADS)) * p
            intra = _mm(smat, _block_diag(vb, HEADS))
            dsum = _mm(smat, ones_bd)
            gg = bx + m_row
            mt = jnp.maximum(gg, mx)
            a = jnp.exp(mx - mt)
            w_prev = jnp.exp(gg - mt)
            inter = _mm(qb, cbd_scr[d])
            dint = _mm(q * n_row, ones_bd)
            num = intra * a + inter * w_prev
            den = dsum * a + dint * w_prev
            h_scr = hf_scr if d == 0 else hb_scr
            h_scr[pl.ds(r0, CHUNK), :] = num / jnp.maximum(jnp.abs(den), jnp.exp(-mt))
            bl = bx[last:last + 1, :]
            mloc = mx[last:last + 1, :]
            m_new = jnp.maximum(bl + m_row, mloc)
            dec = jnp.exp(bl + m_row - m_new)
            fac = jnp.exp(mloc - m_new)
            kw = k * jnp.exp(bl + vx - mloc)
            kv = jnp.where(bd, _mm_tn(kw, vb), 0.0)
            cbd_scr[d] = cbd_scr[d] * dec + kv * fac
            new += [n_row * dec + jnp.sum(kw, axis=0, keepdims=True) * fac, m_new]
        return tuple(new)

    fin_state = lax.fori_loop(0, nc, step, (n_rows[0], m_rows[0], n_rows[1], m_rows[1]))
    for d in range(2):
        n_row, m_row = fin_state[2 * d], fin_state[2 * d + 1]
        for h in range(HEADS):
            hl = slice(h * HEAD_W, (h + 1) * HEAD_W)
            cs_ref[d, h] = cbd_scr[d, hl, hl]
            ns_ref[d, h:h + 1, :] = n_row[:, hl]
            ms_ref[d:d + 1, h:h + 1] = m_row[:, h * HEAD_W:h * HEAD_W + 1]

    blk = 256

    def fin(i, carry):
        r0 = pl.multiple_of(i * blk, blk)
        hs = hf_scr[pl.ds(r0, blk), :] + hb_scr[pl.ds(r0, blk), :]
        hn = _seg_rms(hs, HEAD_W, nw_ref[...])
        o = qkvo_ref[pl.ds(r0, blk), 3 * MIX_W:4 * MIX_W].astype(f32)
        out_ref[pl.ds(r0, blk), :] = (hn * jax.nn.sigmoid(o)).astype(out_ref.dtype)
        return carry

    lax.fori_loop(0, seq // blk, fin, 0)


def _mlstm(u, g, nb, seq, gb_row, nw_row, state):
    has_state = state is not None
    nc = seq // CHUNK
    in_specs = [pl.BlockSpec((seq, 4 * MIX_W), lambda b: (b, COL_MLSTM // (4 * MIX_W))),
                pl.BlockSpec((seq, LANE), lambda b: (b, 0)),
                pl.BlockSpec((1, LANE), lambda b: (0, 0)),
                pl.BlockSpec((1, MIX_W), lambda b: (0, 0))]
    args = [u, g, gb_row, nw_row]
    if has_state:
        in_specs += [pl.BlockSpec((None, 2, HEADS, HEAD_W, HEAD_W), lambda b: (b, 0, 0, 0, 0)),
                     pl.BlockSpec((None, 2, HEADS, HEAD_W), lambda b: (b, 0, 0, 0)),
                     pl.BlockSpec((None, 2, HEADS), lambda b: (b, 0, 0))]
        args += list(state)
    return pl.pallas_call(
        functools.partial(_mlstm_body, seq, has_state),
        grid=(nb,),
        in_specs=in_specs,
        out_specs=[pl.BlockSpec((seq, MIX_W), lambda b: (b, 0)),
                   pl.BlockSpec((None, 2, HEADS, HEAD_W, HEAD_W), lambda b: (b, 0, 0, 0, 0)),
                   pl.BlockSpec((None, 2, HEADS, HEAD_W), lambda b: (b, 0, 0, 0)),
                   pl.BlockSpec((None, 2, HEADS), lambda b: (b, 0, 0))],
        out_shape=[jax.ShapeDtypeStruct((nb * seq, MIX_W), bf16),
                   jax.ShapeDtypeStruct((nb, 2, HEADS, HEAD_W, HEAD_W), f32),
                   jax.ShapeDtypeStruct((nb, 2, HEADS, HEAD_W), f32),
                   jax.ShapeDtypeStruct((nb, 2, HEADS), f32)],
        scratch_shapes=[pltpu.VMEM((2, seq, MIX_W), f32), pltpu.VMEM((2, seq, MIX_W), f32),
                        pltpu.VMEM((2, seq, MIX_W), f32), pltpu.VMEM((2, nc, 1, MIX_W), f32),
                        pltpu.VMEM((2, MIX_W, MIX_W), f32),
                        pltpu.VMEM((seq, MIX_W), f32), pltpu.VMEM((seq, MIX_W), f32)],
        compiler_params=_params(("arbitrary",)),
        name="mlstm_state" if has_state else "mlstm",
    )(*args)


def _ssd_body(seq, has_state, *refs):
    if has_state:
        (z_ref, xbc_ref, g_ref, cw_ref, cb_ref, dtb_ref, alog_ref, dskip_ref, nw_ref, h0_ref,
         out_ref, hs_ref, xpad_scr, xc_scr, ax_scr, dx_scr, arow_scr, ht_scr, yf_scr, yb_scr) = refs
    else:
        (z_ref, xbc_ref, g_ref, cw_ref, cb_ref, dtb_ref, alog_ref, dskip_ref, nw_ref,
         out_ref, hs_ref, xpad_scr, xc_scr, ax_scr, dx_scr, arow_scr, ht_scr, yf_scr, yb_scr) = refs
    nc = seq // CHUNK
    pad = 8
    blk = 256
    nblk = seq // blk
    xpad_scr[0:pad, :] = jnp.zeros((pad, SSM_CONV_CH), f32)
    xpad_scr[pad + seq:2 * pad + seq, :] = jnp.zeros((pad, SSM_CONV_CH), f32)
    xpad_scr[pad:pad + seq, :] = xbc_ref[...].astype(f32)

    cblk = 128
    for i in range(seq // cblk):
        r0 = i * cblk
        acc = jnp.zeros((cblk, SSM_CONV_CH), f32) + cb_ref[...]
        for kk in range(SSM_CONV_K):
            off = r0 + pad - SSM_CONV_K // 2 + kk
            acc = acc + xpad_scr[off:off + cblk, :] * cw_ref[kk:kk + 1, :]
        xc_scr[r0:r0 + cblk, :] = jax.nn.silu(acc)

    dt = jax.nn.softplus(g_ref[...] + dtb_ref[...])
    lane = lax.broadcasted_iota(jnp.int32, dt.shape, 1)
    acum = _dir_scan(dt * (-jnp.exp(alog_ref[...])), jnp.add, 0.0, (lane % 8) >= HEADS)
    for d in range(2):
        e = _head_expand(G_DT + d * HEADS)
        ax = _select_cols(acum, e)
        ax_scr[d] = ax
        dx_scr[d] = _select_cols(dt, e)
        arow = _chunk_rows(ax, seq)
        for c in range(nc):
            arow_scr[d, c] = arow[c:c + 1, :]

    gn = SSM_GROUPS * SSM_STATE
    rep = HEADS // SSM_GROUPS
    ht_scr[...] = jnp.zeros(ht_scr.shape, f32)
    if has_state:
        for d in range(2):
            for h in range(HEADS):
                g0 = (h // rep) * SSM_STATE
                ht_scr[d, g0:g0 + SSM_STATE, h * HEAD_W:(h + 1) * HEAD_W] = h0_ref[d, h].T
    li = lax.broadcasted_iota(jnp.int32, (CHUNK, MIX_W), 0)
    si = lax.broadcasted_iota(jnp.int32, (CHUNK, MIX_W), 1) % HEAD_W
    valid = (si <= li, si >= li)
    state_mask = _head_mask(gn, MIX_W, SSM_STATE, rep * HEAD_W)
    key_mask = _head_mask(HEADS * CHUNK, gn, rep * CHUNK, SSM_STATE)

    def step(c, carry):
        for d in range(2):
            cidx = c if d == 0 else nc - 1 - c
            r0 = pl.multiple_of(cidx * CHUNK, CHUNK)
            last = CHUNK - 1 if d == 0 else 0
            xs = xc_scr[pl.ds(r0, CHUNK), 0:MIX_W]
            bm = xc_scr[pl.ds(r0, CHUNK), MIX_W:MIX_W + gn].astype(bf16)
            cm = xc_scr[pl.ds(r0, CHUNK), MIX_W + gn:MIX_W + 2 * gn].astype(bf16)
            ax = ax_scr[d, pl.ds(r0, CHUNK), :]
            decay = jnp.exp(jnp.where(valid[d], ax - arow_scr[d, cidx], NEG))
            bexp = jnp.concatenate([bm] * HEADS, axis=0)
            bexp = jnp.where(key_mask, bexp, jnp.zeros_like(bexp))
            scores = _mm_nt(cm, bexp) * decay
            xdt = xs * dx_scr[d, pl.ds(r0, CHUNK), :]
            y = _mm(scores, _block_diag(xdt.astype(bf16), HEADS)) + _mm(cm, ht_scr[d]) * jnp.exp(ax)
            y_scr = yf_scr if d == 0 else yb_scr
            y_scr[pl.ds(r0, CHUNK), :] = y
            al = ax[last:last + 1, :]
            upd = jnp.where(state_mask, _mm_tn(bm, xdt * jnp.exp(al - ax)), 0.0)
            ht_scr[d] = ht_scr[d] * jnp.exp(al) + upd
        return carry

    lax.fori_loop(0, nc, step, 0)
    for d in range(2):
        for h in range(HEADS):
            g0 = (h // rep) * SSM_STATE
            hs_ref[d, h] = ht_scr[d, g0:g0 + SSM_STATE, h * HEAD_W:(h + 1) * HEAD_W].T

    def fin(i, carry):
        r0 = pl.multiple_of(i * blk, blk)
        y = yf_scr[pl.ds(r0, blk), :] + yb_scr[pl.ds(r0, blk), :] + dskip_ref[...] * xc_scr[pl.ds(r0, blk), 0:MIX_W]
        y = y * jax.nn.silu(z_ref[pl.ds(r0, blk), :].astype(f32))
        y = y * lax.rsqrt(jnp.mean(y * y, axis=-1, keepdims=True) + EPS) * nw_ref[...]
        out_ref[pl.ds(r0, blk), :] = y.astype(out_ref.dtype)
        return carry

    lax.fori_loop(0, nblk, fin, 0)


def _ssd(u, g, nb, seq, cw, cb_row, dtb_row, alog_row, dskip_row, nw_row, state):
    has_state = state is not None
    nc = seq // CHUNK
    const = lambda shape: pl.BlockSpec(shape, lambda b: (0,) * len(shape))
    in_specs = [pl.BlockSpec((seq, MIX_W), lambda b: (b, COL_Z // MIX_W)),
                pl.BlockSpec((seq, SSM_CONV_CH), lambda b: (b, COL_XBC // SSM_CONV_CH)),
                pl.BlockSpec((seq, LANE), lambda b: (b, 0)),
                const((SSM_CONV_K, SSM_CONV_CH)), const((1, SSM_CONV_CH)), const((1, LANE)), const((1, LANE)),
                const((1, MIX_W)), const((1, MIX_W))]
    args = [u, u, g, cw, cb_row, dtb_row, alog_row, dskip_row, nw_row]
    if has_state:
        in_specs.append(pl.BlockSpec((None, 2, HEADS, HEAD_W, SSM_STATE), lambda b: (b, 0, 0, 0, 0)))
        args.append(state)
    return pl.pallas_call(
        functools.partial(_ssd_body, seq, has_state),
        grid=(nb,),
        in_specs=in_specs,
        out_specs=[pl.BlockSpec((seq, MIX_W), lambda b: (b, 0)),
                   pl.BlockSpec((None, 2, HEADS, HEAD_W, SSM_STATE), lambda b: (b, 0, 0, 0, 0))],
        out_shape=[jax.ShapeDtypeStruct((nb * seq, MIX_W), bf16),
                   jax.ShapeDtypeStruct((nb, 2, HEADS, HEAD_W, SSM_STATE), f32)],
        scratch_shapes=[pltpu.VMEM((seq + 16, SSM_CONV_CH), f32), pltpu.VMEM((seq, SSM_CONV_CH), f32),
                        pltpu.VMEM((2, seq, MIX_W), f32), pltpu.VMEM((2, seq, MIX_W), f32),
                        pltpu.VMEM((2, nc, 1, MIX_W), f32),
                        pltpu.VMEM((2, SSM_GROUPS * SSM_STATE, MIX_W), f32),
                        pltpu.VMEM((seq, MIX_W), f32), pltpu.VMEM((seq, MIX_W), f32)],
        compiler_params=_params(("arbitrary",)),
        name="ssd_state" if has_state else "ssd",
    )(*args)


def _with_ones(v):
    vb = v.astype(bf16)
    ones = jnp.ones((v.shape[0], HEAD_W), bf16)
    return jnp.concatenate([x for h in range(HEADS) for x in (vb[:, h * HEAD_W:(h + 1) * HEAD_W], ones)], axis=1)


def _softmax_av(scores, vexts):
    m = scores[0].max(axis=1, keepdims=True)
    for s in scores[1:]:
        m = jnp.maximum(m, s.max(axis=1, keepdims=True))
    r = None
    for s, vx in zip(scores, vexts):
        t = jnp.dot(jnp.exp(s - m).astype(bf16), vx, preferred_element_type=f32)
        r = t if r is None else r + t
    return r[:, :HEAD_W] / r[:, HEAD_W:]


def _na_prompt_body(q_ref, k_ref, v_ref, qn_ref, kn_ref, o_ref, ko_ref, vo_ref):
    scale = HEAD_W ** -0.5
    qb = (_seg_rms(q_ref[...].astype(f32), HEAD_W, qn_ref[...]) * scale).astype(bf16)
    kn = _seg_rms(k_ref[...].astype(f32), HEAD_W, kn_ref[...])
    v = v_ref[...].astype(f32)
    ko_ref[...] = kn
    vo_ref[...] = v
    kt = kn.T.astype(bf16)
    vx = _with_ones(v)
    for h in range(HEADS):
        hl = slice(h * HEAD_W, (h + 1) * HEAD_W)
        s = jnp.dot(qb[:, hl], kt[hl, :], preferred_element_type=f32)
        o_ref[:, hl] = _softmax_av([s], [vx[:, 2 * h * HEAD_W:2 * (h + 1) * HEAD_W]]).astype(o_ref.dtype)


def _na_prompt(u, nb, seq, qn_row, kn_row):
    cb = COL_NA // MIX_W
    blk = lambda j: pl.BlockSpec((seq, MIX_W), lambda b: (b, j))
    const = pl.BlockSpec((1, MIX_W), lambda b: (0, 0))
    sds = jax.ShapeDtypeStruct((nb * seq, MIX_W), f32)
    return pl.pallas_call(
        _na_prompt_body,
        grid=(nb,),
        in_specs=[blk(cb), blk(cb + 1), blk(cb + 2), const, const],
        out_specs=[blk(0), blk(0), blk(0)],
        out_shape=[jax.ShapeDtypeStruct(sds.shape, bf16), sds, sds],
        compiler_params=_params(("arbitrary",)),
        name="na_prompt",
    )(u, u, u, qn_row, kn_row)


def _na_sample_body(q_ref, k_ref, v_ref, kc_ref, vc_ref, rpb_ref, qn_ref, kn_ref, o_ref,
                    kh_scr, kch_scr, vx_scr, vcx_scr, toe_scr):
    b = pl.program_id(0)
    i = pl.program_id(1)
    ndr = 2 * NA_WIN_R - 1

    @pl.when((b == 0) & (i == 0))
    def _():
        j = lax.broadcasted_iota(jnp.int32, (GRID_W, LANE), 0)
        lane = lax.broadcasted_iota(jnp.int32, (GRID_W, LANE), 1)
        j2 = lane % GRID_W
        cs = jnp.clip(j - NA_WIN_C // 2, 0, GRID_W - NA_WIN_C)
        ok = (j2 >= cs) & (j2 < cs + NA_WIN_C)
        for h in range(HEADS):
            blocks = []
            for dr in range(ndr):
                row = jnp.broadcast_to(rpb_ref[h, dr:dr + 1, :], (GRID_W, LANE))
                blocks.append((pltpu.roll(row, LANE - (NA_WIN_C - 1), 1, stride=1, stride_axis=0),
                               pltpu.roll(row, GRID_W - (NA_WIN_C - 1), 1, stride=1, stride_axis=0)))
            for dr in range(ndr - 1):
                toe_scr[h, dr] = jnp.where(ok, jnp.where(lane < GRID_W, blocks[dr][0], blocks[dr + 1][1]), NEG)

    @pl.when(i == 0)
    def _():
        kn = _seg_rms(k_ref[...].astype(f32), HEAD_W, kn_ref[...]).astype(bf16)
        kc = kc_ref[...].astype(bf16)
        for h in range(HEADS):
            kh_scr[h] = kn[:, h * HEAD_W:(h + 1) * HEAD_W]
            kch_scr[h] = kc[:, h * HEAD_W:(h + 1) * HEAD_W]
        vx_scr[...] = _with_ones(v_ref[...])
        vcx_scr[...] = _with_ones(vc_ref[...])

    rows = DEC_SEQ // GRID_W
    nwin = NA_WIN_R * GRID_W
    qb = (_seg_rms(q_ref[...].astype(f32), HEAD_W, qn_ref[...]) * (HEAD_W ** -0.5)).astype(bf16)
    for qr in range(NA_ROWS_PER_STEP):
        row = i * NA_ROWS_PER_STEP + qr
        rs = jnp.clip(row - NA_WIN_R // 2, 0, rows - NA_WIN_R)
        d0 = rs - row + (NA_WIN_R - 1)
        k0 = pl.multiple_of(rs * GRID_W, GRID_W)
        qs = slice(qr * GRID_W, (qr + 1) * GRID_W)
        for h in range(HEADS):
            hl = slice(h * HEAD_W, (h + 1) * HEAD_W)
            xl = slice(2 * h * HEAD_W, 2 * (h + 1) * HEAD_W)
            bias = jnp.concatenate([toe_scr[h, d0 + 2 * r] for r in range(NA_WIN_R // 2)], axis=1)
            s_win = _mm_nt(qb[qs, hl], kh_scr[h, pl.ds(k0, nwin), :]) + bias
            s_ctx = _mm_nt(qb[qs, hl], kch_scr[h])
            o = _softmax_av([s_win, s_ctx], [vx_scr[pl.ds(k0, nwin), xl], vcx_scr[:, xl]])
            o_ref[qs, hl] = o.astype(o_ref.dtype)


def _na_sample(u, nb, layer, cache_k, cache_v, rpb_pad, qn_row, kn_row):
    seq = DEC_SEQ
    rows = seq // (GRID_W * NA_ROWS_PER_STEP)
    qrows = GRID_W * NA_ROWS_PER_STEP
    cb = COL_NA // MIX_W
    ndr = 2 * NA_WIN_R - 1
    full = lambda j: pl.BlockSpec((seq, MIX_W), lambda b, i: (b, j))
    ctx = pl.BlockSpec((None, None, PAST_LEN, MIX_W), lambda b, i: (b, layer, 0, 0))
    const = pl.BlockSpec((1, MIX_W), lambda b, i: (0, 0))
    return pl.pallas_call(
        _na_sample_body,
        grid=(nb, rows),
        in_specs=[pl.BlockSpec((qrows, MIX_W), lambda b, i: (b * rows + i, cb)), full(cb + 1), full(cb + 2),
                  ctx, ctx,
                  pl.BlockSpec((HEADS, ndr + 1, LANE), lambda b, i: (0, 0, 0)),
                  const, const],
        out_specs=pl.BlockSpec((qrows, MIX_W), lambda b, i: (b * rows + i, 0)),
        out_shape=jax.ShapeDtypeStruct((nb * seq, MIX_W), bf16),
        scratch_shapes=[pltpu.VMEM((HEADS, seq, HEAD_W), bf16), pltpu.VMEM((HEADS, PAST_LEN, HEAD_W), bf16),
                        pltpu.VMEM((seq, 2 * MIX_W), bf16), pltpu.VMEM((PAST_LEN, 2 * MIX_W), bf16),
                        pltpu.VMEM((HEADS, ndr - 1, GRID_W, LANE), f32)],
        compiler_params=_params(("arbitrary", "arbitrary")),
        name="na_sample",
    )(u, u, u, cache_k, cache_v, rpb_pad, qn_row, kn_row)


def _na_rpb_pad(rpb):
    c = rpb.shape[2]
    return jnp.pad(rpb.astype(f32), ((0, 0), (0, 1), (0, LANE - c)))


def _diff_lambda(lam_ref, lam_init):
    lv = lam_ref[...]
    s1 = jnp.sum(lv[0:1, :] * lv[1:2, :], axis=1, keepdims=True)
    s2 = jnp.sum(lv[2:3, :] * lv[3:4, :], axis=1, keepdims=True)
    return jnp.exp(s1) - jnp.exp(s2) + lam_init


def _diff_heads(q, kt, vext, lam):
    outs = []
    for h in range(HEADS):
        o = []
        for m in range(2):
            c0 = h * HEAD_W + m * DIFF_QK_DIM
            s = jnp.dot(q[:, c0:c0 + DIFF_QK_DIM], kt[c0:c0 + DIFF_QK_DIM, :], preferred_element_type=f32)
            o.append(_softmax_av([s], [vext[:, 2 * h * HEAD_W:2 * (h + 1) * HEAD_W]]))
        outs.append(o[0] - lam * o[1])
    return jnp.concatenate(outs, axis=1)


def _diff_prompt_body(lam_init, q_ref, k_ref, v_ref, qn_ref, kn_ref, lam_ref, nw_ref, o_ref, ko_ref, vo_ref):
    lam = _diff_lambda(lam_ref, lam_init)
    qn = _seg_rms(q_ref[...].astype(f32), DIFF_QK_DIM, qn_ref[...]) * (DIFF_QK_DIM ** -0.5)
    kn = _seg_rms(k_ref[...].astype(f32), DIFF_QK_DIM, kn_ref[...])
    v = v_ref[...].astype(f32)
    ko_ref[...] = kn
    vo_ref[...] = v
    o = _diff_heads(qn.astype(bf16), kn.T.astype(bf16), _with_ones(v), lam)
    o_ref[...] = (_seg_rms(o, HEAD_W, nw_ref[...]) * (1.0 - lam_init)).astype(o_ref.dtype)


def _diff_prompt(u, nb, seq, lam_init, qn_row, kn_row, lam_vecs, nw_row):
    cb = COL_DIFF // MIX_W
    blk = lambda j: pl.BlockSpec((seq, MIX_W), lambda b: (b, j))
    const = pl.BlockSpec((1, MIX_W), lambda b: (0, 0))
    sds = jax.ShapeDtypeStruct((nb * seq, MIX_W), f32)
    return pl.pallas_call(
        functools.partial(_diff_prompt_body, lam_init),
        grid=(nb,),
        in_specs=[blk(cb), blk(cb + 1), blk(cb + 2), const, const,
                  pl.BlockSpec((4, DIFF_QK_DIM), lambda b: (0, 0)), const],
        out_specs=[blk(0), blk(0), blk(0)],
        out_shape=[jax.ShapeDtypeStruct(sds.shape, bf16), sds, sds],
        compiler_params=_params(("arbitrary",)),
        name="diff_prompt",
    )(u, u, u, qn_row, kn_row, lam_vecs, nw_row)


def _rope(x, cos, sin):
    w = x.shape[-1]
    up = pltpu.roll(x, w - 8, 1)
    dn = pltpu.roll(x, 8, 1)
    first = (lax.broadcasted_iota(jnp.int32, x.shape, 1) % 16) < 8
    return x * cos + jnp.where(first, up, dn) * sin


def _diff_sample_body(lam_init, qblk, q_ref, k_ref, v_ref, kc_ref, vc_ref, cos_ref, sin_ref, qn_ref, kn_ref,
                      lam_ref, nw_ref, o_ref, kt_scr, vx_scr):
    i = pl.program_id(1)
    seq = k_ref.shape[0]

    @pl.when(i == 0)
    def _():
        kn = _seg_rms(k_ref[...].astype(f32), DIFF_QK_DIM, kn_ref[...])
        kt_scr[:, 0:seq] = _rope(kn, cos_ref[...], sin_ref[...]).T.astype(bf16)
        kt_scr[:, seq:] = kc_ref[...].T.astype(bf16)
        vx_scr[0:seq, :] = _with_ones(v_ref[...])
        vx_scr[seq:, :] = _with_ones(vc_ref[...])

    lam = _diff_lambda(lam_ref, lam_init)
    r0 = pl.multiple_of(i * qblk, qblk)
    qn = _seg_rms(q_ref[...].astype(f32), DIFF_QK_DIM, qn_ref[...])
    qn = _rope(qn, cos_ref[pl.ds(r0, qblk), :], sin_ref[pl.ds(r0, qblk), :]) * (DIFF_QK_DIM ** -0.5)
    o = _diff_heads(qn.astype(bf16), kt_scr[...], vx_scr[...], lam)
    o_ref[...] = (_seg_rms(o, HEAD_W, nw_ref[...]) * (1.0 - lam_init)).astype(o_ref.dtype)


def _diff_sample(u, nb, layer, lam_init, cache_k, cache_v, cos, sin, qn_row, kn_row, lam_vecs, nw_row):
    seq = DEC_SEQ
    qblk = 64
    nq = seq // qblk
    cb = COL_DIFF // MIX_W
    full = lambda j: pl.BlockSpec((seq, MIX_W), lambda b, i: (b, j))
    ctx = pl.BlockSpec((None, None, PAST_LEN, MIX_W), lambda b, i: (b, layer, 0, 0))
    const = pl.BlockSpec((1, MIX_W), lambda b, i: (0, 0))
    tab = pl.BlockSpec((seq, MIX_W), lambda b, i: (0, 0))
    return pl.pallas_call(
        functools.partial(_diff_sample_body, lam_init, qblk),
        grid=(nb, nq),
        in_specs=[pl.BlockSpec((qblk, MIX_W), lambda b, i: (b * nq + i, cb)), full(cb + 1), full(cb + 2),
                  ctx, ctx, tab, tab, const, const,
                  pl.BlockSpec((4, DIFF_QK_DIM), lambda b, i: (0, 0)), const],
        out_specs=pl.BlockSpec((qblk, MIX_W), lambda b, i: (b * nq + i, 0)),
        out_shape=jax.ShapeDtypeStruct((nb * seq, MIX_W), bf16),
        scratch_shapes=[pltpu.VMEM((MIX_W, seq + PAST_LEN), bf16), pltpu.VMEM((seq + PAST_LEN, 2 * MIX_W), bf16)],
        compiler_params=_params(("arbitrary", "arbitrary")),
        name="diff_sample",
    )(u, u, u, cache_k, cache_v, cos, sin, qn_row, kn_row, lam_vecs, nw_row)


def _rope_tables():
    t = jnp.arange(DEC_SEQ)
    nf = DIFF_QK_DIM // 4
    inv = ROPE_BASE ** (-jnp.arange(nf, dtype=f32) / nf)
    ar = (t // GRID_W).astype(f32)[:, None] * inv
    ac = (t % GRID_W).astype(f32)[:, None] * inv
    cos = jnp.concatenate([jnp.cos(ar), jnp.cos(ar), jnp.cos(ac), jnp.cos(ac)], axis=1)
    sin = jnp.concatenate([-jnp.sin(ar), jnp.sin(ar), -jnp.sin(ac), jnp.sin(ac)], axis=1)
    reps = MIX_W // DIFF_QK_DIM
    return jnp.tile(cos, (1, reps)), jnp.tile(sin, (1, reps))


def _merge_body(x_ref, a_ref, b_ref, c_ref, d_ref, gp_ref, mod_ref, wb_ref, wo_ref, nw_ref, wr_ref,
                xo_ref, h2_ref, aff_ref):
    mixed = None
    for n, br in enumerate((a_ref, b_ref, c_ref, d_ref)):
        gate = jax.nn.sigmoid(gp_ref[:, n * D_MODEL:(n + 1) * D_MODEL].astype(f32))
        t = gate * jnp.dot(br[...].astype(bf16), wb_ref[n], preferred_element_type=f32)
        mixed = t if mixed is None else mixed + t
    y = jnp.dot(mixed.astype(bf16), wo_ref[...], preferred_element_type=f32)
    x = x_ref[...] + mod_ref[:, 2 * D_MODEL:3 * D_MODEL] * y
    xo_ref[...] = x
    h = x * lax.rsqrt(jnp.mean(x * x, axis=-1, keepdims=True) + EPS) * nw_ref[...]
    h = h * (1.0 + mod_ref[:, 4 * D_MODEL:5 * D_MODEL]) + mod_ref[:, 3 * D_MODEL:4 * D_MODEL]
    hb = h.astype(bf16)
    h2_ref[...] = hb
    logits = jnp.dot(hb, wr_ref[...], preferred_element_type=f32)
    lane = lax.broadcasted_iota(jnp.int32, logits.shape, 1)
    logits = jnp.where(lane < N_EXPERTS, logits, NEG)
    e = jnp.exp(logits - logits.max(axis=1, keepdims=True))
    aff_ref[...] = e / e.sum(axis=1, keepdims=True)


def _merge(x, brs, u, mod3, mod_base, mod_stride, seq, wb, wo, nw_row, wr):
    n = x.shape[0]
    tm = 256
    tps = seq // tm
    row = lambda w: pl.BlockSpec((tm, w), lambda i: (i, 0))
    return pl.pallas_call(
        _merge_body,
        grid=(n // tm,),
        in_specs=[row(D_MODEL), row(MIX_W), row(MIX_W), row(MIX_W), row(MIX_W),
                  pl.BlockSpec((tm, N_BRANCH * D_MODEL), lambda i: (i, COL_GPRE // (N_BRANCH * D_MODEL))),
                  pl.BlockSpec((None, 1, 6 * D_MODEL), lambda i: (mod_base + mod_stride * (i // tps), 0, 0)),
                  pl.BlockSpec((N_BRANCH, MIX_W, D_MODEL), lambda i: (0, 0, 0)),
                  pl.BlockSpec((D_MODEL, D_MODEL), lambda i: (0, 0)),
                  pl.BlockSpec((1, D_MODEL), lambda i: (0, 0)),
                  pl.BlockSpec((D_MODEL, LANE), lambda i: (0, 0))],
        out_specs=[row(D_MODEL), row(D_MODEL), row(LANE)],
        out_shape=[jax.ShapeDtypeStruct((n, D_MODEL), f32), jax.ShapeDtypeStruct((n, D_MODEL), bf16),
                   jax.ShapeDtypeStruct((n, LANE), f32)],
        compiler_params=_params(("arbitrary",)),
        name="merge",
    )(x, *brs, u, mod3, wb, wo, nw_row, wr)


def _rows_prefix_sum(x):
    n = x.shape[0]
    pos = lax.broadcasted_iota(jnp.int32, x.shape, 0)
    k = 1
    while k < n:
        x = x + jnp.where(pos >= k, pltpu.roll(x, k, 0), 0.0)
        k *= 2
    return x


def _route_body(t, cap, bpg, aff_ref, h2_ref, xg_ref, g_ref, rank_ref, rrow_scr, arow_scr, oh_scr):
    bb = pl.program_id(1)

    @pl.when(bb == 0)
    def _():
        aff = aff_ref[...]
        tok = lax.broadcasted_iota(jnp.int32, aff.shape, 0)
        val, idx = aff, tok
        k = 2
        while k <= t:
            j = k // 2
            while j >= 1:
                lower = (tok & j) == 0
                pv = jnp.where(lower, pltpu.roll(val, t - j, 0), pltpu.roll(val, j, 0))
                pi = jnp.where(lower, pltpu.roll(idx, t - j, 0), pltpu.roll(idx, j, 0))
                ahead = (val > pv) | ((val == pv) & (idx < pi))
                keep = (lower == ((tok & k) == 0)) == ahead
                val, idx = jnp.where(keep, val, pv), jnp.where(keep, idx, pi)
                j //= 2
            k *= 2
        thr_v, thr_i = val[cap - 1:cap, :], idx[cap - 1:cap, :]
        sel = jnp.where((aff > thr_v) | ((aff == thr_v) & (tok <= thr_i)), 1.0, 0.0)
        rank = jnp.where(sel > 0.0, _rows_prefix_sum(sel) - sel, float(cap))
        rank_ref[...] = rank
        rank_t = rank.T
        aff_t = aff.T
        for k in range(bpg):
            rrow_scr[k] = rank_t[k * N_EXPERTS:(k + 1) * N_EXPERTS, :]
            arow_scr[k] = aff_t[k * N_EXPERTS:(k + 1) * N_EXPERTS, :]

    rk = rrow_scr[bb]
    ar = arow_scr[bb]
    slot = lax.broadcasted_iota(jnp.int32, (cap, t), 0).astype(f32)
    for e in range(N_EXPERTS):
        onehot = jnp.where(rk[e:e + 1, :] == slot, 1.0, 0.0)
        oh_scr[e * cap:(e + 1) * cap, :] = onehot.astype(bf16)
        g_ref[e] = jnp.sum(onehot * ar[e:e + 1, :], axis=1, keepdims=True)
    h2 = h2_ref[...]
    epb = max(1, 512 // cap)
    for e0 in range(0, N_EXPERTS, epb):
        rows = jnp.dot(oh_scr[e0 * cap:(e0 + epb) * cap, :], h2, preferred_element_type=f32)
        xg_ref[e0:e0 + epb] = rows.astype(bf16).reshape(epb, cap, D_MODEL)


def _group_lanes(aff, nb, t, bpg):
    a = aff[:, :N_EXPERTS].reshape(nb // bpg, bpg, t, N_EXPERTS)
    a = jnp.transpose(a, (0, 2, 1, 3)).reshape(nb // bpg * t, bpg * N_EXPERTS)
    return jnp.pad(a, ((0, 0), (0, LANE - bpg * N_EXPERTS)))


def _route(aff, h2, nb, t, bpg):
    cap = CAPACITY_FACTOR * t // N_EXPERTS
    return pl.pallas_call(
        functools.partial(_route_body, t, cap, bpg),
        grid=(nb // bpg, bpg),
        in_specs=[pl.BlockSpec((t, LANE), lambda g, k: (g, 0)),
                  pl.BlockSpec((t, D_MODEL), lambda g, k: (g * bpg + k, 0))],
        out_specs=[pl.BlockSpec((N_EXPERTS, cap, D_MODEL), lambda g, k: (0, g * bpg + k, 0)),
                   pl.BlockSpec((N_EXPERTS, cap, 1), lambda g, k: (0, g * bpg + k, 0)),
                   pl.BlockSpec((t, LANE), lambda g, k: (g, 0))],
        out_shape=[jax.ShapeDtypeStruct((N_EXPERTS, nb * cap, D_MODEL), bf16),
                   jax.ShapeDtypeStruct((N_EXPERTS, nb * cap, 1), f32),
                   jax.ShapeDtypeStruct((nb // bpg * t, LANE), f32)],
        scratch_shapes=[pltpu.VMEM((bpg, N_EXPERTS, t), f32), pltpu.VMEM((bpg, N_EXPERTS, t), f32),
                        pltpu.VMEM((N_EXPERTS * cap, t), bf16)],
        compiler_params=_params(("arbitrary", "arbitrary")),
        name=f"route_{t}",
    )(_group_lanes(aff, nb, t, bpg), h2)


def _ffn_body(xp_ref, xs_ref, gp_ref, gs_ref, wg_ref, wu_ref, wd_ref, yp_ref, ys_ref, wg_scr, wu_scr, wd_scr):
    wg_scr[...] = wg_ref[...].astype(bf16)
    wu_scr[...] = wu_ref[...].astype(bf16)
    wd_scr[...] = wd_ref[...].astype(bf16)
    blk = 512
    for x_ref, g_ref, y_ref in ((xp_ref, gp_ref, yp_ref), (xs_ref, gs_ref, ys_ref)):
        def step(i, carry, x_ref=x_ref, g_ref=g_ref, y_ref=y_ref):
            r0 = pl.multiple_of(i * blk, blk)
            x = x_ref[pl.ds(r0, blk), :]
            a = jnp.dot(x, wg_scr[...], preferred_element_type=f32)
            up = jnp.dot(x, wu_scr[...], preferred_element_type=f32)
            mid = (jax.nn.silu(a) * up).astype(bf16)
            y = jnp.dot(mid, wd_scr[...], preferred_element_type=f32)
            y_ref[pl.ds(r0, blk), :] = (y * g_ref[pl.ds(r0, blk), :]).astype(bf16)
            return carry
        lax.fori_loop(0, x_ref.shape[0] // blk, step, 0)


def _ffn(xp, xs, gp, gs, w_gate, w_up, w_down, layer):
    np_, ns = xp.shape[1], xs.shape[1]
    tok = lambda n, w: pl.BlockSpec((None, n, w), lambda e: (e, 0, 0))
    wspec = lambda a, b: pl.BlockSpec((None, None, a, b), lambda e: (layer, e, 0, 0))
    return pl.pallas_call(
        _ffn_body,
        grid=(N_EXPERTS,),
        in_specs=[tok(np_, D_MODEL), tok(ns, D_MODEL), tok(np_, 1), tok(ns, 1),
                  wspec(D_MODEL, EXPERT_FF), wspec(D_MODEL, EXPERT_FF), wspec(EXPERT_FF, D_MODEL)],
        out_specs=[tok(np_, D_MODEL), tok(ns, D_MODEL)],
        out_shape=[jax.ShapeDtypeStruct(xp.shape, bf16), jax.ShapeDtypeStruct(xs.shape, bf16)],
        scratch_shapes=[pltpu.VMEM((D_MODEL, EXPERT_FF), bf16), pltpu.VMEM((D_MODEL, EXPERT_FF), bf16),
                        pltpu.VMEM((EXPERT_FF, D_MODEL), bf16)],
        compiler_params=_params(("arbitrary",)),
        name="expert_ffn",
    )(xp, xs, gp, gs, w_gate, w_up, w_down)


def _combine_body(t, cap, bpg, x_ref, rank_ref, y_ref, mod_ref, o_ref):
    k = N_EXPERTS * cap
    col0 = (pl.program_id(0) % bpg) * N_EXPERTS
    er = lax.broadcasted_iota(jnp.int32, (LANE, k), 0)
    ec = lax.broadcasted_iota(jnp.int32, (LANE, k), 1) // cap
    expand = jnp.where(er == ec + col0, 1.0, 0.0).astype(bf16)
    rexp = jnp.dot(rank_ref[...].astype(bf16), expand, preferred_element_type=f32)
    slot = (lax.broadcasted_iota(jnp.int32, (t, k), 1) % cap).astype(f32)
    pt = jnp.where(rexp == slot, 1.0, 0.0).astype(bf16)
    moe = jnp.dot(pt, y_ref[...].reshape(k, D_MODEL), preferred_element_type=f32)
    o_ref[...] = x_ref[...] + mod_ref[:, 5 * D_MODEL:6 * D_MODEL] * moe


def _combine(x, rank, y, mod3, mod_base, mod_stride, nb, t, bpg):
    cap = CAPACITY_FACTOR * t // N_EXPERTS
    return pl.pallas_call(
        functools.partial(_combine_body, t, cap, bpg),
        grid=(nb,),
        in_specs=[pl.BlockSpec((t, D_MODEL), lambda b: (b, 0)),
                  pl.BlockSpec((t, LANE), lambda b: (b // bpg, 0)),
                  pl.BlockSpec((N_EXPERTS, cap, D_MODEL), lambda b: (0, b, 0)),
                  pl.BlockSpec((None, 1, 6 * D_MODEL), lambda b: (mod_base + mod_stride * b, 0, 0))],
        out_specs=pl.BlockSpec((t, D_MODEL), lambda b: (b, 0)),
        out_shape=jax.ShapeDtypeStruct(x.shape, f32),
        compiler_params=_params(("arbitrary",)),
        name=f"combine_{t}",
    )(x, rank, y, mod3)


def _pad_row(v, off, width=LANE):
    return jnp.zeros((1, width), f32).at[0, off:off + v.shape[0]].set(v.astype(f32))


def _layer_params(l, a):
    w_in = a["w_in"][l]
    pts, acc = [], 0
    splits = (MIX_W,) * 4 + (4 * HEADS,) + (MIX_W,) * 7 + (SSM_CONV_CH, 2 * HEADS, N_BRANCH * D_MODEL)
    for s in splits:
        pts.append((acc, acc + s))
        acc += s
    seg = lambda i: w_in[:, pts[i][0]:pts[i][1]]
    order = [14, 0, 1, 2, 3, 5, 6, 7, 8, 9, 10, 12, 11]
    w_main = jnp.concatenate([seg(i) for i in order] + [jnp.zeros((D_MODEL, U_COLS - U_USED), f32)],
                             axis=1).astype(bf16)
    w_side = jnp.concatenate([seg(4), seg(13), jnp.zeros((D_MODEL, LANE - 24), f32)], axis=1).astype(bf16)
    tile = lambda v, reps: jnp.tile(v.astype(f32), reps).reshape(1, -1)
    p = dict(
        w_main=w_main, w_side=w_side,
        norm1=a["norm1_w"][l].reshape(1, -1), norm2=a["norm2_w"][l].reshape(1, -1),
        mlstm_gb=_pad_row(a["mlstm_gate_b"][l].reshape(-1), G_MLSTM),
        mlstm_nw=a["mlstm_norm_w"][l].reshape(1, -1),
        na_qn=tile(a["na_qnorm_w"][l], HEADS), na_kn=tile(a["na_knorm_w"][l], HEADS),
        na_rpb=_na_rpb_pad(a["na_rpb"][l]),
        diff_qn=tile(a["diff_qnorm_w"][l], 2 * HEADS), diff_kn=tile(a["diff_knorm_w"][l], 2 * HEADS),
        diff_lam=a["diff_lambda"][l], diff_nw=tile(a["diff_norm_w"][l], HEADS),
        ssm_cw=a["ssm_conv_w"][l], ssm_cb=a["ssm_conv_b"][l].reshape(1, -1),
        ssm_dtb=_pad_row(a["ssm_dt_bias"][l].reshape(-1), G_DT),
        ssm_alog=_pad_row(a["ssm_a_log"][l].reshape(-1), G_DT),
        ssm_d=jnp.repeat(a["ssm_d"][l].astype(f32), HEAD_W).reshape(1, -1),
        ssm_nw=a["ssm_norm_w"][l].reshape(1, -1),
        wb=a["w_branch"][l].astype(bf16), wo=a["w_out"][l].astype(bf16),
        wr=jnp.concatenate([a["w_router"][l], jnp.zeros((D_MODEL, LANE - N_EXPERTS), f32)], axis=1).astype(bf16),
        w_gate=a["w_gate"], w_up=a["w_up"], w_down=a["w_down"],
    )
    return p


def _mixers(u, g, nb, seq, p, l, lam_init, ctx, tables):
    if ctx is None:
        a_out, m_c, m_n, m_m = _mlstm(u, g, nb, seq, p["mlstm_gb"], p["mlstm_nw"], None)
        b_out, na_k, na_v = _na_prompt(u, nb, seq, p["na_qn"], p["na_kn"])
        c_out, df_k, df_v = _diff_prompt(u, nb, seq, lam_init, p["diff_qn"], p["diff_kn"], p["diff_lam"], p["diff_nw"])
        d_out, s_h = _ssd(u, g, nb, seq, p["ssm_cw"], p["ssm_cb"], p["ssm_dtb"], p["ssm_alog"], p["ssm_d"],
                          p["ssm_nw"], None)
        return (a_out, b_out, c_out, d_out), (na_k, na_v, df_k, df_v, m_c, m_n, m_m, s_h)
    (na_ck, na_cv, df_ck, df_cv, st_c, st_n, st_m, st_s) = ctx
    cos, sin = tables
    a_out = _mlstm(u, g, nb, seq, p["mlstm_gb"], p["mlstm_nw"], (st_c[:, l], st_n[:, l], st_m[:, l]))[0]
    b_out = _na_sample(u, nb, l, na_ck, na_cv, p["na_rpb"], p["na_qn"], p["na_kn"])
    c_out = _diff_sample(u, nb, l, lam_init, df_ck, df_cv, cos, sin, p["diff_qn"], p["diff_kn"], p["diff_lam"],
                         p["diff_nw"])
    d_out = _ssd(u, g, nb, seq, p["ssm_cw"], p["ssm_cb"], p["ssm_dtb"], p["ssm_alog"], p["ssm_d"], p["ssm_nw"],
                 st_s[:, l])[0]
    return (a_out, b_out, c_out, d_out), None


def kernel(x_prompt, x_sample, cache_na_k, cache_na_v, cache_diff_k, cache_diff_v, state_mlstm_C, state_mlstm_n, state_mlstm_m, state_ssm, c, c_ctx, norm1_w, norm2_w, w_ada, b_ada, w_in, mlstm_gate_b, mlstm_norm_w, na_qnorm_w, na_knorm_w, na_rpb, diff_qnorm_w, diff_knorm_w, diff_lambda, diff_norm_w, ssm_conv_w, ssm_conv_b, ssm_dt_bias, ssm_a_log, ssm_d, ssm_norm_w, w_branch, w_out, w_router, w_gate, w_up, w_down):
    a = dict(norm1_w=norm1_w, norm2_w=norm2_w, w_in=w_in, mlstm_gate_b=mlstm_gate_b, mlstm_norm_w=mlstm_norm_w,
             na_qnorm_w=na_qnorm_w, na_knorm_w=na_knorm_w, na_rpb=na_rpb, diff_qnorm_w=diff_qnorm_w,
             diff_knorm_w=diff_knorm_w, diff_lambda=diff_lambda, diff_norm_w=diff_norm_w, ssm_conv_w=ssm_conv_w,
             ssm_conv_b=ssm_conv_b, ssm_dt_bias=ssm_dt_bias, ssm_a_log=ssm_a_log, ssm_d=ssm_d,
             ssm_norm_w=ssm_norm_w, w_branch=w_branch, w_out=w_out, w_router=w_router, w_gate=w_gate, w_up=w_up,
             w_down=w_down)
    nb_p, nb_s = x_prompt.shape[0], x_sample.shape[0]
    xp = x_prompt.reshape(nb_p * SEQ, D_MODEL)
    xs = x_sample.reshape(nb_s * DEC_SEQ, D_MODEL)
    cc = jnp.concatenate([c_ctx[None, :], c, jnp.zeros((8 - 1 - nb_s, D_MODEL), f32)], axis=0)
    ctx = (cache_na_k.reshape(nb_s, DEPTH, PAST_LEN, MIX_W), cache_na_v.reshape(nb_s, DEPTH, PAST_LEN, MIX_W),
           cache_diff_k.reshape(nb_s, DEPTH, PAST_LEN, MIX_W), cache_diff_v.reshape(nb_s, DEPTH, PAST_LEN, MIX_W),
           state_mlstm_C, state_mlstm_n, state_mlstm_m, state_ssm)
    tables = _rope_tables()
    new = []
    for l in range(DEPTH):
        p = _layer_params(l, a)
        lam_init = 0.8 - 0.6 * math.exp(-0.3 * l)
        mod3 = _modulation(cc, w_ada, b_ada, l).reshape(8, 1, 6 * D_MODEL)
        groups = []
        for (x, nb, seq, base, stride, gctx) in ((xp, nb_p, SEQ, 0, 0, None), (xs, nb_s, DEC_SEQ, 1, 1, ctx)):
            u, g = _inproj(x, mod3, base, stride, seq, p["norm1"], p["w_main"], p["w_side"])
            brs, st = _mixers(u, g, nb, seq, p, l, lam_init, gctx, tables)
            if st is not None:
                new.append(st)
            x1, h2, aff = _merge(x, brs, u, mod3, base, stride, seq, p["wb"], p["wo"], p["norm2"], p["wr"])
            bpg = min(nb, LANE // N_EXPERTS)
            xg, gg, rank = _route(aff, h2, nb, seq, bpg)
            groups.append((x1, xg, gg, rank, nb, seq, base, stride, bpg))
        yp, ys = _ffn(groups[0][1], groups[1][1], groups[0][2], groups[1][2], p["w_gate"], p["w_up"], p["w_down"], l)
        outs = []
        for (x1, _, _, rank, nb, seq, base, stride, bpg), y in zip(groups, (yp, ys)):
            outs.append(_combine(x1, rank, y, mod3, base, stride, nb, seq, bpg))
        xp, xs = outs
    stack = lambda i, shape: jnp.stack([new[l][i].reshape(shape) for l in range(DEPTH)], axis=1)
    kv = (nb_p, SEQ, HEADS, HEAD_W)
    return (xp.reshape(x_prompt.shape), xs.reshape(x_sample.shape),
            stack(0, kv), stack(1, kv), stack(2, kv), stack(3, kv),
            stack(4, (nb_p, 2, HEADS, HEAD_W, HEAD_W)), stack(5, (nb_p, 2, HEADS, HEAD_W)),
            stack(6, (nb_p, 2, HEADS)), stack(7, (nb_p, 2, HEADS, HEAD_W, SSM_STATE)))
```

```python
import functools
import math

import jax
import jax.numpy as jnp
from jax import lax
from jax.experimental import pallas as pl
from jax.experimental.pallas import tpu as pltpu

f32 = jnp.float32
bf16 = jnp.bfloat16

D_MODEL = 1024
BATCH = 32
SEQ = 256
DEPTH = 2
DEC_BATCH = 4
DEC_SEQ = 1024
PAST_LEN = 512
GRID_W = 64
MIX_W = D_MODEL // 4
N_BRANCH = 4
HEADS = 4
HEAD_W = MIX_W // HEADS
NA_WIN_R = 8
NA_WIN_C = 16
NA_ROWS_PER_STEP = 4
DIFF_QK_DIM = HEAD_W // 2
SSM_GROUPS = 2
SSM_STATE = 64
SSM_CONV_K = 5
SSM_CONV_CH = MIX_W + 2 * SSM_GROUPS * SSM_STATE
CHUNK = 64
N_EXPERTS = 16
EXPERT_FF = 1024
CAPACITY_FACTOR = 2
ROPE_BASE = 10000.0
EPS = 1e-6
NEG = -1e30

LANE = 128
VMEM_LIMIT = 56 * 1024 * 1024

COL_GPRE = 0
COL_MLSTM = 4096
COL_NA = 5120
COL_DIFF = 5888
COL_XBC = 6656
COL_Z = 7168
U_USED = 7424
U_COLS = 7680
G_MLSTM = 0
G_DT = 16


def _params(sem):
    return pltpu.CompilerParams(dimension_semantics=sem, vmem_limit_bytes=VMEM_LIMIT)


def _mm(a, b):
    return jnp.dot(a.astype(bf16), b.astype(bf16), preferred_element_type=f32)


def _mm_nt(a, b):
    return lax.dot_general(a.astype(bf16), b.astype(bf16), (((1,), (1,)), ((), ())), preferred_element_type=f32)


def _mm_tn(a, b):
    return lax.dot_general(a.astype(bf16), b.astype(bf16), (((0,), (0,)), ((), ())), preferred_element_type=f32)


def _seg_sum(x, seg):
    w = x.shape[-1]
    r = lax.broadcasted_iota(jnp.int32, (w, w), 0) // seg
    c = lax.broadcasted_iota(jnp.int32, (w, w), 1) // seg
    ones = jnp.where(r == c, 1.0, 0.0).astype(bf16)
    hi = x.astype(bf16)
    lo = (x - hi.astype(f32)).astype(bf16)
    return jnp.dot(hi, ones, preferred_element_type=f32) + jnp.dot(lo, ones, preferred_element_type=f32)


def _seg_rms(x, seg, w_row):
    ms = _seg_sum(x * x, seg) * (1.0 / seg)
    return x * lax.rsqrt(ms + EPS) * w_row


def _mod_body(c_ref, w_ref, b_ref, o_ref):
    o_ref[...] = _mm(jax.nn.silu(c_ref[...]), w_ref[...]) + b_ref[...]


def _modulation(cc, w_ada, b_ada, layer):
    tn = 1536
    n = w_ada.shape[2]
    return pl.pallas_call(
        _mod_body,
        grid=(n // tn,),
        in_specs=[pl.BlockSpec((8, D_MODEL), lambda j: (0, 0)),
                  pl.BlockSpec((None, D_MODEL, tn), lambda j: (layer, 0, j)),
                  pl.BlockSpec((None, 1, tn), lambda j: (layer, 0, j))],
        out_specs=pl.BlockSpec((8, tn), lambda j: (0, j)),
        out_shape=jax.ShapeDtypeStruct((8, n), f32),
        compiler_params=_params(("arbitrary",)),
        name="modulation",
    )(cc, w_ada, b_ada.reshape(b_ada.shape[0], 1, n))


def _inproj_body(ncol, x_ref, mod_ref, nw_ref, w_ref, wg_ref, u_ref, g_ref):
    x = x_ref[...]
    y = x * lax.rsqrt(jnp.mean(x * x, axis=-1, keepdims=True) + EPS) * nw_ref[...]
    hb = (y * (1.0 + mod_ref[:, D_MODEL:2 * D_MODEL]) + mod_ref[:, 0:D_MODEL]).astype(bf16)
    g_ref[...] = jnp.dot(hb, wg_ref[...], preferred_element_type=f32)
    cw = U_COLS // ncol
    for c in range(ncol):
        u_ref[:, c * cw:(c + 1) * cw] = jnp.dot(hb, w_ref[:, c * cw:(c + 1) * cw],
                                                preferred_element_type=f32).astype(u_ref.dtype)


def _inproj(x, mod3, mod_base, mod_stride, seq, nw_row, w_main, w_side):
    n = x.shape[0]
    tm = 512
    tps = max(seq // tm, 1)
    resident = lambda shape: pl.BlockSpec(shape, lambda i: (0, 0), pipeline_mode=pl.Buffered(1))
    return pl.pallas_call(
        functools.partial(_inproj_body, 6),
        grid=(n // tm,),
        in_specs=[pl.BlockSpec((tm, D_MODEL), lambda i: (i, 0)),
                  pl.BlockSpec((None, 1, 6 * D_MODEL), lambda i: (mod_base + mod_stride * (i // tps), 0, 0)),
                  pl.BlockSpec((1, D_MODEL), lambda i: (0, 0)),
                  resident((D_MODEL, U_COLS)), resident((D_MODEL, LANE))],
        out_specs=[pl.BlockSpec((tm, U_COLS), lambda i: (i, 0)),
                   pl.BlockSpec((tm, LANE), lambda i: (i, 0))],
        out_shape=[jax.ShapeDtypeStruct((n, U_COLS), bf16), jax.ShapeDtypeStruct((n, LANE), f32)],
        compiler_params=_params(("arbitrary",)),
        name="inproj",
    )(x, mod3, nw_row, w_main, w_side)


def _split3(x):
    hi = x.astype(bf16)
    r = x - hi.astype(f32)
    mid = r.astype(bf16)
    lo = (r - mid.astype(f32)).astype(bf16)
    return hi, mid, lo


def _select_cols(x, onehot):
    return jnp.dot(jnp.concatenate(_split3(x), axis=1), jnp.concatenate([onehot] * 3, axis=0),
                   preferred_element_type=f32)


def _select_rows(onehot, x):
    return jnp.dot(jnp.concatenate([onehot] * 3, axis=1), jnp.concatenate(_split3(x), axis=0),
                   preferred_element_type=f32)


def _chunk_scan(x, op, identity, reverse):
    n = x.shape[0]
    pos = lax.broadcasted_iota(jnp.int32, x.shape, 0) % CHUNK
    k = 1
    while k < CHUNK:
        if reverse:
            shifted, ok = pltpu.roll(x, n - k, 0), pos < CHUNK - k
        else:
            shifted, ok = pltpu.roll(x, k, 0), pos >= k
        x = op(x, jnp.where(ok, shifted, identity))
        k *= 2
    return x


def _dir_scan(x, op, identity, bwd_col):
    return jnp.where(bwd_col, _chunk_scan(x, op, identity, True), _chunk_scan(x, op, identity, False))


def _head_expand(col0):
    c = lax.broadcasted_iota(jnp.int32, (LANE, MIX_W), 0)
    h = lax.broadcasted_iota(jnp.int32, (LANE, MIX_W), 1) // HEAD_W
    return jnp.where(c == col0 + h, 1.0, 0.0).astype(bf16)


def _chunk_rows(x, seq):
    nrow = max(seq // CHUNK, 8)
    t = lax.broadcasted_iota(jnp.int32, (seq, MIX_W), 0) % CHUNK
    s_ = lax.broadcasted_iota(jnp.int32, (seq, MIX_W), 1) % HEAD_W
    sel = lax.broadcasted_iota(jnp.int32, (nrow, seq), 1) // CHUNK == lax.broadcasted_iota(jnp.int32, (nrow, seq), 0)
    return _select_rows(jnp.where(sel, 1.0, 0.0).astype(bf16), jnp.where(t == s_, x, 0.0))


def _block_diag(x, reps):
    r, c = x.shape
    t = jnp.concatenate([x] * reps, axis=0)
    rb = lax.broadcasted_iota(jnp.int32, t.shape, 0) // r
    cb = lax.broadcasted_iota(jnp.int32, t.shape, 1) // (c // reps)
    return jnp.where(rb == cb, t, jnp.zeros_like(t))


def _head_mask(rows, cols, rseg, cseg):
    return (lax.broadcasted_iota(jnp.int32, (rows, cols), 0) // rseg) == (lax.broadcasted_iota(jnp.int32, (rows, cols), 1) // cseg)


def _mlstm_body(seq, has_state, *refs):
    if has_state:
        (qkvo_ref, g_ref, gb_ref, nw_ref, c0_ref, n0_ref, m0_ref,
         out_ref, cs_ref, ns_ref, ms_ref, bx_scr, mx_scr, vx_scr, vrow_scr, cbd_scr, hf_scr, hb_scr) = refs
    else:
        (qkvo_ref, g_ref, gb_ref, nw_ref,
         out_ref, cs_ref, ns_ref, ms_ref, bx_scr, mx_scr, vx_scr, vrow_scr, cbd_scr, hf_scr, hb_scr) = refs
    nc = seq // CHUNK
    g = g_ref[...] + gb_ref[...]
    lane = lax.broadcasted_iota(jnp.int32, g.shape, 1)
    bwd_col = (lane % 16) >= 8
    bsum = _dir_scan(jax.nn.log_sigmoid(g), jnp.add, 0.0, bwd_col)
    b_i = pltpu.roll(bsum, LANE - HEADS, 1)
    vcol = g - b_i
    mcol = b_i + _dir_scan(vcol, jnp.maximum, NEG, bwd_col)
    for d in range(2):
        e = _head_expand(G_MLSTM + d * 8)
        bx_scr[d] = _select_cols(b_i, e)
        mx_scr[d] = _select_cols(mcol, e)
        vx = _select_cols(vcol, e)
        vx_scr[d] = vx
        vrow = _chunk_rows(vx, seq)
        for c in range(nc):
            vrow_scr[d, c] = vrow[c:c + 1, :]

    grp = lax.broadcasted_iota(jnp.int32, (1, MIX_W), 1) // HEAD_W
    cbd_scr[...] = jnp.zeros(cbd_scr.shape, f32)
    n_rows, m_rows = [], []
    for d in range(2):
        if has_state:
            for h in range(HEADS):
                cbd_scr[d, h * HEAD_W:(h + 1) * HEAD_W, h * HEAD_W:(h + 1) * HEAD_W] = c0_ref[d, h]
            n_rows.append(jnp.concatenate([n0_ref[d, h:h + 1, :] for h in range(HEADS)], axis=1))
            m_row = jnp.zeros((1, MIX_W), f32)
            for h in range(HEADS):
                m_row = jnp.where(grp == h, m0_ref[d:d + 1, h:h + 1], m_row)
            m_rows.append(m_row)
        else:
            n_rows.append(jnp.zeros((1, MIX_W), f32))
            m_rows.append(jnp.zeros((1, MIX_W), f32))

    li = lax.broadcasted_iota(jnp.int32, (CHUNK, MIX_W), 0)
    si = lax.broadcasted_iota(jnp.int32, (CHUNK, MIX_W), 1) % HEAD_W
    valid = (si <= li, si >= li)
    bd = _head_mask(MIX_W, MIX_W, HEAD_W, HEAD_W)
    ones_bd = jnp.where(bd, 1.0, 0.0).astype(bf16)

    def step(c, carry):
        new = []
        for d in range(2):
            n_row, m_row = carry[2 * d], carry[2 * d + 1]
            cidx = c if d == 0 else nc - 1 - c
            r0 = pl.multiple_of(cidx * CHUNK, CHUNK)
            last = CHUNK - 1 if d == 0 else 0
            q = qkvo_ref[pl.ds(r0, CHUNK), 0:MIX_W].astype(f32)
            k = qkvo_ref[pl.ds(r0, CHUNK), MIX_W:2 * MIX_W].astype(f32) * (HEAD_W ** -0.5)
            vb = qkvo_ref[pl.ds(r0, CHUNK), 2 * MIX_W:3 * MIX_W].astype(bf16)
            qb = q.astype(bf16)
            bx = bx_scr[d, pl.ds(r0, CHUNK), :]
            mx = mx_scr[d, pl.ds(r0, CHUNK), :]
            vx = vx_scr[d, pl.ds(r0, CHUNK), :]
            p = jnp.exp(jnp.where(valid[d], bx + vrow_scr[d, cidx] - mx, NEG))
            smat = _mm_nt(qb, _block_diag(k.astype(bf16), HEADS)) * p
            intra = _mm(smat, _block_diag(vb, HEADS))
            dsum = _mm(smat, ones_bd)
            gg = bx + m_row
            mt = jnp.maximum(gg, mx)
            a = jnp.exp(mx - mt)
            w_prev = jnp.exp(gg - mt)
            inter = _mm(qb, cbd_scr[d])
            dint = _mm(q * n_row, ones_bd)
            num = intra * a + inter * w_prev
            den = dsum * a + dint * w_prev
            h_scr = hf_scr if d == 0 else hb_scr
            h_scr[pl.ds(r0, CHUNK), :] = num / jnp.maximum(jnp.abs(den), jnp.exp(-mt))
            bl = bx[last:last + 1, :]
            mloc = mx[last:last + 1, :]
            m_new = jnp.maximum(bl + m_row, mloc)
            dec = jnp.exp(bl + m_row - m_new)
            fac = jnp.exp(mloc - m_new)
            kw = k * jnp.exp(bl + vx - mloc)
            kv = jnp.where(bd, _mm_tn(kw, vb), 0.0)
            cbd_scr[d] = cbd_scr[d] * dec + kv * fac
            new += [n_row * dec + jnp.sum(kw, axis=0, keepdims=True) * fac, m_new]
        return tuple(new)

    fin_state = lax.fori_loop(0, nc, step, (n_rows[0], m_rows[0], n_rows[1], m_rows[1]))
    for d in range(2):
        n_row, m_row = fin_state[2 * d], fin_state[2 * d + 1]
        for h in range(HEADS):
            hl = slice(h * HEAD_W, (h + 1) * HEAD_W)
            cs_ref[d, h] = cbd_scr[d, hl, hl]
            ns_ref[d, h:h + 1, :] = n_row[:, hl]
            ms_ref[d:d + 1, h:h + 1] = m_row[:, h * HEAD_W:h * HEAD_W + 1]

    blk = 256

    def fin(i, carry):
        r0 = pl.multiple_of(i * blk, blk)
        hs = hf_scr[pl.ds(r0, blk), :] + hb_scr[pl.ds(r0, blk), :]
        hn = _seg_rms(hs, HEAD_W, nw_ref[...])
        o = qkvo_ref[pl.ds(r0, blk), 3 * MIX_W:4 * MIX_W].astype(f32)
        out_ref[pl.ds(r0, blk), :] = (hn * jax.nn.sigmoid(o)).astype(out_ref.dtype)
        return carry

    lax.fori_loop(0, seq // blk, fin, 0)


def _mlstm(u, g, nb, seq, gb_row, nw_row, state):
    has_state = state is not None
    nc = seq // CHUNK
    in_specs = [pl.BlockSpec((seq, 4 * MIX_W), lambda b: (b, COL_MLSTM // (4 * MIX_W))),
                pl.BlockSpec((seq, LANE), lambda b: (b, 0)),
                pl.BlockSpec((1, LANE), lambda b: (0, 0)),
                pl.BlockSpec((1, MIX_W), lambda b: (0, 0))]
    args = [u, g, gb_row, nw_row]
    if has_state:
        in_specs += [pl.BlockSpec((None, 2, HEADS, HEAD_W, HEAD_W), lambda b: (b, 0, 0, 0, 0)),
                     pl.BlockSpec((None, 2, HEADS, HEAD_W), lambda b: (b, 0, 0, 0)),
                     pl.BlockSpec((None, 2, HEADS), lambda b: (b, 0, 0))]
        args += list(state)
    return pl.pallas_call(
        functools.partial(_mlstm_body, seq, has_state),
        grid=(nb,),
        in_specs=in_specs,
        out_specs=[pl.BlockSpec((seq, MIX_W), lambda b: (b, 0)),
                   pl.BlockSpec((None, 2, HEADS, HEAD_W, HEAD_W), lambda b: (b, 0, 0, 0, 0)),
                   pl.BlockSpec((None, 2, HEADS, HEAD_W), lambda b: (b, 0, 0, 0)),
                   pl.BlockSpec((None, 2, HEADS), lambda b: (b, 0, 0))],
        out_shape=[jax.ShapeDtypeStruct((nb * seq, MIX_W), bf16),
                   jax.ShapeDtypeStruct((nb, 2, HEADS, HEAD_W, HEAD_W), f32),
                   jax.ShapeDtypeStruct((nb, 2, HEADS, HEAD_W), f32),
                   jax.ShapeDtypeStruct((nb, 2, HEADS), f32)],
        scratch_shapes=[pltpu.VMEM((2, seq, MIX_W), f32), pltpu.VMEM((2, seq, MIX_W), f32),
                        pltpu.VMEM((2, seq, MIX_W), f32), pltpu.VMEM((2, nc, 1, MIX_W), f32),
                        pltpu.VMEM((2, MIX_W, MIX_W), f32),
                        pltpu.VMEM((seq, MIX_W), f32), pltpu.VMEM((seq, MIX_W), f32)],
        compiler_params=_params(("arbitrary",)),
        name="mlstm_state" if has_state else "mlstm",
    )(*args)


def _ssd_body(seq, has_state, *refs):
    if has_state:
        (z_ref, xbc_ref, g_ref, cw_ref, cb_ref, dtb_ref, alog_ref, dskip_ref, nw_ref, h0_ref,
         out_ref, hs_ref, xpad_scr, xc_scr, ax_scr, dx_scr, arow_scr, ht_scr, yf_scr, yb_scr) = refs
    else:
        (z_ref, xbc_ref, g_ref, cw_ref, cb_ref, dtb_ref, alog_ref, dskip_ref, nw_ref,
         out_ref, hs_ref, xpad_scr, xc_scr, ax_scr, dx_scr, arow_scr, ht_scr, yf_scr, yb_scr) = refs
    nc = seq // CHUNK
    pad = 8
    blk = 256
    nblk = seq // blk
    xpad_scr[0:pad, :] = jnp.zeros((pad, SSM_CONV_CH), f32)
    xpad_scr[pad + seq:2 * pad + seq, :] = jnp.zeros((pad, SSM_CONV_CH), f32)
    xpad_scr[pad:pad + seq, :] = xbc_ref[...].astype(f32)

    cblk = 128
    for i in range(seq // cblk):
        r0 = i * cblk
        acc = jnp.zeros((cblk, SSM_CONV_CH), f32) + cb_ref[...]
        for kk in range(SSM_CONV_K):
            off = r0 + pad - SSM_CONV_K // 2 + kk
            acc = acc + xpad_scr[off:off + cblk, :] * cw_ref[kk:kk + 1, :]
        xc_scr[r0:r0 + cblk, :] = jax.nn.silu(acc)

    dt = jax.nn.softplus(g_ref[...] + dtb_ref[...])
    lane = lax.broadcasted_iota(jnp.int32, dt.shape, 1)
    acum = _dir_scan(dt * (-jnp.exp(alog_ref[...])), jnp.add, 0.0, (lane % 8) >= HEADS)
    for d in range(2):
        e = _head_expand(G_DT + d * HEADS)
        ax = _select_cols(acum, e)
        ax_scr[d] = ax
        dx_scr[d] = _select_cols(dt, e)
        arow = _chunk_rows(ax, seq)
        for c in range(nc):
            arow_scr[d, c] = arow[c:c + 1, :]

    gn = SSM_GROUPS * SSM_STATE
    rep = HEADS // SSM_GROUPS
    ht_scr[...] = jnp.zeros(ht_scr.shape, f32)
    if has_state:
        for d in range(2):
            for h in range(HEADS):
                g0 = (h // rep) * SSM_STATE
                ht_scr[d, g0:g0 + SSM_STATE, h * HEAD_W:(h + 1) * HEAD_W] = h0_ref[d, h].T
    li = lax.broadcasted_iota(jnp.int32, (CHUNK, MIX_W), 0)
    si = lax.broadcasted_iota(jnp.int32, (CHUNK, MIX_W), 1) % HEAD_W
    valid = (si <= li, si >= li)
    state_mask = _head_mask(gn, MIX_W, SSM_STATE, rep * HEAD_W)
    key_mask = _head_mask(HEADS * CHUNK, gn, rep * CHUNK, SSM_STATE)

    def step(c, carry):
        for d in range(2):
            cidx = c if d == 0 else nc - 1 - c
            r0 = pl.multiple_of(cidx * CHUNK, CHUNK)
            last = CHUNK - 1 if d == 0 else 0
            xs = xc_scr[pl.ds(r0, CHUNK), 0:MIX_W]
            bm = xc_scr[pl.ds(r0, CHUNK), MIX_W:MIX_W + gn].astype(bf16)
            cm = xc_scr[pl.ds(r0, CHUNK), MIX_W + gn:MIX_W + 2 * gn].astype(bf16)
            ax = ax_scr[d, pl.ds(r0, CHUNK), :]
            decay = jnp.exp(jnp.where(valid[d], ax - arow_scr[d, cidx], NEG))
            bexp = jnp.concatenate([bm] * HEADS, axis=0)
            bexp = jnp.where(key_mask, bexp, jnp.zeros_like(bexp))
            scores = _mm_nt(cm, bexp) * decay
            xdt = xs * dx_scr[d, pl.ds(r0, CHUNK), :]
            y = _mm(scores, _block_diag(xdt.astype(bf16), HEADS)) + _mm(cm, ht_scr[d]) * jnp.exp(ax)
            y_scr = yf_scr if d == 0 else yb_scr
            y_scr[pl.ds(r0, CHUNK), :] = y
            al = ax[last:last + 1, :]
            upd = jnp.where(state_mask, _mm_tn(bm, xdt * jnp.exp(al - ax)), 0.0)
            ht_scr[d] = ht_scr[d] * jnp.exp(al) + upd
        return carry

    lax.fori_loop(0, nc, step, 0)
    for d in range(2):
        for h in range(HEADS):
            g0 = (h // rep) * SSM_STATE
            hs_ref[d, h] = ht_scr[d, g0:g0 + SSM_STATE, h * HEAD_W:(h + 1) * HEAD_W].T

    def fin(i, carry):
        r0 = pl.multiple_of(i * blk, blk)
        y = yf_scr[pl.ds(r0, blk), :] + yb_scr[pl.ds(r0, blk), :] + dskip_ref[...] * xc_scr[pl.ds(r0, blk), 0:MIX_W]
        y = y * jax.nn.silu(z_ref[pl.ds(r0, blk), :].astype(f32))
        y = y * lax.rsqrt(jnp.mean(y * y, axis=-1, keepdims=True) + EPS) * nw_ref[...]
        out_ref[pl.ds(r0, blk), :] = y.astype(out_ref.dtype)
        return carry

    lax.fori_loop(0, nblk, fin, 0)


def _ssd(u, g, nb, seq, cw, cb_row, dtb_row, alog_row, dskip_row, nw_row, state):
    has_state = state is not None
    nc = seq // CHUNK
    const = lambda shape: pl.BlockSpec(shape, lambda b: (0,) * len(shape))
    in_specs = [pl.BlockSpec((seq, MIX_W), lambda b: (b, COL_Z // MIX_W)),
                pl.BlockSpec((seq, SSM_CONV_CH), lambda b: (b, COL_XBC // SSM_CONV_CH)),
                pl.BlockSpec((seq, LANE), lambda b: (b, 0)),
                const((SSM_CONV_K, SSM_CONV_CH)), const((1, SSM_CONV_CH)), const((1, LANE)), const((1, LANE)),
                const((1, MIX_W)), const((1, MIX_W))]
    args = [u, u, g, cw, cb_row, dtb_row, alog_row, dskip_row, nw_row]
    if has_state:
        in_specs.append(pl.BlockSpec((None, 2, HEADS, HEAD_W, SSM_STATE), lambda b: (b, 0, 0, 0, 0)))
        args.append(state)
    return pl.pallas_call(
        functools.partial(_ssd_body, seq, has_state),
        grid=(nb,),
        in_specs=in_specs,
        out_specs=[pl.BlockSpec((seq, MIX_W), lambda b: (b, 0)),
                   pl.BlockSpec((None, 2, HEADS, HEAD_W, SSM_STATE), lambda b: (b, 0, 0, 0, 0))],
        out_shape=[jax.ShapeDtypeStruct((nb * seq, MIX_W), bf16),
                   jax.ShapeDtypeStruct((nb, 2, HEADS, HEAD_W, SSM_STATE), f32)],
        scratch_shapes=[pltpu.VMEM((seq + 16, SSM_CONV_CH), f32), pltpu.VMEM((seq, SSM_CONV_CH), f32),
                        pltpu.VMEM((2, seq, MIX_W), f32), pltpu.VMEM((2, seq, MIX_W), f32),
                        pltpu.VMEM((2, nc, 1, MIX_W), f32),
                        pltpu.VMEM((2, SSM_GROUPS * SSM_STATE, MIX_W), f32),
                        pltpu.VMEM((seq, MIX_W), f32), pltpu.VMEM((seq, MIX_W), f32)],
        compiler_params=_params(("arbitrary",)),
        name="ssd_state" if has_state else "ssd",
    )(*args)


def _with_ones(v):
    vb = v.astype(bf16)
    ones = jnp.ones((v.shape[0], HEAD_W), bf16)
    return jnp.concatenate([x for h in range(HEADS) for x in (vb[:, h * HEAD_W:(h + 1) * HEAD_W], ones)], axis=1)


def _softmax_av(scores, vexts):
    m = scores[0].max(axis=1, keepdims=True)
    for s in scores[1:]:
        m = jnp.maximum(m, s.max(axis=1, keepdims=True))
    r = None
    for s, vx in zip(scores, vexts):
        t = jnp.dot(jnp.exp(s - m).astype(bf16), vx, preferred_element_type=f32)
        r = t if r is None else r + t
    return r[:, :HEAD_W] / r[:, HEAD_W:]


def _na_prompt_body(q_ref, k_ref, v_ref, qn_ref, kn_ref, o_ref, ko_ref, vo_ref):
    scale = HEAD_W ** -0.5
    qb = (_seg_rms(q_ref[...].astype(f32), HEAD_W, qn_ref[...]) * scale).astype(bf16)
    kn = _seg_rms(k_ref[...].astype(f32), HEAD_W, kn_ref[...])
    v = v_ref[...].astype(f32)
    ko_ref[...] = kn
    vo_ref[...] = v
    kt = kn.T.astype(bf16)
    vx = _with_ones(v)
    for h in range(HEADS):
        hl = slice(h * HEAD_W, (h + 1) * HEAD_W)
        s = jnp.dot(qb[:, hl], kt[hl, :], preferred_element_type=f32)
        o_ref[:, hl] = _softmax_av([s], [vx[:, 2 * h * HEAD_W:2 * (h + 1) * HEAD_W]]).astype(o_ref.dtype)


def _na_prompt(u, nb, seq, qn_row, kn_row):
    cb = COL_NA // MIX_W
    blk = lambda j: pl.BlockSpec((seq, MIX_W), lambda b: (b, j))
    const = pl.BlockSpec((1, MIX_W), lambda b: (0, 0))
    sds = jax.ShapeDtypeStruct((nb * seq, MIX_W), f32)
    return pl.pallas_call(
        _na_prompt_body,
        grid=(nb,),
        in_specs=[blk(cb), blk(cb + 1), blk(cb + 2), const, const],
        out_specs=[blk(0), blk(0), blk(0)],
        out_shape=[jax.ShapeDtypeStruct(sds.shape, bf16), sds, sds],
        compiler_params=_params(("arbitrary",)),
        name="na_prompt",
    )(u, u, u, qn_row, kn_row)


def _na_sample_body(q_ref, k_ref, v_ref, kc_ref, vc_ref, rpb_ref, qn_ref, kn_ref, o_ref,
                    kh_scr, kch_scr, vx_scr, vcx_scr, toe_scr):
    b = pl.program_id(0)
    i = pl.program_id(1)
    ndr = 2 * NA_WIN_R - 1

    @pl.when((b == 0) & (i == 0))
    def _():
        j = lax.broadcasted_iota(jnp.int32, (GRID_W, LANE), 0)
        lane = lax.broadcasted_iota(jnp.int32, (GRID_W, LANE), 1)
        j2 = lane % GRID_W
        cs = jnp.clip(j - NA_WIN_C // 2, 0, GRID_W - NA_WIN_C)
        ok = (j2 >= cs) & (j2 < cs + NA_WIN_C)
        for h in range(HEADS):
            blocks = []
            for dr in range(ndr):
                row = jnp.broadcast_to(rpb_ref[h, dr:dr + 1, :], (GRID_W, LANE))
                blocks.append((pltpu.roll(row, LANE - (NA_WIN_C - 1), 1, stride=1, stride_axis=0),
                               pltpu.roll(row, GRID_W - (NA_WIN_C - 1), 1, stride=1, stride_axis=0)))
            for dr in range(ndr - 1):
                toe_scr[h, dr] = jnp.where(ok, jnp.where(lane < GRID_W, blocks[dr][0], blocks[dr + 1][1]), NEG)

    @pl.when(i == 0)
    def _():
        kn = _seg_rms(k_ref[...].astype(f32), HEAD_W, kn_ref[...]).astype(bf16)
        kc = kc_ref[...].astype(bf16)
        for h in range(HEADS):
            kh_scr[h] = kn[:, h * HEAD_W:(h + 1) * HEAD_W]
            kch_scr[h] = kc[:, h * HEAD_W:(h + 1) * HEAD_W]
        vx_scr[...] = _with_ones(v_ref[...])
        vcx_scr[...] = _with_ones(vc_ref[...])

    rows = DEC_SEQ // GRID_W
    nwin = NA_WIN_R * GRID_W
    qb = (_seg_rms(q_ref[...].astype(f32), HEAD_W, qn_ref[...]) * (HEAD_W ** -0.5)).astype(bf16)
    for qr in range(NA_ROWS_PER_STEP):
        row = i * NA_ROWS_PER_STEP + qr
        rs = jnp.clip(row - NA_WIN_R // 2, 0, rows - NA_WIN_R)
        d0 = rs - row + (NA_WIN_R - 1)
        k0 = pl.multiple_of(rs * GRID_W, GRID_W)
        qs = slice(qr * GRID_W, (qr + 1) * GRID_W)
        for h in range(HEADS):
            hl = slice(h * HEAD_W, (h + 1) * HEAD_W)
            xl = slice(2 * h * HEAD_W, 2 * (h + 1) * HEAD_W)
            bias = jnp.concatenate([toe_scr[h, d0 + 2 * r] for r in range(NA_WIN_R // 2)], axis=1)
            s_win = _mm_nt(qb[qs, hl], kh_scr[h, pl.ds(k0, nwin), :]) + bias
            s_ctx = _mm_nt(qb[qs, hl], kch_scr[h])
            o = _softmax_av([s_win, s_ctx], [vx_scr[pl.ds(k0, nwin), xl], vcx_scr[:, xl]])
            o_ref[qs, hl] = o.astype(o_ref.dtype)


def _na_sample(u, nb, layer, cache_k, cache_v, rpb_pad, qn_row, kn_row):
    seq = DEC_SEQ
    rows = seq // (GRID_W * NA_ROWS_PER_STEP)
    qrows = GRID_W * NA_ROWS_PER_STEP
    cb = COL_NA // MIX_W
    ndr = 2 * NA_WIN_R - 1
    full = lambda j: pl.BlockSpec((seq, MIX_W), lambda b, i: (b, j))
    ctx = pl.BlockSpec((None, None, PAST_LEN, MIX_W), lambda b, i: (b, layer, 0, 0))
    const = pl.BlockSpec((1, MIX_W), lambda b, i: (0, 0))
    return pl.pallas_call(
        _na_sample_body,
        grid=(nb, rows),
        in_specs=[pl.BlockSpec((qrows, MIX_W), lambda b, i: (b * rows + i, cb)), full(cb + 1), full(cb + 2),
                  ctx, ctx,
                  pl.BlockSpec((HEADS, ndr + 1, LANE), lambda b, i: (0, 0, 0)),
                  const, const],
        out_specs=pl.BlockSpec((qrows, MIX_W), lambda b, i: (b * rows + i, 0)),
        out_shape=jax.ShapeDtypeStruct((nb * seq, MIX_W), bf16),
        scratch_shapes=[pltpu.VMEM((HEADS, seq, HEAD_W), bf16), pltpu.VMEM((HEADS, PAST_LEN, HEAD_W), bf16),
                        pltpu.VMEM((seq, 2 * MIX_W), bf16), pltpu.VMEM((PAST_LEN, 2 * MIX_W), bf16),
                        pltpu.VMEM((HEADS, ndr - 1, GRID_W, LANE), f32)],
        compiler_params=_params(("arbitrary", "arbitrary")),
        name="na_sample",
    )(u, u, u, cache_k, cache_v, rpb_pad, qn_row, kn_row)


def _na_rpb_pad(rpb):
    c = rpb.shape[2]
    return jnp.pad(rpb.astype(f32), ((0, 0), (0, 1), (0, LANE - c)))


def _diff_lambda(lam_ref, lam_init):
    lv = lam_ref[...]
    s1 = jnp.sum(lv[0:1, :] * lv[1:2, :], axis=1, keepdims=True)
    s2 = jnp.sum(lv[2:3, :] * lv[3:4, :], axis=1, keepdims=True)
    return jnp.exp(s1) - jnp.exp(s2) + lam_init


def _diff_heads(q, kt, vext, lam):
    outs = []
    for h in range(HEADS):
        o = []
        for m in range(2):
            c0 = h * HEAD_W + m * DIFF_QK_DIM
            s = jnp.dot(q[:, c0:c0 + DIFF_QK_DIM], kt[c0:c0 + DIFF_QK_DIM, :], preferred_element_type=f32)
            o.append(_softmax_av([s], [vext[:, 2 * h * HEAD_W:2 * (h + 1) * HEAD_W]]))
        outs.append(o[0] - lam * o[1])
    return jnp.concatenate(outs, axis=1)


def _diff_prompt_body(lam_init, q_ref, k_ref, v_ref, qn_ref, kn_ref, lam_ref, nw_ref, o_ref, ko_ref, vo_ref):
    lam = _diff_lambda(lam_ref, lam_init)
    qn = _seg_rms(q_ref[...].astype(f32), DIFF_QK_DIM, qn_ref[...]) * (DIFF_QK_DIM ** -0.5)
    kn = _seg_rms(k_ref[...].astype(f32), DIFF_QK_DIM, kn_ref[...])
    v = v_ref[...].astype(f32)
    ko_ref[...] = kn
    vo_ref[...] = v
    o = _diff_heads(qn.astype(bf16), kn.T.astype(bf16), _with_ones(v), lam)
    o_ref[...] = (_seg_rms(o, HEAD_W, nw_ref[...]) * (1.0 - lam_init)).astype(o_ref.dtype)


def _diff_prompt(u, nb, seq, lam_init, qn_row, kn_row, lam_vecs, nw_row):
    cb = COL_DIFF // MIX_W
    blk = lambda j: pl.BlockSpec((seq, MIX_W), lambda b: (b, j))
    const = pl.BlockSpec((1, MIX_W), lambda b: (0, 0))
    sds = jax.ShapeDtypeStruct((nb * seq, MIX_W), f32)
    return pl.pallas_call(
        functools.partial(_diff_prompt_body, lam_init),
        grid=(nb,),
        in_specs=[blk(cb), blk(cb + 1), blk(cb + 2), const, const,
                  pl.BlockSpec((4, DIFF_QK_DIM), lambda b: (0, 0)), const],
        out_specs=[blk(0), blk(0), blk(0)],
        out_shape=[jax.ShapeDtypeStruct(sds.shape, bf16), sds, sds],
        compiler_params=_params(("arbitrary",)),
        name="diff_prompt",
    )(u, u, u, qn_row, kn_row, lam_vecs, nw_row)


def _rope(x, cos, sin):
    w = x.shape[-1]
    up = pltpu.roll(x, w - 8, 1)
    dn = pltpu.roll(x, 8, 1)
    first = (lax.broadcasted_iota(jnp.int32, x.shape, 1) % 16) < 8
    return x * cos + jnp.where(first, up, dn) * sin


def _diff_sample_body(lam_init, qblk, q_ref, k_ref, v_ref, kc_ref, vc_ref, cos_ref, sin_ref, qn_ref, kn_ref,
                      lam_ref, nw_ref, o_ref, kt_scr, vx_scr):
    i = pl.program_id(1)
    seq = k_ref.shape[0]

    @pl.when(i == 0)
    def _():
        kn = _seg_rms(k_ref[...].astype(f32), DIFF_QK_DIM, kn_ref[...])
        kt_scr[:, 0:seq] = _rope(kn, cos_ref[...], sin_ref[...]).T.astype(bf16)
        kt_scr[:, seq:] = kc_ref[...].T.astype(bf16)
        vx_scr[0:seq, :] = _with_ones(v_ref[...])
        vx_scr[seq:, :] = _with_ones(vc_ref[...])

    lam = _diff_lambda(lam_ref, lam_init)
    r0 = pl.multiple_of(i * qblk, qblk)
    qn = _seg_rms(q_ref[...].astype(f32), DIFF_QK_DIM, qn_ref[...])
    qn = _rope(qn, cos_ref[pl.ds(r0, qblk), :], sin_ref[pl.ds(r0, qblk), :]) * (DIFF_QK_DIM ** -0.5)
    o = _diff_heads(qn.astype(bf16), kt_scr[...], vx_scr[...], lam)
    o_ref[...] = (_seg_rms(o, HEAD_W, nw_ref[...]) * (1.0 - lam_init)).astype(o_ref.dtype)


def _diff_sample(u, nb, layer, lam_init, cache_k, cache_v, cos, sin, qn_row, kn_row, lam_vecs, nw_row):
    seq = DEC_SEQ
    qblk = 256
    nq = seq // qblk
    cb = COL_DIFF // MIX_W
    full = lambda j: pl.BlockSpec((seq, MIX_W), lambda b, i: (b, j))
    ctx = pl.BlockSpec((None, None, PAST_LEN, MIX_W), lambda b, i: (b, layer, 0, 0))
    const = pl.BlockSpec((1, MIX_W), lambda b, i: (0, 0))
    tab = pl.BlockSpec((seq, MIX_W), lambda b, i: (0, 0))
    return pl.pallas_call(
        functools.partial(_diff_sample_body, lam_init, qblk),
        grid=(nb, nq),
        in_specs=[pl.BlockSpec((qblk, MIX_W), lambda b, i: (b * nq + i, cb)), full(cb + 1), full(cb + 2),
                  ctx, ctx, tab, tab, const, const,
                  pl.BlockSpec((4, DIFF_QK_DIM), lambda b, i: (0, 0)), const],
        out_specs=pl.BlockSpec((qblk, MIX_W), lambda b, i: (b * nq + i, 0)),
        out_shape=jax.ShapeDtypeStruct((nb * seq, MIX_W), bf16),
        scratch_shapes=[pltpu.VMEM((MIX_W, seq + PAST_LEN), bf16), pltpu.VMEM((seq + PAST_LEN, 2 * MIX_W), bf16)],
        compiler_params=_params(("arbitrary", "arbitrary")),
        name="diff_sample",
    )(u, u, u, cache_k, cache_v, cos, sin, qn_row, kn_row, lam_vecs, nw_row)


def _rope_tables():
    t = jnp.arange(DEC_SEQ)
    nf = DIFF_QK_DIM // 4
    inv = ROPE_BASE ** (-jnp.arange(nf, dtype=f32) / nf)
    ar = (t // GRID_W).astype(f32)[:, None] * inv
    ac = (t % GRID_W).astype(f32)[:, None] * inv
    cos = jnp.concatenate([jnp.cos(ar), jnp.cos(ar), jnp.cos(ac), jnp.cos(ac)], axis=1)
    sin = jnp.concatenate([-jnp.sin(ar), jnp.sin(ar), -jnp.sin(ac), jnp.sin(ac)], axis=1)
    reps = MIX_W // DIFF_QK_DIM
    return jnp.tile(cos, (1, reps)), jnp.tile(sin, (1, reps))


def _merge_body(x_ref, a_ref, b_ref, c_ref, d_ref, gp_ref, mod_ref, wb_ref, wo_ref, nw_ref, wr_ref,
                xo_ref, h2_ref, aff_ref):
    mixed = None
    for n, br in enumerate((a_ref, b_ref, c_ref, d_ref)):
        gate = jax.nn.sigmoid(gp_ref[:, n * D_MODEL:(n + 1) * D_MODEL].astype(f32))
        t = gate * jnp.dot(br[...].astype(bf16), wb_ref[n], preferred_element_type=f32)
        mixed = t if mixed is None else mixed + t
    y = jnp.dot(mixed.astype(bf16), wo_ref[...], preferred_element_type=f32)
    x = x_ref[...] + mod_ref[:, 2 * D_MODEL:3 * D_MODEL] * y
    xo_ref[...] = x
    h = x * lax.rsqrt(jnp.mean(x * x, axis=-1, keepdims=True) + EPS) * nw_ref[...]
    h = h * (1.0 + mod_ref[:, 4 * D_MODEL:5 * D_MODEL]) + mod_ref[:, 3 * D_MODEL:4 * D_MODEL]
    hb = h.astype(bf16)
    h2_ref[...] = hb
    logits = jnp.dot(hb, wr_ref[...], preferred_element_type=f32)
    lane = lax.broadcasted_iota(jnp.int32, logits.shape, 1)
    logits = jnp.where(lane < N_EXPERTS, logits, NEG)
    e = jnp.exp(logits - logits.max(axis=1, keepdims=True))
    aff_ref[...] = e / e.sum(axis=1, keepdims=True)


def _merge(x, brs, u, mod3, mod_base, mod_stride, seq, wb, wo, nw_row, wr):
    n = x.shape[0]
    tm = 256
    tps = seq // tm
    row = lambda w: pl.BlockSpec((tm, w), lambda i: (i, 0))
    return pl.pallas_call(
        _merge_body,
        grid=(n // tm,),
        in_specs=[row(D_MODEL), row(MIX_W), row(MIX_W), row(MIX_W), row(MIX_W),
                  pl.BlockSpec((tm, N_BRANCH * D_MODEL), lambda i: (i, COL_GPRE // (N_BRANCH * D_MODEL))),
                  pl.BlockSpec((None, 1, 6 * D_MODEL), lambda i: (mod_base + mod_stride * (i // tps), 0, 0)),
                  pl.BlockSpec((N_BRANCH, MIX_W, D_MODEL), lambda i: (0, 0, 0)),
                  pl.BlockSpec((D_MODEL, D_MODEL), lambda i: (0, 0)),
                  pl.BlockSpec((1, D_MODEL), lambda i: (0, 0)),
                  pl.BlockSpec((D_MODEL, LANE), lambda i: (0, 0))],
        out_specs=[row(D_MODEL), row(D_MODEL), row(LANE)],
        out_shape=[jax.ShapeDtypeStruct((n, D_MODEL), f32), jax.ShapeDtypeStruct((n, D_MODEL), bf16),
                   jax.ShapeDtypeStruct((n, LANE), f32)],
        compiler_params=_params(("arbitrary",)),
        name="merge",
    )(x, *brs, u, mod3, wb, wo, nw_row, wr)


def _rows_prefix_sum(x):
    n = x.shape[0]
    pos = lax.broadcasted_iota(jnp.int32, x.shape, 0)
    k = 1
    while k < n:
        x = x + jnp.where(pos >= k, pltpu.roll(x, k, 0), 0.0)
        k *= 2
    return x


def _route_body(t, cap, bpg, aff_ref, h2_ref, xg_ref, g_ref, rank_ref, rrow_scr, arow_scr, oh_scr):
    bb = pl.program_id(1)

    @pl.when(bb == 0)
    def _():
        aff = aff_ref[...]
        tok = lax.broadcasted_iota(jnp.int32, aff.shape, 0)
        val = aff
        k = 2
        while k <= t:
            j = k // 2
            while j >= 1:
                lower = (tok & j) == 0
                pv = jnp.where(lower, pltpu.roll(val, t - j, 0), pltpu.roll(val, j, 0))
                val = jnp.where(lower == ((tok & k) == 0), jnp.maximum(val, pv), jnp.minimum(val, pv))
                j //= 2
            k *= 2
        thr = val[cap - 1:cap, :]
        above = aff > thr
        tie = jnp.where(aff == thr, 1.0, 0.0)
        need = cap - jnp.sum(jnp.where(above, 1.0, 0.0), axis=0, keepdims=True)
        sel = jnp.where(above | ((tie > 0.0) & (_rows_prefix_sum(tie) - tie < need)), 1.0, 0.0)
        rank = jnp.where(sel > 0.0, _rows_prefix_sum(sel) - sel, float(cap))
        rank_ref[...] = rank
        rank_t = rank.T
        aff_t = aff.T
        for k in range(bpg):
            rrow_scr[k] = rank_t[k * N_EXPERTS:(k + 1) * N_EXPERTS, :]
            arow_scr[k] = aff_t[k * N_EXPERTS:(k + 1) * N_EXPERTS, :]

    rk = rrow_scr[bb]
    ar = arow_scr[bb]
    slot = lax.broadcasted_iota(jnp.int32, (cap, t), 0).astype(f32)
    for e in range(N_EXPERTS):
        onehot = jnp.where(rk[e:e + 1, :] == slot, 1.0, 0.0)
        oh_scr[e * cap:(e + 1) * cap, :] = onehot.astype(bf16)
        g_ref[e] = jnp.sum(onehot * ar[e:e + 1, :], axis=1, keepdims=True)
    h2 = h2_ref[...]
    epb = max(1, 512 // cap)
    for e0 in range(0, N_EXPERTS, epb):
        rows = jnp.dot(oh_scr[e0 * cap:(e0 + epb) * cap, :], h2, preferred_element_type=f32)
        xg_ref[e0:e0 + epb] = rows.astype(bf16).reshape(epb, cap, D_MODEL)


def _group_lanes(aff, nb, t, bpg):
    a = aff[:, :N_EXPERTS].reshape(nb // bpg, bpg, t, N_EXPERTS)
    a = jnp.transpose(a, (0, 2, 1, 3)).reshape(nb // bpg * t, bpg * N_EXPERTS)
    return jnp.pad(a, ((0, 0), (0, LANE - bpg * N_EXPERTS)))


def _route(aff, h2, nb, t, bpg):
    cap = CAPACITY_FACTOR * t // N_EXPERTS
    return pl.pallas_call(
        functools.partial(_route_body, t, cap, bpg),
        grid=(nb // bpg, bpg),
        in_specs=[pl.BlockSpec((t, LANE), lambda g, k: (g, 0)),
                  pl.BlockSpec((t, D_MODEL), lambda g, k: (g * bpg + k, 0))],
        out_specs=[pl.BlockSpec((N_EXPERTS, cap, D_MODEL), lambda g, k: (0, g * bpg + k, 0)),
                   pl.BlockSpec((N_EXPERTS, cap, 1), lambda g, k: (0, g * bpg + k, 0)),
                   pl.BlockSpec((t, LANE), lambda g, k: (g, 0))],
        out_shape=[jax.ShapeDtypeStruct((N_EXPERTS, nb * cap, D_MODEL), bf16),
                   jax.ShapeDtypeStruct((N_EXPERTS, nb * cap, 1), f32),
                   jax.ShapeDtypeStruct((nb // bpg * t, LANE), f32)],
        scratch_shapes=[pltpu.VMEM((bpg, N_EXPERTS, t), f32), pltpu.VMEM((bpg, N_EXPERTS, t), f32),
                        pltpu.VMEM((N_EXPERTS * cap, t), bf16)],
        compiler_params=_params(("arbitrary", "arbitrary")),
        name=f"route_{t}",
    )(_group_lanes(aff, nb, t, bpg), h2)


def _ffn_body(xp_ref, xs_ref, gp_ref, gs_ref, wg_ref, wu_ref, wd_ref, yp_ref, ys_ref, wg_scr, wu_scr, wd_scr):
    wg_scr[...] = wg_ref[...].astype(bf16)
    wu_scr[...] = wu_ref[...].astype(bf16)
    wd_scr[...] = wd_ref[...].astype(bf16)
    blk = 512
    for x_ref, g_ref, y_ref in ((xp_ref, gp_ref, yp_ref), (xs_ref, gs_ref, ys_ref)):
        def step(i, carry, x_ref=x_ref, g_ref=g_ref, y_ref=y_ref):
            r0 = pl.multiple_of(i * blk, blk)
            x = x_ref[pl.ds(r0, blk), :]
            a = jnp.dot(x, wg_scr[...], preferred_element_type=f32)
            up = jnp.dot(x, wu_scr[...], preferred_element_type=f32)
            mid = (jax.nn.silu(a) * up).astype(bf16)
            y = jnp.dot(mid, wd_scr[...], preferred_element_type=f32)
            y_ref[pl.ds(r0, blk), :] = (y * g_ref[pl.ds(r0, blk), :]).astype(bf16)
            return carry
        lax.fori_loop(0, x_ref.shape[0] // blk, step, 0)


def _ffn(xp, xs, gp, gs, w_gate, w_up, w_down, layer):
    np_, ns = xp.shape[1], xs.shape[1]
    tok = lambda n, w: pl.BlockSpec((None, n, w), lambda e: (e, 0, 0))
    wspec = lambda a, b: pl.BlockSpec((None, None, a, b), lambda e: (layer, e, 0, 0))
    return pl.pallas_call(
        _ffn_body,
        grid=(N_EXPERTS,),
        in_specs=[tok(np_, D_MODEL), tok(ns, D_MODEL), tok(np_, 1), tok(ns, 1),
                  wspec(D_MODEL, EXPERT_FF), wspec(D_MODEL, EXPERT_FF), wspec(EXPERT_FF, D_MODEL)],
        out_specs=[tok(np_, D_MODEL), tok(ns, D_MODEL)],
        out_shape=[jax.ShapeDtypeStruct(xp.shape, bf16), jax.ShapeDtypeStruct(xs.shape, bf16)],
        scratch_shapes=[pltpu.VMEM((D_MODEL, EXPERT_FF), bf16), pltpu.VMEM((D_MODEL, EXPERT_FF), bf16),
                        pltpu.VMEM((EXPERT_FF, D_MODEL), bf16)],
        compiler_params=_params(("arbitrary",)),
        name="expert_ffn",
    )(xp, xs, gp, gs, w_gate, w_up, w_down)


def _combine_body(t, cap, bpg, x_ref, rank_ref, y_ref, mod_ref, o_ref):
    k = N_EXPERTS * cap
    col0 = (pl.program_id(0) % bpg) * N_EXPERTS
    er = lax.broadcasted_iota(jnp.int32, (LANE, k), 0)
    ec = lax.broadcasted_iota(jnp.int32, (LANE, k), 1) // cap
    expand = jnp.where(er == ec + col0, 1.0, 0.0).astype(bf16)
    rexp = jnp.dot(rank_ref[...].astype(bf16), expand, preferred_element_type=f32)
    slot = (lax.broadcasted_iota(jnp.int32, (t, k), 1) % cap).astype(f32)
    pt = jnp.where(rexp == slot, 1.0, 0.0).astype(bf16)
    moe = jnp.dot(pt, y_ref[...].reshape(k, D_MODEL), preferred_element_type=f32)
    o_ref[...] = x_ref[...] + mod_ref[:, 5 * D_MODEL:6 * D_MODEL] * moe


def _combine(x, rank, y, mod3, mod_base, mod_stride, nb, t, bpg):
    cap = CAPACITY_FACTOR * t // N_EXPERTS
    return pl.pallas_call(
        functools.partial(_combine_body, t, cap, bpg),
        grid=(nb,),
        in_specs=[pl.BlockSpec((t, D_MODEL), lambda b: (b, 0)),
                  pl.BlockSpec((t, LANE), lambda b: (b // bpg, 0)),
                  pl.BlockSpec((N_EXPERTS, cap, D_MODEL), lambda b: (0, b, 0)),
                  pl.BlockSpec((None, 1, 6 * D_MODEL), lambda b: (mod_base + mod_stride * b, 0, 0))],
        out_specs=pl.BlockSpec((t, D_MODEL), lambda b: (b, 0)),
        out_shape=jax.ShapeDtypeStruct(x.shape, f32),
        compiler_params=_params(("arbitrary",)),
        name=f"combine_{t}",
    )(x, rank, y, mod3)


def _pad_row(v, off, width=LANE):
    return jnp.zeros((1, width), f32).at[0, off:off + v.shape[0]].set(v.astype(f32))


def _layer_params(l, a):
    w_in = a["w_in"][l]
    pts, acc = [], 0
    splits = (MIX_W,) * 4 + (4 * HEADS,) + (MIX_W,) * 7 + (SSM_CONV_CH, 2 * HEADS, N_BRANCH * D_MODEL)
    for s in splits:
        pts.append((acc, acc + s))
        acc += s
    seg = lambda i: w_in[:, pts[i][0]:pts[i][1]]
    order = [14, 0, 1, 2, 3, 5, 6, 7, 8, 9, 10, 12, 11]
    w_main = jnp.concatenate([seg(i) for i in order] + [jnp.zeros((D_MODEL, U_COLS - U_USED), f32)],
                             axis=1).astype(bf16)
    w_side = jnp.concatenate([seg(4), seg(13), jnp.zeros((D_MODEL, LANE - 24), f32)], axis=1).astype(bf16)
    tile = lambda v, reps: jnp.tile(v.astype(f32), reps).reshape(1, -1)
    p = dict(
        w_main=w_main, w_side=w_side,
        norm1=a["norm1_w"][l].reshape(1, -1), norm2=a["norm2_w"][l].reshape(1, -1),
        mlstm_gb=_pad_row(a["mlstm_gate_b"][l].reshape(-1), G_MLSTM),
        mlstm_nw=a["mlstm_norm_w"][l].reshape(1, -1),
        na_qn=tile(a["na_qnorm_w"][l], HEADS), na_kn=tile(a["na_knorm_w"][l], HEADS),
        na_rpb=_na_rpb_pad(a["na_rpb"][l]),
        diff_qn=tile(a["diff_qnorm_w"][l], 2 * HEADS), diff_kn=tile(a["diff_knorm_w"][l], 2 * HEADS),
        diff_lam=a["diff_lambda"][l], diff_nw=tile(a["diff_norm_w"][l], HEADS),
        ssm_cw=a["ssm_conv_w"][l], ssm_cb=a["ssm_conv_b"][l].reshape(1, -1),
        ssm_dtb=_pad_row(a["ssm_dt_bias"][l].reshape(-1), G_DT),
        ssm_alog=_pad_row(a["ssm_a_log"][l].reshape(-1), G_DT),
        ssm_d=jnp.repeat(a["ssm_d"][l].astype(f32), HEAD_W).reshape(1, -1),
        ssm_nw=a["ssm_norm_w"][l].reshape(1, -1),
        wb=a["w_branch"][l].astype(bf16), wo=a["w_out"][l].astype(bf16),
        wr=jnp.concatenate([a["w_router"][l], jnp.zeros((D_MODEL, LANE - N_EXPERTS), f32)], axis=1).astype(bf16),
        w_gate=a["w_gate"], w_up=a["w_up"], w_down=a["w_down"],
    )
    return p


def _mixers(u, g, nb, seq, p, l, lam_init, ctx, tables):
    if ctx is None:
        a_out, m_c, m_n, m_m = _mlstm(u, g, nb, seq, p["mlstm_gb"], p["mlstm_nw"], None)
        b_out, na_k, na_v = _na_prompt(u, nb, seq, p["na_qn"], p["na_kn"])
        c_out, df_k, df_v = _diff_prompt(u, nb, seq, lam_init, p["diff_qn"], p["diff_kn"], p["diff_lam"], p["diff_nw"])
        d_out, s_h = _ssd(u, g, nb, seq, p["ssm_cw"], p["ssm_cb"], p["ssm_dtb"], p["ssm_alog"], p["ssm_d"],
                          p["ssm_nw"], None)
        return (a_out, b_out, c_out, d_out), (na_k, na_v, df_k, df_v, m_c, m_n, m_m, s_h)
    (na_ck, na_cv, df_ck, df_cv, st_c, st_n, st_m, st_s) = ctx
    cos, sin = tables
    a_out = _mlstm(u, g, nb, seq, p["mlstm_gb"], p["mlstm_nw"], (st_c[:, l], st_n[:, l], st_m[:, l]))[0]
    b_out = _na_sample(u, nb, l, na_ck, na_cv, p["na_rpb"], p["na_qn"], p["na_kn"])
    c_out = _diff_sample(u, nb, l, lam_init, df_ck, df_cv, cos, sin, p["diff_qn"], p["diff_kn"], p["diff_lam"],
                         p["diff_nw"])
    d_out = _ssd(u, g, nb, seq, p["ssm_cw"], p["ssm_cb"], p["ssm_dtb"], p["ssm_alog"], p["ssm_d"], p["ssm_nw"],
                 st_s[:, l])[0]
    return (a_out, b_out, c_out, d_out), None


def kernel(x_prompt, x_sample, cache_na_k, cache_na_v, cache_diff_k, cache_diff_v, state_mlstm_C, state_mlstm_n, state_mlstm_m, state_ssm, c, c_ctx, norm1_w, norm2_w, w_ada, b_ada, w_in, mlstm_gate_b, mlstm_norm_w, na_qnorm_w, na_knorm_w, na_rpb, diff_qnorm_w, diff_knorm_w, diff_lambda, diff_norm_w, ssm_conv_w, ssm_conv_b, ssm_dt_bias, ssm_a_log, ssm_d, ssm_norm_w, w_branch, w_out, w_router, w_gate, w_up, w_down):
    a = dict(norm1_w=norm1_w, norm2_w=norm2_w, w_in=w_in, mlstm_gate_b=mlstm_gate_b, mlstm_norm_w=mlstm_norm_w,
             na_qnorm_w=na_qnorm_w, na_knorm_w=na_knorm_w, na_rpb=na_rpb, diff_qnorm_w=diff_qnorm_w,
             diff_knorm_w=diff_knorm_w, diff_lambda=diff_lambda, diff_norm_w=diff_norm_w, ssm_conv_w=ssm_conv_w,
             ssm_conv_b=ssm_conv_b, ssm_dt_bias=ssm_dt_bias, ssm_a_log=ssm_a_log, ssm_d=ssm_d,
             ssm_norm_w=ssm_norm_w, w_branch=w_branch, w_out=w_out, w_router=w_router, w_gate=w_gate, w_up=w_up,
             w_down=w_down)
    nb_p, nb_s = x_prompt.shape[0], x_sample.shape[0]
    xp = x_prompt.reshape(nb_p * SEQ, D_MODEL)
    xs = x_sample.reshape(nb_s * DEC_SEQ, D_MODEL)
    cc = jnp.concatenate([c_ctx[None, :], c, jnp.zeros((8 - 1 - nb_s, D_MODEL), f32)], axis=0)
    ctx = (cache_na_k.reshape(nb_s, DEPTH, PAST_LEN, MIX_W), cache_na_v.reshape(nb_s, DEPTH, PAST_LEN, MIX_W),
           cache_diff_k.reshape(nb_s, DEPTH, PAST_LEN, MIX_W), cache_diff_v.reshape(nb_s, DEPTH, PAST_LEN, MIX_W),
           state_mlstm_C, state_mlstm_n, state_mlstm_m, state_ssm)
    tables = _rope_tables()
    new = []
    for l in range(DEPTH):
        p = _layer_params(l, a)
        lam_init = 0.8 - 0.6 * math.exp(-0.3 * l)
        mod3 = _modulation(cc, w_ada, b_ada, l).reshape(8, 1, 6 * D_MODEL)
        groups = []
        for (x, nb, seq, base, stride, gctx) in ((xp, nb_p, SEQ, 0, 0, None), (xs, nb_s, DEC_SEQ, 1, 1, ctx)):
            u, g = _inproj(x, mod3, base, stride, seq, p["norm1"], p["w_main"], p["w_side"])
            brs, st = _mixers(u, g, nb, seq, p, l, lam_init, gctx, tables)
            if st is not None:
                new.append(st)
            x1, h2, aff = _merge(x, brs, u, mod3, base, stride, seq, p["wb"], p["wo"], p["norm2"], p["wr"])
            bpg = min(nb, LANE // N_EXPERTS)
            xg, gg, rank = _route(aff, h2, nb, seq, bpg)
            groups.append((x1, xg, gg, rank, nb, seq, base, stride, bpg))
        yp, ys = _ffn(groups[0][1], groups[1][1], groups[0][2], groups[1][2], p["w_gate"], p["w_up"], p["w_down"], l)
        outs = []
        for (x1, _, _, rank, nb, seq, base, stride, bpg), y in zip(groups, (yp, ys)):
            outs.append(_combine(x1, rank, y, mod3, base, stride, nb, seq, bpg))
        xp, xs = outs
    stack = lambda i, shape: jnp.stack([new[l][i].reshape(shape) for l in range(DEPTH)], axis=1)
    kv = (nb_p, SEQ, HEADS, HEAD_W)
    return (xp.reshape(x_prompt.shape), xs.reshape(x_sample.shape),
            stack(0, kv), stack(1, kv), stack(2, kv), stack(3, kv),
            stack(4, (nb_p, 2, HEADS, HEAD_W, HEAD_W)), stack(5, (nb_p, 2, HEADS, HEAD_W)),
            stack(6, (nb_p, 2, HEADS)), stack(7, (nb_p, 2, HEADS, HEAD_W, SSM_STATE)))
```

```python
import functools
import math

import jax
import jax.numpy as jnp
from jax import lax
from jax.experimental import pallas as pl
from jax.experimental.pallas import tpu as pltpu

f32 = jnp.float32
bf16 = jnp.bfloat16

D_MODEL = 1024
BATCH = 32
SEQ = 256
DEPTH = 2
DEC_BATCH = 4
DEC_SEQ = 1024
PAST_LEN = 512
GRID_W = 64
MIX_W = D_MODEL // 4
N_BRANCH = 4
HEADS = 4
HEAD_W = MIX_W // HEADS
NA_WIN_R = 8
NA_WIN_C = 16
DIFF_QK_DIM = HEAD_W // 2
SSM_GROUPS = 2
SSM_STATE = 64
SSM_CONV_K = 5
SSM_CONV_CH = MIX_W + 2 * SSM_GROUPS * SSM_STATE
CHUNK = 64
SCAN_UNROLL = 4
N_EXPERTS = 16
EXPERT_FF = 1024
CAPACITY_FACTOR = 2
ROPE_BASE = 10000.0
EPS = 1e-6
NEG = -1e30

LANE = 128
VMEM_LIMIT = 56 * 1024 * 1024

COL_GPRE = 0
COL_MLSTM = 4096
COL_NA = 5120
COL_DIFF = 5888
COL_XBC = 6656
COL_Z = 7168
U_USED = 7424
U_COLS = 7680
G_MLSTM = 0
G_DT = 16


def _params(sem):
    return pltpu.CompilerParams(dimension_semantics=sem, vmem_limit_bytes=VMEM_LIMIT)


def _mm(a, b):
    return jnp.dot(a.astype(bf16), b.astype(bf16), preferred_element_type=f32)


def _mm_nt(a, b):
    return lax.dot_general(a.astype(bf16), b.astype(bf16), (((1,), (1,)), ((), ())), preferred_element_type=f32)


def _mm_tn(a, b):
    return lax.dot_general(a.astype(bf16), b.astype(bf16), (((0,), (0,)), ((), ())), preferred_element_type=f32)


def _seg_sum(x, seg):
    w = x.shape[-1]
    r = lax.broadcasted_iota(jnp.int32, (w, w), 0) // seg
    c = lax.broadcasted_iota(jnp.int32, (w, w), 1) // seg
    ones = jnp.where(r == c, 1.0, 0.0).astype(bf16)
    hi = x.astype(bf16)
    lo = (x - hi.astype(f32)).astype(bf16)
    return jnp.dot(hi, ones, preferred_element_type=f32) + jnp.dot(lo, ones, preferred_element_type=f32)


def _seg_rms(x, seg, w_row):
    ms = _seg_sum(x * x, seg) * (1.0 / seg)
    return x * lax.rsqrt(ms + EPS) * w_row


def _mod_body(c_ref, w_ref, b_ref, o_ref):
    o_ref[...] = _mm(jax.nn.silu(c_ref[...]), w_ref[...]) + b_ref[...]


def _modulation(cc, w_ada, b_ada, layer):
    tn = 1536
    n = w_ada.shape[2]
    return pl.pallas_call(
        _mod_body,
        grid=(n // tn,),
        in_specs=[pl.BlockSpec((8, D_MODEL), lambda j: (0, 0)),
                  pl.BlockSpec((None, D_MODEL, tn), lambda j: (layer, 0, j)),
                  pl.BlockSpec((None, 1, tn), lambda j: (layer, 0, j))],
        out_specs=pl.BlockSpec((8, tn), lambda j: (0, j)),
        out_shape=jax.ShapeDtypeStruct((8, n), f32),
        compiler_params=_params(("arbitrary",)),
        name="modulation",
    )(cc, w_ada, b_ada.reshape(b_ada.shape[0], 1, n))


def _inproj_body(ncol, x_ref, mod_ref, nw_ref, w_ref, wg_ref, u_ref, g_ref):
    x = x_ref[...]
    y = x * lax.rsqrt(jnp.mean(x * x, axis=-1, keepdims=True) + EPS) * nw_ref[...]
    hb = (y * (1.0 + mod_ref[:, D_MODEL:2 * D_MODEL]) + mod_ref[:, 0:D_MODEL]).astype(bf16)
    g_ref[...] = jnp.dot(hb, wg_ref[...], preferred_element_type=f32)
    cw = U_COLS // ncol
    for c in range(ncol):
        u_ref[:, c * cw:(c + 1) * cw] = jnp.dot(hb, w_ref[:, c * cw:(c + 1) * cw],
                                                preferred_element_type=f32).astype(u_ref.dtype)


def _inproj(x, mod3, mod_base, mod_stride, seq, nw_row, w_main, w_side):
    n = x.shape[0]
    tm = 512
    tps = max(seq // tm, 1)
    resident = lambda shape: pl.BlockSpec(shape, lambda i: (0, 0), pipeline_mode=pl.Buffered(1))
    return pl.pallas_call(
        functools.partial(_inproj_body, 6),
        grid=(n // tm,),
        in_specs=[pl.BlockSpec((tm, D_MODEL), lambda i: (i, 0)),
                  pl.BlockSpec((None, 1, 6 * D_MODEL), lambda i: (mod_base + mod_stride * (i // tps), 0, 0)),
                  pl.BlockSpec((1, D_MODEL), lambda i: (0, 0)),
                  resident((D_MODEL, U_COLS)), resident((D_MODEL, LANE))],
        out_specs=[pl.BlockSpec((tm, U_COLS), lambda i: (i, 0)),
                   pl.BlockSpec((tm, LANE), lambda i: (i, 0))],
        out_shape=[jax.ShapeDtypeStruct((n, U_COLS), bf16), jax.ShapeDtypeStruct((n, LANE), f32)],
        compiler_params=_params(("arbitrary",)),
        name="inproj",
    )(x, mod3, nw_row, w_main, w_side)


def _split3(x):
    hi = x.astype(bf16)
    r = x - hi.astype(f32)
    mid = r.astype(bf16)
    lo = (r - mid.astype(f32)).astype(bf16)
    return hi, mid, lo


def _select_cols(x, onehot):
    return jnp.dot(jnp.concatenate(_split3(x), axis=1), jnp.concatenate([onehot] * 3, axis=0),
                   preferred_element_type=f32)


def _select_rows(onehot, x):
    return jnp.dot(jnp.concatenate([onehot] * 3, axis=1), jnp.concatenate(_split3(x), axis=0),
                   preferred_element_type=f32)


def _chunk_scan(x, op, identity, reverse):
    n = x.shape[0]
    pos = lax.broadcasted_iota(jnp.int32, x.shape, 0) % CHUNK
    k = 1
    while k < CHUNK:
        if reverse:
            shifted, ok = pltpu.roll(x, n - k, 0), pos < CHUNK - k
        else:
            shifted, ok = pltpu.roll(x, k, 0), pos >= k
        x = op(x, jnp.where(ok, shifted, identity))
        k *= 2
    return x


def _dir_scan(x, op, identity, bwd_col):
    return jnp.where(bwd_col, _chunk_scan(x, op, identity, True), _chunk_scan(x, op, identity, False))


def _head_expand(col0):
    c = lax.broadcasted_iota(jnp.int32, (LANE, MIX_W), 0)
    h = lax.broadcasted_iota(jnp.int32, (LANE, MIX_W), 1) // HEAD_W
    return jnp.where(c == col0 + h, 1.0, 0.0).astype(bf16)


def _chunk_rows(x, seq):
    nrow = max(seq // CHUNK, 8)
    t = lax.broadcasted_iota(jnp.int32, (seq, MIX_W), 0) % CHUNK
    s_ = lax.broadcasted_iota(jnp.int32, (seq, MIX_W), 1) % HEAD_W
    sel = lax.broadcasted_iota(jnp.int32, (nrow, seq), 1) // CHUNK == lax.broadcasted_iota(jnp.int32, (nrow, seq), 0)
    return _select_rows(jnp.where(sel, 1.0, 0.0).astype(bf16), jnp.where(t == s_, x, 0.0))


def _block_diag(x, reps, keep):
    return jnp.concatenate([x] * reps, axis=0) * keep


def _head_mask(rows, cols, rseg, cseg):
    return (lax.broadcasted_iota(jnp.int32, (rows, cols), 0) // rseg) == (lax.broadcasted_iota(jnp.int32, (rows, cols), 1) // cseg)


def _mlstm_body(seq, has_state, *refs):
    if has_state:
        (qkvo_ref, g_ref, gb_ref, nw_ref, c0_ref, n0_ref, m0_ref,
         out_ref, cs_ref, ns_ref, ms_ref, bx_scr, mx_scr, vx_scr, vrow_scr, cbd_scr, hf_scr, hb_scr) = refs
    else:
        (qkvo_ref, g_ref, gb_ref, nw_ref,
         out_ref, cs_ref, ns_ref, ms_ref, bx_scr, mx_scr, vx_scr, vrow_scr, cbd_scr, hf_scr, hb_scr) = refs
    nc = seq // CHUNK
    g = g_ref[...] + gb_ref[...]
    lane = lax.broadcasted_iota(jnp.int32, g.shape, 1)
    bwd_col = (lane % 16) >= 8
    bsum = _dir_scan(jax.nn.log_sigmoid(g), jnp.add, 0.0, bwd_col)
    b_i = pltpu.roll(bsum, LANE - HEADS, 1)
    vcol = g - b_i
    mcol = b_i + _dir_scan(vcol, jnp.maximum, NEG, bwd_col)
    for d in range(2):
        e = _head_expand(G_MLSTM + d * 8)
        bx_scr[d] = _select_cols(b_i, e)
        mx_scr[d] = _select_cols(mcol, e)
        vx = _select_cols(vcol, e)
        vx_scr[d] = vx
        vrow = _chunk_rows(vx, seq)
        for c in range(nc):
            vrow_scr[d, c] = vrow[c:c + 1, :]

    grp = lax.broadcasted_iota(jnp.int32, (1, MIX_W), 1) // HEAD_W
    cbd_scr[...] = jnp.zeros(cbd_scr.shape, f32)
    n_rows, m_rows = [], []
    for d in range(2):
        if has_state:
            for h in range(HEADS):
                cbd_scr[d, h * HEAD_W:(h + 1) * HEAD_W, h * HEAD_W:(h + 1) * HEAD_W] = c0_ref[d, h]
            n_rows.append(jnp.concatenate([n0_ref[d, h:h + 1, :] for h in range(HEADS)], axis=1))
            m_row = jnp.zeros((1, MIX_W), f32)
            for h in range(HEADS):
                m_row = jnp.where(grp == h, m0_ref[d:d + 1, h:h + 1], m_row)
            m_rows.append(m_row)
        else:
            n_rows.append(jnp.zeros((1, MIX_W), f32))
            m_rows.append(jnp.zeros((1, MIX_W), f32))

    li = lax.broadcasted_iota(jnp.int32, (CHUNK, MIX_W), 0)
    si = lax.broadcasted_iota(jnp.int32, (CHUNK, MIX_W), 1) % HEAD_W
    valid = (si <= li, si >= li)
    bd = _head_mask(MIX_W, MIX_W, HEAD_W, HEAD_W)
    bd_f32 = jnp.where(bd, 1.0, 0.0)
    ones_bd = bd_f32.astype(bf16)

    def step(c, carry):
        new = []
        for d in range(2):
            n_row, m_row = carry[2 * d], carry[2 * d + 1]
            cidx = c if d == 0 else nc - 1 - c
            r0 = pl.multiple_of(cidx * CHUNK, CHUNK)
            last = CHUNK - 1 if d == 0 else 0
            q = qkvo_ref[pl.ds(r0, CHUNK), 0:MIX_W].astype(f32)
            k = qkvo_ref[pl.ds(r0, CHUNK), MIX_W:2 * MIX_W].astype(f32) * (HEAD_W ** -0.5)
            vb = qkvo_ref[pl.ds(r0, CHUNK), 2 * MIX_W:3 * MIX_W].astype(bf16)
            qb = q.astype(bf16)
            bx = bx_scr[d, pl.ds(r0, CHUNK), :]
            mx = mx_scr[d, pl.ds(r0, CHUNK), :]
            vx = vx_scr[d, pl.ds(r0, CHUNK), :]
            p = jnp.exp(jnp.where(valid[d], bx + vrow_scr[d, cidx] - mx, NEG))
            smat = _mm_nt(qb, _block_diag(k.astype(bf16), HEADS, ones_bd)) * p
            intra = _mm(smat, _block_diag(vb, HEADS, ones_bd))
            dsum = _mm(smat, ones_bd)
            gg = bx + m_row
            mt = jnp.maximum(gg, mx)
            a = jnp.exp(mx - mt)
            w_prev = jnp.exp(gg - mt)
            inter = _mm(qb, cbd_scr[d])
            dint = _mm(q * n_row, ones_bd)
            num = intra * a + inter * w_prev
            den = dsum * a + dint * w_prev
            h_scr = hf_scr if d == 0 else hb_scr
            h_scr[pl.ds(r0, CHUNK), :] = num / jnp.maximum(jnp.abs(den), jnp.exp(-mt))
            bl = bx[last:last + 1, :]
            mloc = mx[last:last + 1, :]
            m_new = jnp.maximum(bl + m_row, mloc)
            dec = jnp.exp(bl + m_row - m_new)
            fac = jnp.exp(mloc - m_new)
            kw = k * jnp.exp(bl + vx - mloc)
            kv = _mm_tn(kw, vb) * bd_f32
            cbd_scr[d] = cbd_scr[d] * dec + kv * fac
            new += [n_row * dec + jnp.sum(kw, axis=0, keepdims=True) * fac, m_new]
        return tuple(new)

    fin_state = lax.fori_loop(0, nc, step, (n_rows[0], m_rows[0], n_rows[1], m_rows[1]), unroll=SCAN_UNROLL)
    for d in range(2):
        n_row, m_row = fin_state[2 * d], fin_state[2 * d + 1]
        for h in range(HEADS):
            hl = slice(h * HEAD_W, (h + 1) * HEAD_W)
            cs_ref[d, h] = cbd_scr[d, hl, hl]
            ns_ref[d, h:h + 1, :] = n_row[:, hl]
            ms_ref[d:d + 1, h:h + 1] = m_row[:, h * HEAD_W:h * HEAD_W + 1]

    blk = 256

    def fin(i, carry):
        r0 = pl.multiple_of(i * blk, blk)
        hs = hf_scr[pl.ds(r0, blk), :] + hb_scr[pl.ds(r0, blk), :]
        hn = _seg_rms(hs, HEAD_W, nw_ref[...])
        o = qkvo_ref[pl.ds(r0, blk), 3 * MIX_W:4 * MIX_W].astype(f32)
        out_ref[pl.ds(r0, blk), :] = (hn * jax.nn.sigmoid(o)).astype(out_ref.dtype)
        return carry

    lax.fori_loop(0, seq // blk, fin, 0)


def _mlstm(u, g, nb, seq, gb_row, nw_row, state):
    has_state = state is not None
    nc = seq // CHUNK
    in_specs = [pl.BlockSpec((seq, 4 * MIX_W), lambda b: (b, COL_MLSTM // (4 * MIX_W))),
                pl.BlockSpec((seq, LANE), lambda b: (b, 0)),
                pl.BlockSpec((1, LANE), lambda b: (0, 0)),
                pl.BlockSpec((1, MIX_W), lambda b: (0, 0))]
    args = [u, g, gb_row, nw_row]
    if has_state:
        in_specs += [pl.BlockSpec((None, 2, HEADS, HEAD_W, HEAD_W), lambda b: (b, 0, 0, 0, 0)),
                     pl.BlockSpec((None, 2, HEADS, HEAD_W), lambda b: (b, 0, 0, 0)),
                     pl.BlockSpec((None, 2, HEADS), lambda b: (b, 0, 0))]
        args += list(state)
    return pl.pallas_call(
        functools.partial(_mlstm_body, seq, has_state),
        grid=(nb,),
        in_specs=in_specs,
        out_specs=[pl.BlockSpec((seq, MIX_W), lambda b: (b, 0)),
                   pl.BlockSpec((None, 2, HEADS, HEAD_W, HEAD_W), lambda b: (b, 0, 0, 0, 0)),
                   pl.BlockSpec((None, 2, HEADS, HEAD_W), lambda b: (b, 0, 0, 0)),
                   pl.BlockSpec((None, 2, HEADS), lambda b: (b, 0, 0))],
        out_shape=[jax.ShapeDtypeStruct((nb * seq, MIX_W), bf16),
                   jax.ShapeDtypeStruct((nb, 2, HEADS, HEAD_W, HEAD_W), f32),
                   jax.ShapeDtypeStruct((nb, 2, HEADS, HEAD_W), f32),
                   jax.ShapeDtypeStruct((nb, 2, HEADS), f32)],
        scratch_shapes=[pltpu.VMEM((2, seq, MIX_W), f32), pltpu.VMEM((2, seq, MIX_W), f32),
                        pltpu.VMEM((2, seq, MIX_W), f32), pltpu.VMEM((2, nc, 1, MIX_W), f32),
                        pltpu.VMEM((2, MIX_W, MIX_W), f32),
                        pltpu.VMEM((seq, MIX_W), f32), pltpu.VMEM((seq, MIX_W), f32)],
        compiler_params=_params(("arbitrary",)),
        name="mlstm_state" if has_state else "mlstm",
    )(*args)


def _ssd_body(seq, has_state, *refs):
    if has_state:
        (z_ref, xbc_ref, g_ref, cw_ref, cb_ref, dtb_ref, alog_ref, dskip_ref, nw_ref, h0_ref,
         out_ref, hs_ref, xpad_scr, xc_scr, ax_scr, dx_scr, arow_scr, ht_scr, yf_scr, yb_scr) = refs
    else:
        (z_ref, xbc_ref, g_ref, cw_ref, cb_ref, dtb_ref, alog_ref, dskip_ref, nw_ref,
         out_ref, hs_ref, xpad_scr, xc_scr, ax_scr, dx_scr, arow_scr, ht_scr, yf_scr, yb_scr) = refs
    nc = seq // CHUNK
    pad = 8
    blk = 256
    nblk = seq // blk
    xpad_scr[0:pad, :] = jnp.zeros((pad, SSM_CONV_CH), f32)
    xpad_scr[pad + seq:2 * pad + seq, :] = jnp.zeros((pad, SSM_CONV_CH), f32)
    xpad_scr[pad:pad + seq, :] = xbc_ref[...].astype(f32)

    cblk = 128
    for i in range(seq // cblk):
        r0 = i * cblk
        acc = jnp.zeros((cblk, SSM_CONV_CH), f32) + cb_ref[...]
        for kk in range(SSM_CONV_K):
            off = r0 + pad - SSM_CONV_K // 2 + kk
            acc = acc + xpad_scr[off:off + cblk, :] * cw_ref[kk:kk + 1, :]
        xc_scr[r0:r0 + cblk, :] = jax.nn.silu(acc)

    dt = jax.nn.softplus(g_ref[...] + dtb_ref[...])
    lane = lax.broadcasted_iota(jnp.int32, dt.shape, 1)
    acum = _dir_scan(dt * (-jnp.exp(alog_ref[...])), jnp.add, 0.0, (lane % 8) >= HEADS)
    for d in range(2):
        e = _head_expand(G_DT + d * HEADS)
        ax = _select_cols(acum, e)
        ax_scr[d] = ax
        dx_scr[d] = _select_cols(dt, e)
        arow = _chunk_rows(ax, seq)
        for c in range(nc):
            arow_scr[d, c] = arow[c:c + 1, :]

    gn = SSM_GROUPS * SSM_STATE
    rep = HEADS // SSM_GROUPS
    ht_scr[...] = jnp.zeros(ht_scr.shape, f32)
    if has_state:
        for d in range(2):
            for h in range(HEADS):
                g0 = (h // rep) * SSM_STATE
                ht_scr[d, g0:g0 + SSM_STATE, h * HEAD_W:(h + 1) * HEAD_W] = h0_ref[d, h].T
    li = lax.broadcasted_iota(jnp.int32, (CHUNK, MIX_W), 0)
    si = lax.broadcasted_iota(jnp.int32, (CHUNK, MIX_W), 1) % HEAD_W
    valid = (si <= li, si >= li)
    state_keep = jnp.where(_head_mask(gn, MIX_W, SSM_STATE, rep * HEAD_W), 1.0, 0.0)
    key_keep = jnp.where(_head_mask(HEADS * CHUNK, gn, rep * CHUNK, SSM_STATE), 1.0, 0.0).astype(bf16)
    head_keep = jnp.where(_head_mask(MIX_W, MIX_W, HEAD_W, HEAD_W), 1.0, 0.0).astype(bf16)

    def step(c, carry):
        for d in range(2):
            cidx = c if d == 0 else nc - 1 - c
            r0 = pl.multiple_of(cidx * CHUNK, CHUNK)
            last = CHUNK - 1 if d == 0 else 0
            xs = xc_scr[pl.ds(r0, CHUNK), 0:MIX_W]
            bm = xc_scr[pl.ds(r0, CHUNK), MIX_W:MIX_W + gn].astype(bf16)
            cm = xc_scr[pl.ds(r0, CHUNK), MIX_W + gn:MIX_W + 2 * gn].astype(bf16)
            ax = ax_scr[d, pl.ds(r0, CHUNK), :]
            decay = jnp.exp(jnp.where(valid[d], ax - arow_scr[d, cidx], NEG))
            scores = _mm_nt(cm, _block_diag(bm, HEADS, key_keep)) * decay
            xdt = xs * dx_scr[d, pl.ds(r0, CHUNK), :]
            y = _mm(scores, _block_diag(xdt.astype(bf16), HEADS, head_keep)) + _mm(cm, ht_scr[d]) * jnp.exp(ax)
            y_scr = yf_scr if d == 0 else yb_scr
            y_scr[pl.ds(r0, CHUNK), :] = y
            al = ax[last:last + 1, :]
            ht_scr[d] = ht_scr[d] * jnp.exp(al) + _mm_tn(bm, xdt * jnp.exp(al - ax)) * state_keep
        return carry

    lax.fori_loop(0, nc, step, 0, unroll=SCAN_UNROLL)
    for d in range(2):
        for h in range(HEADS):
            g0 = (h // rep) * SSM_STATE
            hs_ref[d, h] = ht_scr[d, g0:g0 + SSM_STATE, h * HEAD_W:(h + 1) * HEAD_W].T

    def fin(i, carry):
        r0 = pl.multiple_of(i * blk, blk)
        y = yf_scr[pl.ds(r0, blk), :] + yb_scr[pl.ds(r0, blk), :] + dskip_ref[...] * xc_scr[pl.ds(r0, blk), 0:MIX_W]
        y = y * jax.nn.silu(z_ref[pl.ds(r0, blk), :].astype(f32))
        y = y * lax.rsqrt(jnp.mean(y * y, axis=-1, keepdims=True) + EPS) * nw_ref[...]
        out_ref[pl.ds(r0, blk), :] = y.astype(out_ref.dtype)
        return carry

    lax.fori_loop(0, nblk, fin, 0)


def _ssd(u, g, nb, seq, cw, cb_row, dtb_row, alog_row, dskip_row, nw_row, state):
    has_state = state is not None
    nc = seq // CHUNK
    const = lambda shape: pl.BlockSpec(shape, lambda b: (0,) * len(shape))
    in_specs = [pl.BlockSpec((seq, MIX_W), lambda b: (b, COL_Z // MIX_W)),
                pl.BlockSpec((seq, SSM_CONV_CH), lambda b: (b, COL_XBC // SSM_CONV_CH)),
                pl.BlockSpec((seq, LANE), lambda b: (b, 0)),
                const((SSM_CONV_K, SSM_CONV_CH)), const((1, SSM_CONV_CH)), const((1, LANE)), const((1, LANE)),
                const((1, MIX_W)), const((1, MIX_W))]
    args = [u, u, g, cw, cb_row, dtb_row, alog_row, dskip_row, nw_row]
    if has_state:
        in_specs.append(pl.BlockSpec((None, 2, HEADS, HEAD_W, SSM_STATE), lambda b: (b, 0, 0, 0, 0)))
        args.append(state)
    return pl.pallas_call(
        functools.partial(_ssd_body, seq, has_state),
        grid=(nb,),
        in_specs=in_specs,
        out_specs=[pl.BlockSpec((seq, MIX_W), lambda b: (b, 0)),
                   pl.BlockSpec((None, 2, HEADS, HEAD_W, SSM_STATE), lambda b: (b, 0, 0, 0, 0))],
        out_shape=[jax.ShapeDtypeStruct((nb * seq, MIX_W), bf16),
                   jax.ShapeDtypeStruct((nb, 2, HEADS, HEAD_W, SSM_STATE), f32)],
        scratch_shapes=[pltpu.VMEM((seq + 16, SSM_CONV_CH), f32), pltpu.VMEM((seq, SSM_CONV_CH), f32),
                        pltpu.VMEM((2, seq, MIX_W), f32), pltpu.VMEM((2, seq, MIX_W), f32),
                        pltpu.VMEM((2, nc, 1, MIX_W), f32),
                        pltpu.VMEM((2, SSM_GROUPS * SSM_STATE, MIX_W), f32),
                        pltpu.VMEM((seq, MIX_W), f32), pltpu.VMEM((seq, MIX_W), f32)],
        compiler_params=_params(("arbitrary",)),
        name="ssd_state" if has_state else "ssd",
    )(*args)


def _with_ones(v):
    vb = v.astype(bf16)
    ones = jnp.ones((v.shape[0], HEAD_W), bf16)
    return jnp.concatenate([x for h in range(HEADS) for x in (vb[:, h * HEAD_W:(h + 1) * HEAD_W], ones)], axis=1)


def _softmax_av(scores, vexts):
    m = scores[0].max(axis=1, keepdims=True)
    for s in scores[1:]:
        m = jnp.maximum(m, s.max(axis=1, keepdims=True))
    r = None
    for s, vx in zip(scores, vexts):
        t = jnp.dot(jnp.exp(s - m).astype(bf16), vx, preferred_element_type=f32)
        r = t if r is None else r + t
    return r[:, :HEAD_W] / r[:, HEAD_W:]


def _na_prompt_body(q_ref, k_ref, v_ref, qn_ref, kn_ref, o_ref, ko_ref, vo_ref):
    scale = HEAD_W ** -0.5
    qb = (_seg_rms(q_ref[...].astype(f32), HEAD_W, qn_ref[...]) * scale).astype(bf16)
    kn = _seg_rms(k_ref[...].astype(f32), HEAD_W, kn_ref[...])
    v = v_ref[...].astype(f32)
    ko_ref[...] = kn
    vo_ref[...] = v
    kt = kn.T.astype(bf16)
    vx = _with_ones(v)
    for h in range(HEADS):
        hl = slice(h * HEAD_W, (h + 1) * HEAD_W)
        s = jnp.dot(qb[:, hl], kt[hl, :], preferred_element_type=f32)
        o_ref[:, hl] = _softmax_av([s], [vx[:, 2 * h * HEAD_W:2 * (h + 1) * HEAD_W]]).astype(o_ref.dtype)


def _na_prompt(u, nb, seq, qn_row, kn_row):
    cb = COL_NA // MIX_W
    blk = lambda j: pl.BlockSpec((seq, MIX_W), lambda b: (b, j))
    const = pl.BlockSpec((1, MIX_W), lambda b: (0, 0))
    sds = jax.ShapeDtypeStruct((nb * seq, MIX_W), f32)
    return pl.pallas_call(
        _na_prompt_body,
        grid=(nb,),
        in_specs=[blk(cb), blk(cb + 1), blk(cb + 2), const, const],
        out_specs=[blk(0), blk(0), blk(0)],
        out_shape=[jax.ShapeDtypeStruct(sds.shape, bf16), sds, sds],
        compiler_params=_params(("arbitrary",)),
        name="na_prompt",
    )(u, u, u, qn_row, kn_row)


def _na_sample_body(qblk, q_ref, k_ref, v_ref, kc_ref, vc_ref, rpb_ref, qn_ref, kn_ref, o_ref,
                    kt_scr, vx_scr, bias_scr):
    b = pl.program_id(0)
    i = pl.program_id(1)
    seq = k_ref.shape[0]
    rows = seq // GRID_W

    @pl.when((b == 0) & (i == 0))
    def _():
        j = lax.broadcasted_iota(jnp.int32, (GRID_W, LANE), 0)
        lane = lax.broadcasted_iota(jnp.int32, (GRID_W, LANE), 1)
        j2 = lane % GRID_W
        cs = jnp.clip(j - NA_WIN_C // 2, 0, GRID_W - NA_WIN_C)
        ok = (j2 >= cs) & (j2 < cs + NA_WIN_C)
        left = lane < GRID_W
        neg = jnp.full((GRID_W, LANE), NEG, f32)
        for h in range(HEADS):
            lo, hi = [], []
            for dr in range(2 * NA_WIN_R - 1):
                row = jnp.broadcast_to(rpb_ref[h, dr:dr + 1, :], (GRID_W, LANE))
                lo.append(jnp.where(ok, pltpu.roll(row, LANE - (NA_WIN_C - 1), 1, stride=1, stride_axis=0), NEG))
                hi.append(jnp.where(ok, pltpu.roll(row, GRID_W - (NA_WIN_C - 1), 1, stride=1, stride_axis=0), NEG))
            for qi in range(rows):
                rs = min(max(qi - NA_WIN_R // 2, 0), rows - NA_WIN_R)
                for pair in range(rows // 2):
                    k_even, k_odd = 2 * pair, 2 * pair + 1
                    a = lo[k_even - qi + NA_WIN_R - 1] if rs <= k_even < rs + NA_WIN_R else neg
                    c = hi[k_odd - qi + NA_WIN_R - 1] if rs <= k_odd < rs + NA_WIN_R else neg
                    bias_scr[h, qi * GRID_W:(qi + 1) * GRID_W, pair * LANE:(pair + 1) * LANE] = jnp.where(left, a, c)

    @pl.when(i == 0)
    def _():
        kn = _seg_rms(k_ref[...].astype(f32), HEAD_W, kn_ref[...])
        kt_scr[:, 0:seq] = kn.T.astype(bf16)
        kt_scr[:, seq:] = kc_ref[...].T.astype(bf16)
        vx_scr[0:seq, :] = _with_ones(v_ref[...])
        vx_scr[seq:, :] = _with_ones(vc_ref[...])

    r0 = pl.multiple_of(i * qblk, qblk)
    qb = (_seg_rms(q_ref[...].astype(f32), HEAD_W, qn_ref[...]) * (HEAD_W ** -0.5)).astype(bf16)
    for h in range(HEADS):
        hl = slice(h * HEAD_W, (h + 1) * HEAD_W)
        xl = slice(2 * h * HEAD_W, 2 * (h + 1) * HEAD_W)
        s_own = jnp.dot(qb[:, hl], kt_scr[hl, 0:seq], preferred_element_type=f32) + bias_scr[h, pl.ds(r0, qblk), :]
        s_ctx = jnp.dot(qb[:, hl], kt_scr[hl, seq:], preferred_element_type=f32)
        o = _softmax_av([s_own, s_ctx], [vx_scr[0:seq, xl], vx_scr[seq:, xl]])
        o_ref[:, hl] = o.astype(o_ref.dtype)


def _na_sample(u, nb, layer, cache_k, cache_v, rpb_pad, qn_row, kn_row):
    seq = DEC_SEQ
    qblk = 256
    nq = seq // qblk
    cb = COL_NA // MIX_W
    full = lambda j: pl.BlockSpec((seq, MIX_W), lambda b, i: (b, j))
    ctx = pl.BlockSpec((None, None, PAST_LEN, MIX_W), lambda b, i: (b, layer, 0, 0))
    const = pl.BlockSpec((1, MIX_W), lambda b, i: (0, 0))
    return pl.pallas_call(
        functools.partial(_na_sample_body, qblk),
        grid=(nb, nq),
        in_specs=[pl.BlockSpec((qblk, MIX_W), lambda b, i: (b * nq + i, cb)), full(cb + 1), full(cb + 2),
                  ctx, ctx,
                  pl.BlockSpec((HEADS, 2 * NA_WIN_R, LANE), lambda b, i: (0, 0, 0)),
                  const, const],
        out_specs=pl.BlockSpec((qblk, MIX_W), lambda b, i: (b * nq + i, 0)),
        out_shape=jax.ShapeDtypeStruct((nb * seq, MIX_W), bf16),
        scratch_shapes=[pltpu.VMEM((MIX_W, seq + PAST_LEN), bf16), pltpu.VMEM((seq + PAST_LEN, 2 * MIX_W), bf16),
                        pltpu.VMEM((HEADS, seq, seq), f32)],
        compiler_params=_params(("arbitrary", "arbitrary")),
        name="na_sample",
    )(u, u, u, cache_k, cache_v, rpb_pad, qn_row, kn_row)


def _na_rpb_pad(rpb):
    c = rpb.shape[2]
    return jnp.pad(rpb.astype(f32), ((0, 0), (0, 1), (0, LANE - c)))


def _diff_lambda(lam_ref, lam_init):
    lv = lam_ref[...]
    s1 = jnp.sum(lv[0:1, :] * lv[1:2, :], axis=1, keepdims=True)
    s2 = jnp.sum(lv[2:3, :] * lv[3:4, :], axis=1, keepdims=True)
    return jnp.exp(s1) - jnp.exp(s2) + lam_init


def _diff_heads(q, kt, vext, lam):
    outs = []
    for h in range(HEADS):
        o = []
        for m in range(2):
            c0 = h * HEAD_W + m * DIFF_QK_DIM
            s = jnp.dot(q[:, c0:c0 + DIFF_QK_DIM], kt[c0:c0 + DIFF_QK_DIM, :], preferred_element_type=f32)
            o.append(_softmax_av([s], [vext[:, 2 * h * HEAD_W:2 * (h + 1) * HEAD_W]]))
        outs.append(o[0] - lam * o[1])
    return jnp.concatenate(outs, axis=1)


def _diff_prompt_body(lam_init, q_ref, k_ref, v_ref, qn_ref, kn_ref, lam_ref, nw_ref, o_ref, ko_ref, vo_ref):
    lam = _diff_lambda(lam_ref, lam_init)
    qn = _seg_rms(q_ref[...].astype(f32), DIFF_QK_DIM, qn_ref[...]) * (DIFF_QK_DIM ** -0.5)
    kn = _seg_rms(k_ref[...].astype(f32), DIFF_QK_DIM, kn_ref[...])
    v = v_ref[...].astype(f32)
    ko_ref[...] = kn
    vo_ref[...] = v
    o = _diff_heads(qn.astype(bf16), kn.T.astype(bf16), _with_ones(v), lam)
    o_ref[...] = (_seg_rms(o, HEAD_W, nw_ref[...]) * (1.0 - lam_init)).astype(o_ref.dtype)


def _diff_prompt(u, nb, seq, lam_init, qn_row, kn_row, lam_vecs, nw_row):
    cb = COL_DIFF // MIX_W
    blk = lambda j: pl.BlockSpec((seq, MIX_W), lambda b: (b, j))
    const = pl.BlockSpec((1, MIX_W), lambda b: (0, 0))
    sds = jax.ShapeDtypeStruct((nb * seq, MIX_W), f32)
    return pl.pallas_call(
        functools.partial(_diff_prompt_body, lam_init),
        grid=(nb,),
        in_specs=[blk(cb), blk(cb + 1), blk(cb + 2), const, const,
                  pl.BlockSpec((4, DIFF_QK_DIM), lambda b: (0, 0)), const],
        out_specs=[blk(0), blk(0), blk(0)],
        out_shape=[jax.ShapeDtypeStruct(sds.shape, bf16), sds, sds],
        compiler_params=_params(("arbitrary",)),
        name="diff_prompt",
    )(u, u, u, qn_row, kn_row, lam_vecs, nw_row)


def _rope(x, cos, sin):
    w = x.shape[-1]
    up = pltpu.roll(x, w - 8, 1)
    dn = pltpu.roll(x, 8, 1)
    first = (lax.broadcasted_iota(jnp.int32, x.shape, 1) % 16) < 8
    return x * cos + jnp.where(first, up, dn) * sin


def _diff_sample_body(lam_init, qblk, q_ref, k_ref, v_ref, kc_ref, vc_ref, cos_ref, sin_ref, qn_ref, kn_ref,
                      lam_ref, nw_ref, o_ref, kt_scr, vx_scr):
    i = pl.program_id(1)
    seq = k_ref.shape[0]

    @pl.when(i == 0)
    def _():
        kn = _seg_rms(k_ref[...].astype(f32), DIFF_QK_DIM, kn_ref[...])
        kt_scr[:, 0:seq] = _rope(kn, cos_ref[...], sin_ref[...]).T.astype(bf16)
        kt_scr[:, seq:] = kc_ref[...].T.astype(bf16)
        vx_scr[0:seq, :] = _with_ones(v_ref[...])
        vx_scr[seq:, :] = _with_ones(vc_ref[...])

    lam = _diff_lambda(lam_ref, lam_init)
    r0 = pl.multiple_of(i * qblk, qblk)
    qn = _seg_rms(q_ref[...].astype(f32), DIFF_QK_DIM, qn_ref[...])
    qn = _rope(qn, cos_ref[pl.ds(r0, qblk), :], sin_ref[pl.ds(r0, qblk), :]) * (DIFF_QK_DIM ** -0.5)
    o = _diff_heads(qn.astype(bf16), kt_scr[...], vx_scr[...], lam)
    o_ref[...] = (_seg_rms(o, HEAD_W, nw_ref[...]) * (1.0 - lam_init)).astype(o_ref.dtype)


def _diff_sample(u, nb, layer, lam_init, cache_k, cache_v, cos, sin, qn_row, kn_row, lam_vecs, nw_row):
    seq = DEC_SEQ
    qblk = 512
    nq = seq // qblk
    cb = COL_DIFF // MIX_W
    full = lambda j: pl.BlockSpec((seq, MIX_W), lambda b, i: (b, j))
    ctx = pl.BlockSpec((None, None, PAST_LEN, MIX_W), lambda b, i: (b, layer, 0, 0))
    const = pl.BlockSpec((1, MIX_W), lambda b, i: (0, 0))
    tab = pl.BlockSpec((seq, MIX_W), lambda b, i: (0, 0))
    return pl.pallas_call(
        functools.partial(_diff_sample_body, lam_init, qblk),
        grid=(nb, nq),
        in_specs=[pl.BlockSpec((qblk, MIX_W), lambda b, i: (b * nq + i, cb)), full(cb + 1), full(cb + 2),
                  ctx, ctx, tab, tab, const, const,
                  pl.BlockSpec((4, DIFF_QK_DIM), lambda b, i: (0, 0)), const],
        out_specs=pl.BlockSpec((qblk, MIX_W), lambda b, i: (b * nq + i, 0)),
        out_shape=jax.ShapeDtypeStruct((nb * seq, MIX_W), bf16),
        scratch_shapes=[pltpu.VMEM((MIX_W, seq + PAST_LEN), bf16), pltpu.VMEM((seq + PAST_LEN, 2 * MIX_W), bf16)],
        compiler_params=_params(("arbitrary", "arbitrary")),
        name="diff_sample",
    )(u, u, u, cache_k, cache_v, cos, sin, qn_row, kn_row, lam_vecs, nw_row)


def _rope_tables():
    t = jnp.arange(DEC_SEQ)
    nf = DIFF_QK_DIM // 4
    inv = ROPE_BASE ** (-jnp.arange(nf, dtype=f32) / nf)
    ar = (t // GRID_W).astype(f32)[:, None] * inv
    ac = (t % GRID_W).astype(f32)[:, None] * inv
    cos = jnp.concatenate([jnp.cos(ar), jnp.cos(ar), jnp.cos(ac), jnp.cos(ac)], axis=1)
    sin = jnp.concatenate([-jnp.sin(ar), jnp.sin(ar), -jnp.sin(ac), jnp.sin(ac)], axis=1)
    reps = MIX_W // DIFF_QK_DIM
    return jnp.tile(cos, (1, reps)), jnp.tile(sin, (1, reps))


def _merge_body(x_ref, a_ref, b_ref, c_ref, d_ref, gp_ref, mod_ref, wb_ref, wo_ref, nw_ref, wr_ref,
                xo_ref, h2_ref, aff_ref):
    mixed = None
    for n, br in enumerate((a_ref, b_ref, c_ref, d_ref)):
        gate = jax.nn.sigmoid(gp_ref[:, n * D_MODEL:(n + 1) * D_MODEL].astype(f32))
        t = gate * jnp.dot(br[...].astype(bf16), wb_ref[n], preferred_element_type=f32)
        mixed = t if mixed is None else mixed + t
    y = jnp.dot(mixed.astype(bf16), wo_ref[...], preferred_element_type=f32)
    x = x_ref[...] + mod_ref[:, 2 * D_MODEL:3 * D_MODEL] * y
    xo_ref[...] = x
    h = x * lax.rsqrt(jnp.mean(x * x, axis=-1, keepdims=True) + EPS) * nw_ref[...]
    h = h * (1.0 + mod_ref[:, 4 * D_MODEL:5 * D_MODEL]) + mod_ref[:, 3 * D_MODEL:4 * D_MODEL]
    hb = h.astype(bf16)
    h2_ref[...] = hb
    logits = jnp.dot(hb, wr_ref[...], preferred_element_type=f32)
    lane = lax.broadcasted_iota(jnp.int32, logits.shape, 1)
    logits = jnp.where(lane < N_EXPERTS, logits, NEG)
    e = jnp.exp(logits - logits.max(axis=1, keepdims=True))
    aff_ref[...] = e / e.sum(axis=1, keepdims=True)


def _merge(x, brs, u, mod3, mod_base, mod_stride, seq, wb, wo, nw_row, wr):
    n = x.shape[0]
    tm = 512
    tps = max(seq // tm, 1)
    row = lambda w: pl.BlockSpec((tm, w), lambda i: (i, 0))
    return pl.pallas_call(
        _merge_body,
        grid=(n // tm,),
        in_specs=[row(D_MODEL), row(MIX_W), row(MIX_W), row(MIX_W), row(MIX_W),
                  pl.BlockSpec((tm, N_BRANCH * D_MODEL), lambda i: (i, COL_GPRE // (N_BRANCH * D_MODEL))),
                  pl.BlockSpec((None, 1, 6 * D_MODEL), lambda i: (mod_base + mod_stride * (i // tps), 0, 0)),
                  pl.BlockSpec((N_BRANCH, MIX_W, D_MODEL), lambda i: (0, 0, 0)),
                  pl.BlockSpec((D_MODEL, D_MODEL), lambda i: (0, 0)),
                  pl.BlockSpec((1, D_MODEL), lambda i: (0, 0)),
                  pl.BlockSpec((D_MODEL, LANE), lambda i: (0, 0))],
        out_specs=[row(D_MODEL), row(D_MODEL), row(LANE)],
        out_shape=[jax.ShapeDtypeStruct((n, D_MODEL), f32), jax.ShapeDtypeStruct((n, D_MODEL), bf16),
                   jax.ShapeDtypeStruct((n, LANE), f32)],
        compiler_params=_params(("arbitrary",)),
        name="merge",
    )(x, *brs, u, mod3, wb, wo, nw_row, wr)


def _rows_prefix_sum(x):
    n = x.shape[0]
    pos = lax.broadcasted_iota(jnp.int32, x.shape, 0)
    k = 1
    while k < n:
        x = x + jnp.where(pos >= k, pltpu.roll(x, k, 0), 0.0)
        k *= 2
    return x


def _route_body(t, cap, bpg, aff_ref, h2_ref, xg_ref, g_ref, rank_ref, rrow_scr, arow_scr, oh_scr):
    bb = pl.program_id(1)

    @pl.when(bb == 0)
    def _():
        aff = aff_ref[...]
        tok = lax.broadcasted_iota(jnp.int32, aff.shape, 0)
        val = aff
        k = 2
        while k <= t:
            j = k // 2
            while j >= 1:
                lower = (tok & j) == 0
                pv = jnp.where(lower, pltpu.roll(val, t - j, 0), pltpu.roll(val, j, 0))
                val = jnp.where(lower == ((tok & k) == 0), jnp.maximum(val, pv), jnp.minimum(val, pv))
                j //= 2
            k *= 2
        thr = val[cap - 1:cap, :]
        above = aff > thr
        tie = jnp.where(aff == thr, 1.0, 0.0)
        need = cap - jnp.sum(jnp.where(above, 1.0, 0.0), axis=0, keepdims=True)
        sel = jnp.where(above | ((tie > 0.0) & (_rows_prefix_sum(tie) - tie < need)), 1.0, 0.0)
        rank = jnp.where(sel > 0.0, _rows_prefix_sum(sel) - sel, float(cap))
        rank_ref[...] = rank
        rank_t = rank.T
        aff_t = aff.T
        for k in range(bpg):
            rrow_scr[k] = rank_t[k * N_EXPERTS:(k + 1) * N_EXPERTS, :]
            arow_scr[k] = aff_t[k * N_EXPERTS:(k + 1) * N_EXPERTS, :]

    rk = rrow_scr[bb]
    ar = arow_scr[bb]
    slot = lax.broadcasted_iota(jnp.int32, (cap, t), 0).astype(f32)
    for e in range(N_EXPERTS):
        onehot = jnp.where(rk[e:e + 1, :] == slot, 1.0, 0.0)
        oh_scr[e * cap:(e + 1) * cap, :] = onehot.astype(bf16)
        g_ref[e] = jnp.sum(onehot * ar[e:e + 1, :], axis=1, keepdims=True)
    h2 = h2_ref[...]
    epb = max(1, 512 // cap)
    for e0 in range(0, N_EXPERTS, epb):
        rows = jnp.dot(oh_scr[e0 * cap:(e0 + epb) * cap, :], h2, preferred_element_type=f32)
        xg_ref[e0:e0 + epb] = rows.astype(bf16).reshape(epb, cap, D_MODEL)


def _group_lanes(aff, nb, t, bpg):
    a = aff[:, :N_EXPERTS].reshape(nb // bpg, bpg, t, N_EXPERTS)
    a = jnp.transpose(a, (0, 2, 1, 3)).reshape(nb // bpg * t, bpg * N_EXPERTS)
    return jnp.pad(a, ((0, 0), (0, LANE - bpg * N_EXPERTS)))


def _route(aff, h2, nb, t, bpg):
    cap = CAPACITY_FACTOR * t // N_EXPERTS
    return pl.pallas_call(
        functools.partial(_route_body, t, cap, bpg),
        grid=(nb // bpg, bpg),
        in_specs=[pl.BlockSpec((t, LANE), lambda g, k: (g, 0)),
                  pl.BlockSpec((t, D_MODEL), lambda g, k: (g * bpg + k, 0))],
        out_specs=[pl.BlockSpec((N_EXPERTS, cap, D_MODEL), lambda g, k: (0, g * bpg + k, 0)),
                   pl.BlockSpec((N_EXPERTS, cap, 1), lambda g, k: (0, g * bpg + k, 0)),
                   pl.BlockSpec((t, LANE), lambda g, k: (g, 0))],
        out_shape=[jax.ShapeDtypeStruct((N_EXPERTS, nb * cap, D_MODEL), bf16),
                   jax.ShapeDtypeStruct((N_EXPERTS, nb * cap, 1), f32),
                   jax.ShapeDtypeStruct((nb // bpg * t, LANE), f32)],
        scratch_shapes=[pltpu.VMEM((bpg, N_EXPERTS, t), f32), pltpu.VMEM((bpg, N_EXPERTS, t), f32),
                        pltpu.VMEM((N_EXPERTS * cap, t), bf16)],
        compiler_params=_params(("arbitrary", "arbitrary")),
        name=f"route_{t}",
    )(_group_lanes(aff, nb, t, bpg), h2)


def _ffn_body(xp_ref, xs_ref, gp_ref, gs_ref, wg_ref, wu_ref, wd_ref, yp_ref, ys_ref, wg_scr, wu_scr, wd_scr):
    wg_scr[...] = wg_ref[...].astype(bf16)
    wu_scr[...] = wu_ref[...].astype(bf16)
    wd_scr[...] = wd_ref[...].astype(bf16)
    blk = 512
    for x_ref, g_ref, y_ref in ((xp_ref, gp_ref, yp_ref), (xs_ref, gs_ref, ys_ref)):
        def step(i, carry, x_ref=x_ref, g_ref=g_ref, y_ref=y_ref):
            r0 = pl.multiple_of(i * blk, blk)
            x = x_ref[pl.ds(r0, blk), :]
            a = jnp.dot(x, wg_scr[...], preferred_element_type=f32)
            up = jnp.dot(x, wu_scr[...], preferred_element_type=f32)
            mid = (jax.nn.silu(a) * up).astype(bf16)
            y = jnp.dot(mid, wd_scr[...], preferred_element_type=f32)
            y_ref[pl.ds(r0, blk), :] = (y * g_ref[pl.ds(r0, blk), :]).astype(bf16)
            return carry
        lax.fori_loop(0, x_ref.shape[0] // blk, step, 0)


def _ffn(xp, xs, gp, gs, w_gate, w_up, w_down, layer):
    np_, ns = xp.shape[1], xs.shape[1]
    tok = lambda n, w: pl.BlockSpec((None, n, w), lambda e: (e, 0, 0))
    wspec = lambda a, b: pl.BlockSpec((None, None, a, b), lambda e: (layer, e, 0, 0))
    return pl.pallas_call(
        _ffn_body,
        grid=(N_EXPERTS,),
        in_specs=[tok(np_, D_MODEL), tok(ns, D_MODEL), tok(np_, 1), tok(ns, 1),
                  wspec(D_MODEL, EXPERT_FF), wspec(D_MODEL, EXPERT_FF), wspec(EXPERT_FF, D_MODEL)],
        out_specs=[tok(np_, D_MODEL), tok(ns, D_MODEL)],
        out_shape=[jax.ShapeDtypeStruct(xp.shape, bf16), jax.ShapeDtypeStruct(xs.shape, bf16)],
        scratch_shapes=[pltpu.VMEM((D_MODEL, EXPERT_FF), bf16), pltpu.VMEM((D_MODEL, EXPERT_FF), bf16),
                        pltpu.VMEM((EXPERT_FF, D_MODEL), bf16)],
        compiler_params=_params(("arbitrary",)),
        name="expert_ffn",
    )(xp, xs, gp, gs, w_gate, w_up, w_down)


def _combine_body(t, cap, bpg, x_ref, rank_ref, y_ref, mod_ref, o_ref):
    k = N_EXPERTS * cap
    col0 = (pl.program_id(0) % bpg) * N_EXPERTS
    er = lax.broadcasted_iota(jnp.int32, (LANE, k), 0)
    ec = lax.broadcasted_iota(jnp.int32, (LANE, k), 1) // cap
    expand = jnp.where(er == ec + col0, 1.0, 0.0).astype(bf16)
    rexp = jnp.dot(rank_ref[...].astype(bf16), expand, preferred_element_type=f32)
    slot = (lax.broadcasted_iota(jnp.int32, (t, k), 1) % cap).astype(f32)
    pt = jnp.where(rexp == slot, 1.0, 0.0).astype(bf16)
    moe = jnp.dot(pt, y_ref[...].reshape(k, D_MODEL), preferred_element_type=f32)
    o_ref[...] = x_ref[...] + mod_ref[:, 5 * D_MODEL:6 * D_MODEL] * moe


def _combine(x, rank, y, mod3, mod_base, mod_stride, nb, t, bpg):
    cap = CAPACITY_FACTOR * t // N_EXPERTS
    return pl.pallas_call(
        functools.partial(_combine_body, t, cap, bpg),
        grid=(nb,),
        in_specs=[pl.BlockSpec((t, D_MODEL), lambda b: (b, 0)),
                  pl.BlockSpec((t, LANE), lambda b: (b // bpg, 0)),
                  pl.BlockSpec((N_EXPERTS, cap, D_MODEL), lambda b: (0, b, 0)),
                  pl.BlockSpec((None, 1, 6 * D_MODEL), lambda b: (mod_base + mod_stride * b, 0, 0))],
        out_specs=pl.BlockSpec((t, D_MODEL), lambda b: (b, 0)),
        out_shape=jax.ShapeDtypeStruct(x.shape, f32),
        compiler_params=_params(("arbitrary",)),
        name=f"combine_{t}",
    )(x, rank, y, mod3)


def _pad_row(v, off, width=LANE):
    return jnp.zeros((1, width), f32).at[0, off:off + v.shape[0]].set(v.astype(f32))


def _layer_params(l, a):
    w_in = a["w_in"][l]
    pts, acc = [], 0
    splits = (MIX_W,) * 4 + (4 * HEADS,) + (MIX_W,) * 7 + (SSM_CONV_CH, 2 * HEADS, N_BRANCH * D_MODEL)
    for s in splits:
        pts.append((acc, acc + s))
        acc += s
    seg = lambda i: w_in[:, pts[i][0]:pts[i][1]]
    order = [14, 0, 1, 2, 3, 5, 6, 7, 8, 9, 10, 12, 11]
    w_main = jnp.concatenate([seg(i) for i in order] + [jnp.zeros((D_MODEL, U_COLS - U_USED), f32)],
                             axis=1).astype(bf16)
    w_side = jnp.concatenate([seg(4), seg(13), jnp.zeros((D_MODEL, LANE - 24), f32)], axis=1).astype(bf16)
    tile = lambda v, reps: jnp.tile(v.astype(f32), reps).reshape(1, -1)
    p = dict(
        w_main=w_main, w_side=w_side,
        norm1=a["norm1_w"][l].reshape(1, -1), norm2=a["norm2_w"][l].reshape(1, -1),
        mlstm_gb=_pad_row(a["mlstm_gate_b"][l].reshape(-1), G_MLSTM),
        mlstm_nw=a["mlstm_norm_w"][l].reshape(1, -1),
        na_qn=tile(a["na_qnorm_w"][l], HEADS), na_kn=tile(a["na_knorm_w"][l], HEADS),
        na_rpb=_na_rpb_pad(a["na_rpb"][l]),
        diff_qn=tile(a["diff_qnorm_w"][l], 2 * HEADS), diff_kn=tile(a["diff_knorm_w"][l], 2 * HEADS),
        diff_lam=a["diff_lambda"][l], diff_nw=tile(a["diff_norm_w"][l], HEADS),
        ssm_cw=a["ssm_conv_w"][l], ssm_cb=a["ssm_conv_b"][l].reshape(1, -1),
        ssm_dtb=_pad_row(a["ssm_dt_bias"][l].reshape(-1), G_DT),
        ssm_alog=_pad_row(a["ssm_a_log"][l].reshape(-1), G_DT),
        ssm_d=jnp.repeat(a["ssm_d"][l].astype(f32), HEAD_W).reshape(1, -1),
        ssm_nw=a["ssm_norm_w"][l].reshape(1, -1),
        wb=a["w_branch"][l].astype(bf16), wo=a["w_out"][l].astype(bf16),
        wr=jnp.concatenate([a["w_router"][l], jnp.zeros((D_MODEL, LANE - N_EXPERTS), f32)], axis=1).astype(bf16),
        w_gate=a["w_gate"], w_up=a["w_up"], w_down=a["w_down"],
    )
    return p


def _mixers(u, g, nb, seq, p, l, lam_init, ctx, tables):
    if ctx is None:
        a_out, m_c, m_n, m_m = _mlstm(u, g, nb, seq, p["mlstm_gb"], p["mlstm_nw"], None)
        b_out, na_k, na_v = _na_prompt(u, nb, seq, p["na_qn"], p["na_kn"])
        c_out, df_k, df_v = _diff_prompt(u, nb, seq, lam_init, p["diff_qn"], p["diff_kn"], p["diff_lam"], p["diff_nw"])
        d_out, s_h = _ssd(u, g, nb, seq, p["ssm_cw"], p["ssm_cb"], p["ssm_dtb"], p["ssm_alog"], p["ssm_d"],
                          p["ssm_nw"], None)
        return (a_out, b_out, c_out, d_out), (na_k, na_v, df_k, df_v, m_c, m_n, m_m, s_h)
    (na_ck, na_cv, df_ck, df_cv, st_c, st_n, st_m, st_s) = ctx
    cos, sin = tables
    a_out = _mlstm(u, g, nb, seq, p["mlstm_gb"], p["mlstm_nw"], (st_c[:, l], st_n[:, l], st_m[:, l]))[0]
    b_out = _na_sample(u, nb, l, na_ck, na_cv, p["na_rpb"], p["na_qn"], p["na_kn"])
    c_out = _diff_sample(u, nb, l, lam_init, df_ck, df_cv, cos, sin, p["diff_qn"], p["diff_kn"], p["diff_lam"],
                         p["diff_nw"])
    d_out = _ssd(u, g, nb, seq, p["ssm_cw"], p["ssm_cb"], p["ssm_dtb"], p["ssm_alog"], p["ssm_d"], p["ssm_nw"],
                 st_s[:, l])[0]
    return (a_out, b_out, c_out, d_out), None


def kernel(x_prompt, x_sample, cache_na_k, cache_na_v, cache_diff_k, cache_diff_v, state_mlstm_C, state_mlstm_n, state_mlstm_m, state_ssm, c, c_ctx, norm1_w, norm2_w, w_ada, b_ada, w_in, mlstm_gate_b, mlstm_norm_w, na_qnorm_w, na_knorm_w, na_rpb, diff_qnorm_w, diff_knorm_w, diff_lambda, diff_norm_w, ssm_conv_w, ssm_conv_b, ssm_dt_bias, ssm_a_log, ssm_d, ssm_norm_w, w_branch, w_out, w_router, w_gate, w_up, w_down):
    a = dict(norm1_w=norm1_w, norm2_w=norm2_w, w_in=w_in, mlstm_gate_b=mlstm_gate_b, mlstm_norm_w=mlstm_norm_w,
             na_qnorm_w=na_qnorm_w, na_knorm_w=na_knorm_w, na_rpb=na_rpb, diff_qnorm_w=diff_qnorm_w,
             diff_knorm_w=diff_knorm_w, diff_lambda=diff_lambda, diff_norm_w=diff_norm_w, ssm_conv_w=ssm_conv_w,
             ssm_conv_b=ssm_conv_b, ssm_dt_bias=ssm_dt_bias, ssm_a_log=ssm_a_log, ssm_d=ssm_d,
             ssm_norm_w=ssm_norm_w, w_branch=w_branch, w_out=w_out, w_router=w_router, w_gate=w_gate, w_up=w_up,
             w_down=w_down)
    nb_p, nb_s = x_prompt.shape[0], x_sample.shape[0]
    xp = x_prompt.reshape(nb_p * SEQ, D_MODEL)
    xs = x_sample.reshape(nb_s * DEC_SEQ, D_MODEL)
    cc = jnp.concatenate([c_ctx[None, :], c, jnp.zeros((8 - 1 - nb_s, D_MODEL), f32)], axis=0)
    ctx = (cache_na_k.reshape(nb_s, DEPTH, PAST_LEN, MIX_W), cache_na_v.reshape(nb_s, DEPTH, PAST_LEN, MIX_W),
           cache_diff_k.reshape(nb_s, DEPTH, PAST_LEN, MIX_W), cache_diff_v.reshape(nb_s, DEPTH, PAST_LEN, MIX_W),
           state_mlstm_C, state_mlstm_n, state_mlstm_m, state_ssm)
    tables = _rope_tables()
    new = []
    for l in range(DEPTH):
        p = _layer_params(l, a)
        lam_init = 0.8 - 0.6 * math.exp(-0.3 * l)
        mod3 = _modulation(cc, w_ada, b_ada, l).reshape(8, 1, 6 * D_MODEL)
        groups = []
        for (x, nb, seq, base, stride, gctx) in ((xp, nb_p, SEQ, 0, 0, None), (xs, nb_s, DEC_SEQ, 1, 1, ctx)):
            u, g = _inproj(x, mod3, base, stride, seq, p["norm1"], p["w_main"], p["w_side"])
            brs, st = _mixers(u, g, nb, seq, p, l, lam_init, gctx, tables)
            if st is not None:
                new.append(st)
            x1, h2, aff = _merge(x, brs, u, mod3, base, stride, seq, p["wb"], p["wo"], p["norm2"], p["wr"])
            bpg = min(nb, LANE // N_EXPERTS)
            xg, gg, rank = _route(aff, h2, nb, seq, bpg)
            groups.append((x1, xg, gg, rank, nb, seq, base, stride, bpg))
        yp, ys = _ffn(groups[0][1], groups[1][1], groups[0][2], groups[1][2], p["w_gate"], p["w_up"], p["w_down"], l)
        outs = []
        for (x1, _, _, rank, nb, seq, base, stride, bpg), y in zip(groups, (yp, ys)):
            outs.append(_combine(x1, rank, y, mod3, base, stride, nb, seq, bpg))
        xp, xs = outs
    stack = lambda i, shape: jnp.stack([new[l][i].reshape(shape) for l in range(DEPTH)], axis=1)
    kv = (nb_p, SEQ, HEADS, HEAD_W)
    return (xp.reshape(x_prompt.shape), xs.reshape(x_sample.shape),
            stack(0, kv), stack(1, kv), stack(2, kv), stack(3, kv),
            stack(4, (nb_p, 2, HEADS, HEAD_W, HEAD_W)), stack(5, (nb_p, 2, HEADS, HEAD_W)),
            stack(6, (nb_p, 2, HEADS)), stack(7, (nb_p, 2, HEADS, HEAD_W, SSM_STATE)))
```

```python
import functools
import math

import jax
import jax.numpy as jnp
from jax import lax
from jax.experimental import pallas as pl
from jax.experimental.pallas import tpu as pltpu

f32 = jnp.float32
bf16 = jnp.bfloat16

D_MODEL = 1024
BATCH = 32
SEQ = 256
DEPTH = 2
DEC_BATCH = 4
DEC_SEQ = 1024
PAST_LEN = 512
GRID_W = 64
MIX_W = D_MODEL // 4
N_BRANCH = 4
HEADS = 4
HEAD_W = MIX_W // HEADS
NA_WIN_R = 8
NA_WIN_C = 16
DIFF_QK_DIM = HEAD_W // 2
SSM_GROUPS = 2
SSM_STATE = 64
SSM_CONV_K = 5
SSM_CONV_CH = MIX_W + 2 * SSM_GROUPS * SSM_STATE
CHUNK = 64
PROMPT_BPS = 2
SCAN_UNROLL = 4
N_EXPERTS = 16
EXPERT_FF = 1024
CAPACITY_FACTOR = 2
ROPE_BASE = 10000.0
EPS = 1e-6
NEG = -1e30

LANE = 128
VMEM_LIMIT = 56 * 1024 * 1024

COL_GPRE = 0
COL_MLSTM = 4096
COL_NA = 5120
COL_DIFF = 5888
COL_XBC = 6656
COL_Z = 7168
U_USED = 7424
U_COLS = 7680
G_MLSTM = 0
G_DT = 16


def _params(sem):
    return pltpu.CompilerParams(dimension_semantics=sem, vmem_limit_bytes=VMEM_LIMIT)


def _mm(a, b):
    return jnp.dot(a.astype(bf16), b.astype(bf16), preferred_element_type=f32)


def _mm_nt(a, b):
    return lax.dot_general(a.astype(bf16), b.astype(bf16), (((1,), (1,)), ((), ())), preferred_element_type=f32)


def _mm_tn(a, b):
    return lax.dot_general(a.astype(bf16), b.astype(bf16), (((0,), (0,)), ((), ())), preferred_element_type=f32)


def _sigmoid(x):
    return 0.5 * jnp.tanh(0.5 * x) + 0.5


def _seg_sum(x, seg):
    w = x.shape[-1]
    r = lax.broadcasted_iota(jnp.int32, (w, w), 0) // seg
    c = lax.broadcasted_iota(jnp.int32, (w, w), 1) // seg
    ones = jnp.where(r == c, 1.0, 0.0).astype(bf16)
    hi = x.astype(bf16)
    lo = (x - hi.astype(f32)).astype(bf16)
    return jnp.dot(hi, ones, preferred_element_type=f32) + jnp.dot(lo, ones, preferred_element_type=f32)


def _seg_rms(x, seg, w_row):
    ms = _seg_sum(x * x, seg) * (1.0 / seg)
    return x * lax.rsqrt(ms + EPS) * w_row


def _mod_body(c_ref, w_ref, b_ref, o_ref):
    o_ref[...] = _mm(jax.nn.silu(c_ref[...]), w_ref[...]) + b_ref[...]


def _modulation(cc, w_ada, b_ada, layer):
    tn = 1536
    n = w_ada.shape[2]
    return pl.pallas_call(
        _mod_body,
        grid=(n // tn,),
        in_specs=[pl.BlockSpec((8, D_MODEL), lambda j: (0, 0)),
                  pl.BlockSpec((None, D_MODEL, tn), lambda j: (layer, 0, j)),
                  pl.BlockSpec((None, 1, tn), lambda j: (layer, 0, j))],
        out_specs=pl.BlockSpec((8, tn), lambda j: (0, j)),
        out_shape=jax.ShapeDtypeStruct((8, n), f32),
        compiler_params=_params(("arbitrary",)),
        name="modulation",
    )(cc, w_ada, b_ada.reshape(b_ada.shape[0], 1, n))


def _inproj_body(ncol, x_ref, mod_ref, nw_ref, w_ref, wg_ref, u_ref, g_ref):
    x = x_ref[...]
    y = x * lax.rsqrt(jnp.mean(x * x, axis=-1, keepdims=True) + EPS) * nw_ref[...]
    hb = (y * (1.0 + mod_ref[:, D_MODEL:2 * D_MODEL]) + mod_ref[:, 0:D_MODEL]).astype(bf16)
    g_ref[...] = jnp.dot(hb, wg_ref[...], preferred_element_type=f32)
    cw = U_COLS // ncol
    for c in range(ncol):
        u_ref[:, c * cw:(c + 1) * cw] = jnp.dot(hb, w_ref[:, c * cw:(c + 1) * cw],
                                                preferred_element_type=f32).astype(u_ref.dtype)


def _inproj(x, mod3, mod_base, mod_stride, seq, nw_row, w_main, w_side):
    n = x.shape[0]
    tm = 512
    tps = max(seq // tm, 1)
    resident = lambda shape: pl.BlockSpec(shape, lambda i: (0, 0), pipeline_mode=pl.Buffered(1))
    return pl.pallas_call(
        functools.partial(_inproj_body, 6),
        grid=(n // tm,),
        in_specs=[pl.BlockSpec((tm, D_MODEL), lambda i: (i, 0)),
                  pl.BlockSpec((None, 1, 6 * D_MODEL), lambda i: (mod_base + mod_stride * (i // tps), 0, 0)),
                  pl.BlockSpec((1, D_MODEL), lambda i: (0, 0)),
                  resident((D_MODEL, U_COLS)), resident((D_MODEL, LANE))],
        out_specs=[pl.BlockSpec((tm, U_COLS), lambda i: (i, 0)),
                   pl.BlockSpec((tm, LANE), lambda i: (i, 0))],
        out_shape=[jax.ShapeDtypeStruct((n, U_COLS), bf16), jax.ShapeDtypeStruct((n, LANE), f32)],
        compiler_params=_params(("arbitrary",)),
        name="inproj",
    )(x, mod3, nw_row, w_main, w_side)


def _split3(x):
    hi = x.astype(bf16)
    r = x - hi.astype(f32)
    mid = r.astype(bf16)
    lo = (r - mid.astype(f32)).astype(bf16)
    return hi, mid, lo


def _select_cols(x, onehot):
    return jnp.dot(jnp.concatenate(_split3(x), axis=1), jnp.concatenate([onehot] * 3, axis=0),
                   preferred_element_type=f32)


def _select_rows(onehot, x):
    return jnp.dot(jnp.concatenate([onehot] * 3, axis=1), jnp.concatenate(_split3(x), axis=0),
                   preferred_element_type=f32)


def _chunk_scan(x, op, identity, reverse):
    n = x.shape[0]
    pos = lax.broadcasted_iota(jnp.int32, x.shape, 0) % CHUNK
    k = 1
    while k < CHUNK:
        if reverse:
            shifted, ok = pltpu.roll(x, n - k, 0), pos < CHUNK - k
        else:
            shifted, ok = pltpu.roll(x, k, 0), pos >= k
        x = op(x, jnp.where(ok, shifted, identity))
        k *= 2
    return x


def _dir_scan(x, op, identity, bwd_col):
    return jnp.where(bwd_col, _chunk_scan(x, op, identity, True), _chunk_scan(x, op, identity, False))


def _head_expand(col0):
    c = lax.broadcasted_iota(jnp.int32, (LANE, MIX_W), 0)
    h = lax.broadcasted_iota(jnp.int32, (LANE, MIX_W), 1) // HEAD_W
    return jnp.where(c == col0 + h, 1.0, 0.0).astype(bf16)


def _chunk_rows(x, seq):
    nrow = max(seq // CHUNK, 8)
    t = lax.broadcasted_iota(jnp.int32, (seq, MIX_W), 0) % CHUNK
    s_ = lax.broadcasted_iota(jnp.int32, (seq, MIX_W), 1) % HEAD_W
    sel = lax.broadcasted_iota(jnp.int32, (nrow, seq), 1) // CHUNK == lax.broadcasted_iota(jnp.int32, (nrow, seq), 0)
    return _select_rows(jnp.where(sel, 1.0, 0.0).astype(bf16), jnp.where(t == s_, x, 0.0))


def _block_diag(x, reps, keep):
    return jnp.concatenate([x] * reps, axis=0) * keep


def _head_mask(rows, cols, rseg, cseg):
    return (lax.broadcasted_iota(jnp.int32, (rows, cols), 0) // rseg) == (lax.broadcasted_iota(jnp.int32, (rows, cols), 1) // cseg)


def _mlstm_body(seq, has_state, *refs):
    if has_state:
        (qkvo_ref, g_ref, gb_ref, nw_ref, c0_ref, n0_ref, m0_ref,
         out_ref, cs_ref, ns_ref, ms_ref, bx_scr, mx_scr, vx_scr, vrow_scr, cbd_scr, hf_scr, hb_scr) = refs
    else:
        (qkvo_ref, g_ref, gb_ref, nw_ref,
         out_ref, cs_ref, ns_ref, ms_ref, bx_scr, mx_scr, vx_scr, vrow_scr, cbd_scr, hf_scr, hb_scr) = refs
    nc = seq // CHUNK
    g = g_ref[...] + gb_ref[...]
    lane = lax.broadcasted_iota(jnp.int32, g.shape, 1)
    bwd_col = (lane % 16) >= 8
    bsum = _dir_scan(jax.nn.log_sigmoid(g), jnp.add, 0.0, bwd_col)
    b_i = pltpu.roll(bsum, LANE - HEADS, 1)
    vcol = g - b_i
    mcol = b_i + _dir_scan(vcol, jnp.maximum, NEG, bwd_col)
    for d in range(2):
        e = _head_expand(G_MLSTM + d * 8)
        bx_scr[d] = _select_cols(b_i, e)
        mx_scr[d] = _select_cols(mcol, e)
        vx = _select_cols(vcol, e)
        vx_scr[d] = vx
        vrow = _chunk_rows(vx, seq)
        for c in range(nc):
            vrow_scr[d, c] = vrow[c:c + 1, :]

    grp = lax.broadcasted_iota(jnp.int32, (1, MIX_W), 1) // HEAD_W
    cbd_scr[...] = jnp.zeros(cbd_scr.shape, f32)
    n_rows, m_rows = [], []
    for d in range(2):
        if has_state:
            for h in range(HEADS):
                cbd_scr[d, h * HEAD_W:(h + 1) * HEAD_W, h * HEAD_W:(h + 1) * HEAD_W] = c0_ref[d, h]
            n_rows.append(jnp.concatenate([n0_ref[d, h:h + 1, :] for h in range(HEADS)], axis=1))
            m_row = jnp.zeros((1, MIX_W), f32)
            for h in range(HEADS):
                m_row = jnp.where(grp == h, m0_ref[d:d + 1, h:h + 1], m_row)
            m_rows.append(m_row)
        else:
            n_rows.append(jnp.zeros((1, MIX_W), f32))
            m_rows.append(jnp.zeros((1, MIX_W), f32))

    li = lax.broadcasted_iota(jnp.int32, (CHUNK, MIX_W), 0)
    si = lax.broadcasted_iota(jnp.int32, (CHUNK, MIX_W), 1) % HEAD_W
    valid = (si <= li, si >= li)
    bd = _head_mask(MIX_W, MIX_W, HEAD_W, HEAD_W)
    bd_f32 = jnp.where(bd, 1.0, 0.0)
    ones_bd = bd_f32.astype(bf16)

    def step(c, carry):
        new = []
        for d in range(2):
            n_row, m_row = carry[2 * d], carry[2 * d + 1]
            cidx = c if d == 0 else nc - 1 - c
            r0 = pl.multiple_of(cidx * CHUNK, CHUNK)
            last = CHUNK - 1 if d == 0 else 0
            q = qkvo_ref[pl.ds(r0, CHUNK), 0:MIX_W].astype(f32)
            k = qkvo_ref[pl.ds(r0, CHUNK), MIX_W:2 * MIX_W].astype(f32) * (HEAD_W ** -0.5)
            vb = qkvo_ref[pl.ds(r0, CHUNK), 2 * MIX_W:3 * MIX_W].astype(bf16)
            qb = q.astype(bf16)
            bx = bx_scr[d, pl.ds(r0, CHUNK), :]
            mx = mx_scr[d, pl.ds(r0, CHUNK), :]
            vx = vx_scr[d, pl.ds(r0, CHUNK), :]
            p = jnp.exp(jnp.where(valid[d], bx + vrow_scr[d, cidx] - mx, NEG))
            smat = _mm_nt(qb, _block_diag(k.astype(bf16), HEADS, ones_bd)) * p
            intra = _mm(smat, _block_diag(vb, HEADS, ones_bd))
            dsum = _mm(smat, ones_bd)
            gg = bx + m_row
            mt = jnp.maximum(gg, mx)
            a = jnp.exp(mx - mt)
            w_prev = jnp.exp(gg - mt)
            inter = _mm(qb, cbd_scr[d])
            dint = _mm(q * n_row, ones_bd)
            num = intra * a + inter * w_prev
            den = dsum * a + dint * w_prev
            h_scr = hf_scr if d == 0 else hb_scr
            h_scr[pl.ds(r0, CHUNK), :] = num / jnp.maximum(jnp.abs(den), jnp.exp(-mt))
            bl = bx[last:last + 1, :]
            mloc = mx[last:last + 1, :]
            m_new = jnp.maximum(bl + m_row, mloc)
            dec = jnp.exp(bl + m_row - m_new)
            fac = jnp.exp(mloc - m_new)
            kw = k * jnp.exp(bl + vx - mloc)
            kv = _mm_tn(kw, vb) * bd_f32
            cbd_scr[d] = cbd_scr[d] * dec + kv * fac
            new += [n_row * dec + jnp.sum(kw, axis=0, keepdims=True) * fac, m_new]
        return tuple(new)

    fin_state = lax.fori_loop(0, nc, step, (n_rows[0], m_rows[0], n_rows[1], m_rows[1]), unroll=SCAN_UNROLL)
    for d in range(2):
        n_row, m_row = fin_state[2 * d], fin_state[2 * d + 1]
        for h in range(HEADS):
            hl = slice(h * HEAD_W, (h + 1) * HEAD_W)
            cs_ref[d, h] = cbd_scr[d, hl, hl]
            ns_ref[d, h:h + 1, :] = n_row[:, hl]
            ms_ref[d:d + 1, h:h + 1] = m_row[:, h * HEAD_W:h * HEAD_W + 1]

    blk = 256

    def fin(i, carry):
        r0 = pl.multiple_of(i * blk, blk)
        hs = hf_scr[pl.ds(r0, blk), :] + hb_scr[pl.ds(r0, blk), :]
        hn = _seg_rms(hs, HEAD_W, nw_ref[...])
        o = qkvo_ref[pl.ds(r0, blk), 3 * MIX_W:4 * MIX_W].astype(f32)
        out_ref[pl.ds(r0, blk), :] = (hn * jax.nn.sigmoid(o)).astype(out_ref.dtype)
        return carry

    lax.fori_loop(0, seq // blk, fin, 0)


def _mlstm(u, g, nb, seq, gb_row, nw_row, state):
    has_state = state is not None
    nc = seq // CHUNK
    in_specs = [pl.BlockSpec((seq, 4 * MIX_W), lambda b: (b, COL_MLSTM // (4 * MIX_W))),
                pl.BlockSpec((seq, LANE), lambda b: (b, 0)),
                pl.BlockSpec((1, LANE), lambda b: (0, 0)),
                pl.BlockSpec((1, MIX_W), lambda b: (0, 0))]
    args = [u, g, gb_row, nw_row]
    if has_state:
        in_specs += [pl.BlockSpec((None, 2, HEADS, HEAD_W, HEAD_W), lambda b: (b, 0, 0, 0, 0)),
                     pl.BlockSpec((None, 2, HEADS, HEAD_W), lambda b: (b, 0, 0, 0)),
                     pl.BlockSpec((None, 2, HEADS), lambda b: (b, 0, 0))]
        args += list(state)
    return pl.pallas_call(
        functools.partial(_mlstm_body, seq, has_state),
        grid=(nb,),
        in_specs=in_specs,
        out_specs=[pl.BlockSpec((seq, MIX_W), lambda b: (b, 0)),
                   pl.BlockSpec((None, 2, HEADS, HEAD_W, HEAD_W), lambda b: (b, 0, 0, 0, 0)),
                   pl.BlockSpec((None, 2, HEADS, HEAD_W), lambda b: (b, 0, 0, 0)),
                   pl.BlockSpec((None, 2, HEADS), lambda b: (b, 0, 0))],
        out_shape=[jax.ShapeDtypeStruct((nb * seq, MIX_W), bf16),
                   jax.ShapeDtypeStruct((nb, 2, HEADS, HEAD_W, HEAD_W), f32),
                   jax.ShapeDtypeStruct((nb, 2, HEADS, HEAD_W), f32),
                   jax.ShapeDtypeStruct((nb, 2, HEADS), f32)],
        scratch_shapes=[pltpu.VMEM((2, seq, MIX_W), f32), pltpu.VMEM((2, seq, MIX_W), f32),
                        pltpu.VMEM((2, seq, MIX_W), f32), pltpu.VMEM((2, nc, 1, MIX_W), f32),
                        pltpu.VMEM((2, MIX_W, MIX_W), f32),
                        pltpu.VMEM((seq, MIX_W), f32), pltpu.VMEM((seq, MIX_W), f32)],
        compiler_params=_params(("arbitrary",)),
        name="mlstm_state" if has_state else "mlstm",
    )(*args)


def _ssd_body(seq, has_state, *refs):
    if has_state:
        (z_ref, xbc_ref, g_ref, cw_ref, cb_ref, dtb_ref, alog_ref, dskip_ref, nw_ref, h0_ref,
         out_ref, hs_ref, xpad_scr, xc_scr, ax_scr, dx_scr, arow_scr, ht_scr, yf_scr, yb_scr) = refs
    else:
        (z_ref, xbc_ref, g_ref, cw_ref, cb_ref, dtb_ref, alog_ref, dskip_ref, nw_ref,
         out_ref, hs_ref, xpad_scr, xc_scr, ax_scr, dx_scr, arow_scr, ht_scr, yf_scr, yb_scr) = refs
    nc = seq // CHUNK
    pad = 8
    blk = 256
    nblk = seq // blk
    xpad_scr[0:pad, :] = jnp.zeros((pad, SSM_CONV_CH), f32)
    xpad_scr[pad + seq:2 * pad + seq, :] = jnp.zeros((pad, SSM_CONV_CH), f32)
    xpad_scr[pad:pad + seq, :] = xbc_ref[...].astype(f32)

    cblk = 128
    for i in range(seq // cblk):
        r0 = i * cblk
        acc = jnp.zeros((cblk, SSM_CONV_CH), f32) + cb_ref[...]
        for kk in range(SSM_CONV_K):
            off = r0 + pad - SSM_CONV_K // 2 + kk
            acc = acc + xpad_scr[off:off + cblk, :] * cw_ref[kk:kk + 1, :]
        xc_scr[r0:r0 + cblk, :] = jax.nn.silu(acc)

    dt = jax.nn.softplus(g_ref[...] + dtb_ref[...])
    lane = lax.broadcasted_iota(jnp.int32, dt.shape, 1)
    acum = _dir_scan(dt * (-jnp.exp(alog_ref[...])), jnp.add, 0.0, (lane % 8) >= HEADS)
    for d in range(2):
        e = _head_expand(G_DT + d * HEADS)
        ax = _select_cols(acum, e)
        ax_scr[d] = ax
        dx_scr[d] = _select_cols(dt, e)
        arow = _chunk_rows(ax, seq)
        for c in range(nc):
            arow_scr[d, c] = arow[c:c + 1, :]

    gn = SSM_GROUPS * SSM_STATE
    rep = HEADS // SSM_GROUPS
    ht_scr[...] = jnp.zeros(ht_scr.shape, f32)
    if has_state:
        for d in range(2):
            for h in range(HEADS):
                g0 = (h // rep) * SSM_STATE
                ht_scr[d, g0:g0 + SSM_STATE, h * HEAD_W:(h + 1) * HEAD_W] = h0_ref[d, h].T
    li = lax.broadcasted_iota(jnp.int32, (CHUNK, MIX_W), 0)
    si = lax.broadcasted_iota(jnp.int32, (CHUNK, MIX_W), 1) % HEAD_W
    valid = (si <= li, si >= li)
    state_keep = jnp.where(_head_mask(gn, MIX_W, SSM_STATE, rep * HEAD_W), 1.0, 0.0)
    key_keep = jnp.where(_head_mask(HEADS * CHUNK, gn, rep * CHUNK, SSM_STATE), 1.0, 0.0).astype(bf16)
    head_keep = jnp.where(_head_mask(MIX_W, MIX_W, HEAD_W, HEAD_W), 1.0, 0.0).astype(bf16)

    def step(c, carry):
        for d in range(2):
            cidx = c if d == 0 else nc - 1 - c
            r0 = pl.multiple_of(cidx * CHUNK, CHUNK)
            last = CHUNK - 1 if d == 0 else 0
            xs = xc_scr[pl.ds(r0, CHUNK), 0:MIX_W]
            bm = xc_scr[pl.ds(r0, CHUNK), MIX_W:MIX_W + gn].astype(bf16)
            cm = xc_scr[pl.ds(r0, CHUNK), MIX_W + gn:MIX_W + 2 * gn].astype(bf16)
            ax = ax_scr[d, pl.ds(r0, CHUNK), :]
            decay = jnp.exp(jnp.where(valid[d], ax - arow_scr[d, cidx], NEG))
            scores = _mm_nt(cm, _block_diag(bm, HEADS, key_keep)) * decay
            xdt = xs * dx_scr[d, pl.ds(r0, CHUNK), :]
            y = _mm(scores, _block_diag(xdt.astype(bf16), HEADS, head_keep)) + _mm(cm, ht_scr[d]) * jnp.exp(ax)
            y_scr = yf_scr if d == 0 else yb_scr
            y_scr[pl.ds(r0, CHUNK), :] = y
            al = ax[last:last + 1, :]
            ht_scr[d] = ht_scr[d] * jnp.exp(al) + _mm_tn(bm, xdt * jnp.exp(al - ax)) * state_keep
        return carry

    lax.fori_loop(0, nc, step, 0, unroll=SCAN_UNROLL)
    for d in range(2):
        for h in range(HEADS):
            g0 = (h // rep) * SSM_STATE
            hs_ref[d, h] = ht_scr[d, g0:g0 + SSM_STATE, h * HEAD_W:(h + 1) * HEAD_W].T

    def fin(i, carry):
        r0 = pl.multiple_of(i * blk, blk)
        y = yf_scr[pl.ds(r0, blk), :] + yb_scr[pl.ds(r0, blk), :] + dskip_ref[...] * xc_scr[pl.ds(r0, blk), 0:MIX_W]
        y = y * jax.nn.silu(z_ref[pl.ds(r0, blk), :].astype(f32))
        y = y * lax.rsqrt(jnp.mean(y * y, axis=-1, keepdims=True) + EPS) * nw_ref[...]
        out_ref[pl.ds(r0, blk), :] = y.astype(out_ref.dtype)
        return carry

    lax.fori_loop(0, nblk, fin, 0)


def _ssd(u, g, nb, seq, cw, cb_row, dtb_row, alog_row, dskip_row, nw_row, state):
    has_state = state is not None
    nc = seq // CHUNK
    const = lambda shape: pl.BlockSpec(shape, lambda b: (0,) * len(shape))
    in_specs = [pl.BlockSpec((seq, MIX_W), lambda b: (b, COL_Z // MIX_W)),
                pl.BlockSpec((seq, SSM_CONV_CH), lambda b: (b, COL_XBC // SSM_CONV_CH)),
                pl.BlockSpec((seq, LANE), lambda b: (b, 0)),
                const((SSM_CONV_K, SSM_CONV_CH)), const((1, SSM_CONV_CH)), const((1, LANE)), const((1, LANE)),
                const((1, MIX_W)), const((1, MIX_W))]
    args = [u, u, g, cw, cb_row, dtb_row, alog_row, dskip_row, nw_row]
    if has_state:
        in_specs.append(pl.BlockSpec((None, 2, HEADS, HEAD_W, SSM_STATE), lambda b: (b, 0, 0, 0, 0)))
        args.append(state)
    return pl.pallas_call(
        functools.partial(_ssd_body, seq, has_state),
        grid=(nb,),
        in_specs=in_specs,
        out_specs=[pl.BlockSpec((seq, MIX_W), lambda b: (b, 0)),
                   pl.BlockSpec((None, 2, HEADS, HEAD_W, SSM_STATE), lambda b: (b, 0, 0, 0, 0))],
        out_shape=[jax.ShapeDtypeStruct((nb * seq, MIX_W), bf16),
                   jax.ShapeDtypeStruct((nb, 2, HEADS, HEAD_W, SSM_STATE), f32)],
        scratch_shapes=[pltpu.VMEM((seq + 16, SSM_CONV_CH), f32), pltpu.VMEM((seq, SSM_CONV_CH), f32),
                        pltpu.VMEM((2, seq, MIX_W), f32), pltpu.VMEM((2, seq, MIX_W), f32),
                        pltpu.VMEM((2, nc, 1, MIX_W), f32),
                        pltpu.VMEM((2, SSM_GROUPS * SSM_STATE, MIX_W), f32),
                        pltpu.VMEM((seq, MIX_W), f32), pltpu.VMEM((seq, MIX_W), f32)],
        compiler_params=_params(("arbitrary",)),
        name="ssd_state" if has_state else "ssd",
    )(*args)


def _with_ones(v):
    vb = v.astype(bf16)
    ones = jnp.ones((v.shape[0], HEAD_W), bf16)
    return jnp.concatenate([x for h in range(HEADS) for x in (vb[:, h * HEAD_W:(h + 1) * HEAD_W], ones)], axis=1)


def _softmax_av(scores, vexts):
    m = scores[0].max(axis=1, keepdims=True)
    for s in scores[1:]:
        m = jnp.maximum(m, s.max(axis=1, keepdims=True))
    r = None
    for s, vx in zip(scores, vexts):
        t = jnp.dot(jnp.exp(s - m).astype(bf16), vx, preferred_element_type=f32)
        r = t if r is None else r + t
    return r[:, :HEAD_W] / r[:, HEAD_W:]


def _na_prompt_body(seq, q_ref, k_ref, v_ref, qn_ref, kn_ref, o_ref, ko_ref, vo_ref):
    scale = HEAD_W ** -0.5
    qb = (_seg_rms(q_ref[...].astype(f32), HEAD_W, qn_ref[...]) * scale).astype(bf16)
    kn = _seg_rms(k_ref[...].astype(f32), HEAD_W, kn_ref[...])
    v = v_ref[...].astype(f32)
    ko_ref[...] = kn
    vo_ref[...] = v
    for s0 in range(0, q_ref.shape[0], seq):
        rs = slice(s0, s0 + seq)
        kt = kn[rs].T.astype(bf16)
        vx = _with_ones(v[rs])
        for h in range(HEADS):
            hl = slice(h * HEAD_W, (h + 1) * HEAD_W)
            s = jnp.dot(qb[rs, hl], kt[hl, :], preferred_element_type=f32)
            o_ref[rs, hl] = _softmax_av([s], [vx[:, 2 * h * HEAD_W:2 * (h + 1) * HEAD_W]]).astype(o_ref.dtype)


def _na_prompt(u, nb, seq, qn_row, kn_row):
    cb = COL_NA // MIX_W
    rows = PROMPT_BPS * seq
    blk = lambda j: pl.BlockSpec((rows, MIX_W), lambda b: (b, j))
    const = pl.BlockSpec((1, MIX_W), lambda b: (0, 0))
    sds = jax.ShapeDtypeStruct((nb * seq, MIX_W), f32)
    return pl.pallas_call(
        functools.partial(_na_prompt_body, seq),
        grid=(nb // PROMPT_BPS,),
        in_specs=[blk(cb), blk(cb + 1), blk(cb + 2), const, const],
        out_specs=[blk(0), blk(0), blk(0)],
        out_shape=[jax.ShapeDtypeStruct(sds.shape, bf16), sds, sds],
        compiler_params=_params(("arbitrary",)),
        name="na_prompt",
    )(u, u, u, qn_row, kn_row)


def _na_sample_body(qblk, q_ref, k_ref, v_ref, kc_ref, vc_ref, rpb_ref, qn_ref, kn_ref, o_ref,
                    kt_scr, vx_scr, bias_scr):
    b = pl.program_id(0)
    i = pl.program_id(1)
    seq = k_ref.shape[0]
    rows = seq // GRID_W

    @pl.when((b == 0) & (i == 0))
    def _():
        j = lax.broadcasted_iota(jnp.int32, (GRID_W, LANE), 0)
        lane = lax.broadcasted_iota(jnp.int32, (GRID_W, LANE), 1)
        j2 = lane % GRID_W
        cs = jnp.clip(j - NA_WIN_C // 2, 0, GRID_W - NA_WIN_C)
        ok = (j2 >= cs) & (j2 < cs + NA_WIN_C)
        left = lane < GRID_W
        neg = jnp.full((GRID_W, LANE), NEG, f32)
        for h in range(HEADS):
            lo, hi = [], []
            for dr in range(2 * NA_WIN_R - 1):
                row = jnp.broadcast_to(rpb_ref[h, dr:dr + 1, :], (GRID_W, LANE))
                lo.append(jnp.where(ok, pltpu.roll(row, LANE - (NA_WIN_C - 1), 1, stride=1, stride_axis=0), NEG))
                hi.append(jnp.where(ok, pltpu.roll(row, GRID_W - (NA_WIN_C - 1), 1, stride=1, stride_axis=0), NEG))
            for qi in range(rows):
                rs = min(max(qi - NA_WIN_R // 2, 0), rows - NA_WIN_R)
                for pair in range(rows // 2):
                    k_even, k_odd = 2 * pair, 2 * pair + 1
                    a = lo[k_even - qi + NA_WIN_R - 1] if rs <= k_even < rs + NA_WIN_R else neg
                    c = hi[k_odd - qi + NA_WIN_R - 1] if rs <= k_odd < rs + NA_WIN_R else neg
                    bias_scr[h, qi * GRID_W:(qi + 1) * GRID_W, pair * LANE:(pair + 1) * LANE] = jnp.where(left, a, c)

    @pl.when(i == 0)
    def _():
        kn = _seg_rms(k_ref[...].astype(f32), HEAD_W, kn_ref[...])
        kt_scr[:, 0:seq] = kn.T.astype(bf16)
        kt_scr[:, seq:] = kc_ref[...].T.astype(bf16)
        vx_scr[0:seq, :] = _with_ones(v_ref[...])
        vx_scr[seq:, :] = _with_ones(vc_ref[...])

    r0 = pl.multiple_of(i * qblk, qblk)
    qb = (_seg_rms(q_ref[...].astype(f32), HEAD_W, qn_ref[...]) * (HEAD_W ** -0.5)).astype(bf16)
    for h in range(HEADS):
        hl = slice(h * HEAD_W, (h + 1) * HEAD_W)
        xl = slice(2 * h * HEAD_W, 2 * (h + 1) * HEAD_W)
        s_own = jnp.dot(qb[:, hl], kt_scr[hl, 0:seq], preferred_element_type=f32) + bias_scr[h, pl.ds(r0, qblk), :]
        s_ctx = jnp.dot(qb[:, hl], kt_scr[hl, seq:], preferred_element_type=f32)
        o = _softmax_av([s_own, s_ctx], [vx_scr[0:seq, xl], vx_scr[seq:, xl]])
        o_ref[:, hl] = o.astype(o_ref.dtype)


def _na_sample(u, nb, layer, cache_k, cache_v, rpb_pad, qn_row, kn_row):
    seq = DEC_SEQ
    qblk = 256
    nq = seq // qblk
    cb = COL_NA // MIX_W
    full = lambda j: pl.BlockSpec((seq, MIX_W), lambda b, i: (b, j))
    ctx = pl.BlockSpec((None, None, PAST_LEN, MIX_W), lambda b, i: (b, layer, 0, 0))
    const = pl.BlockSpec((1, MIX_W), lambda b, i: (0, 0))
    return pl.pallas_call(
        functools.partial(_na_sample_body, qblk),
        grid=(nb, nq),
        in_specs=[pl.BlockSpec((qblk, MIX_W), lambda b, i: (b * nq + i, cb)), full(cb + 1), full(cb + 2),
                  ctx, ctx,
                  pl.BlockSpec((HEADS, 2 * NA_WIN_R, LANE), lambda b, i: (0, 0, 0)),
                  const, const],
        out_specs=pl.BlockSpec((qblk, MIX_W), lambda b, i: (b * nq + i, 0)),
        out_shape=jax.ShapeDtypeStruct((nb * seq, MIX_W), bf16),
        scratch_shapes=[pltpu.VMEM((MIX_W, seq + PAST_LEN), bf16), pltpu.VMEM((seq + PAST_LEN, 2 * MIX_W), bf16),
                        pltpu.VMEM((HEADS, seq, seq), f32)],
        compiler_params=_params(("arbitrary", "arbitrary")),
        name="na_sample",
    )(u, u, u, cache_k, cache_v, rpb_pad, qn_row, kn_row)


def _na_rpb_pad(rpb):
    c = rpb.shape[2]
    return jnp.pad(rpb.astype(f32), ((0, 0), (0, 1), (0, LANE - c)))


def _diff_lambda(lam_ref, lam_init):
    lv = lam_ref[...]
    s1 = jnp.sum(lv[0:1, :] * lv[1:2, :], axis=1, keepdims=True)
    s2 = jnp.sum(lv[2:3, :] * lv[3:4, :], axis=1, keepdims=True)
    return jnp.exp(s1) - jnp.exp(s2) + lam_init


def _diff_heads(q, kt, vext, lam):
    outs = []
    for h in range(HEADS):
        o = []
        for m in range(2):
            c0 = h * HEAD_W + m * DIFF_QK_DIM
            s = jnp.dot(q[:, c0:c0 + DIFF_QK_DIM], kt[c0:c0 + DIFF_QK_DIM, :], preferred_element_type=f32)
            o.append(_softmax_av([s], [vext[:, 2 * h * HEAD_W:2 * (h + 1) * HEAD_W]]))
        outs.append(o[0] - lam * o[1])
    return jnp.concatenate(outs, axis=1)


def _diff_prompt_body(lam_init, seq, q_ref, k_ref, v_ref, qn_ref, kn_ref, lam_ref, nw_ref, o_ref, ko_ref, vo_ref):
    lam = _diff_lambda(lam_ref, lam_init)
    qb = (_seg_rms(q_ref[...].astype(f32), DIFF_QK_DIM, qn_ref[...]) * (DIFF_QK_DIM ** -0.5)).astype(bf16)
    kn = _seg_rms(k_ref[...].astype(f32), DIFF_QK_DIM, kn_ref[...])
    v = v_ref[...].astype(f32)
    ko_ref[...] = kn
    vo_ref[...] = v
    for s0 in range(0, q_ref.shape[0], seq):
        rs = slice(s0, s0 + seq)
        o = _diff_heads(qb[rs], kn[rs].T.astype(bf16), _with_ones(v[rs]), lam)
        o_ref[rs, :] = (_seg_rms(o, HEAD_W, nw_ref[...]) * (1.0 - lam_init)).astype(o_ref.dtype)


def _diff_prompt(u, nb, seq, lam_init, qn_row, kn_row, lam_vecs, nw_row):
    cb = COL_DIFF // MIX_W
    rows = PROMPT_BPS * seq
    blk = lambda j: pl.BlockSpec((rows, MIX_W), lambda b: (b, j))
    const = pl.BlockSpec((1, MIX_W), lambda b: (0, 0))
    sds = jax.ShapeDtypeStruct((nb * seq, MIX_W), f32)
    return pl.pallas_call(
        functools.partial(_diff_prompt_body, lam_init, seq),
        grid=(nb // PROMPT_BPS,),
        in_specs=[blk(cb), blk(cb + 1), blk(cb + 2), const, const,
                  pl.BlockSpec((4, DIFF_QK_DIM), lambda b: (0, 0)), const],
        out_specs=[blk(0), blk(0), blk(0)],
        out_shape=[jax.ShapeDtypeStruct(sds.shape, bf16), sds, sds],
        compiler_params=_params(("arbitrary",)),
        name="diff_prompt",
    )(u, u, u, qn_row, kn_row, lam_vecs, nw_row)


def _rope(x, cos, sin):
    w = x.shape[-1]
    up = pltpu.roll(x, w - 8, 1)
    dn = pltpu.roll(x, 8, 1)
    first = (lax.broadcasted_iota(jnp.int32, x.shape, 1) % 16) < 8
    return x * cos + jnp.where(first, up, dn) * sin


def _diff_sample_body(lam_init, qblk, q_ref, k_ref, v_ref, kc_ref, vc_ref, cos_ref, sin_ref, qn_ref, kn_ref,
                      lam_ref, nw_ref, o_ref, kt_scr, vx_scr):
    i = pl.program_id(1)
    seq = k_ref.shape[0]

    @pl.when(i == 0)
    def _():
        kn = _seg_rms(k_ref[...].astype(f32), DIFF_QK_DIM, kn_ref[...])
        kt_scr[:, 0:seq] = _rope(kn, cos_ref[...], sin_ref[...]).T.astype(bf16)
        kt_scr[:, seq:] = kc_ref[...].T.astype(bf16)
        vx_scr[0:seq, :] = _with_ones(v_ref[...])
        vx_scr[seq:, :] = _with_ones(vc_ref[...])

    lam = _diff_lambda(lam_ref, lam_init)
    r0 = pl.multiple_of(i * qblk, qblk)
    qn = _seg_rms(q_ref[...].astype(f32), DIFF_QK_DIM, qn_ref[...])
    qn = _rope(qn, cos_ref[pl.ds(r0, qblk), :], sin_ref[pl.ds(r0, qblk), :]) * (DIFF_QK_DIM ** -0.5)
    o = _diff_heads(qn.astype(bf16), kt_scr[...], vx_scr[...], lam)
    o_ref[...] = (_seg_rms(o, HEAD_W, nw_ref[...]) * (1.0 - lam_init)).astype(o_ref.dtype)


def _diff_sample(u, nb, layer, lam_init, cache_k, cache_v, cos, sin, qn_row, kn_row, lam_vecs, nw_row):
    seq = DEC_SEQ
    qblk = 512
    nq = seq // qblk
    cb = COL_DIFF // MIX_W
    full = lambda j: pl.BlockSpec((seq, MIX_W), lambda b, i: (b, j))
    ctx = pl.BlockSpec((None, None, PAST_LEN, MIX_W), lambda b, i: (b, layer, 0, 0))
    const = pl.BlockSpec((1, MIX_W), lambda b, i: (0, 0))
    tab = pl.BlockSpec((seq, MIX_W), lambda b, i: (0, 0))
    return pl.pallas_call(
        functools.partial(_diff_sample_body, lam_init, qblk),
        grid=(nb, nq),
        in_specs=[pl.BlockSpec((qblk, MIX_W), lambda b, i: (b * nq + i, cb)), full(cb + 1), full(cb + 2),
                  ctx, ctx, tab, tab, const, const,
                  pl.BlockSpec((4, DIFF_QK_DIM), lambda b, i: (0, 0)), const],
        out_specs=pl.BlockSpec((qblk, MIX_W), lambda b, i: (b * nq + i, 0)),
        out_shape=jax.ShapeDtypeStruct((nb * seq, MIX_W), bf16),
        scratch_shapes=[pltpu.VMEM((MIX_W, seq + PAST_LEN), bf16), pltpu.VMEM((seq + PAST_LEN, 2 * MIX_W), bf16)],
        compiler_params=_params(("arbitrary", "arbitrary")),
        name="diff_sample",
    )(u, u, u, cache_k, cache_v, cos, sin, qn_row, kn_row, lam_vecs, nw_row)


def _rope_tables():
    t = jnp.arange(DEC_SEQ)
    nf = DIFF_QK_DIM // 4
    inv = ROPE_BASE ** (-jnp.arange(nf, dtype=f32) / nf)
    ar = (t // GRID_W).astype(f32)[:, None] * inv
    ac = (t % GRID_W).astype(f32)[:, None] * inv
    cos = jnp.concatenate([jnp.cos(ar), jnp.cos(ar), jnp.cos(ac), jnp.cos(ac)], axis=1)
    sin = jnp.concatenate([-jnp.sin(ar), jnp.sin(ar), -jnp.sin(ac), jnp.sin(ac)], axis=1)
    reps = MIX_W // DIFF_QK_DIM
    return jnp.tile(cos, (1, reps)), jnp.tile(sin, (1, reps))


def _merge_body(x_ref, a_ref, b_ref, c_ref, d_ref, gp_ref, mod_ref, wb_ref, wo_ref, nw_ref, wr_ref,
                xo_ref, h2_ref, aff_ref):
    mixed = None
    for n, br in enumerate((a_ref, b_ref, c_ref, d_ref)):
        gate = _sigmoid(gp_ref[:, n * D_MODEL:(n + 1) * D_MODEL].astype(f32))
        t = gate * jnp.dot(br[...].astype(bf16), wb_ref[n], preferred_element_type=f32)
        mixed = t if mixed is None else mixed + t
    y = jnp.dot(mixed.astype(bf16), wo_ref[...], preferred_element_type=f32)
    x = x_ref[...] + mod_ref[:, 2 * D_MODEL:3 * D_MODEL] * y
    xo_ref[...] = x
    h = x * lax.rsqrt(jnp.mean(x * x, axis=-1, keepdims=True) + EPS) * nw_ref[...]
    h = h * (1.0 + mod_ref[:, 4 * D_MODEL:5 * D_MODEL]) + mod_ref[:, 3 * D_MODEL:4 * D_MODEL]
    hb = h.astype(bf16)
    h2_ref[...] = hb
    logits = jnp.dot(hb, wr_ref[...], preferred_element_type=f32)
    lane = lax.broadcasted_iota(jnp.int32, logits.shape, 1)
    logits = jnp.where(lane < N_EXPERTS, logits, NEG)
    e = jnp.exp(logits - logits.max(axis=1, keepdims=True))
    aff_ref[...] = e / e.sum(axis=1, keepdims=True)


def _merge(x, brs, u, mod3, mod_base, mod_stride, seq, wb, wo, nw_row, wr):
    n = x.shape[0]
    tm = 512
    tps = max(seq // tm, 1)
    row = lambda w: pl.BlockSpec((tm, w), lambda i: (i, 0))
    return pl.pallas_call(
        _merge_body,
        grid=(n // tm,),
        in_specs=[row(D_MODEL), row(MIX_W), row(MIX_W), row(MIX_W), row(MIX_W),
                  pl.BlockSpec((tm, N_BRANCH * D_MODEL), lambda i: (i, COL_GPRE // (N_BRANCH * D_MODEL))),
                  pl.BlockSpec((None, 1, 6 * D_MODEL), lambda i: (mod_base + mod_stride * (i // tps), 0, 0)),
                  pl.BlockSpec((N_BRANCH, MIX_W, D_MODEL), lambda i: (0, 0, 0)),
                  pl.BlockSpec((D_MODEL, D_MODEL), lambda i: (0, 0)),
                  pl.BlockSpec((1, D_MODEL), lambda i: (0, 0)),
                  pl.BlockSpec((D_MODEL, LANE), lambda i: (0, 0))],
        out_specs=[row(D_MODEL), row(D_MODEL), row(LANE)],
        out_shape=[jax.ShapeDtypeStruct((n, D_MODEL), f32), jax.ShapeDtypeStruct((n, D_MODEL), bf16),
                   jax.ShapeDtypeStruct((n, LANE), f32)],
        compiler_params=_params(("arbitrary",)),
        name="merge",
    )(x, *brs, u, mod3, wb, wo, nw_row, wr)


def _rows_prefix_sum(x):
    n = x.shape[0]
    pos = lax.broadcasted_iota(jnp.int32, x.shape, 0)
    k = 1
    while k < n:
        x = x + jnp.where(pos >= k, pltpu.roll(x, k, 0), 0.0)
        k *= 2
    return x


def _route_body(t, cap, bpg, aff_ref, h2_ref, xg_ref, g_ref, rank_ref, rrow_scr, arow_scr, oh_scr):
    bb = pl.program_id(1)

    @pl.when(bb == 0)
    def _():
        aff = aff_ref[...]
        tok = lax.broadcasted_iota(jnp.int32, aff.shape, 0)
        val = aff
        k = 2
        while k <= t:
            j = k // 2
            while j >= 1:
                lower = (tok & j) == 0
                pv = jnp.where(lower, pltpu.roll(val, t - j, 0), pltpu.roll(val, j, 0))
                val = jnp.where(lower == ((tok & k) == 0), jnp.maximum(val, pv), jnp.minimum(val, pv))
                j //= 2
            k *= 2
        thr = val[cap - 1:cap, :]
        above = aff > thr
        tie = jnp.where(aff == thr, 1.0, 0.0)
        need = cap - jnp.sum(jnp.where(above, 1.0, 0.0), axis=0, keepdims=True)
        sel = jnp.where(above | ((tie > 0.0) & (_rows_prefix_sum(tie) - tie < need)), 1.0, 0.0)
        rank = jnp.where(sel > 0.0, _rows_prefix_sum(sel) - sel, float(cap))
        rank_ref[...] = rank
        rank_t = rank.T
        aff_t = aff.T
        for k in range(bpg):
            rrow_scr[k] = rank_t[k * N_EXPERTS:(k + 1) * N_EXPERTS, :]
            arow_scr[k] = aff_t[k * N_EXPERTS:(k + 1) * N_EXPERTS, :]

    rk = rrow_scr[bb]
    ar = arow_scr[bb]
    slot = lax.broadcasted_iota(jnp.int32, (cap, t), 0).astype(f32)
    for e in range(N_EXPERTS):
        onehot = jnp.where(rk[e:e + 1, :] == slot, 1.0, 0.0)
        oh_scr[e * cap:(e + 1) * cap, :] = onehot.astype(bf16)
        g_ref[e] = jnp.sum(onehot * ar[e:e + 1, :], axis=1, keepdims=True)
    h2 = h2_ref[...]
    epb = max(1, 512 // cap)
    for e0 in range(0, N_EXPERTS, epb):
        rows = jnp.dot(oh_scr[e0 * cap:(e0 + epb) * cap, :], h2, preferred_element_type=f32)
        xg_ref[e0:e0 + epb] = rows.astype(bf16).reshape(epb, cap, D_MODEL)


def _group_lanes(aff, nb, t, bpg):
    a = aff[:, :N_EXPERTS].reshape(nb // bpg, bpg, t, N_EXPERTS)
    a = jnp.transpose(a, (0, 2, 1, 3)).reshape(nb // bpg * t, bpg * N_EXPERTS)
    return jnp.pad(a, ((0, 0), (0, LANE - bpg * N_EXPERTS)))


def _route(aff, h2, nb, t, bpg):
    cap = CAPACITY_FACTOR * t // N_EXPERTS
    return pl.pallas_call(
        functools.partial(_route_body, t, cap, bpg),
        grid=(nb // bpg, bpg),
        in_specs=[pl.BlockSpec((t, LANE), lambda g, k: (g, 0)),
                  pl.BlockSpec((t, D_MODEL), lambda g, k: (g * bpg + k, 0))],
        out_specs=[pl.BlockSpec((N_EXPERTS, cap, D_MODEL), lambda g, k: (0, g * bpg + k, 0)),
                   pl.BlockSpec((N_EXPERTS, cap, 1), lambda g, k: (0, g * bpg + k, 0)),
                   pl.BlockSpec((t, LANE), lambda g, k: (g, 0))],
        out_shape=[jax.ShapeDtypeStruct((N_EXPERTS, nb * cap, D_MODEL), bf16),
                   jax.ShapeDtypeStruct((N_EXPERTS, nb * cap, 1), f32),
                   jax.ShapeDtypeStruct((nb // bpg * t, LANE), f32)],
        scratch_shapes=[pltpu.VMEM((bpg, N_EXPERTS, t), f32), pltpu.VMEM((bpg, N_EXPERTS, t), f32),
                        pltpu.VMEM((N_EXPERTS * cap, t), bf16)],
        compiler_params=_params(("arbitrary", "arbitrary")),
        name=f"route_{t}",
    )(_group_lanes(aff, nb, t, bpg), h2)


def _ffn_body(xp_ref, xs_ref, gp_ref, gs_ref, wg_ref, wu_ref, wd_ref, yp_ref, ys_ref, wg_scr, wu_scr, wd_scr):
    wg_scr[...] = wg_ref[...].astype(bf16)
    wu_scr[...] = wu_ref[...].astype(bf16)
    wd_scr[...] = wd_ref[...].astype(bf16)
    blk = 512
    for x_ref, g_ref, y_ref in ((xp_ref, gp_ref, yp_ref), (xs_ref, gs_ref, ys_ref)):
        def step(i, carry, x_ref=x_ref, g_ref=g_ref, y_ref=y_ref):
            r0 = pl.multiple_of(i * blk, blk)
            x = x_ref[pl.ds(r0, blk), :]
            a = jnp.dot(x, wg_scr[...], preferred_element_type=f32)
            up = jnp.dot(x, wu_scr[...], preferred_element_type=f32)
            mid = (jax.nn.silu(a) * up).astype(bf16)
            y = jnp.dot(mid, wd_scr[...], preferred_element_type=f32)
            y_ref[pl.ds(r0, blk), :] = (y * g_ref[pl.ds(r0, blk), :]).astype(bf16)
            return carry
        lax.fori_loop(0, x_ref.shape[0] // blk, step, 0)


def _ffn(xp, xs, gp, gs, w_gate, w_up, w_down, layer):
    np_, ns = xp.shape[1], xs.shape[1]
    tok = lambda n, w: pl.BlockSpec((None, n, w), lambda e: (e, 0, 0))
    wspec = lambda a, b: pl.BlockSpec((None, None, a, b), lambda e: (layer, e, 0, 0))
    return pl.pallas_call(
        _ffn_body,
        grid=(N_EXPERTS,),
        in_specs=[tok(np_, D_MODEL), tok(ns, D_MODEL), tok(np_, 1), tok(ns, 1),
                  wspec(D_MODEL, EXPERT_FF), wspec(D_MODEL, EXPERT_FF), wspec(EXPERT_FF, D_MODEL)],
        out_specs=[tok(np_, D_MODEL), tok(ns, D_MODEL)],
        out_shape=[jax.ShapeDtypeStruct(xp.shape, bf16), jax.ShapeDtypeStruct(xs.shape, bf16)],
        scratch_shapes=[pltpu.VMEM((D_MODEL, EXPERT_FF), bf16), pltpu.VMEM((D_MODEL, EXPERT_FF), bf16),
                        pltpu.VMEM((EXPERT_FF, D_MODEL), bf16)],
        compiler_params=_params(("arbitrary",)),
        name="expert_ffn",
    )(xp, xs, gp, gs, w_gate, w_up, w_down)


def _combine_body(t, cap, bpg, bps, x_ref, rank_ref, y_ref, mod_ref, o_ref):
    k = N_EXPERTS * cap
    er = lax.broadcasted_iota(jnp.int32, (LANE, k), 0)
    ec = lax.broadcasted_iota(jnp.int32, (LANE, k), 1) // cap
    slot = (lax.broadcasted_iota(jnp.int32, (t, k), 1) % cap).astype(f32)
    rank = rank_ref[...].astype(bf16)
    for sub in range(bps):
        col0 = ((pl.program_id(0) * bps + sub) % bpg) * N_EXPERTS
        expand = jnp.where(er == ec + col0, 1.0, 0.0).astype(bf16)
        rexp = jnp.dot(rank, expand, preferred_element_type=f32)
        pt = jnp.where(rexp == slot, 1.0, 0.0).astype(bf16)
        y = y_ref[:, sub * cap:(sub + 1) * cap, :].reshape(k, D_MODEL)
        moe = jnp.dot(pt, y, preferred_element_type=f32)
        rs = slice(sub * t, (sub + 1) * t)
        o_ref[rs, :] = x_ref[rs, :] + mod_ref[:, 5 * D_MODEL:6 * D_MODEL] * moe


def _combine(x, rank, y, mod3, mod_base, mod_stride, nb, t, bpg):
    cap = CAPACITY_FACTOR * t // N_EXPERTS
    bps = 2 if (mod_stride == 0 and bpg % 2 == 0) else 1
    return pl.pallas_call(
        functools.partial(_combine_body, t, cap, bpg, bps),
        grid=(nb // bps,),
        in_specs=[pl.BlockSpec((bps * t, D_MODEL), lambda b: (b, 0)),
                  pl.BlockSpec((t, LANE), lambda b: (b * bps // bpg, 0)),
                  pl.BlockSpec((N_EXPERTS, bps * cap, D_MODEL), lambda b: (0, b, 0)),
                  pl.BlockSpec((None, 1, 6 * D_MODEL), lambda b: (mod_base + mod_stride * b, 0, 0))],
        out_specs=pl.BlockSpec((bps * t, D_MODEL), lambda b: (b, 0)),
        out_shape=jax.ShapeDtypeStruct(x.shape, f32),
        compiler_params=_params(("arbitrary",)),
        name=f"combine_{t}",
    )(x, rank, y, mod3)


def _pad_row(v, off, width=LANE):
    return jnp.zeros((1, width), f32).at[0, off:off + v.shape[0]].set(v.astype(f32))


def _repack_body(pts, w_ref, main_ref, side_ref):
    w = w_ref[...]
    seg = lambda i: w[:, pts[i][0]:pts[i][1]]
    rows = w.shape[0]
    order = [14, 0, 1, 2, 3, 5, 6, 7, 8, 9, 10, 12, 11]
    main_ref[...] = jnp.concatenate([seg(i) for i in order] + [jnp.zeros((rows, U_COLS - U_USED), f32)],
                                    axis=1).astype(bf16)
    side_ref[...] = jnp.concatenate([seg(4), seg(13), jnp.zeros((rows, LANE - 24), f32)], axis=1).astype(bf16)


def _repack_w_in(w_in, layer):
    pts, acc = [], 0
    splits = (MIX_W,) * 4 + (4 * HEADS,) + (MIX_W,) * 7 + (SSM_CONV_CH, 2 * HEADS, N_BRANCH * D_MODEL)
    for s in splits:
        pts.append((acc, acc + s))
        acc += s
    tk = 256
    return pl.pallas_call(
        functools.partial(_repack_body, tuple(pts)),
        grid=(D_MODEL // tk,),
        in_specs=[pl.BlockSpec((None, tk, acc), lambda i: (layer, i, 0))],
        out_specs=[pl.BlockSpec((tk, U_COLS), lambda i: (i, 0)), pl.BlockSpec((tk, LANE), lambda i: (i, 0))],
        out_shape=[jax.ShapeDtypeStruct((D_MODEL, U_COLS), bf16), jax.ShapeDtypeStruct((D_MODEL, LANE), bf16)],
        compiler_params=_params(("arbitrary",)),
        name="repack_w_in",
    )(w_in)


def _layer_params(l, a):
    w_main, w_side = _repack_w_in(a["w_in"], l)
    tile = lambda v, reps: jnp.tile(v.astype(f32), reps).reshape(1, -1)
    p = dict(
        w_main=w_main, w_side=w_side,
        norm1=a["norm1_w"][l].reshape(1, -1), norm2=a["norm2_w"][l].reshape(1, -1),
        mlstm_gb=_pad_row(a["mlstm_gate_b"][l].reshape(-1), G_MLSTM),
        mlstm_nw=a["mlstm_norm_w"][l].reshape(1, -1),
        na_qn=tile(a["na_qnorm_w"][l], HEADS), na_kn=tile(a["na_knorm_w"][l], HEADS),
        na_rpb=_na_rpb_pad(a["na_rpb"][l]),
        diff_qn=tile(a["diff_qnorm_w"][l], 2 * HEADS), diff_kn=tile(a["diff_knorm_w"][l], 2 * HEADS),
        diff_lam=a["diff_lambda"][l], diff_nw=tile(a["diff_norm_w"][l], HEADS),
        ssm_cw=a["ssm_conv_w"][l], ssm_cb=a["ssm_conv_b"][l].reshape(1, -1),
        ssm_dtb=_pad_row(a["ssm_dt_bias"][l].reshape(-1), G_DT),
        ssm_alog=_pad_row(a["ssm_a_log"][l].reshape(-1), G_DT),
        ssm_d=jnp.repeat(a["ssm_d"][l].astype(f32), HEAD_W).reshape(1, -1),
        ssm_nw=a["ssm_norm_w"][l].reshape(1, -1),
        wb=a["w_branch"][l].astype(bf16), wo=a["w_out"][l].astype(bf16),
        wr=jnp.concatenate([a["w_router"][l], jnp.zeros((D_MODEL, LANE - N_EXPERTS), f32)], axis=1).astype(bf16),
        w_gate=a["w_gate"], w_up=a["w_up"], w_down=a["w_down"],
    )
    return p


def _mixers(u, g, nb, seq, p, l, lam_init, ctx, tables):
    if ctx is None:
        a_out, m_c, m_n, m_m = _mlstm(u, g, nb, seq, p["mlstm_gb"], p["mlstm_nw"], None)
        b_out, na_k, na_v = _na_prompt(u, nb, seq, p["na_qn"], p["na_kn"])
        c_out, df_k, df_v = _diff_prompt(u, nb, seq, lam_init, p["diff_qn"], p["diff_kn"], p["diff_lam"], p["diff_nw"])
        d_out, s_h = _ssd(u, g, nb, seq, p["ssm_cw"], p["ssm_cb"], p["ssm_dtb"], p["ssm_alog"], p["ssm_d"],
                          p["ssm_nw"], None)
        return (a_out, b_out, c_out, d_out), (na_k, na_v, df_k, df_v, m_c, m_n, m_m, s_h)
    (na_ck, na_cv, df_ck, df_cv, st_c, st_n, st_m, st_s) = ctx
    cos, sin = tables
    a_out = _mlstm(u, g, nb, seq, p["mlstm_gb"], p["mlstm_nw"], (st_c[:, l], st_n[:, l], st_m[:, l]))[0]
    b_out = _na_sample(u, nb, l, na_ck, na_cv, p["na_rpb"], p["na_qn"], p["na_kn"])
    c_out = _diff_sample(u, nb, l, lam_init, df_ck, df_cv, cos, sin, p["diff_qn"], p["diff_kn"], p["diff_lam"],
                         p["diff_nw"])
    d_out = _ssd(u, g, nb, seq, p["ssm_cw"], p["ssm_cb"], p["ssm_dtb"], p["ssm_alog"], p["ssm_d"], p["ssm_nw"],
                 st_s[:, l])[0]
    return (a_out, b_out, c_out, d_out), None


def kernel(x_prompt, x_sample, cache_na_k, cache_na_v, cache_diff_k, cache_diff_v, state_mlstm_C, state_mlstm_n, state_mlstm_m, state_ssm, c, c_ctx, norm1_w, norm2_w, w_ada, b_ada, w_in, mlstm_gate_b, mlstm_norm_w, na_qnorm_w, na_knorm_w, na_rpb, diff_qnorm_w, diff_knorm_w, diff_lambda, diff_norm_w, ssm_conv_w, ssm_conv_b, ssm_dt_bias, ssm_a_log, ssm_d, ssm_norm_w, w_branch, w_out, w_router, w_gate, w_up, w_down):
    a = dict(norm1_w=norm1_w, norm2_w=norm2_w, w_in=w_in, mlstm_gate_b=mlstm_gate_b, mlstm_norm_w=mlstm_norm_w,
             na_qnorm_w=na_qnorm_w, na_knorm_w=na_knorm_w, na_rpb=na_rpb, diff_qnorm_w=diff_qnorm_w,
             diff_knorm_w=diff_knorm_w, diff_lambda=diff_lambda, diff_norm_w=diff_norm_w, ssm_conv_w=ssm_conv_w,
             ssm_conv_b=ssm_conv_b, ssm_dt_bias=ssm_dt_bias, ssm_a_log=ssm_a_log, ssm_d=ssm_d,
             ssm_norm_w=ssm_norm_w, w_branch=w_branch, w_out=w_out, w_router=w_router, w_gate=w_gate, w_up=w_up,
             w_down=w_down)
    nb_p, nb_s = x_prompt.shape[0], x_sample.shape[0]
    xp = x_prompt.reshape(nb_p * SEQ, D_MODEL)
    xs = x_sample.reshape(nb_s * DEC_SEQ, D_MODEL)
    cc = jnp.concatenate([c_ctx[None, :], c, jnp.zeros((8 - 1 - nb_s, D_MODEL), f32)], axis=0)
    ctx = (cache_na_k.reshape(nb_s, DEPTH, PAST_LEN, MIX_W), cache_na_v.reshape(nb_s, DEPTH, PAST_LEN, MIX_W),
           cache_diff_k.reshape(nb_s, DEPTH, PAST_LEN, MIX_W), cache_diff_v.reshape(nb_s, DEPTH, PAST_LEN, MIX_W),
           state_mlstm_C, state_mlstm_n, state_mlstm_m, state_ssm)
    tables = _rope_tables()
    new = []
    for l in range(DEPTH):
        p = _layer_params(l, a)
        lam_init = 0.8 - 0.6 * math.exp(-0.3 * l)
        mod3 = _modulation(cc, w_ada, b_ada, l).reshape(8, 1, 6 * D_MODEL)
        groups = []
        for (x, nb, seq, base, stride, gctx) in ((xp, nb_p, SEQ, 0, 0, None), (xs, nb_s, DEC_SEQ, 1, 1, ctx)):
            u, g = _inproj(x, mod3, base, stride, seq, p["norm1"], p["w_main"], p["w_side"])
            brs, st = _mixers(u, g, nb, seq, p, l, lam_init, gctx, tables)
            if st is not None:
                new.append(st)
            x1, h2, aff = _merge(x, brs, u, mod3, base, stride, seq, p["wb"], p["wo"], p["norm2"], p["wr"])
            bpg = min(nb, LANE // N_EXPERTS)
            xg, gg, rank = _route(aff, h2, nb, seq, bpg)
            groups.append((x1, xg, gg, rank, nb, seq, base, stride, bpg))
        yp, ys = _ffn(groups[0][1], groups[1][1], groups[0][2], groups[1][2], p["w_gate"], p["w_up"], p["w_down"], l)
        outs = []
        for (x1, _, _, rank, nb, seq, base, stride, bpg), y in zip(groups, (yp, ys)):
            outs.append(_combine(x1, rank, y, mod3, base, stride, nb, seq, bpg))
        xp, xs = outs
    stack = lambda i, shape: jnp.stack([new[l][i].reshape(shape) for l in range(DEPTH)], axis=1)
    kv = (nb_p, SEQ, HEADS, HEAD_W)
    return (xp.reshape(x_prompt.shape), xs.reshape(x_sample.shape),
            stack(0, kv), stack(1, kv), stack(2, kv), stack(3, kv),
            stack(4, (nb_p, 2, HEADS, HEAD_W, HEAD_W)), stack(5, (nb_p, 2, HEADS, HEAD_W)),
            stack(6, (nb_p, 2, HEADS)), stack(7, (nb_p, 2, HEADS, HEAD_W, SSM_STATE)))
```

```python
import functools
import math

import jax
import jax.numpy as jnp
from jax import lax
from jax.experimental import pallas as pl
from jax.experimental.pallas import tpu as pltpu

f32 = jnp.float32
bf16 = jnp.bfloat16

D_MODEL = 1024
BATCH = 32
SEQ = 256
DEPTH = 2
DEC_BATCH = 4
DEC_SEQ = 1024
PAST_LEN = 512
GRID_W = 64
MIX_W = D_MODEL // 4
N_BRANCH = 4
HEADS = 4
HEAD_W = MIX_W // HEADS
NA_WIN_R = 8
NA_WIN_C = 16
DIFF_QK_DIM = HEAD_W // 2
SSM_GROUPS = 2
SSM_STATE = 64
SSM_CONV_K = 5
SSM_CONV_CH = MIX_W + 2 * SSM_GROUPS * SSM_STATE
CHUNK = 64
PROMPT_BPS = 4
SCAN_UNROLL = 8
N_EXPERTS = 16
EXPERT_FF = 1024
CAPACITY_FACTOR = 2
ROPE_BASE = 10000.0
EPS = 1e-6
NEG = -1e30

LANE = 128
VMEM_LIMIT = 56 * 1024 * 1024

COL_GPRE = 0
COL_MLSTM = 4096
COL_NA = 5120
COL_DIFF = 5888
COL_XBC = 6656
COL_Z = 7168
U_USED = 7424
U_COLS = 7680
G_MLSTM = 0
G_DT = 16


def _params(sem):
    return pltpu.CompilerParams(dimension_semantics=sem, vmem_limit_bytes=VMEM_LIMIT)


def _mm(a, b):
    return jnp.dot(a.astype(bf16), b.astype(bf16), preferred_element_type=f32)


def _mm_nt(a, b):
    return lax.dot_general(a.astype(bf16), b.astype(bf16), (((1,), (1,)), ((), ())), preferred_element_type=f32)


def _mm_tn(a, b):
    return lax.dot_general(a.astype(bf16), b.astype(bf16), (((0,), (0,)), ((), ())), preferred_element_type=f32)


def _sigmoid(x):
    return 0.5 * jnp.tanh(0.5 * x) + 0.5


def _seg_sum(x, seg):
    w = x.shape[-1]
    r = lax.broadcasted_iota(jnp.int32, (w, w), 0) // seg
    c = lax.broadcasted_iota(jnp.int32, (w, w), 1) // seg
    ones = jnp.where(r == c, 1.0, 0.0).astype(bf16)
    hi = x.astype(bf16)
    lo = (x - hi.astype(f32)).astype(bf16)
    return jnp.dot(hi, ones, preferred_element_type=f32) + jnp.dot(lo, ones, preferred_element_type=f32)


def _seg_rms(x, seg, w_row):
    ms = _seg_sum(x * x, seg) * (1.0 / seg)
    return x * lax.rsqrt(ms + EPS) * w_row


def _mod_body(c_ref, w_ref, b_ref, o_ref):
    o_ref[...] = _mm(jax.nn.silu(c_ref[...]), w_ref[...]) + b_ref[...]


def _modulation(cc, w_ada, b_ada, layer):
    tn = 1536
    n = w_ada.shape[2]
    return pl.pallas_call(
        _mod_body,
        grid=(n // tn,),
        in_specs=[pl.BlockSpec((8, D_MODEL), lambda j: (0, 0)),
                  pl.BlockSpec((None, D_MODEL, tn), lambda j: (layer, 0, j)),
                  pl.BlockSpec((None, 1, tn), lambda j: (layer, 0, j))],
        out_specs=pl.BlockSpec((8, tn), lambda j: (0, j)),
        out_shape=jax.ShapeDtypeStruct((8, n), f32),
        compiler_params=_params(("arbitrary",)),
        name="modulation",
    )(cc, w_ada, b_ada.reshape(b_ada.shape[0], 1, n))


def _inproj_body(ncol, x_ref, mod_ref, nw_ref, w_ref, wg_ref, u_ref, g_ref):
    x = x_ref[...]
    y = x * lax.rsqrt(jnp.mean(x * x, axis=-1, keepdims=True) + EPS) * nw_ref[...]
    hb = (y * (1.0 + mod_ref[:, D_MODEL:2 * D_MODEL]) + mod_ref[:, 0:D_MODEL]).astype(bf16)
    g_ref[...] = jnp.dot(hb, wg_ref[...], preferred_element_type=f32)
    cw = U_COLS // ncol
    for c in range(ncol):
        u_ref[:, c * cw:(c + 1) * cw] = jnp.dot(hb, w_ref[:, c * cw:(c + 1) * cw],
                                                preferred_element_type=f32).astype(u_ref.dtype)


def _inproj(x, mod3, mod_base, mod_stride, seq, nw_row, w_main, w_side):
    n = x.shape[0]
    tm = 512
    tps = max(seq // tm, 1)
    resident = lambda shape: pl.BlockSpec(shape, lambda i: (0, 0), pipeline_mode=pl.Buffered(1))
    return pl.pallas_call(
        functools.partial(_inproj_body, 3),
        grid=(n // tm,),
        in_specs=[pl.BlockSpec((tm, D_MODEL), lambda i: (i, 0)),
                  pl.BlockSpec((None, 1, 6 * D_MODEL), lambda i: (mod_base + mod_stride * (i // tps), 0, 0)),
                  pl.BlockSpec((1, D_MODEL), lambda i: (0, 0)),
                  resident((D_MODEL, U_COLS)), resident((D_MODEL, LANE))],
        out_specs=[pl.BlockSpec((tm, U_COLS), lambda i: (i, 0)),
                   pl.BlockSpec((tm, LANE), lambda i: (i, 0))],
        out_shape=[jax.ShapeDtypeStruct((n, U_COLS), bf16), jax.ShapeDtypeStruct((n, LANE), f32)],
        compiler_params=_params(("arbitrary",)),
        name="inproj",
    )(x, mod3, nw_row, w_main, w_side)


def _split3(x):
    hi = x.astype(bf16)
    r = x - hi.astype(f32)
    mid = r.astype(bf16)
    lo = (r - mid.astype(f32)).astype(bf16)
    return hi, mid, lo


def _select_cols(x, onehot):
    return jnp.dot(jnp.concatenate(_split3(x), axis=1), jnp.concatenate([onehot] * 3, axis=0),
                   preferred_element_type=f32)


def _select_rows(onehot, x):
    return jnp.dot(jnp.concatenate([onehot] * 3, axis=1), jnp.concatenate(_split3(x), axis=0),
                   preferred_element_type=f32)


def _chunk_scan(x, op, identity, reverse):
    n = x.shape[0]
    pos = lax.broadcasted_iota(jnp.int32, x.shape, 0) % CHUNK
    k = 1
    while k < CHUNK:
        if reverse:
            shifted, ok = pltpu.roll(x, n - k, 0), pos < CHUNK - k
        else:
            shifted, ok = pltpu.roll(x, k, 0), pos >= k
        x = op(x, jnp.where(ok, shifted, identity))
        k *= 2
    return x


def _dir_scan(x, op, identity, bwd_col):
    return jnp.where(bwd_col, _chunk_scan(x, op, identity, True), _chunk_scan(x, op, identity, False))


def _head_expand(col0):
    c = lax.broadcasted_iota(jnp.int32, (LANE, MIX_W), 0)
    h = lax.broadcasted_iota(jnp.int32, (LANE, MIX_W), 1) // HEAD_W
    return jnp.where(c == col0 + h, 1.0, 0.0).astype(bf16)


def _chunk_rows(x, seq):
    nrow = max(seq // CHUNK, 8)
    t = lax.broadcasted_iota(jnp.int32, (seq, MIX_W), 0) % CHUNK
    s_ = lax.broadcasted_iota(jnp.int32, (seq, MIX_W), 1) % HEAD_W
    sel = lax.broadcasted_iota(jnp.int32, (nrow, seq), 1) // CHUNK == lax.broadcasted_iota(jnp.int32, (nrow, seq), 0)
    return _select_rows(jnp.where(sel, 1.0, 0.0).astype(bf16), jnp.where(t == s_, x, 0.0))


def _block_diag(x, reps, keep):
    return jnp.concatenate([x] * reps, axis=0) * keep


def _head_mask(rows, cols, rseg, cseg):
    return (lax.broadcasted_iota(jnp.int32, (rows, cols), 0) // rseg) == (lax.broadcasted_iota(jnp.int32, (rows, cols), 1) // cseg)


def _mlstm_body(seq, has_state, *refs):
    if has_state:
        (qkvo_ref, g_ref, gb_ref, nw_ref, c0_ref, n0_ref, m0_ref,
         out_ref, cs_ref, ns_ref, ms_ref, bx_scr, mx_scr, vx_scr, vrow_scr, cbd_scr, hf_scr, hb_scr) = refs
    else:
        (qkvo_ref, g_ref, gb_ref, nw_ref,
         out_ref, cs_ref, ns_ref, ms_ref, bx_scr, mx_scr, vx_scr, vrow_scr, cbd_scr, hf_scr, hb_scr) = refs
    nc = seq // CHUNK
    g = g_ref[...] + gb_ref[...]
    lane = lax.broadcasted_iota(jnp.int32, g.shape, 1)
    bwd_col = (lane % 16) >= 8
    bsum = _dir_scan(jax.nn.log_sigmoid(g), jnp.add, 0.0, bwd_col)
    b_i = pltpu.roll(bsum, LANE - HEADS, 1)
    vcol = g - b_i
    mcol = b_i + _dir_scan(vcol, jnp.maximum, NEG, bwd_col)
    for d in range(2):
        e = _head_expand(G_MLSTM + d * 8)
        bx_scr[d] = _select_cols(b_i, e)
        mx_scr[d] = _select_cols(mcol, e)
        vx = _select_cols(vcol, e)
        vx_scr[d] = vx
        vrow = _chunk_rows(vx, seq)
        for c in range(nc):
            vrow_scr[d, c] = vrow[c:c + 1, :]

    grp = lax.broadcasted_iota(jnp.int32, (1, MIX_W), 1) // HEAD_W
    cbd_scr[...] = jnp.zeros(cbd_scr.shape, f32)
    n_rows, m_rows = [], []
    for d in range(2):
        if has_state:
            for h in range(HEADS):
                cbd_scr[d, h * HEAD_W:(h + 1) * HEAD_W, h * HEAD_W:(h + 1) * HEAD_W] = c0_ref[d, h]
            n_rows.append(jnp.concatenate([n0_ref[d, h:h + 1, :] for h in range(HEADS)], axis=1))
            m_row = jnp.zeros((1, MIX_W), f32)
            for h in range(HEADS):
                m_row = jnp.where(grp == h, m0_ref[d:d + 1, h:h + 1], m_row)
            m_rows.append(m_row)
        else:
            n_rows.append(jnp.zeros((1, MIX_W), f32))
            m_rows.append(jnp.zeros((1, MIX_W), f32))

    li = lax.broadcasted_iota(jnp.int32, (CHUNK, MIX_W), 0)
    si = lax.broadcasted_iota(jnp.int32, (CHUNK, MIX_W), 1) % HEAD_W
    valid = (si <= li, si >= li)
    bd = _head_mask(MIX_W, MIX_W, HEAD_W, HEAD_W)
    bd_f32 = jnp.where(bd, 1.0, 0.0)
    ones_bd = bd_f32.astype(bf16)

    def step(c, carry):
        new = []
        for d in range(2):
            n_row, m_row = carry[2 * d], carry[2 * d + 1]
            cidx = c if d == 0 else nc - 1 - c
            r0 = pl.multiple_of(cidx * CHUNK, CHUNK)
            last = CHUNK - 1 if d == 0 else 0
            q = qkvo_ref[pl.ds(r0, CHUNK), 0:MIX_W].astype(f32)
            k = qkvo_ref[pl.ds(r0, CHUNK), MIX_W:2 * MIX_W].astype(f32) * (HEAD_W ** -0.5)
            vb = qkvo_ref[pl.ds(r0, CHUNK), 2 * MIX_W:3 * MIX_W].astype(bf16)
            qb = q.astype(bf16)
            bx = bx_scr[d, pl.ds(r0, CHUNK), :]
            mx = mx_scr[d, pl.ds(r0, CHUNK), :]
            vx = vx_scr[d, pl.ds(r0, CHUNK), :]
            p = jnp.exp(jnp.where(valid[d], bx + vrow_scr[d, cidx] - mx, NEG))
            smat = _mm_nt(qb, _block_diag(k.astype(bf16), HEADS, ones_bd)) * p
            intra = _mm(smat, _block_diag(vb, HEADS, ones_bd))
            dsum = _mm(smat, ones_bd)
            gg = bx + m_row
            mt = jnp.maximum(gg, mx)
            a = jnp.exp(mx - mt)
            w_prev = jnp.exp(gg - mt)
            inter = _mm(qb, cbd_scr[d])
            dint = _mm(q * n_row, ones_bd)
            num = intra * a + inter * w_prev
            den = dsum * a + dint * w_prev
            h_scr = hf_scr if d == 0 else hb_scr
            h_scr[pl.ds(r0, CHUNK), :] = num / jnp.maximum(jnp.abs(den), jnp.exp(-mt))
            bl = bx[last:last + 1, :]
            mloc = mx[last:last + 1, :]
            m_new = jnp.maximum(bl + m_row, mloc)
            dec = jnp.exp(bl + m_row - m_new)
            fac = jnp.exp(mloc - m_new)
            kw = k * jnp.exp(bl + vx - mloc)
            kv = _mm_tn(kw, vb) * bd_f32
            cbd_scr[d] = cbd_scr[d] * dec + kv * fac
            new += [n_row * dec + jnp.sum(kw, axis=0, keepdims=True) * fac, m_new]
        return tuple(new)

    fin_state = lax.fori_loop(0, nc, step, (n_rows[0], m_rows[0], n_rows[1], m_rows[1]),
                              unroll=min(SCAN_UNROLL, nc))
    for d in range(2):
        n_row, m_row = fin_state[2 * d], fin_state[2 * d + 1]
        for h in range(HEADS):
            hl = slice(h * HEAD_W, (h + 1) * HEAD_W)
            cs_ref[d, h] = cbd_scr[d, hl, hl]
            ns_ref[d, h:h + 1, :] = n_row[:, hl]
            ms_ref[d:d + 1, h:h + 1] = m_row[:, h * HEAD_W:h * HEAD_W + 1]

    blk = 256

    def fin(i, carry):
        r0 = pl.multiple_of(i * blk, blk)
        hs = hf_scr[pl.ds(r0, blk), :] + hb_scr[pl.ds(r0, blk), :]
        hn = _seg_rms(hs, HEAD_W, nw_ref[...])
        o = qkvo_ref[pl.ds(r0, blk), 3 * MIX_W:4 * MIX_W].astype(f32)
        out_ref[pl.ds(r0, blk), :] = (hn * jax.nn.sigmoid(o)).astype(out_ref.dtype)
        return carry

    lax.fori_loop(0, seq // blk, fin, 0)


def _mlstm(u, g, nb, seq, gb_row, nw_row, state):
    has_state = state is not None
    nc = seq // CHUNK
    in_specs = [pl.BlockSpec((seq, 4 * MIX_W), lambda b: (b, COL_MLSTM // (4 * MIX_W))),
                pl.BlockSpec((seq, LANE), lambda b: (b, 0)),
                pl.BlockSpec((1, LANE), lambda b: (0, 0)),
                pl.BlockSpec((1, MIX_W), lambda b: (0, 0))]
    args = [u, g, gb_row, nw_row]
    if has_state:
        in_specs += [pl.BlockSpec((None, 2, HEADS, HEAD_W, HEAD_W), lambda b: (b, 0, 0, 0, 0)),
                     pl.BlockSpec((None, 2, HEADS, HEAD_W), lambda b: (b, 0, 0, 0)),
                     pl.BlockSpec((None, 2, HEADS), lambda b: (b, 0, 0))]
        args += list(state)
    return pl.pallas_call(
        functools.partial(_mlstm_body, seq, has_state),
        grid=(nb,),
        in_specs=in_specs,
        out_specs=[pl.BlockSpec((seq, MIX_W), lambda b: (b, 0)),
                   pl.BlockSpec((None, 2, HEADS, HEAD_W, HEAD_W), lambda b: (b, 0, 0, 0, 0)),
                   pl.BlockSpec((None, 2, HEADS, HEAD_W), lambda b: (b, 0, 0, 0)),
                   pl.BlockSpec((None, 2, HEADS), lambda b: (b, 0, 0))],
        out_shape=[jax.ShapeDtypeStruct((nb * seq, MIX_W), bf16),
                   jax.ShapeDtypeStruct((nb, 2, HEADS, HEAD_W, HEAD_W), f32),
                   jax.ShapeDtypeStruct((nb, 2, HEADS, HEAD_W), f32),
                   jax.ShapeDtypeStruct((nb, 2, HEADS), f32)],
        scratch_shapes=[pltpu.VMEM((2, seq, MIX_W), f32), pltpu.VMEM((2, seq, MIX_W), f32),
                        pltpu.VMEM((2, seq, MIX_W), f32), pltpu.VMEM((2, nc, 1, MIX_W), f32),
                        pltpu.VMEM((2, MIX_W, MIX_W), f32),
                        pltpu.VMEM((seq, MIX_W), f32), pltpu.VMEM((seq, MIX_W), f32)],
        compiler_params=_params(("arbitrary",)),
        name="mlstm_state" if has_state else "mlstm",
    )(*args)


def _ssd_body(seq, has_state, *refs):
    if has_state:
        (z_ref, xbc_ref, g_ref, cw_ref, cb_ref, dtb_ref, alog_ref, dskip_ref, nw_ref, h0_ref,
         out_ref, hs_ref, xpad_scr, xc_scr, ax_scr, dx_scr, arow_scr, ht_scr, yf_scr, yb_scr) = refs
    else:
        (z_ref, xbc_ref, g_ref, cw_ref, cb_ref, dtb_ref, alog_ref, dskip_ref, nw_ref,
         out_ref, hs_ref, xpad_scr, xc_scr, ax_scr, dx_scr, arow_scr, ht_scr, yf_scr, yb_scr) = refs
    nc = seq // CHUNK
    pad = 8
    blk = 256
    nblk = seq // blk
    xpad_scr[0:pad, :] = jnp.zeros((pad, SSM_CONV_CH), f32)
    xpad_scr[pad + seq:2 * pad + seq, :] = jnp.zeros((pad, SSM_CONV_CH), f32)
    xpad_scr[pad:pad + seq, :] = xbc_ref[...].astype(f32)

    cblk = 128
    for i in range(seq // cblk):
        r0 = i * cblk
        acc = jnp.zeros((cblk, SSM_CONV_CH), f32) + cb_ref[...]
        for kk in range(SSM_CONV_K):
            off = r0 + pad - SSM_CONV_K // 2 + kk
            acc = acc + xpad_scr[off:off + cblk, :] * cw_ref[kk:kk + 1, :]
        xc_scr[r0:r0 + cblk, :] = jax.nn.silu(acc)

    dt = jax.nn.softplus(g_ref[...] + dtb_ref[...])
    lane = lax.broadcasted_iota(jnp.int32, dt.shape, 1)
    acum = _dir_scan(dt * (-jnp.exp(alog_ref[...])), jnp.add, 0.0, (lane % 8) >= HEADS)
    for d in range(2):
        e = _head_expand(G_DT + d * HEADS)
        ax = _select_cols(acum, e)
        ax_scr[d] = ax
        dx_scr[d] = _select_cols(dt, e)
        arow = _chunk_rows(ax, seq)
        for c in range(nc):
            arow_scr[d, c] = arow[c:c + 1, :]

    gn = SSM_GROUPS * SSM_STATE
    rep = HEADS // SSM_GROUPS
    ht_scr[...] = jnp.zeros(ht_scr.shape, f32)
    if has_state:
        for d in range(2):
            for h in range(HEADS):
                g0 = (h // rep) * SSM_STATE
                ht_scr[d, g0:g0 + SSM_STATE, h * HEAD_W:(h + 1) * HEAD_W] = h0_ref[d, h].T
    li = lax.broadcasted_iota(jnp.int32, (CHUNK, MIX_W), 0)
    si = lax.broadcasted_iota(jnp.int32, (CHUNK, MIX_W), 1) % HEAD_W
    valid = (si <= li, si >= li)
    state_keep = jnp.where(_head_mask(gn, MIX_W, SSM_STATE, rep * HEAD_W), 1.0, 0.0)
    key_keep = jnp.where(_head_mask(HEADS * CHUNK, gn, rep * CHUNK, SSM_STATE), 1.0, 0.0).astype(bf16)
    head_keep = jnp.where(_head_mask(MIX_W, MIX_W, HEAD_W, HEAD_W), 1.0, 0.0).astype(bf16)

    def step(c, carry):
        for d in range(2):
            cidx = c if d == 0 else nc - 1 - c
            r0 = pl.multiple_of(cidx * CHUNK, CHUNK)
            last = CHUNK - 1 if d == 0 else 0
            xs = xc_scr[pl.ds(r0, CHUNK), 0:MIX_W]
            bm = xc_scr[pl.ds(r0, CHUNK), MIX_W:MIX_W + gn].astype(bf16)
            cm = xc_scr[pl.ds(r0, CHUNK), MIX_W + gn:MIX_W + 2 * gn].astype(bf16)
            ax = ax_scr[d, pl.ds(r0, CHUNK), :]
            decay = jnp.exp(jnp.where(valid[d], ax - arow_scr[d, cidx], NEG))
            scores = _mm_nt(cm, _block_diag(bm, HEADS, key_keep)) * decay
            xdt = xs * dx_scr[d, pl.ds(r0, CHUNK), :]
            y = _mm(scores, _block_diag(xdt.astype(bf16), HEADS, head_keep)) + _mm(cm, ht_scr[d]) * jnp.exp(ax)
            y_scr = yf_scr if d == 0 else yb_scr
            y_scr[pl.ds(r0, CHUNK), :] = y
            al = ax[last:last + 1, :]
            ht_scr[d] = ht_scr[d] * jnp.exp(al) + _mm_tn(bm, xdt * jnp.exp(al - ax)) * state_keep
        return carry

    lax.fori_loop(0, nc, step, 0, unroll=min(SCAN_UNROLL, nc))
    for d in range(2):
        for h in range(HEADS):
            g0 = (h // rep) * SSM_STATE
            hs_ref[d, h] = ht_scr[d, g0:g0 + SSM_STATE, h * HEAD_W:(h + 1) * HEAD_W].T

    def fin(i, carry):
        r0 = pl.multiple_of(i * blk, blk)
        y = yf_scr[pl.ds(r0, blk), :] + yb_scr[pl.ds(r0, blk), :] + dskip_ref[...] * xc_scr[pl.ds(r0, blk), 0:MIX_W]
        y = y * jax.nn.silu(z_ref[pl.ds(r0, blk), :].astype(f32))
        y = y * lax.rsqrt(jnp.mean(y * y, axis=-1, keepdims=True) + EPS) * nw_ref[...]
        out_ref[pl.ds(r0, blk), :] = y.astype(out_ref.dtype)
        return carry

    lax.fori_loop(0, nblk, fin, 0)


def _ssd(u, g, nb, seq, cw, cb_row, dtb_row, alog_row, dskip_row, nw_row, state):
    has_state = state is not None
    nc = seq // CHUNK
    const = lambda shape: pl.BlockSpec(shape, lambda b: (0,) * len(shape))
    in_specs = [pl.BlockSpec((seq, MIX_W), lambda b: (b, COL_Z // MIX_W)),
                pl.BlockSpec((seq, SSM_CONV_CH), lambda b: (b, COL_XBC // SSM_CONV_CH)),
                pl.BlockSpec((seq, LANE), lambda b: (b, 0)),
                const((SSM_CONV_K, SSM_CONV_CH)), const((1, SSM_CONV_CH)), const((1, LANE)), const((1, LANE)),
                const((1, MIX_W)), const((1, MIX_W))]
    args = [u, u, g, cw, cb_row, dtb_row, alog_row, dskip_row, nw_row]
    if has_state:
        in_specs.append(pl.BlockSpec((None, 2, HEADS, HEAD_W, SSM_STATE), lambda b: (b, 0, 0, 0, 0)))
        args.append(state)
    return pl.pallas_call(
        functools.partial(_ssd_body, seq, has_state),
        grid=(nb,),
        in_specs=in_specs,
        out_specs=[pl.BlockSpec((seq, MIX_W), lambda b: (b, 0)),
                   pl.BlockSpec((None, 2, HEADS, HEAD_W, SSM_STATE), lambda b: (b, 0, 0, 0, 0))],
        out_shape=[jax.ShapeDtypeStruct((nb * seq, MIX_W), bf16),
                   jax.ShapeDtypeStruct((nb, 2, HEADS, HEAD_W, SSM_STATE), f32)],
        scratch_shapes=[pltpu.VMEM((seq + 16, SSM_CONV_CH), f32), pltpu.VMEM((seq, SSM_CONV_CH), f32),
                        pltpu.VMEM((2, seq, MIX_W), f32), pltpu.VMEM((2, seq, MIX_W), f32),
                        pltpu.VMEM((2, nc, 1, MIX_W), f32),
                        pltpu.VMEM((2, SSM_GROUPS * SSM_STATE, MIX_W), f32),
                        pltpu.VMEM((seq, MIX_W), f32), pltpu.VMEM((seq, MIX_W), f32)],
        compiler_params=_params(("arbitrary",)),
        name="ssd_state" if has_state else "ssd",
    )(*args)


def _with_ones(v):
    vb = v.astype(bf16)
    ones = jnp.ones((v.shape[0], HEAD_W), bf16)
    return jnp.concatenate([x for h in range(HEADS) for x in (vb[:, h * HEAD_W:(h + 1) * HEAD_W], ones)], axis=1)


def _softmax_av(scores, vexts):
    m = scores[0].max(axis=1, keepdims=True)
    for s in scores[1:]:
        m = jnp.maximum(m, s.max(axis=1, keepdims=True))
    r = None
    for s, vx in zip(scores, vexts):
        t = jnp.dot(jnp.exp(s - m).astype(bf16), vx, preferred_element_type=f32)
        r = t if r is None else r + t
    return r[:, :HEAD_W] / r[:, HEAD_W:]


def _na_prompt_body(seq, q_ref, k_ref, v_ref, qn_ref, kn_ref, o_ref, ko_ref, vo_ref):
    scale = HEAD_W ** -0.5
    qb = (_seg_rms(q_ref[...].astype(f32), HEAD_W, qn_ref[...]) * scale).astype(bf16)
    kn = _seg_rms(k_ref[...].astype(f32), HEAD_W, kn_ref[...])
    v = v_ref[...].astype(f32)
    ko_ref[...] = kn
    vo_ref[...] = v
    for s0 in range(0, q_ref.shape[0], seq):
        rs = slice(s0, s0 + seq)
        kt = kn[rs].T.astype(bf16)
        vx = _with_ones(v[rs])
        for h in range(HEADS):
            hl = slice(h * HEAD_W, (h + 1) * HEAD_W)
            s = jnp.dot(qb[rs, hl], kt[hl, :], preferred_element_type=f32)
            o_ref[rs, hl] = _softmax_av([s], [vx[:, 2 * h * HEAD_W:2 * (h + 1) * HEAD_W]]).astype(o_ref.dtype)


def _na_prompt(u, nb, seq, qn_row, kn_row):
    cb = COL_NA // MIX_W
    rows = PROMPT_BPS * seq
    blk = lambda j: pl.BlockSpec((rows, MIX_W), lambda b: (b, j))
    const = pl.BlockSpec((1, MIX_W), lambda b: (0, 0))
    sds = jax.ShapeDtypeStruct((nb * seq, MIX_W), f32)
    return pl.pallas_call(
        functools.partial(_na_prompt_body, seq),
        grid=(nb // PROMPT_BPS,),
        in_specs=[blk(cb), blk(cb + 1), blk(cb + 2), const, const],
        out_specs=[blk(0), blk(0), blk(0)],
        out_shape=[jax.ShapeDtypeStruct(sds.shape, bf16), sds, sds],
        compiler_params=_params(("arbitrary",)),
        name="na_prompt",
    )(u, u, u, qn_row, kn_row)


def _na_sample_body(qblk, q_ref, k_ref, v_ref, kc_ref, vc_ref, rpb_ref, qn_ref, kn_ref, o_ref,
                    kt_scr, vx_scr, bias_scr):
    b = pl.program_id(0)
    i = pl.program_id(1)
    seq = k_ref.shape[0]
    rows = seq // GRID_W

    @pl.when((b == 0) & (i == 0))
    def _():
        j = lax.broadcasted_iota(jnp.int32, (GRID_W, LANE), 0)
        lane = lax.broadcasted_iota(jnp.int32, (GRID_W, LANE), 1)
        j2 = lane % GRID_W
        cs = jnp.clip(j - NA_WIN_C // 2, 0, GRID_W - NA_WIN_C)
        ok = (j2 >= cs) & (j2 < cs + NA_WIN_C)
        left = lane < GRID_W
        neg = jnp.full((GRID_W, LANE), NEG, f32)
        for h in range(HEADS):
            lo, hi = [], []
            for dr in range(2 * NA_WIN_R - 1):
                row = jnp.broadcast_to(rpb_ref[h, dr:dr + 1, :], (GRID_W, LANE))
                lo.append(jnp.where(ok, pltpu.roll(row, LANE - (NA_WIN_C - 1), 1, stride=1, stride_axis=0), NEG))
                hi.append(jnp.where(ok, pltpu.roll(row, GRID_W - (NA_WIN_C - 1), 1, stride=1, stride_axis=0), NEG))
            for qi in range(rows):
                rs = min(max(qi - NA_WIN_R // 2, 0), rows - NA_WIN_R)
                for pair in range(rows // 2):
                    k_even, k_odd = 2 * pair, 2 * pair + 1
                    a = lo[k_even - qi + NA_WIN_R - 1] if rs <= k_even < rs + NA_WIN_R else neg
                    c = hi[k_odd - qi + NA_WIN_R - 1] if rs <= k_odd < rs + NA_WIN_R else neg
                    bias_scr[h, qi * GRID_W:(qi + 1) * GRID_W, pair * LANE:(pair + 1) * LANE] = jnp.where(left, a, c)

    @pl.when(i == 0)
    def _():
        kn = _seg_rms(k_ref[...].astype(f32), HEAD_W, kn_ref[...])
        kt_scr[:, 0:seq] = kn.T.astype(bf16)
        kt_scr[:, seq:] = kc_ref[...].T.astype(bf16)
        vx_scr[0:seq, :] = _with_ones(v_ref[...])
        vx_scr[seq:, :] = _with_ones(vc_ref[...])

    r0 = pl.multiple_of(i * qblk, qblk)
    qb = (_seg_rms(q_ref[...].astype(f32), HEAD_W, qn_ref[...]) * (HEAD_W ** -0.5)).astype(bf16)
    for h in range(HEADS):
        hl = slice(h * HEAD_W, (h + 1) * HEAD_W)
        xl = slice(2 * h * HEAD_W, 2 * (h + 1) * HEAD_W)
        s_own = jnp.dot(qb[:, hl], kt_scr[hl, 0:seq], preferred_element_type=f32) + bias_scr[h, pl.ds(r0, qblk), :]
        s_ctx = jnp.dot(qb[:, hl], kt_scr[hl, seq:], preferred_element_type=f32)
        o = _softmax_av([s_own, s_ctx], [vx_scr[0:seq, xl], vx_scr[seq:, xl]])
        o_ref[:, hl] = o.astype(o_ref.dtype)


def _na_sample(u, nb, layer, cache_k, cache_v, rpb_pad, qn_row, kn_row):
    seq = DEC_SEQ
    qblk = 256
    nq = seq // qblk
    cb = COL_NA // MIX_W
    full = lambda j: pl.BlockSpec((seq, MIX_W), lambda b, i: (b, j))
    ctx = pl.BlockSpec((None, None, PAST_LEN, MIX_W), lambda b, i: (b, layer, 0, 0))
    const = pl.BlockSpec((1, MIX_W), lambda b, i: (0, 0))
    return pl.pallas_call(
        functools.partial(_na_sample_body, qblk),
        grid=(nb, nq),
        in_specs=[pl.BlockSpec((qblk, MIX_W), lambda b, i: (b * nq + i, cb)), full(cb + 1), full(cb + 2),
                  ctx, ctx,
                  pl.BlockSpec((HEADS, 2 * NA_WIN_R, LANE), lambda b, i: (0, 0, 0)),
                  const, const],
        out_specs=pl.BlockSpec((qblk, MIX_W), lambda b, i: (b * nq + i, 0)),
        out_shape=jax.ShapeDtypeStruct((nb * seq, MIX_W), bf16),
        scratch_shapes=[pltpu.VMEM((MIX_W, seq + PAST_LEN), bf16), pltpu.VMEM((seq + PAST_LEN, 2 * MIX_W), bf16),
                        pltpu.VMEM((HEADS, seq, seq), f32)],
        compiler_params=_params(("arbitrary", "arbitrary")),
        name="na_sample",
    )(u, u, u, cache_k, cache_v, rpb_pad, qn_row, kn_row)


def _na_rpb_pad(rpb):
    c = rpb.shape[2]
    return jnp.pad(rpb.astype(f32), ((0, 0), (0, 1), (0, LANE - c)))


def _diff_lambda(lam_ref, lam_init):
    lv = lam_ref[...]
    s1 = jnp.sum(lv[0:1, :] * lv[1:2, :], axis=1, keepdims=True)
    s2 = jnp.sum(lv[2:3, :] * lv[3:4, :], axis=1, keepdims=True)
    return jnp.exp(s1) - jnp.exp(s2) + lam_init


def _diff_heads(q, kt, vext, lam):
    outs = []
    for h in range(HEADS):
        o = []
        for m in range(2):
            c0 = h * HEAD_W + m * DIFF_QK_DIM
            s = jnp.dot(q[:, c0:c0 + DIFF_QK_DIM], kt[c0:c0 + DIFF_QK_DIM, :], preferred_element_type=f32)
            o.append(_softmax_av([s], [vext[:, 2 * h * HEAD_W:2 * (h + 1) * HEAD_W]]))
        outs.append(o[0] - lam * o[1])
    return jnp.concatenate(outs, axis=1)


def _diff_prompt_body(lam_init, seq, q_ref, k_ref, v_ref, qn_ref, kn_ref, lam_ref, nw_ref, o_ref, ko_ref, vo_ref):
    lam = _diff_lambda(lam_ref, lam_init)
    qb = (_seg_rms(q_ref[...].astype(f32), DIFF_QK_DIM, qn_ref[...]) * (DIFF_QK_DIM ** -0.5)).astype(bf16)
    kn = _seg_rms(k_ref[...].astype(f32), DIFF_QK_DIM, kn_ref[...])
    v = v_ref[...].astype(f32)
    ko_ref[...] = kn
    vo_ref[...] = v
    for s0 in range(0, q_ref.shape[0], seq):
        rs = slice(s0, s0 + seq)
        o = _diff_heads(qb[rs], kn[rs].T.astype(bf16), _with_ones(v[rs]), lam)
        o_ref[rs, :] = (_seg_rms(o, HEAD_W, nw_ref[...]) * (1.0 - lam_init)).astype(o_ref.dtype)


def _diff_prompt(u, nb, seq, lam_init, qn_row, kn_row, lam_vecs, nw_row):
    cb = COL_DIFF // MIX_W
    rows = PROMPT_BPS * seq
    blk = lambda j: pl.BlockSpec((rows, MIX_W), lambda b: (b, j))
    const = pl.BlockSpec((1, MIX_W), lambda b: (0, 0))
    sds = jax.ShapeDtypeStruct((nb * seq, MIX_W), f32)
    return pl.pallas_call(
        functools.partial(_diff_prompt_body, lam_init, seq),
        grid=(nb // PROMPT_BPS,),
        in_specs=[blk(cb), blk(cb + 1), blk(cb + 2), const, const,
                  pl.BlockSpec((4, DIFF_QK_DIM), lambda b: (0, 0)), const],
        out_specs=[blk(0), blk(0), blk(0)],
        out_shape=[jax.ShapeDtypeStruct(sds.shape, bf16), sds, sds],
        compiler_params=_params(("arbitrary",)),
        name="diff_prompt",
    )(u, u, u, qn_row, kn_row, lam_vecs, nw_row)


def _rope(x, cos, sin):
    w = x.shape[-1]
    up = pltpu.roll(x, w - 8, 1)
    dn = pltpu.roll(x, 8, 1)
    first = (lax.broadcasted_iota(jnp.int32, x.shape, 1) % 16) < 8
    return x * cos + jnp.where(first, up, dn) * sin


def _diff_sample_body(lam_init, qblk, q_ref, k_ref, v_ref, kc_ref, vc_ref, cos_ref, sin_ref, qn_ref, kn_ref,
                      lam_ref, nw_ref, o_ref, kt_scr, vx_scr):
    i = pl.program_id(1)
    seq = k_ref.shape[0]

    @pl.when(i == 0)
    def _():
        kn = _seg_rms(k_ref[...].astype(f32), DIFF_QK_DIM, kn_ref[...])
        kt_scr[:, 0:seq] = _rope(kn, cos_ref[...], sin_ref[...]).T.astype(bf16)
        kt_scr[:, seq:] = kc_ref[...].T.astype(bf16)
        vx_scr[0:seq, :] = _with_ones(v_ref[...])
        vx_scr[seq:, :] = _with_ones(vc_ref[...])

    lam = _diff_lambda(lam_ref, lam_init)
    r0 = pl.multiple_of(i * qblk, qblk)
    qn = _seg_rms(q_ref[...].astype(f32), DIFF_QK_DIM, qn_ref[...])
    qn = _rope(qn, cos_ref[pl.ds(r0, qblk), :], sin_ref[pl.ds(r0, qblk), :]) * (DIFF_QK_DIM ** -0.5)
    o = _diff_heads(qn.astype(bf16), kt_scr[...], vx_scr[...], lam)
    o_ref[...] = (_seg_rms(o, HEAD_W, nw_ref[...]) * (1.0 - lam_init)).astype(o_ref.dtype)


def _diff_sample(u, nb, layer, lam_init, cache_k, cache_v, cos, sin, qn_row, kn_row, lam_vecs, nw_row):
    seq = DEC_SEQ
    qblk = 512
    nq = seq // qblk
    cb = COL_DIFF // MIX_W
    full = lambda j: pl.BlockSpec((seq, MIX_W), lambda b, i: (b, j))
    ctx = pl.BlockSpec((None, None, PAST_LEN, MIX_W), lambda b, i: (b, layer, 0, 0))
    const = pl.BlockSpec((1, MIX_W), lambda b, i: (0, 0))
    tab = pl.BlockSpec((seq, MIX_W), lambda b, i: (0, 0))
    return pl.pallas_call(
        functools.partial(_diff_sample_body, lam_init, qblk),
        grid=(nb, nq),
        in_specs=[pl.BlockSpec((qblk, MIX_W), lambda b, i: (b * nq + i, cb)), full(cb + 1), full(cb + 2),
                  ctx, ctx, tab, tab, const, const,
                  pl.BlockSpec((4, DIFF_QK_DIM), lambda b, i: (0, 0)), const],
        out_specs=pl.BlockSpec((qblk, MIX_W), lambda b, i: (b * nq + i, 0)),
        out_shape=jax.ShapeDtypeStruct((nb * seq, MIX_W), bf16),
        scratch_shapes=[pltpu.VMEM((MIX_W, seq + PAST_LEN), bf16), pltpu.VMEM((seq + PAST_LEN, 2 * MIX_W), bf16)],
        compiler_params=_params(("arbitrary", "arbitrary")),
        name="diff_sample",
    )(u, u, u, cache_k, cache_v, cos, sin, qn_row, kn_row, lam_vecs, nw_row)


def _rope_tables():
    t = jnp.arange(DEC_SEQ)
    nf = DIFF_QK_DIM // 4
    inv = ROPE_BASE ** (-jnp.arange(nf, dtype=f32) / nf)
    ar = (t // GRID_W).astype(f32)[:, None] * inv
    ac = (t % GRID_W).astype(f32)[:, None] * inv
    cos = jnp.concatenate([jnp.cos(ar), jnp.cos(ar), jnp.cos(ac), jnp.cos(ac)], axis=1)
    sin = jnp.concatenate([-jnp.sin(ar), jnp.sin(ar), -jnp.sin(ac), jnp.sin(ac)], axis=1)
    reps = MIX_W // DIFF_QK_DIM
    return jnp.tile(cos, (1, reps)), jnp.tile(sin, (1, reps))


def _merge_body(x_ref, a_ref, b_ref, c_ref, d_ref, gp_ref, mod_ref, wb_ref, wo_ref, nw_ref, wr_ref,
                xo_ref, h2_ref, aff_ref):
    mixed = None
    for n, br in enumerate((a_ref, b_ref, c_ref, d_ref)):
        gate = _sigmoid(gp_ref[:, n * D_MODEL:(n + 1) * D_MODEL].astype(f32))
        t = gate * jnp.dot(br[...].astype(bf16), wb_ref[n], preferred_element_type=f32)
        mixed = t if mixed is None else mixed + t
    y = jnp.dot(mixed.astype(bf16), wo_ref[...], preferred_element_type=f32)
    x = x_ref[...] + mod_ref[:, 2 * D_MODEL:3 * D_MODEL] * y
    xo_ref[...] = x
    h = x * lax.rsqrt(jnp.mean(x * x, axis=-1, keepdims=True) + EPS) * nw_ref[...]
    h = h * (1.0 + mod_ref[:, 4 * D_MODEL:5 * D_MODEL]) + mod_ref[:, 3 * D_MODEL:4 * D_MODEL]
    hb = h.astype(bf16)
    h2_ref[...] = hb
    logits = jnp.dot(hb, wr_ref[...], preferred_element_type=f32)
    lane = lax.broadcasted_iota(jnp.int32, logits.shape, 1)
    logits = jnp.where(lane < N_EXPERTS, logits, NEG)
    e = jnp.exp(logits - logits.max(axis=1, keepdims=True))
    aff_ref[...] = e / e.sum(axis=1, keepdims=True)


def _merge(x, brs, u, mod3, mod_base, mod_stride, seq, wb, wo, nw_row, wr):
    n = x.shape[0]
    tm = 512
    tps = max(seq // tm, 1)
    row = lambda w: pl.BlockSpec((tm, w), lambda i: (i, 0))
    return pl.pallas_call(
        _merge_body,
        grid=(n // tm,),
        in_specs=[row(D_MODEL), row(MIX_W), row(MIX_W), row(MIX_W), row(MIX_W),
                  pl.BlockSpec((tm, N_BRANCH * D_MODEL), lambda i: (i, COL_GPRE // (N_BRANCH * D_MODEL))),
                  pl.BlockSpec((None, 1, 6 * D_MODEL), lambda i: (mod_base + mod_stride * (i // tps), 0, 0)),
                  pl.BlockSpec((N_BRANCH, MIX_W, D_MODEL), lambda i: (0, 0, 0)),
                  pl.BlockSpec((D_MODEL, D_MODEL), lambda i: (0, 0)),
                  pl.BlockSpec((1, D_MODEL), lambda i: (0, 0)),
                  pl.BlockSpec((D_MODEL, LANE), lambda i: (0, 0))],
        out_specs=[row(D_MODEL), row(D_MODEL), row(LANE)],
        out_shape=[jax.ShapeDtypeStruct((n, D_MODEL), f32), jax.ShapeDtypeStruct((n, D_MODEL), bf16),
                   jax.ShapeDtypeStruct((n, LANE), f32)],
        compiler_params=_params(("arbitrary",)),
        name="merge",
    )(x, *brs, u, mod3, wb, wo, nw_row, wr)


def _rows_prefix_sum(x):
    n = x.shape[0]
    pos = lax.broadcasted_iota(jnp.int32, x.shape, 0)
    k = 1
    while k < n:
        x = x + jnp.where(pos >= k, pltpu.roll(x, k, 0), 0.0)
        k *= 2
    return x


def _route_body(t, cap, bpg, aff_ref, h2_ref, xg_ref, g_ref, rank_ref, rrow_scr, arow_scr, oh_scr):
    bb = pl.program_id(1)

    @pl.when(bb == 0)
    def _():
        aff = aff_ref[...]
        tok = lax.broadcasted_iota(jnp.int32, aff.shape, 0)
        val = aff
        k = 2
        while k <= t:
            j = k // 2
            while j >= 1:
                lower = (tok & j) == 0
                pv = jnp.where(lower, pltpu.roll(val, t - j, 0), pltpu.roll(val, j, 0))
                val = jnp.where(lower == ((tok & k) == 0), jnp.maximum(val, pv), jnp.minimum(val, pv))
                j //= 2
            k *= 2
        thr = val[cap - 1:cap, :]
        above = aff > thr
        tie = jnp.where(aff == thr, 1.0, 0.0)
        need = cap - jnp.sum(jnp.where(above, 1.0, 0.0), axis=0, keepdims=True)
        sel = jnp.where(above | ((tie > 0.0) & (_rows_prefix_sum(tie) - tie < need)), 1.0, 0.0)
        rank = jnp.where(sel > 0.0, _rows_prefix_sum(sel) - sel, float(cap))
        rank_ref[...] = rank
        rank_t = rank.T
        aff_t = aff.T
        for k in range(bpg):
            rrow_scr[k] = rank_t[k * N_EXPERTS:(k + 1) * N_EXPERTS, :]
            arow_scr[k] = aff_t[k * N_EXPERTS:(k + 1) * N_EXPERTS, :]

    rk = rrow_scr[bb]
    ar = arow_scr[bb]
    slot = lax.broadcasted_iota(jnp.int32, (cap, t), 0).astype(f32)
    for e in range(N_EXPERTS):
        onehot = jnp.where(rk[e:e + 1, :] == slot, 1.0, 0.0)
        oh_scr[e * cap:(e + 1) * cap, :] = onehot.astype(bf16)
        g_ref[e] = jnp.sum(onehot * ar[e:e + 1, :], axis=1, keepdims=True)
    h2 = h2_ref[...]
    epb = max(1, 512 // cap)
    for e0 in range(0, N_EXPERTS, epb):
        rows = jnp.dot(oh_scr[e0 * cap:(e0 + epb) * cap, :], h2, preferred_element_type=f32)
        xg_ref[e0:e0 + epb] = rows.astype(bf16).reshape(epb, cap, D_MODEL)


def _group_lanes(aff, nb, t, bpg):
    a = aff[:, :N_EXPERTS].reshape(nb // bpg, bpg, t, N_EXPERTS)
    a = jnp.transpose(a, (0, 2, 1, 3)).reshape(nb // bpg * t, bpg * N_EXPERTS)
    return jnp.pad(a, ((0, 0), (0, LANE - bpg * N_EXPERTS)))


def _route(aff, h2, nb, t, bpg):
    cap = CAPACITY_FACTOR * t // N_EXPERTS
    return pl.pallas_call(
        functools.partial(_route_body, t, cap, bpg),
        grid=(nb // bpg, bpg),
        in_specs=[pl.BlockSpec((t, LANE), lambda g, k: (g, 0)),
                  pl.BlockSpec((t, D_MODEL), lambda g, k: (g * bpg + k, 0))],
        out_specs=[pl.BlockSpec((N_EXPERTS, cap, D_MODEL), lambda g, k: (0, g * bpg + k, 0)),
                   pl.BlockSpec((N_EXPERTS, cap, 1), lambda g, k: (0, g * bpg + k, 0)),
                   pl.BlockSpec((t, LANE), lambda g, k: (g, 0))],
        out_shape=[jax.ShapeDtypeStruct((N_EXPERTS, nb * cap, D_MODEL), bf16),
                   jax.ShapeDtypeStruct((N_EXPERTS, nb * cap, 1), f32),
                   jax.ShapeDtypeStruct((nb // bpg * t, LANE), f32)],
        scratch_shapes=[pltpu.VMEM((bpg, N_EXPERTS, t), f32), pltpu.VMEM((bpg, N_EXPERTS, t), f32),
                        pltpu.VMEM((N_EXPERTS * cap, t), bf16)],
        compiler_params=_params(("arbitrary", "arbitrary")),
        name=f"route_{t}",
    )(_group_lanes(aff, nb, t, bpg), h2)


def _ffn_body(xp_ref, xs_ref, gp_ref, gs_ref, wg_ref, wu_ref, wd_ref, yp_ref, ys_ref, wg_scr, wu_scr, wd_scr):
    wg_scr[...] = wg_ref[...].astype(bf16)
    wu_scr[...] = wu_ref[...].astype(bf16)
    wd_scr[...] = wd_ref[...].astype(bf16)
    blk = 512
    for x_ref, g_ref, y_ref in ((xp_ref, gp_ref, yp_ref), (xs_ref, gs_ref, ys_ref)):
        def step(i, carry, x_ref=x_ref, g_ref=g_ref, y_ref=y_ref):
            r0 = pl.multiple_of(i * blk, blk)
            x = x_ref[pl.ds(r0, blk), :]
            a = jnp.dot(x, wg_scr[...], preferred_element_type=f32)
            up = jnp.dot(x, wu_scr[...], preferred_element_type=f32)
            mid = (jax.nn.silu(a) * up).astype(bf16)
            y = jnp.dot(mid, wd_scr[...], preferred_element_type=f32)
            y_ref[pl.ds(r0, blk), :] = (y * g_ref[pl.ds(r0, blk), :]).astype(bf16)
            return carry
        lax.fori_loop(0, x_ref.shape[0] // blk, step, 0)


def _ffn(xp, xs, gp, gs, w_gate, w_up, w_down, layer):
    np_, ns = xp.shape[1], xs.shape[1]
    tok = lambda n, w: pl.BlockSpec((None, n, w), lambda e: (e, 0, 0))
    wspec = lambda a, b: pl.BlockSpec((None, None, a, b), lambda e: (layer, e, 0, 0))
    return pl.pallas_call(
        _ffn_body,
        grid=(N_EXPERTS,),
        in_specs=[tok(np_, D_MODEL), tok(ns, D_MODEL), tok(np_, 1), tok(ns, 1),
                  wspec(D_MODEL, EXPERT_FF), wspec(D_MODEL, EXPERT_FF), wspec(EXPERT_FF, D_MODEL)],
        out_specs=[tok(np_, D_MODEL), tok(ns, D_MODEL)],
        out_shape=[jax.ShapeDtypeStruct(xp.shape, bf16), jax.ShapeDtypeStruct(xs.shape, bf16)],
        scratch_shapes=[pltpu.VMEM((D_MODEL, EXPERT_FF), bf16), pltpu.VMEM((D_MODEL, EXPERT_FF), bf16),
                        pltpu.VMEM((EXPERT_FF, D_MODEL), bf16)],
        compiler_params=_params(("arbitrary",)),
        name="expert_ffn",
    )(xp, xs, gp, gs, w_gate, w_up, w_down)


def _combine_body(t, cap, bpg, bps, x_ref, rank_ref, y_ref, mod_ref, o_ref):
    k = N_EXPERTS * cap
    er = lax.broadcasted_iota(jnp.int32, (LANE, k), 0)
    ec = lax.broadcasted_iota(jnp.int32, (LANE, k), 1) // cap
    slot = (lax.broadcasted_iota(jnp.int32, (t, k), 1) % cap).astype(f32)
    rank = rank_ref[...].astype(bf16)
    for sub in range(bps):
        col0 = ((pl.program_id(0) * bps + sub) % bpg) * N_EXPERTS
        expand = jnp.where(er == ec + col0, 1.0, 0.0).astype(bf16)
        rexp = jnp.dot(rank, expand, preferred_element_type=f32)
        pt = jnp.where(rexp == slot, 1.0, 0.0).astype(bf16)
        y = y_ref[:, sub * cap:(sub + 1) * cap, :].reshape(k, D_MODEL)
        moe = jnp.dot(pt, y, preferred_element_type=f32)
        rs = slice(sub * t, (sub + 1) * t)
        o_ref[rs, :] = x_ref[rs, :] + mod_ref[:, 5 * D_MODEL:6 * D_MODEL] * moe


def _combine(x, rank, y, mod3, mod_base, mod_stride, nb, t, bpg):
    cap = CAPACITY_FACTOR * t // N_EXPERTS
    bps = 2 if (mod_stride == 0 and bpg % 2 == 0) else 1
    return pl.pallas_call(
        functools.partial(_combine_body, t, cap, bpg, bps),
        grid=(nb // bps,),
        in_specs=[pl.BlockSpec((bps * t, D_MODEL), lambda b: (b, 0)),
                  pl.BlockSpec((t, LANE), lambda b: (b * bps // bpg, 0)),
                  pl.BlockSpec((N_EXPERTS, bps * cap, D_MODEL), lambda b: (0, b, 0)),
                  pl.BlockSpec((None, 1, 6 * D_MODEL), lambda b: (mod_base + mod_stride * b, 0, 0))],
        out_specs=pl.BlockSpec((bps * t, D_MODEL), lambda b: (b, 0)),
        out_shape=jax.ShapeDtypeStruct(x.shape, f32),
        compiler_params=_params(("arbitrary",)),
        name=f"combine_{t}",
    )(x, rank, y, mod3)


def _pad_row(v, off, width=LANE):
    return jnp.zeros((1, width), f32).at[0, off:off + v.shape[0]].set(v.astype(f32))


def _layer_params(l, a):
    w_in = a["w_in"][l]
    pts, acc = [], 0
    splits = (MIX_W,) * 4 + (4 * HEADS,) + (MIX_W,) * 7 + (SSM_CONV_CH, 2 * HEADS, N_BRANCH * D_MODEL)
    for s in splits:
        pts.append((acc, acc + s))
        acc += s
    seg = lambda i: w_in[:, pts[i][0]:pts[i][1]]
    order = [14, 0, 1, 2, 3, 5, 6, 7, 8, 9, 10, 12, 11]
    w_main = jnp.concatenate([seg(i) for i in order] + [jnp.zeros((D_MODEL, U_COLS - U_USED), f32)],
                             axis=1).astype(bf16)
    w_side = jnp.concatenate([seg(4), seg(13), jnp.zeros((D_MODEL, LANE - 24), f32)], axis=1).astype(bf16)
    tile = lambda v, reps: jnp.tile(v.astype(f32), reps).reshape(1, -1)
    p = dict(
        w_main=w_main, w_side=w_side,
        norm1=a["norm1_w"][l].reshape(1, -1), norm2=a["norm2_w"][l].reshape(1, -1),
        mlstm_gb=_pad_row(a["mlstm_gate_b"][l].reshape(-1), G_MLSTM),
        mlstm_nw=a["mlstm_norm_w"][l].reshape(1, -1),
        na_qn=tile(a["na_qnorm_w"][l], HEADS), na_kn=tile(a["na_knorm_w"][l], HEADS),
        na_rpb=_na_rpb_pad(a["na_rpb"][l]),
        diff_qn=tile(a["diff_qnorm_w"][l], 2 * HEADS), diff_kn=tile(a["diff_knorm_w"][l], 2 * HEADS),
        diff_lam=a["diff_lambda"][l], diff_nw=tile(a["diff_norm_w"][l], HEADS),
        ssm_cw=a["ssm_conv_w"][l], ssm_cb=a["ssm_conv_b"][l].reshape(1, -1),
        ssm_dtb=_pad_row(a["ssm_dt_bias"][l].reshape(-1), G_DT),
        ssm_alog=_pad_row(a["ssm_a_log"][l].reshape(-1), G_DT),
        ssm_d=jnp.repeat(a["ssm_d"][l].astype(f32), HEAD_W).reshape(1, -1),
        ssm_nw=a["ssm_norm_w"][l].reshape(1, -1),
        wb=a["w_branch"][l].astype(bf16), wo=a["w_out"][l].astype(bf16),
        wr=jnp.concatenate([a["w_router"][l], jnp.zeros((D_MODEL, LANE - N_EXPERTS), f32)], axis=1).astype(bf16),
        w_gate=a["w_gate"], w_up=a["w_up"], w_down=a["w_down"],
    )
    return p


def _mixers(u, g, nb, seq, p, l, lam_init, ctx, tables):
    if ctx is None:
        a_out, m_c, m_n, m_m = _mlstm(u, g, nb, seq, p["mlstm_gb"], p["mlstm_nw"], None)
        b_out, na_k, na_v = _na_prompt(u, nb, seq, p["na_qn"], p["na_kn"])
        c_out, df_k, df_v = _diff_prompt(u, nb, seq, lam_init, p["diff_qn"], p["diff_kn"], p["diff_lam"], p["diff_nw"])
        d_out, s_h = _ssd(u, g, nb, seq, p["ssm_cw"], p["ssm_cb"], p["ssm_dtb"], p["ssm_alog"], p["ssm_d"],
                          p["ssm_nw"], None)
        return (a_out, b_out, c_out, d_out), (na_k, na_v, df_k, df_v, m_c, m_n, m_m, s_h)
    (na_ck, na_cv, df_ck, df_cv, st_c, st_n, st_m, st_s) = ctx
    cos, sin = tables
    a_out = _mlstm(u, g, nb, seq, p["mlstm_gb"], p["mlstm_nw"], (st_c[:, l], st_n[:, l], st_m[:, l]))[0]
    b_out = _na_sample(u, nb, l, na_ck, na_cv, p["na_rpb"], p["na_qn"], p["na_kn"])
    c_out = _diff_sample(u, nb, l, lam_init, df_ck, df_cv, cos, sin, p["diff_qn"], p["diff_kn"], p["diff_lam"],
                         p["diff_nw"])
    d_out = _ssd(u, g, nb, seq, p["ssm_cw"], p["ssm_cb"], p["ssm_dtb"], p["ssm_alog"], p["ssm_d"], p["ssm_nw"],
                 st_s[:, l])[0]
    return (a_out, b_out, c_out, d_out), None


def kernel(x_prompt, x_sample, cache_na_k, cache_na_v, cache_diff_k, cache_diff_v, state_mlstm_C, state_mlstm_n, state_mlstm_m, state_ssm, c, c_ctx, norm1_w, norm2_w, w_ada, b_ada, w_in, mlstm_gate_b, mlstm_norm_w, na_qnorm_w, na_knorm_w, na_rpb, diff_qnorm_w, diff_knorm_w, diff_lambda, diff_norm_w, ssm_conv_w, ssm_conv_b, ssm_dt_bias, ssm_a_log, ssm_d, ssm_norm_w, w_branch, w_out, w_router, w_gate, w_up, w_down):
    a = dict(norm1_w=norm1_w, norm2_w=norm2_w, w_in=w_in, mlstm_gate_b=mlstm_gate_b, mlstm_norm_w=mlstm_norm_w,
             na_qnorm_w=na_qnorm_w, na_knorm_w=na_knorm_w, na_rpb=na_rpb, diff_qnorm_w=diff_qnorm_w,
             diff_knorm_w=diff_knorm_w, diff_lambda=diff_lambda, diff_norm_w=diff_norm_w, ssm_conv_w=ssm_conv_w,
             ssm_conv_b=ssm_conv_b, ssm_dt_bias=ssm_dt_bias, ssm_a_log=ssm_a_log, ssm_d=ssm_d,
             ssm_norm_w=ssm_norm_w, w_branch=w_branch, w_out=w_out, w_router=w_router, w_gate=w_gate, w_up=w_up,
             w_down=w_down)
    nb_p, nb_s = x_prompt.shape[0], x_sample.shape[0]
    xp = x_prompt.reshape(nb_p * SEQ, D_MODEL)
    xs = x_sample.reshape(nb_s * DEC_SEQ, D_MODEL)
    cc = jnp.concatenate([c_ctx[None, :], c, jnp.zeros((8 - 1 - nb_s, D_MODEL), f32)], axis=0)
    ctx = (cache_na_k.reshape(nb_s, DEPTH, PAST_LEN, MIX_W), cache_na_v.reshape(nb_s, DEPTH, PAST_LEN, MIX_W),
           cache_diff_k.reshape(nb_s, DEPTH, PAST_LEN, MIX_W), cache_diff_v.reshape(nb_s, DEPTH, PAST_LEN, MIX_W),
           state_mlstm_C, state_mlstm_n, state_mlstm_m, state_ssm)
    tables = _rope_tables()
    new = []
    for l in range(DEPTH):
        p = _layer_params(l, a)
        lam_init = 0.8 - 0.6 * math.exp(-0.3 * l)
        mod3 = _modulation(cc, w_ada, b_ada, l).reshape(8, 1, 6 * D_MODEL)
        groups = []
        for (x, nb, seq, base, stride, gctx) in ((xp, nb_p, SEQ, 0, 0, None), (xs, nb_s, DEC_SEQ, 1, 1, ctx)):
            u, g = _inproj(x, mod3, base, stride, seq, p["norm1"], p["w_main"], p["w_side"])
            brs, st = _mixers(u, g, nb, seq, p, l, lam_init, gctx, tables)
            if st is not None:
                new.append(st)
            x1, h2, aff = _merge(x, brs, u, mod3, base, stride, seq, p["wb"], p["wo"], p["norm2"], p["wr"])
            bpg = min(nb, LANE // N_EXPERTS)
            xg, gg, rank = _route(aff, h2, nb, seq, bpg)
            groups.append((x1, xg, gg, rank, nb, seq, base, stride, bpg))
        yp, ys = _ffn(groups[0][1], groups[1][1], groups[0][2], groups[1][2], p["w_gate"], p["w_up"], p["w_down"], l)
        outs = []
        for (x1, _, _, rank, nb, seq, base, stride, bpg), y in zip(groups, (yp, ys)):
            outs.append(_combine(x1, rank, y, mod3, base, stride, nb, seq, bpg))
        xp, xs = outs
    stack = lambda i, shape: jnp.stack([new[l][i].reshape(shape) for l in range(DEPTH)], axis=1)
    kv = (nb_p, SEQ, HEADS, HEAD_W)
    return (xp.reshape(x_prompt.shape), xs.reshape(x_sample.shape),
            stack(0, kv), stack(1, kv), stack(2, kv), stack(3, kv),
            stack(4, (nb_p, 2, HEADS, HEAD_W, HEAD_W)), stack(5, (nb_p, 2, HEADS, HEAD_W)),
            stack(6, (nb_p, 2, HEADS)), stack(7, (nb_p, 2, HEADS, HEAD_W, SSM_STATE)))
```

```python
import functools
import math

import jax
import jax.numpy as jnp
from jax import lax
from jax.experimental import pallas as pl
from jax.experimental.pallas import tpu as pltpu

f32 = jnp.float32
bf16 = jnp.bfloat16

D_MODEL = 1024
BATCH = 32
SEQ = 256
DEPTH = 2
DEC_BATCH = 4
DEC_SEQ = 1024
PAST_LEN = 512
GRID_W = 64
MIX_W = D_MODEL // 4
N_BRANCH = 4
HEADS = 4
HEAD_W = MIX_W // HEADS
NA_WIN_R = 8
NA_WIN_C = 16
DIFF_QK_DIM = HEAD_W // 2
SSM_GROUPS = 2
SSM_STATE = 64
SSM_CONV_K = 5
SSM_CONV_CH = MIX_W + 2 * SSM_GROUPS * SSM_STATE
CHUNK = 64
PROMPT_BPS = 2
SCAN_UNROLL = 8
N_EXPERTS = 16
EXPERT_FF = 1024
CAPACITY_FACTOR = 2
ROPE_BASE = 10000.0
EPS = 1e-6
NEG = -1e30

LANE = 128
VMEM_LIMIT = 56 * 1024 * 1024

COL_GPRE = 0
COL_MLSTM = 4096
COL_NA = 5120
COL_DIFF = 5888
COL_XBC = 6656
COL_Z = 7168
U_USED = 7424
U_COLS = 7680
G_MLSTM = 0
G_DT = 16


def _params(sem):
    return pltpu.CompilerParams(dimension_semantics=sem, vmem_limit_bytes=VMEM_LIMIT)


def _mm(a, b):
    return jnp.dot(a.astype(bf16), b.astype(bf16), preferred_element_type=f32)


def _mm_nt(a, b):
    return lax.dot_general(a.astype(bf16), b.astype(bf16), (((1,), (1,)), ((), ())), preferred_element_type=f32)


def _mm_tn(a, b):
    return lax.dot_general(a.astype(bf16), b.astype(bf16), (((0,), (0,)), ((), ())), preferred_element_type=f32)


def _sigmoid(x):
    return 0.5 * jnp.tanh(0.5 * x) + 0.5


def _seg_sum(x, seg):
    w = x.shape[-1]
    r = lax.broadcasted_iota(jnp.int32, (w, w), 0) // seg
    c = lax.broadcasted_iota(jnp.int32, (w, w), 1) // seg
    ones = jnp.where(r == c, 1.0, 0.0).astype(bf16)
    hi = x.astype(bf16)
    lo = (x - hi.astype(f32)).astype(bf16)
    return jnp.dot(hi, ones, preferred_element_type=f32) + jnp.dot(lo, ones, preferred_element_type=f32)


def _seg_rms(x, seg, w_row):
    ms = _seg_sum(x * x, seg) * (1.0 / seg)
    return x * lax.rsqrt(ms + EPS) * w_row


def _mod_body(c_ref, w_ref, b_ref, o_ref):
    o_ref[...] = _mm(jax.nn.silu(c_ref[...]), w_ref[...]) + b_ref[...]


def _modulation(cc, w_ada, b_ada, layer):
    tn = 1536
    n = w_ada.shape[2]
    return pl.pallas_call(
        _mod_body,
        grid=(n // tn,),
        in_specs=[pl.BlockSpec((8, D_MODEL), lambda j: (0, 0)),
                  pl.BlockSpec((None, D_MODEL, tn), lambda j: (layer, 0, j)),
                  pl.BlockSpec((None, 1, tn), lambda j: (layer, 0, j))],
        out_specs=pl.BlockSpec((8, tn), lambda j: (0, j)),
        out_shape=jax.ShapeDtypeStruct((8, n), f32),
        compiler_params=_params(("arbitrary",)),
        name="modulation",
    )(cc, w_ada, b_ada.reshape(b_ada.shape[0], 1, n))


def _inproj_body(ncol, x_ref, mod_ref, nw_ref, w_ref, wg_ref, u_ref, g_ref):
    x = x_ref[...]
    y = x * lax.rsqrt(jnp.mean(x * x, axis=-1, keepdims=True) + EPS) * nw_ref[...]
    hb = (y * (1.0 + mod_ref[:, D_MODEL:2 * D_MODEL]) + mod_ref[:, 0:D_MODEL]).astype(bf16)
    g_ref[...] = jnp.dot(hb, wg_ref[...], preferred_element_type=f32)
    cw = U_COLS // ncol
    for c in range(ncol):
        u_ref[:, c * cw:(c + 1) * cw] = jnp.dot(hb, w_ref[:, c * cw:(c + 1) * cw],
                                                preferred_element_type=f32).astype(u_ref.dtype)


def _inproj(x, mod3, mod_base, mod_stride, seq, nw_row, w_main, w_side):
    n = x.shape[0]
    tm = 512
    tps = max(seq // tm, 1)
    resident = lambda shape: pl.BlockSpec(shape, lambda i: (0, 0), pipeline_mode=pl.Buffered(1))
    return pl.pallas_call(
        functools.partial(_inproj_body, 3),
        grid=(n // tm,),
        in_specs=[pl.BlockSpec((tm, D_MODEL), lambda i: (i, 0)),
                  pl.BlockSpec((None, 1, 6 * D_MODEL), lambda i: (mod_base + mod_stride * (i // tps), 0, 0)),
                  pl.BlockSpec((1, D_MODEL), lambda i: (0, 0)),
                  resident((D_MODEL, U_COLS)), resident((D_MODEL, LANE))],
        out_specs=[pl.BlockSpec((tm, U_COLS), lambda i: (i, 0)),
                   pl.BlockSpec((tm, LANE), lambda i: (i, 0))],
        out_shape=[jax.ShapeDtypeStruct((n, U_COLS), bf16), jax.ShapeDtypeStruct((n, LANE), f32)],
        compiler_params=_params(("arbitrary",)),
        name="inproj",
    )(x, mod3, nw_row, w_main, w_side)


def _split3(x):
    hi = x.astype(bf16)
    r = x - hi.astype(f32)
    mid = r.astype(bf16)
    lo = (r - mid.astype(f32)).astype(bf16)
    return hi, mid, lo


def _select_cols(x, onehot):
    return jnp.dot(jnp.concatenate(_split3(x), axis=1), jnp.concatenate([onehot] * 3, axis=0),
                   preferred_element_type=f32)


def _select_rows(onehot, x):
    return jnp.dot(jnp.concatenate([onehot] * 3, axis=1), jnp.concatenate(_split3(x), axis=0),
                   preferred_element_type=f32)


def _chunk_scan(x, op, identity, reverse):
    n = x.shape[0]
    pos = lax.broadcasted_iota(jnp.int32, x.shape, 0) % CHUNK
    k = 1
    while k < CHUNK:
        if reverse:
            shifted, ok = pltpu.roll(x, n - k, 0), pos < CHUNK - k
        else:
            shifted, ok = pltpu.roll(x, k, 0), pos >= k
        x = op(x, jnp.where(ok, shifted, identity))
        k *= 2
    return x


def _dir_scan(x, op, identity, bwd_col):
    return jnp.where(bwd_col, _chunk_scan(x, op, identity, True), _chunk_scan(x, op, identity, False))


def _head_expand(col0):
    c = lax.broadcasted_iota(jnp.int32, (LANE, MIX_W), 0)
    h = lax.broadcasted_iota(jnp.int32, (LANE, MIX_W), 1) // HEAD_W
    return jnp.where(c == col0 + h, 1.0, 0.0).astype(bf16)


def _chunk_rows(x, seq):
    nrow = max(seq // CHUNK, 8)
    t = lax.broadcasted_iota(jnp.int32, (seq, MIX_W), 0) % CHUNK
    s_ = lax.broadcasted_iota(jnp.int32, (seq, MIX_W), 1) % HEAD_W
    sel = lax.broadcasted_iota(jnp.int32, (nrow, seq), 1) // CHUNK == lax.broadcasted_iota(jnp.int32, (nrow, seq), 0)
    return _select_rows(jnp.where(sel, 1.0, 0.0).astype(bf16), jnp.where(t == s_, x, 0.0))


def _block_diag(x, reps, keep):
    return jnp.concatenate([x] * reps, axis=0) * keep


def _head_mask(rows, cols, rseg, cseg):
    return (lax.broadcasted_iota(jnp.int32, (rows, cols), 0) // rseg) == (lax.broadcasted_iota(jnp.int32, (rows, cols), 1) // cseg)


def _mlstm_body(seq, has_state, *refs):
    if has_state:
        (qkvo_ref, g_ref, gb_ref, nw_ref, c0_ref, n0_ref, m0_ref,
         out_ref, cs_ref, ns_ref, ms_ref, bx_scr, mx_scr, vx_scr, vrow_scr, cbd_scr, hf_scr, hb_scr) = refs
    else:
        (qkvo_ref, g_ref, gb_ref, nw_ref,
         out_ref, cs_ref, ns_ref, ms_ref, bx_scr, mx_scr, vx_scr, vrow_scr, cbd_scr, hf_scr, hb_scr) = refs
    nc = seq // CHUNK
    g = g_ref[...] + gb_ref[...]
    lane = lax.broadcasted_iota(jnp.int32, g.shape, 1)
    bwd_col = (lane % 16) >= 8
    bsum = _dir_scan(jax.nn.log_sigmoid(g), jnp.add, 0.0, bwd_col)
    b_i = pltpu.roll(bsum, LANE - HEADS, 1)
    vcol = g - b_i
    mcol = b_i + _dir_scan(vcol, jnp.maximum, NEG, bwd_col)
    for d in range(2):
        e = _head_expand(G_MLSTM + d * 8)
        bx_scr[d] = _select_cols(b_i, e)
        mx_scr[d] = _select_cols(mcol, e)
        vx = _select_cols(vcol, e)
        vx_scr[d] = vx
        vrow = _chunk_rows(vx, seq)
        for c in range(nc):
            vrow_scr[d, c] = vrow[c:c + 1, :]

    grp = lax.broadcasted_iota(jnp.int32, (1, MIX_W), 1) // HEAD_W
    cbd_scr[...] = jnp.zeros(cbd_scr.shape, f32)
    n_rows, m_rows = [], []
    for d in range(2):
        if has_state:
            for h in range(HEADS):
                cbd_scr[d, h * HEAD_W:(h + 1) * HEAD_W, h * HEAD_W:(h + 1) * HEAD_W] = c0_ref[d, h]
            n_rows.append(jnp.concatenate([n0_ref[d, h:h + 1, :] for h in range(HEADS)], axis=1))
            m_row = jnp.zeros((1, MIX_W), f32)
            for h in range(HEADS):
                m_row = jnp.where(grp == h, m0_ref[d:d + 1, h:h + 1], m_row)
            m_rows.append(m_row)
        else:
            n_rows.append(jnp.zeros((1, MIX_W), f32))
            m_rows.append(jnp.zeros((1, MIX_W), f32))

    li = lax.broadcasted_iota(jnp.int32, (CHUNK, MIX_W), 0)
    si = lax.broadcasted_iota(jnp.int32, (CHUNK, MIX_W), 1) % HEAD_W
    valid = (si <= li, si >= li)
    bd = _head_mask(MIX_W, MIX_W, HEAD_W, HEAD_W)
    bd_f32 = jnp.where(bd, 1.0, 0.0)
    ones_bd = bd_f32.astype(bf16)

    def step(c, carry):
        new = []
        for d in range(2):
            n_row, m_row = carry[2 * d], carry[2 * d + 1]
            cidx = c if d == 0 else nc - 1 - c
            r0 = pl.multiple_of(cidx * CHUNK, CHUNK)
            last = CHUNK - 1 if d == 0 else 0
            q = qkvo_ref[pl.ds(r0, CHUNK), 0:MIX_W].astype(f32)
            k = qkvo_ref[pl.ds(r0, CHUNK), MIX_W:2 * MIX_W].astype(f32) * (HEAD_W ** -0.5)
            vb = qkvo_ref[pl.ds(r0, CHUNK), 2 * MIX_W:3 * MIX_W].astype(bf16)
            qb = q.astype(bf16)
            bx = bx_scr[d, pl.ds(r0, CHUNK), :]
            mx = mx_scr[d, pl.ds(r0, CHUNK), :]
            vx = vx_scr[d, pl.ds(r0, CHUNK), :]
            p = jnp.exp(jnp.where(valid[d], bx + vrow_scr[d, cidx] - mx, NEG))
            smat = _mm_nt(qb, _block_diag(k.astype(bf16), HEADS, ones_bd)) * p
            intra = _mm(smat, _block_diag(vb, HEADS, ones_bd))
            dsum = _mm(smat, ones_bd)
            gg = bx + m_row
            mt = jnp.maximum(gg, mx)
            a = jnp.exp(mx - mt)
            w_prev = jnp.exp(gg - mt)
            inter = _mm(qb, cbd_scr[d])
            dint = _mm(q * n_row, ones_bd)
            num = intra * a + inter * w_prev
            den = dsum * a + dint * w_prev
            h_scr = hf_scr if d == 0 else hb_scr
            h_scr[pl.ds(r0, CHUNK), :] = num / jnp.maximum(jnp.abs(den), jnp.exp(-mt))
            bl = bx[last:last + 1, :]
            mloc = mx[last:last + 1, :]
            m_new = jnp.maximum(bl + m_row, mloc)
            dec = jnp.exp(bl + m_row - m_new)
            fac = jnp.exp(mloc - m_new)
            kw = k * jnp.exp(bl + vx - mloc)
            kv = _mm_tn(kw, vb) * bd_f32
            cbd_scr[d] = cbd_scr[d] * dec + kv * fac
            new += [n_row * dec + jnp.sum(kw, axis=0, keepdims=True) * fac, m_new]
        return tuple(new)

    fin_state = lax.fori_loop(0, nc, step, (n_rows[0], m_rows[0], n_rows[1], m_rows[1]),
                              unroll=min(SCAN_UNROLL, nc))
    for d in range(2):
        n_row, m_row = fin_state[2 * d], fin_state[2 * d + 1]
        for h in range(HEADS):
            hl = slice(h * HEAD_W, (h + 1) * HEAD_W)
            cs_ref[d, h] = cbd_scr[d, hl, hl]
            ns_ref[d, h:h + 1, :] = n_row[:, hl]
            ms_ref[d:d + 1, h:h + 1] = m_row[:, h * HEAD_W:h * HEAD_W + 1]

    blk = 256

    def fin(i, carry):
        r0 = pl.multiple_of(i * blk, blk)
        hs = hf_scr[pl.ds(r0, blk), :] + hb_scr[pl.ds(r0, blk), :]
        hn = _seg_rms(hs, HEAD_W, nw_ref[...])
        o = qkvo_ref[pl.ds(r0, blk), 3 * MIX_W:4 * MIX_W].astype(f32)
        out_ref[pl.ds(r0, blk), :] = (hn * jax.nn.sigmoid(o)).astype(out_ref.dtype)
        return carry

    lax.fori_loop(0, seq // blk, fin, 0)


def _mlstm(u, g, nb, seq, gb_row, nw_row, state):
    has_state = state is not None
    nc = seq // CHUNK
    in_specs = [pl.BlockSpec((seq, 4 * MIX_W), lambda b: (b, COL_MLSTM // (4 * MIX_W))),
                pl.BlockSpec((seq, LANE), lambda b: (b, 0)),
                pl.BlockSpec((1, LANE), lambda b: (0, 0)),
                pl.BlockSpec((1, MIX_W), lambda b: (0, 0))]
    args = [u, g, gb_row, nw_row]
    if has_state:
        in_specs += [pl.BlockSpec((None, 2, HEADS, HEAD_W, HEAD_W), lambda b: (b, 0, 0, 0, 0)),
                     pl.BlockSpec((None, 2, HEADS, HEAD_W), lambda b: (b, 0, 0, 0)),
                     pl.BlockSpec((None, 2, HEADS), lambda b: (b, 0, 0))]
        args += list(state)
    return pl.pallas_call(
        functools.partial(_mlstm_body, seq, has_state),
        grid=(nb,),
        in_specs=in_specs,
        out_specs=[pl.BlockSpec((seq, MIX_W), lambda b: (b, 0)),
                   pl.BlockSpec((None, 2, HEADS, HEAD_W, HEAD_W), lambda b: (b, 0, 0, 0, 0)),
                   pl.BlockSpec((None, 2, HEADS, HEAD_W), lambda b: (b, 0, 0, 0)),
                   pl.BlockSpec((None, 2, HEADS), lambda b: (b, 0, 0))],
        out_shape=[jax.ShapeDtypeStruct((nb * seq, MIX_W), bf16),
                   jax.ShapeDtypeStruct((nb, 2, HEADS, HEAD_W, HEAD_W), f32),
                   jax.ShapeDtypeStruct((nb, 2, HEADS, HEAD_W), f32),
                   jax.ShapeDtypeStruct((nb, 2, HEADS), f32)],
        scratch_shapes=[pltpu.VMEM((2, seq, MIX_W), f32), pltpu.VMEM((2, seq, MIX_W), f32),
                        pltpu.VMEM((2, seq, MIX_W), f32), pltpu.VMEM((2, nc, 1, MIX_W), f32),
                        pltpu.VMEM((2, MIX_W, MIX_W), f32),
                        pltpu.VMEM((seq, MIX_W), f32), pltpu.VMEM((seq, MIX_W), f32)],
        compiler_params=_params(("arbitrary",)),
        name="mlstm_state" if has_state else "mlstm",
    )(*args)


def _ssd_body(seq, has_state, *refs):
    if has_state:
        (z_ref, xbc_ref, g_ref, cw_ref, cb_ref, dtb_ref, alog_ref, dskip_ref, nw_ref, h0_ref,
         out_ref, hs_ref, xpad_scr, xc_scr, ax_scr, dx_scr, arow_scr, ht_scr, yf_scr, yb_scr) = refs
    else:
        (z_ref, xbc_ref, g_ref, cw_ref, cb_ref, dtb_ref, alog_ref, dskip_ref, nw_ref,
         out_ref, hs_ref, xpad_scr, xc_scr, ax_scr, dx_scr, arow_scr, ht_scr, yf_scr, yb_scr) = refs
    nc = seq // CHUNK
    pad = 8
    blk = 256
    nblk = seq // blk
    xpad_scr[0:pad, :] = jnp.zeros((pad, SSM_CONV_CH), f32)
    xpad_scr[pad + seq:2 * pad + seq, :] = jnp.zeros((pad, SSM_CONV_CH), f32)
    xpad_scr[pad:pad + seq, :] = xbc_ref[...].astype(f32)

    cblk = 128
    for i in range(seq // cblk):
        r0 = i * cblk
        acc = jnp.zeros((cblk, SSM_CONV_CH), f32) + cb_ref[...]
        for kk in range(SSM_CONV_K):
            off = r0 + pad - SSM_CONV_K // 2 + kk
            acc = acc + xpad_scr[off:off + cblk, :] * cw_ref[kk:kk + 1, :]
        xc_scr[r0:r0 + cblk, :] = jax.nn.silu(acc)

    dt = jax.nn.softplus(g_ref[...] + dtb_ref[...])
    lane = lax.broadcasted_iota(jnp.int32, dt.shape, 1)
    acum = _dir_scan(dt * (-jnp.exp(alog_ref[...])), jnp.add, 0.0, (lane % 8) >= HEADS)
    for d in range(2):
        e = _head_expand(G_DT + d * HEADS)
        ax = _select_cols(acum, e)
        ax_scr[d] = ax
        dx_scr[d] = _select_cols(dt, e)
        arow = _chunk_rows(ax, seq)
        for c in range(nc):
            arow_scr[d, c] = arow[c:c + 1, :]

    gn = SSM_GROUPS * SSM_STATE
    rep = HEADS // SSM_GROUPS
    ht_scr[...] = jnp.zeros(ht_scr.shape, f32)
    if has_state:
        for d in range(2):
            for h in range(HEADS):
                g0 = (h // rep) * SSM_STATE
                ht_scr[d, g0:g0 + SSM_STATE, h * HEAD_W:(h + 1) * HEAD_W] = h0_ref[d, h].T
    li = lax.broadcasted_iota(jnp.int32, (CHUNK, MIX_W), 0)
    si = lax.broadcasted_iota(jnp.int32, (CHUNK, MIX_W), 1) % HEAD_W
    valid = (si <= li, si >= li)
    state_keep = jnp.where(_head_mask(gn, MIX_W, SSM_STATE, rep * HEAD_W), 1.0, 0.0)
    key_keep = jnp.where(_head_mask(HEADS * CHUNK, gn, rep * CHUNK, SSM_STATE), 1.0, 0.0).astype(bf16)
    head_keep = jnp.where(_head_mask(MIX_W, MIX_W, HEAD_W, HEAD_W), 1.0, 0.0).astype(bf16)

    def step(c, carry):
        for d in range(2):
            cidx = c if d == 0 else nc - 1 - c
            r0 = pl.multiple_of(cidx * CHUNK, CHUNK)
            last = CHUNK - 1 if d == 0 else 0
            xs = xc_scr[pl.ds(r0, CHUNK), 0:MIX_W]
            bm = xc_scr[pl.ds(r0, CHUNK), MIX_W:MIX_W + gn].astype(bf16)
            cm = xc_scr[pl.ds(r0, CHUNK), MIX_W + gn:MIX_W + 2 * gn].astype(bf16)
            ax = ax_scr[d, pl.ds(r0, CHUNK), :]
            decay = jnp.exp(jnp.where(valid[d], ax - arow_scr[d, cidx], NEG))
            scores = _mm_nt(cm, _block_diag(bm, HEADS, key_keep)) * decay
            xdt = xs * dx_scr[d, pl.ds(r0, CHUNK), :]
            y = _mm(scores, _block_diag(xdt.astype(bf16), HEADS, head_keep)) + _mm(cm, ht_scr[d]) * jnp.exp(ax)
            y_scr = yf_scr if d == 0 else yb_scr
            y_scr[pl.ds(r0, CHUNK), :] = y
            al = ax[last:last + 1, :]
            ht_scr[d] = ht_scr[d] * jnp.exp(al) + _mm_tn(bm, xdt * jnp.exp(al - ax)) * state_keep
        return carry

    lax.fori_loop(0, nc, step, 0, unroll=min(SCAN_UNROLL, nc))
    for d in range(2):
        for h in range(HEADS):
            g0 = (h // rep) * SSM_STATE
            hs_ref[d, h] = ht_scr[d, g0:g0 + SSM_STATE, h * HEAD_W:(h + 1) * HEAD_W].T

    def fin(i, carry):
        r0 = pl.multiple_of(i * blk, blk)
        y = yf_scr[pl.ds(r0, blk), :] + yb_scr[pl.ds(r0, blk), :] + dskip_ref[...] * xc_scr[pl.ds(r0, blk), 0:MIX_W]
        y = y * jax.nn.silu(z_ref[pl.ds(r0, blk), :].astype(f32))
        y = y * lax.rsqrt(jnp.mean(y * y, axis=-1, keepdims=True) + EPS) * nw_ref[...]
        out_ref[pl.ds(r0, blk), :] = y.astype(out_ref.dtype)
        return carry

    lax.fori_loop(0, nblk, fin, 0)


def _ssd(u, g, nb, seq, cw, cb_row, dtb_row, alog_row, dskip_row, nw_row, state):
    has_state = state is not None
    nc = seq // CHUNK
    const = lambda shape: pl.BlockSpec(shape, lambda b: (0,) * len(shape))
    in_specs = [pl.BlockSpec((seq, MIX_W), lambda b: (b, COL_Z // MIX_W)),
                pl.BlockSpec((seq, SSM_CONV_CH), lambda b: (b, COL_XBC // SSM_CONV_CH)),
                pl.BlockSpec((seq, LANE), lambda b: (b, 0)),
                const((SSM_CONV_K, SSM_CONV_CH)), const((1, SSM_CONV_CH)), const((1, LANE)), const((1, LANE)),
                const((1, MIX_W)), const((1, MIX_W))]
    args = [u, u, g, cw, cb_row, dtb_row, alog_row, dskip_row, nw_row]
    if has_state:
        in_specs.append(pl.BlockSpec((None, 2, HEADS, HEAD_W, SSM_STATE), lambda b: (b, 0, 0, 0, 0)))
        args.append(state)
    return pl.pallas_call(
        functools.partial(_ssd_body, seq, has_state),
        grid=(nb,),
        in_specs=in_specs,
        out_specs=[pl.BlockSpec((seq, MIX_W), lambda b: (b, 0)),
                   pl.BlockSpec((None, 2, HEADS, HEAD_W, SSM_STATE), lambda b: (b, 0, 0, 0, 0))],
        out_shape=[jax.ShapeDtypeStruct((nb * seq, MIX_W), bf16),
                   jax.ShapeDtypeStruct((nb, 2, HEADS, HEAD_W, SSM_STATE), f32)],
        scratch_shapes=[pltpu.VMEM((seq + 16, SSM_CONV_CH), f32), pltpu.VMEM((seq, SSM_CONV_CH), f32),
                        pltpu.VMEM((2, seq, MIX_W), f32), pltpu.VMEM((2, seq, MIX_W), f32),
                        pltpu.VMEM((2, nc, 1, MIX_W), f32),
                        pltpu.VMEM((2, SSM_GROUPS * SSM_STATE, MIX_W), f32),
                        pltpu.VMEM((seq, MIX_W), f32), pltpu.VMEM((seq, MIX_W), f32)],
        compiler_params=_params(("arbitrary",)),
        name="ssd_state" if has_state else "ssd",
    )(*args)


def _with_ones(v):
    vb = v.astype(bf16)
    ones = jnp.ones((v.shape[0], HEAD_W), bf16)
    return jnp.concatenate([x for h in range(HEADS) for x in (vb[:, h * HEAD_W:(h + 1) * HEAD_W], ones)], axis=1)


def _softmax_av(scores, vexts):
    m = scores[0].max(axis=1, keepdims=True)
    for s in scores[1:]:
        m = jnp.maximum(m, s.max(axis=1, keepdims=True))
    r = None
    for s, vx in zip(scores, vexts):
        t = jnp.dot(jnp.exp(s - m).astype(bf16), vx, preferred_element_type=f32)
        r = t if r is None else r + t
    return r[:, :HEAD_W] / r[:, HEAD_W:]


def _store_layer(ref, sub, layer, whole, val):
    if whole:
        for l2 in range(DEPTH):
            ref[sub, l2] = val if l2 == layer else jnp.zeros_like(val)
    else:
        ref[sub] = val


def _layer_out(nb, bps, tail, layer, prev):
    zeros = (0,) * len(tail)
    shape = jax.ShapeDtypeStruct((nb, DEPTH) + tail, f32)
    if prev is None:
        return shape, pl.BlockSpec((bps, DEPTH) + tail, lambda b: (b, 0) + zeros), True
    return shape, pl.BlockSpec((bps, None) + tail, lambda b: (b, layer) + zeros), False


def _na_prompt_body(seq, layer, whole, q_ref, k_ref, v_ref, qn_ref, kn_ref, *rest):
    o_ref, ko_ref, vo_ref = rest[-3:]
    scale = HEAD_W ** -0.5
    qb = (_seg_rms(q_ref[...].astype(f32), HEAD_W, qn_ref[...]) * scale).astype(bf16)
    kn = _seg_rms(k_ref[...].astype(f32), HEAD_W, kn_ref[...])
    v = v_ref[...].astype(f32)
    for s0 in range(0, q_ref.shape[0], seq):
        rs = slice(s0, s0 + seq)
        _store_layer(ko_ref, s0 // seq, layer, whole, kn[rs])
        _store_layer(vo_ref, s0 // seq, layer, whole, v[rs])
        kt = kn[rs].T.astype(bf16)
        vx = _with_ones(v[rs])
        for h in range(HEADS):
            hl = slice(h * HEAD_W, (h + 1) * HEAD_W)
            s = jnp.dot(qb[rs, hl], kt[hl, :], preferred_element_type=f32)
            o_ref[rs, hl] = _softmax_av([s], [vx[:, 2 * h * HEAD_W:2 * (h + 1) * HEAD_W]]).astype(o_ref.dtype)


def _na_prompt(u, nb, seq, qn_row, kn_row, layer, prev):
    cb = COL_NA // MIX_W
    rows = PROMPT_BPS * seq
    blk = lambda j: pl.BlockSpec((rows, MIX_W), lambda b: (b, j))
    const = pl.BlockSpec((1, MIX_W), lambda b: (0, 0))
    shape, spec, whole = _layer_out(nb, PROMPT_BPS, (seq, MIX_W), layer, prev)
    extra = [] if prev is None else list(prev)
    return pl.pallas_call(
        functools.partial(_na_prompt_body, seq, layer, whole),
        grid=(nb // PROMPT_BPS,),
        in_specs=[blk(cb), blk(cb + 1), blk(cb + 2), const, const] + [pl.BlockSpec(memory_space=pl.ANY)] * len(extra),
        out_specs=[blk(0), spec, spec],
        out_shape=[jax.ShapeDtypeStruct((nb * seq, MIX_W), bf16), shape, shape],
        input_output_aliases={} if prev is None else {5: 1, 6: 2},
        compiler_params=_params(("arbitrary",)),
        name="na_prompt",
    )(u, u, u, qn_row, kn_row, *extra)


def _na_sample_body(qblk, q_ref, k_ref, v_ref, kc_ref, vc_ref, rpb_ref, qn_ref, kn_ref, o_ref,
                    kt_scr, vx_scr, bias_scr):
    b = pl.program_id(0)
    i = pl.program_id(1)
    seq = k_ref.shape[0]
    rows = seq // GRID_W

    @pl.when((b == 0) & (i == 0))
    def _():
        j = lax.broadcasted_iota(jnp.int32, (GRID_W, LANE), 0)
        lane = lax.broadcasted_iota(jnp.int32, (GRID_W, LANE), 1)
        j2 = lane % GRID_W
        cs = jnp.clip(j - NA_WIN_C // 2, 0, GRID_W - NA_WIN_C)
        ok = (j2 >= cs) & (j2 < cs + NA_WIN_C)
        left = lane < GRID_W
        neg = jnp.full((GRID_W, LANE), NEG, f32)
        for h in range(HEADS):
            lo, hi = [], []
            for dr in range(2 * NA_WIN_R - 1):
                row = jnp.broadcast_to(rpb_ref[h, dr:dr + 1, :], (GRID_W, LANE))
                lo.append(jnp.where(ok, pltpu.roll(row, LANE - (NA_WIN_C - 1), 1, stride=1, stride_axis=0), NEG))
                hi.append(jnp.where(ok, pltpu.roll(row, GRID_W - (NA_WIN_C - 1), 1, stride=1, stride_axis=0), NEG))
            for qi in range(rows):
                rs = min(max(qi - NA_WIN_R // 2, 0), rows - NA_WIN_R)
                for pair in range(rows // 2):
                    k_even, k_odd = 2 * pair, 2 * pair + 1
                    a = lo[k_even - qi + NA_WIN_R - 1] if rs <= k_even < rs + NA_WIN_R else neg
                    c = hi[k_odd - qi + NA_WIN_R - 1] if rs <= k_odd < rs + NA_WIN_R else neg
                    bias_scr[h, qi * GRID_W:(qi + 1) * GRID_W, pair * LANE:(pair + 1) * LANE] = jnp.where(left, a, c)

    @pl.when(i == 0)
    def _():
        kn = _seg_rms(k_ref[...].astype(f32), HEAD_W, kn_ref[...])
        kt_scr[:, 0:seq] = kn.T.astype(bf16)
        kt_scr[:, seq:] = kc_ref[...].T.astype(bf16)
        vx_scr[0:seq, :] = _with_ones(v_ref[...])
        vx_scr[seq:, :] = _with_ones(vc_ref[...])

    r0 = pl.multiple_of(i * qblk, qblk)
    qb = (_seg_rms(q_ref[...].astype(f32), HEAD_W, qn_ref[...]) * (HEAD_W ** -0.5)).astype(bf16)
    for h in range(HEADS):
        hl = slice(h * HEAD_W, (h + 1) * HEAD_W)
        xl = slice(2 * h * HEAD_W, 2 * (h + 1) * HEAD_W)
        s_own = jnp.dot(qb[:, hl], kt_scr[hl, 0:seq], preferred_element_type=f32) + bias_scr[h, pl.ds(r0, qblk), :]
        s_ctx = jnp.dot(qb[:, hl], kt_scr[hl, seq:], preferred_element_type=f32)
        o = _softmax_av([s_own, s_ctx], [vx_scr[0:seq, xl], vx_scr[seq:, xl]])
        o_ref[:, hl] = o.astype(o_ref.dtype)


def _na_sample(u, nb, layer, cache_k, cache_v, rpb_pad, qn_row, kn_row):
    seq = DEC_SEQ
    qblk = 256
    nq = seq // qblk
    cb = COL_NA // MIX_W
    full = lambda j: pl.BlockSpec((seq, MIX_W), lambda b, i: (b, j))
    ctx = pl.BlockSpec((None, None, PAST_LEN, MIX_W), lambda b, i: (b, layer, 0, 0))
    const = pl.BlockSpec((1, MIX_W), lambda b, i: (0, 0))
    return pl.pallas_call(
        functools.partial(_na_sample_body, qblk),
        grid=(nb, nq),
        in_specs=[pl.BlockSpec((qblk, MIX_W), lambda b, i: (b * nq + i, cb)), full(cb + 1), full(cb + 2),
                  ctx, ctx,
                  pl.BlockSpec((HEADS, 2 * NA_WIN_R, LANE), lambda b, i: (0, 0, 0)),
                  const, const],
        out_specs=pl.BlockSpec((qblk, MIX_W), lambda b, i: (b * nq + i, 0)),
        out_shape=jax.ShapeDtypeStruct((nb * seq, MIX_W), bf16),
        scratch_shapes=[pltpu.VMEM((MIX_W, seq + PAST_LEN), bf16), pltpu.VMEM((seq + PAST_LEN, 2 * MIX_W), bf16),
                        pltpu.VMEM((HEADS, seq, seq), f32)],
        compiler_params=_params(("arbitrary", "arbitrary")),
        name="na_sample",
    )(u, u, u, cache_k, cache_v, rpb_pad, qn_row, kn_row)


def _na_rpb_pad(rpb):
    c = rpb.shape[2]
    return jnp.pad(rpb.astype(f32), ((0, 0), (0, 1), (0, LANE - c)))


def _diff_lambda(lam_ref, lam_init):
    lv = lam_ref[...]
    s1 = jnp.sum(lv[0:1, :] * lv[1:2, :], axis=1, keepdims=True)
    s2 = jnp.sum(lv[2:3, :] * lv[3:4, :], axis=1, keepdims=True)
    return jnp.exp(s1) - jnp.exp(s2) + lam_init


def _diff_heads(q, kt, vext, lam):
    outs = []
    for h in range(HEADS):
        o = []
        for m in range(2):
            c0 = h * HEAD_W + m * DIFF_QK_DIM
            s = jnp.dot(q[:, c0:c0 + DIFF_QK_DIM], kt[c0:c0 + DIFF_QK_DIM, :], preferred_element_type=f32)
            o.append(_softmax_av([s], [vext[:, 2 * h * HEAD_W:2 * (h + 1) * HEAD_W]]))
        outs.append(o[0] - lam * o[1])
    return jnp.concatenate(outs, axis=1)


def _diff_prompt_body(lam_init, seq, layer, whole, q_ref, k_ref, v_ref, qn_ref, kn_ref, lam_ref, nw_ref, *rest):
    o_ref, ko_ref, vo_ref = rest[-3:]
    lam = _diff_lambda(lam_ref, lam_init)
    qb = (_seg_rms(q_ref[...].astype(f32), DIFF_QK_DIM, qn_ref[...]) * (DIFF_QK_DIM ** -0.5)).astype(bf16)
    kn = _seg_rms(k_ref[...].astype(f32), DIFF_QK_DIM, kn_ref[...])
    v = v_ref[...].astype(f32)
    for s0 in range(0, q_ref.shape[0], seq):
        rs = slice(s0, s0 + seq)
        _store_layer(ko_ref, s0 // seq, layer, whole, kn[rs])
        _store_layer(vo_ref, s0 // seq, layer, whole, v[rs])
        o = _diff_heads(qb[rs], kn[rs].T.astype(bf16), _with_ones(v[rs]), lam)
        o_ref[rs, :] = (_seg_rms(o, HEAD_W, nw_ref[...]) * (1.0 - lam_init)).astype(o_ref.dtype)


def _diff_prompt(u, nb, seq, lam_init, qn_row, kn_row, lam_vecs, nw_row, layer, prev):
    cb = COL_DIFF // MIX_W
    rows = PROMPT_BPS * seq
    blk = lambda j: pl.BlockSpec((rows, MIX_W), lambda b: (b, j))
    const = pl.BlockSpec((1, MIX_W), lambda b: (0, 0))
    shape, spec, whole = _layer_out(nb, PROMPT_BPS, (seq, MIX_W), layer, prev)
    extra = [] if prev is None else list(prev)
    return pl.pallas_call(
        functools.partial(_diff_prompt_body, lam_init, seq, layer, whole),
        grid=(nb // PROMPT_BPS,),
        in_specs=[blk(cb), blk(cb + 1), blk(cb + 2), const, const,
                  pl.BlockSpec((4, DIFF_QK_DIM), lambda b: (0, 0)), const] + [pl.BlockSpec(memory_space=pl.ANY)] * len(extra),
        out_specs=[blk(0), spec, spec],
        out_shape=[jax.ShapeDtypeStruct((nb * seq, MIX_W), bf16), shape, shape],
        input_output_aliases={} if prev is None else {7: 1, 8: 2},
        compiler_params=_params(("arbitrary",)),
        name="diff_prompt",
    )(u, u, u, qn_row, kn_row, lam_vecs, nw_row, *extra)


def _rope(x, cos, sin):
    w = x.shape[-1]
    up = pltpu.roll(x, w - 8, 1)
    dn = pltpu.roll(x, 8, 1)
    first = (lax.broadcasted_iota(jnp.int32, x.shape, 1) % 16) < 8
    return x * cos + jnp.where(first, up, dn) * sin


def _diff_sample_body(lam_init, qblk, q_ref, k_ref, v_ref, kc_ref, vc_ref, cos_ref, sin_ref, qn_ref, kn_ref,
                      lam_ref, nw_ref, o_ref, kt_scr, vx_scr):
    i = pl.program_id(1)
    seq = k_ref.shape[0]

    @pl.when(i == 0)
    def _():
        kn = _seg_rms(k_ref[...].astype(f32), DIFF_QK_DIM, kn_ref[...])
        kt_scr[:, 0:seq] = _rope(kn, cos_ref[...], sin_ref[...]).T.astype(bf16)
        kt_scr[:, seq:] = kc_ref[...].T.astype(bf16)
        vx_scr[0:seq, :] = _with_ones(v_ref[...])
        vx_scr[seq:, :] = _with_ones(vc_ref[...])

    lam = _diff_lambda(lam_ref, lam_init)
    r0 = pl.multiple_of(i * qblk, qblk)
    qn = _seg_rms(q_ref[...].astype(f32), DIFF_QK_DIM, qn_ref[...])
    qn = _rope(qn, cos_ref[pl.ds(r0, qblk), :], sin_ref[pl.ds(r0, qblk), :]) * (DIFF_QK_DIM ** -0.5)
    o = _diff_heads(qn.astype(bf16), kt_scr[...], vx_scr[...], lam)
    o_ref[...] = (_seg_rms(o, HEAD_W, nw_ref[...]) * (1.0 - lam_init)).astype(o_ref.dtype)


def _diff_sample(u, nb, layer, lam_init, cache_k, cache_v, cos, sin, qn_row, kn_row, lam_vecs, nw_row):
    seq = DEC_SEQ
    qblk = 512
    nq = seq // qblk
    cb = COL_DIFF // MIX_W
    full = lambda j: pl.BlockSpec((seq, MIX_W), lambda b, i: (b, j))
    ctx = pl.BlockSpec((None, None, PAST_LEN, MIX_W), lambda b, i: (b, layer, 0, 0))
    const = pl.BlockSpec((1, MIX_W), lambda b, i: (0, 0))
    tab = pl.BlockSpec((seq, MIX_W), lambda b, i: (0, 0))
    return pl.pallas_call(
        functools.partial(_diff_sample_body, lam_init, qblk),
        grid=(nb, nq),
        in_specs=[pl.BlockSpec((qblk, MIX_W), lambda b, i: (b * nq + i, cb)), full(cb + 1), full(cb + 2),
                  ctx, ctx, tab, tab, const, const,
                  pl.BlockSpec((4, DIFF_QK_DIM), lambda b, i: (0, 0)), const],
        out_specs=pl.BlockSpec((qblk, MIX_W), lambda b, i: (b * nq + i, 0)),
        out_shape=jax.ShapeDtypeStruct((nb * seq, MIX_W), bf16),
        scratch_shapes=[pltpu.VMEM((MIX_W, seq + PAST_LEN), bf16), pltpu.VMEM((seq + PAST_LEN, 2 * MIX_W), bf16)],
        compiler_params=_params(("arbitrary", "arbitrary")),
        name="diff_sample",
    )(u, u, u, cache_k, cache_v, cos, sin, qn_row, kn_row, lam_vecs, nw_row)


def _rope_tables():
    t = jnp.arange(DEC_SEQ)
    nf = DIFF_QK_DIM // 4
    inv = ROPE_BASE ** (-jnp.arange(nf, dtype=f32) / nf)
    ar = (t // GRID_W).astype(f32)[:, None] * inv
    ac = (t % GRID_W).astype(f32)[:, None] * inv
    cos = jnp.concatenate([jnp.cos(ar), jnp.cos(ar), jnp.cos(ac), jnp.cos(ac)], axis=1)
    sin = jnp.concatenate([-jnp.sin(ar), jnp.sin(ar), -jnp.sin(ac), jnp.sin(ac)], axis=1)
    reps = MIX_W // DIFF_QK_DIM
    return jnp.tile(cos, (1, reps)), jnp.tile(sin, (1, reps))


def _merge_body(x_ref, a_ref, b_ref, c_ref, d_ref, gp_ref, mod_ref, wb_ref, wo_ref, nw_ref, wr_ref,
                xo_ref, h2_ref, aff_ref):
    mixed = None
    for n, br in enumerate((a_ref, b_ref, c_ref, d_ref)):
        gate = _sigmoid(gp_ref[:, n * D_MODEL:(n + 1) * D_MODEL].astype(f32))
        t = gate * jnp.dot(br[...].astype(bf16), wb_ref[n], preferred_element_type=f32)
        mixed = t if mixed is None else mixed + t
    y = jnp.dot(mixed.astype(bf16), wo_ref[...], preferred_element_type=f32)
    x = x_ref[...] + mod_ref[:, 2 * D_MODEL:3 * D_MODEL] * y
    xo_ref[...] = x
    h = x * lax.rsqrt(jnp.mean(x * x, axis=-1, keepdims=True) + EPS) * nw_ref[...]
    h = h * (1.0 + mod_ref[:, 4 * D_MODEL:5 * D_MODEL]) + mod_ref[:, 3 * D_MODEL:4 * D_MODEL]
    hb = h.astype(bf16)
    h2_ref[...] = hb
    logits = jnp.dot(hb, wr_ref[...], preferred_element_type=f32)
    lane = lax.broadcasted_iota(jnp.int32, logits.shape, 1)
    logits = jnp.where(lane < N_EXPERTS, logits, NEG)
    e = jnp.exp(logits - logits.max(axis=1, keepdims=True))
    aff_ref[...] = e / e.sum(axis=1, keepdims=True)


def _merge(x, brs, u, mod3, mod_base, mod_stride, seq, wb, wo, nw_row, wr):
    n = x.shape[0]
    tm = 512
    tps = max(seq // tm, 1)
    row = lambda w: pl.BlockSpec((tm, w), lambda i: (i, 0))
    return pl.pallas_call(
        _merge_body,
        grid=(n // tm,),
        in_specs=[row(D_MODEL), row(MIX_W), row(MIX_W), row(MIX_W), row(MIX_W),
                  pl.BlockSpec((tm, N_BRANCH * D_MODEL), lambda i: (i, COL_GPRE // (N_BRANCH * D_MODEL))),
                  pl.BlockSpec((None, 1, 6 * D_MODEL), lambda i: (mod_base + mod_stride * (i // tps), 0, 0)),
                  pl.BlockSpec((N_BRANCH, MIX_W, D_MODEL), lambda i: (0, 0, 0)),
                  pl.BlockSpec((D_MODEL, D_MODEL), lambda i: (0, 0)),
                  pl.BlockSpec((1, D_MODEL), lambda i: (0, 0)),
                  pl.BlockSpec((D_MODEL, LANE), lambda i: (0, 0))],
        out_specs=[row(D_MODEL), row(D_MODEL), row(LANE)],
        out_shape=[jax.ShapeDtypeStruct((n, D_MODEL), f32), jax.ShapeDtypeStruct((n, D_MODEL), bf16),
                   jax.ShapeDtypeStruct((n, LANE), f32)],
        compiler_params=_params(("arbitrary",)),
        name="merge",
    )(x, *brs, u, mod3, wb, wo, nw_row, wr)


def _rows_prefix_sum(x):
    n = x.shape[0]
    pos = lax.broadcasted_iota(jnp.int32, x.shape, 0)
    k = 1
    while k < n:
        x = x + jnp.where(pos >= k, pltpu.roll(x, k, 0), 0.0)
        k *= 2
    return x


def _route_body(t, cap, bpg, aff_ref, h2_ref, xg_ref, g_ref, rank_ref, rrow_scr, arow_scr, oh_scr):
    bb = pl.program_id(1)

    @pl.when(bb == 0)
    def _():
        aff = aff_ref[...]
        tok = lax.broadcasted_iota(jnp.int32, aff.shape, 0)
        val = aff
        k = 2
        while k <= t:
            j = k // 2
            while j >= 1:
                lower = (tok & j) == 0
                pv = jnp.where(lower, pltpu.roll(val, t - j, 0), pltpu.roll(val, j, 0))
                val = jnp.where(lower == ((tok & k) == 0), jnp.maximum(val, pv), jnp.minimum(val, pv))
                j //= 2
            k *= 2
        thr = val[cap - 1:cap, :]
        above = aff > thr
        tie = jnp.where(aff == thr, 1.0, 0.0)
        need = cap - jnp.sum(jnp.where(above, 1.0, 0.0), axis=0, keepdims=True)
        sel = jnp.where(above | ((tie > 0.0) & (_rows_prefix_sum(tie) - tie < need)), 1.0, 0.0)
        rank = jnp.where(sel > 0.0, _rows_prefix_sum(sel) - sel, float(cap))
        rank_ref[...] = rank
        rank_t = rank.T
        aff_t = aff.T
        for k in range(bpg):
            rrow_scr[k] = rank_t[k * N_EXPERTS:(k + 1) * N_EXPERTS, :]
            arow_scr[k] = aff_t[k * N_EXPERTS:(k + 1) * N_EXPERTS, :]

    rk = rrow_scr[bb]
    ar = arow_scr[bb]
    slot = lax.broadcasted_iota(jnp.int32, (cap, t), 0).astype(f32)
    for e in range(N_EXPERTS):
        onehot = jnp.where(rk[e:e + 1, :] == slot, 1.0, 0.0)
        oh_scr[e * cap:(e + 1) * cap, :] = onehot.astype(bf16)
        g_ref[e] = jnp.sum(onehot * ar[e:e + 1, :], axis=1, keepdims=True)
    h2 = h2_ref[...]
    epb = max(1, 512 // cap)
    for e0 in range(0, N_EXPERTS, epb):
        rows = jnp.dot(oh_scr[e0 * cap:(e0 + epb) * cap, :], h2, preferred_element_type=f32)
        xg_ref[e0:e0 + epb] = rows.astype(bf16).reshape(epb, cap, D_MODEL)


def _group_lanes(aff, nb, t, bpg):
    a = aff[:, :N_EXPERTS].reshape(nb // bpg, bpg, t, N_EXPERTS)
    a = jnp.transpose(a, (0, 2, 1, 3)).reshape(nb // bpg * t, bpg * N_EXPERTS)
    return jnp.pad(a, ((0, 0), (0, LANE - bpg * N_EXPERTS)))


def _route(aff, h2, nb, t, bpg):
    cap = CAPACITY_FACTOR * t // N_EXPERTS
    return pl.pallas_call(
        functools.partial(_route_body, t, cap, bpg),
        grid=(nb // bpg, bpg),
        in_specs=[pl.BlockSpec((t, LANE), lambda g, k: (g, 0)),
                  pl.BlockSpec((t, D_MODEL), lambda g, k: (g * bpg + k, 0))],
        out_specs=[pl.BlockSpec((N_EXPERTS, cap, D_MODEL), lambda g, k: (0, g * bpg + k, 0)),
                   pl.BlockSpec((N_EXPERTS, cap, 1), lambda g, k: (0, g * bpg + k, 0)),
                   pl.BlockSpec((t, LANE), lambda g, k: (g, 0))],
        out_shape=[jax.ShapeDtypeStruct((N_EXPERTS, nb * cap, D_MODEL), bf16),
                   jax.ShapeDtypeStruct((N_EXPERTS, nb * cap, 1), f32),
                   jax.ShapeDtypeStruct((nb // bpg * t, LANE), f32)],
        scratch_shapes=[pltpu.VMEM((bpg, N_EXPERTS, t), f32), pltpu.VMEM((bpg, N_EXPERTS, t), f32),
                        pltpu.VMEM((N_EXPERTS * cap, t), bf16)],
        compiler_params=_params(("arbitrary", "arbitrary")),
        name=f"route_{t}",
    )(_group_lanes(aff, nb, t, bpg), h2)


def _ffn_body(xp_ref, xs_ref, gp_ref, gs_ref, wg_ref, wu_ref, wd_ref, yp_ref, ys_ref, wg_scr, wu_scr, wd_scr):
    wg_scr[...] = wg_ref[...].astype(bf16)
    wu_scr[...] = wu_ref[...].astype(bf16)
    wd_scr[...] = wd_ref[...].astype(bf16)
    blk = 512
    for x_ref, g_ref, y_ref in ((xp_ref, gp_ref, yp_ref), (xs_ref, gs_ref, ys_ref)):
        def step(i, carry, x_ref=x_ref, g_ref=g_ref, y_ref=y_ref):
            r0 = pl.multiple_of(i * blk, blk)
            x = x_ref[pl.ds(r0, blk), :]
            a = jnp.dot(x, wg_scr[...], preferred_element_type=f32)
            up = jnp.dot(x, wu_scr[...], preferred_element_type=f32)
            mid = (jax.nn.silu(a) * up).astype(bf16)
            y = jnp.dot(mid, wd_scr[...], preferred_element_type=f32)
            y_ref[pl.ds(r0, blk), :] = (y * g_ref[pl.ds(r0, blk), :]).astype(bf16)
            return carry
        lax.fori_loop(0, x_ref.shape[0] // blk, step, 0)


def _ffn(xp, xs, gp, gs, w_gate, w_up, w_down, layer):
    np_, ns = xp.shape[1], xs.shape[1]
    tok = lambda n, w: pl.BlockSpec((None, n, w), lambda e: (e, 0, 0))
    wspec = lambda a, b: pl.BlockSpec((None, None, a, b), lambda e: (layer, e, 0, 0))
    return pl.pallas_call(
        _ffn_body,
        grid=(N_EXPERTS,),
        in_specs=[tok(np_, D_MODEL), tok(ns, D_MODEL), tok(np_, 1), tok(ns, 1),
                  wspec(D_MODEL, EXPERT_FF), wspec(D_MODEL, EXPERT_FF), wspec(EXPERT_FF, D_MODEL)],
        out_specs=[tok(np_, D_MODEL), tok(ns, D_MODEL)],
        out_shape=[jax.ShapeDtypeStruct(xp.shape, bf16), jax.ShapeDtypeStruct(xs.shape, bf16)],
        scratch_shapes=[pltpu.VMEM((D_MODEL, EXPERT_FF), bf16), pltpu.VMEM((D_MODEL, EXPERT_FF), bf16),
                        pltpu.VMEM((EXPERT_FF, D_MODEL), bf16)],
        compiler_params=_params(("arbitrary",)),
        name="expert_ffn",
    )(xp, xs, gp, gs, w_gate, w_up, w_down)


def _combine_body(t, cap, bpg, bps, x_ref, rank_ref, y_ref, mod_ref, o_ref):
    k = N_EXPERTS * cap
    er = lax.broadcasted_iota(jnp.int32, (LANE, k), 0)
    ec = lax.broadcasted_iota(jnp.int32, (LANE, k), 1) // cap
    slot = (lax.broadcasted_iota(jnp.int32, (t, k), 1) % cap).astype(f32)
    rank = rank_ref[...].astype(bf16)
    for sub in range(bps):
        col0 = ((pl.program_id(0) * bps + sub) % bpg) * N_EXPERTS
        expand = jnp.where(er == ec + col0, 1.0, 0.0).astype(bf16)
        rexp = jnp.dot(rank, expand, preferred_element_type=f32)
        pt = jnp.where(rexp == slot, 1.0, 0.0).astype(bf16)
        y = y_ref[:, sub * cap:(sub + 1) * cap, :].reshape(k, D_MODEL)
        moe = jnp.dot(pt, y, preferred_element_type=f32)
        rs = slice(sub * t, (sub + 1) * t)
        o_ref[rs, :] = x_ref[rs, :] + mod_ref[:, 5 * D_MODEL:6 * D_MODEL] * moe


def _combine(x, rank, y, mod3, mod_base, mod_stride, nb, t, bpg):
    cap = CAPACITY_FACTOR * t // N_EXPERTS
    bps = 2 if (mod_stride == 0 and bpg % 2 == 0) else 1
    return pl.pallas_call(
        functools.partial(_combine_body, t, cap, bpg, bps),
        grid=(nb // bps,),
        in_specs=[pl.BlockSpec((bps * t, D_MODEL), lambda b: (b, 0)),
                  pl.BlockSpec((t, LANE), lambda b: (b * bps // bpg, 0)),
                  pl.BlockSpec((N_EXPERTS, bps * cap, D_MODEL), lambda b: (0, b, 0)),
                  pl.BlockSpec((None, 1, 6 * D_MODEL), lambda b: (mod_base + mod_stride * b, 0, 0))],
        out_specs=pl.BlockSpec((bps * t, D_MODEL), lambda b: (b, 0)),
        out_shape=jax.ShapeDtypeStruct(x.shape, f32),
        compiler_params=_params(("arbitrary",)),
        name=f"combine_{t}",
    )(x, rank, y, mod3)


def _pad_row(v, off, width=LANE):
    return jnp.zeros((1, width), f32).at[0, off:off + v.shape[0]].set(v.astype(f32))


def _layer_params(l, a):
    w_in = a["w_in"][l]
    pts, acc = [], 0
    splits = (MIX_W,) * 4 + (4 * HEADS,) + (MIX_W,) * 7 + (SSM_CONV_CH, 2 * HEADS, N_BRANCH * D_MODEL)
    for s in splits:
        pts.append((acc, acc + s))
        acc += s
    seg = lambda i: w_in[:, pts[i][0]:pts[i][1]]
    order = [14, 0, 1, 2, 3, 5, 6, 7, 8, 9, 10, 12, 11]
    w_main = jnp.concatenate([seg(i) for i in order] + [jnp.zeros((D_MODEL, U_COLS - U_USED), f32)],
                             axis=1).astype(bf16)
    w_side = jnp.concatenate([seg(4), seg(13), jnp.zeros((D_MODEL, LANE - 24), f32)], axis=1).astype(bf16)
    tile = lambda v, reps: jnp.tile(v.astype(f32), reps).reshape(1, -1)
    p = dict(
        w_main=w_main, w_side=w_side,
        norm1=a["norm1_w"][l].reshape(1, -1), norm2=a["norm2_w"][l].reshape(1, -1),
        mlstm_gb=_pad_row(a["mlstm_gate_b"][l].reshape(-1), G_MLSTM),
        mlstm_nw=a["mlstm_norm_w"][l].reshape(1, -1),
        na_qn=tile(a["na_qnorm_w"][l], HEADS), na_kn=tile(a["na_knorm_w"][l], HEADS),
        na_rpb=_na_rpb_pad(a["na_rpb"][l]),
        diff_qn=tile(a["diff_qnorm_w"][l], 2 * HEADS), diff_kn=tile(a["diff_knorm_w"][l], 2 * HEADS),
        diff_lam=a["diff_lambda"][l], diff_nw=tile(a["diff_norm_w"][l], HEADS),
        ssm_cw=a["ssm_conv_w"][l], ssm_cb=a["ssm_conv_b"][l].reshape(1, -1),
        ssm_dtb=_pad_row(a["ssm_dt_bias"][l].reshape(-1), G_DT),
        ssm_alog=_pad_row(a["ssm_a_log"][l].reshape(-1), G_DT),
        ssm_d=jnp.repeat(a["ssm_d"][l].astype(f32), HEAD_W).reshape(1, -1),
        ssm_nw=a["ssm_norm_w"][l].reshape(1, -1),
        wb=a["w_branch"][l].astype(bf16), wo=a["w_out"][l].astype(bf16),
        wr=jnp.concatenate([a["w_router"][l], jnp.zeros((D_MODEL, LANE - N_EXPERTS), f32)], axis=1).astype(bf16),
        w_gate=a["w_gate"], w_up=a["w_up"], w_down=a["w_down"],
    )
    return p


def _mixers(u, g, nb, seq, p, l, lam_init, ctx, tables, caches):
    if ctx is None:
        a_out, m_c, m_n, m_m = _mlstm(u, g, nb, seq, p["mlstm_gb"], p["mlstm_nw"], None)
        b_out, na_k, na_v = _na_prompt(u, nb, seq, p["na_qn"], p["na_kn"], l, None if caches is None else caches[0:2])
        c_out, df_k, df_v = _diff_prompt(u, nb, seq, lam_init, p["diff_qn"], p["diff_kn"], p["diff_lam"], p["diff_nw"],
                                         l, None if caches is None else caches[2:4])
        d_out, s_h = _ssd(u, g, nb, seq, p["ssm_cw"], p["ssm_cb"], p["ssm_dtb"], p["ssm_alog"], p["ssm_d"],
                          p["ssm_nw"], None)
        return (a_out, b_out, c_out, d_out), (na_k, na_v, df_k, df_v, m_c, m_n, m_m, s_h)
    (na_ck, na_cv, df_ck, df_cv, st_c, st_n, st_m, st_s) = ctx
    cos, sin = tables
    a_out = _mlstm(u, g, nb, seq, p["mlstm_gb"], p["mlstm_nw"], (st_c[:, l], st_n[:, l], st_m[:, l]))[0]
    b_out = _na_sample(u, nb, l, na_ck, na_cv, p["na_rpb"], p["na_qn"], p["na_kn"])
    c_out = _diff_sample(u, nb, l, lam_init, df_ck, df_cv, cos, sin, p["diff_qn"], p["diff_kn"], p["diff_lam"],
                         p["diff_nw"])
    d_out = _ssd(u, g, nb, seq, p["ssm_cw"], p["ssm_cb"], p["ssm_dtb"], p["ssm_alog"], p["ssm_d"], p["ssm_nw"],
                 st_s[:, l])[0]
    return (a_out, b_out, c_out, d_out), None


def kernel(x_prompt, x_sample, cache_na_k, cache_na_v, cache_diff_k, cache_diff_v, state_mlstm_C, state_mlstm_n, state_mlstm_m, state_ssm, c, c_ctx, norm1_w, norm2_w, w_ada, b_ada, w_in, mlstm_gate_b, mlstm_norm_w, na_qnorm_w, na_knorm_w, na_rpb, diff_qnorm_w, diff_knorm_w, diff_lambda, diff_norm_w, ssm_conv_w, ssm_conv_b, ssm_dt_bias, ssm_a_log, ssm_d, ssm_norm_w, w_branch, w_out, w_router, w_gate, w_up, w_down):
    a = dict(norm1_w=norm1_w, norm2_w=norm2_w, w_in=w_in, mlstm_gate_b=mlstm_gate_b, mlstm_norm_w=mlstm_norm_w,
             na_qnorm_w=na_qnorm_w, na_knorm_w=na_knorm_w, na_rpb=na_rpb, diff_qnorm_w=diff_qnorm_w,
             diff_knorm_w=diff_knorm_w, diff_lambda=diff_lambda, diff_norm_w=diff_norm_w, ssm_conv_w=ssm_conv_w,
             ssm_conv_b=ssm_conv_b, ssm_dt_bias=ssm_dt_bias, ssm_a_log=ssm_a_log, ssm_d=ssm_d,
             ssm_norm_w=ssm_norm_w, w_branch=w_branch, w_out=w_out, w_router=w_router, w_gate=w_gate, w_up=w_up,
             w_down=w_down)
    nb_p, nb_s = x_prompt.shape[0], x_sample.shape[0]
    xp = x_prompt.reshape(nb_p * SEQ, D_MODEL)
    xs = x_sample.reshape(nb_s * DEC_SEQ, D_MODEL)
    cc = jnp.concatenate([c_ctx[None, :], c, jnp.zeros((8 - 1 - nb_s, D_MODEL), f32)], axis=0)
    ctx = (cache_na_k.reshape(nb_s, DEPTH, PAST_LEN, MIX_W), cache_na_v.reshape(nb_s, DEPTH, PAST_LEN, MIX_W),
           cache_diff_k.reshape(nb_s, DEPTH, PAST_LEN, MIX_W), cache_diff_v.reshape(nb_s, DEPTH, PAST_LEN, MIX_W),
           state_mlstm_C, state_mlstm_n, state_mlstm_m, state_ssm)
    tables = _rope_tables()
    new, caches = [], None
    for l in range(DEPTH):
        p = _layer_params(l, a)
        lam_init = 0.8 - 0.6 * math.exp(-0.3 * l)
        mod3 = _modulation(cc, w_ada, b_ada, l).reshape(8, 1, 6 * D_MODEL)
        groups = []
        for (x, nb, seq, base, stride, gctx) in ((xp, nb_p, SEQ, 0, 0, None), (xs, nb_s, DEC_SEQ, 1, 1, ctx)):
            u, g = _inproj(x, mod3, base, stride, seq, p["norm1"], p["w_main"], p["w_side"])
            brs, st = _mixers(u, g, nb, seq, p, l, lam_init, gctx, tables, caches)
            if st is not None:
                new.append(st)
                caches = st[0:4]
            x1, h2, aff = _merge(x, brs, u, mod3, base, stride, seq, p["wb"], p["wo"], p["norm2"], p["wr"])
            bpg = min(nb, LANE // N_EXPERTS)
            xg, gg, rank = _route(aff, h2, nb, seq, bpg)
            groups.append((x1, xg, gg, rank, nb, seq, base, stride, bpg))
        yp, ys = _ffn(groups[0][1], groups[1][1], groups[0][2], groups[1][2], p["w_gate"], p["w_up"], p["w_down"], l)
        outs = []
        for (x1, _, _, rank, nb, seq, base, stride, bpg), y in zip(groups, (yp, ys)):
            outs.append(_combine(x1, rank, y, mod3, base, stride, nb, seq, bpg))
        xp, xs = outs
    stack = lambda i, shape: jnp.stack([new[l][i].reshape(shape) for l in range(DEPTH)], axis=1)
    kv = (nb_p, DEPTH, SEQ, HEADS, HEAD_W)
    return (xp.reshape(x_prompt.shape), xs.reshape(x_sample.shape),
            caches[0].reshape(kv), caches[1].reshape(kv), caches[2].reshape(kv), caches[3].reshape(kv),
            stack(4, (nb_p, 2, HEADS, HEAD_W, HEAD_W)), stack(5, (nb_p, 2, HEADS, HEAD_W)),
            stack(6, (nb_p, 2, HEADS)), stack(7, (nb_p, 2, HEADS, HEAD_W, SSM_STATE)))
```

```python
import functools
import math

import jax
import jax.numpy as jnp
from jax import lax
from jax.experimental import pallas as pl
from jax.experimental.pallas import tpu as pltpu

f32 = jnp.float32
bf16 = jnp.bfloat16

D_MODEL = 1024
BATCH = 32
SEQ = 256
DEPTH = 2
DEC_BATCH = 4
DEC_SEQ = 1024
PAST_LEN = 512
GRID_W = 64
MIX_W = D_MODEL // 4
N_BRANCH = 4
HEADS = 4
HEAD_W = MIX_W // HEADS
NA_WIN_R = 8
NA_WIN_C = 16
DIFF_QK_DIM = HEAD_W // 2
SSM_GROUPS = 2
SSM_STATE = 64
SSM_CONV_K = 5
SSM_CONV_CH = MIX_W + 2 * SSM_GROUPS * SSM_STATE
CHUNK = 64
PROMPT_BPS = 2
SCAN_BPS = 2
SCAN_UNROLL = 8
N_EXPERTS = 16
EXPERT_FF = 1024
CAPACITY_FACTOR = 2
ROPE_BASE = 10000.0
EPS = 1e-6
NEG = -1e30

LANE = 128
VMEM_LIMIT = 56 * 1024 * 1024

COL_GPRE = 0
COL_MLSTM = 4096
COL_NA = 5120
COL_DIFF = 5888
COL_XBC = 6656
COL_Z = 7168
U_USED = 7424
U_COLS = 7680
G_MLSTM = 0
G_DT = 16


def _params(sem):
    return pltpu.CompilerParams(dimension_semantics=sem, vmem_limit_bytes=VMEM_LIMIT)


def _mm(a, b):
    return jnp.dot(a.astype(bf16), b.astype(bf16), preferred_element_type=f32)


def _mm_nt(a, b):
    return lax.dot_general(a.astype(bf16), b.astype(bf16), (((1,), (1,)), ((), ())), preferred_element_type=f32)


def _mm_tn(a, b):
    return lax.dot_general(a.astype(bf16), b.astype(bf16), (((0,), (0,)), ((), ())), preferred_element_type=f32)


def _sigmoid(x):
    return 0.5 * jnp.tanh(0.5 * x) + 0.5


def _seg_sum(x, seg):
    w = x.shape[-1]
    r = lax.broadcasted_iota(jnp.int32, (w, w), 0) // seg
    c = lax.broadcasted_iota(jnp.int32, (w, w), 1) // seg
    ones = jnp.where(r == c, 1.0, 0.0).astype(bf16)
    hi = x.astype(bf16)
    lo = (x - hi.astype(f32)).astype(bf16)
    return jnp.dot(hi, ones, preferred_element_type=f32) + jnp.dot(lo, ones, preferred_element_type=f32)


def _seg_rms(x, seg, w_row):
    ms = _seg_sum(x * x, seg) * (1.0 / seg)
    return x * lax.rsqrt(ms + EPS) * w_row


def _mod_body(c_ref, w_ref, b_ref, o_ref):
    o_ref[...] = _mm(jax.nn.silu(c_ref[...]), w_ref[...]) + b_ref[...]


def _modulation(cc, w_ada, b_ada, layer):
    tn = 1536
    n = w_ada.shape[2]
    return pl.pallas_call(
        _mod_body,
        grid=(n // tn,),
        in_specs=[pl.BlockSpec((8, D_MODEL), lambda j: (0, 0)),
                  pl.BlockSpec((None, D_MODEL, tn), lambda j: (layer, 0, j)),
                  pl.BlockSpec((None, 1, tn), lambda j: (layer, 0, j))],
        out_specs=pl.BlockSpec((8, tn), lambda j: (0, j)),
        out_shape=jax.ShapeDtypeStruct((8, n), f32),
        compiler_params=_params(("arbitrary",)),
        name="modulation",
    )(cc, w_ada, b_ada.reshape(b_ada.shape[0], 1, n))


def _inproj_body(ncol, x_ref, mod_ref, nw_ref, w_ref, wg_ref, u_ref, g_ref):
    x = x_ref[...]
    y = x * lax.rsqrt(jnp.mean(x * x, axis=-1, keepdims=True) + EPS) * nw_ref[...]
    hb = (y * (1.0 + mod_ref[:, D_MODEL:2 * D_MODEL]) + mod_ref[:, 0:D_MODEL]).astype(bf16)
    g_ref[...] = jnp.dot(hb, wg_ref[...], preferred_element_type=f32)
    cw = U_COLS // ncol
    for c in range(ncol):
        u_ref[:, c * cw:(c + 1) * cw] = jnp.dot(hb, w_ref[:, c * cw:(c + 1) * cw],
                                                preferred_element_type=f32).astype(u_ref.dtype)


def _inproj(x, mod3, mod_base, mod_stride, seq, nw_row, w_main, w_side):
    n = x.shape[0]
    tm = 512
    tps = max(seq // tm, 1)
    resident = lambda shape: pl.BlockSpec(shape, lambda i: (0, 0), pipeline_mode=pl.Buffered(1))
    return pl.pallas_call(
        functools.partial(_inproj_body, 3),
        grid=(n // tm,),
        in_specs=[pl.BlockSpec((tm, D_MODEL), lambda i: (i, 0)),
                  pl.BlockSpec((None, 1, 6 * D_MODEL), lambda i: (mod_base + mod_stride * (i // tps), 0, 0)),
                  pl.BlockSpec((1, D_MODEL), lambda i: (0, 0)),
                  resident((D_MODEL, U_COLS)), resident((D_MODEL, LANE))],
        out_specs=[pl.BlockSpec((tm, U_COLS), lambda i: (i, 0)),
                   pl.BlockSpec((tm, LANE), lambda i: (i, 0))],
        out_shape=[jax.ShapeDtypeStruct((n, U_COLS), bf16), jax.ShapeDtypeStruct((n, LANE), f32)],
        compiler_params=_params(("arbitrary",)),
        name="inproj",
    )(x, mod3, nw_row, w_main, w_side)


def _split3(x):
    hi = x.astype(bf16)
    r = x - hi.astype(f32)
    mid = r.astype(bf16)
    lo = (r - mid.astype(f32)).astype(bf16)
    return hi, mid, lo


def _select_cols(x, onehot):
    return jnp.dot(jnp.concatenate(_split3(x), axis=1), jnp.concatenate([onehot] * 3, axis=0),
                   preferred_element_type=f32)


def _select_rows(onehot, x):
    return jnp.dot(jnp.concatenate([onehot] * 3, axis=1), jnp.concatenate(_split3(x), axis=0),
                   preferred_element_type=f32)


def _chunk_scan(x, op, identity, reverse):
    n = x.shape[0]
    pos = lax.broadcasted_iota(jnp.int32, x.shape, 0) % CHUNK
    k = 1
    while k < CHUNK:
        if reverse:
            shifted, ok = pltpu.roll(x, n - k, 0), pos < CHUNK - k
        else:
            shifted, ok = pltpu.roll(x, k, 0), pos >= k
        x = op(x, jnp.where(ok, shifted, identity))
        k *= 2
    return x


def _dir_scan(x, op, identity, bwd_col):
    return jnp.where(bwd_col, _chunk_scan(x, op, identity, True), _chunk_scan(x, op, identity, False))


def _head_expand(col0):
    c = lax.broadcasted_iota(jnp.int32, (LANE, MIX_W), 0)
    h = lax.broadcasted_iota(jnp.int32, (LANE, MIX_W), 1) // HEAD_W
    return jnp.where(c == col0 + h, 1.0, 0.0).astype(bf16)


def _chunk_rows(x, seq):
    nrow = max(seq // CHUNK, 8)
    t = lax.broadcasted_iota(jnp.int32, (seq, MIX_W), 0) % CHUNK
    s_ = lax.broadcasted_iota(jnp.int32, (seq, MIX_W), 1) % HEAD_W
    sel = lax.broadcasted_iota(jnp.int32, (nrow, seq), 1) // CHUNK == lax.broadcasted_iota(jnp.int32, (nrow, seq), 0)
    return _select_rows(jnp.where(sel, 1.0, 0.0).astype(bf16), jnp.where(t == s_, x, 0.0))


def _block_diag(x, reps, keep):
    return jnp.concatenate([x] * reps, axis=0) * keep


def _head_mask(rows, cols, rseg, cseg):
    return (lax.broadcasted_iota(jnp.int32, (rows, cols), 0) // rseg) == (lax.broadcasted_iota(jnp.int32, (rows, cols), 1) // cseg)


def _mlstm_body(seq, bps, has_state, *refs):
    if has_state:
        (qkvo_ref, g_ref, gb_ref, nw_ref, c0_ref, n0_ref, m0_ref,
         out_ref, cs_ref, ns_ref, ms_ref, bx_scr, mx_scr, vx_scr, vrow_scr, cbd_scr, hf_scr, hb_scr) = refs
    else:
        (qkvo_ref, g_ref, gb_ref, nw_ref,
         out_ref, cs_ref, ns_ref, ms_ref, bx_scr, mx_scr, vx_scr, vrow_scr, cbd_scr, hf_scr, hb_scr) = refs
    nc = seq // CHUNK
    tot = bps * seq
    g = g_ref[...] + gb_ref[...]
    lane = lax.broadcasted_iota(jnp.int32, g.shape, 1)
    bwd_col = (lane % 16) >= 8
    bsum = _dir_scan(jax.nn.log_sigmoid(g), jnp.add, 0.0, bwd_col)
    b_i = pltpu.roll(bsum, LANE - HEADS, 1)
    vcol = g - b_i
    mcol = b_i + _dir_scan(vcol, jnp.maximum, NEG, bwd_col)
    for d in range(2):
        e = _head_expand(G_MLSTM + d * 8)
        bx_scr[d] = _select_cols(b_i, e)
        mx_scr[d] = _select_cols(mcol, e)
        vx = _select_cols(vcol, e)
        vx_scr[d] = vx
        vrow = _chunk_rows(vx, tot)
        for c in range(bps * nc):
            vrow_scr[d, c] = vrow[c:c + 1, :]

    grp = lax.broadcasted_iota(jnp.int32, (1, MIX_W), 1) // HEAD_W
    cbd_scr[...] = jnp.zeros(cbd_scr.shape, f32)
    init = []
    for sub in range(bps):
        for d in range(2):
            if has_state:
                for h in range(HEADS):
                    cbd_scr[2 * sub + d, h * HEAD_W:(h + 1) * HEAD_W, h * HEAD_W:(h + 1) * HEAD_W] = c0_ref[sub, d, h]
                init.append(jnp.concatenate([n0_ref[sub, d, h:h + 1, :] for h in range(HEADS)], axis=1))
                m_row = jnp.zeros((1, MIX_W), f32)
                for h in range(HEADS):
                    m_row = jnp.where(grp == h, m0_ref[sub, d:d + 1, h:h + 1], m_row)
                init.append(m_row)
            else:
                init += [jnp.zeros((1, MIX_W), f32), jnp.zeros((1, MIX_W), f32)]

    li = lax.broadcasted_iota(jnp.int32, (CHUNK, MIX_W), 0)
    si = lax.broadcasted_iota(jnp.int32, (CHUNK, MIX_W), 1) % HEAD_W
    valid = (si <= li, si >= li)
    bd = _head_mask(MIX_W, MIX_W, HEAD_W, HEAD_W)
    bd_f32 = jnp.where(bd, 1.0, 0.0)
    ones_bd = bd_f32.astype(bf16)

    def step(c, carry):
        new = []
        for sd in range(2 * bps):
            sub, d = sd // 2, sd % 2
            n_row, m_row = carry[2 * sd], carry[2 * sd + 1]
            cidx = sub * nc + (c if d == 0 else nc - 1 - c)
            r0 = pl.multiple_of(cidx * CHUNK, CHUNK)
            last = CHUNK - 1 if d == 0 else 0
            q = qkvo_ref[pl.ds(r0, CHUNK), 0:MIX_W].astype(f32)
            k = qkvo_ref[pl.ds(r0, CHUNK), MIX_W:2 * MIX_W].astype(f32) * (HEAD_W ** -0.5)
            vb = qkvo_ref[pl.ds(r0, CHUNK), 2 * MIX_W:3 * MIX_W].astype(bf16)
            qb = q.astype(bf16)
            bx = bx_scr[d, pl.ds(r0, CHUNK), :]
            mx = mx_scr[d, pl.ds(r0, CHUNK), :]
            vx = vx_scr[d, pl.ds(r0, CHUNK), :]
            p = jnp.exp(jnp.where(valid[d], bx + vrow_scr[d, cidx] - mx, NEG))
            smat = _mm_nt(qb, _block_diag(k.astype(bf16), HEADS, ones_bd)) * p
            intra = _mm(smat, _block_diag(vb, HEADS, ones_bd))
            dsum = _mm(smat, ones_bd)
            gg = bx + m_row
            mt = jnp.maximum(gg, mx)
            a = jnp.exp(mx - mt)
            w_prev = jnp.exp(gg - mt)
            inter = _mm(qb, cbd_scr[sd])
            dint = _mm(q * n_row, ones_bd)
            num = intra * a + inter * w_prev
            den = dsum * a + dint * w_prev
            h_scr = hf_scr if d == 0 else hb_scr
            h_scr[pl.ds(r0, CHUNK), :] = num / jnp.maximum(jnp.abs(den), jnp.exp(-mt))
            bl = bx[last:last + 1, :]
            mloc = mx[last:last + 1, :]
            m_new = jnp.maximum(bl + m_row, mloc)
            dec = jnp.exp(bl + m_row - m_new)
            fac = jnp.exp(mloc - m_new)
            kw = k * jnp.exp(bl + vx - mloc)
            kv = _mm_tn(kw, vb) * bd_f32
            cbd_scr[sd] = cbd_scr[sd] * dec + kv * fac
            new += [n_row * dec + jnp.sum(kw, axis=0, keepdims=True) * fac, m_new]
        return tuple(new)

    fin_state = lax.fori_loop(0, nc, step, tuple(init), unroll=min(SCAN_UNROLL, nc))
    for sd in range(2 * bps):
        sub, d = sd // 2, sd % 2
        n_row, m_row = fin_state[2 * sd], fin_state[2 * sd + 1]
        for h in range(HEADS):
            hl = slice(h * HEAD_W, (h + 1) * HEAD_W)
            cs_ref[sub, d, h] = cbd_scr[sd, hl, hl]
            ns_ref[sub, d, h:h + 1, :] = n_row[:, hl]
            ms_ref[sub, d:d + 1, h:h + 1] = m_row[:, h * HEAD_W:h * HEAD_W + 1]

    blk = 256

    def fin(i, carry):
        r0 = pl.multiple_of(i * blk, blk)
        hs = hf_scr[pl.ds(r0, blk), :] + hb_scr[pl.ds(r0, blk), :]
        hn = _seg_rms(hs, HEAD_W, nw_ref[...])
        o = qkvo_ref[pl.ds(r0, blk), 3 * MIX_W:4 * MIX_W].astype(f32)
        out_ref[pl.ds(r0, blk), :] = (hn * jax.nn.sigmoid(o)).astype(out_ref.dtype)
        return carry

    lax.fori_loop(0, tot // blk, fin, 0)


def _mlstm(u, g, nb, seq, gb_row, nw_row, state):
    has_state = state is not None
    bps = 1 if has_state else SCAN_BPS
    nc = seq // CHUNK
    rows = bps * seq
    st = lambda tail: pl.BlockSpec((bps, 2, HEADS) + tail, lambda b: (b, 0, 0) + (0,) * len(tail))
    in_specs = [pl.BlockSpec((rows, 4 * MIX_W), lambda b: (b, COL_MLSTM // (4 * MIX_W))),
                pl.BlockSpec((rows, LANE), lambda b: (b, 0)),
                pl.BlockSpec((1, LANE), lambda b: (0, 0)),
                pl.BlockSpec((1, MIX_W), lambda b: (0, 0))]
    args = [u, g, gb_row, nw_row]
    if has_state:
        in_specs += [st((HEAD_W, HEAD_W)), st((HEAD_W,)), st(())]
        args += list(state)
    return pl.pallas_call(
        functools.partial(_mlstm_body, seq, bps, has_state),
        grid=(nb // bps,),
        in_specs=in_specs,
        out_specs=[pl.BlockSpec((rows, MIX_W), lambda b: (b, 0)), st((HEAD_W, HEAD_W)), st((HEAD_W,)), st(())],
        out_shape=[jax.ShapeDtypeStruct((nb * seq, MIX_W), bf16),
                   jax.ShapeDtypeStruct((nb, 2, HEADS, HEAD_W, HEAD_W), f32),
                   jax.ShapeDtypeStruct((nb, 2, HEADS, HEAD_W), f32),
                   jax.ShapeDtypeStruct((nb, 2, HEADS), f32)],
        scratch_shapes=[pltpu.VMEM((2, rows, MIX_W), f32), pltpu.VMEM((2, rows, MIX_W), f32),
                        pltpu.VMEM((2, rows, MIX_W), f32), pltpu.VMEM((2, bps * nc, 1, MIX_W), f32),
                        pltpu.VMEM((2 * bps, MIX_W, MIX_W), f32),
                        pltpu.VMEM((rows, MIX_W), f32), pltpu.VMEM((rows, MIX_W), f32)],
        compiler_params=_params(("arbitrary",)),
        name="mlstm_state" if has_state else "mlstm",
    )(*args)


def _ssd_body(seq, bps, has_state, *refs):
    if has_state:
        (z_ref, xbc_ref, g_ref, cw_ref, cb_ref, dtb_ref, alog_ref, dskip_ref, nw_ref, h0_ref,
         out_ref, hs_ref, xpad_scr, xc_scr, ax_scr, dx_scr, arow_scr, ht_scr, yf_scr, yb_scr) = refs
    else:
        (z_ref, xbc_ref, g_ref, cw_ref, cb_ref, dtb_ref, alog_ref, dskip_ref, nw_ref,
         out_ref, hs_ref, xpad_scr, xc_scr, ax_scr, dx_scr, arow_scr, ht_scr, yf_scr, yb_scr) = refs
    nc = seq // CHUNK
    tot = bps * seq
    pad = 8
    blk = 256
    nblk = tot // blk
    cblk = 128
    for sub in range(bps):
        base = sub * (seq + 2 * pad)
        xpad_scr[base:base + pad, :] = jnp.zeros((pad, SSM_CONV_CH), f32)
        xpad_scr[base + pad + seq:base + 2 * pad + seq, :] = jnp.zeros((pad, SSM_CONV_CH), f32)
        xpad_scr[base + pad:base + pad + seq, :] = xbc_ref[sub * seq:(sub + 1) * seq, :].astype(f32)
        for i in range(seq // cblk):
            r0 = i * cblk
            acc = jnp.zeros((cblk, SSM_CONV_CH), f32) + cb_ref[...]
            for kk in range(SSM_CONV_K):
                off = base + r0 + pad - SSM_CONV_K // 2 + kk
                acc = acc + xpad_scr[off:off + cblk, :] * cw_ref[kk:kk + 1, :]
            xc_scr[sub * seq + r0:sub * seq + r0 + cblk, :] = jax.nn.silu(acc)

    dt = jax.nn.softplus(g_ref[...] + dtb_ref[...])
    lane = lax.broadcasted_iota(jnp.int32, dt.shape, 1)
    acum = _dir_scan(dt * (-jnp.exp(alog_ref[...])), jnp.add, 0.0, (lane % 8) >= HEADS)
    for d in range(2):
        e = _head_expand(G_DT + d * HEADS)
        ax = _select_cols(acum, e)
        ax_scr[d] = ax
        dx_scr[d] = _select_cols(dt, e)
        arow = _chunk_rows(ax, tot)
        for c in range(bps * nc):
            arow_scr[d, c] = arow[c:c + 1, :]

    gn = SSM_GROUPS * SSM_STATE
    rep = HEADS // SSM_GROUPS
    ht_scr[...] = jnp.zeros(ht_scr.shape, f32)
    if has_state:
        for sd in range(2 * bps):
            for h in range(HEADS):
                g0 = (h // rep) * SSM_STATE
                ht_scr[sd, g0:g0 + SSM_STATE, h * HEAD_W:(h + 1) * HEAD_W] = h0_ref[sd // 2, sd % 2, h].T
    li = lax.broadcasted_iota(jnp.int32, (CHUNK, MIX_W), 0)
    si = lax.broadcasted_iota(jnp.int32, (CHUNK, MIX_W), 1) % HEAD_W
    valid = (si <= li, si >= li)
    state_keep = jnp.where(_head_mask(gn, MIX_W, SSM_STATE, rep * HEAD_W), 1.0, 0.0)
    key_keep = jnp.where(_head_mask(HEADS * CHUNK, gn, rep * CHUNK, SSM_STATE), 1.0, 0.0).astype(bf16)
    head_keep = jnp.where(_head_mask(MIX_W, MIX_W, HEAD_W, HEAD_W), 1.0, 0.0).astype(bf16)

    def step(c, carry):
        for sd in range(2 * bps):
            sub, d = sd // 2, sd % 2
            cidx = sub * nc + (c if d == 0 else nc - 1 - c)
            r0 = pl.multiple_of(cidx * CHUNK, CHUNK)
            last = CHUNK - 1 if d == 0 else 0
            xs = xc_scr[pl.ds(r0, CHUNK), 0:MIX_W]
            bm = xc_scr[pl.ds(r0, CHUNK), MIX_W:MIX_W + gn].astype(bf16)
            cm = xc_scr[pl.ds(r0, CHUNK), MIX_W + gn:MIX_W + 2 * gn].astype(bf16)
            ax = ax_scr[d, pl.ds(r0, CHUNK), :]
            decay = jnp.exp(jnp.where(valid[d], ax - arow_scr[d, cidx], NEG))
            scores = _mm_nt(cm, _block_diag(bm, HEADS, key_keep)) * decay
            xdt = xs * dx_scr[d, pl.ds(r0, CHUNK), :]
            y = _mm(scores, _block_diag(xdt.astype(bf16), HEADS, head_keep)) + _mm(cm, ht_scr[sd]) * jnp.exp(ax)
            y_scr = yf_scr if d == 0 else yb_scr
            y_scr[pl.ds(r0, CHUNK), :] = y
            al = ax[last:last + 1, :]
            ht_scr[sd] = ht_scr[sd] * jnp.exp(al) + _mm_tn(bm, xdt * jnp.exp(al - ax)) * state_keep
        return carry

    lax.fori_loop(0, nc, step, 0, unroll=min(SCAN_UNROLL, nc))
    for sd in range(2 * bps):
        for h in range(HEADS):
            g0 = (h // rep) * SSM_STATE
            hs_ref[sd // 2, sd % 2, h] = ht_scr[sd, g0:g0 + SSM_STATE, h * HEAD_W:(h + 1) * HEAD_W].T

    def fin(i, carry):
        r0 = pl.multiple_of(i * blk, blk)
        y = yf_scr[pl.ds(r0, blk), :] + yb_scr[pl.ds(r0, blk), :] + dskip_ref[...] * xc_scr[pl.ds(r0, blk), 0:MIX_W]
        y = y * jax.nn.silu(z_ref[pl.ds(r0, blk), :].astype(f32))
        y = y * lax.rsqrt(jnp.mean(y * y, axis=-1, keepdims=True) + EPS) * nw_ref[...]
        out_ref[pl.ds(r0, blk), :] = y.astype(out_ref.dtype)
        return carry

    lax.fori_loop(0, nblk, fin, 0)


def _ssd(u, g, nb, seq, cw, cb_row, dtb_row, alog_row, dskip_row, nw_row, state):
    has_state = state is not None
    bps = 1 if has_state else SCAN_BPS
    nc = seq // CHUNK
    rows = bps * seq
    const = lambda shape: pl.BlockSpec(shape, lambda b: (0,) * len(shape))
    st = pl.BlockSpec((bps, 2, HEADS, HEAD_W, SSM_STATE), lambda b: (b, 0, 0, 0, 0))
    in_specs = [pl.BlockSpec((rows, MIX_W), lambda b: (b, COL_Z // MIX_W)),
                pl.BlockSpec((rows, SSM_CONV_CH), lambda b: (b, COL_XBC // SSM_CONV_CH)),
                pl.BlockSpec((rows, LANE), lambda b: (b, 0)),
                const((SSM_CONV_K, SSM_CONV_CH)), const((1, SSM_CONV_CH)), const((1, LANE)), const((1, LANE)),
                const((1, MIX_W)), const((1, MIX_W))]
    args = [u, u, g, cw, cb_row, dtb_row, alog_row, dskip_row, nw_row]
    if has_state:
        in_specs.append(st)
        args.append(state)
    return pl.pallas_call(
        functools.partial(_ssd_body, seq, bps, has_state),
        grid=(nb // bps,),
        in_specs=in_specs,
        out_specs=[pl.BlockSpec((rows, MIX_W), lambda b: (b, 0)), st],
        out_shape=[jax.ShapeDtypeStruct((nb * seq, MIX_W), bf16),
                   jax.ShapeDtypeStruct((nb, 2, HEADS, HEAD_W, SSM_STATE), f32)],
        scratch_shapes=[pltpu.VMEM((bps * (seq + 16), SSM_CONV_CH), f32), pltpu.VMEM((rows, SSM_CONV_CH), f32),
                        pltpu.VMEM((2, rows, MIX_W), f32), pltpu.VMEM((2, rows, MIX_W), f32),
                        pltpu.VMEM((2, bps * nc, 1, MIX_W), f32),
                        pltpu.VMEM((2 * bps, SSM_GROUPS * SSM_STATE, MIX_W), f32),
                        pltpu.VMEM((rows, MIX_W), f32), pltpu.VMEM((rows, MIX_W), f32)],
        compiler_params=_params(("arbitrary",)),
        name="ssd_state" if has_state else "ssd",
    )(*args)


def _with_ones(v):
    vb = v.astype(bf16)
    ones = jnp.ones((v.shape[0], HEAD_W), bf16)
    return jnp.concatenate([x for h in range(HEADS) for x in (vb[:, h * HEAD_W:(h + 1) * HEAD_W], ones)], axis=1)


def _softmax_av(scores, vexts):
    m = scores[0].max(axis=1, keepdims=True)
    for s in scores[1:]:
        m = jnp.maximum(m, s.max(axis=1, keepdims=True))
    r = None
    for s, vx in zip(scores, vexts):
        t = jnp.dot(jnp.exp(s - m).astype(bf16), vx, preferred_element_type=f32)
        r = t if r is None else r + t
    return r[:, :HEAD_W] / r[:, HEAD_W:]


def _store_layer(ref, sub, layer, whole, val):
    if whole:
        for l2 in range(DEPTH):
            ref[sub, l2] = val if l2 == layer else jnp.zeros_like(val)
    else:
        ref[sub] = val


def _layer_out(nb, bps, tail, layer, prev):
    zeros = (0,) * len(tail)
    shape = jax.ShapeDtypeStruct((nb, DEPTH) + tail, f32)
    if prev is None:
        return shape, pl.BlockSpec((bps, DEPTH) + tail, lambda b: (b, 0) + zeros), True
    return shape, pl.BlockSpec((bps, None) + tail, lambda b: (b, layer) + zeros), False


def _na_prompt_body(seq, layer, whole, q_ref, k_ref, v_ref, qn_ref, kn_ref, *rest):
    o_ref, ko_ref, vo_ref = rest[-3:]
    scale = HEAD_W ** -0.5
    qb = (_seg_rms(q_ref[...].astype(f32), HEAD_W, qn_ref[...]) * scale).astype(bf16)
    kn = _seg_rms(k_ref[...].astype(f32), HEAD_W, kn_ref[...])
    v = v_ref[...].astype(f32)
    for s0 in range(0, q_ref.shape[0], seq):
        rs = slice(s0, s0 + seq)
        _store_layer(ko_ref, s0 // seq, layer, whole, kn[rs])
        _store_layer(vo_ref, s0 // seq, layer, whole, v[rs])
        kt = kn[rs].T.astype(bf16)
        vx = _with_ones(v[rs])
        for h in range(HEADS):
            hl = slice(h * HEAD_W, (h + 1) * HEAD_W)
            s = jnp.dot(qb[rs, hl], kt[hl, :], preferred_element_type=f32)
            o_ref[rs, hl] = _softmax_av([s], [vx[:, 2 * h * HEAD_W:2 * (h + 1) * HEAD_W]]).astype(o_ref.dtype)


def _na_prompt(u, nb, seq, qn_row, kn_row, layer, prev):
    cb = COL_NA // MIX_W
    rows = PROMPT_BPS * seq
    blk = lambda j: pl.BlockSpec((rows, MIX_W), lambda b: (b, j))
    const = pl.BlockSpec((1, MIX_W), lambda b: (0, 0))
    shape, spec, whole = _layer_out(nb, PROMPT_BPS, (seq, MIX_W), layer, prev)
    extra = [] if prev is None else list(prev)
    return pl.pallas_call(
        functools.partial(_na_prompt_body, seq, layer, whole),
        grid=(nb // PROMPT_BPS,),
        in_specs=[blk(cb), blk(cb + 1), blk(cb + 2), const, const] + [pl.BlockSpec(memory_space=pl.ANY)] * len(extra),
        out_specs=[blk(0), spec, spec],
        out_shape=[jax.ShapeDtypeStruct((nb * seq, MIX_W), bf16), shape, shape],
        input_output_aliases={} if prev is None else {5: 1, 6: 2},
        compiler_params=_params(("arbitrary",)),
        name="na_prompt",
    )(u, u, u, qn_row, kn_row, *extra)


def _na_sample_body(qblk, q_ref, k_ref, v_ref, kc_ref, vc_ref, rpb_ref, qn_ref, kn_ref, o_ref,
                    kt_scr, vx_scr, bias_scr):
    b = pl.program_id(0)
    i = pl.program_id(1)
    seq = k_ref.shape[0]
    rows = seq // GRID_W

    @pl.when((b == 0) & (i == 0))
    def _():
        j = lax.broadcasted_iota(jnp.int32, (GRID_W, LANE), 0)
        lane = lax.broadcasted_iota(jnp.int32, (GRID_W, LANE), 1)
        j2 = lane % GRID_W
        cs = jnp.clip(j - NA_WIN_C // 2, 0, GRID_W - NA_WIN_C)
        ok = (j2 >= cs) & (j2 < cs + NA_WIN_C)
        left = lane < GRID_W
        neg = jnp.full((GRID_W, LANE), NEG, f32)
        for h in range(HEADS):
            lo, hi = [], []
            for dr in range(2 * NA_WIN_R - 1):
                row = jnp.broadcast_to(rpb_ref[h, dr:dr + 1, :], (GRID_W, LANE))
                lo.append(jnp.where(ok, pltpu.roll(row, LANE - (NA_WIN_C - 1), 1, stride=1, stride_axis=0), NEG))
                hi.append(jnp.where(ok, pltpu.roll(row, GRID_W - (NA_WIN_C - 1), 1, stride=1, stride_axis=0), NEG))
            for qi in range(rows):
                rs = min(max(qi - NA_WIN_R // 2, 0), rows - NA_WIN_R)
                for pair in range(rows // 2):
                    k_even, k_odd = 2 * pair, 2 * pair + 1
                    a = lo[k_even - qi + NA_WIN_R - 1] if rs <= k_even < rs + NA_WIN_R else neg
                    c = hi[k_odd - qi + NA_WIN_R - 1] if rs <= k_odd < rs + NA_WIN_R else neg
                    bias_scr[h, qi * GRID_W:(qi + 1) * GRID_W, pair * LANE:(pair + 1) * LANE] = jnp.where(left, a, c)

    @pl.when(i == 0)
    def _():
        kn = _seg_rms(k_ref[...].astype(f32), HEAD_W, kn_ref[...])
        kt_scr[:, 0:seq] = kn.T.astype(bf16)
        kt_scr[:, seq:] = kc_ref[...].T.astype(bf16)
        vx_scr[0:seq, :] = _with_ones(v_ref[...])
        vx_scr[seq:, :] = _with_ones(vc_ref[...])

    r0 = pl.multiple_of(i * qblk, qblk)
    qb = (_seg_rms(q_ref[...].astype(f32), HEAD_W, qn_ref[...]) * (HEAD_W ** -0.5)).astype(bf16)
    for h in range(HEADS):
        hl = slice(h * HEAD_W, (h + 1) * HEAD_W)
        xl = slice(2 * h * HEAD_W, 2 * (h + 1) * HEAD_W)
        s_own = jnp.dot(qb[:, hl], kt_scr[hl, 0:seq], preferred_element_type=f32) + bias_scr[h, pl.ds(r0, qblk), :]
        s_ctx = jnp.dot(qb[:, hl], kt_scr[hl, seq:], preferred_element_type=f32)
        o = _softmax_av([s_own, s_ctx], [vx_scr[0:seq, xl], vx_scr[seq:, xl]])
        o_ref[:, hl] = o.astype(o_ref.dtype)


def _na_sample(u, nb, layer, cache_k, cache_v, rpb_pad, qn_row, kn_row):
    seq = DEC_SEQ
    qblk = 256
    nq = seq // qblk
    cb = COL_NA // MIX_W
    full = lambda j: pl.BlockSpec((seq, MIX_W), lambda b, i: (b, j))
    ctx = pl.BlockSpec((None, None, PAST_LEN, MIX_W), lambda b, i: (b, layer, 0, 0))
    const = pl.BlockSpec((1, MIX_W), lambda b, i: (0, 0))
    return pl.pallas_call(
        functools.partial(_na_sample_body, qblk),
        grid=(nb, nq),
        in_specs=[pl.BlockSpec((qblk, MIX_W), lambda b, i: (b * nq + i, cb)), full(cb + 1), full(cb + 2),
                  ctx, ctx,
                  pl.BlockSpec((HEADS, 2 * NA_WIN_R, LANE), lambda b, i: (0, 0, 0)),
                  const, const],
        out_specs=pl.BlockSpec((qblk, MIX_W), lambda b, i: (b * nq + i, 0)),
        out_shape=jax.ShapeDtypeStruct((nb * seq, MIX_W), bf16),
        scratch_shapes=[pltpu.VMEM((MIX_W, seq + PAST_LEN), bf16), pltpu.VMEM((seq + PAST_LEN, 2 * MIX_W), bf16),
                        pltpu.VMEM((HEADS, seq, seq), f32)],
        compiler_params=_params(("arbitrary", "arbitrary")),
        name="na_sample",
    )(u, u, u, cache_k, cache_v, rpb_pad, qn_row, kn_row)


def _na_rpb_pad(rpb):
    c = rpb.shape[2]
    return jnp.pad(rpb.astype(f32), ((0, 0), (0, 1), (0, LANE - c)))


def _diff_lambda(lam_ref, lam_init):
    lv = lam_ref[...]
    s1 = jnp.sum(lv[0:1, :] * lv[1:2, :], axis=1, keepdims=True)
    s2 = jnp.sum(lv[2:3, :] * lv[3:4, :], axis=1, keepdims=True)
    return jnp.exp(s1) - jnp.exp(s2) + lam_init


def _diff_heads(q, kt, vext, lam):
    outs = []
    for h in range(HEADS):
        o = []
        for m in range(2):
            c0 = h * HEAD_W + m * DIFF_QK_DIM
            s = jnp.dot(q[:, c0:c0 + DIFF_QK_DIM], kt[c0:c0 + DIFF_QK_DIM, :], preferred_element_type=f32)
            o.append(_softmax_av([s], [vext[:, 2 * h * HEAD_W:2 * (h + 1) * HEAD_W]]))
        outs.append(o[0] - lam * o[1])
    return jnp.concatenate(outs, axis=1)


def _diff_prompt_body(lam_init, seq, layer, whole, q_ref, k_ref, v_ref, qn_ref, kn_ref, lam_ref, nw_ref, *rest):
    o_ref, ko_ref, vo_ref = rest[-3:]
    lam = _diff_lambda(lam_ref, lam_init)
    qb = (_seg_rms(q_ref[...].astype(f32), DIFF_QK_DIM, qn_ref[...]) * (DIFF_QK_DIM ** -0.5)).astype(bf16)
    kn = _seg_rms(k_ref[...].astype(f32), DIFF_QK_DIM, kn_ref[...])
    v = v_ref[...].astype(f32)
    for s0 in range(0, q_ref.shape[0], seq):
        rs = slice(s0, s0 + seq)
        _store_layer(ko_ref, s0 // seq, layer, whole, kn[rs])
        _store_layer(vo_ref, s0 // seq, layer, whole, v[rs])
        o = _diff_heads(qb[rs], kn[rs].T.astype(bf16), _with_ones(v[rs]), lam)
        o_ref[rs, :] = (_seg_rms(o, HEAD_W, nw_ref[...]) * (1.0 - lam_init)).astype(o_ref.dtype)


def _diff_prompt(u, nb, seq, lam_init, qn_row, kn_row, lam_vecs, nw_row, layer, prev):
    cb = COL_DIFF // MIX_W
    rows = PROMPT_BPS * seq
    blk = lambda j: pl.BlockSpec((rows, MIX_W), lambda b: (b, j))
    const = pl.BlockSpec((1, MIX_W), lambda b: (0, 0))
    shape, spec, whole = _layer_out(nb, PROMPT_BPS, (seq, MIX_W), layer, prev)
    extra = [] if prev is None else list(prev)
    return pl.pallas_call(
        functools.partial(_diff_prompt_body, lam_init, seq, layer, whole),
        grid=(nb // PROMPT_BPS,),
        in_specs=[blk(cb), blk(cb + 1), blk(cb + 2), const, const,
                  pl.BlockSpec((4, DIFF_QK_DIM), lambda b: (0, 0)), const] + [pl.BlockSpec(memory_space=pl.ANY)] * len(extra),
        out_specs=[blk(0), spec, spec],
        out_shape=[jax.ShapeDtypeStruct((nb * seq, MIX_W), bf16), shape, shape],
        input_output_aliases={} if prev is None else {7: 1, 8: 2},
        compiler_params=_params(("arbitrary",)),
        name="diff_prompt",
    )(u, u, u, qn_row, kn_row, lam_vecs, nw_row, *extra)


def _rope(x, cos, sin):
    w = x.shape[-1]
    up = pltpu.roll(x, w - 8, 1)
    dn = pltpu.roll(x, 8, 1)
    first = (lax.broadcasted_iota(jnp.int32, x.shape, 1) % 16) < 8
    return x * cos + jnp.where(first, up, dn) * sin


def _diff_sample_body(lam_init, qblk, q_ref, k_ref, v_ref, kc_ref, vc_ref, cos_ref, sin_ref, qn_ref, kn_ref,
                      lam_ref, nw_ref, o_ref, kt_scr, vx_scr):
    i = pl.program_id(1)
    seq = k_ref.shape[0]

    @pl.when(i == 0)
    def _():
        kn = _seg_rms(k_ref[...].astype(f32), DIFF_QK_DIM, kn_ref[...])
        kt_scr[:, 0:seq] = _rope(kn, cos_ref[...], sin_ref[...]).T.astype(bf16)
        kt_scr[:, seq:] = kc_ref[...].T.astype(bf16)
        vx_scr[0:seq, :] = _with_ones(v_ref[...])
        vx_scr[seq:, :] = _with_ones(vc_ref[...])

    lam = _diff_lambda(lam_ref, lam_init)
    r0 = pl.multiple_of(i * qblk, qblk)
    qn = _seg_rms(q_ref[...].astype(f32), DIFF_QK_DIM, qn_ref[...])
    qn = _rope(qn, cos_ref[pl.ds(r0, qblk), :], sin_ref[pl.ds(r0, qblk), :]) * (DIFF_QK_DIM ** -0.5)
    o = _diff_heads(qn.astype(bf16), kt_scr[...], vx_scr[...], lam)
    o_ref[...] = (_seg_rms(o, HEAD_W, nw_ref[...]) * (1.0 - lam_init)).astype(o_ref.dtype)


def _diff_sample(u, nb, layer, lam_init, cache_k, cache_v, cos, sin, qn_row, kn_row, lam_vecs, nw_row):
    seq = DEC_SEQ
    qblk = 512
    nq = seq // qblk
    cb = COL_DIFF // MIX_W
    full = lambda j: pl.BlockSpec((seq, MIX_W), lambda b, i: (b, j))
    ctx = pl.BlockSpec((None, None, PAST_LEN, MIX_W), lambda b, i: (b, layer, 0, 0))
    const = pl.BlockSpec((1, MIX_W), lambda b, i: (0, 0))
    tab = pl.BlockSpec((seq, MIX_W), lambda b, i: (0, 0))
    return pl.pallas_call(
        functools.partial(_diff_sample_body, lam_init, qblk),
        grid=(nb, nq),
        in_specs=[pl.BlockSpec((qblk, MIX_W), lambda b, i: (b * nq + i, cb)), full(cb + 1), full(cb + 2),
                  ctx, ctx, tab, tab, const, const,
                  pl.BlockSpec((4, DIFF_QK_DIM), lambda b, i: (0, 0)), const],
        out_specs=pl.BlockSpec((qblk, MIX_W), lambda b, i: (b * nq + i, 0)),
        out_shape=jax.ShapeDtypeStruct((nb * seq, MIX_W), bf16),
        scratch_shapes=[pltpu.VMEM((MIX_W, seq + PAST_LEN), bf16), pltpu.VMEM((seq + PAST_LEN, 2 * MIX_W), bf16)],
        compiler_params=_params(("arbitrary", "arbitrary")),
        name="diff_sample",
    )(u, u, u, cache_k, cache_v, cos, sin, qn_row, kn_row, lam_vecs, nw_row)


def _rope_tables():
    t = jnp.arange(DEC_SEQ)
    nf = DIFF_QK_DIM // 4
    inv = ROPE_BASE ** (-jnp.arange(nf, dtype=f32) / nf)
    ar = (t // GRID_W).astype(f32)[:, None] * inv
    ac = (t % GRID_W).astype(f32)[:, None] * inv
    cos = jnp.concatenate([jnp.cos(ar), jnp.cos(ar), jnp.cos(ac), jnp.cos(ac)], axis=1)
    sin = jnp.concatenate([-jnp.sin(ar), jnp.sin(ar), -jnp.sin(ac), jnp.sin(ac)], axis=1)
    reps = MIX_W // DIFF_QK_DIM
    return jnp.tile(cos, (1, reps)), jnp.tile(sin, (1, reps))


def _merge_body(x_ref, a_ref, b_ref, c_ref, d_ref, gp_ref, mod_ref, wb_ref, wo_ref, nw_ref, wr_ref,
                xo_ref, h2_ref, aff_ref):
    mixed = None
    for n, br in enumerate((a_ref, b_ref, c_ref, d_ref)):
        gate = _sigmoid(gp_ref[:, n * D_MODEL:(n + 1) * D_MODEL].astype(f32))
        t = gate * jnp.dot(br[...].astype(bf16), wb_ref[n], preferred_element_type=f32)
        mixed = t if mixed is None else mixed + t
    y = jnp.dot(mixed.astype(bf16), wo_ref[...], preferred_element_type=f32)
    x = x_ref[...] + mod_ref[:, 2 * D_MODEL:3 * D_MODEL] * y
    xo_ref[...] = x
    h = x * lax.rsqrt(jnp.mean(x * x, axis=-1, keepdims=True) + EPS) * nw_ref[...]
    h = h * (1.0 + mod_ref[:, 4 * D_MODEL:5 * D_MODEL]) + mod_ref[:, 3 * D_MODEL:4 * D_MODEL]
    hb = h.astype(bf16)
    h2_ref[...] = hb
    logits = jnp.dot(hb, wr_ref[...], preferred_element_type=f32)
    lane = lax.broadcasted_iota(jnp.int32, logits.shape, 1)
    logits = jnp.where(lane < N_EXPERTS, logits, NEG)
    e = jnp.exp(logits - logits.max(axis=1, keepdims=True))
    aff_ref[...] = e / e.sum(axis=1, keepdims=True)


def _merge(x, brs, u, mod3, mod_base, mod_stride, seq, wb, wo, nw_row, wr):
    n = x.shape[0]
    tm = 512
    tps = max(seq // tm, 1)
    row = lambda w: pl.BlockSpec((tm, w), lambda i: (i, 0))
    return pl.pallas_call(
        _merge_body,
        grid=(n // tm,),
        in_specs=[row(D_MODEL), row(MIX_W), row(MIX_W), row(MIX_W), row(MIX_W),
                  pl.BlockSpec((tm, N_BRANCH * D_MODEL), lambda i: (i, COL_GPRE // (N_BRANCH * D_MODEL))),
                  pl.BlockSpec((None, 1, 6 * D_MODEL), lambda i: (mod_base + mod_stride * (i // tps), 0, 0)),
                  pl.BlockSpec((N_BRANCH, MIX_W, D_MODEL), lambda i: (0, 0, 0)),
                  pl.BlockSpec((D_MODEL, D_MODEL), lambda i: (0, 0)),
                  pl.BlockSpec((1, D_MODEL), lambda i: (0, 0)),
                  pl.BlockSpec((D_MODEL, LANE), lambda i: (0, 0))],
        out_specs=[row(D_MODEL), row(D_MODEL), row(LANE)],
        out_shape=[jax.ShapeDtypeStruct((n, D_MODEL), f32), jax.ShapeDtypeStruct((n, D_MODEL), bf16),
                   jax.ShapeDtypeStruct((n, LANE), f32)],
        compiler_params=_params(("arbitrary",)),
        name="merge",
    )(x, *brs, u, mod3, wb, wo, nw_row, wr)


def _rows_prefix_sum(x):
    n = x.shape[0]
    pos = lax.broadcasted_iota(jnp.int32, x.shape, 0)
    k = 1
    while k < n:
        x = x + jnp.where(pos >= k, pltpu.roll(x, k, 0), 0.0)
        k *= 2
    return x


def _route_body(t, cap, bpg, aff_ref, h2_ref, xg_ref, g_ref, rank_ref, rrow_scr, arow_scr, oh_scr):
    bb = pl.program_id(1)

    @pl.when(bb == 0)
    def _():
        aff = aff_ref[...]
        tok = lax.broadcasted_iota(jnp.int32, aff.shape, 0)
        val = aff
        k = 2
        while k <= t:
            j = k // 2
            while j >= 1:
                lower = (tok & j) == 0
                pv = jnp.where(lower, pltpu.roll(val, t - j, 0), pltpu.roll(val, j, 0))
                val = jnp.where(lower == ((tok & k) == 0), jnp.maximum(val, pv), jnp.minimum(val, pv))
                j //= 2
            k *= 2
        thr = val[cap - 1:cap, :]
        above = aff > thr
        tie = jnp.where(aff == thr, 1.0, 0.0)
        need = cap - jnp.sum(jnp.where(above, 1.0, 0.0), axis=0, keepdims=True)
        sel = jnp.where(above | ((tie > 0.0) & (_rows_prefix_sum(tie) - tie < need)), 1.0, 0.0)
        rank = jnp.where(sel > 0.0, _rows_prefix_sum(sel) - sel, float(cap))
        rank_ref[...] = rank
        rank_t = rank.T
        aff_t = aff.T
        for k in range(bpg):
            rrow_scr[k] = rank_t[k * N_EXPERTS:(k + 1) * N_EXPERTS, :]
            arow_scr[k] = aff_t[k * N_EXPERTS:(k + 1) * N_EXPERTS, :]

    rk = rrow_scr[bb]
    ar = arow_scr[bb]
    slot = lax.broadcasted_iota(jnp.int32, (cap, t), 0).astype(f32)
    for e in range(N_EXPERTS):
        onehot = jnp.where(rk[e:e + 1, :] == slot, 1.0, 0.0)
        oh_scr[e * cap:(e + 1) * cap, :] = onehot.astype(bf16)
        g_ref[e] = jnp.sum(onehot * ar[e:e + 1, :], axis=1, keepdims=True)
    h2 = h2_ref[...]
    epb = max(1, 512 // cap)
    for e0 in range(0, N_EXPERTS, epb):
        rows = jnp.dot(oh_scr[e0 * cap:(e0 + epb) * cap, :], h2, preferred_element_type=f32)
        xg_ref[e0:e0 + epb] = rows.astype(bf16).reshape(epb, cap, D_MODEL)


def _group_lanes(aff, nb, t, bpg):
    a = aff[:, :N_EXPERTS].reshape(nb // bpg, bpg, t, N_EXPERTS)
    a = jnp.transpose(a, (0, 2, 1, 3)).reshape(nb // bpg * t, bpg * N_EXPERTS)
    return jnp.pad(a, ((0, 0), (0, LANE - bpg * N_EXPERTS)))


def _route(aff, h2, nb, t, bpg):
    cap = CAPACITY_FACTOR * t // N_EXPERTS
    return pl.pallas_call(
        functools.partial(_route_body, t, cap, bpg),
        grid=(nb // bpg, bpg),
        in_specs=[pl.BlockSpec((t, LANE), lambda g, k: (g, 0)),
                  pl.BlockSpec((t, D_MODEL), lambda g, k: (g * bpg + k, 0))],
        out_specs=[pl.BlockSpec((N_EXPERTS, cap, D_MODEL), lambda g, k: (0, g * bpg + k, 0)),
                   pl.BlockSpec((N_EXPERTS, cap, 1), lambda g, k: (0, g * bpg + k, 0)),
                   pl.BlockSpec((t, LANE), lambda g, k: (g, 0))],
        out_shape=[jax.ShapeDtypeStruct((N_EXPERTS, nb * cap, D_MODEL), bf16),
                   jax.ShapeDtypeStruct((N_EXPERTS, nb * cap, 1), f32),
                   jax.ShapeDtypeStruct((nb // bpg * t, LANE), f32)],
        scratch_shapes=[pltpu.VMEM((bpg, N_EXPERTS, t), f32), pltpu.VMEM((bpg, N_EXPERTS, t), f32),
                        pltpu.VMEM((N_EXPERTS * cap, t), bf16)],
        compiler_params=_params(("arbitrary", "arbitrary")),
        name=f"route_{t}",
    )(_group_lanes(aff, nb, t, bpg), h2)


def _ffn_body(xp_ref, xs_ref, gp_ref, gs_ref, wg_ref, wu_ref, wd_ref, yp_ref, ys_ref, wg_scr, wu_scr, wd_scr):
    wg_scr[...] = wg_ref[...].astype(bf16)
    wu_scr[...] = wu_ref[...].astype(bf16)
    wd_scr[...] = wd_ref[...].astype(bf16)
    blk = 512
    for x_ref, g_ref, y_ref in ((xp_ref, gp_ref, yp_ref), (xs_ref, gs_ref, ys_ref)):
        def step(i, carry, x_ref=x_ref, g_ref=g_ref, y_ref=y_ref):
            r0 = pl.multiple_of(i * blk, blk)
            x = x_ref[pl.ds(r0, blk), :]
            a = jnp.dot(x, wg_scr[...], preferred_element_type=f32)
            up = jnp.dot(x, wu_scr[...], preferred_element_type=f32)
            mid = (jax.nn.silu(a) * up).astype(bf16)
            y = jnp.dot(mid, wd_scr[...], preferred_element_type=f32)
            y_ref[pl.ds(r0, blk), :] = (y * g_ref[pl.ds(r0, blk), :]).astype(bf16)
            return carry
        lax.fori_loop(0, x_ref.shape[0] // blk, step, 0)


def _ffn(xp, xs, gp, gs, w_gate, w_up, w_down, layer):
    np_, ns = xp.shape[1], xs.shape[1]
    tok = lambda n, w: pl.BlockSpec((None, n, w), lambda e: (e, 0, 0))
    wspec = lambda a, b: pl.BlockSpec((None, None, a, b), lambda e: (layer, e, 0, 0))
    return pl.pallas_call(
        _ffn_body,
        grid=(N_EXPERTS,),
        in_specs=[tok(np_, D_MODEL), tok(ns, D_MODEL), tok(np_, 1), tok(ns, 1),
                  wspec(D_MODEL, EXPERT_FF), wspec(D_MODEL, EXPERT_FF), wspec(EXPERT_FF, D_MODEL)],
        out_specs=[tok(np_, D_MODEL), tok(ns, D_MODEL)],
        out_shape=[jax.ShapeDtypeStruct(xp.shape, bf16), jax.ShapeDtypeStruct(xs.shape, bf16)],
        scratch_shapes=[pltpu.VMEM((D_MODEL, EXPERT_FF), bf16), pltpu.VMEM((D_MODEL, EXPERT_FF), bf16),
                        pltpu.VMEM((EXPERT_FF, D_MODEL), bf16)],
        compiler_params=_params(("arbitrary",)),
        name="expert_ffn",
    )(xp, xs, gp, gs, w_gate, w_up, w_down)


def _combine_body(t, cap, bpg, bps, x_ref, rank_ref, y_ref, mod_ref, o_ref):
    k = N_EXPERTS * cap
    er = lax.broadcasted_iota(jnp.int32, (LANE, k), 0)
    ec = lax.broadcasted_iota(jnp.int32, (LANE, k), 1) // cap
    slot = (lax.broadcasted_iota(jnp.int32, (t, k), 1) % cap).astype(f32)
    rank = rank_ref[...].astype(bf16)
    for sub in range(bps):
        col0 = ((pl.program_id(0) * bps + sub) % bpg) * N_EXPERTS
        expand = jnp.where(er == ec + col0, 1.0, 0.0).astype(bf16)
        rexp = jnp.dot(rank, expand, preferred_element_type=f32)
        pt = jnp.where(rexp == slot, 1.0, 0.0).astype(bf16)
        y = y_ref[:, sub * cap:(sub + 1) * cap, :].reshape(k, D_MODEL)
        moe = jnp.dot(pt, y, preferred_element_type=f32)
        rs = slice(sub * t, (sub + 1) * t)
        o_ref[rs, :] = x_ref[rs, :] + mod_ref[:, 5 * D_MODEL:6 * D_MODEL] * moe


def _combine(x, rank, y, mod3, mod_base, mod_stride, nb, t, bpg):
    cap = CAPACITY_FACTOR * t // N_EXPERTS
    bps = 2 if (mod_stride == 0 and bpg % 2 == 0) else 1
    return pl.pallas_call(
        functools.partial(_combine_body, t, cap, bpg, bps),
        grid=(nb // bps,),
        in_specs=[pl.BlockSpec((bps * t, D_MODEL), lambda b: (b, 0)),
                  pl.BlockSpec((t, LANE), lambda b: (b * bps // bpg, 0)),
                  pl.BlockSpec((N_EXPERTS, bps * cap, D_MODEL), lambda b: (0, b, 0)),
                  pl.BlockSpec((None, 1, 6 * D_MODEL), lambda b: (mod_base + mod_stride * b, 0, 0))],
        out_specs=pl.BlockSpec((bps * t, D_MODEL), lambda b: (b, 0)),
        out_shape=jax.ShapeDtypeStruct(x.shape, f32),
        compiler_params=_params(("arbitrary",)),
        name=f"combine_{t}",
    )(x, rank, y, mod3)


def _pad_row(v, off, width=LANE):
    return jnp.zeros((1, width), f32).at[0, off:off + v.shape[0]].set(v.astype(f32))


def _layer_params(l, a):
    w_in = a["w_in"][l]
    pts, acc = [], 0
    splits = (MIX_W,) * 4 + (4 * HEADS,) + (MIX_W,) * 7 + (SSM_CONV_CH, 2 * HEADS, N_BRANCH * D_MODEL)
    for s in splits:
        pts.append((acc, acc + s))
        acc += s
    seg = lambda i: w_in[:, pts[i][0]:pts[i][1]]
    order = [14, 0, 1, 2, 3, 5, 6, 7, 8, 9, 10, 12, 11]
    w_main = jnp.concatenate([seg(i) for i in order] + [jnp.zeros((D_MODEL, U_COLS - U_USED), f32)],
                             axis=1).astype(bf16)
    w_side = jnp.concatenate([seg(4), seg(13), jnp.zeros((D_MODEL, LANE - 24), f32)], axis=1).astype(bf16)
    tile = lambda v, reps: jnp.tile(v.astype(f32), reps).reshape(1, -1)
    p = dict(
        w_main=w_main, w_side=w_side,
        norm1=a["norm1_w"][l].reshape(1, -1), norm2=a["norm2_w"][l].reshape(1, -1),
        mlstm_gb=_pad_row(a["mlstm_gate_b"][l].reshape(-1), G_MLSTM),
        mlstm_nw=a["mlstm_norm_w"][l].reshape(1, -1),
        na_qn=tile(a["na_qnorm_w"][l], HEADS), na_kn=tile(a["na_knorm_w"][l], HEADS),
        na_rpb=_na_rpb_pad(a["na_rpb"][l]),
        diff_qn=tile(a["diff_qnorm_w"][l], 2 * HEADS), diff_kn=tile(a["diff_knorm_w"][l], 2 * HEADS),
        diff_lam=a["diff_lambda"][l], diff_nw=tile(a["diff_norm_w"][l], HEADS),
        ssm_cw=a["ssm_conv_w"][l], ssm_cb=a["ssm_conv_b"][l].reshape(1, -1),
        ssm_dtb=_pad_row(a["ssm_dt_bias"][l].reshape(-1), G_DT),
        ssm_alog=_pad_row(a["ssm_a_log"][l].reshape(-1), G_DT),
        ssm_d=jnp.repeat(a["ssm_d"][l].astype(f32), HEAD_W).reshape(1, -1),
        ssm_nw=a["ssm_norm_w"][l].reshape(1, -1),
        wb=a["w_branch"][l].astype(bf16), wo=a["w_out"][l].astype(bf16),
        wr=jnp.concatenate([a["w_router"][l], jnp.zeros((D_MODEL, LANE - N_EXPERTS), f32)], axis=1).astype(bf16),
        w_gate=a["w_gate"], w_up=a["w_up"], w_down=a["w_down"],
    )
    return p


def _mixers(u, g, nb, seq, p, l, lam_init, ctx, tables, caches):
    if ctx is None:
        a_out, m_c, m_n, m_m = _mlstm(u, g, nb, seq, p["mlstm_gb"], p["mlstm_nw"], None)
        b_out, na_k, na_v = _na_prompt(u, nb, seq, p["na_qn"], p["na_kn"], l, None if caches is None else caches[0:2])
        c_out, df_k, df_v = _diff_prompt(u, nb, seq, lam_init, p["diff_qn"], p["diff_kn"], p["diff_lam"], p["diff_nw"],
                                         l, None if caches is None else caches[2:4])
        d_out, s_h = _ssd(u, g, nb, seq, p["ssm_cw"], p["ssm_cb"], p["ssm_dtb"], p["ssm_alog"], p["ssm_d"],
                          p["ssm_nw"], None)
        return (a_out, b_out, c_out, d_out), (na_k, na_v, df_k, df_v, m_c, m_n, m_m, s_h)
    (na_ck, na_cv, df_ck, df_cv, st_c, st_n, st_m, st_s) = ctx
    cos, sin = tables
    a_out = _mlstm(u, g, nb, seq, p["mlstm_gb"], p["mlstm_nw"], (st_c[:, l], st_n[:, l], st_m[:, l]))[0]
    b_out = _na_sample(u, nb, l, na_ck, na_cv, p["na_rpb"], p["na_qn"], p["na_kn"])
    c_out = _diff_sample(u, nb, l, lam_init, df_ck, df_cv, cos, sin, p["diff_qn"], p["diff_kn"], p["diff_lam"],
                         p["diff_nw"])
    d_out = _ssd(u, g, nb, seq, p["ssm_cw"], p["ssm_cb"], p["ssm_dtb"], p["ssm_alog"], p["ssm_d"], p["ssm_nw"],
                 st_s[:, l])[0]
    return (a_out, b_out, c_out, d_out), None


def kernel(x_prompt, x_sample, cache_na_k, cache_na_v, cache_diff_k, cache_diff_v, state_mlstm_C, state_mlstm_n, state_mlstm_m, state_ssm, c, c_ctx, norm1_w, norm2_w, w_ada, b_ada, w_in, mlstm_gate_b, mlstm_norm_w, na_qnorm_w, na_knorm_w, na_rpb, diff_qnorm_w, diff_knorm_w, diff_lambda, diff_norm_w, ssm_conv_w, ssm_conv_b, ssm_dt_bias, ssm_a_log, ssm_d, ssm_norm_w, w_branch, w_out, w_router, w_gate, w_up, w_down):
    a = dict(norm1_w=norm1_w, norm2_w=norm2_w, w_in=w_in, mlstm_gate_b=mlstm_gate_b, mlstm_norm_w=mlstm_norm_w,
             na_qnorm_w=na_qnorm_w, na_knorm_w=na_knorm_w, na_rpb=na_rpb, diff_qnorm_w=diff_qnorm_w,
             diff_knorm_w=diff_knorm_w, diff_lambda=diff_lambda, diff_norm_w=diff_norm_w, ssm_conv_w=ssm_conv_w,
             ssm_conv_b=ssm_conv_b, ssm_dt_bias=ssm_dt_bias, ssm_a_log=ssm_a_log, ssm_d=ssm_d,
             ssm_norm_w=ssm_norm_w, w_branch=w_branch, w_out=w_out, w_router=w_router, w_gate=w_gate, w_up=w_up,
             w_down=w_down)
    nb_p, nb_s = x_prompt.shape[0], x_sample.shape[0]
    xp = x_prompt.reshape(nb_p * SEQ, D_MODEL)
    xs = x_sample.reshape(nb_s * DEC_SEQ, D_MODEL)
    cc = jnp.concatenate([c_ctx[None, :], c, jnp.zeros((8 - 1 - nb_s, D_MODEL), f32)], axis=0)
    ctx = (cache_na_k.reshape(nb_s, DEPTH, PAST_LEN, MIX_W), cache_na_v.reshape(nb_s, DEPTH, PAST_LEN, MIX_W),
           cache_diff_k.reshape(nb_s, DEPTH, PAST_LEN, MIX_W), cache_diff_v.reshape(nb_s, DEPTH, PAST_LEN, MIX_W),
           state_mlstm_C, state_mlstm_n, state_mlstm_m, state_ssm)
    tables = _rope_tables()
    new, caches = [], None
    for l in range(DEPTH):
        p = _layer_params(l, a)
        lam_init = 0.8 - 0.6 * math.exp(-0.3 * l)
        mod3 = _modulation(cc, w_ada, b_ada, l).reshape(8, 1, 6 * D_MODEL)
        groups = []
        for (x, nb, seq, base, stride, gctx) in ((xp, nb_p, SEQ, 0, 0, None), (xs, nb_s, DEC_SEQ, 1, 1, ctx)):
            u, g = _inproj(x, mod3, base, stride, seq, p["norm1"], p["w_main"], p["w_side"])
            brs, st = _mixers(u, g, nb, seq, p, l, lam_init, gctx, tables, caches)
            if st is not None:
                new.append(st)
                caches = st[0:4]
            x1, h2, aff = _merge(x, brs, u, mod3, base, stride, seq, p["wb"], p["wo"], p["norm2"], p["wr"])
            bpg = min(nb, LANE // N_EXPERTS)
            xg, gg, rank = _route(aff, h2, nb, seq, bpg)
            groups.append((x1, xg, gg, rank, nb, seq, base, stride, bpg))
        yp, ys = _ffn(groups[0][1], groups[1][1], groups[0][2], groups[1][2], p["w_gate"], p["w_up"], p["w_down"], l)
        outs = []
        for (x1, _, _, rank, nb, seq, base, stride, bpg), y in zip(groups, (yp, ys)):
            outs.append(_combine(x1, rank, y, mod3, base, stride, nb, seq, bpg))
        xp, xs = outs
    stack = lambda i, shape: jnp.stack([new[l][i].reshape(shape) for l in range(DEPTH)], axis=1)
    kv = (nb_p, DEPTH, SEQ, HEADS, HEAD_W)
    return (xp.reshape(x_prompt.shape), xs.reshape(x_sample.shape),
            caches[0].reshape(kv), caches[1].reshape(kv), caches[2].reshape(kv), caches[3].reshape(kv),
            stack(4, (nb_p, 2, HEADS, HEAD_W, HEAD_W)), stack(5, (nb_p, 2, HEADS, HEAD_W)),
            stack(6, (nb_p, 2, HEADS)), stack(7, (nb_p, 2, HEADS, HEAD_W, SSM_STATE)))
```

```python
import functools
import math

import jax
import jax.numpy as jnp
from jax import lax
from jax.experimental import pallas as pl
from jax.experimental.pallas import tpu as pltpu

f32 = jnp.float32
bf16 = jnp.bfloat16

D_MODEL = 1024
BATCH = 32
SEQ = 256
DEPTH = 2
DEC_BATCH = 4
DEC_SEQ = 1024
PAST_LEN = 512
GRID_W = 64
MIX_W = D_MODEL // 4
N_BRANCH = 4
HEADS = 4
HEAD_W = MIX_W // HEADS
NA_WIN_R = 8
NA_WIN_C = 16
DIFF_QK_DIM = HEAD_W // 2
SSM_GROUPS = 2
SSM_STATE = 64
SSM_CONV_K = 5
SSM_CONV_CH = MIX_W + 2 * SSM_GROUPS * SSM_STATE
CHUNK = 64
PROMPT_BPS = 2
SCAN_BPS = 2
SCAN_UNROLL = 8
N_EXPERTS = 16
EXPERT_FF = 1024
CAPACITY_FACTOR = 2
ROPE_BASE = 10000.0
EPS = 1e-6
NEG = -1e30

LANE = 128
VMEM_LIMIT = 56 * 1024 * 1024

COL_GPRE = 0
COL_MLSTM = 4096
COL_NA = 5120
COL_DIFF = 5888
COL_XBC = 6656
COL_Z = 7168
U_USED = 7424
U_COLS = 7680
G_MLSTM = 0
G_DT = 16


def _params(sem):
    return pltpu.CompilerParams(dimension_semantics=sem, vmem_limit_bytes=VMEM_LIMIT)


def _mm(a, b):
    return jnp.dot(a.astype(bf16), b.astype(bf16), preferred_element_type=f32)


def _mm_nt(a, b):
    return lax.dot_general(a.astype(bf16), b.astype(bf16), (((1,), (1,)), ((), ())), preferred_element_type=f32)


def _mm_tn(a, b):
    return lax.dot_general(a.astype(bf16), b.astype(bf16), (((0,), (0,)), ((), ())), preferred_element_type=f32)


def _sigmoid(x):
    return 0.5 * jnp.tanh(0.5 * x) + 0.5


def _seg_sum(x, seg):
    w = x.shape[-1]
    r = lax.broadcasted_iota(jnp.int32, (w, w), 0) // seg
    c = lax.broadcasted_iota(jnp.int32, (w, w), 1) // seg
    ones = jnp.where(r == c, 1.0, 0.0).astype(bf16)
    hi = x.astype(bf16)
    lo = (x - hi.astype(f32)).astype(bf16)
    return jnp.dot(hi, ones, preferred_element_type=f32) + jnp.dot(lo, ones, preferred_element_type=f32)


def _seg_rms(x, seg, w_row):
    ms = _seg_sum(x * x, seg) * (1.0 / seg)
    return x * lax.rsqrt(ms + EPS) * w_row


def _mod_body(c_ref, w_ref, b_ref, o_ref):
    o_ref[...] = _mm(jax.nn.silu(c_ref[...]), w_ref[...]) + b_ref[...]


def _modulation(cc, w_ada, b_ada, layer):
    tn = 1536
    n = w_ada.shape[2]
    return pl.pallas_call(
        _mod_body,
        grid=(n // tn,),
        in_specs=[pl.BlockSpec((8, D_MODEL), lambda j: (0, 0)),
                  pl.BlockSpec((None, D_MODEL, tn), lambda j: (layer, 0, j)),
                  pl.BlockSpec((None, 1, tn), lambda j: (layer, 0, j))],
        out_specs=pl.BlockSpec((8, tn), lambda j: (0, j)),
        out_shape=jax.ShapeDtypeStruct((8, n), f32),
        compiler_params=_params(("arbitrary",)),
        name="modulation",
    )(cc, w_ada, b_ada.reshape(b_ada.shape[0], 1, n))


def _inproj_body(ncol, x_ref, mod_ref, nw_ref, w_ref, wg_ref, u_ref, g_ref):
    x = x_ref[...]
    y = x * lax.rsqrt(jnp.mean(x * x, axis=-1, keepdims=True) + EPS) * nw_ref[...]
    hb = (y * (1.0 + mod_ref[:, D_MODEL:2 * D_MODEL]) + mod_ref[:, 0:D_MODEL]).astype(bf16)
    g_ref[...] = jnp.dot(hb, wg_ref[...], preferred_element_type=f32)
    cw = U_COLS // ncol
    for c in range(ncol):
        u_ref[:, c * cw:(c + 1) * cw] = jnp.dot(hb, w_ref[:, c * cw:(c + 1) * cw],
                                                preferred_element_type=f32).astype(u_ref.dtype)


def _inproj(x, mod3, mod_base, mod_stride, seq, nw_row, w_main, w_side):
    n = x.shape[0]
    tm = 512
    tps = max(seq // tm, 1)
    resident = lambda shape: pl.BlockSpec(shape, lambda i: (0, 0), pipeline_mode=pl.Buffered(1))
    return pl.pallas_call(
        functools.partial(_inproj_body, 3),
        grid=(n // tm,),
        in_specs=[pl.BlockSpec((tm, D_MODEL), lambda i: (i, 0)),
                  pl.BlockSpec((None, 1, 6 * D_MODEL), lambda i: (mod_base + mod_stride * (i // tps), 0, 0)),
                  pl.BlockSpec((1, D_MODEL), lambda i: (0, 0)),
                  resident((D_MODEL, U_COLS)), resident((D_MODEL, LANE))],
        out_specs=[pl.BlockSpec((tm, U_COLS), lambda i: (i, 0)),
                   pl.BlockSpec((tm, LANE), lambda i: (i, 0))],
        out_shape=[jax.ShapeDtypeStruct((n, U_COLS), bf16), jax.ShapeDtypeStruct((n, LANE), f32)],
        compiler_params=_params(("arbitrary",)),
        name="inproj",
    )(x, mod3, nw_row, w_main, w_side)


def _split3(x):
    hi = x.astype(bf16)
    r = x - hi.astype(f32)
    mid = r.astype(bf16)
    lo = (r - mid.astype(f32)).astype(bf16)
    return hi, mid, lo


def _select_cols(x, onehot):
    return jnp.dot(jnp.concatenate(_split3(x), axis=1), jnp.concatenate([onehot] * 3, axis=0),
                   preferred_element_type=f32)


def _select_rows(onehot, x):
    return jnp.dot(jnp.concatenate([onehot] * 3, axis=1), jnp.concatenate(_split3(x), axis=0),
                   preferred_element_type=f32)


def _chunk_scan(x, op, identity, reverse):
    n = x.shape[0]
    pos = lax.broadcasted_iota(jnp.int32, x.shape, 0) % CHUNK
    k = 1
    while k < CHUNK:
        if reverse:
            shifted, ok = pltpu.roll(x, n - k, 0), pos < CHUNK - k
        else:
            shifted, ok = pltpu.roll(x, k, 0), pos >= k
        x = op(x, jnp.where(ok, shifted, identity))
        k *= 2
    return x


def _dir_scan(x, op, identity, bwd_col):
    return jnp.where(bwd_col, _chunk_scan(x, op, identity, True), _chunk_scan(x, op, identity, False))


def _head_expand(col0):
    c = lax.broadcasted_iota(jnp.int32, (LANE, MIX_W), 0)
    h = lax.broadcasted_iota(jnp.int32, (LANE, MIX_W), 1) // HEAD_W
    return jnp.where(c == col0 + h, 1.0, 0.0).astype(bf16)


def _chunk_rows(x, seq):
    nrow = max(seq // CHUNK, 8)
    t = lax.broadcasted_iota(jnp.int32, (seq, MIX_W), 0) % CHUNK
    s_ = lax.broadcasted_iota(jnp.int32, (seq, MIX_W), 1) % HEAD_W
    sel = lax.broadcasted_iota(jnp.int32, (nrow, seq), 1) // CHUNK == lax.broadcasted_iota(jnp.int32, (nrow, seq), 0)
    return _select_rows(jnp.where(sel, 1.0, 0.0).astype(bf16), jnp.where(t == s_, x, 0.0))


def _block_diag(x, reps, keep):
    return jnp.concatenate([x] * reps, axis=0) * keep


def _head_mask(rows, cols, rseg, cseg):
    return (lax.broadcasted_iota(jnp.int32, (rows, cols), 0) // rseg) == (lax.broadcasted_iota(jnp.int32, (rows, cols), 1) // cseg)


def _layer_view(ref, sub, layer, whole):
    if not whole:
        return ref.at[sub]
    for l2 in range(DEPTH):
        if l2 != layer:
            ref[sub, l2] = jnp.zeros(ref.shape[2:], f32)
    return ref.at[sub, layer]


def _mlstm_body(seq, bps, has_state, layer, n_prev, *refs):
    if has_state:
        (qkvo_ref, g_ref, gb_ref, nw_ref, c0_ref, n0_ref, m0_ref,
         out_ref, cs_ref, ns_ref, ms_ref, bx_scr, mx_scr, vx_scr, vrow_scr, cbd_scr, hf_scr, hb_scr) = refs
    else:
        qkvo_ref, g_ref, gb_ref, nw_ref = refs[:4]
        (out_ref, cs_ref, ns_ref, ms_ref, bx_scr, mx_scr, vx_scr, vrow_scr, cbd_scr, hf_scr,
         hb_scr) = refs[4 + n_prev:]
    whole = None if has_state else n_prev == 0
    nc = seq // CHUNK
    tot = bps * seq
    g = g_ref[...] + gb_ref[...]
    lane = lax.broadcasted_iota(jnp.int32, g.shape, 1)
    bwd_col = (lane % 16) >= 8
    bsum = _dir_scan(jax.nn.log_sigmoid(g), jnp.add, 0.0, bwd_col)
    b_i = pltpu.roll(bsum, LANE - HEADS, 1)
    vcol = g - b_i
    mcol = b_i + _dir_scan(vcol, jnp.maximum, NEG, bwd_col)
    for d in range(2):
        e = _head_expand(G_MLSTM + d * 8)
        bx_scr[d] = _select_cols(b_i, e)
        mx_scr[d] = _select_cols(mcol, e)
        vx = _select_cols(vcol, e)
        vx_scr[d] = vx
        vrow = _chunk_rows(vx, tot)
        for c in range(bps * nc):
            vrow_scr[d, c] = vrow[c:c + 1, :]

    grp = lax.broadcasted_iota(jnp.int32, (1, MIX_W), 1) // HEAD_W
    cbd_scr[...] = jnp.zeros(cbd_scr.shape, f32)
    init = []
    for sub in range(bps):
        for d in range(2):
            if has_state:
                for h in range(HEADS):
                    cbd_scr[2 * sub + d, h * HEAD_W:(h + 1) * HEAD_W, h * HEAD_W:(h + 1) * HEAD_W] = c0_ref[sub, d, h]
                init.append(jnp.concatenate([n0_ref[sub, d, h:h + 1, :] for h in range(HEADS)], axis=1))
                m_row = jnp.zeros((1, MIX_W), f32)
                for h in range(HEADS):
                    m_row = jnp.where(grp == h, m0_ref[sub, d:d + 1, h:h + 1], m_row)
                init.append(m_row)
            else:
                init += [jnp.zeros((1, MIX_W), f32), jnp.zeros((1, MIX_W), f32)]

    li = lax.broadcasted_iota(jnp.int32, (CHUNK, MIX_W), 0)
    si = lax.broadcasted_iota(jnp.int32, (CHUNK, MIX_W), 1) % HEAD_W
    valid = (si <= li, si >= li)
    bd = _head_mask(MIX_W, MIX_W, HEAD_W, HEAD_W)
    bd_f32 = jnp.where(bd, 1.0, 0.0)
    ones_bd = bd_f32.astype(bf16)

    def step(c, carry):
        new = []
        for sd in range(2 * bps):
            sub, d = sd // 2, sd % 2
            n_row, m_row = carry[2 * sd], carry[2 * sd + 1]
            cidx = sub * nc + (c if d == 0 else nc - 1 - c)
            r0 = pl.multiple_of(cidx * CHUNK, CHUNK)
            last = CHUNK - 1 if d == 0 else 0
            q = qkvo_ref[pl.ds(r0, CHUNK), 0:MIX_W].astype(f32)
            k = qkvo_ref[pl.ds(r0, CHUNK), MIX_W:2 * MIX_W].astype(f32) * (HEAD_W ** -0.5)
            vb = qkvo_ref[pl.ds(r0, CHUNK), 2 * MIX_W:3 * MIX_W].astype(bf16)
            qb = q.astype(bf16)
            bx = bx_scr[d, pl.ds(r0, CHUNK), :]
            mx = mx_scr[d, pl.ds(r0, CHUNK), :]
            vx = vx_scr[d, pl.ds(r0, CHUNK), :]
            p = jnp.exp(jnp.where(valid[d], bx + vrow_scr[d, cidx] - mx, NEG))
            smat = _mm_nt(qb, _block_diag(k.astype(bf16), HEADS, ones_bd)) * p
            intra = _mm(smat, _block_diag(vb, HEADS, ones_bd))
            dsum = _mm(smat, ones_bd)
            gg = bx + m_row
            mt = jnp.maximum(gg, mx)
            a = jnp.exp(mx - mt)
            w_prev = jnp.exp(gg - mt)
            inter = _mm(qb, cbd_scr[sd])
            dint = _mm(q * n_row, ones_bd)
            num = intra * a + inter * w_prev
            den = dsum * a + dint * w_prev
            h_scr = hf_scr if d == 0 else hb_scr
            h_scr[pl.ds(r0, CHUNK), :] = num / jnp.maximum(jnp.abs(den), jnp.exp(-mt))
            bl = bx[last:last + 1, :]
            mloc = mx[last:last + 1, :]
            m_new = jnp.maximum(bl + m_row, mloc)
            dec = jnp.exp(bl + m_row - m_new)
            fac = jnp.exp(mloc - m_new)
            kw = k * jnp.exp(bl + vx - mloc)
            kv = _mm_tn(kw, vb) * bd_f32
            cbd_scr[sd] = cbd_scr[sd] * dec + kv * fac
            new += [n_row * dec + jnp.sum(kw, axis=0, keepdims=True) * fac, m_new]
        return tuple(new)

    fin_state = lax.fori_loop(0, nc, step, tuple(init), unroll=min(SCAN_UNROLL, nc))
    for sub in range(bps):
        cs, ns, ms = (_layer_view(r, sub, layer, whole) for r in (cs_ref, ns_ref, ms_ref))
        for d in range(2):
            sd = 2 * sub + d
            n_row, m_row = fin_state[2 * sd], fin_state[2 * sd + 1]
            for h in range(HEADS):
                hl = slice(h * HEAD_W, (h + 1) * HEAD_W)
                cs[d, h] = cbd_scr[sd, hl, hl]
                ns[d, h:h + 1, :] = n_row[:, hl]
                ms[d:d + 1, h:h + 1] = m_row[:, h * HEAD_W:h * HEAD_W + 1]

    blk = 256

    def fin(i, carry):
        r0 = pl.multiple_of(i * blk, blk)
        hs = hf_scr[pl.ds(r0, blk), :] + hb_scr[pl.ds(r0, blk), :]
        hn = _seg_rms(hs, HEAD_W, nw_ref[...])
        o = qkvo_ref[pl.ds(r0, blk), 3 * MIX_W:4 * MIX_W].astype(f32)
        out_ref[pl.ds(r0, blk), :] = (hn * jax.nn.sigmoid(o)).astype(out_ref.dtype)
        return carry

    lax.fori_loop(0, tot // blk, fin, 0)


def _mlstm(u, g, nb, seq, gb_row, nw_row, state, layer=0, prev=None):
    has_state = state is not None
    bps = 1 if has_state else SCAN_BPS
    nc = seq // CHUNK
    rows = bps * seq
    tails = ((HEAD_W, HEAD_W), (HEAD_W,), ())
    st = lambda tail: pl.BlockSpec((bps, 2, HEADS) + tail, lambda b: (b, 0, 0) + (0,) * len(tail))
    in_specs = [pl.BlockSpec((rows, 4 * MIX_W), lambda b: (b, COL_MLSTM // (4 * MIX_W))),
                pl.BlockSpec((rows, LANE), lambda b: (b, 0)),
                pl.BlockSpec((1, LANE), lambda b: (0, 0)),
                pl.BlockSpec((1, MIX_W), lambda b: (0, 0))]
    args = [u, g, gb_row, nw_row]
    aliases = {}
    if has_state:
        in_specs += [st(t) for t in tails]
        args += list(state)
        st_specs = [st(t) for t in tails]
        st_shapes = [jax.ShapeDtypeStruct((nb, 2, HEADS) + t, f32) for t in tails]
    else:
        outs = [_layer_out(nb, bps, (2, HEADS) + t, layer, prev) for t in tails]
        st_shapes, st_specs = [o[0] for o in outs], [o[1] for o in outs]
        if prev is not None:
            in_specs += [pl.BlockSpec(memory_space=pl.ANY)] * len(prev)
            args += list(prev)
            aliases = {4 + i: 1 + i for i in range(len(prev))}
    return pl.pallas_call(
        functools.partial(_mlstm_body, seq, bps, has_state, layer, len(aliases)),
        grid=(nb // bps,),
        in_specs=in_specs,
        out_specs=[pl.BlockSpec((rows, MIX_W), lambda b: (b, 0))] + st_specs,
        out_shape=[jax.ShapeDtypeStruct((nb * seq, MIX_W), bf16)] + st_shapes,
        input_output_aliases=aliases,
        scratch_shapes=[pltpu.VMEM((2, rows, MIX_W), f32), pltpu.VMEM((2, rows, MIX_W), f32),
                        pltpu.VMEM((2, rows, MIX_W), f32), pltpu.VMEM((2, bps * nc, 1, MIX_W), f32),
                        pltpu.VMEM((2 * bps, MIX_W, MIX_W), f32),
                        pltpu.VMEM((rows, MIX_W), f32), pltpu.VMEM((rows, MIX_W), f32)],
        compiler_params=_params(("arbitrary",)),
        name="mlstm_state" if has_state else "mlstm",
    )(*args)


def _ssd_body(seq, bps, has_state, layer, n_prev, *refs):
    if has_state:
        (z_ref, xbc_ref, g_ref, cw_ref, cb_ref, dtb_ref, alog_ref, dskip_ref, nw_ref, h0_ref,
         out_ref, hs_ref, xpad_scr, xc_scr, ax_scr, dx_scr, arow_scr, ht_scr, yf_scr, yb_scr) = refs
    else:
        z_ref, xbc_ref, g_ref, cw_ref, cb_ref, dtb_ref, alog_ref, dskip_ref, nw_ref = refs[:9]
        (out_ref, hs_ref, xpad_scr, xc_scr, ax_scr, dx_scr, arow_scr, ht_scr, yf_scr,
         yb_scr) = refs[9 + n_prev:]
    whole = None if has_state else n_prev == 0
    nc = seq // CHUNK
    tot = bps * seq
    pad = 8
    blk = 256
    nblk = tot // blk
    cblk = 128
    for sub in range(bps):
        base = sub * (seq + 2 * pad)
        xpad_scr[base:base + pad, :] = jnp.zeros((pad, SSM_CONV_CH), f32)
        xpad_scr[base + pad + seq:base + 2 * pad + seq, :] = jnp.zeros((pad, SSM_CONV_CH), f32)
        xpad_scr[base + pad:base + pad + seq, :] = xbc_ref[sub * seq:(sub + 1) * seq, :].astype(f32)
        for i in range(seq // cblk):
            r0 = i * cblk
            acc = jnp.zeros((cblk, SSM_CONV_CH), f32) + cb_ref[...]
            for kk in range(SSM_CONV_K):
                off = base + r0 + pad - SSM_CONV_K // 2 + kk
                acc = acc + xpad_scr[off:off + cblk, :] * cw_ref[kk:kk + 1, :]
            xc_scr[sub * seq + r0:sub * seq + r0 + cblk, :] = jax.nn.silu(acc)

    dt = jax.nn.softplus(g_ref[...] + dtb_ref[...])
    lane = lax.broadcasted_iota(jnp.int32, dt.shape, 1)
    acum = _dir_scan(dt * (-jnp.exp(alog_ref[...])), jnp.add, 0.0, (lane % 8) >= HEADS)
    for d in range(2):
        e = _head_expand(G_DT + d * HEADS)
        ax = _select_cols(acum, e)
        ax_scr[d] = ax
        dx_scr[d] = _select_cols(dt, e)
        arow = _chunk_rows(ax, tot)
        for c in range(bps * nc):
            arow_scr[d, c] = arow[c:c + 1, :]

    gn = SSM_GROUPS * SSM_STATE
    rep = HEADS // SSM_GROUPS
    ht_scr[...] = jnp.zeros(ht_scr.shape, f32)
    if has_state:
        for sd in range(2 * bps):
            for h in range(HEADS):
                g0 = (h // rep) * SSM_STATE
                ht_scr[sd, g0:g0 + SSM_STATE, h * HEAD_W:(h + 1) * HEAD_W] = h0_ref[sd // 2, sd % 2, h].T
    li = lax.broadcasted_iota(jnp.int32, (CHUNK, MIX_W), 0)
    si = lax.broadcasted_iota(jnp.int32, (CHUNK, MIX_W), 1) % HEAD_W
    valid = (si <= li, si >= li)
    state_keep = jnp.where(_head_mask(gn, MIX_W, SSM_STATE, rep * HEAD_W), 1.0, 0.0)
    key_keep = jnp.where(_head_mask(HEADS * CHUNK, gn, rep * CHUNK, SSM_STATE), 1.0, 0.0).astype(bf16)
    head_keep = jnp.where(_head_mask(MIX_W, MIX_W, HEAD_W, HEAD_W), 1.0, 0.0).astype(bf16)

    def step(c, carry):
        for sd in range(2 * bps):
            sub, d = sd // 2, sd % 2
            cidx = sub * nc + (c if d == 0 else nc - 1 - c)
            r0 = pl.multiple_of(cidx * CHUNK, CHUNK)
            last = CHUNK - 1 if d == 0 else 0
            xs = xc_scr[pl.ds(r0, CHUNK), 0:MIX_W]
            bm = xc_scr[pl.ds(r0, CHUNK), MIX_W:MIX_W + gn].astype(bf16)
            cm = xc_scr[pl.ds(r0, CHUNK), MIX_W + gn:MIX_W + 2 * gn].astype(bf16)
            ax = ax_scr[d, pl.ds(r0, CHUNK), :]
            decay = jnp.exp(jnp.where(valid[d], ax - arow_scr[d, cidx], NEG))
            scores = _mm_nt(cm, _block_diag(bm, HEADS, key_keep)) * decay
            xdt = xs * dx_scr[d, pl.ds(r0, CHUNK), :]
            y = _mm(scores, _block_diag(xdt.astype(bf16), HEADS, head_keep)) + _mm(cm, ht_scr[sd]) * jnp.exp(ax)
            y_scr = yf_scr if d == 0 else yb_scr
            y_scr[pl.ds(r0, CHUNK), :] = y
            al = ax[last:last + 1, :]
            ht_scr[sd] = ht_scr[sd] * jnp.exp(al) + _mm_tn(bm, xdt * jnp.exp(al - ax)) * state_keep
        return carry

    lax.fori_loop(0, nc, step, 0, unroll=min(SCAN_UNROLL, nc))
    for sub in range(bps):
        hs = _layer_view(hs_ref, sub, layer, whole)
        for d in range(2):
            for h in range(HEADS):
                g0 = (h // rep) * SSM_STATE
                hs[d, h] = ht_scr[2 * sub + d, g0:g0 + SSM_STATE, h * HEAD_W:(h + 1) * HEAD_W].T

    def fin(i, carry):
        r0 = pl.multiple_of(i * blk, blk)
        y = yf_scr[pl.ds(r0, blk), :] + yb_scr[pl.ds(r0, blk), :] + dskip_ref[...] * xc_scr[pl.ds(r0, blk), 0:MIX_W]
        y = y * jax.nn.silu(z_ref[pl.ds(r0, blk), :].astype(f32))
        y = y * lax.rsqrt(jnp.mean(y * y, axis=-1, keepdims=True) + EPS) * nw_ref[...]
        out_ref[pl.ds(r0, blk), :] = y.astype(out_ref.dtype)
        return carry

    lax.fori_loop(0, nblk, fin, 0)


def _ssd(u, g, nb, seq, cw, cb_row, dtb_row, alog_row, dskip_row, nw_row, state, layer=0, prev=None):
    has_state = state is not None
    bps = 1 if has_state else SCAN_BPS
    nc = seq // CHUNK
    rows = bps * seq
    const = lambda shape: pl.BlockSpec(shape, lambda b: (0,) * len(shape))
    st = pl.BlockSpec((bps, 2, HEADS, HEAD_W, SSM_STATE), lambda b: (b, 0, 0, 0, 0))
    in_specs = [pl.BlockSpec((rows, MIX_W), lambda b: (b, COL_Z // MIX_W)),
                pl.BlockSpec((rows, SSM_CONV_CH), lambda b: (b, COL_XBC // SSM_CONV_CH)),
                pl.BlockSpec((rows, LANE), lambda b: (b, 0)),
                const((SSM_CONV_K, SSM_CONV_CH)), const((1, SSM_CONV_CH)), const((1, LANE)), const((1, LANE)),
                const((1, MIX_W)), const((1, MIX_W))]
    args = [u, u, g, cw, cb_row, dtb_row, alog_row, dskip_row, nw_row]
    aliases = {}
    st_shape = jax.ShapeDtypeStruct((nb, 2, HEADS, HEAD_W, SSM_STATE), f32)
    if has_state:
        in_specs.append(st)
        args.append(state)
    else:
        st_shape, st, _ = _layer_out(nb, bps, (2, HEADS, HEAD_W, SSM_STATE), layer, prev)
        if prev is not None:
            in_specs.append(pl.BlockSpec(memory_space=pl.ANY))
            args.append(prev)
            aliases = {9: 1}
    return pl.pallas_call(
        functools.partial(_ssd_body, seq, bps, has_state, layer, len(aliases)),
        grid=(nb // bps,),
        in_specs=in_specs,
        out_specs=[pl.BlockSpec((rows, MIX_W), lambda b: (b, 0)), st],
        out_shape=[jax.ShapeDtypeStruct((nb * seq, MIX_W), bf16), st_shape],
        input_output_aliases=aliases,
        scratch_shapes=[pltpu.VMEM((bps * (seq + 16), SSM_CONV_CH), f32), pltpu.VMEM((rows, SSM_CONV_CH), f32),
                        pltpu.VMEM((2, rows, MIX_W), f32), pltpu.VMEM((2, rows, MIX_W), f32),
                        pltpu.VMEM((2, bps * nc, 1, MIX_W), f32),
                        pltpu.VMEM((2 * bps, SSM_GROUPS * SSM_STATE, MIX_W), f32),
                        pltpu.VMEM((rows, MIX_W), f32), pltpu.VMEM((rows, MIX_W), f32)],
        compiler_params=_params(("arbitrary",)),
        name="ssd_state" if has_state else "ssd",
    )(*args)


def _with_ones(v):
    vb = v.astype(bf16)
    ones = jnp.ones((v.shape[0], HEAD_W), bf16)
    return jnp.concatenate([x for h in range(HEADS) for x in (vb[:, h * HEAD_W:(h + 1) * HEAD_W], ones)], axis=1)


def _softmax_av(scores, vexts):
    m = scores[0].max(axis=1, keepdims=True)
    for s in scores[1:]:
        m = jnp.maximum(m, s.max(axis=1, keepdims=True))
    r = None
    for s, vx in zip(scores, vexts):
        t = jnp.dot(jnp.exp(s - m).astype(bf16), vx, preferred_element_type=f32)
        r = t if r is None else r + t
    return r[:, :HEAD_W] / r[:, HEAD_W:]


def _store_layer(ref, sub, layer, whole, val):
    if whole:
        for l2 in range(DEPTH):
            ref[sub, l2] = val if l2 == layer else jnp.zeros_like(val)
    else:
        ref[sub] = val


def _layer_out(nb, bps, tail, layer, prev):
    zeros = (0,) * len(tail)
    shape = jax.ShapeDtypeStruct((nb, DEPTH) + tail, f32)
    if prev is None:
        return shape, pl.BlockSpec((bps, DEPTH) + tail, lambda b: (b, 0) + zeros), True
    return shape, pl.BlockSpec((bps, None) + tail, lambda b: (b, layer) + zeros), False


def _na_prompt_body(seq, layer, whole, q_ref, k_ref, v_ref, qn_ref, kn_ref, *rest):
    o_ref, ko_ref, vo_ref = rest[-3:]
    scale = HEAD_W ** -0.5
    qb = (_seg_rms(q_ref[...].astype(f32), HEAD_W, qn_ref[...]) * scale).astype(bf16)
    kn = _seg_rms(k_ref[...].astype(f32), HEAD_W, kn_ref[...])
    v = v_ref[...].astype(f32)
    for s0 in range(0, q_ref.shape[0], seq):
        rs = slice(s0, s0 + seq)
        _store_layer(ko_ref, s0 // seq, layer, whole, kn[rs])
        _store_layer(vo_ref, s0 // seq, layer, whole, v[rs])
        kt = kn[rs].T.astype(bf16)
        vx = _with_ones(v[rs])
        for h in range(HEADS):
            hl = slice(h * HEAD_W, (h + 1) * HEAD_W)
            s = jnp.dot(qb[rs, hl], kt[hl, :], preferred_element_type=f32)
            o_ref[rs, hl] = _softmax_av([s], [vx[:, 2 * h * HEAD_W:2 * (h + 1) * HEAD_W]]).astype(o_ref.dtype)


def _na_prompt(u, nb, seq, qn_row, kn_row, layer, prev):
    cb = COL_NA // MIX_W
    rows = PROMPT_BPS * seq
    blk = lambda j: pl.BlockSpec((rows, MIX_W), lambda b: (b, j))
    const = pl.BlockSpec((1, MIX_W), lambda b: (0, 0))
    shape, spec, whole = _layer_out(nb, PROMPT_BPS, (seq, MIX_W), layer, prev)
    extra = [] if prev is None else list(prev)
    return pl.pallas_call(
        functools.partial(_na_prompt_body, seq, layer, whole),
        grid=(nb // PROMPT_BPS,),
        in_specs=[blk(cb), blk(cb + 1), blk(cb + 2), const, const] + [pl.BlockSpec(memory_space=pl.ANY)] * len(extra),
        out_specs=[blk(0), spec, spec],
        out_shape=[jax.ShapeDtypeStruct((nb * seq, MIX_W), bf16), shape, shape],
        input_output_aliases={} if prev is None else {5: 1, 6: 2},
        compiler_params=_params(("arbitrary",)),
        name="na_prompt",
    )(u, u, u, qn_row, kn_row, *extra)


def _na_sample_body(qblk, q_ref, k_ref, v_ref, kc_ref, vc_ref, rpb_ref, qn_ref, kn_ref, o_ref,
                    kt_scr, vx_scr, bias_scr):
    b = pl.program_id(0)
    i = pl.program_id(1)
    seq = k_ref.shape[0]
    rows = seq // GRID_W

    @pl.when((b == 0) & (i == 0))
    def _():
        j = lax.broadcasted_iota(jnp.int32, (GRID_W, LANE), 0)
        lane = lax.broadcasted_iota(jnp.int32, (GRID_W, LANE), 1)
        j2 = lane % GRID_W
        cs = jnp.clip(j - NA_WIN_C // 2, 0, GRID_W - NA_WIN_C)
        ok = (j2 >= cs) & (j2 < cs + NA_WIN_C)
        left = lane < GRID_W
        neg = jnp.full((GRID_W, LANE), NEG, f32)
        for h in range(HEADS):
            lo, hi = [], []
            for dr in range(2 * NA_WIN_R - 1):
                row = jnp.broadcast_to(rpb_ref[h, dr:dr + 1, :], (GRID_W, LANE))
                lo.append(jnp.where(ok, pltpu.roll(row, LANE - (NA_WIN_C - 1), 1, stride=1, stride_axis=0), NEG))
                hi.append(jnp.where(ok, pltpu.roll(row, GRID_W - (NA_WIN_C - 1), 1, stride=1, stride_axis=0), NEG))
            for qi in range(rows):
                rs = min(max(qi - NA_WIN_R // 2, 0), rows - NA_WIN_R)
                for pair in range(rows // 2):
                    k_even, k_odd = 2 * pair, 2 * pair + 1
                    a = lo[k_even - qi + NA_WIN_R - 1] if rs <= k_even < rs + NA_WIN_R else neg
                    c = hi[k_odd - qi + NA_WIN_R - 1] if rs <= k_odd < rs + NA_WIN_R else neg
                    bias_scr[h, qi * GRID_W:(qi + 1) * GRID_W, pair * LANE:(pair + 1) * LANE] = jnp.where(left, a, c)

    @pl.when(i == 0)
    def _():
        kn = _seg_rms(k_ref[...].astype(f32), HEAD_W, kn_ref[...])
        kt_scr[:, 0:seq] = kn.T.astype(bf16)
        kt_scr[:, seq:] = kc_ref[...].T.astype(bf16)
        vx_scr[0:seq, :] = _with_ones(v_ref[...])
        vx_scr[seq:, :] = _with_ones(vc_ref[...])

    r0 = pl.multiple_of(i * qblk, qblk)
    qb = (_seg_rms(q_ref[...].astype(f32), HEAD_W, qn_ref[...]) * (HEAD_W ** -0.5)).astype(bf16)
    for h in range(HEADS):
        hl = slice(h * HEAD_W, (h + 1) * HEAD_W)
        xl = slice(2 * h * HEAD_W, 2 * (h + 1) * HEAD_W)
        s_own = jnp.dot(qb[:, hl], kt_scr[hl, 0:seq], preferred_element_type=f32) + bias_scr[h, pl.ds(r0, qblk), :]
        s_ctx = jnp.dot(qb[:, hl], kt_scr[hl, seq:], preferred_element_type=f32)
        o = _softmax_av([s_own, s_ctx], [vx_scr[0:seq, xl], vx_scr[seq:, xl]])
        o_ref[:, hl] = o.astype(o_ref.dtype)


def _na_sample(u, nb, layer, cache_k, cache_v, rpb_pad, qn_row, kn_row):
    seq = DEC_SEQ
    qblk = 256
    nq = seq // qblk
    cb = COL_NA // MIX_W
    full = lambda j: pl.BlockSpec((seq, MIX_W), lambda b, i: (b, j))
    ctx = pl.BlockSpec((None, None, PAST_LEN, MIX_W), lambda b, i: (b, layer, 0, 0))
    const = pl.BlockSpec((1, MIX_W), lambda b, i: (0, 0))
    return pl.pallas_call(
        functools.partial(_na_sample_body, qblk),
        grid=(nb, nq),
        in_specs=[pl.BlockSpec((qblk, MIX_W), lambda b, i: (b * nq + i, cb)), full(cb + 1), full(cb + 2),
                  ctx, ctx,
                  pl.BlockSpec((HEADS, 2 * NA_WIN_R, LANE), lambda b, i: (0, 0, 0)),
                  const, const],
        out_specs=pl.BlockSpec((qblk, MIX_W), lambda b, i: (b * nq + i, 0)),
        out_shape=jax.ShapeDtypeStruct((nb * seq, MIX_W), bf16),
        scratch_shapes=[pltpu.VMEM((MIX_W, seq + PAST_LEN), bf16), pltpu.VMEM((seq + PAST_LEN, 2 * MIX_W), bf16),
                        pltpu.VMEM((HEADS, seq, seq), f32)],
        compiler_params=_params(("arbitrary", "arbitrary")),
        name="na_sample",
    )(u, u, u, cache_k, cache_v, rpb_pad, qn_row, kn_row)


def _na_rpb_pad(rpb):
    c = rpb.shape[2]
    return jnp.pad(rpb.astype(f32), ((0, 0), (0, 1), (0, LANE - c)))


def _diff_lambda(lam_ref, lam_init):
    lv = lam_ref[...]
    s1 = jnp.sum(lv[0:1, :] * lv[1:2, :], axis=1, keepdims=True)
    s2 = jnp.sum(lv[2:3, :] * lv[3:4, :], axis=1, keepdims=True)
    return jnp.exp(s1) - jnp.exp(s2) + lam_init


def _diff_heads(q, kt, vext, lam):
    outs = []
    for h in range(HEADS):
        o = []
        for m in range(2):
            c0 = h * HEAD_W + m * DIFF_QK_DIM
            s = jnp.dot(q[:, c0:c0 + DIFF_QK_DIM], kt[c0:c0 + DIFF_QK_DIM, :], preferred_element_type=f32)
            o.append(_softmax_av([s], [vext[:, 2 * h * HEAD_W:2 * (h + 1) * HEAD_W]]))
        outs.append(o[0] - lam * o[1])
    return jnp.concatenate(outs, axis=1)


def _diff_prompt_body(lam_init, seq, layer, whole, q_ref, k_ref, v_ref, qn_ref, kn_ref, lam_ref, nw_ref, *rest):
    o_ref, ko_ref, vo_ref = rest[-3:]
    lam = _diff_lambda(lam_ref, lam_init)
    qb = (_seg_rms(q_ref[...].astype(f32), DIFF_QK_DIM, qn_ref[...]) * (DIFF_QK_DIM ** -0.5)).astype(bf16)
    kn = _seg_rms(k_ref[...].astype(f32), DIFF_QK_DIM, kn_ref[...])
    v = v_ref[...].astype(f32)
    for s0 in range(0, q_ref.shape[0], seq):
        rs = slice(s0, s0 + seq)
        _store_layer(ko_ref, s0 // seq, layer, whole, kn[rs])
        _store_layer(vo_ref, s0 // seq, layer, whole, v[rs])
        o = _diff_heads(qb[rs], kn[rs].T.astype(bf16), _with_ones(v[rs]), lam)
        o_ref[rs, :] = (_seg_rms(o, HEAD_W, nw_ref[...]) * (1.0 - lam_init)).astype(o_ref.dtype)


def _diff_prompt(u, nb, seq, lam_init, qn_row, kn_row, lam_vecs, nw_row, layer, prev):
    cb = COL_DIFF // MIX_W
    rows = PROMPT_BPS * seq
    blk = lambda j: pl.BlockSpec((rows, MIX_W), lambda b: (b, j))
    const = pl.BlockSpec((1, MIX_W), lambda b: (0, 0))
    shape, spec, whole = _layer_out(nb, PROMPT_BPS, (seq, MIX_W), layer, prev)
    extra = [] if prev is None else list(prev)
    return pl.pallas_call(
        functools.partial(_diff_prompt_body, lam_init, seq, layer, whole),
        grid=(nb // PROMPT_BPS,),
        in_specs=[blk(cb), blk(cb + 1), blk(cb + 2), const, const,
                  pl.BlockSpec((4, DIFF_QK_DIM), lambda b: (0, 0)), const] + [pl.BlockSpec(memory_space=pl.ANY)] * len(extra),
        out_specs=[blk(0), spec, spec],
        out_shape=[jax.ShapeDtypeStruct((nb * seq, MIX_W), bf16), shape, shape],
        input_output_aliases={} if prev is None else {7: 1, 8: 2},
        compiler_params=_params(("arbitrary",)),
        name="diff_prompt",
    )(u, u, u, qn_row, kn_row, lam_vecs, nw_row, *extra)


def _rope(x, cos, sin):
    w = x.shape[-1]
    up = pltpu.roll(x, w - 8, 1)
    dn = pltpu.roll(x, 8, 1)
    first = (lax.broadcasted_iota(jnp.int32, x.shape, 1) % 16) < 8
    return x * cos + jnp.where(first, up, dn) * sin


def _diff_sample_body(lam_init, qblk, q_ref, k_ref, v_ref, kc_ref, vc_ref, cos_ref, sin_ref, qn_ref, kn_ref,
                      lam_ref, nw_ref, o_ref, kt_scr, vx_scr):
    i = pl.program_id(1)
    seq = k_ref.shape[0]

    @pl.when(i == 0)
    def _():
        kn = _seg_rms(k_ref[...].astype(f32), DIFF_QK_DIM, kn_ref[...])
        kt_scr[:, 0:seq] = _rope(kn, cos_ref[...], sin_ref[...]).T.astype(bf16)
        kt_scr[:, seq:] = kc_ref[...].T.astype(bf16)
        vx_scr[0:seq, :] = _with_ones(v_ref[...])
        vx_scr[seq:, :] = _with_ones(vc_ref[...])

    lam = _diff_lambda(lam_ref, lam_init)
    r0 = pl.multiple_of(i * qblk, qblk)
    qn = _seg_rms(q_ref[...].astype(f32), DIFF_QK_DIM, qn_ref[...])
    qn = _rope(qn, cos_ref[pl.ds(r0, qblk), :], sin_ref[pl.ds(r0, qblk), :]) * (DIFF_QK_DIM ** -0.5)
    o = _diff_heads(qn.astype(bf16), kt_scr[...], vx_scr[...], lam)
    o_ref[...] = (_seg_rms(o, HEAD_W, nw_ref[...]) * (1.0 - lam_init)).astype(o_ref.dtype)


def _diff_sample(u, nb, layer, lam_init, cache_k, cache_v, cos, sin, qn_row, kn_row, lam_vecs, nw_row):
    seq = DEC_SEQ
    qblk = 512
    nq = seq // qblk
    cb = COL_DIFF // MIX_W
    full = lambda j: pl.BlockSpec((seq, MIX_W), lambda b, i: (b, j))
    ctx = pl.BlockSpec((None, None, PAST_LEN, MIX_W), lambda b, i: (b, layer, 0, 0))
    const = pl.BlockSpec((1, MIX_W), lambda b, i: (0, 0))
    tab = pl.BlockSpec((seq, MIX_W), lambda b, i: (0, 0))
    return pl.pallas_call(
        functools.partial(_diff_sample_body, lam_init, qblk),
        grid=(nb, nq),
        in_specs=[pl.BlockSpec((qblk, MIX_W), lambda b, i: (b * nq + i, cb)), full(cb + 1), full(cb + 2),
                  ctx, ctx, tab, tab, const, const,
                  pl.BlockSpec((4, DIFF_QK_DIM), lambda b, i: (0, 0)), const],
        out_specs=pl.BlockSpec((qblk, MIX_W), lambda b, i: (b * nq + i, 0)),
        out_shape=jax.ShapeDtypeStruct((nb * seq, MIX_W), bf16),
        scratch_shapes=[pltpu.VMEM((MIX_W, seq + PAST_LEN), bf16), pltpu.VMEM((seq + PAST_LEN, 2 * MIX_W), bf16)],
        compiler_params=_params(("arbitrary", "arbitrary")),
        name="diff_sample",
    )(u, u, u, cache_k, cache_v, cos, sin, qn_row, kn_row, lam_vecs, nw_row)


def _rope_tables():
    t = jnp.arange(DEC_SEQ)
    nf = DIFF_QK_DIM // 4
    inv = ROPE_BASE ** (-jnp.arange(nf, dtype=f32) / nf)
    ar = (t // GRID_W).astype(f32)[:, None] * inv
    ac = (t % GRID_W).astype(f32)[:, None] * inv
    cos = jnp.concatenate([jnp.cos(ar), jnp.cos(ar), jnp.cos(ac), jnp.cos(ac)], axis=1)
    sin = jnp.concatenate([-jnp.sin(ar), jnp.sin(ar), -jnp.sin(ac), jnp.sin(ac)], axis=1)
    reps = MIX_W // DIFF_QK_DIM
    return jnp.tile(cos, (1, reps)), jnp.tile(sin, (1, reps))


def _merge_body(x_ref, a_ref, b_ref, c_ref, d_ref, gp_ref, mod_ref, wb_ref, wo_ref, nw_ref, wr_ref,
                xo_ref, h2_ref, aff_ref):
    mixed = None
    for n, br in enumerate((a_ref, b_ref, c_ref, d_ref)):
        gate = _sigmoid(gp_ref[:, n * D_MODEL:(n + 1) * D_MODEL].astype(f32))
        t = gate * jnp.dot(br[...].astype(bf16), wb_ref[n], preferred_element_type=f32)
        mixed = t if mixed is None else mixed + t
    y = jnp.dot(mixed.astype(bf16), wo_ref[...], preferred_element_type=f32)
    x = x_ref[...] + mod_ref[:, 2 * D_MODEL:3 * D_MODEL] * y
    xo_ref[...] = x
    h = x * lax.rsqrt(jnp.mean(x * x, axis=-1, keepdims=True) + EPS) * nw_ref[...]
    h = h * (1.0 + mod_ref[:, 4 * D_MODEL:5 * D_MODEL]) + mod_ref[:, 3 * D_MODEL:4 * D_MODEL]
    hb = h.astype(bf16)
    h2_ref[...] = hb
    logits = jnp.dot(hb, wr_ref[...], preferred_element_type=f32)
    lane = lax.broadcasted_iota(jnp.int32, logits.shape, 1)
    logits = jnp.where(lane < N_EXPERTS, logits, NEG)
    e = jnp.exp(logits - logits.max(axis=1, keepdims=True))
    aff_ref[...] = e / e.sum(axis=1, keepdims=True)


def _merge(x, brs, u, mod3, mod_base, mod_stride, seq, wb, wo, nw_row, wr):
    n = x.shape[0]
    tm = 512
    tps = max(seq // tm, 1)
    row = lambda w: pl.BlockSpec((tm, w), lambda i: (i, 0))
    return pl.pallas_call(
        _merge_body,
        grid=(n // tm,),
        in_specs=[row(D_MODEL), row(MIX_W), row(MIX_W), row(MIX_W), row(MIX_W),
                  pl.BlockSpec((tm, N_BRANCH * D_MODEL), lambda i: (i, COL_GPRE // (N_BRANCH * D_MODEL))),
                  pl.BlockSpec((None, 1, 6 * D_MODEL), lambda i: (mod_base + mod_stride * (i // tps), 0, 0)),
                  pl.BlockSpec((N_BRANCH, MIX_W, D_MODEL), lambda i: (0, 0, 0)),
                  pl.BlockSpec((D_MODEL, D_MODEL), lambda i: (0, 0)),
                  pl.BlockSpec((1, D_MODEL), lambda i: (0, 0)),
                  pl.BlockSpec((D_MODEL, LANE), lambda i: (0, 0))],
        out_specs=[row(D_MODEL), row(D_MODEL), row(LANE)],
        out_shape=[jax.ShapeDtypeStruct((n, D_MODEL), f32), jax.ShapeDtypeStruct((n, D_MODEL), bf16),
                   jax.ShapeDtypeStruct((n, LANE), f32)],
        compiler_params=_params(("arbitrary",)),
        name="merge",
    )(x, *brs, u, mod3, wb, wo, nw_row, wr)


def _rows_prefix_sum(x):
    n = x.shape[0]
    pos = lax.broadcasted_iota(jnp.int32, x.shape, 0)
    k = 1
    while k < n:
        x = x + jnp.where(pos >= k, pltpu.roll(x, k, 0), 0.0)
        k *= 2
    return x


def _route_body(t, cap, bpg, aff_ref, h2_ref, xg_ref, g_ref, rank_ref, rrow_scr, arow_scr, oh_scr):
    bb = pl.program_id(1)

    @pl.when(bb == 0)
    def _():
        aff = aff_ref[...]
        tok = lax.broadcasted_iota(jnp.int32, aff.shape, 0)
        val = aff
        k = 2
        while k <= t:
            j = k // 2
            while j >= 1:
                lower = (tok & j) == 0
                pv = jnp.where(lower, pltpu.roll(val, t - j, 0), pltpu.roll(val, j, 0))
                val = jnp.where(lower == ((tok & k) == 0), jnp.maximum(val, pv), jnp.minimum(val, pv))
                j //= 2
            k *= 2
        thr = val[cap - 1:cap, :]
        above = aff > thr
        tie = jnp.where(aff == thr, 1.0, 0.0)
        need = cap - jnp.sum(jnp.where(above, 1.0, 0.0), axis=0, keepdims=True)
        sel = jnp.where(above | ((tie > 0.0) & (_rows_prefix_sum(tie) - tie < need)), 1.0, 0.0)
        rank = jnp.where(sel > 0.0, _rows_prefix_sum(sel) - sel, float(cap))
        rank_ref[...] = rank
        rank_t = rank.T
        aff_t = aff.T
        for k in range(bpg):
            rrow_scr[k] = rank_t[k * N_EXPERTS:(k + 1) * N_EXPERTS, :]
            arow_scr[k] = aff_t[k * N_EXPERTS:(k + 1) * N_EXPERTS, :]

    rk = rrow_scr[bb]
    ar = arow_scr[bb]
    slot = lax.broadcasted_iota(jnp.int32, (cap, t), 0).astype(f32)
    for e in range(N_EXPERTS):
        onehot = jnp.where(rk[e:e + 1, :] == slot, 1.0, 0.0)
        oh_scr[e * cap:(e + 1) * cap, :] = onehot.astype(bf16)
        g_ref[e] = jnp.sum(onehot * ar[e:e + 1, :], axis=1, keepdims=True)
    h2 = h2_ref[...]
    epb = max(1, 512 // cap)
    for e0 in range(0, N_EXPERTS, epb):
        rows = jnp.dot(oh_scr[e0 * cap:(e0 + epb) * cap, :], h2, preferred_element_type=f32)
        xg_ref[e0:e0 + epb] = rows.astype(bf16).reshape(epb, cap, D_MODEL)


def _group_lanes(aff, nb, t, bpg):
    a = aff[:, :N_EXPERTS].reshape(nb // bpg, bpg, t, N_EXPERTS)
    a = jnp.transpose(a, (0, 2, 1, 3)).reshape(nb // bpg * t, bpg * N_EXPERTS)
    return jnp.pad(a, ((0, 0), (0, LANE - bpg * N_EXPERTS)))


def _route(aff, h2, nb, t, bpg):
    cap = CAPACITY_FACTOR * t // N_EXPERTS
    return pl.pallas_call(
        functools.partial(_route_body, t, cap, bpg),
        grid=(nb // bpg, bpg),
        in_specs=[pl.BlockSpec((t, LANE), lambda g, k: (g, 0)),
                  pl.BlockSpec((t, D_MODEL), lambda g, k: (g * bpg + k, 0))],
        out_specs=[pl.BlockSpec((N_EXPERTS, cap, D_MODEL), lambda g, k: (0, g * bpg + k, 0)),
                   pl.BlockSpec((N_EXPERTS, cap, 1), lambda g, k: (0, g * bpg + k, 0)),
                   pl.BlockSpec((t, LANE), lambda g, k: (g, 0))],
        out_shape=[jax.ShapeDtypeStruct((N_EXPERTS, nb * cap, D_MODEL), bf16),
                   jax.ShapeDtypeStruct((N_EXPERTS, nb * cap, 1), f32),
                   jax.ShapeDtypeStruct((nb // bpg * t, LANE), f32)],
        scratch_shapes=[pltpu.VMEM((bpg, N_EXPERTS, t), f32), pltpu.VMEM((bpg, N_EXPERTS, t), f32),
                        pltpu.VMEM((N_EXPERTS * cap, t), bf16)],
        compiler_params=_params(("arbitrary", "arbitrary")),
        name=f"route_{t}",
    )(_group_lanes(aff, nb, t, bpg), h2)


def _ffn_body(xp_ref, xs_ref, gp_ref, gs_ref, wg_ref, wu_ref, wd_ref, yp_ref, ys_ref, wg_scr, wu_scr, wd_scr):
    wg_scr[...] = wg_ref[...].astype(bf16)
    wu_scr[...] = wu_ref[...].astype(bf16)
    wd_scr[...] = wd_ref[...].astype(bf16)
    blk = 512
    for x_ref, g_ref, y_ref in ((xp_ref, gp_ref, yp_ref), (xs_ref, gs_ref, ys_ref)):
        def step(i, carry, x_ref=x_ref, g_ref=g_ref, y_ref=y_ref):
            r0 = pl.multiple_of(i * blk, blk)
            x = x_ref[pl.ds(r0, blk), :]
            a = jnp.dot(x, wg_scr[...], preferred_element_type=f32)
            up = jnp.dot(x, wu_scr[...], preferred_element_type=f32)
            mid = (jax.nn.silu(a) * up).astype(bf16)
            y = jnp.dot(mid, wd_scr[...], preferred_element_type=f32)
            y_ref[pl.ds(r0, blk), :] = (y * g_ref[pl.ds(r0, blk), :]).astype(bf16)
            return carry
        lax.fori_loop(0, x_ref.shape[0] // blk, step, 0)


def _ffn(xp, xs, gp, gs, w_gate, w_up, w_down, layer):
    np_, ns = xp.shape[1], xs.shape[1]
    tok = lambda n, w: pl.BlockSpec((None, n, w), lambda e: (e, 0, 0))
    wspec = lambda a, b: pl.BlockSpec((None, None, a, b), lambda e: (layer, e, 0, 0))
    return pl.pallas_call(
        _ffn_body,
        grid=(N_EXPERTS,),
        in_specs=[tok(np_, D_MODEL), tok(ns, D_MODEL), tok(np_, 1), tok(ns, 1),
                  wspec(D_MODEL, EXPERT_FF), wspec(D_MODEL, EXPERT_FF), wspec(EXPERT_FF, D_MODEL)],
        out_specs=[tok(np_, D_MODEL), tok(ns, D_MODEL)],
        out_shape=[jax.ShapeDtypeStruct(xp.shape, bf16), jax.ShapeDtypeStruct(xs.shape, bf16)],
        scratch_shapes=[pltpu.VMEM((D_MODEL, EXPERT_FF), bf16), pltpu.VMEM((D_MODEL, EXPERT_FF), bf16),
                        pltpu.VMEM((EXPERT_FF, D_MODEL), bf16)],
        compiler_params=_params(("arbitrary",)),
        name="expert_ffn",
    )(xp, xs, gp, gs, w_gate, w_up, w_down)


def _combine_body(t, cap, bpg, bps, x_ref, rank_ref, y_ref, mod_ref, o_ref):
    k = N_EXPERTS * cap
    er = lax.broadcasted_iota(jnp.int32, (LANE, k), 0)
    ec = lax.broadcasted_iota(jnp.int32, (LANE, k), 1) // cap
    slot = (lax.broadcasted_iota(jnp.int32, (t, k), 1) % cap).astype(f32)
    rank = rank_ref[...].astype(bf16)
    for sub in range(bps):
        col0 = ((pl.program_id(0) * bps + sub) % bpg) * N_EXPERTS
        expand = jnp.where(er == ec + col0, 1.0, 0.0).astype(bf16)
        rexp = jnp.dot(rank, expand, preferred_element_type=f32)
        pt = jnp.where(rexp == slot, 1.0, 0.0).astype(bf16)
        y = y_ref[:, sub * cap:(sub + 1) * cap, :].reshape(k, D_MODEL)
        moe = jnp.dot(pt, y, preferred_element_type=f32)
        rs = slice(sub * t, (sub + 1) * t)
        o_ref[rs, :] = x_ref[rs, :] + mod_ref[:, 5 * D_MODEL:6 * D_MODEL] * moe


def _combine(x, rank, y, mod3, mod_base, mod_stride, nb, t, bpg):
    cap = CAPACITY_FACTOR * t // N_EXPERTS
    bps = 2 if (mod_stride == 0 and bpg % 2 == 0) else 1
    return pl.pallas_call(
        functools.partial(_combine_body, t, cap, bpg, bps),
        grid=(nb // bps,),
        in_specs=[pl.BlockSpec((bps * t, D_MODEL), lambda b: (b, 0)),
                  pl.BlockSpec((t, LANE), lambda b: (b * bps // bpg, 0)),
                  pl.BlockSpec((N_EXPERTS, bps * cap, D_MODEL), lambda b: (0, b, 0)),
                  pl.BlockSpec((None, 1, 6 * D_MODEL), lambda b: (mod_base + mod_stride * b, 0, 0))],
        out_specs=pl.BlockSpec((bps * t, D_MODEL), lambda b: (b, 0)),
        out_shape=jax.ShapeDtypeStruct(x.shape, f32),
        compiler_params=_params(("arbitrary",)),
        name=f"combine_{t}",
    )(x, rank, y, mod3)


def _pad_row(v, off, width=LANE):
    return jnp.zeros((1, width), f32).at[0, off:off + v.shape[0]].set(v.astype(f32))


def _layer_params(l, a):
    w_in = a["w_in"][l]
    pts, acc = [], 0
    splits = (MIX_W,) * 4 + (4 * HEADS,) + (MIX_W,) * 7 + (SSM_CONV_CH, 2 * HEADS, N_BRANCH * D_MODEL)
    for s in splits:
        pts.append((acc, acc + s))
        acc += s
    seg = lambda i: w_in[:, pts[i][0]:pts[i][1]]
    order = [14, 0, 1, 2, 3, 5, 6, 7, 8, 9, 10, 12, 11]
    w_main = jnp.concatenate([seg(i) for i in order] + [jnp.zeros((D_MODEL, U_COLS - U_USED), f32)],
                             axis=1).astype(bf16)
    w_side = jnp.concatenate([seg(4), seg(13), jnp.zeros((D_MODEL, LANE - 24), f32)], axis=1).astype(bf16)
    tile = lambda v, reps: jnp.tile(v.astype(f32), reps).reshape(1, -1)
    p = dict(
        w_main=w_main, w_side=w_side,
        norm1=a["norm1_w"][l].reshape(1, -1), norm2=a["norm2_w"][l].reshape(1, -1),
        mlstm_gb=_pad_row(a["mlstm_gate_b"][l].reshape(-1), G_MLSTM),
        mlstm_nw=a["mlstm_norm_w"][l].reshape(1, -1),
        na_qn=tile(a["na_qnorm_w"][l], HEADS), na_kn=tile(a["na_knorm_w"][l], HEADS),
        na_rpb=_na_rpb_pad(a["na_rpb"][l]),
        diff_qn=tile(a["diff_qnorm_w"][l], 2 * HEADS), diff_kn=tile(a["diff_knorm_w"][l], 2 * HEADS),
        diff_lam=a["diff_lambda"][l], diff_nw=tile(a["diff_norm_w"][l], HEADS),
        ssm_cw=a["ssm_conv_w"][l], ssm_cb=a["ssm_conv_b"][l].reshape(1, -1),
        ssm_dtb=_pad_row(a["ssm_dt_bias"][l].reshape(-1), G_DT),
        ssm_alog=_pad_row(a["ssm_a_log"][l].reshape(-1), G_DT),
        ssm_d=jnp.repeat(a["ssm_d"][l].astype(f32), HEAD_W).reshape(1, -1),
        ssm_nw=a["ssm_norm_w"][l].reshape(1, -1),
        wb=a["w_branch"][l].astype(bf16), wo=a["w_out"][l].astype(bf16),
        wr=jnp.concatenate([a["w_router"][l], jnp.zeros((D_MODEL, LANE - N_EXPERTS), f32)], axis=1).astype(bf16),
        w_gate=a["w_gate"], w_up=a["w_up"], w_down=a["w_down"],
    )
    return p


def _mixers(u, g, nb, seq, p, l, lam_init, ctx, tables, caches):
    if ctx is None:
        a_out, m_c, m_n, m_m = _mlstm(u, g, nb, seq, p["mlstm_gb"], p["mlstm_nw"], None,
                                      l, None if caches is None else caches[4:7])
        b_out, na_k, na_v = _na_prompt(u, nb, seq, p["na_qn"], p["na_kn"], l, None if caches is None else caches[0:2])
        c_out, df_k, df_v = _diff_prompt(u, nb, seq, lam_init, p["diff_qn"], p["diff_kn"], p["diff_lam"], p["diff_nw"],
                                         l, None if caches is None else caches[2:4])
        d_out, s_h = _ssd(u, g, nb, seq, p["ssm_cw"], p["ssm_cb"], p["ssm_dtb"], p["ssm_alog"], p["ssm_d"],
                          p["ssm_nw"], None, l, None if caches is None else caches[7])
        return (a_out, b_out, c_out, d_out), (na_k, na_v, df_k, df_v, m_c, m_n, m_m, s_h)
    (na_ck, na_cv, df_ck, df_cv, st_c, st_n, st_m, st_s) = ctx
    cos, sin = tables
    a_out = _mlstm(u, g, nb, seq, p["mlstm_gb"], p["mlstm_nw"], (st_c[:, l], st_n[:, l], st_m[:, l]))[0]
    b_out = _na_sample(u, nb, l, na_ck, na_cv, p["na_rpb"], p["na_qn"], p["na_kn"])
    c_out = _diff_sample(u, nb, l, lam_init, df_ck, df_cv, cos, sin, p["diff_qn"], p["diff_kn"], p["diff_lam"],
                         p["diff_nw"])
    d_out = _ssd(u, g, nb, seq, p["ssm_cw"], p["ssm_cb"], p["ssm_dtb"], p["ssm_alog"], p["ssm_d"], p["ssm_nw"],
                 st_s[:, l])[0]
    return (a_out, b_out, c_out, d_out), None


def kernel(x_prompt, x_sample, cache_na_k, cache_na_v, cache_diff_k, cache_diff_v, state_mlstm_C, state_mlstm_n, state_mlstm_m, state_ssm, c, c_ctx, norm1_w, norm2_w, w_ada, b_ada, w_in, mlstm_gate_b, mlstm_norm_w, na_qnorm_w, na_knorm_w, na_rpb, diff_qnorm_w, diff_knorm_w, diff_lambda, diff_norm_w, ssm_conv_w, ssm_conv_b, ssm_dt_bias, ssm_a_log, ssm_d, ssm_norm_w, w_branch, w_out, w_router, w_gate, w_up, w_down):
    a = dict(norm1_w=norm1_w, norm2_w=norm2_w, w_in=w_in, mlstm_gate_b=mlstm_gate_b, mlstm_norm_w=mlstm_norm_w,
             na_qnorm_w=na_qnorm_w, na_knorm_w=na_knorm_w, na_rpb=na_rpb, diff_qnorm_w=diff_qnorm_w,
             diff_knorm_w=diff_knorm_w, diff_lambda=diff_lambda, diff_norm_w=diff_norm_w, ssm_conv_w=ssm_conv_w,
             ssm_conv_b=ssm_conv_b, ssm_dt_bias=ssm_dt_bias, ssm_a_log=ssm_a_log, ssm_d=ssm_d,
             ssm_norm_w=ssm_norm_w, w_branch=w_branch, w_out=w_out, w_router=w_router, w_gate=w_gate, w_up=w_up,
             w_down=w_down)
    nb_p, nb_s = x_prompt.shape[0], x_sample.shape[0]
    xp = x_prompt.reshape(nb_p * SEQ, D_MODEL)
    xs = x_sample.reshape(nb_s * DEC_SEQ, D_MODEL)
    cc = jnp.concatenate([c_ctx[None, :], c, jnp.zeros((8 - 1 - nb_s, D_MODEL), f32)], axis=0)
    ctx = (cache_na_k.reshape(nb_s, DEPTH, PAST_LEN, MIX_W), cache_na_v.reshape(nb_s, DEPTH, PAST_LEN, MIX_W),
           cache_diff_k.reshape(nb_s, DEPTH, PAST_LEN, MIX_W), cache_diff_v.reshape(nb_s, DEPTH, PAST_LEN, MIX_W),
           state_mlstm_C, state_mlstm_n, state_mlstm_m, state_ssm)
    tables = _rope_tables()
    caches = None
    for l in range(DEPTH):
        p = _layer_params(l, a)
        lam_init = 0.8 - 0.6 * math.exp(-0.3 * l)
        mod3 = _modulation(cc, w_ada, b_ada, l).reshape(8, 1, 6 * D_MODEL)
        groups = []
        for (x, nb, seq, base, stride, gctx) in ((xp, nb_p, SEQ, 0, 0, None), (xs, nb_s, DEC_SEQ, 1, 1, ctx)):
            u, g = _inproj(x, mod3, base, stride, seq, p["norm1"], p["w_main"], p["w_side"])
            brs, st = _mixers(u, g, nb, seq, p, l, lam_init, gctx, tables, caches)
            if st is not None:
                caches = st
            x1, h2, aff = _merge(x, brs, u, mod3, base, stride, seq, p["wb"], p["wo"], p["norm2"], p["wr"])
            bpg = min(nb, LANE // N_EXPERTS)
            xg, gg, rank = _route(aff, h2, nb, seq, bpg)
            groups.append((x1, xg, gg, rank, nb, seq, base, stride, bpg))
        yp, ys = _ffn(groups[0][1], groups[1][1], groups[0][2], groups[1][2], p["w_gate"], p["w_up"], p["w_down"], l)
        outs = []
        for (x1, _, _, rank, nb, seq, base, stride, bpg), y in zip(groups, (yp, ys)):
            outs.append(_combine(x1, rank, y, mod3, base, stride, nb, seq, bpg))
        xp, xs = outs
    kv = (nb_p, DEPTH, SEQ, HEADS, HEAD_W)
    return (xp.reshape(x_prompt.shape), xs.reshape(x_sample.shape),
            caches[0].reshape(kv), caches[1].reshape(kv), caches[2].reshape(kv), caches[3].reshape(kv),
            caches[4], caches[5], caches[6], caches[7])
```

```python
import functools
import math

import jax
import jax.numpy as jnp
from jax import lax
from jax.experimental import pallas as pl
from jax.experimental.pallas import tpu as pltpu

f32 = jnp.float32
bf16 = jnp.bfloat16

D_MODEL = 1024
BATCH = 32
SEQ = 256
DEPTH = 2
DEC_BATCH = 4
DEC_SEQ = 1024
PAST_LEN = 512
GRID_W = 64
MIX_W = D_MODEL // 4
N_BRANCH = 4
HEADS = 4
HEAD_W = MIX_W // HEADS
NA_WIN_R = 8
NA_WIN_C = 16
DIFF_QK_DIM = HEAD_W // 2
SSM_GROUPS = 2
SSM_STATE = 64
SSM_CONV_K = 5
SSM_CONV_CH = MIX_W + 2 * SSM_GROUPS * SSM_STATE
CHUNK = 64
PROMPT_BPS = 2
SCAN_BPS = 2
SCAN_UNROLL = 8
N_EXPERTS = 16
EXPERT_FF = 1024
CAPACITY_FACTOR = 2
ROPE_BASE = 10000.0
EPS = 1e-6
NEG = -1e30

LANE = 128
VMEM_LIMIT = 56 * 1024 * 1024

COL_GPRE = 0
COL_MLSTM = 4096
COL_NA = 5120
COL_DIFF = 5888
COL_XBC = 6656
COL_Z = 7168
U_USED = 7424
U_COLS = 7680
G_MLSTM = 0
G_DT = 16


def _params(sem):
    return pltpu.CompilerParams(dimension_semantics=sem, vmem_limit_bytes=VMEM_LIMIT)


def _mm(a, b):
    return jnp.dot(a.astype(bf16), b.astype(bf16), preferred_element_type=f32)


def _mm_nt(a, b):
    return lax.dot_general(a.astype(bf16), b.astype(bf16), (((1,), (1,)), ((), ())), preferred_element_type=f32)


def _mm_tn(a, b):
    return lax.dot_general(a.astype(bf16), b.astype(bf16), (((0,), (0,)), ((), ())), preferred_element_type=f32)


def _sigmoid(x):
    return 0.5 * jnp.tanh(0.5 * x) + 0.5


def _seg_sum(x, seg):
    w = x.shape[-1]
    r = lax.broadcasted_iota(jnp.int32, (w, w), 0) // seg
    c = lax.broadcasted_iota(jnp.int32, (w, w), 1) // seg
    ones = jnp.where(r == c, 1.0, 0.0).astype(bf16)
    hi = x.astype(bf16)
    lo = (x - hi.astype(f32)).astype(bf16)
    return jnp.dot(hi, ones, preferred_element_type=f32) + jnp.dot(lo, ones, preferred_element_type=f32)


def _seg_rms(x, seg, w_row):
    ms = _seg_sum(x * x, seg) * (1.0 / seg)
    return x * lax.rsqrt(ms + EPS) * w_row


def _mod_body(c_ref, w_ref, b_ref, o_ref):
    o_ref[...] = _mm(jax.nn.silu(c_ref[...]), w_ref[...]) + b_ref[...]


def _modulation(cc, w_ada, b_ada, layer):
    tn = 1536
    n = w_ada.shape[2]
    return pl.pallas_call(
        _mod_body,
        grid=(n // tn,),
        in_specs=[pl.BlockSpec((8, D_MODEL), lambda j: (0, 0)),
                  pl.BlockSpec((None, D_MODEL, tn), lambda j: (layer, 0, j)),
                  pl.BlockSpec((None, 1, tn), lambda j: (layer, 0, j))],
        out_specs=pl.BlockSpec((8, tn), lambda j: (0, j)),
        out_shape=jax.ShapeDtypeStruct((8, n), f32),
        compiler_params=_params(("arbitrary",)),
        name="modulation",
    )(cc, w_ada, b_ada.reshape(b_ada.shape[0], 1, n))


def _inproj_body(ncol, x_ref, mod_ref, nw_ref, w_ref, wg_ref, u_ref, g_ref):
    x = x_ref[...]
    y = x * lax.rsqrt(jnp.mean(x * x, axis=-1, keepdims=True) + EPS) * nw_ref[...]
    hb = (y * (1.0 + mod_ref[:, D_MODEL:2 * D_MODEL]) + mod_ref[:, 0:D_MODEL]).astype(bf16)
    g_ref[...] = jnp.dot(hb, wg_ref[...], preferred_element_type=f32)
    cw = U_COLS // ncol
    for c in range(ncol):
        u_ref[:, c * cw:(c + 1) * cw] = jnp.dot(hb, w_ref[:, c * cw:(c + 1) * cw],
                                                preferred_element_type=f32).astype(u_ref.dtype)


def _inproj(x, mod3, mod_base, mod_stride, seq, nw_row, w_main, w_side):
    n = x.shape[0]
    tm = 512
    tps = max(seq // tm, 1)
    resident = lambda shape: pl.BlockSpec(shape, lambda i: (0, 0), pipeline_mode=pl.Buffered(1))
    return pl.pallas_call(
        functools.partial(_inproj_body, 3),
        grid=(n // tm,),
        in_specs=[pl.BlockSpec((tm, D_MODEL), lambda i: (i, 0)),
                  pl.BlockSpec((None, 1, 6 * D_MODEL), lambda i: (mod_base + mod_stride * (i // tps), 0, 0)),
                  pl.BlockSpec((1, D_MODEL), lambda i: (0, 0)),
                  resident((D_MODEL, U_COLS)), resident((D_MODEL, LANE))],
        out_specs=[pl.BlockSpec((tm, U_COLS), lambda i: (i, 0)),
                   pl.BlockSpec((tm, LANE), lambda i: (i, 0))],
        out_shape=[jax.ShapeDtypeStruct((n, U_COLS), bf16), jax.ShapeDtypeStruct((n, LANE), f32)],
        compiler_params=_params(("arbitrary",)),
        name="inproj",
    )(x, mod3, nw_row, w_main, w_side)


def _split3(x):
    hi = x.astype(bf16)
    r = x - hi.astype(f32)
    mid = r.astype(bf16)
    lo = (r - mid.astype(f32)).astype(bf16)
    return hi, mid, lo


def _select_cols(x, onehot):
    return jnp.dot(jnp.concatenate(_split3(x), axis=1), jnp.concatenate([onehot] * 3, axis=0),
                   preferred_element_type=f32)


def _select_rows(onehot, x):
    return jnp.dot(jnp.concatenate([onehot] * 3, axis=1), jnp.concatenate(_split3(x), axis=0),
                   preferred_element_type=f32)


def _chunk_scan(x, op, identity, reverse):
    n = x.shape[0]
    pos = lax.broadcasted_iota(jnp.int32, x.shape, 0) % CHUNK
    k = 1
    while k < CHUNK:
        if reverse:
            shifted, ok = pltpu.roll(x, n - k, 0), pos < CHUNK - k
        else:
            shifted, ok = pltpu.roll(x, k, 0), pos >= k
        x = op(x, jnp.where(ok, shifted, identity))
        k *= 2
    return x


def _dir_scan(x, op, identity, bwd_col):
    return jnp.where(bwd_col, _chunk_scan(x, op, identity, True), _chunk_scan(x, op, identity, False))


def _head_expand(col0):
    c = lax.broadcasted_iota(jnp.int32, (LANE, MIX_W), 0)
    h = lax.broadcasted_iota(jnp.int32, (LANE, MIX_W), 1) // HEAD_W
    return jnp.where(c == col0 + h, 1.0, 0.0).astype(bf16)


def _chunk_rows(x, seq):
    nrow = max(seq // CHUNK, 8)
    t = lax.broadcasted_iota(jnp.int32, (seq, MIX_W), 0) % CHUNK
    s_ = lax.broadcasted_iota(jnp.int32, (seq, MIX_W), 1) % HEAD_W
    sel = lax.broadcasted_iota(jnp.int32, (nrow, seq), 1) // CHUNK == lax.broadcasted_iota(jnp.int32, (nrow, seq), 0)
    return _select_rows(jnp.where(sel, 1.0, 0.0).astype(bf16), jnp.where(t == s_, x, 0.0))


def _block_diag(x, reps, keep):
    return jnp.concatenate([x] * reps, axis=0) * keep


def _head_mask(rows, cols, rseg, cseg):
    return (lax.broadcasted_iota(jnp.int32, (rows, cols), 0) // rseg) == (lax.broadcasted_iota(jnp.int32, (rows, cols), 1) // cseg)


def _layer_view(ref, sub, layer, whole):
    if not whole:
        return ref.at[sub]
    for l2 in range(DEPTH):
        if l2 != layer:
            ref[sub, l2] = jnp.zeros(ref.shape[2:], f32)
    return ref.at[sub, layer]


def _mlstm_body(seq, bps, has_state, layer, n_prev, *refs):
    if has_state:
        (qkvo_ref, g_ref, gb_ref, nw_ref, c0_ref, n0_ref, m0_ref,
         out_ref, cs_ref, ns_ref, ms_ref, bx_scr, mx_scr, vx_scr, vrow_scr, cbd_scr, hf_scr, hb_scr) = refs
    else:
        qkvo_ref, g_ref, gb_ref, nw_ref = refs[:4]
        (out_ref, cs_ref, ns_ref, ms_ref, bx_scr, mx_scr, vx_scr, vrow_scr, cbd_scr, hf_scr,
         hb_scr) = refs[4 + n_prev:]
    whole = None if has_state else n_prev == 0
    nc = seq // CHUNK
    tot = bps * seq
    g = g_ref[...] + gb_ref[...]
    lane = lax.broadcasted_iota(jnp.int32, g.shape, 1)
    bwd_col = (lane % 16) >= 8
    bsum = _dir_scan(jax.nn.log_sigmoid(g), jnp.add, 0.0, bwd_col)
    b_i = pltpu.roll(bsum, LANE - HEADS, 1)
    vcol = g - b_i
    mcol = b_i + _dir_scan(vcol, jnp.maximum, NEG, bwd_col)
    for d in range(2):
        e = _head_expand(G_MLSTM + d * 8)
        bx_scr[d] = _select_cols(b_i, e)
        mx_scr[d] = _select_cols(mcol, e)
        vx = _select_cols(vcol, e)
        vx_scr[d] = vx
        vrow = _chunk_rows(vx, tot)
        for c in range(bps * nc):
            vrow_scr[d, c] = vrow[c:c + 1, :]

    grp = lax.broadcasted_iota(jnp.int32, (1, MIX_W), 1) // HEAD_W
    cbd_scr[...] = jnp.zeros(cbd_scr.shape, f32)
    init = []
    for sub in range(bps):
        for d in range(2):
            if has_state:
                for h in range(HEADS):
                    cbd_scr[2 * sub + d, h * HEAD_W:(h + 1) * HEAD_W, h * HEAD_W:(h + 1) * HEAD_W] = c0_ref[sub, d, h]
                init.append(jnp.concatenate([n0_ref[sub, d, h:h + 1, :] for h in range(HEADS)], axis=1))
                m_row = jnp.zeros((1, MIX_W), f32)
                for h in range(HEADS):
                    m_row = jnp.where(grp == h, m0_ref[sub, d:d + 1, h:h + 1], m_row)
                init.append(m_row)
            else:
                init += [jnp.zeros((1, MIX_W), f32), jnp.zeros((1, MIX_W), f32)]

    li = lax.broadcasted_iota(jnp.int32, (CHUNK, MIX_W), 0)
    si = lax.broadcasted_iota(jnp.int32, (CHUNK, MIX_W), 1) % HEAD_W
    valid = (si <= li, si >= li)
    bd = _head_mask(MIX_W, MIX_W, HEAD_W, HEAD_W)
    bd_f32 = jnp.where(bd, 1.0, 0.0)
    ones_bd = bd_f32.astype(bf16)

    def step(c, carry):
        new = []
        for sd in range(2 * bps):
            sub, d = sd // 2, sd % 2
            n_row, m_row = carry[2 * sd], carry[2 * sd + 1]
            cidx = sub * nc + (c if d == 0 else nc - 1 - c)
            r0 = pl.multiple_of(cidx * CHUNK, CHUNK)
            last = CHUNK - 1 if d == 0 else 0
            q = qkvo_ref[pl.ds(r0, CHUNK), 0:MIX_W].astype(f32)
            k = qkvo_ref[pl.ds(r0, CHUNK), MIX_W:2 * MIX_W].astype(f32) * (HEAD_W ** -0.5)
            vb = qkvo_ref[pl.ds(r0, CHUNK), 2 * MIX_W:3 * MIX_W].astype(bf16)
            qb = q.astype(bf16)
            bx = bx_scr[d, pl.ds(r0, CHUNK), :]
            mx = mx_scr[d, pl.ds(r0, CHUNK), :]
            vx = vx_scr[d, pl.ds(r0, CHUNK), :]
            p = jnp.exp(jnp.where(valid[d], bx + vrow_scr[d, cidx] - mx, NEG))
            smat = _mm_nt(qb, _block_diag(k.astype(bf16), HEADS, ones_bd)) * p
            intra = _mm(smat, _block_diag(vb, HEADS, ones_bd))
            dsum = _mm(smat, ones_bd)
            gg = bx + m_row
            mt = jnp.maximum(gg, mx)
            a = jnp.exp(mx - mt)
            w_prev = jnp.exp(gg - mt)
            inter = _mm(qb, cbd_scr[sd])
            dint = _mm(q * n_row, ones_bd)
            num = intra * a + inter * w_prev
            den = dsum * a + dint * w_prev
            h_scr = hf_scr if d == 0 else hb_scr
            h_scr[pl.ds(r0, CHUNK), :] = num / jnp.maximum(jnp.abs(den), jnp.exp(-mt))
            bl = bx[last:last + 1, :]
            mloc = mx[last:last + 1, :]
            m_new = jnp.maximum(bl + m_row, mloc)
            dec = jnp.exp(bl + m_row - m_new)
            fac = jnp.exp(mloc - m_new)
            kw = k * jnp.exp(bl + vx - mloc)
            kv = _mm_tn(kw, vb) * bd_f32
            cbd_scr[sd] = cbd_scr[sd] * dec + kv * fac
            new += [n_row * dec + jnp.sum(kw, axis=0, keepdims=True) * fac, m_new]
        return tuple(new)

    fin_state = lax.fori_loop(0, nc, step, tuple(init), unroll=min(SCAN_UNROLL, nc))
    for sub in range(bps):
        cs, ns, ms = (_layer_view(r, sub, layer, whole) for r in (cs_ref, ns_ref, ms_ref))
        for d in range(2):
            sd = 2 * sub + d
            n_row, m_row = fin_state[2 * sd], fin_state[2 * sd + 1]
            for h in range(HEADS):
                hl = slice(h * HEAD_W, (h + 1) * HEAD_W)
                cs[d, h] = cbd_scr[sd, hl, hl]
                ns[d, h:h + 1, :] = n_row[:, hl]
                ms[d:d + 1, h:h + 1] = m_row[:, h * HEAD_W:h * HEAD_W + 1]

    blk = 256

    def fin(i, carry):
        r0 = pl.multiple_of(i * blk, blk)
        hs = hf_scr[pl.ds(r0, blk), :] + hb_scr[pl.ds(r0, blk), :]
        hn = _seg_rms(hs, HEAD_W, nw_ref[...])
        o = qkvo_ref[pl.ds(r0, blk), 3 * MIX_W:4 * MIX_W].astype(f32)
        out_ref[pl.ds(r0, blk), :] = (hn * jax.nn.sigmoid(o)).astype(out_ref.dtype)
        return carry

    lax.fori_loop(0, tot // blk, fin, 0)


def _mlstm(u, g, nb, seq, gb_row, nw_row, state, layer=0, prev=None):
    has_state = state is not None
    bps = 1 if has_state else SCAN_BPS
    nc = seq // CHUNK
    rows = bps * seq
    tails = ((HEAD_W, HEAD_W), (HEAD_W,), ())
    st = lambda tail: pl.BlockSpec((bps, 2, HEADS) + tail, lambda b: (b, 0, 0) + (0,) * len(tail))
    in_specs = [pl.BlockSpec((rows, 4 * MIX_W), lambda b: (b, COL_MLSTM // (4 * MIX_W))),
                pl.BlockSpec((rows, LANE), lambda b: (b, 0)),
                pl.BlockSpec((1, LANE), lambda b: (0, 0)),
                pl.BlockSpec((1, MIX_W), lambda b: (0, 0))]
    args = [u, g, gb_row, nw_row]
    aliases = {}
    if has_state:
        in_specs += [st(t) for t in tails]
        args += list(state)
        st_specs = [st(t) for t in tails]
        st_shapes = [jax.ShapeDtypeStruct((nb, 2, HEADS) + t, f32) for t in tails]
    else:
        outs = [_layer_out(nb, bps, (2, HEADS) + t, layer, prev) for t in tails]
        st_shapes, st_specs = [o[0] for o in outs], [o[1] for o in outs]
        if prev is not None:
            in_specs += [pl.BlockSpec(memory_space=pl.ANY)] * len(prev)
            args += list(prev)
            aliases = {4 + i: 1 + i for i in range(len(prev))}
    return pl.pallas_call(
        functools.partial(_mlstm_body, seq, bps, has_state, layer, len(aliases)),
        grid=(nb // bps,),
        in_specs=in_specs,
        out_specs=[pl.BlockSpec((rows, MIX_W), lambda b: (b, 0))] + st_specs,
        out_shape=[jax.ShapeDtypeStruct((nb * seq, MIX_W), bf16)] + st_shapes,
        input_output_aliases=aliases,
        scratch_shapes=[pltpu.VMEM((2, rows, MIX_W), f32), pltpu.VMEM((2, rows, MIX_W), f32),
                        pltpu.VMEM((2, rows, MIX_W), f32), pltpu.VMEM((2, bps * nc, 1, MIX_W), f32),
                        pltpu.VMEM((2 * bps, MIX_W, MIX_W), f32),
                        pltpu.VMEM((rows, MIX_W), f32), pltpu.VMEM((rows, MIX_W), f32)],
        compiler_params=_params(("arbitrary",)),
        name="mlstm_state" if has_state else "mlstm",
    )(*args)


def _ssd_body(seq, bps, has_state, layer, n_prev, *refs):
    if has_state:
        (z_ref, xbc_ref, g_ref, cw_ref, cb_ref, dtb_ref, alog_ref, dskip_ref, nw_ref, h0_ref,
         out_ref, hs_ref, xpad_scr, xc_scr, ax_scr, dx_scr, arow_scr, ht_scr, yf_scr, yb_scr) = refs
    else:
        z_ref, xbc_ref, g_ref, cw_ref, cb_ref, dtb_ref, alog_ref, dskip_ref, nw_ref = refs[:9]
        (out_ref, hs_ref, xpad_scr, xc_scr, ax_scr, dx_scr, arow_scr, ht_scr, yf_scr,
         yb_scr) = refs[9 + n_prev:]
    whole = None if has_state else n_prev == 0
    nc = seq // CHUNK
    tot = bps * seq
    pad = 8
    blk = 256
    nblk = tot // blk
    cblk = 128
    for sub in range(bps):
        base = sub * (seq + 2 * pad)
        xpad_scr[base:base + pad, :] = jnp.zeros((pad, SSM_CONV_CH), f32)
        xpad_scr[base + pad + seq:base + 2 * pad + seq, :] = jnp.zeros((pad, SSM_CONV_CH), f32)
        xpad_scr[base + pad:base + pad + seq, :] = xbc_ref[sub * seq:(sub + 1) * seq, :].astype(f32)
        for i in range(seq // cblk):
            r0 = i * cblk
            acc = jnp.zeros((cblk, SSM_CONV_CH), f32) + cb_ref[...]
            for kk in range(SSM_CONV_K):
                off = base + r0 + pad - SSM_CONV_K // 2 + kk
                acc = acc + xpad_scr[off:off + cblk, :] * cw_ref[kk:kk + 1, :]
            xc_scr[sub * seq + r0:sub * seq + r0 + cblk, :] = jax.nn.silu(acc)

    dt = jax.nn.softplus(g_ref[...] + dtb_ref[...])
    lane = lax.broadcasted_iota(jnp.int32, dt.shape, 1)
    acum = _dir_scan(dt * (-jnp.exp(alog_ref[...])), jnp.add, 0.0, (lane % 8) >= HEADS)
    for d in range(2):
        e = _head_expand(G_DT + d * HEADS)
        ax = _select_cols(acum, e)
        ax_scr[d] = ax
        dx_scr[d] = _select_cols(dt, e)
        arow = _chunk_rows(ax, tot)
        for c in range(bps * nc):
            arow_scr[d, c] = arow[c:c + 1, :]

    gn = SSM_GROUPS * SSM_STATE
    rep = HEADS // SSM_GROUPS
    ht_scr[...] = jnp.zeros(ht_scr.shape, f32)
    if has_state:
        for sd in range(2 * bps):
            for h in range(HEADS):
                g0 = (h // rep) * SSM_STATE
                ht_scr[sd, g0:g0 + SSM_STATE, h * HEAD_W:(h + 1) * HEAD_W] = h0_ref[sd // 2, sd % 2, h].T
    li = lax.broadcasted_iota(jnp.int32, (CHUNK, MIX_W), 0)
    si = lax.broadcasted_iota(jnp.int32, (CHUNK, MIX_W), 1) % HEAD_W
    valid = (si <= li, si >= li)
    state_keep = jnp.where(_head_mask(gn, MIX_W, SSM_STATE, rep * HEAD_W), 1.0, 0.0)
    key_keep = jnp.where(_head_mask(HEADS * CHUNK, gn, rep * CHUNK, SSM_STATE), 1.0, 0.0).astype(bf16)
    head_keep = jnp.where(_head_mask(MIX_W, MIX_W, HEAD_W, HEAD_W), 1.0, 0.0).astype(bf16)

    def step(c, carry):
        for sd in range(2 * bps):
            sub, d = sd // 2, sd % 2
            cidx = sub * nc + (c if d == 0 else nc - 1 - c)
            r0 = pl.multiple_of(cidx * CHUNK, CHUNK)
            last = CHUNK - 1 if d == 0 else 0
            xs = xc_scr[pl.ds(r0, CHUNK), 0:MIX_W]
            bm = xc_scr[pl.ds(r0, CHUNK), MIX_W:MIX_W + gn].astype(bf16)
            cm = xc_scr[pl.ds(r0, CHUNK), MIX_W + gn:MIX_W + 2 * gn].astype(bf16)
            ax = ax_scr[d, pl.ds(r0, CHUNK), :]
            decay = jnp.exp(jnp.where(valid[d], ax - arow_scr[d, cidx], NEG))
            scores = _mm_nt(cm, _block_diag(bm, HEADS, key_keep)) * decay
            xdt = xs * dx_scr[d, pl.ds(r0, CHUNK), :]
            y = _mm(scores, _block_diag(xdt.astype(bf16), HEADS, head_keep)) + _mm(cm, ht_scr[sd]) * jnp.exp(ax)
            y_scr = yf_scr if d == 0 else yb_scr
            y_scr[pl.ds(r0, CHUNK), :] = y
            al = ax[last:last + 1, :]
            ht_scr[sd] = ht_scr[sd] * jnp.exp(al) + _mm_tn(bm, xdt * jnp.exp(al - ax)) * state_keep
        return carry

    lax.fori_loop(0, nc, step, 0, unroll=min(SCAN_UNROLL, nc))
    for sub in range(bps):
        hs = _layer_view(hs_ref, sub, layer, whole)
        for d in range(2):
            for h in range(HEADS):
                g0 = (h // rep) * SSM_STATE
                hs[d, h] = ht_scr[2 * sub + d, g0:g0 + SSM_STATE, h * HEAD_W:(h + 1) * HEAD_W].T

    def fin(i, carry):
        r0 = pl.multiple_of(i * blk, blk)
        y = yf_scr[pl.ds(r0, blk), :] + yb_scr[pl.ds(r0, blk), :] + dskip_ref[...] * xc_scr[pl.ds(r0, blk), 0:MIX_W]
        y = y * jax.nn.silu(z_ref[pl.ds(r0, blk), :].astype(f32))
        y = y * lax.rsqrt(jnp.mean(y * y, axis=-1, keepdims=True) + EPS) * nw_ref[...]
        out_ref[pl.ds(r0, blk), :] = y.astype(out_ref.dtype)
        return carry

    lax.fori_loop(0, nblk, fin, 0)


def _ssd(u, g, nb, seq, cw, cb_row, dtb_row, alog_row, dskip_row, nw_row, state, layer=0, prev=None):
    has_state = state is not None
    bps = 1 if has_state else SCAN_BPS
    nc = seq // CHUNK
    rows = bps * seq
    const = lambda shape: pl.BlockSpec(shape, lambda b: (0,) * len(shape))
    st = pl.BlockSpec((bps, 2, HEADS, HEAD_W, SSM_STATE), lambda b: (b, 0, 0, 0, 0))
    in_specs = [pl.BlockSpec((rows, MIX_W), lambda b: (b, COL_Z // MIX_W)),
                pl.BlockSpec((rows, SSM_CONV_CH), lambda b: (b, COL_XBC // SSM_CONV_CH)),
                pl.BlockSpec((rows, LANE), lambda b: (b, 0)),
                const((SSM_CONV_K, SSM_CONV_CH)), const((1, SSM_CONV_CH)), const((1, LANE)), const((1, LANE)),
                const((1, MIX_W)), const((1, MIX_W))]
    args = [u, u, g, cw, cb_row, dtb_row, alog_row, dskip_row, nw_row]
    aliases = {}
    st_shape = jax.ShapeDtypeStruct((nb, 2, HEADS, HEAD_W, SSM_STATE), f32)
    if has_state:
        in_specs.append(st)
        args.append(state)
    else:
        st_shape, st, _ = _layer_out(nb, bps, (2, HEADS, HEAD_W, SSM_STATE), layer, prev)
        if prev is not None:
            in_specs.append(pl.BlockSpec(memory_space=pl.ANY))
            args.append(prev)
            aliases = {9: 1}
    return pl.pallas_call(
        functools.partial(_ssd_body, seq, bps, has_state, layer, len(aliases)),
        grid=(nb // bps,),
        in_specs=in_specs,
        out_specs=[pl.BlockSpec((rows, MIX_W), lambda b: (b, 0)), st],
        out_shape=[jax.ShapeDtypeStruct((nb * seq, MIX_W), bf16), st_shape],
        input_output_aliases=aliases,
        scratch_shapes=[pltpu.VMEM((bps * (seq + 16), SSM_CONV_CH), f32), pltpu.VMEM((rows, SSM_CONV_CH), f32),
                        pltpu.VMEM((2, rows, MIX_W), f32), pltpu.VMEM((2, rows, MIX_W), f32),
                        pltpu.VMEM((2, bps * nc, 1, MIX_W), f32),
                        pltpu.VMEM((2 * bps, SSM_GROUPS * SSM_STATE, MIX_W), f32),
                        pltpu.VMEM((rows, MIX_W), f32), pltpu.VMEM((rows, MIX_W), f32)],
        compiler_params=_params(("arbitrary",)),
        name="ssd_state" if has_state else "ssd",
    )(*args)


def _with_ones(v):
    vb = v.astype(bf16)
    ones = jnp.ones((v.shape[0], HEAD_W), bf16)
    return jnp.concatenate([x for h in range(HEADS) for x in (vb[:, h * HEAD_W:(h + 1) * HEAD_W], ones)], axis=1)


def _softmax_av(scores, vexts):
    m = scores[0].max(axis=1, keepdims=True)
    for s in scores[1:]:
        m = jnp.maximum(m, s.max(axis=1, keepdims=True))
    r = None
    for s, vx in zip(scores, vexts):
        t = jnp.dot(jnp.exp(s - m).astype(bf16), vx, preferred_element_type=f32)
        r = t if r is None else r + t
    return r[:, :HEAD_W] / r[:, HEAD_W:]


def _store_layer(ref, sub, layer, whole, val):
    if whole:
        for l2 in range(DEPTH):
            ref[sub, l2] = val if l2 == layer else jnp.zeros_like(val)
    else:
        ref[sub] = val


def _layer_out(nb, bps, tail, layer, prev):
    zeros = (0,) * len(tail)
    shape = jax.ShapeDtypeStruct((nb, DEPTH) + tail, f32)
    if prev is None:
        return shape, pl.BlockSpec((bps, DEPTH) + tail, lambda b: (b, 0) + zeros), True
    return shape, pl.BlockSpec((bps, None) + tail, lambda b: (b, layer) + zeros), False


def _na_prompt_body(seq, layer, whole, q_ref, k_ref, v_ref, qn_ref, kn_ref, *rest):
    o_ref, ko_ref, vo_ref = rest[-3:]
    scale = HEAD_W ** -0.5
    qb = (_seg_rms(q_ref[...].astype(f32), HEAD_W, qn_ref[...]) * scale).astype(bf16)
    kn = _seg_rms(k_ref[...].astype(f32), HEAD_W, kn_ref[...])
    v = v_ref[...].astype(f32)
    for s0 in range(0, q_ref.shape[0], seq):
        rs = slice(s0, s0 + seq)
        _store_layer(ko_ref, s0 // seq, layer, whole, kn[rs])
        _store_layer(vo_ref, s0 // seq, layer, whole, v[rs])
        kt = kn[rs].T.astype(bf16)
        vx = _with_ones(v[rs])
        for h in range(HEADS):
            hl = slice(h * HEAD_W, (h + 1) * HEAD_W)
            s = jnp.dot(qb[rs, hl], kt[hl, :], preferred_element_type=f32)
            o_ref[rs, hl] = _softmax_av([s], [vx[:, 2 * h * HEAD_W:2 * (h + 1) * HEAD_W]]).astype(o_ref.dtype)


def _na_prompt(u, nb, seq, qn_row, kn_row, layer, prev):
    cb = COL_NA // MIX_W
    rows = PROMPT_BPS * seq
    blk = lambda j: pl.BlockSpec((rows, MIX_W), lambda b: (b, j))
    const = pl.BlockSpec((1, MIX_W), lambda b: (0, 0))
    shape, spec, whole = _layer_out(nb, PROMPT_BPS, (seq, MIX_W), layer, prev)
    extra = [] if prev is None else list(prev)
    return pl.pallas_call(
        functools.partial(_na_prompt_body, seq, layer, whole),
        grid=(nb // PROMPT_BPS,),
        in_specs=[blk(cb), blk(cb + 1), blk(cb + 2), const, const] + [pl.BlockSpec(memory_space=pl.ANY)] * len(extra),
        out_specs=[blk(0), spec, spec],
        out_shape=[jax.ShapeDtypeStruct((nb * seq, MIX_W), bf16), shape, shape],
        input_output_aliases={} if prev is None else {5: 1, 6: 2},
        compiler_params=_params(("arbitrary",)),
        name="na_prompt",
    )(u, u, u, qn_row, kn_row, *extra)


def _na_sample_body(qblk, q_ref, k_ref, v_ref, kc_ref, vc_ref, rpb_ref, qn_ref, kn_ref, o_ref,
                    kt_scr, vx_scr, bias_scr):
    b = pl.program_id(0)
    i = pl.program_id(1)
    seq = k_ref.shape[0]
    rows = seq // GRID_W

    @pl.when((b == 0) & (i == 0))
    def _():
        j = lax.broadcasted_iota(jnp.int32, (GRID_W, LANE), 0)
        lane = lax.broadcasted_iota(jnp.int32, (GRID_W, LANE), 1)
        j2 = lane % GRID_W
        cs = jnp.clip(j - NA_WIN_C // 2, 0, GRID_W - NA_WIN_C)
        ok = (j2 >= cs) & (j2 < cs + NA_WIN_C)
        left = lane < GRID_W
        neg = jnp.full((GRID_W, LANE), NEG, f32)
        for h in range(HEADS):
            lo, hi = [], []
            for dr in range(2 * NA_WIN_R - 1):
                row = jnp.broadcast_to(rpb_ref[h, dr:dr + 1, :], (GRID_W, LANE))
                lo.append(jnp.where(ok, pltpu.roll(row, LANE - (NA_WIN_C - 1), 1, stride=1, stride_axis=0), NEG))
                hi.append(jnp.where(ok, pltpu.roll(row, GRID_W - (NA_WIN_C - 1), 1, stride=1, stride_axis=0), NEG))
            for qi in range(rows):
                rs = min(max(qi - NA_WIN_R // 2, 0), rows - NA_WIN_R)
                for pair in range(rows // 2):
                    k_even, k_odd = 2 * pair, 2 * pair + 1
                    a = lo[k_even - qi + NA_WIN_R - 1] if rs <= k_even < rs + NA_WIN_R else neg
                    c = hi[k_odd - qi + NA_WIN_R - 1] if rs <= k_odd < rs + NA_WIN_R else neg
                    bias_scr[h, qi * GRID_W:(qi + 1) * GRID_W, pair * LANE:(pair + 1) * LANE] = jnp.where(left, a, c)

    @pl.when(i == 0)
    def _():
        kn = _seg_rms(k_ref[...].astype(f32), HEAD_W, kn_ref[...])
        kt_scr[:, 0:seq] = kn.T.astype(bf16)
        kt_scr[:, seq:] = kc_ref[...].T.astype(bf16)
        vx_scr[0:seq, :] = _with_ones(v_ref[...])
        vx_scr[seq:, :] = _with_ones(vc_ref[...])

    r0 = pl.multiple_of(i * qblk, qblk)
    qb = (_seg_rms(q_ref[...].astype(f32), HEAD_W, qn_ref[...]) * (HEAD_W ** -0.5)).astype(bf16)
    for h in range(HEADS):
        hl = slice(h * HEAD_W, (h + 1) * HEAD_W)
        xl = slice(2 * h * HEAD_W, 2 * (h + 1) * HEAD_W)
        s_own = jnp.dot(qb[:, hl], kt_scr[hl, 0:seq], preferred_element_type=f32) + bias_scr[h, pl.ds(r0, qblk), :]
        s_ctx = jnp.dot(qb[:, hl], kt_scr[hl, seq:], preferred_element_type=f32)
        o = _softmax_av([s_own, s_ctx], [vx_scr[0:seq, xl], vx_scr[seq:, xl]])
        o_ref[:, hl] = o.astype(o_ref.dtype)


def _na_sample(u, nb, layer, cache_k, cache_v, rpb_pad, qn_row, kn_row):
    seq = DEC_SEQ
    qblk = 256
    nq = seq // qblk
    cb = COL_NA // MIX_W
    full = lambda j: pl.BlockSpec((seq, MIX_W), lambda b, i: (b, j))
    ctx = pl.BlockSpec((None, None, PAST_LEN, MIX_W), lambda b, i: (b, layer, 0, 0))
    const = pl.BlockSpec((1, MIX_W), lambda b, i: (0, 0))
    return pl.pallas_call(
        functools.partial(_na_sample_body, qblk),
        grid=(nb, nq),
        in_specs=[pl.BlockSpec((qblk, MIX_W), lambda b, i: (b * nq + i, cb)), full(cb + 1), full(cb + 2),
                  ctx, ctx,
                  pl.BlockSpec((HEADS, 2 * NA_WIN_R, LANE), lambda b, i: (0, 0, 0)),
                  const, const],
        out_specs=pl.BlockSpec((qblk, MIX_W), lambda b, i: (b * nq + i, 0)),
        out_shape=jax.ShapeDtypeStruct((nb * seq, MIX_W), bf16),
        scratch_shapes=[pltpu.VMEM((MIX_W, seq + PAST_LEN), bf16), pltpu.VMEM((seq + PAST_LEN, 2 * MIX_W), bf16),
                        pltpu.VMEM((HEADS, seq, seq), f32)],
        compiler_params=_params(("arbitrary", "arbitrary")),
        name="na_sample",
    )(u, u, u, cache_k, cache_v, rpb_pad, qn_row, kn_row)


def _na_rpb_pad(rpb):
    c = rpb.shape[2]
    return jnp.pad(rpb.astype(f32), ((0, 0), (0, 1), (0, LANE - c)))


def _diff_lambda(lam_ref, lam_init):
    lv = lam_ref[...]
    s1 = jnp.sum(lv[0:1, :] * lv[1:2, :], axis=1, keepdims=True)
    s2 = jnp.sum(lv[2:3, :] * lv[3:4, :], axis=1, keepdims=True)
    return jnp.exp(s1) - jnp.exp(s2) + lam_init


def _diff_heads(q, kt, vext, lam):
    outs = []
    for h in range(HEADS):
        o = []
        for m in range(2):
            c0 = h * HEAD_W + m * DIFF_QK_DIM
            s = jnp.dot(q[:, c0:c0 + DIFF_QK_DIM], kt[c0:c0 + DIFF_QK_DIM, :], preferred_element_type=f32)
            o.append(_softmax_av([s], [vext[:, 2 * h * HEAD_W:2 * (h + 1) * HEAD_W]]))
        outs.append(o[0] - lam * o[1])
    return jnp.concatenate(outs, axis=1)


def _diff_prompt_body(lam_init, seq, layer, whole, q_ref, k_ref, v_ref, qn_ref, kn_ref, lam_ref, nw_ref, *rest):
    o_ref, ko_ref, vo_ref = rest[-3:]
    lam = _diff_lambda(lam_ref, lam_init)
    qb = (_seg_rms(q_ref[...].astype(f32), DIFF_QK_DIM, qn_ref[...]) * (DIFF_QK_DIM ** -0.5)).astype(bf16)
    kn = _seg_rms(k_ref[...].astype(f32), DIFF_QK_DIM, kn_ref[...])
    v = v_ref[...].astype(f32)
    for s0 in range(0, q_ref.shape[0], seq):
        rs = slice(s0, s0 + seq)
        _store_layer(ko_ref, s0 // seq, layer, whole, kn[rs])
        _store_layer(vo_ref, s0 // seq, layer, whole, v[rs])
        o = _diff_heads(qb[rs], kn[rs].T.astype(bf16), _with_ones(v[rs]), lam)
        o_ref[rs, :] = (_seg_rms(o, HEAD_W, nw_ref[...]) * (1.0 - lam_init)).astype(o_ref.dtype)


def _diff_prompt(u, nb, seq, lam_init, qn_row, kn_row, lam_vecs, nw_row, layer, prev):
    cb = COL_DIFF // MIX_W
    rows = PROMPT_BPS * seq
    blk = lambda j: pl.BlockSpec((rows, MIX_W), lambda b: (b, j))
    const = pl.BlockSpec((1, MIX_W), lambda b: (0, 0))
    shape, spec, whole = _layer_out(nb, PROMPT_BPS, (seq, MIX_W), layer, prev)
    extra = [] if prev is None else list(prev)
    return pl.pallas_call(
        functools.partial(_diff_prompt_body, lam_init, seq, layer, whole),
        grid=(nb // PROMPT_BPS,),
        in_specs=[blk(cb), blk(cb + 1), blk(cb + 2), const, const,
                  pl.BlockSpec((4, DIFF_QK_DIM), lambda b: (0, 0)), const] + [pl.BlockSpec(memory_space=pl.ANY)] * len(extra),
        out_specs=[blk(0), spec, spec],
        out_shape=[jax.ShapeDtypeStruct((nb * seq, MIX_W), bf16), shape, shape],
        input_output_aliases={} if prev is None else {7: 1, 8: 2},
        compiler_params=_params(("arbitrary",)),
        name="diff_prompt",
    )(u, u, u, qn_row, kn_row, lam_vecs, nw_row, *extra)


def _rope(x, cos, sin):
    w = x.shape[-1]
    up = pltpu.roll(x, w - 8, 1)
    dn = pltpu.roll(x, 8, 1)
    first = (lax.broadcasted_iota(jnp.int32, x.shape, 1) % 16) < 8
    return x * cos + jnp.where(first, up, dn) * sin


def _diff_sample_body(lam_init, qblk, q_ref, k_ref, v_ref, kc_ref, vc_ref, cos_ref, sin_ref, qn_ref, kn_ref,
                      lam_ref, nw_ref, o_ref, kt_scr, vx_scr):
    i = pl.program_id(1)
    seq = k_ref.shape[0]

    @pl.when(i == 0)
    def _():
        kn = _seg_rms(k_ref[...].astype(f32), DIFF_QK_DIM, kn_ref[...])
        kt_scr[:, 0:seq] = _rope(kn, cos_ref[...], sin_ref[...]).T.astype(bf16)
        kt_scr[:, seq:] = kc_ref[...].T.astype(bf16)
        vx_scr[0:seq, :] = _with_ones(v_ref[...])
        vx_scr[seq:, :] = _with_ones(vc_ref[...])

    lam = _diff_lambda(lam_ref, lam_init)
    r0 = pl.multiple_of(i * qblk, qblk)
    qn = _seg_rms(q_ref[...].astype(f32), DIFF_QK_DIM, qn_ref[...])
    qn = _rope(qn, cos_ref[pl.ds(r0, qblk), :], sin_ref[pl.ds(r0, qblk), :]) * (DIFF_QK_DIM ** -0.5)
    o = _diff_heads(qn.astype(bf16), kt_scr[...], vx_scr[...], lam)
    o_ref[...] = (_seg_rms(o, HEAD_W, nw_ref[...]) * (1.0 - lam_init)).astype(o_ref.dtype)


def _diff_sample(u, nb, layer, lam_init, cache_k, cache_v, cos, sin, qn_row, kn_row, lam_vecs, nw_row):
    seq = DEC_SEQ
    qblk = 512
    nq = seq // qblk
    cb = COL_DIFF // MIX_W
    full = lambda j: pl.BlockSpec((seq, MIX_W), lambda b, i: (b, j))
    ctx = pl.BlockSpec((None, None, PAST_LEN, MIX_W), lambda b, i: (b, layer, 0, 0))
    const = pl.BlockSpec((1, MIX_W), lambda b, i: (0, 0))
    tab = pl.BlockSpec((seq, MIX_W), lambda b, i: (0, 0))
    return pl.pallas_call(
        functools.partial(_diff_sample_body, lam_init, qblk),
        grid=(nb, nq),
        in_specs=[pl.BlockSpec((qblk, MIX_W), lambda b, i: (b * nq + i, cb)), full(cb + 1), full(cb + 2),
                  ctx, ctx, tab, tab, const, const,
                  pl.BlockSpec((4, DIFF_QK_DIM), lambda b, i: (0, 0)), const],
        out_specs=pl.BlockSpec((qblk, MIX_W), lambda b, i: (b * nq + i, 0)),
        out_shape=jax.ShapeDtypeStruct((nb * seq, MIX_W), bf16),
        scratch_shapes=[pltpu.VMEM((MIX_W, seq + PAST_LEN), bf16), pltpu.VMEM((seq + PAST_LEN, 2 * MIX_W), bf16)],
        compiler_params=_params(("arbitrary", "arbitrary")),
        name="diff_sample",
    )(u, u, u, cache_k, cache_v, cos, sin, qn_row, kn_row, lam_vecs, nw_row)


def _rope_tables():
    t = jnp.arange(DEC_SEQ)
    nf = DIFF_QK_DIM // 4
    inv = ROPE_BASE ** (-jnp.arange(nf, dtype=f32) / nf)
    ar = (t // GRID_W).astype(f32)[:, None] * inv
    ac = (t % GRID_W).astype(f32)[:, None] * inv
    cos = jnp.concatenate([jnp.cos(ar), jnp.cos(ar), jnp.cos(ac), jnp.cos(ac)], axis=1)
    sin = jnp.concatenate([-jnp.sin(ar), jnp.sin(ar), -jnp.sin(ac), jnp.sin(ac)], axis=1)
    reps = MIX_W // DIFF_QK_DIM
    return jnp.tile(cos, (1, reps)), jnp.tile(sin, (1, reps))


def _merge_body(x_ref, a_ref, b_ref, c_ref, d_ref, gp_ref, mod_ref, wb_ref, wo_ref, nw_ref, wr_ref,
                xo_ref, h2_ref, aff_ref):
    mixed = None
    for n, br in enumerate((a_ref, b_ref, c_ref, d_ref)):
        gate = _sigmoid(gp_ref[:, n * D_MODEL:(n + 1) * D_MODEL].astype(f32))
        t = gate * jnp.dot(br[...].astype(bf16), wb_ref[n], preferred_element_type=f32)
        mixed = t if mixed is None else mixed + t
    y = jnp.dot(mixed.astype(bf16), wo_ref[...], preferred_element_type=f32)
    x = x_ref[...] + mod_ref[:, 2 * D_MODEL:3 * D_MODEL] * y
    xo_ref[...] = x
    h = x * lax.rsqrt(jnp.mean(x * x, axis=-1, keepdims=True) + EPS) * nw_ref[...]
    h = h * (1.0 + mod_ref[:, 4 * D_MODEL:5 * D_MODEL]) + mod_ref[:, 3 * D_MODEL:4 * D_MODEL]
    hb = h.astype(bf16)
    h2_ref[...] = hb
    logits = jnp.dot(hb, wr_ref[...], preferred_element_type=f32)
    lane = lax.broadcasted_iota(jnp.int32, logits.shape, 1)
    logits = jnp.where(lane < N_EXPERTS, logits, NEG)
    e = jnp.exp(logits - logits.max(axis=1, keepdims=True))
    aff_ref[...] = e / e.sum(axis=1, keepdims=True)


def _merge(x, brs, u, mod3, mod_base, mod_stride, seq, wb, wo, nw_row, wr):
    n = x.shape[0]
    tm = 512
    tps = max(seq // tm, 1)
    row = lambda w: pl.BlockSpec((tm, w), lambda i: (i, 0))
    return pl.pallas_call(
        _merge_body,
        grid=(n // tm,),
        in_specs=[row(D_MODEL), row(MIX_W), row(MIX_W), row(MIX_W), row(MIX_W),
                  pl.BlockSpec((tm, N_BRANCH * D_MODEL), lambda i: (i, COL_GPRE // (N_BRANCH * D_MODEL))),
                  pl.BlockSpec((None, 1, 6 * D_MODEL), lambda i: (mod_base + mod_stride * (i // tps), 0, 0)),
                  pl.BlockSpec((N_BRANCH, MIX_W, D_MODEL), lambda i: (0, 0, 0)),
                  pl.BlockSpec((D_MODEL, D_MODEL), lambda i: (0, 0)),
                  pl.BlockSpec((1, D_MODEL), lambda i: (0, 0)),
                  pl.BlockSpec((D_MODEL, LANE), lambda i: (0, 0))],
        out_specs=[row(D_MODEL), row(D_MODEL), row(LANE)],
        out_shape=[jax.ShapeDtypeStruct((n, D_MODEL), f32), jax.ShapeDtypeStruct((n, D_MODEL), bf16),
                   jax.ShapeDtypeStruct((n, LANE), f32)],
        compiler_params=_params(("arbitrary",)),
        name="merge",
    )(x, *brs, u, mod3, wb, wo, nw_row, wr)


def _rows_prefix_sum(x):
    n = x.shape[0]
    pos = lax.broadcasted_iota(jnp.int32, x.shape, 0)
    k = 1
    while k < n:
        x = x + jnp.where(pos >= k, pltpu.roll(x, k, 0), 0.0)
        k *= 2
    return x


def _route_body(t, cap, bpg, aff_ref, h2_ref, xg_ref, g_ref, rank_ref, rrow_scr, arow_scr, oh_scr):
    bb = pl.program_id(1)

    @pl.when(bb == 0)
    def _():
        aff = aff_ref[...]
        tok = lax.broadcasted_iota(jnp.int32, aff.shape, 0)
        val = aff
        k = 2
        while k <= t:
            j = k // 2
            while j >= 1:
                lower = (tok & j) == 0
                pv = jnp.where(lower, pltpu.roll(val, t - j, 0), pltpu.roll(val, j, 0))
                val = jnp.where(lower == ((tok & k) == 0), jnp.maximum(val, pv), jnp.minimum(val, pv))
                j //= 2
            k *= 2
        thr = val[cap - 1:cap, :]
        above = aff > thr
        tie = jnp.where(aff == thr, 1.0, 0.0)
        need = cap - jnp.sum(jnp.where(above, 1.0, 0.0), axis=0, keepdims=True)
        sel = jnp.where(above | ((tie > 0.0) & (_rows_prefix_sum(tie) - tie < need)), 1.0, 0.0)
        rank = jnp.where(sel > 0.0, _rows_prefix_sum(sel) - sel, float(cap))
        rank_ref[...] = rank
        rank_t = rank.T
        aff_t = aff.T
        for k in range(bpg):
            rrow_scr[k] = rank_t[k * N_EXPERTS:(k + 1) * N_EXPERTS, :]
            arow_scr[k] = aff_t[k * N_EXPERTS:(k + 1) * N_EXPERTS, :]

    rk = rrow_scr[bb]
    ar = arow_scr[bb]
    slot = lax.broadcasted_iota(jnp.int32, (cap, t), 0).astype(f32)
    for e in range(N_EXPERTS):
        onehot = jnp.where(rk[e:e + 1, :] == slot, 1.0, 0.0)
        oh_scr[e * cap:(e + 1) * cap, :] = onehot.astype(bf16)
        g_ref[e] = jnp.sum(onehot * ar[e:e + 1, :], axis=1, keepdims=True)
    h2 = h2_ref[...]
    epb = max(1, 512 // cap)
    for e0 in range(0, N_EXPERTS, epb):
        rows = jnp.dot(oh_scr[e0 * cap:(e0 + epb) * cap, :], h2, preferred_element_type=f32)
        xg_ref[e0:e0 + epb] = rows.astype(bf16).reshape(epb, cap, D_MODEL)


def _group_lanes(aff, nb, t, bpg):
    a = aff[:, :N_EXPERTS].reshape(nb // bpg, bpg, t, N_EXPERTS)
    a = jnp.transpose(a, (0, 2, 1, 3)).reshape(nb // bpg * t, bpg * N_EXPERTS)
    return jnp.pad(a, ((0, 0), (0, LANE - bpg * N_EXPERTS)))


def _route(aff, h2, nb, t, bpg):
    cap = CAPACITY_FACTOR * t // N_EXPERTS
    return pl.pallas_call(
        functools.partial(_route_body, t, cap, bpg),
        grid=(nb // bpg, bpg),
        in_specs=[pl.BlockSpec((t, LANE), lambda g, k: (g, 0)),
                  pl.BlockSpec((t, D_MODEL), lambda g, k: (g * bpg + k, 0))],
        out_specs=[pl.BlockSpec((N_EXPERTS, cap, D_MODEL), lambda g, k: (0, g * bpg + k, 0)),
                   pl.BlockSpec((N_EXPERTS, cap, 1), lambda g, k: (0, g * bpg + k, 0)),
                   pl.BlockSpec((t, LANE), lambda g, k: (g, 0))],
        out_shape=[jax.ShapeDtypeStruct((N_EXPERTS, nb * cap, D_MODEL), bf16),
                   jax.ShapeDtypeStruct((N_EXPERTS, nb * cap, 1), f32),
                   jax.ShapeDtypeStruct((nb // bpg * t, LANE), f32)],
        scratch_shapes=[pltpu.VMEM((bpg, N_EXPERTS, t), f32), pltpu.VMEM((bpg, N_EXPERTS, t), f32),
                        pltpu.VMEM((N_EXPERTS * cap, t), bf16)],
        compiler_params=_params(("arbitrary", "arbitrary")),
        name=f"route_{t}",
    )(_group_lanes(aff, nb, t, bpg), h2)


def _ffn_body(xp_ref, xs_ref, gp_ref, gs_ref, wg_ref, wu_ref, wd_ref, yp_ref, ys_ref, wg_scr, wu_scr, wd_scr):
    wg_scr[...] = wg_ref[...].astype(bf16)
    wu_scr[...] = wu_ref[...].astype(bf16)
    wd_scr[...] = wd_ref[...].astype(bf16)
    for x_ref, g_ref, y_ref in ((xp_ref, gp_ref, yp_ref), (xs_ref, gs_ref, ys_ref)):
        blk = min(1024, x_ref.shape[0])

        def step(i, carry, x_ref=x_ref, g_ref=g_ref, y_ref=y_ref, blk=blk):
            r0 = pl.multiple_of(i * blk, blk)
            x = x_ref[pl.ds(r0, blk), :]
            a = jnp.dot(x, wg_scr[...], preferred_element_type=f32)
            up = jnp.dot(x, wu_scr[...], preferred_element_type=f32)
            mid = (jax.nn.silu(a) * up).astype(bf16)
            y = jnp.dot(mid, wd_scr[...], preferred_element_type=f32)
            y_ref[pl.ds(r0, blk), :] = (y * g_ref[pl.ds(r0, blk), :]).astype(bf16)
            return carry
        lax.fori_loop(0, x_ref.shape[0] // blk, step, 0)


def _ffn(xp, xs, gp, gs, w_gate, w_up, w_down, layer):
    np_, ns = xp.shape[1], xs.shape[1]
    tok = lambda n, w: pl.BlockSpec((None, n, w), lambda e: (e, 0, 0))
    wspec = lambda a, b: pl.BlockSpec((None, None, a, b), lambda e: (layer, e, 0, 0))
    return pl.pallas_call(
        _ffn_body,
        grid=(N_EXPERTS,),
        in_specs=[tok(np_, D_MODEL), tok(ns, D_MODEL), tok(np_, 1), tok(ns, 1),
                  wspec(D_MODEL, EXPERT_FF), wspec(D_MODEL, EXPERT_FF), wspec(EXPERT_FF, D_MODEL)],
        out_specs=[tok(np_, D_MODEL), tok(ns, D_MODEL)],
        out_shape=[jax.ShapeDtypeStruct(xp.shape, bf16), jax.ShapeDtypeStruct(xs.shape, bf16)],
        scratch_shapes=[pltpu.VMEM((D_MODEL, EXPERT_FF), bf16), pltpu.VMEM((D_MODEL, EXPERT_FF), bf16),
                        pltpu.VMEM((EXPERT_FF, D_MODEL), bf16)],
        compiler_params=_params(("arbitrary",)),
        name="expert_ffn",
    )(xp, xs, gp, gs, w_gate, w_up, w_down)


def _combine_body(t, cap, bpg, bps, x_ref, rank_ref, y_ref, mod_ref, o_ref):
    k = N_EXPERTS * cap
    er = lax.broadcasted_iota(jnp.int32, (LANE, k), 0)
    ec = lax.broadcasted_iota(jnp.int32, (LANE, k), 1) // cap
    slot = (lax.broadcasted_iota(jnp.int32, (t, k), 1) % cap).astype(f32)
    rank = rank_ref[...].astype(bf16)
    for sub in range(bps):
        col0 = ((pl.program_id(0) * bps + sub) % bpg) * N_EXPERTS
        expand = jnp.where(er == ec + col0, 1.0, 0.0).astype(bf16)
        rexp = jnp.dot(rank, expand, preferred_element_type=f32)
        pt = jnp.where(rexp == slot, 1.0, 0.0).astype(bf16)
        y = y_ref[:, sub * cap:(sub + 1) * cap, :].reshape(k, D_MODEL)
        moe = jnp.dot(pt, y, preferred_element_type=f32)
        rs = slice(sub * t, (sub + 1) * t)
        o_ref[rs, :] = x_ref[rs, :] + mod_ref[:, 5 * D_MODEL:6 * D_MODEL] * moe


def _combine(x, rank, y, mod3, mod_base, mod_stride, nb, t, bpg):
    cap = CAPACITY_FACTOR * t // N_EXPERTS
    bps = 2 if (mod_stride == 0 and bpg % 2 == 0) else 1
    return pl.pallas_call(
        functools.partial(_combine_body, t, cap, bpg, bps),
        grid=(nb // bps,),
        in_specs=[pl.BlockSpec((bps * t, D_MODEL), lambda b: (b, 0)),
                  pl.BlockSpec((t, LANE), lambda b: (b * bps // bpg, 0)),
                  pl.BlockSpec((N_EXPERTS, bps * cap, D_MODEL), lambda b: (0, b, 0)),
                  pl.BlockSpec((None, 1, 6 * D_MODEL), lambda b: (mod_base + mod_stride * b, 0, 0))],
        out_specs=pl.BlockSpec((bps * t, D_MODEL), lambda b: (b, 0)),
        out_shape=jax.ShapeDtypeStruct(x.shape, f32),
        compiler_params=_params(("arbitrary",)),
        name=f"combine_{t}",
    )(x, rank, y, mod3)


def _pad_row(v, off, width=LANE):
    return jnp.zeros((1, width), f32).at[0, off:off + v.shape[0]].set(v.astype(f32))


def _layer_params(l, a):
    w_in = a["w_in"][l]
    pts, acc = [], 0
    splits = (MIX_W,) * 4 + (4 * HEADS,) + (MIX_W,) * 7 + (SSM_CONV_CH, 2 * HEADS, N_BRANCH * D_MODEL)
    for s in splits:
        pts.append((acc, acc + s))
        acc += s
    seg = lambda i: w_in[:, pts[i][0]:pts[i][1]]
    order = [14, 0, 1, 2, 3, 5, 6, 7, 8, 9, 10, 12, 11]
    w_main = jnp.concatenate([seg(i) for i in order] + [jnp.zeros((D_MODEL, U_COLS - U_USED), f32)],
                             axis=1).astype(bf16)
    w_side = jnp.concatenate([seg(4), seg(13), jnp.zeros((D_MODEL, LANE - 24), f32)], axis=1).astype(bf16)
    tile = lambda v, reps: jnp.tile(v.astype(f32), reps).reshape(1, -1)
    p = dict(
        w_main=w_main, w_side=w_side,
        norm1=a["norm1_w"][l].reshape(1, -1), norm2=a["norm2_w"][l].reshape(1, -1),
        mlstm_gb=_pad_row(a["mlstm_gate_b"][l].reshape(-1), G_MLSTM),
        mlstm_nw=a["mlstm_norm_w"][l].reshape(1, -1),
        na_qn=tile(a["na_qnorm_w"][l], HEADS), na_kn=tile(a["na_knorm_w"][l], HEADS),
        na_rpb=_na_rpb_pad(a["na_rpb"][l]),
        diff_qn=tile(a["diff_qnorm_w"][l], 2 * HEADS), diff_kn=tile(a["diff_knorm_w"][l], 2 * HEADS),
        diff_lam=a["diff_lambda"][l], diff_nw=tile(a["diff_norm_w"][l], HEADS),
        ssm_cw=a["ssm_conv_w"][l], ssm_cb=a["ssm_conv_b"][l].reshape(1, -1),
        ssm_dtb=_pad_row(a["ssm_dt_bias"][l].reshape(-1), G_DT),
        ssm_alog=_pad_row(a["ssm_a_log"][l].reshape(-1), G_DT),
        ssm_d=jnp.repeat(a["ssm_d"][l].astype(f32), HEAD_W).reshape(1, -1),
        ssm_nw=a["ssm_norm_w"][l].reshape(1, -1),
        wb=a["w_branch"][l].astype(bf16), wo=a["w_out"][l].astype(bf16),
        wr=jnp.concatenate([a["w_router"][l], jnp.zeros((D_MODEL, LANE - N_EXPERTS), f32)], axis=1).astype(bf16),
        w_gate=a["w_gate"], w_up=a["w_up"], w_down=a["w_down"],
    )
    return p


def _mixers(u, g, nb, seq, p, l, lam_init, ctx, tables, caches):
    if ctx is None:
        a_out, m_c, m_n, m_m = _mlstm(u, g, nb, seq, p["mlstm_gb"], p["mlstm_nw"], None,
                                      l, None if caches is None else caches[4:7])
        b_out, na_k, na_v = _na_prompt(u, nb, seq, p["na_qn"], p["na_kn"], l, None if caches is None else caches[0:2])
        c_out, df_k, df_v = _diff_prompt(u, nb, seq, lam_init, p["diff_qn"], p["diff_kn"], p["diff_lam"], p["diff_nw"],
                                         l, None if caches is None else caches[2:4])
        d_out, s_h = _ssd(u, g, nb, seq, p["ssm_cw"], p["ssm_cb"], p["ssm_dtb"], p["ssm_alog"], p["ssm_d"],
                          p["ssm_nw"], None, l, None if caches is None else caches[7])
        return (a_out, b_out, c_out, d_out), (na_k, na_v, df_k, df_v, m_c, m_n, m_m, s_h)
    (na_ck, na_cv, df_ck, df_cv, st_c, st_n, st_m, st_s) = ctx
    cos, sin = tables
    a_out = _mlstm(u, g, nb, seq, p["mlstm_gb"], p["mlstm_nw"], (st_c[:, l], st_n[:, l], st_m[:, l]))[0]
    b_out = _na_sample(u, nb, l, na_ck, na_cv, p["na_rpb"], p["na_qn"], p["na_kn"])
    c_out = _diff_sample(u, nb, l, lam_init, df_ck, df_cv, cos, sin, p["diff_qn"], p["diff_kn"], p["diff_lam"],
                         p["diff_nw"])
    d_out = _ssd(u, g, nb, seq, p["ssm_cw"], p["ssm_cb"], p["ssm_dtb"], p["ssm_alog"], p["ssm_d"], p["ssm_nw"],
                 st_s[:, l])[0]
    return (a_out, b_out, c_out, d_out), None


def kernel(x_prompt, x_sample, cache_na_k, cache_na_v, cache_diff_k, cache_diff_v, state_mlstm_C, state_mlstm_n, state_mlstm_m, state_ssm, c, c_ctx, norm1_w, norm2_w, w_ada, b_ada, w_in, mlstm_gate_b, mlstm_norm_w, na_qnorm_w, na_knorm_w, na_rpb, diff_qnorm_w, diff_knorm_w, diff_lambda, diff_norm_w, ssm_conv_w, ssm_conv_b, ssm_dt_bias, ssm_a_log, ssm_d, ssm_norm_w, w_branch, w_out, w_router, w_gate, w_up, w_down):
    a = dict(norm1_w=norm1_w, norm2_w=norm2_w, w_in=w_in, mlstm_gate_b=mlstm_gate_b, mlstm_norm_w=mlstm_norm_w,
             na_qnorm_w=na_qnorm_w, na_knorm_w=na_knorm_w, na_rpb=na_rpb, diff_qnorm_w=diff_qnorm_w,
             diff_knorm_w=diff_knorm_w, diff_lambda=diff_lambda, diff_norm_w=diff_norm_w, ssm_conv_w=ssm_conv_w,
             ssm_conv_b=ssm_conv_b, ssm_dt_bias=ssm_dt_bias, ssm_a_log=ssm_a_log, ssm_d=ssm_d,
             ssm_norm_w=ssm_norm_w, w_branch=w_branch, w_out=w_out, w_router=w_router, w_gate=w_gate, w_up=w_up,
             w_down=w_down)
    nb_p, nb_s = x_prompt.shape[0], x_sample.shape[0]
    xp = x_prompt.reshape(nb_p * SEQ, D_MODEL)
    xs = x_sample.reshape(nb_s * DEC_SEQ, D_MODEL)
    cc = jnp.concatenate([c_ctx[None, :], c, jnp.zeros((8 - 1 - nb_s, D_MODEL), f32)], axis=0)
    ctx = (cache_na_k.reshape(nb_s, DEPTH, PAST_LEN, MIX_W), cache_na_v.reshape(nb_s, DEPTH, PAST_LEN, MIX_W),
           cache_diff_k.reshape(nb_s, DEPTH, PAST_LEN, MIX_W), cache_diff_v.reshape(nb_s, DEPTH, PAST_LEN, MIX_W),
           state_mlstm_C, state_mlstm_n, state_mlstm_m, state_ssm)
    tables = _rope_tables()
    caches = None
    for l in range(DEPTH):
        p = _layer_params(l, a)
        lam_init = 0.8 - 0.6 * math.exp(-0.3 * l)
        mod3 = _modulation(cc, w_ada, b_ada, l).reshape(8, 1, 6 * D_MODEL)
        groups = []
        for (x, nb, seq, base, stride, gctx) in ((xp, nb_p, SEQ, 0, 0, None), (xs, nb_s, DEC_SEQ, 1, 1, ctx)):
            u, g = _inproj(x, mod3, base, stride, seq, p["norm1"], p["w_main"], p["w_side"])
            brs, st = _mixers(u, g, nb, seq, p, l, lam_init, gctx, tables, caches)
            if st is not None:
                caches = st
            x1, h2, aff = _merge(x, brs, u, mod3, base, stride, seq, p["wb"], p["wo"], p["norm2"], p["wr"])
            bpg = min(nb, LANE // N_EXPERTS)
            xg, gg, rank = _route(aff, h2, nb, seq, bpg)
            groups.append((x1, xg, gg, rank, nb, seq, base, stride, bpg))
        yp, ys = _ffn(groups[0][1], groups[1][1], groups[0][2], groups[1][2], p["w_gate"], p["w_up"], p["w_down"], l)
        outs = []
        for (x1, _, _, rank, nb, seq, base, stride, bpg), y in zip(groups, (yp, ys)):
            outs.append(_combine(x1, rank, y, mod3, base, stride, nb, seq, bpg))
        xp, xs = outs
    kv = (nb_p, DEPTH, SEQ, HEADS, HEAD_W)
    return (xp.reshape(x_prompt.shape), xs.reshape(x_sample.shape),
            caches[0].reshape(kv), caches[1].reshape(kv), caches[2].reshape(kv), caches[3].reshape(kv),
            caches[4], caches[5], caches[6], caches[7])
```

```python
import functools
import math

import jax
import jax.numpy as jnp
from jax import lax
from jax.experimental import pallas as pl
from jax.experimental.pallas import tpu as pltpu

f32 = jnp.float32
bf16 = jnp.bfloat16

D_MODEL = 1024
BATCH = 32
SEQ = 256
DEPTH = 2
DEC_BATCH = 4
DEC_SEQ = 1024
PAST_LEN = 512
GRID_W = 64
MIX_W = D_MODEL // 4
N_BRANCH = 4
HEADS = 4
HEAD_W = MIX_W // HEADS
NA_WIN_R = 8
NA_WIN_C = 16
DIFF_QK_DIM = HEAD_W // 2
SSM_GROUPS = 2
SSM_STATE = 64
SSM_CONV_K = 5
SSM_CONV_CH = MIX_W + 2 * SSM_GROUPS * SSM_STATE
CHUNK = 64
PROMPT_BPS = 2
SCAN_BPS = 2
SCAN_UNROLL = 8
N_EXPERTS = 16
EXPERT_FF = 1024
CAPACITY_FACTOR = 2
ROPE_BASE = 10000.0
EPS = 1e-6
NEG = -1e30

LANE = 128
VMEM_LIMIT = 56 * 1024 * 1024

COL_GPRE = 0
COL_MLSTM = 4096
COL_NA = 5120
COL_DIFF = 5888
COL_XBC = 6656
COL_Z = 7168
U_USED = 7424
U_COLS = 7680
G_MLSTM = 0
G_DT = 16


def _params(sem):
    return pltpu.CompilerParams(dimension_semantics=sem, vmem_limit_bytes=VMEM_LIMIT)


def _mm(a, b):
    return jnp.dot(a.astype(bf16), b.astype(bf16), preferred_element_type=f32)


def _mm_nt(a, b):
    return lax.dot_general(a.astype(bf16), b.astype(bf16), (((1,), (1,)), ((), ())), preferred_element_type=f32)


def _mm_tn(a, b):
    return lax.dot_general(a.astype(bf16), b.astype(bf16), (((0,), (0,)), ((), ())), preferred_element_type=f32)


def _sigmoid(x):
    return 0.5 * jnp.tanh(0.5 * x) + 0.5


def _seg_sum(x, seg):
    w = x.shape[-1]
    r = lax.broadcasted_iota(jnp.int32, (w, w), 0) // seg
    c = lax.broadcasted_iota(jnp.int32, (w, w), 1) // seg
    ones = jnp.where(r == c, 1.0, 0.0).astype(bf16)
    hi = x.astype(bf16)
    lo = (x - hi.astype(f32)).astype(bf16)
    return jnp.dot(hi, ones, preferred_element_type=f32) + jnp.dot(lo, ones, preferred_element_type=f32)


def _seg_rms(x, seg, w_row):
    ms = _seg_sum(x * x, seg) * (1.0 / seg)
    return x * lax.rsqrt(ms + EPS) * w_row


def _mod_body(c_ref, w_ref, b_ref, o_ref):
    o_ref[...] = _mm(jax.nn.silu(c_ref[...]), w_ref[...]) + b_ref[...]


def _modulation(cc, w_ada, b_ada, layer):
    tn = 1536
    n = w_ada.shape[2]
    return pl.pallas_call(
        _mod_body,
        grid=(n // tn,),
        in_specs=[pl.BlockSpec((8, D_MODEL), lambda j: (0, 0)),
                  pl.BlockSpec((None, D_MODEL, tn), lambda j: (layer, 0, j)),
                  pl.BlockSpec((None, 1, tn), lambda j: (layer, 0, j))],
        out_specs=pl.BlockSpec((8, tn), lambda j: (0, j)),
        out_shape=jax.ShapeDtypeStruct((8, n), f32),
        compiler_params=_params(("arbitrary",)),
        name="modulation",
    )(cc, w_ada, b_ada.reshape(b_ada.shape[0], 1, n))


def _inproj_body(ncol, x_ref, mod_ref, nw_ref, w_ref, wg_ref, u_ref, g_ref):
    x = x_ref[...]
    y = x * lax.rsqrt(jnp.mean(x * x, axis=-1, keepdims=True) + EPS) * nw_ref[...]
    hb = (y * (1.0 + mod_ref[:, D_MODEL:2 * D_MODEL]) + mod_ref[:, 0:D_MODEL]).astype(bf16)
    g_ref[...] = jnp.dot(hb, wg_ref[...], preferred_element_type=f32)
    cw = U_COLS // ncol
    for c in range(ncol):
        u_ref[:, c * cw:(c + 1) * cw] = jnp.dot(hb, w_ref[:, c * cw:(c + 1) * cw],
                                                preferred_element_type=f32).astype(u_ref.dtype)


def _inproj(x, mod3, mod_base, mod_stride, seq, nw_row, w_main, w_side):
    n = x.shape[0]
    tm = 512
    tps = max(seq // tm, 1)
    resident = lambda shape: pl.BlockSpec(shape, lambda i: (0, 0), pipeline_mode=pl.Buffered(1))
    return pl.pallas_call(
        functools.partial(_inproj_body, 3),
        grid=(n // tm,),
        in_specs=[pl.BlockSpec((tm, D_MODEL), lambda i: (i, 0)),
                  pl.BlockSpec((None, 1, 6 * D_MODEL), lambda i: (mod_base + mod_stride * (i // tps), 0, 0)),
                  pl.BlockSpec((1, D_MODEL), lambda i: (0, 0)),
                  resident((D_MODEL, U_COLS)), resident((D_MODEL, LANE))],
        out_specs=[pl.BlockSpec((tm, U_COLS), lambda i: (i, 0)),
                   pl.BlockSpec((tm, LANE), lambda i: (i, 0))],
        out_shape=[jax.ShapeDtypeStruct((n, U_COLS), bf16), jax.ShapeDtypeStruct((n, LANE), f32)],
        compiler_params=_params(("arbitrary",)),
        name="inproj",
    )(x, mod3, nw_row, w_main, w_side)


def _split3(x):
    hi = x.astype(bf16)
    r = x - hi.astype(f32)
    mid = r.astype(bf16)
    lo = (r - mid.astype(f32)).astype(bf16)
    return hi, mid, lo


def _select_cols(x, onehot):
    return jnp.dot(jnp.concatenate(_split3(x), axis=1), jnp.concatenate([onehot] * 3, axis=0),
                   preferred_element_type=f32)


def _select_rows(onehot, x):
    return jnp.dot(jnp.concatenate([onehot] * 3, axis=1), jnp.concatenate(_split3(x), axis=0),
                   preferred_element_type=f32)


def _chunk_scan(x, op, identity, reverse):
    n = x.shape[0]
    pos = lax.broadcasted_iota(jnp.int32, x.shape, 0) % CHUNK
    k = 1
    while k < CHUNK:
        if reverse:
            shifted, ok = pltpu.roll(x, n - k, 0), pos < CHUNK - k
        else:
            shifted, ok = pltpu.roll(x, k, 0), pos >= k
        x = op(x, jnp.where(ok, shifted, identity))
        k *= 2
    return x


def _dir_scan(x, op, identity, bwd_col):
    return jnp.where(bwd_col, _chunk_scan(x, op, identity, True), _chunk_scan(x, op, identity, False))


def _head_expand(col0):
    c = lax.broadcasted_iota(jnp.int32, (LANE, MIX_W), 0)
    h = lax.broadcasted_iota(jnp.int32, (LANE, MIX_W), 1) // HEAD_W
    return jnp.where(c == col0 + h, 1.0, 0.0).astype(bf16)


def _chunk_rows(x, seq):
    nrow = max(seq // CHUNK, 8)
    t = lax.broadcasted_iota(jnp.int32, (seq, MIX_W), 0) % CHUNK
    s_ = lax.broadcasted_iota(jnp.int32, (seq, MIX_W), 1) % HEAD_W
    sel = lax.broadcasted_iota(jnp.int32, (nrow, seq), 1) // CHUNK == lax.broadcasted_iota(jnp.int32, (nrow, seq), 0)
    return _select_rows(jnp.where(sel, 1.0, 0.0).astype(bf16), jnp.where(t == s_, x, 0.0))


def _block_diag(x, reps, keep):
    return jnp.concatenate([x] * reps, axis=0) * keep


def _head_mask(rows, cols, rseg, cseg):
    return (lax.broadcasted_iota(jnp.int32, (rows, cols), 0) // rseg) == (lax.broadcasted_iota(jnp.int32, (rows, cols), 1) // cseg)


def _layer_view(ref, sub, layer, whole):
    if not whole:
        return ref.at[sub]
    for l2 in range(DEPTH):
        if l2 != layer:
            ref[sub, l2] = jnp.zeros(ref.shape[2:], f32)
    return ref.at[sub, layer]


def _mlstm_body(seq, bps, has_state, layer, n_prev, *refs):
    if has_state:
        (qkvo_ref, g_ref, gb_ref, nw_ref, c0_ref, n0_ref, m0_ref,
         out_ref, cs_ref, ns_ref, ms_ref, bx_scr, mx_scr, vx_scr, vrow_scr, cbd_scr, hf_scr, hb_scr) = refs
    else:
        qkvo_ref, g_ref, gb_ref, nw_ref = refs[:4]
        (out_ref, cs_ref, ns_ref, ms_ref, bx_scr, mx_scr, vx_scr, vrow_scr, cbd_scr, hf_scr,
         hb_scr) = refs[4 + n_prev:]
    whole = None if has_state else n_prev == 0
    nc = seq // CHUNK
    tot = bps * seq
    g = g_ref[...] + gb_ref[...]
    lane = lax.broadcasted_iota(jnp.int32, g.shape, 1)
    bwd_col = (lane % 16) >= 8
    bsum = _dir_scan(jax.nn.log_sigmoid(g), jnp.add, 0.0, bwd_col)
    b_i = pltpu.roll(bsum, LANE - HEADS, 1)
    vcol = g - b_i
    mcol = b_i + _dir_scan(vcol, jnp.maximum, NEG, bwd_col)
    for d in range(2):
        e = _head_expand(G_MLSTM + d * 8)
        bx_scr[d] = _select_cols(b_i, e)
        mx_scr[d] = _select_cols(mcol, e)
        vx = _select_cols(vcol, e)
        vx_scr[d] = vx
        vrow = _chunk_rows(vx, tot)
        for c in range(bps * nc):
            vrow_scr[d, c] = vrow[c:c + 1, :]

    grp = lax.broadcasted_iota(jnp.int32, (1, MIX_W), 1) // HEAD_W
    cbd_scr[...] = jnp.zeros(cbd_scr.shape, f32)
    init = []
    for sub in range(bps):
        for d in range(2):
            if has_state:
                for h in range(HEADS):
                    cbd_scr[2 * sub + d, h * HEAD_W:(h + 1) * HEAD_W, h * HEAD_W:(h + 1) * HEAD_W] = c0_ref[sub, d, h]
                init.append(jnp.concatenate([n0_ref[sub, d, h:h + 1, :] for h in range(HEADS)], axis=1))
                m_row = jnp.zeros((1, MIX_W), f32)
                for h in range(HEADS):
                    m_row = jnp.where(grp == h, m0_ref[sub, d:d + 1, h:h + 1], m_row)
                init.append(m_row)
            else:
                init += [jnp.zeros((1, MIX_W), f32), jnp.zeros((1, MIX_W), f32)]

    li = lax.broadcasted_iota(jnp.int32, (CHUNK, MIX_W), 0)
    si = lax.broadcasted_iota(jnp.int32, (CHUNK, MIX_W), 1) % HEAD_W
    valid = (si <= li, si >= li)
    bd = _head_mask(MIX_W, MIX_W, HEAD_W, HEAD_W)
    bd_f32 = jnp.where(bd, 1.0, 0.0)
    ones_bd = bd_f32.astype(bf16)

    def step(c, carry):
        new = []
        for sd in range(2 * bps):
            sub, d = sd // 2, sd % 2
            n_row, m_row = carry[2 * sd], carry[2 * sd + 1]
            cidx = sub * nc + (c if d == 0 else nc - 1 - c)
            r0 = pl.multiple_of(cidx * CHUNK, CHUNK)
            last = CHUNK - 1 if d == 0 else 0
            q = qkvo_ref[pl.ds(r0, CHUNK), 0:MIX_W].astype(f32)
            k = qkvo_ref[pl.ds(r0, CHUNK), MIX_W:2 * MIX_W].astype(f32) * (HEAD_W ** -0.5)
            vb = qkvo_ref[pl.ds(r0, CHUNK), 2 * MIX_W:3 * MIX_W].astype(bf16)
            qb = q.astype(bf16)
            bx = bx_scr[d, pl.ds(r0, CHUNK), :]
            mx = mx_scr[d, pl.ds(r0, CHUNK), :]
            vx = vx_scr[d, pl.ds(r0, CHUNK), :]
            p = jnp.exp(jnp.where(valid[d], bx + vrow_scr[d, cidx] - mx, NEG))
            smat = _mm_nt(qb, _block_diag(k.astype(bf16), HEADS, ones_bd)) * p
            intra = _mm(smat, _block_diag(vb, HEADS, ones_bd))
            dsum = _mm(smat, ones_bd)
            gg = bx + m_row
            mt = jnp.maximum(gg, mx)
            a = jnp.exp(mx - mt)
            w_prev = jnp.exp(gg - mt)
            inter = _mm(qb, cbd_scr[sd])
            dint = _mm(q * n_row, ones_bd)
            num = intra * a + inter * w_prev
            den = dsum * a + dint * w_prev
            h_scr = hf_scr if d == 0 else hb_scr
            h_scr[pl.ds(r0, CHUNK), :] = num / jnp.maximum(jnp.abs(den), jnp.exp(-mt))
            bl = bx[last:last + 1, :]
            mloc = mx[last:last + 1, :]
            m_new = jnp.maximum(bl + m_row, mloc)
            dec = jnp.exp(bl + m_row - m_new)
            fac = jnp.exp(mloc - m_new)
            kw = k * jnp.exp(bl + vx - mloc)
            kv = _mm_tn(kw, vb) * bd_f32
            cbd_scr[sd] = cbd_scr[sd] * dec + kv * fac
            new += [n_row * dec + jnp.sum(kw, axis=0, keepdims=True) * fac, m_new]
        return tuple(new)

    fin_state = lax.fori_loop(0, nc, step, tuple(init), unroll=min(SCAN_UNROLL, nc))
    for sub in range(bps):
        cs, ns, ms = (_layer_view(r, sub, layer, whole) for r in (cs_ref, ns_ref, ms_ref))
        for d in range(2):
            sd = 2 * sub + d
            n_row, m_row = fin_state[2 * sd], fin_state[2 * sd + 1]
            for h in range(HEADS):
                hl = slice(h * HEAD_W, (h + 1) * HEAD_W)
                cs[d, h] = cbd_scr[sd, hl, hl]
                ns[d, h:h + 1, :] = n_row[:, hl]
                ms[d:d + 1, h:h + 1] = m_row[:, h * HEAD_W:h * HEAD_W + 1]

    blk = 256

    def fin(i, carry):
        r0 = pl.multiple_of(i * blk, blk)
        hs = hf_scr[pl.ds(r0, blk), :] + hb_scr[pl.ds(r0, blk), :]
        hn = _seg_rms(hs, HEAD_W, nw_ref[...])
        o = qkvo_ref[pl.ds(r0, blk), 3 * MIX_W:4 * MIX_W].astype(f32)
        out_ref[pl.ds(r0, blk), :] = (hn * jax.nn.sigmoid(o)).astype(out_ref.dtype)
        return carry

    lax.fori_loop(0, tot // blk, fin, 0)


def _mlstm(u, g, nb, seq, gb_row, nw_row, state, layer=0, prev=None):
    has_state = state is not None
    bps = 1 if has_state else SCAN_BPS
    nc = seq // CHUNK
    rows = bps * seq
    tails = ((HEAD_W, HEAD_W), (HEAD_W,), ())
    st = lambda tail: pl.BlockSpec((bps, 2, HEADS) + tail, lambda b: (b, 0, 0) + (0,) * len(tail))
    in_specs = [pl.BlockSpec((rows, 4 * MIX_W), lambda b: (b, COL_MLSTM // (4 * MIX_W))),
                pl.BlockSpec((rows, LANE), lambda b: (b, 0)),
                pl.BlockSpec((1, LANE), lambda b: (0, 0)),
                pl.BlockSpec((1, MIX_W), lambda b: (0, 0))]
    args = [u, g, gb_row, nw_row]
    aliases = {}
    if has_state:
        in_specs += [st(t) for t in tails]
        args += list(state)
        st_specs = [st(t) for t in tails]
        st_shapes = [jax.ShapeDtypeStruct((nb, 2, HEADS) + t, f32) for t in tails]
    else:
        outs = [_layer_out(nb, bps, (2, HEADS) + t, layer, prev) for t in tails]
        st_shapes, st_specs = [o[0] for o in outs], [o[1] for o in outs]
        if prev is not None:
            in_specs += [pl.BlockSpec(memory_space=pl.ANY)] * len(prev)
            args += list(prev)
            aliases = {4 + i: 1 + i for i in range(len(prev))}
    return pl.pallas_call(
        functools.partial(_mlstm_body, seq, bps, has_state, layer, len(aliases)),
        grid=(nb // bps,),
        in_specs=in_specs,
        out_specs=[pl.BlockSpec((rows, MIX_W), lambda b: (b, 0))] + st_specs,
        out_shape=[jax.ShapeDtypeStruct((nb * seq, MIX_W), bf16)] + st_shapes,
        input_output_aliases=aliases,
        scratch_shapes=[pltpu.VMEM((2, rows, MIX_W), f32), pltpu.VMEM((2, rows, MIX_W), f32),
                        pltpu.VMEM((2, rows, MIX_W), f32), pltpu.VMEM((2, bps * nc, 1, MIX_W), f32),
                        pltpu.VMEM((2 * bps, MIX_W, MIX_W), f32),
                        pltpu.VMEM((rows, MIX_W), f32), pltpu.VMEM((rows, MIX_W), f32)],
        compiler_params=_params(("arbitrary",)),
        name="mlstm_state" if has_state else "mlstm",
    )(*args)


def _ssd_body(seq, bps, has_state, layer, n_prev, *refs):
    if has_state:
        (z_ref, xbc_ref, g_ref, cw_ref, cb_ref, dtb_ref, alog_ref, dskip_ref, nw_ref, h0_ref,
         out_ref, hs_ref, xpad_scr, xc_scr, ax_scr, dx_scr, arow_scr, ht_scr, yf_scr, yb_scr) = refs
    else:
        z_ref, xbc_ref, g_ref, cw_ref, cb_ref, dtb_ref, alog_ref, dskip_ref, nw_ref = refs[:9]
        (out_ref, hs_ref, xpad_scr, xc_scr, ax_scr, dx_scr, arow_scr, ht_scr, yf_scr,
         yb_scr) = refs[9 + n_prev:]
    whole = None if has_state else n_prev == 0
    nc = seq // CHUNK
    tot = bps * seq
    pad = 8
    blk = 256
    nblk = tot // blk
    cblk = 128
    for sub in range(bps):
        base = sub * (seq + 2 * pad)
        xpad_scr[base:base + pad, :] = jnp.zeros((pad, SSM_CONV_CH), f32)
        xpad_scr[base + pad + seq:base + 2 * pad + seq, :] = jnp.zeros((pad, SSM_CONV_CH), f32)
        xpad_scr[base + pad:base + pad + seq, :] = xbc_ref[sub * seq:(sub + 1) * seq, :].astype(f32)
        for i in range(seq // cblk):
            r0 = i * cblk
            acc = jnp.zeros((cblk, SSM_CONV_CH), f32) + cb_ref[...]
            for kk in range(SSM_CONV_K):
                off = base + r0 + pad - SSM_CONV_K // 2 + kk
                acc = acc + xpad_scr[off:off + cblk, :] * cw_ref[kk:kk + 1, :]
            xc_scr[sub * seq + r0:sub * seq + r0 + cblk, :] = jax.nn.silu(acc)

    dt = jax.nn.softplus(g_ref[...] + dtb_ref[...])
    lane = lax.broadcasted_iota(jnp.int32, dt.shape, 1)
    acum = _dir_scan(dt * (-jnp.exp(alog_ref[...])), jnp.add, 0.0, (lane % 8) >= HEADS)
    for d in range(2):
        e = _head_expand(G_DT + d * HEADS)
        ax = _select_cols(acum, e)
        ax_scr[d] = ax
        dx_scr[d] = _select_cols(dt, e)
        arow = _chunk_rows(ax, tot)
        for c in range(bps * nc):
            arow_scr[d, c] = arow[c:c + 1, :]

    gn = SSM_GROUPS * SSM_STATE
    rep = HEADS // SSM_GROUPS
    ht_scr[...] = jnp.zeros(ht_scr.shape, f32)
    if has_state:
        for sd in range(2 * bps):
            for h in range(HEADS):
                g0 = (h // rep) * SSM_STATE
                ht_scr[sd, g0:g0 + SSM_STATE, h * HEAD_W:(h + 1) * HEAD_W] = h0_ref[sd // 2, sd % 2, h].T
    li = lax.broadcasted_iota(jnp.int32, (CHUNK, MIX_W), 0)
    si = lax.broadcasted_iota(jnp.int32, (CHUNK, MIX_W), 1) % HEAD_W
    valid = (si <= li, si >= li)
    state_keep = jnp.where(_head_mask(gn, MIX_W, SSM_STATE, rep * HEAD_W), 1.0, 0.0)
    key_keep = jnp.where(_head_mask(HEADS * CHUNK, gn, rep * CHUNK, SSM_STATE), 1.0, 0.0).astype(bf16)
    head_keep = jnp.where(_head_mask(MIX_W, MIX_W, HEAD_W, HEAD_W), 1.0, 0.0).astype(bf16)

    def step(c, carry):
        for sd in range(2 * bps):
            sub, d = sd // 2, sd % 2
            cidx = sub * nc + (c if d == 0 else nc - 1 - c)
            r0 = pl.multiple_of(cidx * CHUNK, CHUNK)
            last = CHUNK - 1 if d == 0 else 0
            xs = xc_scr[pl.ds(r0, CHUNK), 0:MIX_W]
            bm = xc_scr[pl.ds(r0, CHUNK), MIX_W:MIX_W + gn].astype(bf16)
            cm = xc_scr[pl.ds(r0, CHUNK), MIX_W + gn:MIX_W + 2 * gn].astype(bf16)
            ax = ax_scr[d, pl.ds(r0, CHUNK), :]
            decay = jnp.exp(jnp.where(valid[d], ax - arow_scr[d, cidx], NEG))
            scores = _mm_nt(cm, _block_diag(bm, HEADS, key_keep)) * decay
            xdt = xs * dx_scr[d, pl.ds(r0, CHUNK), :]
            y = _mm(scores, _block_diag(xdt.astype(bf16), HEADS, head_keep)) + _mm(cm, ht_scr[sd]) * jnp.exp(ax)
            y_scr = yf_scr if d == 0 else yb_scr
            y_scr[pl.ds(r0, CHUNK), :] = y
            al = ax[last:last + 1, :]
            ht_scr[sd] = ht_scr[sd] * jnp.exp(al) + _mm_tn(bm, xdt * jnp.exp(al - ax)) * state_keep
        return carry

    lax.fori_loop(0, nc, step, 0, unroll=min(SCAN_UNROLL, nc))
    for sub in range(bps):
        hs = _layer_view(hs_ref, sub, layer, whole)
        for d in range(2):
            for h in range(HEADS):
                g0 = (h // rep) * SSM_STATE
                hs[d, h] = ht_scr[2 * sub + d, g0:g0 + SSM_STATE, h * HEAD_W:(h + 1) * HEAD_W].T

    def fin(i, carry):
        r0 = pl.multiple_of(i * blk, blk)
        y = yf_scr[pl.ds(r0, blk), :] + yb_scr[pl.ds(r0, blk), :] + dskip_ref[...] * xc_scr[pl.ds(r0, blk), 0:MIX_W]
        y = y * jax.nn.silu(z_ref[pl.ds(r0, blk), :].astype(f32))
        y = y * lax.rsqrt(jnp.mean(y * y, axis=-1, keepdims=True) + EPS) * nw_ref[...]
        out_ref[pl.ds(r0, blk), :] = y.astype(out_ref.dtype)
        return carry

    lax.fori_loop(0, nblk, fin, 0)


def _ssd(u, g, nb, seq, cw, cb_row, dtb_row, alog_row, dskip_row, nw_row, state, layer=0, prev=None):
    has_state = state is not None
    bps = 1 if has_state else SCAN_BPS
    nc = seq // CHUNK
    rows = bps * seq
    const = lambda shape: pl.BlockSpec(shape, lambda b: (0,) * len(shape))
    st = pl.BlockSpec((bps, 2, HEADS, HEAD_W, SSM_STATE), lambda b: (b, 0, 0, 0, 0))
    in_specs = [pl.BlockSpec((rows, MIX_W), lambda b: (b, COL_Z // MIX_W)),
                pl.BlockSpec((rows, SSM_CONV_CH), lambda b: (b, COL_XBC // SSM_CONV_CH)),
                pl.BlockSpec((rows, LANE), lambda b: (b, 0)),
                const((SSM_CONV_K, SSM_CONV_CH)), const((1, SSM_CONV_CH)), const((1, LANE)), const((1, LANE)),
                const((1, MIX_W)), const((1, MIX_W))]
    args = [u, u, g, cw, cb_row, dtb_row, alog_row, dskip_row, nw_row]
    aliases = {}
    st_shape = jax.ShapeDtypeStruct((nb, 2, HEADS, HEAD_W, SSM_STATE), f32)
    if has_state:
        in_specs.append(st)
        args.append(state)
    else:
        st_shape, st, _ = _layer_out(nb, bps, (2, HEADS, HEAD_W, SSM_STATE), layer, prev)
        if prev is not None:
            in_specs.append(pl.BlockSpec(memory_space=pl.ANY))
            args.append(prev)
            aliases = {9: 1}
    return pl.pallas_call(
        functools.partial(_ssd_body, seq, bps, has_state, layer, len(aliases)),
        grid=(nb // bps,),
        in_specs=in_specs,
        out_specs=[pl.BlockSpec((rows, MIX_W), lambda b: (b, 0)), st],
        out_shape=[jax.ShapeDtypeStruct((nb * seq, MIX_W), bf16), st_shape],
        input_output_aliases=aliases,
        scratch_shapes=[pltpu.VMEM((bps * (seq + 16), SSM_CONV_CH), f32), pltpu.VMEM((rows, SSM_CONV_CH), f32),
                        pltpu.VMEM((2, rows, MIX_W), f32), pltpu.VMEM((2, rows, MIX_W), f32),
                        pltpu.VMEM((2, bps * nc, 1, MIX_W), f32),
                        pltpu.VMEM((2 * bps, SSM_GROUPS * SSM_STATE, MIX_W), f32),
                        pltpu.VMEM((rows, MIX_W), f32), pltpu.VMEM((rows, MIX_W), f32)],
        compiler_params=_params(("arbitrary",)),
        name="ssd_state" if has_state else "ssd",
    )(*args)


def _with_ones(v):
    vb = v.astype(bf16)
    ones = jnp.ones((v.shape[0], HEAD_W), bf16)
    return jnp.concatenate([x for h in range(HEADS) for x in (vb[:, h * HEAD_W:(h + 1) * HEAD_W], ones)], axis=1)


def _softmax_av(scores, vexts):
    m = scores[0].max(axis=1, keepdims=True)
    for s in scores[1:]:
        m = jnp.maximum(m, s.max(axis=1, keepdims=True))
    r = None
    for s, vx in zip(scores, vexts):
        t = jnp.dot(jnp.exp(s - m).astype(bf16), vx, preferred_element_type=f32)
        r = t if r is None else r + t
    return r[:, :HEAD_W] / r[:, HEAD_W:]


def _store_layer(ref, sub, layer, whole, val):
    if whole:
        for l2 in range(DEPTH):
            ref[sub, l2] = val if l2 == layer else jnp.zeros_like(val)
    else:
        ref[sub] = val


def _layer_out(nb, bps, tail, layer, prev):
    zeros = (0,) * len(tail)
    shape = jax.ShapeDtypeStruct((nb, DEPTH) + tail, f32)
    if prev is None:
        return shape, pl.BlockSpec((bps, DEPTH) + tail, lambda b: (b, 0) + zeros), True
    return shape, pl.BlockSpec((bps, None) + tail, lambda b: (b, layer) + zeros), False


def _na_prompt_body(seq, layer, whole, q_ref, k_ref, v_ref, qn_ref, kn_ref, *rest):
    o_ref, ko_ref, vo_ref = rest[-3:]
    scale = HEAD_W ** -0.5
    qb = (_seg_rms(q_ref[...].astype(f32), HEAD_W, qn_ref[...]) * scale).astype(bf16)
    kn = _seg_rms(k_ref[...].astype(f32), HEAD_W, kn_ref[...])
    v = v_ref[...].astype(f32)
    for s0 in range(0, q_ref.shape[0], seq):
        rs = slice(s0, s0 + seq)
        _store_layer(ko_ref, s0 // seq, layer, whole, kn[rs])
        _store_layer(vo_ref, s0 // seq, layer, whole, v[rs])
        kt = kn[rs].T.astype(bf16)
        vx = _with_ones(v[rs])
        for h in range(HEADS):
            hl = slice(h * HEAD_W, (h + 1) * HEAD_W)
            s = jnp.dot(qb[rs, hl], kt[hl, :], preferred_element_type=f32)
            o_ref[rs, hl] = _softmax_av([s], [vx[:, 2 * h * HEAD_W:2 * (h + 1) * HEAD_W]]).astype(o_ref.dtype)


def _na_prompt(u, nb, seq, qn_row, kn_row, layer, prev):
    cb = COL_NA // MIX_W
    rows = PROMPT_BPS * seq
    blk = lambda j: pl.BlockSpec((rows, MIX_W), lambda b: (b, j))
    const = pl.BlockSpec((1, MIX_W), lambda b: (0, 0))
    shape, spec, whole = _layer_out(nb, PROMPT_BPS, (seq, MIX_W), layer, prev)
    extra = [] if prev is None else list(prev)
    return pl.pallas_call(
        functools.partial(_na_prompt_body, seq, layer, whole),
        grid=(nb // PROMPT_BPS,),
        in_specs=[blk(cb), blk(cb + 1), blk(cb + 2), const, const] + [pl.BlockSpec(memory_space=pl.ANY)] * len(extra),
        out_specs=[blk(0), spec, spec],
        out_shape=[jax.ShapeDtypeStruct((nb * seq, MIX_W), bf16), shape, shape],
        input_output_aliases={} if prev is None else {5: 1, 6: 2},
        compiler_params=_params(("arbitrary",)),
        name="na_prompt",
    )(u, u, u, qn_row, kn_row, *extra)


def _na_sample_body(qblk, q_ref, k_ref, v_ref, kc_ref, vc_ref, rpb_ref, qn_ref, kn_ref, o_ref,
                    kt_scr, vx_scr, bias_scr):
    b = pl.program_id(0)
    i = pl.program_id(1)
    seq = k_ref.shape[0]
    rows = seq // GRID_W

    @pl.when((b == 0) & (i == 0))
    def _():
        j = lax.broadcasted_iota(jnp.int32, (GRID_W, LANE), 0)
        lane = lax.broadcasted_iota(jnp.int32, (GRID_W, LANE), 1)
        j2 = lane % GRID_W
        cs = jnp.clip(j - NA_WIN_C // 2, 0, GRID_W - NA_WIN_C)
        ok = (j2 >= cs) & (j2 < cs + NA_WIN_C)
        left = lane < GRID_W
        neg = jnp.full((GRID_W, LANE), NEG, f32)
        for h in range(HEADS):
            lo, hi = [], []
            for dr in range(2 * NA_WIN_R - 1):
                row = jnp.broadcast_to(rpb_ref[h, dr:dr + 1, :], (GRID_W, LANE))
                lo.append(jnp.where(ok, pltpu.roll(row, LANE - (NA_WIN_C - 1), 1, stride=1, stride_axis=0), NEG))
                hi.append(jnp.where(ok, pltpu.roll(row, GRID_W - (NA_WIN_C - 1), 1, stride=1, stride_axis=0), NEG))
            for qi in range(rows):
                rs = min(max(qi - NA_WIN_R // 2, 0), rows - NA_WIN_R)
                for pair in range(rows // 2):
                    k_even, k_odd = 2 * pair, 2 * pair + 1
                    a = lo[k_even - qi + NA_WIN_R - 1] if rs <= k_even < rs + NA_WIN_R else neg
                    c = hi[k_odd - qi + NA_WIN_R - 1] if rs <= k_odd < rs + NA_WIN_R else neg
                    bias_scr[h, qi * GRID_W:(qi + 1) * GRID_W, pair * LANE:(pair + 1) * LANE] = jnp.where(left, a, c)

    @pl.when(i == 0)
    def _():
        kn = _seg_rms(k_ref[...].astype(f32), HEAD_W, kn_ref[...])
        kt_scr[:, 0:seq] = kn.T.astype(bf16)
        kt_scr[:, seq:] = kc_ref[...].T.astype(bf16)
        vx_scr[0:seq, :] = _with_ones(v_ref[...])
        vx_scr[seq:, :] = _with_ones(vc_ref[...])

    r0 = pl.multiple_of(i * qblk, qblk)
    qb = (_seg_rms(q_ref[...].astype(f32), HEAD_W, qn_ref[...]) * (HEAD_W ** -0.5)).astype(bf16)
    for h in range(HEADS):
        hl = slice(h * HEAD_W, (h + 1) * HEAD_W)
        xl = slice(2 * h * HEAD_W, 2 * (h + 1) * HEAD_W)
        s_own = jnp.dot(qb[:, hl], kt_scr[hl, 0:seq], preferred_element_type=f32) + bias_scr[h, pl.ds(r0, qblk), :]
        s_ctx = jnp.dot(qb[:, hl], kt_scr[hl, seq:], preferred_element_type=f32)
        o = _softmax_av([s_own, s_ctx], [vx_scr[0:seq, xl], vx_scr[seq:, xl]])
        o_ref[:, hl] = o.astype(o_ref.dtype)


def _na_sample(u, nb, layer, cache_k, cache_v, rpb_pad, qn_row, kn_row):
    seq = DEC_SEQ
    qblk = 256
    nq = seq // qblk
    cb = COL_NA // MIX_W
    full = lambda j: pl.BlockSpec((seq, MIX_W), lambda b, i: (b, j))
    ctx = pl.BlockSpec((None, None, PAST_LEN, MIX_W), lambda b, i: (b, layer, 0, 0))
    const = pl.BlockSpec((1, MIX_W), lambda b, i: (0, 0))
    return pl.pallas_call(
        functools.partial(_na_sample_body, qblk),
        grid=(nb, nq),
        in_specs=[pl.BlockSpec((qblk, MIX_W), lambda b, i: (b * nq + i, cb)), full(cb + 1), full(cb + 2),
                  ctx, ctx,
                  pl.BlockSpec((HEADS, 2 * NA_WIN_R, LANE), lambda b, i: (0, 0, 0)),
                  const, const],
        out_specs=pl.BlockSpec((qblk, MIX_W), lambda b, i: (b * nq + i, 0)),
        out_shape=jax.ShapeDtypeStruct((nb * seq, MIX_W), bf16),
        scratch_shapes=[pltpu.VMEM((MIX_W, seq + PAST_LEN), bf16), pltpu.VMEM((seq + PAST_LEN, 2 * MIX_W), bf16),
                        pltpu.VMEM((HEADS, seq, seq), f32)],
        compiler_params=_params(("arbitrary", "arbitrary")),
        name="na_sample",
    )(u, u, u, cache_k, cache_v, rpb_pad, qn_row, kn_row)


def _na_rpb_pad(rpb):
    c = rpb.shape[2]
    return jnp.pad(rpb.astype(f32), ((0, 0), (0, 1), (0, LANE - c)))


def _diff_lambda(lam_ref, lam_init):
    lv = lam_ref[...]
    s1 = jnp.sum(lv[0:1, :] * lv[1:2, :], axis=1, keepdims=True)
    s2 = jnp.sum(lv[2:3, :] * lv[3:4, :], axis=1, keepdims=True)
    return jnp.exp(s1) - jnp.exp(s2) + lam_init


def _diff_heads(q, kt, vext, lam):
    outs = []
    for h in range(HEADS):
        o = []
        for m in range(2):
            c0 = h * HEAD_W + m * DIFF_QK_DIM
            s = jnp.dot(q[:, c0:c0 + DIFF_QK_DIM], kt[c0:c0 + DIFF_QK_DIM, :], preferred_element_type=f32)
            o.append(_softmax_av([s], [vext[:, 2 * h * HEAD_W:2 * (h + 1) * HEAD_W]]))
        outs.append(o[0] - lam * o[1])
    return jnp.concatenate(outs, axis=1)


def _diff_prompt_body(lam_init, seq, layer, whole, q_ref, k_ref, v_ref, qn_ref, kn_ref, lam_ref, nw_ref, *rest):
    o_ref, ko_ref, vo_ref = rest[-3:]
    lam = _diff_lambda(lam_ref, lam_init)
    qb = (_seg_rms(q_ref[...].astype(f32), DIFF_QK_DIM, qn_ref[...]) * (DIFF_QK_DIM ** -0.5)).astype(bf16)
    kn = _seg_rms(k_ref[...].astype(f32), DIFF_QK_DIM, kn_ref[...])
    v = v_ref[...].astype(f32)
    for s0 in range(0, q_ref.shape[0], seq):
        rs = slice(s0, s0 + seq)
        _store_layer(ko_ref, s0 // seq, layer, whole, kn[rs])
        _store_layer(vo_ref, s0 // seq, layer, whole, v[rs])
        o = _diff_heads(qb[rs], kn[rs].T.astype(bf16), _with_ones(v[rs]), lam)
        o_ref[rs, :] = (_seg_rms(o, HEAD_W, nw_ref[...]) * (1.0 - lam_init)).astype(o_ref.dtype)


def _diff_prompt(u, nb, seq, lam_init, qn_row, kn_row, lam_vecs, nw_row, layer, prev):
    cb = COL_DIFF // MIX_W
    rows = PROMPT_BPS * seq
    blk = lambda j: pl.BlockSpec((rows, MIX_W), lambda b: (b, j))
    const = pl.BlockSpec((1, MIX_W), lambda b: (0, 0))
    shape, spec, whole = _layer_out(nb, PROMPT_BPS, (seq, MIX_W), layer, prev)
    extra = [] if prev is None else list(prev)
    return pl.pallas_call(
        functools.partial(_diff_prompt_body, lam_init, seq, layer, whole),
        grid=(nb // PROMPT_BPS,),
        in_specs=[blk(cb), blk(cb + 1), blk(cb + 2), const, const,
                  pl.BlockSpec((4, DIFF_QK_DIM), lambda b: (0, 0)), const] + [pl.BlockSpec(memory_space=pl.ANY)] * len(extra),
        out_specs=[blk(0), spec, spec],
        out_shape=[jax.ShapeDtypeStruct((nb * seq, MIX_W), bf16), shape, shape],
        input_output_aliases={} if prev is None else {7: 1, 8: 2},
        compiler_params=_params(("arbitrary",)),
        name="diff_prompt",
    )(u, u, u, qn_row, kn_row, lam_vecs, nw_row, *extra)


def _rope(x, cos, sin):
    w = x.shape[-1]
    up = pltpu.roll(x, w - 8, 1)
    dn = pltpu.roll(x, 8, 1)
    first = (lax.broadcasted_iota(jnp.int32, x.shape, 1) % 16) < 8
    return x * cos + jnp.where(first, up, dn) * sin


def _diff_sample_body(lam_init, qblk, q_ref, k_ref, v_ref, kc_ref, vc_ref, cos_ref, sin_ref, qn_ref, kn_ref,
                      lam_ref, nw_ref, o_ref, kt_scr, vx_scr):
    i = pl.program_id(1)
    seq = k_ref.shape[0]

    @pl.when(i == 0)
    def _():
        kn = _seg_rms(k_ref[...].astype(f32), DIFF_QK_DIM, kn_ref[...])
        kt_scr[:, 0:seq] = _rope(kn, cos_ref[...], sin_ref[...]).T.astype(bf16)
        kt_scr[:, seq:] = kc_ref[...].T.astype(bf16)
        vx_scr[0:seq, :] = _with_ones(v_ref[...])
        vx_scr[seq:, :] = _with_ones(vc_ref[...])

    lam = _diff_lambda(lam_ref, lam_init)
    r0 = pl.multiple_of(i * qblk, qblk)
    qn = _seg_rms(q_ref[...].astype(f32), DIFF_QK_DIM, qn_ref[...])
    qn = _rope(qn, cos_ref[pl.ds(r0, qblk), :], sin_ref[pl.ds(r0, qblk), :]) * (DIFF_QK_DIM ** -0.5)
    o = _diff_heads(qn.astype(bf16), kt_scr[...], vx_scr[...], lam)
    o_ref[...] = (_seg_rms(o, HEAD_W, nw_ref[...]) * (1.0 - lam_init)).astype(o_ref.dtype)


def _diff_sample(u, nb, layer, lam_init, cache_k, cache_v, cos, sin, qn_row, kn_row, lam_vecs, nw_row):
    seq = DEC_SEQ
    qblk = 512
    nq = seq // qblk
    cb = COL_DIFF // MIX_W
    full = lambda j: pl.BlockSpec((seq, MIX_W), lambda b, i: (b, j))
    ctx = pl.BlockSpec((None, None, PAST_LEN, MIX_W), lambda b, i: (b, layer, 0, 0))
    const = pl.BlockSpec((1, MIX_W), lambda b, i: (0, 0))
    tab = pl.BlockSpec((seq, MIX_W), lambda b, i: (0, 0))
    return pl.pallas_call(
        functools.partial(_diff_sample_body, lam_init, qblk),
        grid=(nb, nq),
        in_specs=[pl.BlockSpec((qblk, MIX_W), lambda b, i: (b * nq + i, cb)), full(cb + 1), full(cb + 2),
                  ctx, ctx, tab, tab, const, const,
                  pl.BlockSpec((4, DIFF_QK_DIM), lambda b, i: (0, 0)), const],
        out_specs=pl.BlockSpec((qblk, MIX_W), lambda b, i: (b * nq + i, 0)),
        out_shape=jax.ShapeDtypeStruct((nb * seq, MIX_W), bf16),
        scratch_shapes=[pltpu.VMEM((MIX_W, seq + PAST_LEN), bf16), pltpu.VMEM((seq + PAST_LEN, 2 * MIX_W), bf16)],
        compiler_params=_params(("arbitrary", "arbitrary")),
        name="diff_sample",
    )(u, u, u, cache_k, cache_v, cos, sin, qn_row, kn_row, lam_vecs, nw_row)


def _rope_tables():
    t = jnp.arange(DEC_SEQ)
    nf = DIFF_QK_DIM // 4
    inv = ROPE_BASE ** (-jnp.arange(nf, dtype=f32) / nf)
    ar = (t // GRID_W).astype(f32)[:, None] * inv
    ac = (t % GRID_W).astype(f32)[:, None] * inv
    cos = jnp.concatenate([jnp.cos(ar), jnp.cos(ar), jnp.cos(ac), jnp.cos(ac)], axis=1)
    sin = jnp.concatenate([-jnp.sin(ar), jnp.sin(ar), -jnp.sin(ac), jnp.sin(ac)], axis=1)
    reps = MIX_W // DIFF_QK_DIM
    return jnp.tile(cos, (1, reps)), jnp.tile(sin, (1, reps))


def _merge_body(x_ref, a_ref, b_ref, c_ref, d_ref, gp_ref, mod_ref, wb_ref, wo_ref, nw_ref, wr_ref,
                xo_ref, h2_ref, aff_ref):
    mixed = None
    for n, br in enumerate((a_ref, b_ref, c_ref, d_ref)):
        gate = _sigmoid(gp_ref[:, n * D_MODEL:(n + 1) * D_MODEL].astype(f32))
        t = gate * jnp.dot(br[...].astype(bf16), wb_ref[n], preferred_element_type=f32)
        mixed = t if mixed is None else mixed + t
    y = jnp.dot(mixed.astype(bf16), wo_ref[...], preferred_element_type=f32)
    x = x_ref[...] + mod_ref[:, 2 * D_MODEL:3 * D_MODEL] * y
    xo_ref[...] = x
    h = x * lax.rsqrt(jnp.mean(x * x, axis=-1, keepdims=True) + EPS) * nw_ref[...]
    h = h * (1.0 + mod_ref[:, 4 * D_MODEL:5 * D_MODEL]) + mod_ref[:, 3 * D_MODEL:4 * D_MODEL]
    hb = h.astype(bf16)
    h2_ref[...] = hb
    logits = jnp.dot(hb, wr_ref[...], preferred_element_type=f32)
    lane = lax.broadcasted_iota(jnp.int32, logits.shape, 1)
    logits = jnp.where(lane < N_EXPERTS, logits, NEG)
    e = jnp.exp(logits - logits.max(axis=1, keepdims=True))
    aff_ref[...] = e / e.sum(axis=1, keepdims=True)


def _merge(x, brs, u, mod3, mod_base, mod_stride, seq, wb, wo, nw_row, wr):
    n = x.shape[0]
    tm = 512
    tps = max(seq // tm, 1)
    row = lambda w: pl.BlockSpec((tm, w), lambda i: (i, 0))
    return pl.pallas_call(
        _merge_body,
        grid=(n // tm,),
        in_specs=[row(D_MODEL), row(MIX_W), row(MIX_W), row(MIX_W), row(MIX_W),
                  pl.BlockSpec((tm, N_BRANCH * D_MODEL), lambda i: (i, COL_GPRE // (N_BRANCH * D_MODEL))),
                  pl.BlockSpec((None, 1, 6 * D_MODEL), lambda i: (mod_base + mod_stride * (i // tps), 0, 0)),
                  pl.BlockSpec((N_BRANCH, MIX_W, D_MODEL), lambda i: (0, 0, 0)),
                  pl.BlockSpec((D_MODEL, D_MODEL), lambda i: (0, 0)),
                  pl.BlockSpec((1, D_MODEL), lambda i: (0, 0)),
                  pl.BlockSpec((D_MODEL, LANE), lambda i: (0, 0))],
        out_specs=[row(D_MODEL), row(D_MODEL), row(LANE)],
        out_shape=[jax.ShapeDtypeStruct((n, D_MODEL), f32), jax.ShapeDtypeStruct((n, D_MODEL), bf16),
                   jax.ShapeDtypeStruct((n, LANE), f32)],
        compiler_params=_params(("arbitrary",)),
        name="merge",
    )(x, *brs, u, mod3, wb, wo, nw_row, wr)


def _rows_prefix_sum(x):
    n = x.shape[0]
    pos = lax.broadcasted_iota(jnp.int32, x.shape, 0)
    k = 1
    while k < n:
        x = x + jnp.where(pos >= k, pltpu.roll(x, k, 0), 0.0)
        k *= 2
    return x


def _route_body(t, cap, bpg, aff_ref, h2_ref, xg_ref, g_ref, rank_ref, rrow_scr, arow_scr, oh_scr):
    bb = pl.program_id(1)

    @pl.when(bb == 0)
    def _():
        aff = aff_ref[...]
        tok = lax.broadcasted_iota(jnp.int32, aff.shape, 0)
        val = aff
        k = 2
        while k <= t:
            j = k // 2
            while j >= 1:
                lower = (tok & j) == 0
                pv = jnp.where(lower, pltpu.roll(val, t - j, 0), pltpu.roll(val, j, 0))
                val = jnp.where(lower == ((tok & k) == 0), jnp.maximum(val, pv), jnp.minimum(val, pv))
                j //= 2
            k *= 2
        thr = val[cap - 1:cap, :]
        above = aff > thr
        tie = jnp.where(aff == thr, 1.0, 0.0)
        need = cap - jnp.sum(jnp.where(above, 1.0, 0.0), axis=0, keepdims=True)
        sel = jnp.where(above | ((tie > 0.0) & (_rows_prefix_sum(tie) - tie < need)), 1.0, 0.0)
        rank = jnp.where(sel > 0.0, _rows_prefix_sum(sel) - sel, float(cap))
        rank_ref[...] = rank
        rank_t = rank.T
        aff_t = aff.T
        for k in range(bpg):
            rrow_scr[k] = rank_t[k * N_EXPERTS:(k + 1) * N_EXPERTS, :]
            arow_scr[k] = aff_t[k * N_EXPERTS:(k + 1) * N_EXPERTS, :]

    rk = rrow_scr[bb]
    ar = arow_scr[bb]
    slot = lax.broadcasted_iota(jnp.int32, (cap, t), 0).astype(f32)
    for e in range(N_EXPERTS):
        onehot = jnp.where(rk[e:e + 1, :] == slot, 1.0, 0.0)
        oh_scr[e * cap:(e + 1) * cap, :] = onehot.astype(bf16)
        g_ref[e] = jnp.sum(onehot * ar[e:e + 1, :], axis=1, keepdims=True)
    h2 = h2_ref[...]
    epb = max(1, 512 // cap)
    for e0 in range(0, N_EXPERTS, epb):
        rows = jnp.dot(oh_scr[e0 * cap:(e0 + epb) * cap, :], h2, preferred_element_type=f32)
        xg_ref[e0:e0 + epb] = rows.astype(bf16).reshape(epb, cap, D_MODEL)


def _group_lanes(aff, nb, t, bpg):
    a = aff[:, :N_EXPERTS].reshape(nb // bpg, bpg, t, N_EXPERTS)
    a = jnp.transpose(a, (0, 2, 1, 3)).reshape(nb // bpg * t, bpg * N_EXPERTS)
    return jnp.pad(a, ((0, 0), (0, LANE - bpg * N_EXPERTS)))


def _route(aff, h2, nb, t, bpg):
    cap = CAPACITY_FACTOR * t // N_EXPERTS
    return pl.pallas_call(
        functools.partial(_route_body, t, cap, bpg),
        grid=(nb // bpg, bpg),
        in_specs=[pl.BlockSpec((t, LANE), lambda g, k: (g, 0)),
                  pl.BlockSpec((t, D_MODEL), lambda g, k: (g * bpg + k, 0))],
        out_specs=[pl.BlockSpec((N_EXPERTS, cap, D_MODEL), lambda g, k: (0, g * bpg + k, 0)),
                   pl.BlockSpec((N_EXPERTS, cap, 1), lambda g, k: (0, g * bpg + k, 0)),
                   pl.BlockSpec((t, LANE), lambda g, k: (g, 0))],
        out_shape=[jax.ShapeDtypeStruct((N_EXPERTS, nb * cap, D_MODEL), bf16),
                   jax.ShapeDtypeStruct((N_EXPERTS, nb * cap, 1), f32),
                   jax.ShapeDtypeStruct((nb // bpg * t, LANE), f32)],
        scratch_shapes=[pltpu.VMEM((bpg, N_EXPERTS, t), f32), pltpu.VMEM((bpg, N_EXPERTS, t), f32),
                        pltpu.VMEM((N_EXPERTS * cap, t), bf16)],
        compiler_params=_params(("arbitrary", "arbitrary")),
        name=f"route_{t}",
    )(_group_lanes(aff, nb, t, bpg), h2)


def _ffn_body(xp_ref, xs_ref, gp_ref, gs_ref, wg_ref, wu_ref, wd_ref, yp_ref, ys_ref, wg_scr, wu_scr, wd_scr):
    wg_scr[...] = wg_ref[...].astype(bf16)
    wu_scr[...] = wu_ref[...].astype(bf16)
    wd_scr[...] = wd_ref[...].astype(bf16)
    np_ = xp_ref.shape[0]
    x = jnp.concatenate([xp_ref[...], xs_ref[...]], axis=0)
    a = jnp.dot(x, wg_scr[...], preferred_element_type=f32)
    up = jnp.dot(x, wu_scr[...], preferred_element_type=f32)
    mid = (jax.nn.silu(a) * up).astype(bf16)
    y = jnp.dot(mid, wd_scr[...], preferred_element_type=f32)
    yp_ref[...] = (y[:np_] * gp_ref[...]).astype(bf16)
    ys_ref[...] = (y[np_:] * gs_ref[...]).astype(bf16)


def _ffn(xp, xs, gp, gs, w_gate, w_up, w_down, layer):
    np_, ns = xp.shape[1], xs.shape[1]
    tok = lambda n, w: pl.BlockSpec((None, n, w), lambda e: (e, 0, 0))
    wspec = lambda a, b: pl.BlockSpec((None, None, a, b), lambda e: (layer, e, 0, 0))
    return pl.pallas_call(
        _ffn_body,
        grid=(N_EXPERTS,),
        in_specs=[tok(np_, D_MODEL), tok(ns, D_MODEL), tok(np_, 1), tok(ns, 1),
                  wspec(D_MODEL, EXPERT_FF), wspec(D_MODEL, EXPERT_FF), wspec(EXPERT_FF, D_MODEL)],
        out_specs=[tok(np_, D_MODEL), tok(ns, D_MODEL)],
        out_shape=[jax.ShapeDtypeStruct(xp.shape, bf16), jax.ShapeDtypeStruct(xs.shape, bf16)],
        scratch_shapes=[pltpu.VMEM((D_MODEL, EXPERT_FF), bf16), pltpu.VMEM((D_MODEL, EXPERT_FF), bf16),
                        pltpu.VMEM((EXPERT_FF, D_MODEL), bf16)],
        compiler_params=_params(("arbitrary",)),
        name="expert_ffn",
    )(xp, xs, gp, gs, w_gate, w_up, w_down)


def _combine_body(t, cap, bpg, bps, x_ref, rank_ref, y_ref, mod_ref, o_ref):
    k = N_EXPERTS * cap
    er = lax.broadcasted_iota(jnp.int32, (LANE, k), 0)
    ec = lax.broadcasted_iota(jnp.int32, (LANE, k), 1) // cap
    slot = (lax.broadcasted_iota(jnp.int32, (t, k), 1) % cap).astype(f32)
    rank = rank_ref[...].astype(bf16)
    for sub in range(bps):
        col0 = ((pl.program_id(0) * bps + sub) % bpg) * N_EXPERTS
        expand = jnp.where(er == ec + col0, 1.0, 0.0).astype(bf16)
        rexp = jnp.dot(rank, expand, preferred_element_type=f32)
        pt = jnp.where(rexp == slot, 1.0, 0.0).astype(bf16)
        y = y_ref[:, sub * cap:(sub + 1) * cap, :].reshape(k, D_MODEL)
        moe = jnp.dot(pt, y, preferred_element_type=f32)
        rs = slice(sub * t, (sub + 1) * t)
        o_ref[rs, :] = x_ref[rs, :] + mod_ref[:, 5 * D_MODEL:6 * D_MODEL] * moe


def _combine(x, rank, y, mod3, mod_base, mod_stride, nb, t, bpg):
    cap = CAPACITY_FACTOR * t // N_EXPERTS
    bps = 2 if (mod_stride == 0 and bpg % 2 == 0) else 1
    return pl.pallas_call(
        functools.partial(_combine_body, t, cap, bpg, bps),
        grid=(nb // bps,),
        in_specs=[pl.BlockSpec((bps * t, D_MODEL), lambda b: (b, 0)),
                  pl.BlockSpec((t, LANE), lambda b: (b * bps // bpg, 0)),
                  pl.BlockSpec((N_EXPERTS, bps * cap, D_MODEL), lambda b: (0, b, 0)),
                  pl.BlockSpec((None, 1, 6 * D_MODEL), lambda b: (mod_base + mod_stride * b, 0, 0))],
        out_specs=pl.BlockSpec((bps * t, D_MODEL), lambda b: (b, 0)),
        out_shape=jax.ShapeDtypeStruct(x.shape, f32),
        compiler_params=_params(("arbitrary",)),
        name=f"combine_{t}",
    )(x, rank, y, mod3)


def _pad_row(v, off, width=LANE):
    return jnp.zeros((1, width), f32).at[0, off:off + v.shape[0]].set(v.astype(f32))


def _layer_params(l, a):
    w_in = a["w_in"][l]
    pts, acc = [], 0
    splits = (MIX_W,) * 4 + (4 * HEADS,) + (MIX_W,) * 7 + (SSM_CONV_CH, 2 * HEADS, N_BRANCH * D_MODEL)
    for s in splits:
        pts.append((acc, acc + s))
        acc += s
    seg = lambda i: w_in[:, pts[i][0]:pts[i][1]]
    order = [14, 0, 1, 2, 3, 5, 6, 7, 8, 9, 10, 12, 11]
    w_main = jnp.concatenate([seg(i) for i in order] + [jnp.zeros((D_MODEL, U_COLS - U_USED), f32)],
                             axis=1).astype(bf16)
    w_side = jnp.concatenate([seg(4), seg(13), jnp.zeros((D_MODEL, LANE - 24), f32)], axis=1).astype(bf16)
    tile = lambda v, reps: jnp.tile(v.astype(f32), reps).reshape(1, -1)
    p = dict(
        w_main=w_main, w_side=w_side,
        norm1=a["norm1_w"][l].reshape(1, -1), norm2=a["norm2_w"][l].reshape(1, -1),
        mlstm_gb=_pad_row(a["mlstm_gate_b"][l].reshape(-1), G_MLSTM),
        mlstm_nw=a["mlstm_norm_w"][l].reshape(1, -1),
        na_qn=tile(a["na_qnorm_w"][l], HEADS), na_kn=tile(a["na_knorm_w"][l], HEADS),
        na_rpb=_na_rpb_pad(a["na_rpb"][l]),
        diff_qn=tile(a["diff_qnorm_w"][l], 2 * HEADS), diff_kn=tile(a["diff_knorm_w"][l], 2 * HEADS),
        diff_lam=a["diff_lambda"][l], diff_nw=tile(a["diff_norm_w"][l], HEADS),
        ssm_cw=a["ssm_conv_w"][l], ssm_cb=a["ssm_conv_b"][l].reshape(1, -1),
        ssm_dtb=_pad_row(a["ssm_dt_bias"][l].reshape(-1), G_DT),
        ssm_alog=_pad_row(a["ssm_a_log"][l].reshape(-1), G_DT),
        ssm_d=jnp.repeat(a["ssm_d"][l].astype(f32), HEAD_W).reshape(1, -1),
        ssm_nw=a["ssm_norm_w"][l].reshape(1, -1),
        wb=a["w_branch"][l].astype(bf16), wo=a["w_out"][l].astype(bf16),
        wr=jnp.concatenate([a["w_router"][l], jnp.zeros((D_MODEL, LANE - N_EXPERTS), f32)], axis=1).astype(bf16),
        w_gate=a["w_gate"], w_up=a["w_up"], w_down=a["w_down"],
    )
    return p


def _mixers(u, g, nb, seq, p, l, lam_init, ctx, tables, caches):
    if ctx is None:
        a_out, m_c, m_n, m_m = _mlstm(u, g, nb, seq, p["mlstm_gb"], p["mlstm_nw"], None,
                                      l, None if caches is None else caches[4:7])
        b_out, na_k, na_v = _na_prompt(u, nb, seq, p["na_qn"], p["na_kn"], l, None if caches is None else caches[0:2])
        c_out, df_k, df_v = _diff_prompt(u, nb, seq, lam_init, p["diff_qn"], p["diff_kn"], p["diff_lam"], p["diff_nw"],
                                         l, None if caches is None else caches[2:4])
        d_out, s_h = _ssd(u, g, nb, seq, p["ssm_cw"], p["ssm_cb"], p["ssm_dtb"], p["ssm_alog"], p["ssm_d"],
                          p["ssm_nw"], None, l, None if caches is None else caches[7])
        return (a_out, b_out, c_out, d_out), (na_k, na_v, df_k, df_v, m_c, m_n, m_m, s_h)
    (na_ck, na_cv, df_ck, df_cv, st_c, st_n, st_m, st_s) = ctx
    cos, sin = tables
    a_out = _mlstm(u, g, nb, seq, p["mlstm_gb"], p["mlstm_nw"], (st_c[:, l], st_n[:, l], st_m[:, l]))[0]
    b_out = _na_sample(u, nb, l, na_ck, na_cv, p["na_rpb"], p["na_qn"], p["na_kn"])
    c_out = _diff_sample(u, nb, l, lam_init, df_ck, df_cv, cos, sin, p["diff_qn"], p["diff_kn"], p["diff_lam"],
                         p["diff_nw"])
    d_out = _ssd(u, g, nb, seq, p["ssm_cw"], p["ssm_cb"], p["ssm_dtb"], p["ssm_alog"], p["ssm_d"], p["ssm_nw"],
                 st_s[:, l])[0]
    return (a_out, b_out, c_out, d_out), None


def kernel(x_prompt, x_sample, cache_na_k, cache_na_v, cache_diff_k, cache_diff_v, state_mlstm_C, state_mlstm_n, state_mlstm_m, state_ssm, c, c_ctx, norm1_w, norm2_w, w_ada, b_ada, w_in, mlstm_gate_b, mlstm_norm_w, na_qnorm_w, na_knorm_w, na_rpb, diff_qnorm_w, diff_knorm_w, diff_lambda, diff_norm_w, ssm_conv_w, ssm_conv_b, ssm_dt_bias, ssm_a_log, ssm_d, ssm_norm_w, w_branch, w_out, w_router, w_gate, w_up, w_down):
    a = dict(norm1_w=norm1_w, norm2_w=norm2_w, w_in=w_in, mlstm_gate_b=mlstm_gate_b, mlstm_norm_w=mlstm_norm_w,
             na_qnorm_w=na_qnorm_w, na_knorm_w=na_knorm_w, na_rpb=na_rpb, diff_qnorm_w=diff_qnorm_w,
             diff_knorm_w=diff_knorm_w, diff_lambda=diff_lambda, diff_norm_w=diff_norm_w, ssm_conv_w=ssm_conv_w,
             ssm_conv_b=ssm_conv_b, ssm_dt_bias=ssm_dt_bias, ssm_a_log=ssm_a_log, ssm_d=ssm_d,
             ssm_norm_w=ssm_norm_w, w_branch=w_branch, w_out=w_out, w_router=w_router, w_gate=w_gate, w_up=w_up,
             w_down=w_down)
    nb_p, nb_s = x_prompt.shape[0], x_sample.shape[0]
    xp = x_prompt.reshape(nb_p * SEQ, D_MODEL)
    xs = x_sample.reshape(nb_s * DEC_SEQ, D_MODEL)
    cc = jnp.concatenate([c_ctx[None, :], c, jnp.zeros((8 - 1 - nb_s, D_MODEL), f32)], axis=0)
    ctx = (cache_na_k.reshape(nb_s, DEPTH, PAST_LEN, MIX_W), cache_na_v.reshape(nb_s, DEPTH, PAST_LEN, MIX_W),
           cache_diff_k.reshape(nb_s, DEPTH, PAST_LEN, MIX_W), cache_diff_v.reshape(nb_s, DEPTH, PAST_LEN, MIX_W),
           state_mlstm_C, state_mlstm_n, state_mlstm_m, state_ssm)
    tables = _rope_tables()
    caches = None
    for l in range(DEPTH):
        p = _layer_params(l, a)
        lam_init = 0.8 - 0.6 * math.exp(-0.3 * l)
        mod3 = _modulation(cc, w_ada, b_ada, l).reshape(8, 1, 6 * D_MODEL)
        groups = []
        for (x, nb, seq, base, stride, gctx) in ((xp, nb_p, SEQ, 0, 0, None), (xs, nb_s, DEC_SEQ, 1, 1, ctx)):
            u, g = _inproj(x, mod3, base, stride, seq, p["norm1"], p["w_main"], p["w_side"])
            brs, st = _mixers(u, g, nb, seq, p, l, lam_init, gctx, tables, caches)
            if st is not None:
                caches = st
            x1, h2, aff = _merge(x, brs, u, mod3, base, stride, seq, p["wb"], p["wo"], p["norm2"], p["wr"])
            bpg = min(nb, LANE // N_EXPERTS)
            xg, gg, rank = _route(aff, h2, nb, seq, bpg)
            groups.append((x1, xg, gg, rank, nb, seq, base, stride, bpg))
        yp, ys = _ffn(groups[0][1], groups[1][1], groups[0][2], groups[1][2], p["w_gate"], p["w_up"], p["w_down"], l)
        outs = []
        for (x1, _, _, rank, nb, seq, base, stride, bpg), y in zip(groups, (yp, ys)):
            outs.append(_combine(x1, rank, y, mod3, base, stride, nb, seq, bpg))
        xp, xs = outs
    kv = (nb_p, DEPTH, SEQ, HEADS, HEAD_W)
    return (xp.reshape(x_prompt.shape), xs.reshape(x_sample.shape),
            caches[0].reshape(kv), caches[1].reshape(kv), caches[2].reshape(kv), caches[3].reshape(kv),
            caches[4], caches[5], caches[6], caches[7])
```
